```python
import math
import jax, jax.numpy as jnp
from jax import lax
import numpy as np

D_MODEL = 1024
BATCH = 16
SEQ = 4096
DEPTH = 4

EXPAND = 2
D_INNER = EXPAND * D_MODEL
CONV_WIDTH = 31
CHUNK = 128
SGU_GROUPS = 8
PLE_DIM = 256
N_MIXERS = 2
N_CONV_LAYERS = (DEPTH + 1) // 2
N_SGU_LAYERS = DEPTH // 2
EPS = 1e-6

kernel_name = "hybrid_conformer_conv_gmlp_trunk"


def _rmsnorm(x, g):
    xf = x.astype(jnp.float32)
    y = xf * lax.rsqrt(jnp.mean(xf * xf, axis=-1, keepdims=True) + EPS)
    return (y * g.astype(jnp.float32)).astype(x.dtype)


def _layernorm(x, g, b):
    xf = x.astype(jnp.float32)
    mu = jnp.mean(xf, axis=-1, keepdims=True)
    xc = xf - mu
    var = jnp.mean(xc * xc, axis=-1, keepdims=True)
    y = xc * lax.rsqrt(var + EPS)
    return (y * g.astype(jnp.float32) + b.astype(jnp.float32)).astype(x.dtype)


def _causal_depthwise_conv(x, w, b):
    k = w.shape[0]
    y = lax.conv_general_dilated(
        x, w[:, None, :].astype(x.dtype),
        window_strides=(1,), padding=[(k - 1, 0)],
        dimension_numbers=("NWC", "WIO", "NWC"),
        feature_group_count=x.shape[-1])
    return y + b.astype(x.dtype)


def _conformer_conv_mixer(a, b_gate, conv_w, conv_b, ln_g, ln_b):
    y = a * jax.nn.sigmoid(b_gate)
    y = _causal_depthwise_conv(y, conv_w, conv_b)
    y = _layernorm(y, ln_g, ln_b)
    return jax.nn.silu(y)


def _chunked_sgu_mixer(a, b_half, ln_g, ln_b, w_s, b_s):
    bsz, seq, e = a.shape
    n_chunks = seq // CHUNK
    u = jax.nn.gelu(a, approximate=False)
    v = _layernorm(jax.nn.gelu(b_half, approximate=False), ln_g, ln_b)
    vg = v.reshape(bsz, n_chunks, CHUNK, SGU_GROUPS, e // SGU_GROUPS)
    mask = jnp.tril(jnp.ones((CHUNK, CHUNK), dtype=bool))
    w = jnp.where(mask[None], w_s, jnp.zeros((), w_s.dtype)).astype(v.dtype)
    mixed = jnp.einsum("gts,bnsgc->bntgc", w, vg)
    mixed = mixed + b_s.T.astype(v.dtype)[None, None, :, :, None]
    return u * mixed.reshape(bsz, seq, e)


def _fwd_setup_inputs(seed: int = 0) -> dict:
    key = jax.random.key(seed)
    ks = jax.random.split(key, 20)
    f32 = jnp.float32
    E = D_INNER
    x = jax.random.normal(ks[0], (BATCH, SEQ, D_MODEL), f32)
    p = jax.random.normal(ks[1], (DEPTH, BATCH, SEQ, PLE_DIM), f32)
    norm_g = 1.0 + 0.02 * jax.random.normal(ks[2], (DEPTH, D_MODEL), f32)
    w_in = jax.random.normal(ks[3], (DEPTH, D_MODEL, 3 * E), f32) * D_MODEL ** -0.5
    w_out = jax.random.normal(ks[4], (DEPTH, E, D_MODEL), f32) * E ** -0.5
    conv_w = jax.random.normal(ks[5], (N_CONV_LAYERS, CONV_WIDTH, E), f32) * CONV_WIDTH ** -0.5
    conv_b = 0.02 * jax.random.normal(ks[6], (N_CONV_LAYERS, E), f32)
    conv_ln_g = 1.0 + 0.02 * jax.random.normal(ks[7], (N_CONV_LAYERS, E), f32)
    conv_ln_b = 0.02 * jax.random.normal(ks[8], (N_CONV_LAYERS, E), f32)
    sgu_ln_g = 1.0 + 0.02 * jax.random.normal(ks[9], (N_SGU_LAYERS, E), f32)
    sgu_ln_b = 0.02 * jax.random.normal(ks[10], (N_SGU_LAYERS, E), f32)
    sgu_w = jax.random.normal(ks[11], (N_SGU_LAYERS, SGU_GROUPS, CHUNK, CHUNK), f32) * CHUNK ** -0.5
    sgu_b = 1.0 + 0.1 * jax.random.normal(ks[12], (N_SGU_LAYERS, SGU_GROUPS, CHUNK), f32)
    pl_norm_g = 1.0 + 0.02 * jax.random.normal(ks[13], (DEPTH, D_MODEL), f32)
    pl_gate_w = jax.random.normal(ks[14], (DEPTH, D_MODEL, D_MODEL), f32) * D_MODEL ** -0.5
    pl_proj_w = jax.random.normal(ks[15], (DEPTH, PLE_DIM, D_MODEL), f32) * PLE_DIM ** -0.5
    final_g = 1.0 + 0.02 * jax.random.normal(ks[16], (D_MODEL,), f32)
    return {"x": x, "p": p, "norm_g": norm_g, "w_in": w_in, "w_out": w_out,
            "conv_w": conv_w, "conv_b": conv_b, "conv_ln_g": conv_ln_g, "conv_ln_b": conv_ln_b,
            "sgu_ln_g": sgu_ln_g, "sgu_ln_b": sgu_ln_b, "sgu_w": sgu_w, "sgu_b": sgu_b,
            "pl_norm_g": pl_norm_g, "pl_gate_w": pl_gate_w, "pl_proj_w": pl_proj_w,
            "final_g": final_g}


def _fwd_reference(x, p, norm_g, w_in, w_out, conv_w, conv_b, conv_ln_g, conv_ln_b,
              sgu_ln_g, sgu_ln_b, sgu_w, sgu_b, pl_norm_g, pl_gate_w, pl_proj_w, final_g):
    for i in range(DEPTH):
        h = _rmsnorm(x, norm_g[i])
        proj = jnp.einsum("bsd,de->bse", h, w_in[i])
        a, b_half, z = jnp.split(proj, 3, axis=-1)
        j = i // N_MIXERS
        if i % N_MIXERS == 0:
            y = _conformer_conv_mixer(a, b_half, conv_w[j], conv_b[j], conv_ln_g[j], conv_ln_b[j])
        else:
            y = _chunked_sgu_mixer(a, b_half, sgu_ln_g[j], sgu_ln_b[j], sgu_w[j], sgu_b[j])
        x = x + jnp.einsum("bse,ed->bsd", y * jax.nn.silu(z), w_out[i])
        gate = jax.nn.sigmoid(jnp.einsum("bsd,de->bse", _rmsnorm(x, pl_norm_g[i]), pl_gate_w[i]))
        x = x + gate * jnp.einsum("bsk,kd->bsd", p[i], pl_proj_w[i])
    return _rmsnorm(x, final_g)


import jax as _jax
import jax.numpy as _jnp

TWIN_FORMAT = 'train_step'
FWD_PARAMS = ['x', 'p', 'norm_g', 'w_in', 'w_out', 'conv_w', 'conv_b', 'conv_ln_g', 'conv_ln_b', 'sgu_ln_g', 'sgu_ln_b', 'sgu_w', 'sgu_b', 'pl_norm_g', 'pl_gate_w', 'pl_proj_w', 'final_g']
TWIN_WEIGHTS = ['norm_g', 'w_in', 'w_out', 'conv_w', 'conv_b', 'conv_ln_g', 'conv_ln_b', 'sgu_ln_g', 'sgu_ln_b', 'sgu_w', 'sgu_b', 'pl_norm_g', 'pl_gate_w', 'pl_proj_w', 'final_g']
TWIN_DIFF_INPUT = 'x'
TWIN_INPUTS = ['x', 'p', 'norm_g', 'w_in', 'w_out', 'conv_w', 'conv_b', 'conv_ln_g', 'conv_ln_b', 'sgu_ln_g', 'sgu_ln_b', 'sgu_w', 'sgu_b', 'pl_norm_g', 'pl_gate_w', 'pl_proj_w', 'final_g', 'loss_target', 'm_norm_g', 'm_w_in', 'm_w_out', 'm_conv_w', 'm_conv_b', 'm_conv_ln_g', 'm_conv_ln_b', 'm_sgu_ln_g', 'm_sgu_ln_b', 'm_sgu_w', 'm_sgu_b', 'm_pl_norm_g', 'm_pl_gate_w', 'm_pl_proj_w', 'm_final_g', 'v_norm_g', 'v_w_in', 'v_w_out', 'v_conv_w', 'v_conv_b', 'v_conv_ln_g', 'v_conv_ln_b', 'v_sgu_ln_g', 'v_sgu_ln_b', 'v_sgu_w', 'v_sgu_b', 'v_pl_norm_g', 'v_pl_gate_w', 'v_pl_proj_w', 'v_final_g']
TWIN_OUTPUTS = ['loss', 'grad_x', 'grad_norm_g', 'grad_w_in', 'grad_w_out', 'grad_conv_w', 'grad_conv_b', 'grad_conv_ln_g', 'grad_conv_ln_b', 'grad_sgu_ln_g', 'grad_sgu_ln_b', 'grad_sgu_w', 'grad_sgu_b', 'grad_pl_norm_g', 'grad_pl_gate_w', 'grad_pl_proj_w', 'grad_final_g', 'delta_norm_g', 'delta_w_in', 'delta_w_out', 'delta_conv_w', 'delta_conv_b', 'delta_conv_ln_g', 'delta_conv_ln_b', 'delta_sgu_ln_g', 'delta_sgu_ln_b', 'delta_sgu_w', 'delta_sgu_b', 'delta_pl_norm_g', 'delta_pl_gate_w', 'delta_pl_proj_w', 'delta_final_g', 'new_m_norm_g', 'new_m_w_in', 'new_m_w_out', 'new_m_conv_w', 'new_m_conv_b', 'new_m_conv_ln_g', 'new_m_conv_ln_b', 'new_m_sgu_ln_g', 'new_m_sgu_ln_b', 'new_m_sgu_w', 'new_m_sgu_b', 'new_m_pl_norm_g', 'new_m_pl_gate_w', 'new_m_pl_proj_w', 'new_m_final_g', 'new_v_norm_g', 'new_v_w_in', 'new_v_w_out', 'new_v_conv_w', 'new_v_conv_b', 'new_v_conv_ln_g', 'new_v_conv_ln_b', 'new_v_sgu_ln_g', 'new_v_sgu_ln_b', 'new_v_sgu_w', 'new_v_sgu_b', 'new_v_pl_norm_g', 'new_v_pl_gate_w', 'new_v_pl_proj_w', 'new_v_final_g']
TWIN_LEAF_KINDS = {'loss': 'loss', 'grad_x': 'grad_x', 'grad_norm_g': 'grad_w', 'grad_w_in': 'grad_w', 'grad_w_out': 'grad_w', 'grad_conv_w': 'grad_w', 'grad_conv_b': 'grad_w', 'grad_conv_ln_g': 'grad_w', 'grad_conv_ln_b': 'grad_w', 'grad_sgu_ln_g': 'grad_w', 'grad_sgu_ln_b': 'grad_w', 'grad_sgu_w': 'grad_w', 'grad_sgu_b': 'grad_w', 'grad_pl_norm_g': 'grad_w', 'grad_pl_gate_w': 'grad_w', 'grad_pl_proj_w': 'grad_w', 'grad_final_g': 'grad_w', 'delta_norm_g': 'delta_w', 'delta_w_in': 'delta_w', 'delta_w_out': 'delta_w', 'delta_conv_w': 'delta_w', 'delta_conv_b': 'delta_w', 'delta_conv_ln_g': 'delta_w', 'delta_conv_ln_b': 'delta_w', 'delta_sgu_ln_g': 'delta_w', 'delta_sgu_ln_b': 'delta_w', 'delta_sgu_w': 'delta_w', 'delta_sgu_b': 'delta_w', 'delta_pl_norm_g': 'delta_w', 'delta_pl_gate_w': 'delta_w', 'delta_pl_proj_w': 'delta_w', 'delta_final_g': 'delta_w', 'new_m_norm_g': 'new_m', 'new_m_w_in': 'new_m', 'new_m_w_out': 'new_m', 'new_m_conv_w': 'new_m', 'new_m_conv_b': 'new_m', 'new_m_conv_ln_g': 'new_m', 'new_m_conv_ln_b': 'new_m', 'new_m_sgu_ln_g': 'new_m', 'new_m_sgu_ln_b': 'new_m', 'new_m_sgu_w': 'new_m', 'new_m_sgu_b': 'new_m', 'new_m_pl_norm_g': 'new_m', 'new_m_pl_gate_w': 'new_m', 'new_m_pl_proj_w': 'new_m', 'new_m_final_g': 'new_m', 'new_v_norm_g': 'new_v', 'new_v_w_in': 'new_v', 'new_v_w_out': 'new_v', 'new_v_conv_w': 'new_v', 'new_v_conv_b': 'new_v', 'new_v_conv_ln_g': 'new_v', 'new_v_conv_ln_b': 'new_v', 'new_v_sgu_ln_g': 'new_v', 'new_v_sgu_ln_b': 'new_v', 'new_v_sgu_w': 'new_v', 'new_v_sgu_b': 'new_v', 'new_v_pl_norm_g': 'new_v', 'new_v_pl_gate_w': 'new_v', 'new_v_pl_proj_w': 'new_v', 'new_v_final_g': 'new_v'}


def _forward(args):
    return _fwd_reference(*[args[k] for k in FWD_PARAMS])


def _output_shape():
    out = _jax.eval_shape(lambda: _forward(_fwd_setup_inputs(0)))
    return out.shape, out.dtype

N_MICROBATCH = 1
ADAM_LR = 0.001
ADAM_B1 = 0.9
ADAM_B2 = 0.999
ADAM_EPS = 1e-08
ADAM_WD = 0.01
ADAM_STEP = 10
PER_EXAMPLE_BATCH_AXIS = {'x': 0, 'p': 1, 'loss_target': 0}
SHARED_INPUTS = []
_WEIGHT_DTYPES = {'norm_g': _jnp.float32, 'w_in': _jnp.float32, 'w_out': _jnp.float32, 'conv_w': _jnp.float32, 'conv_b': _jnp.float32, 'conv_ln_g': _jnp.float32, 'conv_ln_b': _jnp.float32, 'sgu_ln_g': _jnp.float32, 'sgu_ln_b': _jnp.float32, 'sgu_w': _jnp.float32, 'sgu_b': _jnp.float32, 'pl_norm_g': _jnp.float32, 'pl_gate_w': _jnp.float32, 'pl_proj_w': _jnp.float32, 'final_g': _jnp.float32}
MOMENT_SCALE = {'norm_g': 1.166391e-01, 'w_in': 4.826861e-02, 'w_out': 7.540610e-02, 'conv_w': 5.181913e-02, 'conv_b': 1.073591e-01, 'conv_ln_g': 6.337454e-02, 'conv_ln_b': 5.644643e-02, 'sgu_ln_g': 3.245886e-02, 'sgu_ln_b': 3.193748e-02, 'sgu_w': 4.679244e-02, 'sgu_b': 6.479206e-02, 'pl_norm_g': 3.970237e-02, 'pl_gate_w': 3.823355e-02, 'pl_proj_w': 9.759240e-02, 'final_g': 6.400192e+01}


def _to_microbatches(a, axis):
    t = _jnp.moveaxis(a, axis, 0)
    t = t.reshape((N_MICROBATCH, t.shape[0] // N_MICROBATCH) + t.shape[1:])
    return _jnp.moveaxis(t, 1, axis + 1)


def setup_inputs(seed: int = 0) -> dict:
    inp = _fwd_setup_inputs(seed)
    key = _jax.random.fold_in(_jax.random.key(seed), 7919)
    shape, _ = _output_shape()
    out = dict(inp)
    out["loss_target"] = _jax.random.normal(_jax.random.fold_in(key, 0), shape, _jnp.float32)
    for i, name in enumerate(TWIN_WEIGHTS):
        w = inp[name].astype(_jnp.float32)
        if MOMENT_SCALE is None:
            s = _jnp.sqrt(_jnp.mean(_jnp.square(w)) + 1e-30)
        else:
            s = MOMENT_SCALE[name]
        km, kv = _jax.random.split(_jax.random.fold_in(key, i + 1))
        out[name] = w
        out["m_" + name] = s * _jax.random.normal(km, w.shape, _jnp.float32)
        out["v_" + name] = (s * s) * _jax.random.uniform(kv, w.shape, _jnp.float32, 0.5, 1.5)
    if N_MICROBATCH > 1:
        for name, axis in PER_EXAMPLE_BATCH_AXIS.items():
            out[name] = _to_microbatches(out[name], axis)
    return {'x': out['x'], 'p': out['p'], 'norm_g': out['norm_g'], 'w_in': out['w_in'], 'w_out': out['w_out'], 'conv_w': out['conv_w'], 'conv_b': out['conv_b'], 'conv_ln_g': out['conv_ln_g'], 'conv_ln_b': out['conv_ln_b'], 'sgu_ln_g': out['sgu_ln_g'], 'sgu_ln_b': out['sgu_ln_b'], 'sgu_w': out['sgu_w'], 'sgu_b': out['sgu_b'], 'pl_norm_g': out['pl_norm_g'], 'pl_gate_w': out['pl_gate_w'], 'pl_proj_w': out['pl_proj_w'], 'final_g': out['final_g'], 'loss_target': out['loss_target'], 'm_norm_g': out['m_norm_g'], 'm_w_in': out['m_w_in'], 'm_w_out': out['m_w_out'], 'm_conv_w': out['m_conv_w'], 'm_conv_b': out['m_conv_b'], 'm_conv_ln_g': out['m_conv_ln_g'], 'm_conv_ln_b': out['m_conv_ln_b'], 'm_sgu_ln_g': out['m_sgu_ln_g'], 'm_sgu_ln_b': out['m_sgu_ln_b'], 'm_sgu_w': out['m_sgu_w'], 'm_sgu_b': out['m_sgu_b'], 'm_pl_norm_g': out['m_pl_norm_g'], 'm_pl_gate_w': out['m_pl_gate_w'], 'm_pl_proj_w': out['m_pl_proj_w'], 'm_final_g': out['m_final_g'], 'v_norm_g': out['v_norm_g'], 'v_w_in': out['v_w_in'], 'v_w_out': out['v_w_out'], 'v_conv_w': out['v_conv_w'], 'v_conv_b': out['v_conv_b'], 'v_conv_ln_g': out['v_conv_ln_g'], 'v_conv_ln_b': out['v_conv_ln_b'], 'v_sgu_ln_g': out['v_sgu_ln_g'], 'v_sgu_ln_b': out['v_sgu_ln_b'], 'v_sgu_w': out['v_sgu_w'], 'v_sgu_b': out['v_sgu_b'], 'v_pl_norm_g': out['v_pl_norm_g'], 'v_pl_gate_w': out['v_pl_gate_w'], 'v_pl_proj_w': out['v_pl_proj_w'], 'v_final_g': out['v_final_g']}


def _loss(weights, diff, rest, loss_target):
    with _jax.named_scope("forward"):
        args = {**rest, TWIN_DIFF_INPUT: diff, **{k: w.astype(_WEIGHT_DTYPES[k]) for k, w in weights.items()}}
        y = _forward(args)
    with _jax.named_scope("loss_head"):
        err = _jnp.square(y.astype(_jnp.float32) - loss_target)
        return 0.5 * _jnp.sum(_jnp.mean(err, axis=-1)) if err.ndim else 0.5 * err


def _adamw(w, g, m, v):
    m = ADAM_B1 * m + (1.0 - ADAM_B1) * g
    v = ADAM_B2 * v + (1.0 - ADAM_B2) * _jnp.square(g)
    m_hat = m / (1.0 - ADAM_B1 ** ADAM_STEP)
    v_hat = v / (1.0 - ADAM_B2 ** ADAM_STEP)
    delta = -ADAM_LR * (m_hat / (_jnp.sqrt(v_hat) + ADAM_EPS) + ADAM_WD * w)
    return delta, m, v


def reference(x, p, norm_g, w_in, w_out, conv_w, conv_b, conv_ln_g, conv_ln_b, sgu_ln_g, sgu_ln_b, sgu_w, sgu_b, pl_norm_g, pl_gate_w, pl_proj_w, final_g, loss_target, m_norm_g, m_w_in, m_w_out, m_conv_w, m_conv_b, m_conv_ln_g, m_conv_ln_b, m_sgu_ln_g, m_sgu_ln_b, m_sgu_w, m_sgu_b, m_pl_norm_g, m_pl_gate_w, m_pl_proj_w, m_final_g, v_norm_g, v_w_in, v_w_out, v_conv_w, v_conv_b, v_conv_ln_g, v_conv_ln_b, v_sgu_ln_g, v_sgu_ln_b, v_sgu_w, v_sgu_b, v_pl_norm_g, v_pl_gate_w, v_pl_proj_w, v_final_g):
    given = dict(x=x, p=p, norm_g=norm_g, w_in=w_in, w_out=w_out, conv_w=conv_w, conv_b=conv_b, conv_ln_g=conv_ln_g, conv_ln_b=conv_ln_b, sgu_ln_g=sgu_ln_g, sgu_ln_b=sgu_ln_b, sgu_w=sgu_w, sgu_b=sgu_b, pl_norm_g=pl_norm_g, pl_gate_w=pl_gate_w, pl_proj_w=pl_proj_w, final_g=final_g, loss_target=loss_target, m_norm_g=m_norm_g, m_w_in=m_w_in, m_w_out=m_w_out, m_conv_w=m_conv_w, m_conv_b=m_conv_b, m_conv_ln_g=m_conv_ln_g, m_conv_ln_b=m_conv_ln_b, m_sgu_ln_g=m_sgu_ln_g, m_sgu_ln_b=m_sgu_ln_b, m_sgu_w=m_sgu_w, m_sgu_b=m_sgu_b, m_pl_norm_g=m_pl_norm_g, m_pl_gate_w=m_pl_gate_w, m_pl_proj_w=m_pl_proj_w, m_final_g=m_final_g, v_norm_g=v_norm_g, v_w_in=v_w_in, v_w_out=v_w_out, v_conv_w=v_conv_w, v_conv_b=v_conv_b, v_conv_ln_g=v_conv_ln_g, v_conv_ln_b=v_conv_ln_b, v_sgu_ln_g=v_sgu_ln_g, v_sgu_ln_b=v_sgu_ln_b, v_sgu_w=v_sgu_w, v_sgu_b=v_sgu_b, v_pl_norm_g=v_pl_norm_g, v_pl_gate_w=v_pl_gate_w, v_pl_proj_w=v_pl_proj_w, v_final_g=v_final_g)
    weights = {n: given[n] for n in TWIN_WEIGHTS}
    shared = {n: given[n] for n in SHARED_INPUTS}
    per_example = {n: given[n] for n in ['x', 'p']}
    grad_fn = _jax.value_and_grad(_loss, argnums=(0, 1))

    def one_microbatch(ex, loss_target):
        ex = dict(ex)
        diff = ex.pop(TWIN_DIFF_INPUT)
        return grad_fn(weights, diff, {**shared, **ex}, loss_target)

    if N_MICROBATCH == 1:
        loss, (grad_w, grad_x) = one_microbatch(per_example, given["loss_target"])
    else:
        def body(carry, xs):
            loss_sum, grad_sum = carry
            l_k, (gw_k, gx_k) = one_microbatch(xs[0], xs[1])
            with _jax.named_scope("update"):
                return (loss_sum + l_k, _jax.tree.map(_jnp.add, grad_sum, gw_k)), gx_k

        init = (_jnp.zeros((), _jnp.float32), _jax.tree.map(_jnp.zeros_like, weights))
        (loss, grad_w), grad_x = _jax.lax.scan(body, init, (per_example, given["loss_target"]))
    with _jax.named_scope("update"):
        delta_w, new_m, new_v = {}, {}, {}
        for n in TWIN_WEIGHTS:
            delta_w[n], new_m[n], new_v[n] = _adamw(weights[n], grad_w[n], given["m_" + n], given["v_" + n])
    return (loss, grad_x, *[grad_w[n] for n in TWIN_WEIGHTS], *[delta_w[n] for n in TWIN_WEIGHTS],
            *[new_m[n] for n in TWIN_WEIGHTS], *[new_v[n] for n in TWIN_WEIGHTS])
```

```python
import math

import jax
import jax.numpy as jnp
from jax import lax
from jax.experimental import pallas as pl
from jax.experimental.pallas import tpu as pltpu

F32 = jnp.float32
MXU_DTYPE = jnp.bfloat16
WIRE_DTYPE = jnp.bfloat16

EPS = 1e-6
CONV_K = 31
CHUNK = 128
GROUPS = 8
HALO = 32
N_DEV = 8
DEPTH = 4

ADAM_LR = 0.001
ADAM_B1 = 0.9
ADAM_B2 = 0.999
ADAM_EPS = 1e-08
ADAM_WD = 0.01
ADAM_STEP = 10

TM_IN = 512
TM_MIX = 256
TM_OUT = 512
CONV_RC = 32
CONV_CC = 512
PACK_COLS = 1024

MESH_ID = pl.DeviceIdType.MESH
INV_SQRT2 = 1.0 / math.sqrt(2.0)
INV_SQRT_2PI = 1.0 / math.sqrt(2.0 * math.pi)


def _params(n_grid, vmem_mb):
    return pltpu.CompilerParams(dimension_semantics=("arbitrary",) * n_grid, vmem_limit_bytes=vmem_mb << 20)


def _whole(shape):
    nd = len(shape)
    return pl.BlockSpec(shape, lambda *_: (0,) * nd, pipeline_mode=pl.Buffered(1))


def _acc_out(shape):
    nd = len(shape)
    return pl.BlockSpec(shape, lambda *_: (0,) * nd)


def _dot(a, b):
    return jnp.dot(a.astype(MXU_DTYPE), b.astype(MXU_DTYPE), preferred_element_type=F32)


def _dot_nt(a, b):
    return lax.dot_general(a.astype(MXU_DTYPE), b.astype(MXU_DTYPE), (((1,), (1,)), ((), ())),
                           preferred_element_type=F32)


def _dot_tn(a, b):
    return lax.dot_general(a.astype(MXU_DTYPE), b.astype(MXU_DTYPE), (((0,), (0,)), ((), ())),
                           preferred_element_type=F32)


def _sigmoid(x):
    return jax.nn.sigmoid(x)


def _rms_rstd(x):
    return lax.rsqrt(jnp.mean(x * x, axis=-1, keepdims=True) + EPS)


def _rms_bwd(dy, x, rstd, g):
    gy = dy * g
    xr = x * rstd
    dx = rstd * (gy - xr * jnp.mean(gy * xr, axis=-1, keepdims=True))
    dg = jnp.sum(dy * xr, axis=0, keepdims=True)
    return dx, dg


def _ln_stats(x):
    mu = jnp.mean(x, axis=-1, keepdims=True)
    xc = x - mu
    var = jnp.mean(xc * xc, axis=-1, keepdims=True)
    rstd = lax.rsqrt(var + EPS)
    return xc * rstd, rstd


def _ln_bwd(dxhat, xhat, rstd):
    return rstd * (dxhat - jnp.mean(dxhat, axis=-1, keepdims=True)
                   - xhat * jnp.mean(dxhat * xhat, axis=-1, keepdims=True))


def _silu_grad(x, s):
    return s * (1.0 + x * (1.0 - s))


def _tril_mask():
    r = lax.broadcasted_iota(jnp.int32, (CHUNK, CHUNK), 0)
    c = lax.broadcasted_iota(jnp.int32, (CHUNK, CHUNK), 1)
    return r >= c


def _conv_apply(src_ref, w_ref, base, tm, e, flip, emit):
    for r0 in range(0, tm, CONV_RC):
        for c0 in range(0, e, CONV_CC):
            acc = None
            for k in range(CONV_K):
                wk = (CONV_K - 1 - k) if flip else k
                term = w_ref[wk:wk + 1, c0:c0 + CONV_CC] * src_ref[pl.ds(base + k + r0, CONV_RC), c0:c0 + CONV_CC]
                acc = term if acc is None else acc + term
            emit(r0, c0, acc)


def _mesh_pos():
    return lax.axis_index("x"), lax.axis_index("y"), lax.axis_index("c")


def _slot(px, py, pc):
    return 4 * px + 2 * py + pc


def _all_gather(items, name):
    n = len(items)

    def body(*refs):
        in_refs, out_refs = refs[:n], refs[n:2 * n]
        send_sems, recv_sems, local_sems = refs[2 * n:]
        x, y, c = _mesh_pos()
        me, sibling = (x, y, c), (x, y, 1 - c)
        chips = [(1 - x, y), (x, 1 - y), (1 - x, 1 - y)]

        def copy(i, k, block, to, src=None):
            dst = out_refs[i].at[_slot(*block)]
            return pltpu.make_async_remote_copy(
                src_ref=dst if src is None else src, dst_ref=dst,
                send_sem=send_sems.at[i * 7 + k], recv_sem=recv_sems.at[i * 7 + k],
                device_id=to, device_id_type=MESH_ID)

        mine = [pltpu.make_async_copy(in_refs[i], out_refs[i].at[_slot(*me)], local_sems.at[i]) for i in range(n)]
        for cp in mine:
            cp.start()
        first = []
        for i in range(n):
            first.append(copy(i, 0, me, sibling, src=in_refs[i]))
            for j, chip in enumerate(chips):
                first.append(copy(i, 1 + j, me, (*chip, c), src=in_refs[i]))
        for cp in first:
            cp.start()
        passed = []
        for j, chip in enumerate(chips):
            for i in range(n):
                copy(i, 1 + j, (*chip, c), me).wait_recv()
                fwd = copy(i, 4 + j, (*chip, c), sibling)
                fwd.start()
                passed.append(fwd)
        for i in range(n):
            copy(i, 0, sibling, me).wait_recv()
            for j, chip in enumerate(chips):
                copy(i, 4 + j, (*chip, 1 - c), me).wait_recv()
        for cp in first + passed:
            cp.wait_send()
        for cp in mine:
            cp.wait()

    any_spec = pl.BlockSpec(memory_space=pl.ANY)
    return pl.pallas_call(
        body, name=name,
        out_shape=[jax.ShapeDtypeStruct((N_DEV,) + a.shape, a.dtype) for a in items],
        in_specs=[any_spec] * n, out_specs=[any_spec] * n,
        scratch_shapes=[pltpu.SemaphoreType.DMA((7 * n,)), pltpu.SemaphoreType.DMA((7 * n,)),
                        pltpu.SemaphoreType.DMA((n,))],
    )(*items)


def _all_to_all(groups, name):
    flat = [(gi, li, a) for gi, grp in enumerate(groups) for li, a in enumerate(grp)]
    n, ng = len(flat), len(groups)

    def body(*refs):
        in_refs, out_refs = refs[:n], refs[n:n + ng]
        send_sems, recv_sems, local_sems = refs[n + ng:]
        x, y, c = _mesh_pos()
        me = _slot(x, y, c)
        local, sends, recvs = [], [], []
        for i, (gi, li, _) in enumerate(flat):
            local.append(pltpu.make_async_copy(in_refs[i].at[me], out_refs[gi].at[me, li], local_sems.at[i]))
            for k in range(1, N_DEV):
                px = (1 - x) if (k & 4) else x
                py = (1 - y) if (k & 2) else y
                pc = (1 - c) if (k & 1) else c
                peer = _slot(px, py, pc)
                sem = i * 7 + k - 1
                sends.append(pltpu.make_async_remote_copy(
                    src_ref=in_refs[i].at[peer], dst_ref=out_refs[gi].at[me, li],
                    send_sem=send_sems.at[sem], recv_sem=recv_sems.at[sem],
                    device_id=(px, py, pc), device_id_type=MESH_ID))
                recvs.append(pltpu.make_async_remote_copy(
                    src_ref=in_refs[i].at[peer], dst_ref=out_refs[gi].at[peer, li],
                    send_sem=send_sems.at[sem], recv_sem=recv_sems.at[sem],
                    device_id=(px, py, pc), device_id_type=MESH_ID))
        for cp in local:
            cp.start()
        for cp in sends:
            cp.start()
        for cp in recvs:
            cp.wait_recv()
        for cp in sends:
            cp.wait_send()
        for cp in local:
            cp.wait()

    any_spec = pl.BlockSpec(memory_space=pl.ANY)
    return pl.pallas_call(
        body, name=name,
        out_shape=[jax.ShapeDtypeStruct((N_DEV, len(grp)) + grp[0].shape[1:], grp[0].dtype) for grp in groups],
        in_specs=[any_spec] * n, out_specs=[any_spec] * ng,
        scratch_shapes=[pltpu.SemaphoreType.DMA((7 * n,)), pltpu.SemaphoreType.DMA((7 * n,)),
                        pltpu.SemaphoreType.DMA((n,))],
    )(*[a for _, _, a in flat])


def _inproj(x, g, w_blk):
    t, d = x.shape
    nb, _, bn = w_blk.shape
    tm = TM_IN

    def body(x_ref, g_ref, w_ref, h_ref, proj_ref, hs):
        @pl.when(pl.program_id(1) == 0)
        def _():
            xv = x_ref[...]
            hv = (xv * _rms_rstd(xv) * g_ref[...]).astype(MXU_DTYPE)
            hs[...] = hv
            h_ref[...] = hv
        proj_ref[...] = jnp.dot(hs[...], w_ref[0], preferred_element_type=F32)

    return pl.pallas_call(
        body, name="inproj", grid=(t // tm, nb),
        in_specs=[pl.BlockSpec((tm, d), lambda i, j: (i, 0)), _whole((1, d)),
                  pl.BlockSpec((1, d, bn), lambda i, j: (j, 0, 0))],
        out_specs=[pl.BlockSpec((tm, d), lambda i, j: (i, 0)), pl.BlockSpec((tm, bn), lambda i, j: (i, j))],
        out_shape=[jax.ShapeDtypeStruct((t, d), MXU_DTYPE), jax.ShapeDtypeStruct((t, nb * bn), F32)],
        scratch_shapes=[pltpu.VMEM((tm, d), MXU_DTYPE)],
        compiler_params=_params(2, 40),
    )(x, g, w_blk)


def _conv_fwd(proj, cw, cb, lg, lb, seq):
    t = proj.shape[0]
    e = proj.shape[1] // 3
    tm = TM_MIX
    nt = seq // tm
    hb = tm // HALO

    def body(a_ref, b_ref, z_ref, ah_ref, bh_ref, cw_ref, cb_ref, lg_ref, lb_ref, y1_ref, y_ref, q_ref, y0s):
        first = lax.rem(pl.program_id(0), nt) == 0
        y0s[pl.ds(HALO, tm), :] = a_ref[...] * _sigmoid(b_ref[...])

        @pl.when(first)
        def _():
            y0s[pl.ds(0, HALO), :] = jnp.zeros((HALO, e), F32)

        @pl.when(jnp.logical_not(first))
        def _():
            y0s[pl.ds(0, HALO), :] = ah_ref[...] * _sigmoid(bh_ref[...])

        def emit(r0, c0, acc):
            y1_ref[r0:r0 + CONV_RC, c0:c0 + CONV_CC] = acc + cb_ref[:, c0:c0 + CONV_CC]
        _conv_apply(y0s, cw_ref, HALO - (CONV_K - 1), tm, e, False, emit)

        xhat, _ = _ln_stats(y1_ref[...])
        y2 = xhat * lg_ref[...] + lb_ref[...]
        y = y2 * _sigmoid(y2)
        y_ref[...] = y
        z = z_ref[...]
        q_ref[...] = (y * (z * _sigmoid(z))).astype(q_ref.dtype)

    tile = lambda col: pl.BlockSpec((tm, e), lambda i: (i, col))
    prev = lambda col: pl.BlockSpec((HALO, e), lambda i: (jnp.maximum(i * hb - 1, 0), col))
    return pl.pallas_call(
        body, name="conv_fwd", grid=(t // tm,),
        in_specs=[tile(0), tile(1), tile(2), prev(0), prev(1), _whole(cw.shape), _whole((1, e)), _whole((1, e)),
                  _whole((1, e))],
        out_specs=[tile(0), tile(0), tile(0)],
        out_shape=[jax.ShapeDtypeStruct((t, e), F32), jax.ShapeDtypeStruct((t, e), F32),
                   jax.ShapeDtypeStruct((t, e), MXU_DTYPE)],
        scratch_shapes=[pltpu.VMEM((tm + HALO, e), F32)],
        compiler_params=_params(1, 48),
    )(proj, proj, proj, proj, proj, cw, cb, lg, lb)


def _sgu_parts(a, b, lg, lb, sw_ref, sbt_ref, mixed_s, tm, e):
    eg = e // GROUPS
    ea = lax.erf(a * INV_SQRT2)
    eb = lax.erf(b * INV_SQRT2)
    u = 0.5 * a * (1.0 + ea)
    v0 = 0.5 * b * (1.0 + eb)
    xhat, rstd = _ln_stats(v0)
    v = (xhat * lg + lb).astype(MXU_DTYPE)
    mask = _tril_mask()
    for g in range(GROUPS):
        wt = jnp.where(mask, sw_ref[g], 0.0).astype(MXU_DTYPE)
        bcol = sbt_ref[:, g:g + 1]
        for ch in range(tm // CHUNK):
            rows = slice(ch * CHUNK, (ch + 1) * CHUNK)
            cols = slice(g * eg, (g + 1) * eg)
            mixed_s[rows, cols] = jnp.dot(wt, v[rows, cols], preferred_element_type=F32) + bcol
    return ea, eb, u, xhat, rstd, v


def _sgu_fwd(proj, lg, lb, sw, sbt):
    t = proj.shape[0]
    e = proj.shape[1] // 3
    tm = TM_MIX

    def body(a_ref, b_ref, z_ref, lg_ref, lb_ref, sw_ref, sbt_ref, y_ref, q_ref, mixed_s):
        _, _, u, _, _, _ = _sgu_parts(a_ref[...], b_ref[...], lg_ref[...], lb_ref[...], sw_ref, sbt_ref, mixed_s, tm, e)
        y = u * mixed_s[...]
        y_ref[...] = y
        z = z_ref[...]
        q_ref[...] = (y * (z * _sigmoid(z))).astype(q_ref.dtype)

    tile = lambda col: pl.BlockSpec((tm, e), lambda i: (i, col))
    return pl.pallas_call(
        body, name="sgu_fwd", grid=(t // tm,),
        in_specs=[tile(0), tile(1), tile(2), _whole((1, e)), _whole((1, e)), _whole(sw.shape), _whole(sbt.shape)],
        out_specs=[tile(0), tile(0)],
        out_shape=[jax.ShapeDtypeStruct((t, e), F32), jax.ShapeDtypeStruct((t, e), MXU_DTYPE)],
        scratch_shapes=[pltpu.VMEM((tm, e), F32)],
        compiler_params=_params(1, 48),
    )(proj, proj, proj, lg, lb, sw, sbt)


def _outproj_fwd(q, x, p_l, w_out, plg, gate_w, proj_w):
    t, d = x.shape
    e = q.shape[1]
    pd = p_l.shape[1]
    tm = TM_OUT

    def body(q_ref, x_ref, p_ref, wo_ref, plg_ref, gw_ref, pw_ref, x1_ref, x2_ref):
        x1 = x_ref[...] + jnp.dot(q_ref[...], wo_ref[...], preferred_element_type=F32)
        x1_ref[...] = x1
        rn = x1 * _rms_rstd(x1) * plg_ref[...]
        gate = _sigmoid(_dot(rn, gw_ref[...]))
        pp = _dot(p_ref[...], pw_ref[...])
        x2_ref[...] = x1 + gate * pp

    row = lambda w: pl.BlockSpec((tm, w), lambda i: (i, 0))
    return pl.pallas_call(
        body, name="outproj_fwd", grid=(t // tm,),
        in_specs=[row(e), row(d), row(pd), _whole((e, d)), _whole((1, d)), _whole((d, d)), _whole((pd, d))],
        out_specs=[row(d), row(d)],
        out_shape=[jax.ShapeDtypeStruct((t, d), F32), jax.ShapeDtypeStruct((t, d), F32)],
        compiler_params=_params(1, 48),
    )(q, x, p_l, w_out, plg, gate_w, proj_w)


def _loss_head(xf, fg, tgt):
    t, d = xf.shape
    tm = TM_OUT
    nsteps = t // tm

    def body(x_ref, g_ref, t_ref, loss_ref, dx_ref, dg_ref, sq_s):
        i = pl.program_id(0)

        @pl.when(i == 0)
        def _():
            sq_s[...] = jnp.zeros_like(sq_s)
            dg_ref[...] = jnp.zeros_like(dg_ref)
        x = x_ref[...]
        rstd = _rms_rstd(x)
        err = x * rstd * g_ref[...] - t_ref[...]
        sq_s[...] += jnp.sum(err * err, axis=0, keepdims=True)
        dx, dg = _rms_bwd(err * (1.0 / d), x, rstd, g_ref[...])
        dx_ref[...] = dx
        dg_ref[...] += dg

        @pl.when(i == nsteps - 1)
        def _():
            loss_ref[...] = jnp.sum(sq_s[...], axis=1, keepdims=True) * (0.5 / d)

    row = pl.BlockSpec((tm, d), lambda i: (i, 0))
    return pl.pallas_call(
        body, name="loss_head", grid=(nsteps,),
        in_specs=[row, _whole((1, d)), row],
        out_specs=[_acc_out((1, 1)), row, _acc_out((1, d))],
        out_shape=[jax.ShapeDtypeStruct((1, 1), F32), jax.ShapeDtypeStruct((t, d), F32),
                   jax.ShapeDtypeStruct((1, d), F32)],
        scratch_shapes=[pltpu.VMEM((1, d), F32)],
        compiler_params=_params(1, 32),
    )(xf, fg, tgt)


def _ple_bwd(dx2, x1, p_l, plg, gate_w, proj_w):
    t, d = x1.shape
    pd = p_l.shape[1]
    tm = TM_OUT
    nsteps = t // tm
    bn = d // N_DEV

    def body(dx2_ref, x1_ref, p_ref, plg_ref, gw_ref, pw_ref, dx1_ref, dgw_ref, dpw_ref, dplg_ref, gw_acc, pw_acc):
        i = pl.program_id(0)

        @pl.when(i == 0)
        def _():
            gw_acc[...] = jnp.zeros_like(gw_acc)
            pw_acc[...] = jnp.zeros_like(pw_acc)
            dplg_ref[...] = jnp.zeros_like(dplg_ref)
        dx2 = dx2_ref[...]
        x1 = x1_ref[...]
        plg = plg_ref[...]
        rstd = _rms_rstd(x1)
        rn = (x1 * rstd * plg).astype(MXU_DTYPE)
        gate = _sigmoid(jnp.dot(rn, gw_ref[...], preferred_element_type=F32))
        p_b = p_ref[...].astype(MXU_DTYPE)
        pp = jnp.dot(p_b, pw_ref[...], preferred_element_type=F32)
        dpp = (dx2 * gate).astype(MXU_DTYPE)
        dgpre = (dx2 * pp * gate * (1.0 - gate)).astype(MXU_DTYPE)
        pw_acc[...] += _dot_tn(p_b, dpp)
        gw_acc[...] += _dot_tn(rn, dgpre)
        drn = _dot_nt(dgpre, gw_ref[...])
        dx, dg = _rms_bwd(drn, x1, rstd, plg)
        dx1_ref[...] = dx2 + dx
        dplg_ref[...] += dg

        @pl.when(i == nsteps - 1)
        def _():
            dgw_ref[...] = gw_acc[...].astype(dgw_ref.dtype)
            for j in range(N_DEV):
                dpw_ref[j] = pw_acc[:, j * bn:(j + 1) * bn].astype(dpw_ref.dtype)

    row = lambda w: pl.BlockSpec((tm, w), lambda i: (i, 0))
    return pl.pallas_call(
        body, name="ple_bwd", grid=(nsteps,),
        in_specs=[row(d), row(d), row(pd), _whole((1, d)), _whole((d, d)), _whole((pd, d))],
        out_specs=[row(d), _acc_out((d, d)), _acc_out((N_DEV, pd, bn)), _acc_out((1, d))],
        out_shape=[jax.ShapeDtypeStruct((t, d), F32), jax.ShapeDtypeStruct((d, d), WIRE_DTYPE),
                   jax.ShapeDtypeStruct((N_DEV, pd, bn), WIRE_DTYPE), jax.ShapeDtypeStruct((1, d), F32)],
        scratch_shapes=[pltpu.VMEM((d, d), F32), pltpu.VMEM((pd, d), F32)],
        compiler_params=_params(1, 48),
    )(dx2, x1, p_l, plg, gate_w, proj_w)


def _outproj_bwd(dx1, y, proj, w_out):
    t, d = dx1.shape
    e = y.shape[1]
    tm = TM_MIX
    nsteps = t // tm

    def body(dx1_ref, y_ref, z_ref, wo_ref, dy_ref, dz_ref, dwo_ref, wo_acc):
        i = pl.program_id(0)

        @pl.when(i == 0)
        def _():
            wo_acc[...] = jnp.zeros_like(wo_acc)
        dx1 = dx1_ref[...].astype(MXU_DTYPE)
        y = y_ref[...]
        z = z_ref[...]
        s = _sigmoid(z)
        sz = z * s
        q = (y * sz).astype(MXU_DTYPE)
        wo_acc[...] += _dot_tn(q, dx1)
        dq = _dot_nt(dx1, wo_ref[...])
        dy_ref[...] = dq * sz
        dz_ref[...] = (dq * y * _silu_grad(z, s)).astype(dz_ref.dtype)

        @pl.when(i == nsteps - 1)
        def _():
            dwo_ref[...] = wo_acc[...].astype(dwo_ref.dtype)

    return pl.pallas_call(
        body, name="outproj_bwd", grid=(nsteps,),
        in_specs=[pl.BlockSpec((tm, d), lambda i: (i, 0)), pl.BlockSpec((tm, e), lambda i: (i, 0)),
                  pl.BlockSpec((tm, e), lambda i: (i, 2)), _whole((e, d))],
        out_specs=[pl.BlockSpec((tm, e), lambda i: (i, 0)), pl.BlockSpec((tm, e), lambda i: (i, 0)),
                   _acc_out((e, d))],
        out_shape=[jax.ShapeDtypeStruct((t, e), F32), jax.ShapeDtypeStruct((t, e), MXU_DTYPE),
                   jax.ShapeDtypeStruct((e, d), WIRE_DTYPE)],
        scratch_shapes=[pltpu.VMEM((e, d), F32)],
        compiler_params=_params(1, 48),
    )(dx1, y, proj, w_out)


def _conv_ln_bwd(dy, y1, lg, lb):
    t, e = dy.shape
    tm = TM_MIX

    def body(dy_ref, y1_ref, lg_ref, lb_ref, dy1_ref, dlg_ref, dlb_ref, dcb_ref):
        @pl.when(pl.program_id(0) == 0)
        def _():
            dlg_ref[...] = jnp.zeros_like(dlg_ref)
            dlb_ref[...] = jnp.zeros_like(dlb_ref)
            dcb_ref[...] = jnp.zeros_like(dcb_ref)
        xhat, rstd = _ln_stats(y1_ref[...])
        lg = lg_ref[...]
        y2 = xhat * lg + lb_ref[...]
        dy2 = dy_ref[...] * _silu_grad(y2, _sigmoid(y2))
        dlg_ref[...] += jnp.sum(dy2 * xhat, axis=0, keepdims=True)
        dlb_ref[...] += jnp.sum(dy2, axis=0, keepdims=True)
        dy1 = _ln_bwd(dy2 * lg, xhat, rstd)
        dy1_ref[...] = dy1
        dcb_ref[...] += jnp.sum(dy1, axis=0, keepdims=True)

    row = pl.BlockSpec((tm, e), lambda i: (i, 0))
    return pl.pallas_call(
        body, name="conv_ln_bwd", grid=(t // tm,),
        in_specs=[row, row, _whole((1, e)), _whole((1, e))],
        out_specs=[row, _acc_out((1, e)), _acc_out((1, e)), _acc_out((1, e))],
        out_shape=[jax.ShapeDtypeStruct((t, e), F32)] + [jax.ShapeDtypeStruct((1, e), F32)] * 3,
        compiler_params=_params(1, 48),
    )(dy, y1, lg, lb)


def _conv_bwd(dy1, proj, dz, cw, seq):
    t, e = dy1.shape
    tm = TM_MIX
    nt = seq // tm
    hb = tm // HALO
    n_halo_blocks = t // HALO

    def body(d_ref, dn_ref, a_ref, b_ref, ah_ref, bh_ref, dz_ref, cw_ref, dproj_ref, dcw_ref, y0s, d1s):
        i = pl.program_id(0)
        pos = lax.rem(i, nt)

        @pl.when(i == 0)
        def _():
            dcw_ref[...] = jnp.zeros_like(dcw_ref)
        a = a_ref[...]
        sb = _sigmoid(b_ref[...])
        y0s[pl.ds(HALO, tm), :] = a * sb
        d1s[pl.ds(0, tm), :] = d_ref[...]

        @pl.when(pos == 0)
        def _():
            y0s[pl.ds(0, HALO), :] = jnp.zeros((HALO, e), F32)

        @pl.when(pos != 0)
        def _():
            y0s[pl.ds(0, HALO), :] = ah_ref[...] * _sigmoid(bh_ref[...])

        @pl.when(pos == nt - 1)
        def _():
            d1s[pl.ds(tm, HALO), :] = jnp.zeros((HALO, e), F32)

        @pl.when(pos != nt - 1)
        def _():
            d1s[pl.ds(tm, HALO), :] = dn_ref[...]

        for c0 in range(0, e, CONV_CC):
            dcur = d_ref[:, c0:c0 + CONV_CC]
            for k in range(CONV_K):
                win = y0s[pl.ds(HALO - (CONV_K - 1) + k, tm), c0:c0 + CONV_CC]
                dcw_ref[k:k + 1, c0:c0 + CONV_CC] += jnp.sum(dcur * win, axis=0, keepdims=True)

        def emit(r0, c0, dy0):
            rs, cs = slice(r0, r0 + CONV_RC), slice(c0, c0 + CONV_CC)
            sbv = _sigmoid(b_ref[rs, cs])
            av = a_ref[rs, cs]
            dproj_ref[rs, c0:c0 + CONV_CC] = (dy0 * sbv).astype(dproj_ref.dtype)
            dproj_ref[rs, e + c0:e + c0 + CONV_CC] = (dy0 * av * sbv * (1.0 - sbv)).astype(dproj_ref.dtype)
        _conv_apply(d1s, cw_ref, 0, tm, e, True, emit)
        dproj_ref[:, 2 * e:3 * e] = dz_ref[...]

    tile = lambda col: pl.BlockSpec((tm, e), lambda i: (i, col))
    prev = lambda col: pl.BlockSpec((HALO, e), lambda i: (jnp.maximum(i * hb - 1, 0), col))
    nxt = pl.BlockSpec((HALO, e), lambda i: (jnp.minimum((i + 1) * hb, n_halo_blocks - 1), 0))
    return pl.pallas_call(
        body, name="conv_bwd", grid=(t // tm,),
        in_specs=[tile(0), nxt, tile(0), tile(1), prev(0), prev(1), tile(0), _whole(cw.shape)],
        out_specs=[pl.BlockSpec((tm, 3 * e), lambda i: (i, 0)), _acc_out(cw.shape)],
        out_shape=[jax.ShapeDtypeStruct((t, 3 * e), MXU_DTYPE), jax.ShapeDtypeStruct(cw.shape, F32)],
        scratch_shapes=[pltpu.VMEM((tm + HALO, e), F32), pltpu.VMEM((tm + HALO, e), F32)],
        compiler_params=_params(1, 56),
    )(dy1, dy1, proj, proj, proj, proj, dz, cw)


def _sgu_bwd(dy, proj, dz, lg, lb, sw, swt, sbt):
    t, e = dy.shape
    eg = e // GROUPS
    tm = TM_MIX
    nsteps = t // tm

    def body(dy_ref, a_ref, b_ref, dz_ref, lg_ref, lb_ref, sw_ref, swt_ref, sbt_ref,
             dproj_ref, dsw_ref, dsb_ref, dlg_ref, dlb_ref, mixed_s, dv_s, sb_acc):
        i = pl.program_id(0)

        @pl.when(i == 0)
        def _():
            dsw_ref[...] = jnp.zeros_like(dsw_ref)
            sb_acc[...] = jnp.zeros_like(sb_acc)
            dlg_ref[...] = jnp.zeros_like(dlg_ref)
            dlb_ref[...] = jnp.zeros_like(dlb_ref)
        a = a_ref[...]
        b = b_ref[...]
        lg = lg_ref[...]
        ea, eb, u, xhat, rstd, v = _sgu_parts(a, b, lg, lb_ref[...], sw_ref, sbt_ref, mixed_s, tm, e)
        dy = dy_ref[...]
        du = dy * mixed_s[...]
        dmixed = (dy * u).astype(MXU_DTYPE)
        mask = _tril_mask()
        mask_t = (lax.broadcasted_iota(jnp.int32, (CHUNK, CHUNK), 0)
                  <= lax.broadcasted_iota(jnp.int32, (CHUNK, CHUNK), 1))
        ones = jnp.ones((8, eg), MXU_DTYPE)
        for g in range(GROUPS):
            wtt = jnp.where(mask_t, swt_ref[g], 0.0).astype(MXU_DTYPE)
            cols = slice(g * eg, (g + 1) * eg)
            for ch in range(tm // CHUNK):
                rows = slice(ch * CHUNK, (ch + 1) * CHUNK)
                dm = dmixed[rows, cols]
                dv_s[rows, cols] = jnp.dot(wtt, dm, preferred_element_type=F32)
                dsw_ref[g] += _dot_nt(dm, v[rows, cols])
                sb_acc[g] += _dot_nt(ones, dm)
        dv = dv_s[...]
        dlg_ref[...] += jnp.sum(dv * xhat, axis=0, keepdims=True)
        dlb_ref[...] += jnp.sum(dv, axis=0, keepdims=True)
        dv0 = _ln_bwd(dv * lg, xhat, rstd)
        pdf_a = jnp.exp(-0.5 * a * a) * INV_SQRT_2PI
        pdf_b = jnp.exp(-0.5 * b * b) * INV_SQRT_2PI
        dproj_ref[:, 0:e] = (du * (0.5 * (1.0 + ea) + a * pdf_a)).astype(dproj_ref.dtype)
        dproj_ref[:, e:2 * e] = (dv0 * (0.5 * (1.0 + eb) + b * pdf_b)).astype(dproj_ref.dtype)
        dproj_ref[:, 2 * e:3 * e] = dz_ref[...]

        @pl.when(i == nsteps - 1)
        def _():
            for g in range(GROUPS):
                dsw_ref[g] = jnp.where(mask, dsw_ref[g], 0.0)
                dsb_ref[g:g + 1, :] = sb_acc[g, 0:1, :]

    tile = lambda col: pl.BlockSpec((tm, e), lambda i: (i, col))
    return pl.pallas_call(
        body, name="sgu_bwd", grid=(nsteps,),
        in_specs=[tile(0), tile(0), tile(1), tile(0), _whole((1, e)), _whole((1, e)), _whole(sw.shape),
                  _whole(swt.shape), _whole(sbt.shape)],
        out_specs=[pl.BlockSpec((tm, 3 * e), lambda i: (i, 0)), _acc_out(sw.shape), _acc_out((GROUPS, CHUNK)),
                   _acc_out((1, e)), _acc_out((1, e))],
        out_shape=[jax.ShapeDtypeStruct((t, 3 * e), MXU_DTYPE), jax.ShapeDtypeStruct(sw.shape, F32),
                   jax.ShapeDtypeStruct((GROUPS, CHUNK), F32), jax.ShapeDtypeStruct((1, e), F32),
                   jax.ShapeDtypeStruct((1, e), F32)],
        scratch_shapes=[pltpu.VMEM((tm, e), F32), pltpu.VMEM((tm, e), F32), pltpu.VMEM((GROUPS, 8, CHUNK), F32)],
        compiler_params=_params(1, 56),
    )(dy, proj, proj, dz, lg, lb, sw, swt, sbt)


def _inproj_bwd_x(dproj, w_blk, dx1, x, g):
    t, d = x.shape
    nb, _, bn = w_blk.shape
    tm = TM_IN

    def body(dp_ref, w_ref, dx1_ref, x_ref, g_ref, dx_ref, dg_ref, dh_acc):
        i, j = pl.program_id(0), pl.program_id(1)

        @pl.when((i == 0) & (j == 0))
        def _():
            dg_ref[...] = jnp.zeros_like(dg_ref)

        @pl.when(j == 0)
        def _():
            dh_acc[...] = jnp.zeros_like(dh_acc)
        dh_acc[...] += _dot_nt(dp_ref[...], w_ref[0])

        @pl.when(j == nb - 1)
        def _():
            xv = x_ref[...]
            dx, dg = _rms_bwd(dh_acc[...], xv, _rms_rstd(xv), g_ref[...])
            dx_ref[...] = dx1_ref[...] + dx
            dg_ref[...] += dg

    row = pl.BlockSpec((tm, d), lambda i, j: (i, 0))
    return pl.pallas_call(
        body, name="inproj_bwd_x", grid=(t // tm, nb),
        in_specs=[pl.BlockSpec((tm, bn), lambda i, j: (i, j)), pl.BlockSpec((1, d, bn), lambda i, j: (j, 0, 0)),
                  row, row, _whole((1, d))],
        out_specs=[row, _acc_out((1, d))],
        out_shape=[jax.ShapeDtypeStruct((t, d), F32), jax.ShapeDtypeStruct((1, d), F32)],
        scratch_shapes=[pltpu.VMEM((tm, d), F32)],
        compiler_params=_params(2, 40),
    )(dproj, w_blk, dx1, x, g)


def _inproj_bwd_w(h, dproj):
    t, d = h.shape
    bn = dproj.shape[1] // N_DEV
    tm = TM_IN
    nsteps = t // tm

    def body(h_ref, dp_ref, dw_ref, acc):
        i = pl.program_id(1)

        @pl.when(i == 0)
        def _():
            acc[...] = jnp.zeros_like(acc)
        acc[...] += _dot_tn(h_ref[...], dp_ref[...])

        @pl.when(i == nsteps - 1)
        def _():
            dw_ref[0] = acc[...].astype(dw_ref.dtype)

    return pl.pallas_call(
        body, name="inproj_bwd_w", grid=(N_DEV, nsteps),
        in_specs=[pl.BlockSpec((tm, d), lambda j, i: (i, 0)), pl.BlockSpec((tm, bn), lambda j, i: (i, j))],
        out_specs=[pl.BlockSpec((1, d, bn), lambda j, i: (j, 0, 0))],
        out_shape=[jax.ShapeDtypeStruct((N_DEV, d, bn), WIRE_DTYPE)],
        scratch_shapes=[pltpu.VMEM((d, bn), F32)],
        compiler_params=_params(2, 40),
    )(h, dproj)[0]


def _adamw(parts, w, m, v):
    r, c = w.shape
    tr = r
    for cand in (512, 256, 128, 64, 32, 16, 8):
        if r % cand == 0 and cand * c * 4 <= (1 << 20):
            tr = cand
            break
    bc1 = 1.0 - ADAM_B1 ** ADAM_STEP
    bc2 = 1.0 - ADAM_B2 ** ADAM_STEP

    def body(p_ref, w_ref, m_ref, v_ref, g_ref, d_ref, nm_ref, nv_ref):
        g = p_ref[0].astype(F32)
        for s in range(1, N_DEV):
            g = g + p_ref[s].astype(F32)
        nm = ADAM_B1 * m_ref[...] + (1.0 - ADAM_B1) * g
        nv = ADAM_B2 * v_ref[...] + (1.0 - ADAM_B2) * (g * g)
        g_ref[...] = g
        nm_ref[...] = nm
        nv_ref[...] = nv
        d_ref[...] = -ADAM_LR * ((nm / bc1) / (jnp.sqrt(nv / bc2) + ADAM_EPS) + ADAM_WD * w_ref[...])

    row = pl.BlockSpec((tr, c), lambda i: (i, 0))
    return pl.pallas_call(
        body, name="adamw", grid=(r // tr,),
        in_specs=[pl.BlockSpec((N_DEV, tr, c), lambda i: (0, i, 0)), row, row, row],
        out_specs=[row] * 4,
        out_shape=[jax.ShapeDtypeStruct((r, c), F32)] * 4,
        compiler_params=_params(1, 40),
    )(parts, w, m, v)


def _pack(arrays):
    flat = jnp.concatenate([a.reshape(-1).astype(F32) for a in arrays])
    unit = 8 * PACK_COLS
    padded = -(-flat.shape[0] // unit) * unit
    return jnp.pad(flat, (0, padded - flat.shape[0])).reshape(-1, PACK_COLS)


def _unpack(packed, shapes):
    flat = packed.reshape(-1)
    out, off = [], 0
    for s in shapes:
        n = math.prod(s)
        out.append(flat[off:off + n].reshape(s))
        off += n
    return out


def kernel(x, p, norm_g, w_in, w_out, conv_w, conv_b, conv_ln_g, conv_ln_b, sgu_ln_g, sgu_ln_b, sgu_w, sgu_b, pl_norm_g, pl_gate_w, pl_proj_w, final_g, loss_target, m_norm_g, m_w_in, m_w_out, m_conv_w, m_conv_b, m_conv_ln_g, m_conv_ln_b, m_sgu_ln_g, m_sgu_ln_b, m_sgu_w, m_sgu_b, m_pl_norm_g, m_pl_gate_w, m_pl_proj_w, m_final_g, v_norm_g, v_w_in, v_w_out, v_conv_w, v_conv_b, v_conv_ln_g, v_conv_ln_b, v_sgu_ln_g, v_sgu_ln_b, v_sgu_w, v_sgu_b, v_pl_norm_g, v_pl_gate_w, v_pl_proj_w, v_final_g):
    bsz, seq, d = x.shape
    t = bsz * seq
    depth = w_in.shape[0]
    e = w_out.shape[1] * N_DEV
    pd = p.shape[-1]
    n_conv, n_sgu = conv_w.shape[0], sgu_ln_g.shape[0]
    eg = e // N_DEV

    small_shapes = [conv_w.shape, sgu_ln_g.shape, sgu_ln_b.shape]
    items = ([w_in[l].astype(MXU_DTYPE) for l in range(depth)]
             + [w_out[l].astype(MXU_DTYPE) for l in range(depth)]
             + [pl_gate_w[l].astype(MXU_DTYPE) for l in range(depth)]
             + [pl_proj_w.astype(MXU_DTYPE), _pack([conv_w, sgu_ln_g, sgu_ln_b])])
    gathered = _all_gather(items, "gather_weights")
    w_in_g = gathered[0:depth]
    w_out_g = [a.reshape(e, d) for a in gathered[depth:2 * depth]]
    gate_g = [a.reshape(d, d) for a in gathered[2 * depth:3 * depth]]
    proj_g = jnp.transpose(gathered[3 * depth], (1, 2, 0, 3)).reshape(depth, pd, d)
    small_g = [_unpack(gathered[3 * depth + 1][s], small_shapes) for s in range(N_DEV)]
    conv_w_g = jnp.concatenate([sg[0] for sg in small_g], axis=-1)
    sgu_ln_g_g = jnp.concatenate([sg[1] for sg in small_g], axis=-1)
    sgu_ln_b_g = jnp.concatenate([sg[2] for sg in small_g], axis=-1)
    sgu_wt = jnp.swapaxes(sgu_w, -1, -2)
    sgu_bt = jnp.swapaxes(sgu_b, -1, -2)

    xs = [x.reshape(t, d)]
    saved = []
    for l in range(depth):
        j = l // 2
        h, proj = _inproj(xs[-1], norm_g[l:l + 1], w_in_g[l])
        if l % 2 == 0:
            y1, y, q = _conv_fwd(proj, conv_w_g[j], conv_b[j:j + 1], conv_ln_g[j:j + 1], conv_ln_b[j:j + 1], seq)
        else:
            y1 = None
            y, q = _sgu_fwd(proj, sgu_ln_g_g[j:j + 1], sgu_ln_b_g[j:j + 1], sgu_w[j], sgu_bt[j])
        p_l = p[l].reshape(t, pd)
        x1, x2 = _outproj_fwd(q, xs[-1], p_l, w_out_g[l], pl_norm_g[l:l + 1], gate_g[l], proj_g[l])
        saved.append((h, proj, y1, y, x1, p_l))
        xs.append(x2)

    loss_part, dx, d_final_g = _loss_head(xs[-1], final_g.reshape(1, d), loss_target.reshape(t, d))
    loss = lax.psum(loss_part[0, 0], ("x", "y", "c"))

    d_norm_g, d_pl_norm_g = [None] * depth, [None] * depth
    dw_in_p, dw_out_p, dgate_p, dprojw_p = [None] * depth, [None] * depth, [None] * depth, [None] * depth
    d_conv_w, d_conv_b, d_conv_ln_g, d_conv_ln_b = [None] * n_conv, [None] * n_conv, [None] * n_conv, [None] * n_conv
    d_sgu_ln_g, d_sgu_ln_b, d_sgu_w, d_sgu_b = [None] * n_sgu, [None] * n_sgu, [None] * n_sgu, [None] * n_sgu
    for l in reversed(range(depth)):
        j = l // 2
        h, proj, y1, y, x1, p_l = saved[l]
        dx1, dgate_p[l], dprojw_p[l], d_pl_norm_g[l] = _ple_bwd(dx, x1, p_l, pl_norm_g[l:l + 1], gate_g[l], proj_g[l])
        dy, dz, dw_out_p[l] = _outproj_bwd(dx1, y, proj, w_out_g[l])
        if l % 2 == 0:
            dy1, d_conv_ln_g[j], d_conv_ln_b[j], d_conv_b[j] = _conv_ln_bwd(dy, y1, conv_ln_g[j:j + 1], conv_ln_b[j:j + 1])
            dproj, d_conv_w[j] = _conv_bwd(dy1, proj, dz, conv_w_g[j], seq)
        else:
            dproj, d_sgu_w[j], d_sgu_b[j], d_sgu_ln_g[j], d_sgu_ln_b[j] = _sgu_bwd(
                dy, proj, dz, sgu_ln_g_g[j:j + 1], sgu_ln_b_g[j:j + 1], sgu_w[j], sgu_wt[j], sgu_bt[j])
        dx, d_norm_g[l] = _inproj_bwd_x(dproj, w_in_g[l], dx1, xs[l], norm_g[l:l + 1])
        dw_in_p[l] = _inproj_bwd_w(h, dproj)
    grad_x = dx.reshape(bsz, seq, d)

    d_conv_w_full = jnp.stack(d_conv_w)
    d_sgu_ln_g_full = jnp.concatenate(d_sgu_ln_g, axis=0)
    d_sgu_ln_b_full = jnp.concatenate(d_sgu_ln_b, axis=0)
    small_part = jnp.stack([
        _pack([d_conv_w_full[..., s * eg:(s + 1) * eg], d_sgu_ln_g_full[:, s * eg:(s + 1) * eg],
               d_sgu_ln_b_full[:, s * eg:(s + 1) * eg]]) for s in range(N_DEV)])
    r_w_in, r_w_out, r_gate, r_projw, r_small = _all_to_all(
        [dw_in_p, [a.reshape(N_DEV, e // N_DEV, d) for a in dw_out_p],
         [a.reshape(N_DEV, d // N_DEV, d) for a in dgate_p], dprojw_p, [small_part]], "scatter_grads")

    rep_names_shapes = [norm_g.shape, conv_b.shape, conv_ln_g.shape, conv_ln_b.shape, sgu_w.shape, sgu_b.shape,
                        pl_norm_g.shape, final_g.shape]
    rep_part = _pack([jnp.concatenate(d_norm_g, axis=0), jnp.concatenate(d_conv_b, axis=0),
                      jnp.concatenate(d_conv_ln_g, axis=0), jnp.concatenate(d_conv_ln_b, axis=0),
                      jnp.stack(d_sgu_w), jnp.stack(d_sgu_b), jnp.concatenate(d_pl_norm_g, axis=0), d_final_g])
    r_rep = _all_gather([rep_part], "gather_replicated_grads")[0]

    def big(parts, w, m, v):
        shape = w.shape
        c = shape[-1]
        outs = _adamw(parts.reshape(N_DEV, -1, c), w.reshape(-1, c), m.reshape(-1, c), v.reshape(-1, c))
        return [o.reshape(shape) for o in outs]

    o_w_in = big(r_w_in, w_in, m_w_in, v_w_in)
    o_w_out = big(r_w_out, w_out, m_w_out, v_w_out)
    o_gate = big(r_gate, pl_gate_w, m_pl_gate_w, v_pl_gate_w)
    o_projw = big(r_projw, pl_proj_w, m_pl_proj_w, v_pl_proj_w)

    def packed(parts, ws, ms, vs):
        shapes = [a.shape for a in ws]
        outs = _adamw(parts.reshape(N_DEV, -1, PACK_COLS), _pack(ws), _pack(ms), _pack(vs))
        return [_unpack(o, shapes) for o in outs]

    o_small = packed(r_small, [conv_w, sgu_ln_g, sgu_ln_b], [m_conv_w, m_sgu_ln_g, m_sgu_ln_b],
                     [v_conv_w, v_sgu_ln_g, v_sgu_ln_b])
    o_rep = packed(r_rep, [norm_g, conv_b, conv_ln_g, conv_ln_b, sgu_w, sgu_b, pl_norm_g, final_g],
                   [m_norm_g, m_conv_b, m_conv_ln_g, m_conv_ln_b, m_sgu_w, m_sgu_b, m_pl_norm_g, m_final_g],
                   [v_norm_g, v_conv_b, v_conv_ln_g, v_conv_ln_b, v_sgu_w, v_sgu_b, v_pl_norm_g, v_final_g])

    def leaf(kind):
        rep, small = o_rep[kind], o_small[kind]
        return [rep[0], o_w_in[kind], o_w_out[kind], small[0], rep[1], rep[2], rep[3], small[1], small[2], rep[4],
                rep[5], rep[6], o_gate[kind], o_projw[kind], rep[7]]

    return (loss, grad_x, *leaf(0), *leaf(1), *leaf(2), *leaf(3))
```

```python
import math

import jax
import jax.numpy as jnp
from jax import lax
from jax.experimental import pallas as pl
from jax.experimental.pallas import tpu as pltpu

F32 = jnp.float32
MXU_DTYPE = jnp.bfloat16
WIRE_DTYPE = jnp.bfloat16

EPS = 1e-6
CONV_K = 31
CHUNK = 128
GROUPS = 8
HALO = 32
N_DEV = 8
DEPTH = 4

ADAM_LR = 0.001
ADAM_B1 = 0.9
ADAM_B2 = 0.999
ADAM_EPS = 1e-08
ADAM_WD = 0.01
ADAM_STEP = 10

TM_IN = 512
TM_MIX = 256
TM_OUT = 512
CONV_RC = 64
CONV_CC = 256
DCW_CC = 512
PACK_COLS = 1024

MESH_ID = pl.DeviceIdType.MESH
INV_SQRT2 = 1.0 / math.sqrt(2.0)
INV_SQRT_2PI = 1.0 / math.sqrt(2.0 * math.pi)


def _params(n_grid, vmem_mb):
    return pltpu.CompilerParams(dimension_semantics=("arbitrary",) * n_grid, vmem_limit_bytes=vmem_mb << 20)


def _whole(shape):
    nd = len(shape)
    return pl.BlockSpec(shape, lambda *_: (0,) * nd, pipeline_mode=pl.Buffered(1))


def _acc_out(shape):
    nd = len(shape)
    return pl.BlockSpec(shape, lambda *_: (0,) * nd)


def _dot(a, b):
    return jnp.dot(a.astype(MXU_DTYPE), b.astype(MXU_DTYPE), preferred_element_type=F32)


def _dot_nt(a, b):
    return lax.dot_general(a.astype(MXU_DTYPE), b.astype(MXU_DTYPE), (((1,), (1,)), ((), ())),
                           preferred_element_type=F32)


def _dot_tn(a, b):
    return lax.dot_general(a.astype(MXU_DTYPE), b.astype(MXU_DTYPE), (((0,), (0,)), ((), ())),
                           preferred_element_type=F32)


def _sigmoid(x):
    return jax.nn.sigmoid(x)


def _rms_rstd(x):
    return lax.rsqrt(jnp.mean(x * x, axis=-1, keepdims=True) + EPS)


def _rms_bwd(dy, x, rstd, g):
    gy = dy * g
    xr = x * rstd
    dx = rstd * (gy - xr * jnp.mean(gy * xr, axis=-1, keepdims=True))
    dg = jnp.sum(dy * xr, axis=0, keepdims=True)
    return dx, dg


def _ln_stats(x):
    mu = jnp.mean(x, axis=-1, keepdims=True)
    xc = x - mu
    var = jnp.mean(xc * xc, axis=-1, keepdims=True)
    rstd = lax.rsqrt(var + EPS)
    return xc * rstd, rstd


def _ln_bwd(dxhat, xhat, rstd):
    return rstd * (dxhat - jnp.mean(dxhat, axis=-1, keepdims=True)
                   - xhat * jnp.mean(dxhat * xhat, axis=-1, keepdims=True))


def _silu_grad(x, s):
    return s * (1.0 + x * (1.0 - s))


def _tril_mask():
    r = lax.broadcasted_iota(jnp.int32, (CHUNK, CHUNK), 0)
    c = lax.broadcasted_iota(jnp.int32, (CHUNK, CHUNK), 1)
    return r >= c


def _conv_apply(src_ref, w_ref, zs_ref, base, tm, e, flip, emit):
    for r0 in range(0, tm, CONV_RC):
        for c0 in range(0, e, CONV_CC):
            cols = slice(c0, c0 + CONV_CC)
            acc = None
            for s in range(8):
                nrows = CONV_RC if s == 0 else CONV_RC + 8
                z = None
                for k in range(CONV_K):
                    if (base + k) % 8 != s:
                        continue
                    wk = (CONV_K - 1 - k) if flip else k
                    term = w_ref[wk:wk + 1, cols] * src_ref[pl.ds(r0 + base + k - s, nrows), cols]
                    z = term if z is None else z + term
                if s == 0:
                    acc = z
                else:
                    zs_ref[s - 1, pl.ds(0, nrows), :] = z
                    acc = acc + zs_ref[s - 1, pl.ds(s, CONV_RC), :]
            emit(r0, c0, acc)


def _mesh_pos():
    return lax.axis_index("x"), lax.axis_index("y"), lax.axis_index("c")


def _slot(px, py, pc):
    return 4 * px + 2 * py + pc


def _all_gather(items, name):
    n = len(items)

    def body(*refs):
        in_refs, out_refs = refs[:n], refs[n:2 * n]
        send_sems, recv_sems, local_sems = refs[2 * n:]
        x, y, c = _mesh_pos()
        me, sibling = (x, y, c), (x, y, 1 - c)
        chips = [(1 - x, y), (x, 1 - y), (1 - x, 1 - y)]

        def copy(i, k, block, to, src=None):
            dst = out_refs[i].at[_slot(*block)]
            return pltpu.make_async_remote_copy(
                src_ref=dst if src is None else src, dst_ref=dst,
                send_sem=send_sems.at[i * 7 + k], recv_sem=recv_sems.at[i * 7 + k],
                device_id=to, device_id_type=MESH_ID)

        mine = [pltpu.make_async_copy(in_refs[i], out_refs[i].at[_slot(*me)], local_sems.at[i]) for i in range(n)]
        for cp in mine:
            cp.start()
        first = []
        for i in range(n):
            first.append(copy(i, 0, me, sibling, src=in_refs[i]))
            for j, chip in enumerate(chips):
                first.append(copy(i, 1 + j, me, (*chip, c), src=in_refs[i]))
        for cp in first:
            cp.start()
        passed = []
        for j, chip in enumerate(chips):
            for i in range(n):
                copy(i, 1 + j, (*chip, c), me).wait_recv()
                fwd = copy(i, 4 + j, (*chip, c), sibling)
                fwd.start()
                passed.append(fwd)
        for i in range(n):
            copy(i, 0, sibling, me).wait_recv()
            for j, chip in enumerate(chips):
                copy(i, 4 + j, (*chip, 1 - c), me).wait_recv()
        for cp in first + passed:
            cp.wait_send()
        for cp in mine:
            cp.wait()

    any_spec = pl.BlockSpec(memory_space=pl.ANY)
    return pl.pallas_call(
        body, name=name,
        out_shape=[jax.ShapeDtypeStruct((N_DEV,) + a.shape, a.dtype) for a in items],
        in_specs=[any_spec] * n, out_specs=[any_spec] * n,
        scratch_shapes=[pltpu.SemaphoreType.DMA((7 * n,)), pltpu.SemaphoreType.DMA((7 * n,)),
                        pltpu.SemaphoreType.DMA((n,))],
    )(*items)


def _all_to_all(groups, name):
    flat = [(gi, li, a) for gi, grp in enumerate(groups) for li, a in enumerate(grp)]
    n, ng = len(flat), len(groups)

    def body(*refs):
        in_refs, out_refs = refs[:n], refs[n:n + ng]
        send_sems, recv_sems, local_sems = refs[n + ng:]
        x, y, c = _mesh_pos()
        me = _slot(x, y, c)
        local, sends, recvs = [], [], []
        for i, (gi, li, _) in enumerate(flat):
            local.append(pltpu.make_async_copy(in_refs[i].at[me], out_refs[gi].at[me, li], local_sems.at[i]))
            for k in range(1, N_DEV):
                px = (1 - x) if (k & 4) else x
                py = (1 - y) if (k & 2) else y
                pc = (1 - c) if (k & 1) else c
                peer = _slot(px, py, pc)
                sem = i * 7 + k - 1
                sends.append(pltpu.make_async_remote_copy(
                    src_ref=in_refs[i].at[peer], dst_ref=out_refs[gi].at[me, li],
                    send_sem=send_sems.at[sem], recv_sem=recv_sems.at[sem],
                    device_id=(px, py, pc), device_id_type=MESH_ID))
                recvs.append(pltpu.make_async_remote_copy(
                    src_ref=in_refs[i].at[peer], dst_ref=out_refs[gi].at[peer, li],
                    send_sem=send_sems.at[sem], recv_sem=recv_sems.at[sem],
                    device_id=(px, py, pc), device_id_type=MESH_ID))
        for cp in local:
            cp.start()
        for cp in sends:
            cp.start()
        for cp in recvs:
            cp.wait_recv()
        for cp in sends:
            cp.wait_send()
        for cp in local:
            cp.wait()

    any_spec = pl.BlockSpec(memory_space=pl.ANY)
    return pl.pallas_call(
        body, name=name,
        out_shape=[jax.ShapeDtypeStruct((N_DEV, len(grp)) + grp[0].shape[1:], grp[0].dtype) for grp in groups],
        in_specs=[any_spec] * n, out_specs=[any_spec] * ng,
        scratch_shapes=[pltpu.SemaphoreType.DMA((7 * n,)), pltpu.SemaphoreType.DMA((7 * n,)),
                        pltpu.SemaphoreType.DMA((n,))],
    )(*[a for _, _, a in flat])


def _inproj(x, g, w_blk):
    t, d = x.shape
    nb, _, bn = w_blk.shape
    tm = TM_IN

    def body(x_ref, g_ref, w_ref, h_ref, proj_ref, hs):
        @pl.when(pl.program_id(1) == 0)
        def _():
            xv = x_ref[...]
            hv = (xv * _rms_rstd(xv) * g_ref[...]).astype(MXU_DTYPE)
            hs[...] = hv
            h_ref[...] = hv
        proj_ref[...] = jnp.dot(hs[...], w_ref[0], preferred_element_type=F32)

    return pl.pallas_call(
        body, name="inproj", grid=(t // tm, nb),
        in_specs=[pl.BlockSpec((tm, d), lambda i, j: (i, 0)), _whole((1, d)),
                  pl.BlockSpec((1, d, bn), lambda i, j: (j, 0, 0))],
        out_specs=[pl.BlockSpec((tm, d), lambda i, j: (i, 0)), pl.BlockSpec((tm, bn), lambda i, j: (i, j))],
        out_shape=[jax.ShapeDtypeStruct((t, d), MXU_DTYPE), jax.ShapeDtypeStruct((t, nb * bn), F32)],
        scratch_shapes=[pltpu.VMEM((tm, d), MXU_DTYPE)],
        compiler_params=_params(2, 40),
    )(x, g, w_blk)


def _conv_fwd(proj, cw, cb, lg, lb, seq):
    t = proj.shape[0]
    e = proj.shape[1] // 3
    tm = TM_MIX
    nt = seq // tm
    hb = tm // HALO

    def body(a_ref, b_ref, z_ref, ah_ref, bh_ref, cw_ref, cb_ref, lg_ref, lb_ref, y1_ref, y_ref, q_ref, y0s, zs):
        first = lax.rem(pl.program_id(0), nt) == 0
        y0s[pl.ds(HALO, tm), :] = a_ref[...] * _sigmoid(b_ref[...])

        @pl.when(first)
        def _():
            y0s[pl.ds(0, HALO), :] = jnp.zeros((HALO, e), F32)

        @pl.when(jnp.logical_not(first))
        def _():
            y0s[pl.ds(0, HALO), :] = ah_ref[...] * _sigmoid(bh_ref[...])

        def emit(r0, c0, acc):
            y1_ref[r0:r0 + CONV_RC, c0:c0 + CONV_CC] = acc + cb_ref[:, c0:c0 + CONV_CC]
        _conv_apply(y0s, cw_ref, zs, HALO - (CONV_K - 1), tm, e, False, emit)

        xhat, _ = _ln_stats(y1_ref[...])
        y2 = xhat * lg_ref[...] + lb_ref[...]
        y = y2 * _sigmoid(y2)
        y_ref[...] = y
        z = z_ref[...]
        q_ref[...] = (y * (z * _sigmoid(z))).astype(q_ref.dtype)

    tile = lambda col: pl.BlockSpec((tm, e), lambda i: (i, col))
    prev = lambda col: pl.BlockSpec((HALO, e), lambda i: (jnp.maximum(i * hb - 1, 0), col))
    return pl.pallas_call(
        body, name="conv_fwd", grid=(t // tm,),
        in_specs=[tile(0), tile(1), tile(2), prev(0), prev(1), _whole(cw.shape), _whole((1, e)), _whole((1, e)),
                  _whole((1, e))],
        out_specs=[tile(0), tile(0), tile(0)],
        out_shape=[jax.ShapeDtypeStruct((t, e), F32), jax.ShapeDtypeStruct((t, e), F32),
                   jax.ShapeDtypeStruct((t, e), MXU_DTYPE)],
        scratch_shapes=[pltpu.VMEM((tm + HALO, e), F32), pltpu.VMEM((7, CONV_RC + 8, CONV_CC), F32)],
        compiler_params=_params(1, 48),
    )(proj, proj, proj, proj, proj, cw, cb, lg, lb)


def _sgu_parts(a, b, lg, lb, sw_ref, sbt_ref, mixed_s, tm, e):
    eg = e // GROUPS
    ea = lax.erf(a * INV_SQRT2)
    eb = lax.erf(b * INV_SQRT2)
    u = 0.5 * a * (1.0 + ea)
    v0 = 0.5 * b * (1.0 + eb)
    xhat, rstd = _ln_stats(v0)
    v = (xhat * lg + lb).astype(MXU_DTYPE)
    mask = _tril_mask()
    for g in range(GROUPS):
        wt = jnp.where(mask, sw_ref[g], 0.0).astype(MXU_DTYPE)
        bcol = sbt_ref[:, g:g + 1]
        for ch in range(tm // CHUNK):
            rows = slice(ch * CHUNK, (ch + 1) * CHUNK)
            cols = slice(g * eg, (g + 1) * eg)
            mixed_s[rows, cols] = jnp.dot(wt, v[rows, cols], preferred_element_type=F32) + bcol
    return ea, eb, u, xhat, rstd, v


def _sgu_fwd(proj, lg, lb, sw, sbt):
    t = proj.shape[0]
    e = proj.shape[1] // 3
    tm = TM_MIX

    def body(a_ref, b_ref, z_ref, lg_ref, lb_ref, sw_ref, sbt_ref, y_ref, q_ref, mixed_s):
        _, _, u, _, _, _ = _sgu_parts(a_ref[...], b_ref[...], lg_ref[...], lb_ref[...], sw_ref, sbt_ref, mixed_s, tm, e)
        y = u * mixed_s[...]
        y_ref[...] = y
        z = z_ref[...]
        q_ref[...] = (y * (z * _sigmoid(z))).astype(q_ref.dtype)

    tile = lambda col: pl.BlockSpec((tm, e), lambda i: (i, col))
    return pl.pallas_call(
        body, name="sgu_fwd", grid=(t // tm,),
        in_specs=[tile(0), tile(1), tile(2), _whole((1, e)), _whole((1, e)), _whole(sw.shape), _whole(sbt.shape)],
        out_specs=[tile(0), tile(0)],
        out_shape=[jax.ShapeDtypeStruct((t, e), F32), jax.ShapeDtypeStruct((t, e), MXU_DTYPE)],
        scratch_shapes=[pltpu.VMEM((tm, e), F32)],
        compiler_params=_params(1, 48),
    )(proj, proj, proj, lg, lb, sw, sbt)


def _outproj_fwd(q, x, p_l, w_out, plg, gate_w, proj_w):
    t, d = x.shape
    e = q.shape[1]
    pd = p_l.shape[1]
    tm = TM_OUT

    def body(q_ref, x_ref, p_ref, wo_ref, plg_ref, gw_ref, pw_ref, x1_ref, x2_ref):
        x1 = x_ref[...] + jnp.dot(q_ref[...], wo_ref[...], preferred_element_type=F32)
        x1_ref[...] = x1
        rn = x1 * _rms_rstd(x1) * plg_ref[...]
        gate = _sigmoid(_dot(rn, gw_ref[...]))
        pp = _dot(p_ref[...], pw_ref[...])
        x2_ref[...] = x1 + gate * pp

    row = lambda w: pl.BlockSpec((tm, w), lambda i: (i, 0))
    return pl.pallas_call(
        body, name="outproj_fwd", grid=(t // tm,),
        in_specs=[row(e), row(d), row(pd), _whole((e, d)), _whole((1, d)), _whole((d, d)), _whole((pd, d))],
        out_specs=[row(d), row(d)],
        out_shape=[jax.ShapeDtypeStruct((t, d), F32), jax.ShapeDtypeStruct((t, d), F32)],
        compiler_params=_params(1, 48),
    )(q, x, p_l, w_out, plg, gate_w, proj_w)


def _loss_head(xf, fg, tgt):
    t, d = xf.shape
    tm = TM_OUT
    nsteps = t // tm

    def body(x_ref, g_ref, t_ref, loss_ref, dx_ref, dg_ref, sq_s):
        i = pl.program_id(0)

        @pl.when(i == 0)
        def _():
            sq_s[...] = jnp.zeros_like(sq_s)
            dg_ref[...] = jnp.zeros_like(dg_ref)
        x = x_ref[...]
        rstd = _rms_rstd(x)
        err = x * rstd * g_ref[...] - t_ref[...]
        sq_s[...] += jnp.sum(err * err, axis=0, keepdims=True)
        dx, dg = _rms_bwd(err * (1.0 / d), x, rstd, g_ref[...])
        dx_ref[...] = dx
        dg_ref[...] += dg

        @pl.when(i == nsteps - 1)
        def _():
            loss_ref[...] = jnp.sum(sq_s[...], axis=1, keepdims=True) * (0.5 / d)

    row = pl.BlockSpec((tm, d), lambda i: (i, 0))
    return pl.pallas_call(
        body, name="loss_head", grid=(nsteps,),
        in_specs=[row, _whole((1, d)), row],
        out_specs=[_acc_out((1, 1)), row, _acc_out((1, d))],
        out_shape=[jax.ShapeDtypeStruct((1, 1), F32), jax.ShapeDtypeStruct((t, d), F32),
                   jax.ShapeDtypeStruct((1, d), F32)],
        scratch_shapes=[pltpu.VMEM((1, d), F32)],
        compiler_params=_params(1, 32),
    )(xf, fg, tgt)


def _ple_bwd(dx2, x1, p_l, plg, gate_w, proj_w):
    t, d = x1.shape
    pd = p_l.shape[1]
    tm = TM_OUT
    nsteps = t // tm
    bn = d // N_DEV

    def body(dx2_ref, x1_ref, p_ref, plg_ref, gw_ref, pw_ref, dx1_ref, dgw_ref, dpw_ref, dplg_ref, gw_acc, pw_acc):
        i = pl.program_id(0)

        @pl.when(i == 0)
        def _():
            gw_acc[...] = jnp.zeros_like(gw_acc)
            pw_acc[...] = jnp.zeros_like(pw_acc)
            dplg_ref[...] = jnp.zeros_like(dplg_ref)
        dx2 = dx2_ref[...]
        x1 = x1_ref[...]
        plg = plg_ref[...]
        rstd = _rms_rstd(x1)
        rn = (x1 * rstd * plg).astype(MXU_DTYPE)
        gate = _sigmoid(jnp.dot(rn, gw_ref[...], preferred_element_type=F32))
        p_b = p_ref[...].astype(MXU_DTYPE)
        pp = jnp.dot(p_b, pw_ref[...], preferred_element_type=F32)
        dpp = (dx2 * gate).astype(MXU_DTYPE)
        dgpre = (dx2 * pp * gate * (1.0 - gate)).astype(MXU_DTYPE)
        pw_acc[...] += _dot_tn(p_b, dpp)
        gw_acc[...] += _dot_tn(rn, dgpre)
        drn = _dot_nt(dgpre, gw_ref[...])
        dx, dg = _rms_bwd(drn, x1, rstd, plg)
        dx1_ref[...] = dx2 + dx
        dplg_ref[...] += dg

        @pl.when(i == nsteps - 1)
        def _():
            dgw_ref[...] = gw_acc[...].astype(dgw_ref.dtype)
            for j in range(N_DEV):
                dpw_ref[j] = pw_acc[:, j * bn:(j + 1) * bn].astype(dpw_ref.dtype)

    row = lambda w: pl.BlockSpec((tm, w), lambda i: (i, 0))
    return pl.pallas_call(
        body, name="ple_bwd", grid=(nsteps,),
        in_specs=[row(d), row(d), row(pd), _whole((1, d)), _whole((d, d)), _whole((pd, d))],
        out_specs=[row(d), _acc_out((d, d)), _acc_out((N_DEV, pd, bn)), _acc_out((1, d))],
        out_shape=[jax.ShapeDtypeStruct((t, d), F32), jax.ShapeDtypeStruct((d, d), WIRE_DTYPE),
                   jax.ShapeDtypeStruct((N_DEV, pd, bn), WIRE_DTYPE), jax.ShapeDtypeStruct((1, d), F32)],
        scratch_shapes=[pltpu.VMEM((d, d), F32), pltpu.VMEM((pd, d), F32)],
        compiler_params=_params(1, 48),
    )(dx2, x1, p_l, plg, gate_w, proj_w)


def _outproj_bwd(dx1, y, proj, w_out):
    t, d = dx1.shape
    e = y.shape[1]
    tm = TM_MIX
    nsteps = t // tm

    def body(dx1_ref, y_ref, z_ref, wo_ref, dy_ref, dz_ref, dwo_ref, wo_acc):
        i = pl.program_id(0)

        @pl.when(i == 0)
        def _():
            wo_acc[...] = jnp.zeros_like(wo_acc)
        dx1 = dx1_ref[...].astype(MXU_DTYPE)
        y = y_ref[...]
        z = z_ref[...]
        s = _sigmoid(z)
        sz = z * s
        q = (y * sz).astype(MXU_DTYPE)
        wo_acc[...] += _dot_tn(q, dx1)
        dq = _dot_nt(dx1, wo_ref[...])
        dy_ref[...] = dq * sz
        dz_ref[...] = (dq * y * _silu_grad(z, s)).astype(dz_ref.dtype)

        @pl.when(i == nsteps - 1)
        def _():
            dwo_ref[...] = wo_acc[...].astype(dwo_ref.dtype)

    return pl.pallas_call(
        body, name="outproj_bwd", grid=(nsteps,),
        in_specs=[pl.BlockSpec((tm, d), lambda i: (i, 0)), pl.BlockSpec((tm, e), lambda i: (i, 0)),
                  pl.BlockSpec((tm, e), lambda i: (i, 2)), _whole((e, d))],
        out_specs=[pl.BlockSpec((tm, e), lambda i: (i, 0)), pl.BlockSpec((tm, e), lambda i: (i, 0)),
                   _acc_out((e, d))],
        out_shape=[jax.ShapeDtypeStruct((t, e), F32), jax.ShapeDtypeStruct((t, e), MXU_DTYPE),
                   jax.ShapeDtypeStruct((e, d), WIRE_DTYPE)],
        scratch_shapes=[pltpu.VMEM((e, d), F32)],
        compiler_params=_params(1, 48),
    )(dx1, y, proj, w_out)


def _conv_ln_bwd(dy, y1, lg, lb):
    t, e = dy.shape
    tm = TM_MIX

    def body(dy_ref, y1_ref, lg_ref, lb_ref, dy1_ref, dlg_ref, dlb_ref, dcb_ref):
        @pl.when(pl.program_id(0) == 0)
        def _():
            dlg_ref[...] = jnp.zeros_like(dlg_ref)
            dlb_ref[...] = jnp.zeros_like(dlb_ref)
            dcb_ref[...] = jnp.zeros_like(dcb_ref)
        xhat, rstd = _ln_stats(y1_ref[...])
        lg = lg_ref[...]
        y2 = xhat * lg + lb_ref[...]
        dy2 = dy_ref[...] * _silu_grad(y2, _sigmoid(y2))
        dlg_ref[...] += jnp.sum(dy2 * xhat, axis=0, keepdims=True)
        dlb_ref[...] += jnp.sum(dy2, axis=0, keepdims=True)
        dy1 = _ln_bwd(dy2 * lg, xhat, rstd)
        dy1_ref[...] = dy1
        dcb_ref[...] += jnp.sum(dy1, axis=0, keepdims=True)

    row = pl.BlockSpec((tm, e), lambda i: (i, 0))
    return pl.pallas_call(
        body, name="conv_ln_bwd", grid=(t // tm,),
        in_specs=[row, row, _whole((1, e)), _whole((1, e))],
        out_specs=[row, _acc_out((1, e)), _acc_out((1, e)), _acc_out((1, e))],
        out_shape=[jax.ShapeDtypeStruct((t, e), F32)] + [jax.ShapeDtypeStruct((1, e), F32)] * 3,
        compiler_params=_params(1, 48),
    )(dy, y1, lg, lb)


def _conv_bwd(dy1, proj, dz, cw, seq):
    t, e = dy1.shape
    tm = TM_MIX
    nt = seq // tm
    hb = tm // HALO
    n_halo_blocks = t // HALO

    def body(d_ref, dn_ref, a_ref, b_ref, ah_ref, bh_ref, dz_ref, cw_ref, dproj_ref, dcw_ref, y0s, d1s, zs, dsh, dcw8):
        i = pl.program_id(0)
        pos = lax.rem(i, nt)

        @pl.when(i == 0)
        def _():
            dcw8[...] = jnp.zeros_like(dcw8)
        a = a_ref[...]
        sb = _sigmoid(b_ref[...])
        y0s[pl.ds(HALO, tm), :] = a * sb
        d1s[pl.ds(0, tm), :] = d_ref[...]

        @pl.when(pos == 0)
        def _():
            y0s[pl.ds(0, HALO), :] = jnp.zeros((HALO, e), F32)

        @pl.when(pos != 0)
        def _():
            y0s[pl.ds(0, HALO), :] = ah_ref[...] * _sigmoid(bh_ref[...])

        @pl.when(pos == nt - 1)
        def _():
            d1s[pl.ds(tm, HALO), :] = jnp.zeros((HALO, e), F32)

        @pl.when(pos != nt - 1)
        def _():
            d1s[pl.ds(tm, HALO), :] = dn_ref[...]

        base = HALO - (CONV_K - 1)
        for c0 in range(0, e, DCW_CC):
            cols = slice(c0, c0 + DCW_CC)
            dcur = d_ref[:, cols]
            for s in range(1, 8):
                dsh[s - 1, pl.ds(0, 8), :] = jnp.zeros((8, DCW_CC), F32)
                dsh[s - 1, pl.ds(tm, 8), :] = jnp.zeros((8, DCW_CC), F32)
                dsh[s - 1, pl.ds(s, tm), :] = dcur
            for k in range(CONV_K):
                s = (base + k) % 8
                off = base + k - s
                if s == 0:
                    prod, n = dcur * y0s[pl.ds(off, tm), cols], tm
                else:
                    prod, n = dsh[s - 1] * y0s[pl.ds(off, tm + 8), cols], tm + 8
                dcw8[k, :, cols] += jnp.sum(prod.reshape(n // 8, 8, DCW_CC), axis=0)

        def emit(r0, c0, dy0):
            rs, cs = slice(r0, r0 + CONV_RC), slice(c0, c0 + CONV_CC)
            sbv = _sigmoid(b_ref[rs, cs])
            av = a_ref[rs, cs]
            dproj_ref[rs, c0:c0 + CONV_CC] = (dy0 * sbv).astype(dproj_ref.dtype)
            dproj_ref[rs, e + c0:e + c0 + CONV_CC] = (dy0 * av * sbv * (1.0 - sbv)).astype(dproj_ref.dtype)
        _conv_apply(d1s, cw_ref, zs, 0, tm, e, True, emit)
        dproj_ref[:, 2 * e:3 * e] = dz_ref[...]

        @pl.when(i == t // tm - 1)
        def _():
            dcw_ref[...] = jnp.sum(dcw8[...], axis=1)

    tile = lambda col: pl.BlockSpec((tm, e), lambda i: (i, col))
    prev = lambda col: pl.BlockSpec((HALO, e), lambda i: (jnp.maximum(i * hb - 1, 0), col))
    nxt = pl.BlockSpec((HALO, e), lambda i: (jnp.minimum((i + 1) * hb, n_halo_blocks - 1), 0))
    return pl.pallas_call(
        body, name="conv_bwd", grid=(t // tm,),
        in_specs=[tile(0), nxt, tile(0), tile(1), prev(0), prev(1), tile(0), _whole(cw.shape)],
        out_specs=[pl.BlockSpec((tm, 3 * e), lambda i: (i, 0)), _acc_out(cw.shape)],
        out_shape=[jax.ShapeDtypeStruct((t, 3 * e), MXU_DTYPE), jax.ShapeDtypeStruct(cw.shape, F32)],
        scratch_shapes=[pltpu.VMEM((tm + HALO, e), F32), pltpu.VMEM((tm + HALO, e), F32),
                        pltpu.VMEM((7, CONV_RC + 8, CONV_CC), F32), pltpu.VMEM((7, tm + 8, DCW_CC), F32),
                        pltpu.VMEM((CONV_K, 8, e), F32)],
        compiler_params=_params(1, 56),
    )(dy1, dy1, proj, proj, proj, proj, dz, cw)


def _sgu_bwd(dy, proj, dz, lg, lb, sw, swt, sbt):
    t, e = dy.shape
    eg = e // GROUPS
    tm = TM_MIX
    nsteps = t // tm

    def body(dy_ref, a_ref, b_ref, dz_ref, lg_ref, lb_ref, sw_ref, swt_ref, sbt_ref,
             dproj_ref, dsw_ref, dsb_ref, dlg_ref, dlb_ref, mixed_s, dv_s, sb_acc):
        i = pl.program_id(0)

        @pl.when(i == 0)
        def _():
            dsw_ref[...] = jnp.zeros_like(dsw_ref)
            sb_acc[...] = jnp.zeros_like(sb_acc)
            dlg_ref[...] = jnp.zeros_like(dlg_ref)
            dlb_ref[...] = jnp.zeros_like(dlb_ref)
        a = a_ref[...]
        b = b_ref[...]
        lg = lg_ref[...]
        ea, eb, u, xhat, rstd, v = _sgu_parts(a, b, lg, lb_ref[...], sw_ref, sbt_ref, mixed_s, tm, e)
        dy = dy_ref[...]
        du = dy * mixed_s[...]
        dmixed = (dy * u).astype(MXU_DTYPE)
        mask = _tril_mask()
        mask_t = (lax.broadcasted_iota(jnp.int32, (CHUNK, CHUNK), 0)
                  <= lax.broadcasted_iota(jnp.int32, (CHUNK, CHUNK), 1))
        ones = jnp.ones((8, eg), MXU_DTYPE)
        for g in range(GROUPS):
            wtt = jnp.where(mask_t, swt_ref[g], 0.0).astype(MXU_DTYPE)
            cols = slice(g * eg, (g + 1) * eg)
            for ch in range(tm // CHUNK):
                rows = slice(ch * CHUNK, (ch + 1) * CHUNK)
                dm = dmixed[rows, cols]
                dv_s[rows, cols] = jnp.dot(wtt, dm, preferred_element_type=F32)
                dsw_ref[g] += _dot_nt(dm, v[rows, cols])
                sb_acc[g] += _dot_nt(ones, dm)
        dv = dv_s[...]
        dlg_ref[...] += jnp.sum(dv * xhat, axis=0, keepdims=True)
        dlb_ref[...] += jnp.sum(dv, axis=0, keepdims=True)
        dv0 = _ln_bwd(dv * lg, xhat, rstd)
        pdf_a = jnp.exp(-0.5 * a * a) * INV_SQRT_2PI
        pdf_b = jnp.exp(-0.5 * b * b) * INV_SQRT_2PI
        dproj_ref[:, 0:e] = (du * (0.5 * (1.0 + ea) + a * pdf_a)).astype(dproj_ref.dtype)
        dproj_ref[:, e:2 * e] = (dv0 * (0.5 * (1.0 + eb) + b * pdf_b)).astype(dproj_ref.dtype)
        dproj_ref[:, 2 * e:3 * e] = dz_ref[...]

        @pl.when(i == nsteps - 1)
        def _():
            for g in range(GROUPS):
                dsw_ref[g] = jnp.where(mask, dsw_ref[g], 0.0)
                dsb_ref[g:g + 1, :] = sb_acc[g, 0:1, :]

    tile = lambda col: pl.BlockSpec((tm, e), lambda i: (i, col))
    return pl.pallas_call(
        body, name="sgu_bwd", grid=(nsteps,),
        in_specs=[tile(0), tile(0), tile(1), tile(0), _whole((1, e)), _whole((1, e)), _whole(sw.shape),
                  _whole(swt.shape), _whole(sbt.shape)],
        out_specs=[pl.BlockSpec((tm, 3 * e), lambda i: (i, 0)), _acc_out(sw.shape), _acc_out((GROUPS, CHUNK)),
                   _acc_out((1, e)), _acc_out((1, e))],
        out_shape=[jax.ShapeDtypeStruct((t, 3 * e), MXU_DTYPE), jax.ShapeDtypeStruct(sw.shape, F32),
                   jax.ShapeDtypeStruct((GROUPS, CHUNK), F32), jax.ShapeDtypeStruct((1, e), F32),
                   jax.ShapeDtypeStruct((1, e), F32)],
        scratch_shapes=[pltpu.VMEM((tm, e), F32), pltpu.VMEM((tm, e), F32), pltpu.VMEM((GROUPS, 8, CHUNK), F32)],
        compiler_params=_params(1, 56),
    )(dy, proj, proj, dz, lg, lb, sw, swt, sbt)


def _inproj_bwd_x(dproj, w_blk, dx1, x, g):
    t, d = x.shape
    nb, _, bn = w_blk.shape
    tm = TM_IN

    def body(dp_ref, w_ref, dx1_ref, x_ref, g_ref, dx_ref, dg_ref, dh_acc):
        i, j = pl.program_id(0), pl.program_id(1)

        @pl.when((i == 0) & (j == 0))
        def _():
            dg_ref[...] = jnp.zeros_like(dg_ref)

        @pl.when(j == 0)
        def _():
            dh_acc[...] = jnp.zeros_like(dh_acc)
        dh_acc[...] += _dot_nt(dp_ref[...], w_ref[0])

        @pl.when(j == nb - 1)
        def _():
            xv = x_ref[...]
            dx, dg = _rms_bwd(dh_acc[...], xv, _rms_rstd(xv), g_ref[...])
            dx_ref[...] = dx1_ref[...] + dx
            dg_ref[...] += dg

    row = pl.BlockSpec((tm, d), lambda i, j: (i, 0))
    return pl.pallas_call(
        body, name="inproj_bwd_x", grid=(t // tm, nb),
        in_specs=[pl.BlockSpec((tm, bn), lambda i, j: (i, j)), pl.BlockSpec((1, d, bn), lambda i, j: (j, 0, 0)),
                  row, row, _whole((1, d))],
        out_specs=[row, _acc_out((1, d))],
        out_shape=[jax.ShapeDtypeStruct((t, d), F32), jax.ShapeDtypeStruct((1, d), F32)],
        scratch_shapes=[pltpu.VMEM((tm, d), F32)],
        compiler_params=_params(2, 40),
    )(dproj, w_blk, dx1, x, g)


def _inproj_bwd_w(h, dproj):
    t, d = h.shape
    bn = dproj.shape[1] // N_DEV
    tm = TM_IN
    nsteps = t // tm

    def body(h_ref, dp_ref, dw_ref, acc):
        i = pl.program_id(1)

        @pl.when(i == 0)
        def _():
            acc[...] = jnp.zeros_like(acc)
        acc[...] += _dot_tn(h_ref[...], dp_ref[...])

        @pl.when(i == nsteps - 1)
        def _():
            dw_ref[0] = acc[...].astype(dw_ref.dtype)

    return pl.pallas_call(
        body, name="inproj_bwd_w", grid=(N_DEV, nsteps),
        in_specs=[pl.BlockSpec((tm, d), lambda j, i: (i, 0)), pl.BlockSpec((tm, bn), lambda j, i: (i, j))],
        out_specs=[pl.BlockSpec((1, d, bn), lambda j, i: (j, 0, 0))],
        out_shape=[jax.ShapeDtypeStruct((N_DEV, d, bn), WIRE_DTYPE)],
        scratch_shapes=[pltpu.VMEM((d, bn), F32)],
        compiler_params=_params(2, 40),
    )(h, dproj)[0]


def _adamw(parts, w, m, v):
    r, c = w.shape
    tr = r
    for cand in (512, 256, 128, 64, 32, 16, 8):
        if r % cand == 0 and cand * c * 4 <= (1 << 20):
            tr = cand
            break
    bc1 = 1.0 - ADAM_B1 ** ADAM_STEP
    bc2 = 1.0 - ADAM_B2 ** ADAM_STEP

    def body(p_ref, w_ref, m_ref, v_ref, g_ref, d_ref, nm_ref, nv_ref):
        g = p_ref[0].astype(F32)
        for s in range(1, N_DEV):
            g = g + p_ref[s].astype(F32)
        nm = ADAM_B1 * m_ref[...] + (1.0 - ADAM_B1) * g
        nv = ADAM_B2 * v_ref[...] + (1.0 - ADAM_B2) * (g * g)
        g_ref[...] = g
        nm_ref[...] = nm
        nv_ref[...] = nv
        d_ref[...] = -ADAM_LR * ((nm / bc1) / (jnp.sqrt(nv / bc2) + ADAM_EPS) + ADAM_WD * w_ref[...])

    row = pl.BlockSpec((tr, c), lambda i: (i, 0))
    return pl.pallas_call(
        body, name="adamw", grid=(r // tr,),
        in_specs=[pl.BlockSpec((N_DEV, tr, c), lambda i: (0, i, 0)), row, row, row],
        out_specs=[row] * 4,
        out_shape=[jax.ShapeDtypeStruct((r, c), F32)] * 4,
        compiler_params=_params(1, 40),
    )(parts, w, m, v)


def _pack(arrays):
    flat = jnp.concatenate([a.reshape(-1).astype(F32) for a in arrays])
    unit = 8 * PACK_COLS
    padded = -(-flat.shape[0] // unit) * unit
    return jnp.pad(flat, (0, padded - flat.shape[0])).reshape(-1, PACK_COLS)


def _unpack(packed, shapes):
    flat = packed.reshape(-1)
    out, off = [], 0
    for s in shapes:
        n = math.prod(s)
        out.append(flat[off:off + n].reshape(s))
        off += n
    return out


def kernel(x, p, norm_g, w_in, w_out, conv_w, conv_b, conv_ln_g, conv_ln_b, sgu_ln_g, sgu_ln_b, sgu_w, sgu_b, pl_norm_g, pl_gate_w, pl_proj_w, final_g, loss_target, m_norm_g, m_w_in, m_w_out, m_conv_w, m_conv_b, m_conv_ln_g, m_conv_ln_b, m_sgu_ln_g, m_sgu_ln_b, m_sgu_w, m_sgu_b, m_pl_norm_g, m_pl_gate_w, m_pl_proj_w, m_final_g, v_norm_g, v_w_in, v_w_out, v_conv_w, v_conv_b, v_conv_ln_g, v_conv_ln_b, v_sgu_ln_g, v_sgu_ln_b, v_sgu_w, v_sgu_b, v_pl_norm_g, v_pl_gate_w, v_pl_proj_w, v_final_g):
    bsz, seq, d = x.shape
    t = bsz * seq
    depth = w_in.shape[0]
    e = w_out.shape[1] * N_DEV
    pd = p.shape[-1]
    n_conv, n_sgu = conv_w.shape[0], sgu_ln_g.shape[0]
    eg = e // N_DEV

    small_shapes = [conv_w.shape, sgu_ln_g.shape, sgu_ln_b.shape]
    items = ([w_in[l].astype(MXU_DTYPE) for l in range(depth)]
             + [w_out[l].astype(MXU_DTYPE) for l in range(depth)]
             + [pl_gate_w[l].astype(MXU_DTYPE) for l in range(depth)]
             + [pl_proj_w.astype(MXU_DTYPE), _pack([conv_w, sgu_ln_g, sgu_ln_b])])
    gathered = _all_gather(items, "gather_weights")
    w_in_g = gathered[0:depth]
    w_out_g = [a.reshape(e, d) for a in gathered[depth:2 * depth]]
    gate_g = [a.reshape(d, d) for a in gathered[2 * depth:3 * depth]]
    proj_g = jnp.transpose(gathered[3 * depth], (1, 2, 0, 3)).reshape(depth, pd, d)
    small_g = [_unpack(gathered[3 * depth + 1][s], small_shapes) for s in range(N_DEV)]
    conv_w_g = jnp.concatenate([sg[0] for sg in small_g], axis=-1)
    sgu_ln_g_g = jnp.concatenate([sg[1] for sg in small_g], axis=-1)
    sgu_ln_b_g = jnp.concatenate([sg[2] for sg in small_g], axis=-1)
    sgu_wt = jnp.swapaxes(sgu_w, -1, -2)
    sgu_bt = jnp.swapaxes(sgu_b, -1, -2)

    xs = [x.reshape(t, d)]
    saved = []
    for l in range(depth):
        j = l // 2
        h, proj = _inproj(xs[-1], norm_g[l:l + 1], w_in_g[l])
        if l % 2 == 0:
            y1, y, q = _conv_fwd(proj, conv_w_g[j], conv_b[j:j + 1], conv_ln_g[j:j + 1], conv_ln_b[j:j + 1], seq)
        else:
            y1 = None
            y, q = _sgu_fwd(proj, sgu_ln_g_g[j:j + 1], sgu_ln_b_g[j:j + 1], sgu_w[j], sgu_bt[j])
        p_l = p[l].reshape(t, pd)
        x1, x2 = _outproj_fwd(q, xs[-1], p_l, w_out_g[l], pl_norm_g[l:l + 1], gate_g[l], proj_g[l])
        saved.append((h, proj, y1, y, x1, p_l))
        xs.append(x2)

    loss_part, dx, d_final_g = _loss_head(xs[-1], final_g.reshape(1, d), loss_target.reshape(t, d))
    loss = lax.psum(loss_part[0, 0], ("x", "y", "c"))

    d_norm_g, d_pl_norm_g = [None] * depth, [None] * depth
    dw_in_p, dw_out_p, dgate_p, dprojw_p = [None] * depth, [None] * depth, [None] * depth, [None] * depth
    d_conv_w, d_conv_b, d_conv_ln_g, d_conv_ln_b = [None] * n_conv, [None] * n_conv, [None] * n_conv, [None] * n_conv
    d_sgu_ln_g, d_sgu_ln_b, d_sgu_w, d_sgu_b = [None] * n_sgu, [None] * n_sgu, [None] * n_sgu, [None] * n_sgu
    for l in reversed(range(depth)):
        j = l // 2
        h, proj, y1, y, x1, p_l = saved[l]
        dx1, dgate_p[l], dprojw_p[l], d_pl_norm_g[l] = _ple_bwd(dx, x1, p_l, pl_norm_g[l:l + 1], gate_g[l], proj_g[l])
        dy, dz, dw_out_p[l] = _outproj_bwd(dx1, y, proj, w_out_g[l])
        if l % 2 == 0:
            dy1, d_conv_ln_g[j], d_conv_ln_b[j], d_conv_b[j] = _conv_ln_bwd(dy, y1, conv_ln_g[j:j + 1], conv_ln_b[j:j + 1])
            dproj, d_conv_w[j] = _conv_bwd(dy1, proj, dz, conv_w_g[j], seq)
        else:
            dproj, d_sgu_w[j], d_sgu_b[j], d_sgu_ln_g[j], d_sgu_ln_b[j] = _sgu_bwd(
                dy, proj, dz, sgu_ln_g_g[j:j + 1], sgu_ln_b_g[j:j + 1], sgu_w[j], sgu_wt[j], sgu_bt[j])
        dx, d_norm_g[l] = _inproj_bwd_x(dproj, w_in_g[l], dx1, xs[l], norm_g[l:l + 1])
        dw_in_p[l] = _inproj_bwd_w(h, dproj)
    grad_x = dx.reshape(bsz, seq, d)

    d_conv_w_full = jnp.stack(d_conv_w)
    d_sgu_ln_g_full = jnp.concatenate(d_sgu_ln_g, axis=0)
    d_sgu_ln_b_full = jnp.concatenate(d_sgu_ln_b, axis=0)
    small_part = jnp.stack([
        _pack([d_conv_w_full[..., s * eg:(s + 1) * eg], d_sgu_ln_g_full[:, s * eg:(s + 1) * eg],
               d_sgu_ln_b_full[:, s * eg:(s + 1) * eg]]) for s in range(N_DEV)])
    r_w_in, r_w_out, r_gate, r_projw, r_small = _all_to_all(
        [dw_in_p, [a.reshape(N_DEV, e // N_DEV, d) for a in dw_out_p],
         [a.reshape(N_DEV, d // N_DEV, d) for a in dgate_p], dprojw_p, [small_part]], "scatter_grads")

    rep_names_shapes = [norm_g.shape, conv_b.shape, conv_ln_g.shape, conv_ln_b.shape, sgu_w.shape, sgu_b.shape,
                        pl_norm_g.shape, final_g.shape]
    rep_part = _pack([jnp.concatenate(d_norm_g, axis=0), jnp.concatenate(d_conv_b, axis=0),
                      jnp.concatenate(d_conv_ln_g, axis=0), jnp.concatenate(d_conv_ln_b, axis=0),
                      jnp.stack(d_sgu_w), jnp.stack(d_sgu_b), jnp.concatenate(d_pl_norm_g, axis=0), d_final_g])
    r_rep = _all_gather([rep_part], "gather_replicated_grads")[0]

    def big(parts, w, m, v):
        shape = w.shape
        c = shape[-1]
        outs = _adamw(parts.reshape(N_DEV, -1, c), w.reshape(-1, c), m.reshape(-1, c), v.reshape(-1, c))
        return [o.reshape(shape) for o in outs]

    o_w_in = big(r_w_in, w_in, m_w_in, v_w_in)
    o_w_out = big(r_w_out, w_out, m_w_out, v_w_out)
    o_gate = big(r_gate, pl_gate_w, m_pl_gate_w, v_pl_gate_w)
    o_projw = big(r_projw, pl_proj_w, m_pl_proj_w, v_pl_proj_w)

    def packed(parts, ws, ms, vs):
        shapes = [a.shape for a in ws]
        outs = _adamw(parts.reshape(N_DEV, -1, PACK_COLS), _pack(ws), _pack(ms), _pack(vs))
        return [_unpack(o, shapes) for o in outs]

    o_small = packed(r_small, [conv_w, sgu_ln_g, sgu_ln_b], [m_conv_w, m_sgu_ln_g, m_sgu_ln_b],
                     [v_conv_w, v_sgu_ln_g, v_sgu_ln_b])
    o_rep = packed(r_rep, [norm_g, conv_b, conv_ln_g, conv_ln_b, sgu_w, sgu_b, pl_norm_g, final_g],
                   [m_norm_g, m_conv_b, m_conv_ln_g, m_conv_ln_b, m_sgu_w, m_sgu_b, m_pl_norm_g, m_final_g],
                   [v_norm_g, v_conv_b, v_conv_ln_g, v_conv_ln_b, v_sgu_w, v_sgu_b, v_pl_norm_g, v_final_g])

    def leaf(kind):
        rep, small = o_rep[kind], o_small[kind]
        return [rep[0], o_w_in[kind], o_w_out[kind], small[0], rep[1], rep[2], rep[3], small[1], small[2], rep[4],
                rep[5], rep[6], o_gate[kind], o_projw[kind], rep[7]]

    return (loss, grad_x, *leaf(0), *leaf(1), *leaf(2), *leaf(3))
```

```python
import math

import jax
import jax.numpy as jnp
from jax import lax
from jax.experimental import pallas as pl
from jax.experimental.pallas import tpu as pltpu

F32 = jnp.float32
MXU_DTYPE = jnp.bfloat16
WIRE_DTYPE = jnp.bfloat16

EPS = 1e-6
CONV_K = 31
CHUNK = 128
GROUPS = 8
HALO = 32
N_DEV = 8
DEPTH = 4

ADAM_LR = 0.001
ADAM_B1 = 0.9
ADAM_B2 = 0.999
ADAM_EPS = 1e-08
ADAM_WD = 0.01
ADAM_STEP = 10

TM_IN = 512
TM_MIX = 256
TM_OUT = 512
CONV_RC = 64
CONV_CC = 256
DCW_CC = 512
PACK_COLS = 1024

MESH_ID = pl.DeviceIdType.MESH
INV_SQRT2 = 1.0 / math.sqrt(2.0)
INV_SQRT_2PI = 1.0 / math.sqrt(2.0 * math.pi)


def _params(n_grid, vmem_mb):
    return pltpu.CompilerParams(dimension_semantics=("arbitrary",) * n_grid, vmem_limit_bytes=vmem_mb << 20)


def _whole(shape):
    nd = len(shape)
    return pl.BlockSpec(shape, lambda *_: (0,) * nd, pipeline_mode=pl.Buffered(1))


def _acc_out(shape):
    nd = len(shape)
    return pl.BlockSpec(shape, lambda *_: (0,) * nd)


def _dot(a, b):
    return jnp.dot(a.astype(MXU_DTYPE), b.astype(MXU_DTYPE), preferred_element_type=F32)


def _dot_nt(a, b):
    return lax.dot_general(a.astype(MXU_DTYPE), b.astype(MXU_DTYPE), (((1,), (1,)), ((), ())),
                           preferred_element_type=F32)


def _dot_tn(a, b):
    return lax.dot_general(a.astype(MXU_DTYPE), b.astype(MXU_DTYPE), (((0,), (0,)), ((), ())),
                           preferred_element_type=F32)


def _sigmoid(x):
    return jax.nn.sigmoid(x)


def _rms_rstd(x):
    return lax.rsqrt(jnp.mean(x * x, axis=-1, keepdims=True) + EPS)


def _rms_bwd(dy, x, rstd, g):
    gy = dy * g
    xr = x * rstd
    dx = rstd * (gy - xr * jnp.mean(gy * xr, axis=-1, keepdims=True))
    dg = jnp.sum(dy * xr, axis=0, keepdims=True)
    return dx, dg


def _ln_stats(x):
    mu = jnp.mean(x, axis=-1, keepdims=True)
    xc = x - mu
    var = jnp.mean(xc * xc, axis=-1, keepdims=True)
    rstd = lax.rsqrt(var + EPS)
    return xc * rstd, rstd


def _ln_bwd(dxhat, xhat, rstd):
    return rstd * (dxhat - jnp.mean(dxhat, axis=-1, keepdims=True)
                   - xhat * jnp.mean(dxhat * xhat, axis=-1, keepdims=True))


def _silu_grad(x, s):
    return s * (1.0 + x * (1.0 - s))


def _tril_mask():
    r = lax.broadcasted_iota(jnp.int32, (CHUNK, CHUNK), 0)
    c = lax.broadcasted_iota(jnp.int32, (CHUNK, CHUNK), 1)
    return r >= c


def _conv_apply(src_ref, w_ref, zs_ref, base, tm, e, flip, emit):
    for r0 in range(0, tm, CONV_RC):
        for c0 in range(0, e, CONV_CC):
            cols = slice(c0, c0 + CONV_CC)
            acc = None
            for s in range(8):
                nrows = CONV_RC if s == 0 else CONV_RC + 8
                z = None
                for k in range(CONV_K):
                    if (base + k) % 8 != s:
                        continue
                    wk = (CONV_K - 1 - k) if flip else k
                    term = w_ref[wk:wk + 1, cols] * src_ref[pl.ds(r0 + base + k - s, nrows), cols]
                    z = term if z is None else z + term
                if s == 0:
                    acc = z
                else:
                    zs_ref[s - 1, pl.ds(0, nrows), :] = z
                    acc = acc + zs_ref[s - 1, pl.ds(s, CONV_RC), :]
            emit(r0, c0, acc)


def _mesh_pos():
    return lax.axis_index("x"), lax.axis_index("y"), lax.axis_index("c")


def _slot(px, py, pc):
    return 4 * px + 2 * py + pc


def _peers(x, y, c):
    return [((1 - x) if (k & 4) else x, (1 - y) if (k & 2) else y, (1 - c) if (k & 1) else c)
            for k in range(1, N_DEV)]


HBM_SPEC = pl.BlockSpec(memory_space=pltpu.HBM)
SEM_SPEC = pl.BlockSpec(memory_space=pltpu.SEMAPHORE)
SIDE_EFFECT = pltpu.SideEffectType.DATAFLOW_SIDE_EFFECTING


def _exchange_copy(src_refs, land_refs, send_sems, recv_sems, i, k, peer, scatter, me):
    slot = _slot(*peer)
    return pltpu.make_async_remote_copy(
        src_ref=src_refs[i].at[slot] if scatter else src_refs[i],
        dst_ref=land_refs[i].at[me if me is not None else slot],
        send_sem=send_sems.at[i * 7 + k], recv_sem=recv_sems.at[i * 7 + k],
        device_id=peer, device_id_type=MESH_ID)


def _exchange_start(srcs, lands, scatter, name):
    n = len(srcs)

    def body(*refs):
        src_refs, land_refs = refs[:n], refs[n:2 * n]
        send_sems, recv_sems, token = refs[2 * n], refs[2 * n + 1], refs[-1]
        x, y, c = _mesh_pos()
        me = _slot(x, y, c)
        for i in range(n):
            for k, peer in enumerate(_peers(x, y, c)):
                _exchange_copy(src_refs, land_refs, send_sems, recv_sems, i, k, peer, scatter, me).start()
        token[...] = jnp.zeros_like(token)

    arrays = list(srcs) + list(lands)
    outs = pl.pallas_call(
        body, name=name,
        out_shape=(pltpu.SemaphoreType.DMA((7 * n,)), pltpu.SemaphoreType.DMA((7 * n,)),
                   *[pltpu.HBM(a.shape, a.dtype) for a in arrays], jax.ShapeDtypeStruct((8, 128), F32)),
        in_specs=[HBM_SPEC] * (2 * n),
        out_specs=(SEM_SPEC, SEM_SPEC, *[HBM_SPEC] * (2 * n), pl.BlockSpec(memory_space=pltpu.VMEM)),
        input_output_aliases={i: 2 + i for i in range(2 * n)},
        compiler_params=pltpu.CompilerParams(has_side_effects=SIDE_EFFECT),
    )(*[pltpu.with_memory_space_constraint(a, pltpu.HBM) for a in arrays])
    return outs[0], outs[1], list(outs[2:2 + n]), list(outs[2 + n:2 + 2 * n]), outs[-1]


def _exchange_wait(send_sems, recv_sems, srcs, lands, after, scatter, name):
    n = len(srcs)

    def body(*refs):
        src_refs, land_refs = refs[:n], refs[n:2 * n]
        send, recv = refs[2 * n], refs[2 * n + 1]
        x, y, c = _mesh_pos()
        for i in range(n):
            for k, peer in enumerate(_peers(x, y, c)):
                cp = _exchange_copy(src_refs, land_refs, send, recv, i, k, peer, scatter, None)
                cp.wait_send()
                cp.wait_recv()

    arrays = list(srcs) + list(lands)
    outs = pl.pallas_call(
        body, name=name,
        out_shape=tuple(pltpu.HBM(a.shape, a.dtype) for a in arrays),
        in_specs=[HBM_SPEC] * (2 * n) + [SEM_SPEC, SEM_SPEC, pl.BlockSpec(memory_space=pl.ANY)],
        out_specs=tuple([HBM_SPEC] * (2 * n)),
        input_output_aliases={i: i for i in range(2 * n)},
        compiler_params=pltpu.CompilerParams(has_side_effects=SIDE_EFFECT),
    )(*arrays, send_sems, recv_sems, after)
    return list(outs[n:])


def _place_own(parts, name):
    n = len(parts)

    def body(*refs):
        in_refs, out_refs, sems = refs[:n], refs[n:2 * n], refs[2 * n]
        me = _slot(*_mesh_pos())
        copies = [pltpu.make_async_copy(in_refs[i].at[me], out_refs[i].at[me], sems.at[i]) for i in range(n)]
        for cp in copies:
            cp.start()
        for cp in copies:
            cp.wait()

    any_spec = pl.BlockSpec(memory_space=pl.ANY)
    return pl.pallas_call(
        body, name=name,
        out_shape=[jax.ShapeDtypeStruct(a.shape, a.dtype) for a in parts],
        in_specs=[any_spec] * n, out_specs=[any_spec] * n,
        scratch_shapes=[pltpu.SemaphoreType.DMA((n,))],
    )(*parts)


def _all_gather(items, name, place=()):
    n = len(items)
    n_place = len(place)

    def body(*refs):
        in_refs, out_refs = refs[:n], refs[n + n_place:2 * n + n_place]
        place_refs, land_refs = refs[n:n + n_place], refs[2 * n + n_place:2 * (n + n_place)]
        send_sems, recv_sems, local_sems = refs[2 * (n + n_place):]
        x, y, c = _mesh_pos()
        me, sibling = (x, y, c), (x, y, 1 - c)
        chips = [(1 - x, y), (x, 1 - y), (1 - x, 1 - y)]

        def copy(i, k, block, to, src=None):
            dst = out_refs[i].at[_slot(*block)]
            return pltpu.make_async_remote_copy(
                src_ref=dst if src is None else src, dst_ref=dst,
                send_sem=send_sems.at[i * 7 + k], recv_sem=recv_sems.at[i * 7 + k],
                device_id=to, device_id_type=MESH_ID)

        mine = [pltpu.make_async_copy(in_refs[i], out_refs[i].at[_slot(*me)], local_sems.at[i]) for i in range(n)]
        mine += [pltpu.make_async_copy(place_refs[i], land_refs[i].at[_slot(*me)], local_sems.at[n + i])
                 for i in range(n_place)]
        for cp in mine:
            cp.start()
        first = []
        for i in range(n):
            first.append(copy(i, 0, me, sibling, src=in_refs[i]))
            for j, chip in enumerate(chips):
                first.append(copy(i, 1 + j, me, (*chip, c), src=in_refs[i]))
        for cp in first:
            cp.start()
        passed = []
        for j, chip in enumerate(chips):
            for i in range(n):
                copy(i, 1 + j, (*chip, c), me).wait_recv()
                fwd = copy(i, 4 + j, (*chip, c), sibling)
                fwd.start()
                passed.append(fwd)
        for i in range(n):
            copy(i, 0, sibling, me).wait_recv()
            for j, chip in enumerate(chips):
                copy(i, 4 + j, (*chip, 1 - c), me).wait_recv()
        for cp in first + passed:
            cp.wait_send()
        for cp in mine:
            cp.wait()

    any_spec = pl.BlockSpec(memory_space=pl.ANY)
    outs = pl.pallas_call(
        body, name=name,
        out_shape=[jax.ShapeDtypeStruct((N_DEV,) + a.shape, a.dtype) for a in list(items) + list(place)],
        in_specs=[any_spec] * (n + n_place), out_specs=[any_spec] * (n + n_place),
        scratch_shapes=[pltpu.SemaphoreType.DMA((7 * n,)), pltpu.SemaphoreType.DMA((7 * n,)),
                        pltpu.SemaphoreType.DMA((n + n_place,))],
    )(*items, *place)
    return (outs[:n], outs[n:]) if place else outs


def _all_to_all(groups, name):
    flat = [(gi, li, a) for gi, grp in enumerate(groups) for li, a in enumerate(grp)]
    n, ng = len(flat), len(groups)

    def body(*refs):
        in_refs, out_refs = refs[:n], refs[n:n + ng]
        send_sems, recv_sems, local_sems = refs[n + ng:]
        x, y, c = _mesh_pos()
        me = _slot(x, y, c)
        local, sends, recvs = [], [], []
        for i, (gi, li, _) in enumerate(flat):
            local.append(pltpu.make_async_copy(in_refs[i].at[me], out_refs[gi].at[me, li], local_sems.at[i]))
            for k in range(1, N_DEV):
                px = (1 - x) if (k & 4) else x
                py = (1 - y) if (k & 2) else y
                pc = (1 - c) if (k & 1) else c
                peer = _slot(px, py, pc)
                sem = i * 7 + k - 1
                sends.append(pltpu.make_async_remote_copy(
                    src_ref=in_refs[i].at[peer], dst_ref=out_refs[gi].at[me, li],
                    send_sem=send_sems.at[sem], recv_sem=recv_sems.at[sem],
                    device_id=(px, py, pc), device_id_type=MESH_ID))
                recvs.append(pltpu.make_async_remote_copy(
                    src_ref=in_refs[i].at[peer], dst_ref=out_refs[gi].at[peer, li],
                    send_sem=send_sems.at[sem], recv_sem=recv_sems.at[sem],
                    device_id=(px, py, pc), device_id_type=MESH_ID))
        for cp in local:
            cp.start()
        for cp in sends:
            cp.start()
        for cp in recvs:
            cp.wait_recv()
        for cp in sends:
            cp.wait_send()
        for cp in local:
            cp.wait()

    any_spec = pl.BlockSpec(memory_space=pl.ANY)
    return pl.pallas_call(
        body, name=name,
        out_shape=[jax.ShapeDtypeStruct((N_DEV, len(grp)) + grp[0].shape[1:], grp[0].dtype) for grp in groups],
        in_specs=[any_spec] * n, out_specs=[any_spec] * ng,
        scratch_shapes=[pltpu.SemaphoreType.DMA((7 * n,)), pltpu.SemaphoreType.DMA((7 * n,)),
                        pltpu.SemaphoreType.DMA((n,))],
    )(*[a for _, _, a in flat])


def _inproj(x, g, w_blk):
    t, d = x.shape
    nb, _, bn = w_blk.shape
    tm = TM_IN

    def body(x_ref, g_ref, w_ref, h_ref, proj_ref, hs):
        @pl.when(pl.program_id(1) == 0)
        def _():
            xv = x_ref[...]
            hv = (xv * _rms_rstd(xv) * g_ref[...]).astype(MXU_DTYPE)
            hs[...] = hv
            h_ref[...] = hv
        proj_ref[...] = jnp.dot(hs[...], w_ref[0], preferred_element_type=F32)

    return pl.pallas_call(
        body, name="inproj", grid=(t // tm, nb),
        in_specs=[pl.BlockSpec((tm, d), lambda i, j: (i, 0)), _whole((1, d)),
                  pl.BlockSpec((1, d, bn), lambda i, j: (j, 0, 0))],
        out_specs=[pl.BlockSpec((tm, d), lambda i, j: (i, 0)), pl.BlockSpec((tm, bn), lambda i, j: (i, j))],
        out_shape=[jax.ShapeDtypeStruct((t, d), MXU_DTYPE), jax.ShapeDtypeStruct((t, nb * bn), F32)],
        scratch_shapes=[pltpu.VMEM((tm, d), MXU_DTYPE)],
        compiler_params=_params(2, 40),
    )(x, g, w_blk)


def _conv_fwd(proj, cw, cb, lg, lb, seq):
    t = proj.shape[0]
    e = proj.shape[1] // 3
    tm = TM_MIX
    nt = seq // tm
    hb = tm // HALO

    def body(a_ref, b_ref, z_ref, ah_ref, bh_ref, cw_ref, cb_ref, lg_ref, lb_ref, y1_ref, y_ref, q_ref, y0s, zs):
        first = lax.rem(pl.program_id(0), nt) == 0
        y0s[pl.ds(HALO, tm), :] = a_ref[...] * _sigmoid(b_ref[...])

        @pl.when(first)
        def _():
            y0s[pl.ds(0, HALO), :] = jnp.zeros((HALO, e), F32)

        @pl.when(jnp.logical_not(first))
        def _():
            y0s[pl.ds(0, HALO), :] = ah_ref[...] * _sigmoid(bh_ref[...])

        def emit(r0, c0, acc):
            y1_ref[r0:r0 + CONV_RC, c0:c0 + CONV_CC] = acc + cb_ref[:, c0:c0 + CONV_CC]
        _conv_apply(y0s, cw_ref, zs, HALO - (CONV_K - 1), tm, e, False, emit)

        xhat, _ = _ln_stats(y1_ref[...])
        y2 = xhat * lg_ref[...] + lb_ref[...]
        y = y2 * _sigmoid(y2)
        y_ref[...] = y
        z = z_ref[...]
        q_ref[...] = (y * (z * _sigmoid(z))).astype(q_ref.dtype)

    tile = lambda col: pl.BlockSpec((tm, e), lambda i: (i, col))
    prev = lambda col: pl.BlockSpec((HALO, e), lambda i: (jnp.maximum(i * hb - 1, 0), col))
    return pl.pallas_call(
        body, name="conv_fwd", grid=(t // tm,),
        in_specs=[tile(0), tile(1), tile(2), prev(0), prev(1), _whole(cw.shape), _whole((1, e)), _whole((1, e)),
                  _whole((1, e))],
        out_specs=[tile(0), tile(0), tile(0)],
        out_shape=[jax.ShapeDtypeStruct((t, e), F32), jax.ShapeDtypeStruct((t, e), F32),
                   jax.ShapeDtypeStruct((t, e), MXU_DTYPE)],
        scratch_shapes=[pltpu.VMEM((tm + HALO, e), F32), pltpu.VMEM((7, CONV_RC + 8, CONV_CC), F32)],
        compiler_params=_params(1, 48),
    )(proj, proj, proj, proj, proj, cw, cb, lg, lb)


def _sgu_parts(a, b, lg, lb, sw_ref, sbt_ref, mixed_s, tm, e):
    eg = e // GROUPS
    ea = lax.erf(a * INV_SQRT2)
    eb = lax.erf(b * INV_SQRT2)
    u = 0.5 * a * (1.0 + ea)
    v0 = 0.5 * b * (1.0 + eb)
    xhat, rstd = _ln_stats(v0)
    v = (xhat * lg + lb).astype(MXU_DTYPE)
    mask = _tril_mask()
    for g in range(GROUPS):
        wt = jnp.where(mask, sw_ref[g], 0.0).astype(MXU_DTYPE)
        bcol = sbt_ref[:, g:g + 1]
        for ch in range(tm // CHUNK):
            rows = slice(ch * CHUNK, (ch + 1) * CHUNK)
            cols = slice(g * eg, (g + 1) * eg)
            mixed_s[rows, cols] = jnp.dot(wt, v[rows, cols], preferred_element_type=F32) + bcol
    return ea, eb, u, xhat, rstd, v


def _sgu_fwd(proj, lg, lb, sw, sbt):
    t = proj.shape[0]
    e = proj.shape[1] // 3
    tm = TM_MIX

    def body(a_ref, b_ref, z_ref, lg_ref, lb_ref, sw_ref, sbt_ref, y_ref, q_ref, mixed_s):
        _, _, u, _, _, _ = _sgu_parts(a_ref[...], b_ref[...], lg_ref[...], lb_ref[...], sw_ref, sbt_ref, mixed_s, tm, e)
        y = u * mixed_s[...]
        y_ref[...] = y
        z = z_ref[...]
        q_ref[...] = (y * (z * _sigmoid(z))).astype(q_ref.dtype)

    tile = lambda col: pl.BlockSpec((tm, e), lambda i: (i, col))
    return pl.pallas_call(
        body, name="sgu_fwd", grid=(t // tm,),
        in_specs=[tile(0), tile(1), tile(2), _whole((1, e)), _whole((1, e)), _whole(sw.shape), _whole(sbt.shape)],
        out_specs=[tile(0), tile(0)],
        out_shape=[jax.ShapeDtypeStruct((t, e), F32), jax.ShapeDtypeStruct((t, e), MXU_DTYPE)],
        scratch_shapes=[pltpu.VMEM((tm, e), F32)],
        compiler_params=_params(1, 48),
    )(proj, proj, proj, lg, lb, sw, sbt)


def _outproj_fwd(q, x, p_l, w_out, plg, gate_w, proj_w):
    t, d = x.shape
    e = q.shape[1]
    pd = p_l.shape[1]
    tm = TM_OUT

    def body(q_ref, x_ref, p_ref, wo_ref, plg_ref, gw_ref, pw_ref, x1_ref, x2_ref):
        x1 = x_ref[...] + jnp.dot(q_ref[...], wo_ref[...], preferred_element_type=F32)
        x1_ref[...] = x1
        rn = x1 * _rms_rstd(x1) * plg_ref[...]
        gate = _sigmoid(_dot(rn, gw_ref[...]))
        pp = _dot(p_ref[...], pw_ref[...])
        x2_ref[...] = x1 + gate * pp

    row = lambda w: pl.BlockSpec((tm, w), lambda i: (i, 0))
    return pl.pallas_call(
        body, name="outproj_fwd", grid=(t // tm,),
        in_specs=[row(e), row(d), row(pd), _whole((e, d)), _whole((1, d)), _whole((d, d)), _whole((pd, d))],
        out_specs=[row(d), row(d)],
        out_shape=[jax.ShapeDtypeStruct((t, d), F32), jax.ShapeDtypeStruct((t, d), F32)],
        compiler_params=_params(1, 48),
    )(q, x, p_l, w_out, plg, gate_w, proj_w)


def _loss_head(xf, fg, tgt):
    t, d = xf.shape
    tm = TM_OUT
    nsteps = t // tm

    def body(x_ref, g_ref, t_ref, loss_ref, dx_ref, dg_ref, sq_s):
        i = pl.program_id(0)

        @pl.when(i == 0)
        def _():
            sq_s[...] = jnp.zeros_like(sq_s)
            dg_ref[...] = jnp.zeros_like(dg_ref)
        x = x_ref[...]
        rstd = _rms_rstd(x)
        err = x * rstd * g_ref[...] - t_ref[...]
        sq_s[...] += jnp.sum(err * err, axis=0, keepdims=True)
        dx, dg = _rms_bwd(err * (1.0 / d), x, rstd, g_ref[...])
        dx_ref[...] = dx
        dg_ref[...] += dg

        @pl.when(i == nsteps - 1)
        def _():
            loss_ref[...] = jnp.sum(sq_s[...], axis=1, keepdims=True) * (0.5 / d)

    row = pl.BlockSpec((tm, d), lambda i: (i, 0))
    return pl.pallas_call(
        body, name="loss_head", grid=(nsteps,),
        in_specs=[row, _whole((1, d)), row],
        out_specs=[_acc_out((1, 1)), row, _acc_out((1, d))],
        out_shape=[jax.ShapeDtypeStruct((1, 1), F32), jax.ShapeDtypeStruct((t, d), F32),
                   jax.ShapeDtypeStruct((1, d), F32)],
        scratch_shapes=[pltpu.VMEM((1, d), F32)],
        compiler_params=_params(1, 32),
    )(xf, fg, tgt)


def _ple_bwd(dx2, x1, p_l, plg, gate_w, proj_w):
    t, d = x1.shape
    pd = p_l.shape[1]
    tm = TM_OUT
    nsteps = t // tm
    bn = d // N_DEV

    def body(dx2_ref, x1_ref, p_ref, plg_ref, gw_ref, pw_ref, dx1_ref, dgw_ref, dpw_ref, dplg_ref, gw_acc, pw_acc):
        i = pl.program_id(0)

        @pl.when(i == 0)
        def _():
            gw_acc[...] = jnp.zeros_like(gw_acc)
            pw_acc[...] = jnp.zeros_like(pw_acc)
            dplg_ref[...] = jnp.zeros_like(dplg_ref)
        dx2 = dx2_ref[...]
        x1 = x1_ref[...]
        plg = plg_ref[...]
        rstd = _rms_rstd(x1)
        rn = (x1 * rstd * plg).astype(MXU_DTYPE)
        gate = _sigmoid(jnp.dot(rn, gw_ref[...], preferred_element_type=F32))
        p_b = p_ref[...].astype(MXU_DTYPE)
        pp = jnp.dot(p_b, pw_ref[...], preferred_element_type=F32)
        dpp = (dx2 * gate).astype(MXU_DTYPE)
        dgpre = (dx2 * pp * gate * (1.0 - gate)).astype(MXU_DTYPE)
        pw_acc[...] += _dot_tn(p_b, dpp)
        gw_acc[...] += _dot_tn(rn, dgpre)
        drn = _dot_nt(dgpre, gw_ref[...])
        dx, dg = _rms_bwd(drn, x1, rstd, plg)
        dx1_ref[...] = dx2 + dx
        dplg_ref[...] += dg

        @pl.when(i == nsteps - 1)
        def _():
            dgw_ref[...] = gw_acc[...].astype(dgw_ref.dtype)
            for j in range(N_DEV):
                dpw_ref[j] = pw_acc[:, j * bn:(j + 1) * bn].astype(dpw_ref.dtype)

    row = lambda w: pl.BlockSpec((tm, w), lambda i: (i, 0))
    return pl.pallas_call(
        body, name="ple_bwd", grid=(nsteps,),
        in_specs=[row(d), row(d), row(pd), _whole((1, d)), _whole((d, d)), _whole((pd, d))],
        out_specs=[row(d), _acc_out((d, d)), _acc_out((N_DEV, pd, bn)), _acc_out((1, d))],
        out_shape=[jax.ShapeDtypeStruct((t, d), F32), jax.ShapeDtypeStruct((d, d), WIRE_DTYPE),
                   jax.ShapeDtypeStruct((N_DEV, pd, bn), WIRE_DTYPE), jax.ShapeDtypeStruct((1, d), F32)],
        scratch_shapes=[pltpu.VMEM((d, d), F32), pltpu.VMEM((pd, d), F32)],
        compiler_params=_params(1, 48),
    )(dx2, x1, p_l, plg, gate_w, proj_w)


def _outproj_bwd(dx1, y, proj, w_out):
    t, d = dx1.shape
    e = y.shape[1]
    tm = TM_MIX
    nsteps = t // tm

    def body(dx1_ref, y_ref, z_ref, wo_ref, dy_ref, dz_ref, dwo_ref, wo_acc):
        i = pl.program_id(0)

        @pl.when(i == 0)
        def _():
            wo_acc[...] = jnp.zeros_like(wo_acc)
        dx1 = dx1_ref[...].astype(MXU_DTYPE)
        y = y_ref[...]
        z = z_ref[...]
        s = _sigmoid(z)
        sz = z * s
        q = (y * sz).astype(MXU_DTYPE)
        wo_acc[...] += _dot_tn(q, dx1)
        dq = _dot_nt(dx1, wo_ref[...])
        dy_ref[...] = dq * sz
        dz_ref[...] = (dq * y * _silu_grad(z, s)).astype(dz_ref.dtype)

        @pl.when(i == nsteps - 1)
        def _():
            dwo_ref[...] = wo_acc[...].astype(dwo_ref.dtype)

    return pl.pallas_call(
        body, name="outproj_bwd", grid=(nsteps,),
        in_specs=[pl.BlockSpec((tm, d), lambda i: (i, 0)), pl.BlockSpec((tm, e), lambda i: (i, 0)),
                  pl.BlockSpec((tm, e), lambda i: (i, 2)), _whole((e, d))],
        out_specs=[pl.BlockSpec((tm, e), lambda i: (i, 0)), pl.BlockSpec((tm, e), lambda i: (i, 0)),
                   _acc_out((e, d))],
        out_shape=[jax.ShapeDtypeStruct((t, e), F32), jax.ShapeDtypeStruct((t, e), MXU_DTYPE),
                   jax.ShapeDtypeStruct((e, d), WIRE_DTYPE)],
        scratch_shapes=[pltpu.VMEM((e, d), F32)],
        compiler_params=_params(1, 48),
    )(dx1, y, proj, w_out)


def _conv_ln_bwd(dy, y1, lg, lb):
    t, e = dy.shape
    tm = TM_MIX

    def body(dy_ref, y1_ref, lg_ref, lb_ref, dy1_ref, dlg_ref, dlb_ref, dcb_ref):
        @pl.when(pl.program_id(0) == 0)
        def _():
            dlg_ref[...] = jnp.zeros_like(dlg_ref)
            dlb_ref[...] = jnp.zeros_like(dlb_ref)
            dcb_ref[...] = jnp.zeros_like(dcb_ref)
        xhat, rstd = _ln_stats(y1_ref[...])
        lg = lg_ref[...]
        y2 = xhat * lg + lb_ref[...]
        dy2 = dy_ref[...] * _silu_grad(y2, _sigmoid(y2))
        dlg_ref[...] += jnp.sum(dy2 * xhat, axis=0, keepdims=True)
        dlb_ref[...] += jnp.sum(dy2, axis=0, keepdims=True)
        dy1 = _ln_bwd(dy2 * lg, xhat, rstd)
        dy1_ref[...] = dy1
        dcb_ref[...] += jnp.sum(dy1, axis=0, keepdims=True)

    row = pl.BlockSpec((tm, e), lambda i: (i, 0))
    return pl.pallas_call(
        body, name="conv_ln_bwd", grid=(t // tm,),
        in_specs=[row, row, _whole((1, e)), _whole((1, e))],
        out_specs=[row, _acc_out((1, e)), _acc_out((1, e)), _acc_out((1, e))],
        out_shape=[jax.ShapeDtypeStruct((t, e), F32)] + [jax.ShapeDtypeStruct((1, e), F32)] * 3,
        compiler_params=_params(1, 48),
    )(dy, y1, lg, lb)


def _conv_bwd(dy1, proj, dz, cw, seq):
    t, e = dy1.shape
    tm = TM_MIX
    nt = seq // tm
    hb = tm // HALO
    n_halo_blocks = t // HALO

    def body(d_ref, dn_ref, a_ref, b_ref, ah_ref, bh_ref, dz_ref, cw_ref, dproj_ref, dcw_ref, y0s, d1s, zs, dsh, dcw8):
        i = pl.program_id(0)
        pos = lax.rem(i, nt)

        @pl.when(i == 0)
        def _():
            dcw8[...] = jnp.zeros_like(dcw8)
        a = a_ref[...]
        sb = _sigmoid(b_ref[...])
        y0s[pl.ds(HALO, tm), :] = a * sb
        d1s[pl.ds(0, tm), :] = d_ref[...]

        @pl.when(pos == 0)
        def _():
            y0s[pl.ds(0, HALO), :] = jnp.zeros((HALO, e), F32)

        @pl.when(pos != 0)
        def _():
            y0s[pl.ds(0, HALO), :] = ah_ref[...] * _sigmoid(bh_ref[...])

        @pl.when(pos == nt - 1)
        def _():
            d1s[pl.ds(tm, HALO), :] = jnp.zeros((HALO, e), F32)

        @pl.when(pos != nt - 1)
        def _():
            d1s[pl.ds(tm, HALO), :] = dn_ref[...]

        base = HALO - (CONV_K - 1)
        for c0 in range(0, e, DCW_CC):
            cols = slice(c0, c0 + DCW_CC)
            dcur = d_ref[:, cols]
            for s in range(1, 8):
                dsh[s - 1, pl.ds(0, 8), :] = jnp.zeros((8, DCW_CC), F32)
                dsh[s - 1, pl.ds(tm, 8), :] = jnp.zeros((8, DCW_CC), F32)
                dsh[s - 1, pl.ds(s, tm), :] = dcur
            for k in range(CONV_K):
                s = (base + k) % 8
                off = base + k - s
                if s == 0:
                    prod, n = dcur * y0s[pl.ds(off, tm), cols], tm
                else:
                    prod, n = dsh[s - 1] * y0s[pl.ds(off, tm + 8), cols], tm + 8
                dcw8[k, :, cols] += jnp.sum(prod.reshape(n // 8, 8, DCW_CC), axis=0)

        def emit(r0, c0, dy0):
            rs, cs = slice(r0, r0 + CONV_RC), slice(c0, c0 + CONV_CC)
            sbv = _sigmoid(b_ref[rs, cs])
            av = a_ref[rs, cs]
            dproj_ref[rs, c0:c0 + CONV_CC] = (dy0 * sbv).astype(dproj_ref.dtype)
            dproj_ref[rs, e + c0:e + c0 + CONV_CC] = (dy0 * av * sbv * (1.0 - sbv)).astype(dproj_ref.dtype)
        _conv_apply(d1s, cw_ref, zs, 0, tm, e, True, emit)
        dproj_ref[:, 2 * e:3 * e] = dz_ref[...]

        @pl.when(i == t // tm - 1)
        def _():
            dcw_ref[...] = jnp.sum(dcw8[...], axis=1)

    tile = lambda col: pl.BlockSpec((tm, e), lambda i: (i, col))
    prev = lambda col: pl.BlockSpec((HALO, e), lambda i: (jnp.maximum(i * hb - 1, 0), col))
    nxt = pl.BlockSpec((HALO, e), lambda i: (jnp.minimum((i + 1) * hb, n_halo_blocks - 1), 0))
    return pl.pallas_call(
        body, name="conv_bwd", grid=(t // tm,),
        in_specs=[tile(0), nxt, tile(0), tile(1), prev(0), prev(1), tile(0), _whole(cw.shape)],
        out_specs=[pl.BlockSpec((tm, 3 * e), lambda i: (i, 0)), _acc_out(cw.shape)],
        out_shape=[jax.ShapeDtypeStruct((t, 3 * e), MXU_DTYPE), jax.ShapeDtypeStruct(cw.shape, F32)],
        scratch_shapes=[pltpu.VMEM((tm + HALO, e), F32), pltpu.VMEM((tm + HALO, e), F32),
                        pltpu.VMEM((7, CONV_RC + 8, CONV_CC), F32), pltpu.VMEM((7, tm + 8, DCW_CC), F32),
                        pltpu.VMEM((CONV_K, 8, e), F32)],
        compiler_params=_params(1, 56),
    )(dy1, dy1, proj, proj, proj, proj, dz, cw)


def _sgu_bwd(dy, proj, dz, lg, lb, sw, swt, sbt):
    t, e = dy.shape
    eg = e // GROUPS
    tm = TM_MIX
    nsteps = t // tm

    def body(dy_ref, a_ref, b_ref, dz_ref, lg_ref, lb_ref, sw_ref, swt_ref, sbt_ref,
             dproj_ref, dsw_ref, dsb_ref, dlg_ref, dlb_ref, mixed_s, dv_s, sb_acc):
        i = pl.program_id(0)

        @pl.when(i == 0)
        def _():
            dsw_ref[...] = jnp.zeros_like(dsw_ref)
            sb_acc[...] = jnp.zeros_like(sb_acc)
            dlg_ref[...] = jnp.zeros_like(dlg_ref)
            dlb_ref[...] = jnp.zeros_like(dlb_ref)
        a = a_ref[...]
        b = b_ref[...]
        lg = lg_ref[...]
        ea, eb, u, xhat, rstd, v = _sgu_parts(a, b, lg, lb_ref[...], sw_ref, sbt_ref, mixed_s, tm, e)
        dy = dy_ref[...]
        du = dy * mixed_s[...]
        dmixed = (dy * u).astype(MXU_DTYPE)
        mask = _tril_mask()
        mask_t = (lax.broadcasted_iota(jnp.int32, (CHUNK, CHUNK), 0)
                  <= lax.broadcasted_iota(jnp.int32, (CHUNK, CHUNK), 1))
        ones = jnp.ones((8, eg), MXU_DTYPE)
        for g in range(GROUPS):
            wtt = jnp.where(mask_t, swt_ref[g], 0.0).astype(MXU_DTYPE)
            cols = slice(g * eg, (g + 1) * eg)
            for ch in range(tm // CHUNK):
                rows = slice(ch * CHUNK, (ch + 1) * CHUNK)
                dm = dmixed[rows, cols]
                dv_s[rows, cols] = jnp.dot(wtt, dm, preferred_element_type=F32)
                dsw_ref[g] += _dot_nt(dm, v[rows, cols])
                sb_acc[g] += _dot_nt(ones, dm)
        dv = dv_s[...]
        dlg_ref[...] += jnp.sum(dv * xhat, axis=0, keepdims=True)
        dlb_ref[...] += jnp.sum(dv, axis=0, keepdims=True)
        dv0 = _ln_bwd(dv * lg, xhat, rstd)
        pdf_a = jnp.exp(-0.5 * a * a) * INV_SQRT_2PI
        pdf_b = jnp.exp(-0.5 * b * b) * INV_SQRT_2PI
        dproj_ref[:, 0:e] = (du * (0.5 * (1.0 + ea) + a * pdf_a)).astype(dproj_ref.dtype)
        dproj_ref[:, e:2 * e] = (dv0 * (0.5 * (1.0 + eb) + b * pdf_b)).astype(dproj_ref.dtype)
        dproj_ref[:, 2 * e:3 * e] = dz_ref[...]

        @pl.when(i == nsteps - 1)
        def _():
            for g in range(GROUPS):
                dsw_ref[g] = jnp.where(mask, dsw_ref[g], 0.0)
                dsb_ref[g:g + 1, :] = sb_acc[g, 0:1, :]

    tile = lambda col: pl.BlockSpec((tm, e), lambda i: (i, col))
    return pl.pallas_call(
        body, name="sgu_bwd", grid=(nsteps,),
        in_specs=[tile(0), tile(0), tile(1), tile(0), _whole((1, e)), _whole((1, e)), _whole(sw.shape),
                  _whole(swt.shape), _whole(sbt.shape)],
        out_specs=[pl.BlockSpec((tm, 3 * e), lambda i: (i, 0)), _acc_out(sw.shape), _acc_out((GROUPS, CHUNK)),
                   _acc_out((1, e)), _acc_out((1, e))],
        out_shape=[jax.ShapeDtypeStruct((t, 3 * e), MXU_DTYPE), jax.ShapeDtypeStruct(sw.shape, F32),
                   jax.ShapeDtypeStruct((GROUPS, CHUNK), F32), jax.ShapeDtypeStruct((1, e), F32),
                   jax.ShapeDtypeStruct((1, e), F32)],
        scratch_shapes=[pltpu.VMEM((tm, e), F32), pltpu.VMEM((tm, e), F32), pltpu.VMEM((GROUPS, 8, CHUNK), F32)],
        compiler_params=_params(1, 56),
    )(dy, proj, proj, dz, lg, lb, sw, swt, sbt)


def _inproj_bwd_x(dproj, w_blk, dx1, x, g):
    t, d = x.shape
    nb, _, bn = w_blk.shape
    tm = TM_IN

    def body(dp_ref, w_ref, dx1_ref, x_ref, g_ref, dx_ref, dg_ref, dh_acc):
        i, j = pl.program_id(0), pl.program_id(1)

        @pl.when((i == 0) & (j == 0))
        def _():
            dg_ref[...] = jnp.zeros_like(dg_ref)

        @pl.when(j == 0)
        def _():
            dh_acc[...] = jnp.zeros_like(dh_acc)
        dh_acc[...] += _dot_nt(dp_ref[...], w_ref[0])

        @pl.when(j == nb - 1)
        def _():
            xv = x_ref[...]
            dx, dg = _rms_bwd(dh_acc[...], xv, _rms_rstd(xv), g_ref[...])
            dx_ref[...] = dx1_ref[...] + dx
            dg_ref[...] += dg

    row = pl.BlockSpec((tm, d), lambda i, j: (i, 0))
    return pl.pallas_call(
        body, name="inproj_bwd_x", grid=(t // tm, nb),
        in_specs=[pl.BlockSpec((tm, bn), lambda i, j: (i, j)), pl.BlockSpec((1, d, bn), lambda i, j: (j, 0, 0)),
                  row, row, _whole((1, d))],
        out_specs=[row, _acc_out((1, d))],
        out_shape=[jax.ShapeDtypeStruct((t, d), F32), jax.ShapeDtypeStruct((1, d), F32)],
        scratch_shapes=[pltpu.VMEM((tm, d), F32)],
        compiler_params=_params(2, 40),
    )(dproj, w_blk, dx1, x, g)


def _inproj_bwd_w(h, dproj):
    t, d = h.shape
    bn = dproj.shape[1] // N_DEV
    tm = TM_IN
    nsteps = t // tm

    def body(h_ref, dp_ref, dw_ref, acc):
        i = pl.program_id(1)

        @pl.when(i == 0)
        def _():
            acc[...] = jnp.zeros_like(acc)
        acc[...] += _dot_tn(h_ref[...], dp_ref[...])

        @pl.when(i == nsteps - 1)
        def _():
            dw_ref[0] = acc[...].astype(dw_ref.dtype)

    return pl.pallas_call(
        body, name="inproj_bwd_w", grid=(N_DEV, nsteps),
        in_specs=[pl.BlockSpec((tm, d), lambda j, i: (i, 0)), pl.BlockSpec((tm, bn), lambda j, i: (i, j))],
        out_specs=[pl.BlockSpec((1, d, bn), lambda j, i: (j, 0, 0))],
        out_shape=[jax.ShapeDtypeStruct((N_DEV, d, bn), WIRE_DTYPE)],
        scratch_shapes=[pltpu.VMEM((d, bn), F32)],
        compiler_params=_params(2, 40),
    )(h, dproj)[0]


def _adamw(parts, w, m, v):
    nl, r, c = w.shape
    tr = r
    for cand in (512, 256, 128, 64, 32, 16, 8):
        if r % cand == 0 and cand * c * 4 <= (1 << 19):
            tr = cand
            break
    bc1 = 1.0 - ADAM_B1 ** ADAM_STEP
    bc2 = 1.0 - ADAM_B2 ** ADAM_STEP

    def body(*refs):
        p_refs = refs[:nl]
        w_ref, m_ref, v_ref, g_ref, d_ref, nm_ref, nv_ref = refs[nl:]

        def update(p_ref):
            g = p_ref[0].astype(F32)
            for s in range(1, N_DEV):
                g = g + p_ref[s].astype(F32)
            nm = ADAM_B1 * m_ref[0] + (1.0 - ADAM_B1) * g
            nv = ADAM_B2 * v_ref[0] + (1.0 - ADAM_B2) * (g * g)
            g_ref[0] = g
            nm_ref[0] = nm
            nv_ref[0] = nv
            d_ref[0] = -ADAM_LR * ((nm / bc1) / (jnp.sqrt(nv / bc2) + ADAM_EPS) + ADAM_WD * w_ref[0])

        if nl == 1:
            update(p_refs[0])
        else:
            for kk in range(nl):
                pl.when(pl.program_id(0) == kk)(lambda kk=kk: update(p_refs[kk]))

    part_spec = lambda kk: pl.BlockSpec((N_DEV, tr, c), lambda l, i: (0, jnp.where(l == kk, i, 0), 0))
    row = pl.BlockSpec((1, tr, c), lambda l, i: (l, i, 0))
    return pl.pallas_call(
        body, name="adamw", grid=(nl, r // tr),
        in_specs=[part_spec(kk) for kk in range(nl)] + [row, row, row],
        out_specs=[row] * 4,
        out_shape=[jax.ShapeDtypeStruct((nl, r, c), F32)] * 4,
        compiler_params=_params(2, 48),
    )(*parts, w, m, v)


def _pack(arrays):
    flat = jnp.concatenate([a.reshape(-1).astype(F32) for a in arrays])
    unit = 8 * PACK_COLS
    padded = -(-flat.shape[0] // unit) * unit
    return jnp.pad(flat, (0, padded - flat.shape[0])).reshape(-1, PACK_COLS)


def _unpack(packed, shapes):
    flat = packed.reshape(-1)
    out, off = [], 0
    for s in shapes:
        n = math.prod(s)
        out.append(flat[off:off + n].reshape(s))
        off += n
    return out


def kernel(x, p, norm_g, w_in, w_out, conv_w, conv_b, conv_ln_g, conv_ln_b, sgu_ln_g, sgu_ln_b, sgu_w, sgu_b, pl_norm_g, pl_gate_w, pl_proj_w, final_g, loss_target, m_norm_g, m_w_in, m_w_out, m_conv_w, m_conv_b, m_conv_ln_g, m_conv_ln_b, m_sgu_ln_g, m_sgu_ln_b, m_sgu_w, m_sgu_b, m_pl_norm_g, m_pl_gate_w, m_pl_proj_w, m_final_g, v_norm_g, v_w_in, v_w_out, v_conv_w, v_conv_b, v_conv_ln_g, v_conv_ln_b, v_sgu_ln_g, v_sgu_ln_b, v_sgu_w, v_sgu_b, v_pl_norm_g, v_pl_gate_w, v_pl_proj_w, v_final_g):
    bsz, seq, d = x.shape
    t = bsz * seq
    depth = w_in.shape[0]
    e = w_out.shape[1] * N_DEV
    pd = p.shape[-1]
    n_conv, n_sgu = conv_w.shape[0], sgu_ln_g.shape[0]
    eg = e // N_DEV

    small_shapes = [conv_w.shape, sgu_ln_g.shape, sgu_ln_b.shape]
    cast = lambda a: a.astype(MXU_DTYPE)
    first = [cast(w_in[0]), cast(w_out[0]), cast(pl_gate_w[0]), cast(pl_proj_w), _pack([conv_w, sgu_ln_g, sgu_ln_b])]
    later = [[cast(w_in[l]), cast(w_out[l]), cast(pl_gate_w[l])] for l in range(1, depth)]
    gathered, lands = _all_gather(first, "gather_weights", place=[a for grp in later for a in grp])
    gather_pending, gather_tokens = {}, 0.0
    for l in range(1, depth):
        send, recv, srcs, lnds, token = _exchange_start(later[l - 1], lands[3 * (l - 1):3 * l], False,
                                                        f"gather_start_{l}")
        gather_pending[l] = (send, recv, srcs, lnds)
        gather_tokens = gather_tokens + token[0, 0]
    w_in_g = {0: gathered[0]}
    w_out_g = {0: gathered[1].reshape(e, d)}
    gate_g = {0: gathered[2].reshape(d, d)}
    proj_g = jnp.transpose(gathered[3], (1, 2, 0, 3)).reshape(depth, pd, d)
    small_g = [_unpack(gathered[4][s], small_shapes) for s in range(N_DEV)]
    conv_w_g = jnp.concatenate([sg[0] for sg in small_g], axis=-1)
    sgu_ln_g_g = jnp.concatenate([sg[1] for sg in small_g], axis=-1)
    sgu_ln_b_g = jnp.concatenate([sg[2] for sg in small_g], axis=-1)
    sgu_wt = jnp.swapaxes(sgu_w, -1, -2)
    sgu_bt = jnp.swapaxes(sgu_b, -1, -2)

    xs = [x.reshape(t, d)]
    saved = []
    for l in range(depth):
        j = l // 2
        if l == 0:
            g_l = norm_g[0:1] + gather_tokens
        else:
            g_l = norm_g[l:l + 1]
            got = _exchange_wait(*gather_pending.pop(l), xs[-1], False, f"gather_wait_{l}")
            w_in_g[l], w_out_g[l], gate_g[l] = got[0], got[1].reshape(e, d), got[2].reshape(d, d)
        h, proj = _inproj(xs[-1], g_l, w_in_g[l])
        if l % 2 == 0:
            y1, y, q = _conv_fwd(proj, conv_w_g[j], conv_b[j:j + 1], conv_ln_g[j:j + 1], conv_ln_b[j:j + 1], seq)
        else:
            y1 = None
            y, q = _sgu_fwd(proj, sgu_ln_g_g[j:j + 1], sgu_ln_b_g[j:j + 1], sgu_w[j], sgu_bt[j])
        p_l = p[l].reshape(t, pd)
        x1, x2 = _outproj_fwd(q, xs[-1], p_l, w_out_g[l], pl_norm_g[l:l + 1], gate_g[l], proj_g[l])
        saved.append((h, proj, y1, y, x1, p_l))
        xs.append(x2)

    loss_part, dx, d_final_g = _loss_head(xs[-1], final_g.reshape(1, d), loss_target.reshape(t, d))
    loss = lax.psum(loss_part[0, 0], ("x", "y", "c"))

    d_norm_g, d_pl_norm_g = [None] * depth, [None] * depth
    scatter_pending = {}
    d_conv_w, d_conv_b, d_conv_ln_g, d_conv_ln_b = [None] * n_conv, [None] * n_conv, [None] * n_conv, [None] * n_conv
    d_sgu_ln_g, d_sgu_ln_b, d_sgu_w, d_sgu_b = [None] * n_sgu, [None] * n_sgu, [None] * n_sgu, [None] * n_sgu
    for l in reversed(range(depth)):
        j = l // 2
        h, proj, y1, y, x1, p_l = saved[l]
        dx1, dgate_p, dprojw_p, d_pl_norm_g[l] = _ple_bwd(dx, x1, p_l, pl_norm_g[l:l + 1], gate_g[l], proj_g[l])
        dy, dz, dw_out_p = _outproj_bwd(dx1, y, proj, w_out_g[l])
        if l % 2 == 0:
            dy1, d_conv_ln_g[j], d_conv_ln_b[j], d_conv_b[j] = _conv_ln_bwd(dy, y1, conv_ln_g[j:j + 1], conv_ln_b[j:j + 1])
            dproj, d_conv_w[j] = _conv_bwd(dy1, proj, dz, conv_w_g[j], seq)
        else:
            dproj, d_sgu_w[j], d_sgu_b[j], d_sgu_ln_g[j], d_sgu_ln_b[j] = _sgu_bwd(
                dy, proj, dz, sgu_ln_g_g[j:j + 1], sgu_ln_b_g[j:j + 1], sgu_w[j], sgu_wt[j], sgu_bt[j])
        parts = [_inproj_bwd_w(h, dproj), dw_out_p.reshape(N_DEV, e // N_DEV, d),
                 dgate_p.reshape(N_DEV, d // N_DEV, d), dprojw_p]
        send, recv, srcs, lnds, token = _exchange_start(parts, _place_own(parts, "place_own_grads"), True,
                                                        f"scatter_start_{l}")
        scatter_pending[l] = (send, recv, srcs, lnds)
        dx, d_norm_g[l] = _inproj_bwd_x(dproj, w_in_g[l], dx1, xs[l], norm_g[l:l + 1] + token[0, 0])
    grad_x = dx.reshape(bsz, seq, d)

    d_conv_w_full = jnp.stack(d_conv_w)
    d_sgu_ln_g_full = jnp.concatenate(d_sgu_ln_g, axis=0)
    d_sgu_ln_b_full = jnp.concatenate(d_sgu_ln_b, axis=0)
    small_part = jnp.stack([
        _pack([d_conv_w_full[..., s * eg:(s + 1) * eg], d_sgu_ln_g_full[:, s * eg:(s + 1) * eg],
               d_sgu_ln_b_full[:, s * eg:(s + 1) * eg]]) for s in range(N_DEV)])
    r_small = _all_to_all([[small_part]], "scatter_small_grads")[0]
    rep_part = _pack([jnp.concatenate(d_norm_g, axis=0), jnp.concatenate(d_conv_b, axis=0),
                      jnp.concatenate(d_conv_ln_g, axis=0), jnp.concatenate(d_conv_ln_b, axis=0),
                      jnp.stack(d_sgu_w), jnp.stack(d_sgu_b), jnp.concatenate(d_pl_norm_g, axis=0), d_final_g])
    r_rep = _all_gather([rep_part], "gather_replicated_grads")[0]

    landed = {}
    for l in reversed(range(depth)):
        landed[l] = _exchange_wait(*scatter_pending.pop(l), r_rep, True, f"scatter_wait_{l}")

    o_w_in = _adamw([landed[l][0] for l in range(depth)], w_in, m_w_in, v_w_in)
    o_w_out = _adamw([landed[l][1] for l in range(depth)], w_out, m_w_out, v_w_out)
    o_gate = _adamw([landed[l][2] for l in range(depth)], pl_gate_w, m_pl_gate_w, v_pl_gate_w)
    o_projw = _adamw([landed[l][3] for l in range(depth)], pl_proj_w, m_pl_proj_w, v_pl_proj_w)

    def packed(parts, ws, ms, vs):
        shapes = [a.shape for a in ws]
        outs = _adamw([parts.reshape(N_DEV, -1, PACK_COLS)], _pack(ws)[None], _pack(ms)[None], _pack(vs)[None])
        return [_unpack(o[0], shapes) for o in outs]

    o_small = packed(r_small, [conv_w, sgu_ln_g, sgu_ln_b], [m_conv_w, m_sgu_ln_g, m_sgu_ln_b],
                     [v_conv_w, v_sgu_ln_g, v_sgu_ln_b])
    o_rep = packed(r_rep, [norm_g, conv_b, conv_ln_g, conv_ln_b, sgu_w, sgu_b, pl_norm_g, final_g],
                   [m_norm_g, m_conv_b, m_conv_ln_g, m_conv_ln_b, m_sgu_w, m_sgu_b, m_pl_norm_g, m_final_g],
                   [v_norm_g, v_conv_b, v_conv_ln_g, v_conv_ln_b, v_sgu_w, v_sgu_b, v_pl_norm_g, v_final_g])

    def leaf(kind):
        rep, small = o_rep[kind], o_small[kind]
        return [rep[0], o_w_in[kind], o_w_out[kind], small[0], rep[1], rep[2], rep[3], small[1], small[2], rep[4],
                rep[5], rep[6], o_gate[kind], o_projw[kind], rep[7]]

    return (loss, grad_x, *leaf(0), *leaf(1), *leaf(2), *leaf(3))
```

```python
import math

import jax
import jax.numpy as jnp
from jax import lax
from jax.experimental import pallas as pl
from jax.experimental.pallas import tpu as pltpu

F32 = jnp.float32
MXU_DTYPE = jnp.bfloat16
WIRE_DTYPE = jnp.bfloat16

EPS = 1e-6
CONV_K = 31
CHUNK = 128
GROUPS = 8
HALO = 32
N_DEV = 8
DEPTH = 4

ADAM_LR = 0.001
ADAM_B1 = 0.9
ADAM_B2 = 0.999
ADAM_EPS = 1e-08
ADAM_WD = 0.01
ADAM_STEP = 10

TM_IN = 512
TM_MIX = 256
TM_OUT = 512
CONV_RC = 64
CONV_CC = 256
DCW_CC = 512
PACK_COLS = 1024

MESH_ID = pl.DeviceIdType.MESH
INV_SQRT2 = 1.0 / math.sqrt(2.0)
INV_SQRT_2PI = 1.0 / math.sqrt(2.0 * math.pi)


def _params(n_grid, vmem_mb):
    return pltpu.CompilerParams(dimension_semantics=("arbitrary",) * n_grid, vmem_limit_bytes=vmem_mb << 20)


def _whole(shape):
    nd = len(shape)
    return pl.BlockSpec(shape, lambda *_: (0,) * nd, pipeline_mode=pl.Buffered(1))


def _acc_out(shape):
    nd = len(shape)
    return pl.BlockSpec(shape, lambda *_: (0,) * nd)


def _dot(a, b):
    return jnp.dot(a.astype(MXU_DTYPE), b.astype(MXU_DTYPE), preferred_element_type=F32)


def _dot_nt(a, b):
    return lax.dot_general(a.astype(MXU_DTYPE), b.astype(MXU_DTYPE), (((1,), (1,)), ((), ())),
                           preferred_element_type=F32)


def _dot_tn(a, b):
    return lax.dot_general(a.astype(MXU_DTYPE), b.astype(MXU_DTYPE), (((0,), (0,)), ((), ())),
                           preferred_element_type=F32)


def _sigmoid(x):
    return jax.nn.sigmoid(x)


def _rms_rstd(x):
    return lax.rsqrt(jnp.mean(x * x, axis=-1, keepdims=True) + EPS)


def _rms_bwd(dy, x, rstd, g):
    gy = dy * g
    xr = x * rstd
    dx = rstd * (gy - xr * jnp.mean(gy * xr, axis=-1, keepdims=True))
    dg = jnp.sum(dy * xr, axis=0, keepdims=True)
    return dx, dg


def _ln_stats(x):
    mu = jnp.mean(x, axis=-1, keepdims=True)
    xc = x - mu
    var = jnp.mean(xc * xc, axis=-1, keepdims=True)
    rstd = lax.rsqrt(var + EPS)
    return xc * rstd, rstd


def _ln_bwd(dxhat, xhat, rstd):
    return rstd * (dxhat - jnp.mean(dxhat, axis=-1, keepdims=True)
                   - xhat * jnp.mean(dxhat * xhat, axis=-1, keepdims=True))


def _silu_grad(x, s):
    return s * (1.0 + x * (1.0 - s))


def _tril_mask():
    r = lax.broadcasted_iota(jnp.int32, (CHUNK, CHUNK), 0)
    c = lax.broadcasted_iota(jnp.int32, (CHUNK, CHUNK), 1)
    return r >= c


def _conv_apply(src_ref, w_ref, zs_ref, base, tm, e, flip, emit):
    for r0 in range(0, tm, CONV_RC):
        for c0 in range(0, e, CONV_CC):
            cols = slice(c0, c0 + CONV_CC)
            acc = None
            for s in range(8):
                nrows = CONV_RC if s == 0 else CONV_RC + 8
                z = None
                for k in range(CONV_K):
                    if (base + k) % 8 != s:
                        continue
                    wk = (CONV_K - 1 - k) if flip else k
                    term = w_ref[wk:wk + 1, cols] * src_ref[pl.ds(r0 + base + k - s, nrows), cols]
                    z = term if z is None else z + term
                if s == 0:
                    acc = z
                else:
                    zs_ref[s - 1, pl.ds(0, nrows), :] = z
                    acc = acc + zs_ref[s - 1, pl.ds(s, CONV_RC), :]
            emit(r0, c0, acc)


def _mesh_pos():
    return lax.axis_index("x"), lax.axis_index("y"), lax.axis_index("c")


def _slot(px, py, pc):
    return 4 * px + 2 * py + pc


def _peers(x, y, c):
    return [((1 - x) if (k & 4) else x, (1 - y) if (k & 2) else y, (1 - c) if (k & 1) else c)
            for k in range(1, N_DEV)]


HBM_SPEC = pl.BlockSpec(memory_space=pltpu.HBM)
SEM_SPEC = pl.BlockSpec(memory_space=pltpu.SEMAPHORE)
SIDE_EFFECT = pltpu.SideEffectType.DATAFLOW_SIDE_EFFECTING


def _exchange_copy(src_refs, land_refs, send_sems, recv_sems, i, k, peer, scatter, me):
    slot = _slot(*peer)
    return pltpu.make_async_remote_copy(
        src_ref=src_refs[i].at[slot] if scatter else src_refs[i],
        dst_ref=land_refs[i].at[me if me is not None else slot],
        send_sem=send_sems.at[i * 7 + k], recv_sem=recv_sems.at[i * 7 + k],
        device_id=peer, device_id_type=MESH_ID)


def _exchange_start(srcs, lands, scatter, name):
    n = len(srcs)

    def body(*refs):
        src_refs, land_refs = refs[:n], refs[n:2 * n]
        send_sems, recv_sems, token = refs[2 * n], refs[2 * n + 1], refs[-1]
        x, y, c = _mesh_pos()
        me = _slot(x, y, c)
        for i in range(n):
            for k, peer in enumerate(_peers(x, y, c)):
                _exchange_copy(src_refs, land_refs, send_sems, recv_sems, i, k, peer, scatter, me).start()
        token[...] = jnp.zeros_like(token)

    arrays = list(srcs) + list(lands)
    outs = pl.pallas_call(
        body, name=name,
        out_shape=(pltpu.SemaphoreType.DMA((7 * n,)), pltpu.SemaphoreType.DMA((7 * n,)),
                   *[pltpu.HBM(a.shape, a.dtype) for a in arrays], jax.ShapeDtypeStruct((8, 128), F32)),
        in_specs=[HBM_SPEC] * (2 * n),
        out_specs=(SEM_SPEC, SEM_SPEC, *[HBM_SPEC] * (2 * n), pl.BlockSpec(memory_space=pltpu.VMEM)),
        input_output_aliases={i: 2 + i for i in range(2 * n)},
        compiler_params=pltpu.CompilerParams(has_side_effects=SIDE_EFFECT),
    )(*[pltpu.with_memory_space_constraint(a, pltpu.HBM) for a in arrays])
    return outs[0], outs[1], list(outs[2:2 + n]), list(outs[2 + n:2 + 2 * n]), outs[-1]


def _exchange_wait(send_sems, recv_sems, srcs, lands, after, scatter, name):
    n = len(srcs)

    def body(*refs):
        src_refs, land_refs = refs[:n], refs[n:2 * n]
        send, recv = refs[2 * n], refs[2 * n + 1]
        x, y, c = _mesh_pos()
        for i in range(n):
            for k, peer in enumerate(_peers(x, y, c)):
                cp = _exchange_copy(src_refs, land_refs, send, recv, i, k, peer, scatter, None)
                cp.wait_send()
                cp.wait_recv()

    arrays = list(srcs) + list(lands)
    outs = pl.pallas_call(
        body, name=name,
        out_shape=tuple(pltpu.HBM(a.shape, a.dtype) for a in arrays),
        in_specs=[HBM_SPEC] * (2 * n) + [SEM_SPEC, SEM_SPEC, pl.BlockSpec(memory_space=pl.ANY)],
        out_specs=tuple([HBM_SPEC] * (2 * n)),
        input_output_aliases={i: i for i in range(2 * n)},
        compiler_params=pltpu.CompilerParams(has_side_effects=SIDE_EFFECT),
    )(*arrays, send_sems, recv_sems, after)
    return list(outs[n:])


def _place_own(parts, scatter, name):
    n = len(parts)
    me = jnp.reshape(_slot(*_mesh_pos()), (1,)).astype(jnp.int32)

    def body(me_ref, *refs):
        for i in range(n):
            refs[n + i][0] = refs[i][0] if scatter else refs[i][...]

    def slot_spec(shape):
        rest = len(shape)
        return pl.BlockSpec((1,) + tuple(shape), lambda i, me_ref: (me_ref[0],) + (0,) * rest)

    def whole_spec(shape):
        nd = len(shape)
        return pl.BlockSpec(tuple(shape), lambda i, me_ref: (0,) * nd)

    blocks = [a.shape[1:] if scatter else a.shape for a in parts]
    return pl.pallas_call(
        body, name=name,
        grid_spec=pltpu.PrefetchScalarGridSpec(
            num_scalar_prefetch=1, grid=(1,),
            in_specs=[slot_spec(b) if scatter else whole_spec(b) for b in blocks],
            out_specs=[slot_spec(b) for b in blocks]),
        out_shape=[jax.ShapeDtypeStruct((N_DEV,) + tuple(b), a.dtype) for a, b in zip(parts, blocks)],
        compiler_params=_params(1, 32),
    )(me, *parts)


def _all_gather(items, name):
    n = len(items)

    def body(*refs):
        in_refs, out_refs = refs[:n], refs[n:2 * n]
        send_sems, recv_sems, local_sems = refs[2 * n:]
        x, y, c = _mesh_pos()
        me, sibling = (x, y, c), (x, y, 1 - c)
        chips = [(1 - x, y), (x, 1 - y), (1 - x, 1 - y)]

        def copy(i, k, block, to, src=None):
            dst = out_refs[i].at[_slot(*block)]
            return pltpu.make_async_remote_copy(
                src_ref=dst if src is None else src, dst_ref=dst,
                send_sem=send_sems.at[i * 7 + k], recv_sem=recv_sems.at[i * 7 + k],
                device_id=to, device_id_type=MESH_ID)

        mine = [pltpu.make_async_copy(in_refs[i], out_refs[i].at[_slot(*me)], local_sems.at[i]) for i in range(n)]
        for cp in mine:
            cp.start()
        first = []
        for i in range(n):
            first.append(copy(i, 0, me, sibling, src=in_refs[i]))
            for j, chip in enumerate(chips):
                first.append(copy(i, 1 + j, me, (*chip, c), src=in_refs[i]))
        for cp in first:
            cp.start()
        passed = []
        for j, chip in enumerate(chips):
            for i in range(n):
                copy(i, 1 + j, (*chip, c), me).wait_recv()
                fwd = copy(i, 4 + j, (*chip, c), sibling)
                fwd.start()
                passed.append(fwd)
        for i in range(n):
            copy(i, 0, sibling, me).wait_recv()
            for j, chip in enumerate(chips):
                copy(i, 4 + j, (*chip, 1 - c), me).wait_recv()
        for cp in first + passed:
            cp.wait_send()
        for cp in mine:
            cp.wait()

    any_spec = pl.BlockSpec(memory_space=pl.ANY)
    return pl.pallas_call(
        body, name=name,
        out_shape=[jax.ShapeDtypeStruct((N_DEV,) + a.shape, a.dtype) for a in items],
        in_specs=[any_spec] * n, out_specs=[any_spec] * n,
        scratch_shapes=[pltpu.SemaphoreType.DMA((7 * n,)), pltpu.SemaphoreType.DMA((7 * n,)),
                        pltpu.SemaphoreType.DMA((n,))],
    )(*items)


def _all_to_all(groups, name):
    flat = [(gi, li, a) for gi, grp in enumerate(groups) for li, a in enumerate(grp)]
    n, ng = len(flat), len(groups)

    def body(*refs):
        in_refs, out_refs = refs[:n], refs[n:n + ng]
        send_sems, recv_sems, local_sems = refs[n + ng:]
        x, y, c = _mesh_pos()
        me = _slot(x, y, c)
        local, sends, recvs = [], [], []
        for i, (gi, li, _) in enumerate(flat):
            local.append(pltpu.make_async_copy(in_refs[i].at[me], out_refs[gi].at[me, li], local_sems.at[i]))
            for k in range(1, N_DEV):
                px = (1 - x) if (k & 4) else x
                py = (1 - y) if (k & 2) else y
                pc = (1 - c) if (k & 1) else c
                peer = _slot(px, py, pc)
                sem = i * 7 + k - 1
                sends.append(pltpu.make_async_remote_copy(
                    src_ref=in_refs[i].at[peer], dst_ref=out_refs[gi].at[me, li],
                    send_sem=send_sems.at[sem], recv_sem=recv_sems.at[sem],
                    device_id=(px, py, pc), device_id_type=MESH_ID))
                recvs.append(pltpu.make_async_remote_copy(
                    src_ref=in_refs[i].at[peer], dst_ref=out_refs[gi].at[peer, li],
                    send_sem=send_sems.at[sem], recv_sem=recv_sems.at[sem],
                    device_id=(px, py, pc), device_id_type=MESH_ID))
        for cp in local:
            cp.start()
        for cp in sends:
            cp.start()
        for cp in recvs:
            cp.wait_recv()
        for cp in sends:
            cp.wait_send()
        for cp in local:
            cp.wait()

    any_spec = pl.BlockSpec(memory_space=pl.ANY)
    return pl.pallas_call(
        body, name=name,
        out_shape=[jax.ShapeDtypeStruct((N_DEV, len(grp)) + grp[0].shape[1:], grp[0].dtype) for grp in groups],
        in_specs=[any_spec] * n, out_specs=[any_spec] * ng,
        scratch_shapes=[pltpu.SemaphoreType.DMA((7 * n,)), pltpu.SemaphoreType.DMA((7 * n,)),
                        pltpu.SemaphoreType.DMA((n,))],
    )(*[a for _, _, a in flat])


def _inproj(x, g, w_blk):
    t, d = x.shape
    nb, _, bn = w_blk.shape
    tm = TM_IN

    def body(x_ref, g_ref, w_ref, h_ref, proj_ref):
        xv = x_ref[...]
        hv = (xv * _rms_rstd(xv) * g_ref[...]).astype(MXU_DTYPE)
        h_ref[...] = hv
        for j in range(nb):
            proj_ref[:, j * bn:(j + 1) * bn] = jnp.dot(hv, w_ref[j], preferred_element_type=F32)

    row = lambda w: pl.BlockSpec((tm, w), lambda i: (i, 0))
    return pl.pallas_call(
        body, name="inproj", grid=(t // tm,),
        in_specs=[row(d), _whole((1, d)), _whole(w_blk.shape)],
        out_specs=[row(d), row(nb * bn)],
        out_shape=[jax.ShapeDtypeStruct((t, d), MXU_DTYPE), jax.ShapeDtypeStruct((t, nb * bn), F32)],
        compiler_params=_params(1, 56),
    )(x, g, w_blk)


def _conv_fwd(proj, cw, cb, lg, lb, seq):
    t = proj.shape[0]
    e = proj.shape[1] // 3
    tm = TM_MIX
    nt = seq // tm
    hb = tm // HALO

    def body(a_ref, b_ref, z_ref, ah_ref, bh_ref, cw_ref, cb_ref, lg_ref, lb_ref, y1_ref, y_ref, q_ref, y0s, zs):
        first = lax.rem(pl.program_id(0), nt) == 0
        y0s[pl.ds(HALO, tm), :] = a_ref[...] * _sigmoid(b_ref[...])

        @pl.when(first)
        def _():
            y0s[pl.ds(0, HALO), :] = jnp.zeros((HALO, e), F32)

        @pl.when(jnp.logical_not(first))
        def _():
            y0s[pl.ds(0, HALO), :] = ah_ref[...] * _sigmoid(bh_ref[...])

        def emit(r0, c0, acc):
            y1_ref[r0:r0 + CONV_RC, c0:c0 + CONV_CC] = acc + cb_ref[:, c0:c0 + CONV_CC]
        _conv_apply(y0s, cw_ref, zs, HALO - (CONV_K - 1), tm, e, False, emit)

        xhat, _ = _ln_stats(y1_ref[...])
        y2 = xhat * lg_ref[...] + lb_ref[...]
        y = y2 * _sigmoid(y2)
        y_ref[...] = y
        z = z_ref[...]
        q_ref[...] = (y * (z * _sigmoid(z))).astype(q_ref.dtype)

    tile = lambda col: pl.BlockSpec((tm, e), lambda i: (i, col))
    prev = lambda col: pl.BlockSpec((HALO, e), lambda i: (jnp.maximum(i * hb - 1, 0), col))
    return pl.pallas_call(
        body, name="conv_fwd", grid=(t // tm,),
        in_specs=[tile(0), tile(1), tile(2), prev(0), prev(1), _whole(cw.shape), _whole((1, e)), _whole((1, e)),
                  _whole((1, e))],
        out_specs=[tile(0), tile(0), tile(0)],
        out_shape=[jax.ShapeDtypeStruct((t, e), F32), jax.ShapeDtypeStruct((t, e), F32),
                   jax.ShapeDtypeStruct((t, e), MXU_DTYPE)],
        scratch_shapes=[pltpu.VMEM((tm + HALO, e), F32), pltpu.VMEM((7, CONV_RC + 8, CONV_CC), F32)],
        compiler_params=_params(1, 48),
    )(proj, proj, proj, proj, proj, cw, cb, lg, lb)


def _sgu_parts(a, b, lg, lb, sw_ref, sbt_ref, mixed_s, tm, e):
    eg = e // GROUPS
    ea = lax.erf(a * INV_SQRT2)
    eb = lax.erf(b * INV_SQRT2)
    u = 0.5 * a * (1.0 + ea)
    v0 = 0.5 * b * (1.0 + eb)
    xhat, rstd = _ln_stats(v0)
    v = (xhat * lg + lb).astype(MXU_DTYPE)
    mask = _tril_mask()
    for g in range(GROUPS):
        wt = jnp.where(mask, sw_ref[g], 0.0).astype(MXU_DTYPE)
        bcol = sbt_ref[:, g:g + 1]
        for ch in range(tm // CHUNK):
            rows = slice(ch * CHUNK, (ch + 1) * CHUNK)
            cols = slice(g * eg, (g + 1) * eg)
            mixed_s[rows, cols] = jnp.dot(wt, v[rows, cols], preferred_element_type=F32) + bcol
    return ea, eb, u, xhat, rstd, v


def _sgu_fwd(proj, lg, lb, sw, sbt):
    t = proj.shape[0]
    e = proj.shape[1] // 3
    tm = TM_MIX

    def body(a_ref, b_ref, z_ref, lg_ref, lb_ref, sw_ref, sbt_ref, y_ref, q_ref, mixed_s):
        _, _, u, _, _, _ = _sgu_parts(a_ref[...], b_ref[...], lg_ref[...], lb_ref[...], sw_ref, sbt_ref, mixed_s, tm, e)
        y = u * mixed_s[...]
        y_ref[...] = y
        z = z_ref[...]
        q_ref[...] = (y * (z * _sigmoid(z))).astype(q_ref.dtype)

    tile = lambda col: pl.BlockSpec((tm, e), lambda i: (i, col))
    return pl.pallas_call(
        body, name="sgu_fwd", grid=(t // tm,),
        in_specs=[tile(0), tile(1), tile(2), _whole((1, e)), _whole((1, e)), _whole(sw.shape), _whole(sbt.shape)],
        out_specs=[tile(0), tile(0)],
        out_shape=[jax.ShapeDtypeStruct((t, e), F32), jax.ShapeDtypeStruct((t, e), MXU_DTYPE)],
        scratch_shapes=[pltpu.VMEM((tm, e), F32)],
        compiler_params=_params(1, 48),
    )(proj, proj, proj, lg, lb, sw, sbt)


def _outproj_fwd(q, x, p_l, w_out, plg, gate_w, proj_w):
    t, d = x.shape
    e = q.shape[1]
    pd = p_l.shape[1]
    tm = TM_OUT

    def body(q_ref, x_ref, p_ref, wo_ref, plg_ref, gw_ref, pw_ref, x1_ref, x2_ref):
        x1 = x_ref[...] + jnp.dot(q_ref[...], wo_ref[...], preferred_element_type=F32)
        x1_ref[...] = x1
        rn = x1 * _rms_rstd(x1) * plg_ref[...]
        gate = _sigmoid(_dot(rn, gw_ref[...]))
        pp = _dot(p_ref[...], pw_ref[...])
        x2_ref[...] = x1 + gate * pp

    row = lambda w: pl.BlockSpec((tm, w), lambda i: (i, 0))
    return pl.pallas_call(
        body, name="outproj_fwd", grid=(t // tm,),
        in_specs=[row(e), row(d), row(pd), _whole((e, d)), _whole((1, d)), _whole((d, d)), _whole((pd, d))],
        out_specs=[row(d), row(d)],
        out_shape=[jax.ShapeDtypeStruct((t, d), F32), jax.ShapeDtypeStruct((t, d), F32)],
        compiler_params=_params(1, 48),
    )(q, x, p_l, w_out, plg, gate_w, proj_w)


def _loss_head(xf, fg, tgt):
    t, d = xf.shape
    tm = TM_OUT
    nsteps = t // tm

    def body(x_ref, g_ref, t_ref, loss_ref, dx_ref, dg_ref, sq_s):
        i = pl.program_id(0)

        @pl.when(i == 0)
        def _():
            sq_s[...] = jnp.zeros_like(sq_s)
            dg_ref[...] = jnp.zeros_like(dg_ref)
        x = x_ref[...]
        rstd = _rms_rstd(x)
        err = x * rstd * g_ref[...] - t_ref[...]
        sq_s[...] += jnp.sum(err * err, axis=0, keepdims=True)
        dx, dg = _rms_bwd(err * (1.0 / d), x, rstd, g_ref[...])
        dx_ref[...] = dx
        dg_ref[...] += dg

        @pl.when(i == nsteps - 1)
        def _():
            loss_ref[...] = jnp.sum(sq_s[...], axis=1, keepdims=True) * (0.5 / d)

    row = pl.BlockSpec((tm, d), lambda i: (i, 0))
    return pl.pallas_call(
        body, name="loss_head", grid=(nsteps,),
        in_specs=[row, _whole((1, d)), row],
        out_specs=[_acc_out((1, 1)), row, _acc_out((1, d))],
        out_shape=[jax.ShapeDtypeStruct((1, 1), F32), jax.ShapeDtypeStruct((t, d), F32),
                   jax.ShapeDtypeStruct((1, d), F32)],
        scratch_shapes=[pltpu.VMEM((1, d), F32)],
        compiler_params=_params(1, 32),
    )(xf, fg, tgt)


def _ple_bwd(dx2, x1, p_l, plg, gate_w, proj_w):
    t, d = x1.shape
    pd = p_l.shape[1]
    tm = TM_OUT
    nsteps = t // tm
    bn = d // N_DEV

    def body(dx2_ref, x1_ref, p_ref, plg_ref, gw_ref, pw_ref, dx1_ref, dgw_ref, dpw_ref, dplg_ref, gw_acc, pw_acc):
        i = pl.program_id(0)

        @pl.when(i == 0)
        def _():
            gw_acc[...] = jnp.zeros_like(gw_acc)
            pw_acc[...] = jnp.zeros_like(pw_acc)
            dplg_ref[...] = jnp.zeros_like(dplg_ref)
        dx2 = dx2_ref[...]
        x1 = x1_ref[...]
        plg = plg_ref[...]
        rstd = _rms_rstd(x1)
        rn = (x1 * rstd * plg).astype(MXU_DTYPE)
        gate = _sigmoid(jnp.dot(rn, gw_ref[...], preferred_element_type=F32))
        p_b = p_ref[...].astype(MXU_DTYPE)
        pp = jnp.dot(p_b, pw_ref[...], preferred_element_type=F32)
        dpp = (dx2 * gate).astype(MXU_DTYPE)
        dgpre = (dx2 * pp * gate * (1.0 - gate)).astype(MXU_DTYPE)
        pw_acc[...] += _dot_tn(p_b, dpp)
        gw_acc[...] += _dot_tn(rn, dgpre)
        drn = _dot_nt(dgpre, gw_ref[...])
        dx, dg = _rms_bwd(drn, x1, rstd, plg)
        dx1_ref[...] = dx2 + dx
        dplg_ref[...] += dg

        @pl.when(i == nsteps - 1)
        def _():
            dgw_ref[...] = gw_acc[...].astype(dgw_ref.dtype)
            for j in range(N_DEV):
                dpw_ref[j] = pw_acc[:, j * bn:(j + 1) * bn].astype(dpw_ref.dtype)

    row = lambda w: pl.BlockSpec((tm, w), lambda i: (i, 0))
    return pl.pallas_call(
        body, name="ple_bwd", grid=(nsteps,),
        in_specs=[row(d), row(d), row(pd), _whole((1, d)), _whole((d, d)), _whole((pd, d))],
        out_specs=[row(d), _acc_out((d, d)), _acc_out((N_DEV, pd, bn)), _acc_out((1, d))],
        out_shape=[jax.ShapeDtypeStruct((t, d), F32), jax.ShapeDtypeStruct((d, d), WIRE_DTYPE),
                   jax.ShapeDtypeStruct((N_DEV, pd, bn), WIRE_DTYPE), jax.ShapeDtypeStruct((1, d), F32)],
        scratch_shapes=[pltpu.VMEM((d, d), F32), pltpu.VMEM((pd, d), F32)],
        compiler_params=_params(1, 48),
    )(dx2, x1, p_l, plg, gate_w, proj_w)


def _outproj_bwd(dx1, y, proj, w_out):
    t, d = dx1.shape
    e = y.shape[1]
    tm = TM_MIX
    nsteps = t // tm

    def body(dx1_ref, y_ref, z_ref, wo_ref, dy_ref, dz_ref, dwo_ref, wo_acc):
        i = pl.program_id(0)

        @pl.when(i == 0)
        def _():
            wo_acc[...] = jnp.zeros_like(wo_acc)
        dx1 = dx1_ref[...].astype(MXU_DTYPE)
        y = y_ref[...]
        z = z_ref[...]
        s = _sigmoid(z)
        sz = z * s
        q = (y * sz).astype(MXU_DTYPE)
        wo_acc[...] += _dot_tn(q, dx1)
        dq = _dot_nt(dx1, wo_ref[...])
        dy_ref[...] = dq * sz
        dz_ref[...] = (dq * y * _silu_grad(z, s)).astype(dz_ref.dtype)

        @pl.when(i == nsteps - 1)
        def _():
            dwo_ref[...] = wo_acc[...].astype(dwo_ref.dtype)

    return pl.pallas_call(
        body, name="outproj_bwd", grid=(nsteps,),
        in_specs=[pl.BlockSpec((tm, d), lambda i: (i, 0)), pl.BlockSpec((tm, e), lambda i: (i, 0)),
                  pl.BlockSpec((tm, e), lambda i: (i, 2)), _whole((e, d))],
        out_specs=[pl.BlockSpec((tm, e), lambda i: (i, 0)), pl.BlockSpec((tm, e), lambda i: (i, 0)),
                   _acc_out((e, d))],
        out_shape=[jax.ShapeDtypeStruct((t, e), F32), jax.ShapeDtypeStruct((t, e), MXU_DTYPE),
                   jax.ShapeDtypeStruct((e, d), WIRE_DTYPE)],
        scratch_shapes=[pltpu.VMEM((e, d), F32)],
        compiler_params=_params(1, 48),
    )(dx1, y, proj, w_out)


def _conv_ln_bwd(dy, y1, lg, lb):
    t, e = dy.shape
    tm = TM_MIX

    def body(dy_ref, y1_ref, lg_ref, lb_ref, dy1_ref, dlg_ref, dlb_ref, dcb_ref):
        @pl.when(pl.program_id(0) == 0)
        def _():
            dlg_ref[...] = jnp.zeros_like(dlg_ref)
            dlb_ref[...] = jnp.zeros_like(dlb_ref)
            dcb_ref[...] = jnp.zeros_like(dcb_ref)
        xhat, rstd = _ln_stats(y1_ref[...])
        lg = lg_ref[...]
        y2 = xhat * lg + lb_ref[...]
        dy2 = dy_ref[...] * _silu_grad(y2, _sigmoid(y2))
        dlg_ref[...] += jnp.sum(dy2 * xhat, axis=0, keepdims=True)
        dlb_ref[...] += jnp.sum(dy2, axis=0, keepdims=True)
        dy1 = _ln_bwd(dy2 * lg, xhat, rstd)
        dy1_ref[...] = dy1
        dcb_ref[...] += jnp.sum(dy1, axis=0, keepdims=True)

    row = pl.BlockSpec((tm, e), lambda i: (i, 0))
    return pl.pallas_call(
        body, name="conv_ln_bwd", grid=(t // tm,),
        in_specs=[row, row, _whole((1, e)), _whole((1, e))],
        out_specs=[row, _acc_out((1, e)), _acc_out((1, e)), _acc_out((1, e))],
        out_shape=[jax.ShapeDtypeStruct((t, e), F32)] + [jax.ShapeDtypeStruct((1, e), F32)] * 3,
        compiler_params=_params(1, 48),
    )(dy, y1, lg, lb)


def _conv_bwd(dy1, proj, dz, cw, seq):
    t, e = dy1.shape
    tm = TM_MIX
    nt = seq // tm
    hb = tm // HALO
    n_halo_blocks = t // HALO

    def body(d_ref, dn_ref, a_ref, b_ref, ah_ref, bh_ref, dz_ref, cw_ref, dproj_ref, dcw_ref, y0s, d1s, zs, dsh, dcw8):
        i = pl.program_id(0)
        pos = lax.rem(i, nt)

        @pl.when(i == 0)
        def _():
            dcw8[...] = jnp.zeros_like(dcw8)
        a = a_ref[...]
        sb = _sigmoid(b_ref[...])
        y0s[pl.ds(HALO, tm), :] = a * sb
        d1s[pl.ds(0, tm), :] = d_ref[...]

        @pl.when(pos == 0)
        def _():
            y0s[pl.ds(0, HALO), :] = jnp.zeros((HALO, e), F32)

        @pl.when(pos != 0)
        def _():
            y0s[pl.ds(0, HALO), :] = ah_ref[...] * _sigmoid(bh_ref[...])

        @pl.when(pos == nt - 1)
        def _():
            d1s[pl.ds(tm, HALO), :] = jnp.zeros((HALO, e), F32)

        @pl.when(pos != nt - 1)
        def _():
            d1s[pl.ds(tm, HALO), :] = dn_ref[...]

        base = HALO - (CONV_K - 1)
        for c0 in range(0, e, DCW_CC):
            cols = slice(c0, c0 + DCW_CC)
            dcur = d_ref[:, cols]
            for s in range(1, 8):
                dsh[s - 1, pl.ds(0, 8), :] = jnp.zeros((8, DCW_CC), F32)
                dsh[s - 1, pl.ds(tm, 8), :] = jnp.zeros((8, DCW_CC), F32)
                dsh[s - 1, pl.ds(s, tm), :] = dcur
            for k in range(CONV_K):
                s = (base + k) % 8
                off = base + k - s
                if s == 0:
                    prod, n = dcur * y0s[pl.ds(off, tm), cols], tm
                else:
                    prod, n = dsh[s - 1] * y0s[pl.ds(off, tm + 8), cols], tm + 8
                dcw8[k, :, cols] += jnp.sum(prod.reshape(n // 8, 8, DCW_CC), axis=0)

        def emit(r0, c0, dy0):
            rs, cs = slice(r0, r0 + CONV_RC), slice(c0, c0 + CONV_CC)
            sbv = _sigmoid(b_ref[rs, cs])
            av = a_ref[rs, cs]
            dproj_ref[rs, c0:c0 + CONV_CC] = (dy0 * sbv).astype(dproj_ref.dtype)
            dproj_ref[rs, e + c0:e + c0 + CONV_CC] = (dy0 * av * sbv * (1.0 - sbv)).astype(dproj_ref.dtype)
        _conv_apply(d1s, cw_ref, zs, 0, tm, e, True, emit)
        dproj_ref[:, 2 * e:3 * e] = dz_ref[...]

        @pl.when(i == t // tm - 1)
        def _():
            dcw_ref[...] = jnp.sum(dcw8[...], axis=1)

    tile = lambda col: pl.BlockSpec((tm, e), lambda i: (i, col))
    prev = lambda col: pl.BlockSpec((HALO, e), lambda i: (jnp.maximum(i * hb - 1, 0), col))
    nxt = pl.BlockSpec((HALO, e), lambda i: (jnp.minimum((i + 1) * hb, n_halo_blocks - 1), 0))
    return pl.pallas_call(
        body, name="conv_bwd", grid=(t // tm,),
        in_specs=[tile(0), nxt, tile(0), tile(1), prev(0), prev(1), tile(0), _whole(cw.shape)],
        out_specs=[pl.BlockSpec((tm, 3 * e), lambda i: (i, 0)), _acc_out(cw.shape)],
        out_shape=[jax.ShapeDtypeStruct((t, 3 * e), MXU_DTYPE), jax.ShapeDtypeStruct(cw.shape, F32)],
        scratch_shapes=[pltpu.VMEM((tm + HALO, e), F32), pltpu.VMEM((tm + HALO, e), F32),
                        pltpu.VMEM((7, CONV_RC + 8, CONV_CC), F32), pltpu.VMEM((7, tm + 8, DCW_CC), F32),
                        pltpu.VMEM((CONV_K, 8, e), F32)],
        compiler_params=_params(1, 56),
    )(dy1, dy1, proj, proj, proj, proj, dz, cw)


def _sgu_bwd(dy, proj, dz, lg, lb, sw, swt, sbt):
    t, e = dy.shape
    eg = e // GROUPS
    tm = TM_MIX
    nsteps = t // tm

    def body(dy_ref, a_ref, b_ref, dz_ref, lg_ref, lb_ref, sw_ref, swt_ref, sbt_ref,
             dproj_ref, dsw_ref, dsb_ref, dlg_ref, dlb_ref, mixed_s, dv_s, sb_acc):
        i = pl.program_id(0)

        @pl.when(i == 0)
        def _():
            dsw_ref[...] = jnp.zeros_like(dsw_ref)
            sb_acc[...] = jnp.zeros_like(sb_acc)
            dlg_ref[...] = jnp.zeros_like(dlg_ref)
            dlb_ref[...] = jnp.zeros_like(dlb_ref)
        a = a_ref[...]
        b = b_ref[...]
        lg = lg_ref[...]
        ea, eb, u, xhat, rstd, v = _sgu_parts(a, b, lg, lb_ref[...], sw_ref, sbt_ref, mixed_s, tm, e)
        dy = dy_ref[...]
        du = dy * mixed_s[...]
        dmixed = (dy * u).astype(MXU_DTYPE)
        mask = _tril_mask()
        mask_t = (lax.broadcasted_iota(jnp.int32, (CHUNK, CHUNK), 0)
                  <= lax.broadcasted_iota(jnp.int32, (CHUNK, CHUNK), 1))
        ones = jnp.ones((8, eg), MXU_DTYPE)
        for g in range(GROUPS):
            wtt = jnp.where(mask_t, swt_ref[g], 0.0).astype(MXU_DTYPE)
            cols = slice(g * eg, (g + 1) * eg)
            for ch in range(tm // CHUNK):
                rows = slice(ch * CHUNK, (ch + 1) * CHUNK)
                dm = dmixed[rows, cols]
                dv_s[rows, cols] = jnp.dot(wtt, dm, preferred_element_type=F32)
                dsw_ref[g] += _dot_nt(dm, v[rows, cols])
                sb_acc[g] += _dot_nt(ones, dm)
        dv = dv_s[...]
        dlg_ref[...] += jnp.sum(dv * xhat, axis=0, keepdims=True)
        dlb_ref[...] += jnp.sum(dv, axis=0, keepdims=True)
        dv0 = _ln_bwd(dv * lg, xhat, rstd)
        pdf_a = jnp.exp(-0.5 * a * a) * INV_SQRT_2PI
        pdf_b = jnp.exp(-0.5 * b * b) * INV_SQRT_2PI
        dproj_ref[:, 0:e] = (du * (0.5 * (1.0 + ea) + a * pdf_a)).astype(dproj_ref.dtype)
        dproj_ref[:, e:2 * e] = (dv0 * (0.5 * (1.0 + eb) + b * pdf_b)).astype(dproj_ref.dtype)
        dproj_ref[:, 2 * e:3 * e] = dz_ref[...]

        @pl.when(i == nsteps - 1)
        def _():
            for g in range(GROUPS):
                dsw_ref[g] = jnp.where(mask, dsw_ref[g], 0.0)
                dsb_ref[g:g + 1, :] = sb_acc[g, 0:1, :]

    tile = lambda col: pl.BlockSpec((tm, e), lambda i: (i, col))
    return pl.pallas_call(
        body, name="sgu_bwd", grid=(nsteps,),
        in_specs=[tile(0), tile(0), tile(1), tile(0), _whole((1, e)), _whole((1, e)), _whole(sw.shape),
                  _whole(swt.shape), _whole(sbt.shape)],
        out_specs=[pl.BlockSpec((tm, 3 * e), lambda i: (i, 0)), _acc_out(sw.shape), _acc_out((GROUPS, CHUNK)),
                   _acc_out((1, e)), _acc_out((1, e))],
        out_shape=[jax.ShapeDtypeStruct((t, 3 * e), MXU_DTYPE), jax.ShapeDtypeStruct(sw.shape, F32),
                   jax.ShapeDtypeStruct((GROUPS, CHUNK), F32), jax.ShapeDtypeStruct((1, e), F32),
                   jax.ShapeDtypeStruct((1, e), F32)],
        scratch_shapes=[pltpu.VMEM((tm, e), F32), pltpu.VMEM((tm, e), F32), pltpu.VMEM((GROUPS, 8, CHUNK), F32)],
        compiler_params=_params(1, 56),
    )(dy, proj, proj, dz, lg, lb, sw, swt, sbt)


def _inproj_bwd_x(dproj, w_blk, dx1, x, g):
    t, d = x.shape
    nb, _, bn = w_blk.shape
    tm = TM_IN

    def body(dp_ref, w_ref, dx1_ref, x_ref, g_ref, dx_ref, dg_ref):
        @pl.when(pl.program_id(0) == 0)
        def _():
            dg_ref[...] = jnp.zeros_like(dg_ref)
        dh = None
        for j in range(nb):
            term = _dot_nt(dp_ref[:, j * bn:(j + 1) * bn], w_ref[j])
            dh = term if dh is None else dh + term
        xv = x_ref[...]
        dx, dg = _rms_bwd(dh, xv, _rms_rstd(xv), g_ref[...])
        dx_ref[...] = dx1_ref[...] + dx
        dg_ref[...] += dg

    row = lambda w: pl.BlockSpec((tm, w), lambda i: (i, 0))
    return pl.pallas_call(
        body, name="inproj_bwd_x", grid=(t // tm,),
        in_specs=[row(nb * bn), _whole(w_blk.shape), row(d), row(d), _whole((1, d))],
        out_specs=[row(d), _acc_out((1, d))],
        out_shape=[jax.ShapeDtypeStruct((t, d), F32), jax.ShapeDtypeStruct((1, d), F32)],
        compiler_params=_params(1, 56),
    )(dproj, w_blk, dx1, x, g)


def _inproj_bwd_w(h, dproj):
    t, d = h.shape
    bn = dproj.shape[1] // N_DEV
    tm = TM_IN
    nsteps = t // tm

    nh = 2
    per = N_DEV // nh

    def body(h_ref, dp_ref, dw_ref, acc):
        i = pl.program_id(1)

        @pl.when(i == 0)
        def _():
            acc[...] = jnp.zeros_like(acc)
        hv = h_ref[...]
        for jj in range(per):
            acc[jj] += _dot_tn(hv, dp_ref[:, jj * bn:(jj + 1) * bn])

        @pl.when(i == nsteps - 1)
        def _():
            dw_ref[...] = acc[...].astype(dw_ref.dtype)

    return pl.pallas_call(
        body, name="inproj_bwd_w", grid=(nh, nsteps),
        in_specs=[pl.BlockSpec((tm, d), lambda hh, i: (i, 0)), pl.BlockSpec((tm, per * bn), lambda hh, i: (i, hh))],
        out_specs=[pl.BlockSpec((per, d, bn), lambda hh, i: (hh, 0, 0))],
        out_shape=[jax.ShapeDtypeStruct((N_DEV, d, bn), WIRE_DTYPE)],
        scratch_shapes=[pltpu.VMEM((per, d, bn), F32)],
        compiler_params=_params(2, 56),
    )(h, dproj)[0]


def _adamw(parts, w, m, v):
    nl, r, c = w.shape
    tr = r
    for cand in (512, 256, 128, 64, 32, 16, 8):
        if r % cand == 0 and cand * c * 4 <= (1 << 19):
            tr = cand
            break
    bc1 = 1.0 - ADAM_B1 ** ADAM_STEP
    bc2 = 1.0 - ADAM_B2 ** ADAM_STEP

    def body(*refs):
        p_refs = refs[:nl]
        w_ref, m_ref, v_ref, g_ref, d_ref, nm_ref, nv_ref = refs[nl:]

        def update(p_ref):
            g = p_ref[0].astype(F32)
            for s in range(1, N_DEV):
                g = g + p_ref[s].astype(F32)
            nm = ADAM_B1 * m_ref[0] + (1.0 - ADAM_B1) * g
            nv = ADAM_B2 * v_ref[0] + (1.0 - ADAM_B2) * (g * g)
            g_ref[0] = g
            nm_ref[0] = nm
            nv_ref[0] = nv
            d_ref[0] = -ADAM_LR * ((nm / bc1) / (jnp.sqrt(nv / bc2) + ADAM_EPS) + ADAM_WD * w_ref[0])

        if nl == 1:
            update(p_refs[0])
        else:
            for kk in range(nl):
                pl.when(pl.program_id(0) == kk)(lambda kk=kk: update(p_refs[kk]))

    part_spec = lambda kk: pl.BlockSpec((N_DEV, tr, c), lambda l, i: (0, jnp.where(l == kk, i, 0), 0))
    row = pl.BlockSpec((1, tr, c), lambda l, i: (l, i, 0))
    return pl.pallas_call(
        body, name="adamw", grid=(nl, r // tr),
        in_specs=[part_spec(kk) for kk in range(nl)] + [row, row, row],
        out_specs=[row] * 4,
        out_shape=[jax.ShapeDtypeStruct((nl, r, c), F32)] * 4,
        compiler_params=_params(2, 48),
    )(*parts, w, m, v)


def _pack(arrays):
    flat = jnp.concatenate([a.reshape(-1).astype(F32) for a in arrays])
    unit = 8 * PACK_COLS
    padded = -(-flat.shape[0] // unit) * unit
    return jnp.pad(flat, (0, padded - flat.shape[0])).reshape(-1, PACK_COLS)


def _unpack(packed, shapes):
    flat = packed.reshape(-1)
    out, off = [], 0
    for s in shapes:
        n = math.prod(s)
        out.append(flat[off:off + n].reshape(s))
        off += n
    return out


def kernel(x, p, norm_g, w_in, w_out, conv_w, conv_b, conv_ln_g, conv_ln_b, sgu_ln_g, sgu_ln_b, sgu_w, sgu_b, pl_norm_g, pl_gate_w, pl_proj_w, final_g, loss_target, m_norm_g, m_w_in, m_w_out, m_conv_w, m_conv_b, m_conv_ln_g, m_conv_ln_b, m_sgu_ln_g, m_sgu_ln_b, m_sgu_w, m_sgu_b, m_pl_norm_g, m_pl_gate_w, m_pl_proj_w, m_final_g, v_norm_g, v_w_in, v_w_out, v_conv_w, v_conv_b, v_conv_ln_g, v_conv_ln_b, v_sgu_ln_g, v_sgu_ln_b, v_sgu_w, v_sgu_b, v_pl_norm_g, v_pl_gate_w, v_pl_proj_w, v_final_g):
    bsz, seq, d = x.shape
    t = bsz * seq
    depth = w_in.shape[0]
    e = w_out.shape[1] * N_DEV
    pd = p.shape[-1]
    n_conv, n_sgu = conv_w.shape[0], sgu_ln_g.shape[0]
    eg = e // N_DEV

    small_shapes = [conv_w.shape, sgu_ln_g.shape, sgu_ln_b.shape]
    cast = lambda a: a.astype(MXU_DTYPE)
    first = [cast(w_in[0]), cast(w_out[0]), cast(pl_gate_w[0]), cast(pl_proj_w), _pack([conv_w, sgu_ln_g, sgu_ln_b])]
    later = [[cast(w_in[l]), cast(w_out[l]), cast(pl_gate_w[l])] for l in range(1, depth)]
    gathered = _all_gather(first, "gather_weights")
    gather_pending, gather_tokens = {}, 0.0
    for l in range(1, depth):
        lands = _place_own(later[l - 1], False, "place_own_weights")
        send, recv, srcs, lnds, token = _exchange_start(later[l - 1], lands, False, f"gather_start_{l}")
        gather_pending[l] = (send, recv, srcs, lnds)
        gather_tokens = gather_tokens + token[0, 0]
    w_in_g = {0: gathered[0]}
    w_out_g = {0: gathered[1].reshape(e, d)}
    gate_g = {0: gathered[2].reshape(d, d)}
    proj_g = jnp.transpose(gathered[3], (1, 2, 0, 3)).reshape(depth, pd, d)
    small_g = [_unpack(gathered[4][s], small_shapes) for s in range(N_DEV)]
    conv_w_g = jnp.concatenate([sg[0] for sg in small_g], axis=-1)
    sgu_ln_g_g = jnp.concatenate([sg[1] for sg in small_g], axis=-1)
    sgu_ln_b_g = jnp.concatenate([sg[2] for sg in small_g], axis=-1)
    sgu_wt = jnp.swapaxes(sgu_w, -1, -2)
    sgu_bt = jnp.swapaxes(sgu_b, -1, -2)

    xs = [x.reshape(t, d)]
    saved = []
    for l in range(depth):
        j = l // 2
        if l == 0:
            g_l = norm_g[0:1] + gather_tokens
        else:
            g_l = norm_g[l:l + 1]
            got = _exchange_wait(*gather_pending.pop(l), xs[-1], False, f"gather_wait_{l}")
            w_in_g[l], w_out_g[l], gate_g[l] = got[0], got[1].reshape(e, d), got[2].reshape(d, d)
        h, proj = _inproj(xs[-1], g_l, w_in_g[l])
        if l % 2 == 0:
            y1, y, q = _conv_fwd(proj, conv_w_g[j], conv_b[j:j + 1], conv_ln_g[j:j + 1], conv_ln_b[j:j + 1], seq)
        else:
            y1 = None
            y, q = _sgu_fwd(proj, sgu_ln_g_g[j:j + 1], sgu_ln_b_g[j:j + 1], sgu_w[j], sgu_bt[j])
        p_l = p[l].reshape(t, pd)
        x1, x2 = _outproj_fwd(q, xs[-1], p_l, w_out_g[l], pl_norm_g[l:l + 1], gate_g[l], proj_g[l])
        saved.append((h, proj, y1, y, x1, p_l))
        xs.append(x2)

    loss_part, dx, d_final_g = _loss_head(xs[-1], final_g.reshape(1, d), loss_target.reshape(t, d))
    loss = lax.psum(loss_part[0, 0], ("x", "y", "c"))

    d_norm_g, d_pl_norm_g = [None] * depth, [None] * depth
    scatter_pending = {}
    d_conv_w, d_conv_b, d_conv_ln_g, d_conv_ln_b = [None] * n_conv, [None] * n_conv, [None] * n_conv, [None] * n_conv
    d_sgu_ln_g, d_sgu_ln_b, d_sgu_w, d_sgu_b = [None] * n_sgu, [None] * n_sgu, [None] * n_sgu, [None] * n_sgu
    for l in reversed(range(depth)):
        j = l // 2
        h, proj, y1, y, x1, p_l = saved[l]
        dx1, dgate_p, dprojw_p, d_pl_norm_g[l] = _ple_bwd(dx, x1, p_l, pl_norm_g[l:l + 1], gate_g[l], proj_g[l])
        dy, dz, dw_out_p = _outproj_bwd(dx1, y, proj, w_out_g[l])
        if l % 2 == 0:
            dy1, d_conv_ln_g[j], d_conv_ln_b[j], d_conv_b[j] = _conv_ln_bwd(dy, y1, conv_ln_g[j:j + 1], conv_ln_b[j:j + 1])
            dproj, d_conv_w[j] = _conv_bwd(dy1, proj, dz, conv_w_g[j], seq)
        else:
            dproj, d_sgu_w[j], d_sgu_b[j], d_sgu_ln_g[j], d_sgu_ln_b[j] = _sgu_bwd(
                dy, proj, dz, sgu_ln_g_g[j:j + 1], sgu_ln_b_g[j:j + 1], sgu_w[j], sgu_wt[j], sgu_bt[j])
        parts = [_inproj_bwd_w(h, dproj), dw_out_p.reshape(N_DEV, e // N_DEV, d),
                 dgate_p.reshape(N_DEV, d // N_DEV, d), dprojw_p]
        send, recv, srcs, lnds, token = _exchange_start(parts, _place_own(parts, True, "place_own_grads"), True,
                                                        f"scatter_start_{l}")
        scatter_pending[l] = (send, recv, srcs, lnds)
        dx, d_norm_g[l] = _inproj_bwd_x(dproj, w_in_g[l], dx1, xs[l], norm_g[l:l + 1] + token[0, 0])
    grad_x = dx.reshape(bsz, seq, d)

    d_conv_w_full = jnp.stack(d_conv_w)
    d_sgu_ln_g_full = jnp.concatenate(d_sgu_ln_g, axis=0)
    d_sgu_ln_b_full = jnp.concatenate(d_sgu_ln_b, axis=0)
    small_part = jnp.stack([
        _pack([d_conv_w_full[..., s * eg:(s + 1) * eg], d_sgu_ln_g_full[:, s * eg:(s + 1) * eg],
               d_sgu_ln_b_full[:, s * eg:(s + 1) * eg]]) for s in range(N_DEV)])
    r_small = _all_to_all([[small_part]], "scatter_small_grads")[0]
    rep_part = _pack([jnp.concatenate(d_norm_g, axis=0), jnp.concatenate(d_conv_b, axis=0),
                      jnp.concatenate(d_conv_ln_g, axis=0), jnp.concatenate(d_conv_ln_b, axis=0),
                      jnp.stack(d_sgu_w), jnp.stack(d_sgu_b), jnp.concatenate(d_pl_norm_g, axis=0), d_final_g])
    r_rep = _all_gather([rep_part], "gather_replicated_grads")[0]

    landed = {}
    for l in reversed(range(depth)):
        landed[l] = _exchange_wait(*scatter_pending.pop(l), r_rep, True, f"scatter_wait_{l}")

    o_w_in = _adamw([landed[l][0] for l in range(depth)], w_in, m_w_in, v_w_in)
    o_w_out = _adamw([landed[l][1] for l in range(depth)], w_out, m_w_out, v_w_out)
    o_gate = _adamw([landed[l][2] for l in range(depth)], pl_gate_w, m_pl_gate_w, v_pl_gate_w)
    o_projw = _adamw([landed[l][3] for l in range(depth)], pl_proj_w, m_pl_proj_w, v_pl_proj_w)

    def packed(parts, ws, ms, vs):
        shapes = [a.shape for a in ws]
        outs = _adamw([parts.reshape(N_DEV, -1, PACK_COLS)], _pack(ws)[None], _pack(ms)[None], _pack(vs)[None])
        return [_unpack(o[0], shapes) for o in outs]

    o_small = packed(r_small, [conv_w, sgu_ln_g, sgu_ln_b], [m_conv_w, m_sgu_ln_g, m_sgu_ln_b],
                     [v_conv_w, v_sgu_ln_g, v_sgu_ln_b])
    o_rep = packed(r_rep, [norm_g, conv_b, conv_ln_g, conv_ln_b, sgu_w, sgu_b, pl_norm_g, final_g],
                   [m_norm_g, m_conv_b, m_conv_ln_g, m_conv_ln_b, m_sgu_w, m_sgu_b, m_pl_norm_g, m_final_g],
                   [v_norm_g, v_conv_b, v_conv_ln_g, v_conv_ln_b, v_sgu_w, v_sgu_b, v_pl_norm_g, v_final_g])

    def leaf(kind):
        rep, small = o_rep[kind], o_small[kind]
        return [rep[0], o_w_in[kind], o_w_out[kind], small[0], rep[1], rep[2], rep[3], small[1], small[2], rep[4],
                rep[5], rep[6], o_gate[kind], o_projw[kind], rep[7]]

    return (loss, grad_x, *leaf(0), *leaf(1), *leaf(2), *leaf(3))
```

```python
import math

import jax
import jax.numpy as jnp
from jax import lax
from jax.experimental import pallas as pl
from jax.experimental.pallas import tpu as pltpu

F32 = jnp.float32
MXU_DTYPE = jnp.bfloat16
WIRE_DTYPE = jnp.bfloat16

EPS = 1e-6
CONV_K = 31
CHUNK = 128
GROUPS = 8
HALO = 32
N_DEV = 8
DEPTH = 4

ADAM_LR = 0.001
ADAM_B1 = 0.9
ADAM_B2 = 0.999
ADAM_EPS = 1e-08
ADAM_WD = 0.01
ADAM_STEP = 10

TM_IN = 512
TM_MIX = 256
TM_OUT = 512
FUSE_SB = 128
CONV_RC = 64
CONV_CC = 128
DCW_CC = 256
DCW_RC = 64
DCW_CHUNKS = 3
PACK_COLS = 1024

MESH_ID = pl.DeviceIdType.MESH
INV_SQRT2 = 1.0 / math.sqrt(2.0)
INV_SQRT_2PI = 1.0 / math.sqrt(2.0 * math.pi)


def _params(n_grid, vmem_mb):
    return pltpu.CompilerParams(dimension_semantics=("arbitrary",) * n_grid, vmem_limit_bytes=vmem_mb << 20)


def _whole(shape):
    nd = len(shape)
    return pl.BlockSpec(shape, lambda *_: (0,) * nd, pipeline_mode=pl.Buffered(1))


def _acc_out(shape):
    nd = len(shape)
    return pl.BlockSpec(shape, lambda *_: (0,) * nd)


def _dot(a, b):
    return jnp.dot(a.astype(MXU_DTYPE), b.astype(MXU_DTYPE), preferred_element_type=F32)


def _dot_nt(a, b):
    return lax.dot_general(a.astype(MXU_DTYPE), b.astype(MXU_DTYPE), (((1,), (1,)), ((), ())),
                           preferred_element_type=F32)


def _dot_tn(a, b):
    return lax.dot_general(a.astype(MXU_DTYPE), b.astype(MXU_DTYPE), (((0,), (0,)), ((), ())),
                           preferred_element_type=F32)


def _sigmoid(x):
    return jax.nn.sigmoid(x)


def _rms_rstd(x):
    return lax.rsqrt(jnp.mean(x * x, axis=-1, keepdims=True) + EPS)


def _rms_bwd(dy, x, rstd, g):
    gy = dy * g
    xr = x * rstd
    dx = rstd * (gy - xr * jnp.mean(gy * xr, axis=-1, keepdims=True))
    dg = jnp.sum(dy * xr, axis=0, keepdims=True)
    return dx, dg


def _ln_stats(x):
    mu = jnp.mean(x, axis=-1, keepdims=True)
    xc = x - mu
    var = jnp.mean(xc * xc, axis=-1, keepdims=True)
    rstd = lax.rsqrt(var + EPS)
    return xc * rstd, rstd


def _ln_bwd(dxhat, xhat, rstd):
    return rstd * (dxhat - jnp.mean(dxhat, axis=-1, keepdims=True)
                   - xhat * jnp.mean(dxhat * xhat, axis=-1, keepdims=True))


def _silu_grad(x, s):
    return s * (1.0 + x * (1.0 - s))


def _tril_mask():
    r = lax.broadcasted_iota(jnp.int32, (CHUNK, CHUNK), 0)
    c = lax.broadcasted_iota(jnp.int32, (CHUNK, CHUNK), 1)
    return r >= c


def _conv_weights_to_sublanes(w_ref, w8_ref):
    for k in range(CONV_K):
        w8_ref[k] = jnp.broadcast_to(w_ref[k:k + 1, :], w8_ref.shape[1:])


def _conv_apply(src_ref, w8_ref, zs_ref, base, tm, e, flip, emit):
    for r0 in range(0, tm, CONV_RC):
        for c0 in range(0, e, CONV_CC):
            cols = slice(c0, c0 + CONV_CC)
            acc = None
            for s in range(8):
                nrows = CONV_RC if s == 0 else CONV_RC + 8
                taps = [k for k in range(CONV_K) if (base + k) % 8 == s]
                off0 = base + taps[0] - s
                span = nrows + 8 * (len(taps) - 1)
                window = src_ref[pl.ds(r0 + off0, span), cols].reshape(span // 8, 8, CONV_CC)
                z = None
                for m, k in enumerate(taps):
                    wk = (CONV_K - 1 - k) if flip else k
                    term = w8_ref[wk, :, cols][None] * window[m:m + nrows // 8]
                    z = term if z is None else z + term
                z = z.reshape(nrows, CONV_CC)
                if s == 0:
                    acc = z
                else:
                    zs_ref[s - 1, pl.ds(0, nrows), :] = z
                    acc = acc + zs_ref[s - 1, pl.ds(s, CONV_RC), :]
            emit(r0, c0, acc)


def _mesh_pos():
    return lax.axis_index("x"), lax.axis_index("y"), lax.axis_index("c")


def _slot(px, py, pc):
    return 4 * px + 2 * py + pc


def _peers(x, y, c):
    return [((1 - x) if (k & 4) else x, (1 - y) if (k & 2) else y, (1 - c) if (k & 1) else c)
            for k in range(1, N_DEV)]


HBM_SPEC = pl.BlockSpec(memory_space=pltpu.HBM)
SEM_SPEC = pl.BlockSpec(memory_space=pltpu.SEMAPHORE)
SIDE_EFFECT = pltpu.SideEffectType.DATAFLOW_SIDE_EFFECTING


def _exchange_copy(src_refs, land_refs, send_sems, recv_sems, i, k, peer, scatter, me):
    slot = _slot(*peer)
    return pltpu.make_async_remote_copy(
        src_ref=src_refs[i].at[slot] if scatter else src_refs[i],
        dst_ref=land_refs[i].at[me if me is not None else slot],
        send_sem=send_sems.at[i * 7 + k], recv_sem=recv_sems.at[i * 7 + k],
        device_id=peer, device_id_type=MESH_ID)


def _exchange_start(srcs, lands, scatter, name):
    n = len(srcs)

    def body(*refs):
        src_refs, land_refs = refs[:n], refs[n:2 * n]
        send_sems, recv_sems, token = refs[2 * n], refs[2 * n + 1], refs[-1]
        x, y, c = _mesh_pos()
        me = _slot(x, y, c)
        for i in range(n):
            for k, peer in enumerate(_peers(x, y, c)):
                _exchange_copy(src_refs, land_refs, send_sems, recv_sems, i, k, peer, scatter, me).start()
        token[...] = jnp.zeros_like(token)

    arrays = list(srcs) + list(lands)
    outs = pl.pallas_call(
        body, name=name,
        out_shape=(pltpu.SemaphoreType.DMA((7 * n,)), pltpu.SemaphoreType.DMA((7 * n,)),
                   *[pltpu.HBM(a.shape, a.dtype) for a in arrays], jax.ShapeDtypeStruct((8, 128), F32)),
        in_specs=[HBM_SPEC] * (2 * n),
        out_specs=(SEM_SPEC, SEM_SPEC, *[HBM_SPEC] * (2 * n), pl.BlockSpec(memory_space=pltpu.VMEM)),
        input_output_aliases={i: 2 + i for i in range(2 * n)},
        compiler_params=pltpu.CompilerParams(has_side_effects=SIDE_EFFECT),
    )(*[pltpu.with_memory_space_constraint(a, pltpu.HBM) for a in arrays])
    return outs[0], outs[1], list(outs[2:2 + n]), list(outs[2 + n:2 + 2 * n]), outs[-1]


def _exchange_wait(send_sems, recv_sems, srcs, lands, after, scatter, name):
    n = len(srcs)

    def body(*refs):
        src_refs, land_refs = refs[:n], refs[n:2 * n]
        send, recv = refs[2 * n], refs[2 * n + 1]
        x, y, c = _mesh_pos()
        for i in range(n):
            for k, peer in enumerate(_peers(x, y, c)):
                cp = _exchange_copy(src_refs, land_refs, send, recv, i, k, peer, scatter, None)
                cp.wait_send()
                cp.wait_recv()

    arrays = list(srcs) + list(lands)
    outs = pl.pallas_call(
        body, name=name,
        out_shape=tuple(pltpu.HBM(a.shape, a.dtype) for a in arrays),
        in_specs=[HBM_SPEC] * (2 * n) + [SEM_SPEC, SEM_SPEC, pl.BlockSpec(memory_space=pl.ANY)],
        out_specs=tuple([HBM_SPEC] * (2 * n)),
        input_output_aliases={i: i for i in range(2 * n)},
        compiler_params=pltpu.CompilerParams(has_side_effects=SIDE_EFFECT),
    )(*arrays, send_sems, recv_sems, after)
    return list(outs[n:])


def _place_own(parts, scatter, name):
    n = len(parts)
    me = jnp.reshape(_slot(*_mesh_pos()), (1,)).astype(jnp.int32)

    def body(me_ref, *refs):
        for i in range(n):
            refs[n + i][0] = refs[i][0] if scatter else refs[i][...]

    def slot_spec(shape):
        rest = len(shape)
        return pl.BlockSpec((1,) + tuple(shape), lambda i, me_ref: (me_ref[0],) + (0,) * rest)

    def whole_spec(shape):
        nd = len(shape)
        return pl.BlockSpec(tuple(shape), lambda i, me_ref: (0,) * nd)

    blocks = [a.shape[1:] if scatter else a.shape for a in parts]
    return pl.pallas_call(
        body, name=name,
        grid_spec=pltpu.PrefetchScalarGridSpec(
            num_scalar_prefetch=1, grid=(1,),
            in_specs=[slot_spec(b) if scatter else whole_spec(b) for b in blocks],
            out_specs=[slot_spec(b) for b in blocks]),
        out_shape=[jax.ShapeDtypeStruct((N_DEV,) + tuple(b), a.dtype) for a, b in zip(parts, blocks)],
        compiler_params=_params(1, 32),
    )(me, *parts)


def _all_gather(items, name):
    n = len(items)

    def body(*refs):
        in_refs, out_refs = refs[:n], refs[n:2 * n]
        send_sems, recv_sems, local_sems = refs[2 * n:]
        x, y, c = _mesh_pos()
        me, sibling = (x, y, c), (x, y, 1 - c)
        chips = [(1 - x, y), (x, 1 - y), (1 - x, 1 - y)]

        def copy(i, k, block, to, src=None):
            dst = out_refs[i].at[_slot(*block)]
            return pltpu.make_async_remote_copy(
                src_ref=dst if src is None else src, dst_ref=dst,
                send_sem=send_sems.at[i * 7 + k], recv_sem=recv_sems.at[i * 7 + k],
                device_id=to, device_id_type=MESH_ID)

        mine = [pltpu.make_async_copy(in_refs[i], out_refs[i].at[_slot(*me)], local_sems.at[i]) for i in range(n)]
        for cp in mine:
            cp.start()
        first = []
        for i in range(n):
            first.append(copy(i, 0, me, sibling, src=in_refs[i]))
            for j, chip in enumerate(chips):
                first.append(copy(i, 1 + j, me, (*chip, c), src=in_refs[i]))
        for cp in first:
            cp.start()
        passed = []
        for j, chip in enumerate(chips):
            for i in range(n):
                copy(i, 1 + j, (*chip, c), me).wait_recv()
                fwd = copy(i, 4 + j, (*chip, c), sibling)
                fwd.start()
                passed.append(fwd)
        for i in range(n):
            copy(i, 0, sibling, me).wait_recv()
            for j, chip in enumerate(chips):
                copy(i, 4 + j, (*chip, 1 - c), me).wait_recv()
        for cp in first + passed:
            cp.wait_send()
        for cp in mine:
            cp.wait()

    any_spec = pl.BlockSpec(memory_space=pl.ANY)
    return pl.pallas_call(
        body, name=name,
        out_shape=[jax.ShapeDtypeStruct((N_DEV,) + a.shape, a.dtype) for a in items],
        in_specs=[any_spec] * n, out_specs=[any_spec] * n,
        scratch_shapes=[pltpu.SemaphoreType.DMA((7 * n,)), pltpu.SemaphoreType.DMA((7 * n,)),
                        pltpu.SemaphoreType.DMA((n,))],
    )(*items)


def _all_to_all(groups, name):
    flat = [(gi, li, a) for gi, grp in enumerate(groups) for li, a in enumerate(grp)]
    n, ng = len(flat), len(groups)

    def body(*refs):
        in_refs, out_refs = refs[:n], refs[n:n + ng]
        send_sems, recv_sems, local_sems = refs[n + ng:]
        x, y, c = _mesh_pos()
        me = _slot(x, y, c)
        local, sends, recvs = [], [], []
        for i, (gi, li, _) in enumerate(flat):
            local.append(pltpu.make_async_copy(in_refs[i].at[me], out_refs[gi].at[me, li], local_sems.at[i]))
            for k in range(1, N_DEV):
                px = (1 - x) if (k & 4) else x
                py = (1 - y) if (k & 2) else y
                pc = (1 - c) if (k & 1) else c
                peer = _slot(px, py, pc)
                sem = i * 7 + k - 1
                sends.append(pltpu.make_async_remote_copy(
                    src_ref=in_refs[i].at[peer], dst_ref=out_refs[gi].at[me, li],
                    send_sem=send_sems.at[sem], recv_sem=recv_sems.at[sem],
                    device_id=(px, py, pc), device_id_type=MESH_ID))
                recvs.append(pltpu.make_async_remote_copy(
                    src_ref=in_refs[i].at[peer], dst_ref=out_refs[gi].at[peer, li],
                    send_sem=send_sems.at[sem], recv_sem=recv_sems.at[sem],
                    device_id=(px, py, pc), device_id_type=MESH_ID))
        for cp in local:
            cp.start()
        for cp in sends:
            cp.start()
        for cp in recvs:
            cp.wait_recv()
        for cp in sends:
            cp.wait_send()
        for cp in local:
            cp.wait()

    any_spec = pl.BlockSpec(memory_space=pl.ANY)
    return pl.pallas_call(
        body, name=name,
        out_shape=[jax.ShapeDtypeStruct((N_DEV, len(grp)) + grp[0].shape[1:], grp[0].dtype) for grp in groups],
        in_specs=[any_spec] * n, out_specs=[any_spec] * ng,
        scratch_shapes=[pltpu.SemaphoreType.DMA((7 * n,)), pltpu.SemaphoreType.DMA((7 * n,)),
                        pltpu.SemaphoreType.DMA((n,))],
    )(*[a for _, _, a in flat])


def _layer_fwd(x, p_l, g, w_blk, w_out, plg, gate_w, proj_w, seq, conv=None, sgu=None):
    t, d = x.shape
    nb, _, bn = w_blk.shape
    e = w_out.shape[0]
    pd = p_l.shape[1]
    tm = TM_MIX
    nt = seq // tm
    is_conv = conv is not None
    mixer_args = conv if is_conv else sgu
    n_mix = len(mixer_args)

    def body(*refs):
        x_ref, p_ref, g_ref, w_ref, wo_ref, plg_ref, gw_ref, pw_ref = refs[:8]
        mix = refs[8:8 + n_mix]
        outs = refs[8 + n_mix:]
        if is_conv:
            cw_ref, cb_ref, lg_ref, lb_ref = mix
            h_ref, proj_ref, y1_ref, y_ref, x1_ref, x2_ref, y0s, zs, w8 = outs

            @pl.when(lax.rem(pl.program_id(0), nt) == 0)
            def _():
                y0s[pl.ds(0, HALO), :] = jnp.zeros((HALO, e), F32)
            _conv_weights_to_sublanes(cw_ref, w8)
        else:
            lg_ref, lb_ref, sw_ref, sbt_ref = mix
            h_ref, proj_ref, y_ref, x1_ref, x2_ref, mixed_s = outs

        for sb in range(tm // FUSE_SB):
            rows = pl.ds(sb * FUSE_SB, FUSE_SB)
            xv = x_ref[rows, :]
            hv = (xv * _rms_rstd(xv) * g_ref[...]).astype(MXU_DTYPE)
            h_ref[rows, :] = hv
            for j in range(nb):
                proj_ref[rows, j * bn:(j + 1) * bn] = jnp.dot(hv, w_ref[j], preferred_element_type=F32)
            if is_conv:
                y0s[pl.ds(HALO + sb * FUSE_SB, FUSE_SB), :] = proj_ref[rows, 0:e] * _sigmoid(proj_ref[rows, e:2 * e])

        if is_conv:
            def emit(r0, c0, acc):
                y1_ref[r0:r0 + CONV_RC, c0:c0 + CONV_CC] = acc + cb_ref[:, c0:c0 + CONV_CC]
            _conv_apply(y0s, w8, zs, HALO - (CONV_K - 1), tm, e, False, emit)

        for sb in range(tm // FUSE_SB):
            rows = pl.ds(sb * FUSE_SB, FUSE_SB)
            if is_conv:
                xhat, _ = _ln_stats(y1_ref[rows, :])
                y2 = xhat * lg_ref[...] + lb_ref[...]
                y = y2 * _sigmoid(y2)
            else:
                _, _, u, _, _, _ = _sgu_parts(proj_ref[rows, 0:e], proj_ref[rows, e:2 * e], lg_ref[...], lb_ref[...],
                                              sw_ref, sbt_ref, mixed_s, FUSE_SB, e)
                y = u * mixed_s[...]
            y_ref[rows, :] = y
            z = proj_ref[rows, 2 * e:3 * e]
            q = (y * (z * _sigmoid(z))).astype(MXU_DTYPE)
            x1 = x_ref[rows, :] + jnp.dot(q, wo_ref[...], preferred_element_type=F32)
            x1_ref[rows, :] = x1
            rn = x1 * _rms_rstd(x1) * plg_ref[...]
            gate = _sigmoid(_dot(rn, gw_ref[...]))
            x2_ref[rows, :] = x1 + gate * _dot(p_ref[rows, :], pw_ref[...])

        if is_conv:
            y0s[pl.ds(0, HALO), :] = y0s[pl.ds(tm, HALO), :]

    row = lambda w: pl.BlockSpec((tm, w), lambda i: (i, 0))
    f32 = lambda w: jax.ShapeDtypeStruct((t, w), F32)
    out_shape = [jax.ShapeDtypeStruct((t, d), MXU_DTYPE), f32(3 * e)] + ([f32(e)] if is_conv else []) + [f32(e), f32(d), f32(d)]
    out_specs = [row(d), row(3 * e)] + ([row(e)] if is_conv else []) + [row(e), row(d), row(d)]
    scratch = ([pltpu.VMEM((tm + HALO, e), F32), pltpu.VMEM((7, CONV_RC + 8, CONV_CC), F32),
                pltpu.VMEM((CONV_K, 8, e), F32)] if is_conv else [pltpu.VMEM((FUSE_SB, e), F32)])
    return pl.pallas_call(
        body, name="layer_fwd_conv" if is_conv else "layer_fwd_sgu", grid=(t // tm,),
        in_specs=[row(d), row(pd), _whole((1, d)), _whole(w_blk.shape), _whole((e, d)), _whole((1, d)),
                  _whole((d, d)), _whole((pd, d))] + [_whole(a.shape) for a in mixer_args],
        out_specs=out_specs, out_shape=out_shape, scratch_shapes=scratch,
        compiler_params=_params(1, 60),
    )(x, p_l, g, w_blk, w_out, plg, gate_w, proj_w, *mixer_args)


def _sgu_parts(a, b, lg, lb, sw_ref, sbt_ref, mixed_s, tm, e):
    eg = e // GROUPS
    ea = lax.erf(a * INV_SQRT2)
    eb = lax.erf(b * INV_SQRT2)
    u = 0.5 * a * (1.0 + ea)
    v0 = 0.5 * b * (1.0 + eb)
    xhat, rstd = _ln_stats(v0)
    v = (xhat * lg + lb).astype(MXU_DTYPE)
    mask = _tril_mask()
    for g in range(GROUPS):
        wt = jnp.where(mask, sw_ref[g], 0.0).astype(MXU_DTYPE)
        bcol = sbt_ref[:, g:g + 1]
        for ch in range(tm // CHUNK):
            rows = slice(ch * CHUNK, (ch + 1) * CHUNK)
            cols = slice(g * eg, (g + 1) * eg)
            mixed_s[rows, cols] = jnp.dot(wt, v[rows, cols], preferred_element_type=F32) + bcol
    return ea, eb, u, xhat, rstd, v


def _loss_head(xf, fg, tgt):
    t, d = xf.shape
    tm = TM_OUT
    nsteps = t // tm

    def body(x_ref, g_ref, t_ref, loss_ref, dx_ref, dg_ref, sq_s):
        i = pl.program_id(0)

        @pl.when(i == 0)
        def _():
            sq_s[...] = jnp.zeros_like(sq_s)
            dg_ref[...] = jnp.zeros_like(dg_ref)
        x = x_ref[...]
        rstd = _rms_rstd(x)
        err = x * rstd * g_ref[...] - t_ref[...]
        sq_s[...] += jnp.sum(err * err, axis=0, keepdims=True)
        dx, dg = _rms_bwd(err * (1.0 / d), x, rstd, g_ref[...])
        dx_ref[...] = dx
        dg_ref[...] += dg

        @pl.when(i == nsteps - 1)
        def _():
            loss_ref[...] = jnp.sum(sq_s[...], axis=1, keepdims=True) * (0.5 / d)

    row = pl.BlockSpec((tm, d), lambda i: (i, 0))
    return pl.pallas_call(
        body, name="loss_head", grid=(nsteps,),
        in_specs=[row, _whole((1, d)), row],
        out_specs=[_acc_out((1, 1)), row, _acc_out((1, d))],
        out_shape=[jax.ShapeDtypeStruct((1, 1), F32), jax.ShapeDtypeStruct((t, d), F32),
                   jax.ShapeDtypeStruct((1, d), F32)],
        scratch_shapes=[pltpu.VMEM((1, d), F32)],
        compiler_params=_params(1, 32),
    )(xf, fg, tgt)


def _ple_bwd(dx2, x1, p_l, plg, gate_w, proj_w):
    t, d = x1.shape
    pd = p_l.shape[1]
    tm = TM_OUT
    nsteps = t // tm
    bn = d // N_DEV

    def body(dx2_ref, x1_ref, p_ref, plg_ref, gw_ref, pw_ref, dx1_ref, dgw_ref, dpw_ref, dplg_ref, gw_acc, pw_acc):
        i = pl.program_id(0)

        @pl.when(i == 0)
        def _():
            gw_acc[...] = jnp.zeros_like(gw_acc)
            pw_acc[...] = jnp.zeros_like(pw_acc)
            dplg_ref[...] = jnp.zeros_like(dplg_ref)
        dx2 = dx2_ref[...]
        x1 = x1_ref[...]
        plg = plg_ref[...]
        rstd = _rms_rstd(x1)
        rn = (x1 * rstd * plg).astype(MXU_DTYPE)
        gate = _sigmoid(jnp.dot(rn, gw_ref[...], preferred_element_type=F32))
        p_b = p_ref[...].astype(MXU_DTYPE)
        pp = jnp.dot(p_b, pw_ref[...], preferred_element_type=F32)
        dpp = (dx2 * gate).astype(MXU_DTYPE)
        dgpre = (dx2 * pp * gate * (1.0 - gate)).astype(MXU_DTYPE)
        pw_acc[...] += _dot_tn(p_b, dpp)
        gw_acc[...] += _dot_tn(rn, dgpre)
        drn = _dot_nt(dgpre, gw_ref[...])
        dx, dg = _rms_bwd(drn, x1, rstd, plg)
        dx1_ref[...] = dx2 + dx
        dplg_ref[...] += dg

        @pl.when(i == nsteps - 1)
        def _():
            dgw_ref[...] = gw_acc[...].astype(dgw_ref.dtype)
            for j in range(N_DEV):
                dpw_ref[j] = pw_acc[:, j * bn:(j + 1) * bn].astype(dpw_ref.dtype)

    row = lambda w: pl.BlockSpec((tm, w), lambda i: (i, 0))
    return pl.pallas_call(
        body, name="ple_bwd", grid=(nsteps,),
        in_specs=[row(d), row(d), row(pd), _whole((1, d)), _whole((d, d)), _whole((pd, d))],
        out_specs=[row(d), _acc_out((d, d)), _acc_out((N_DEV, pd, bn)), _acc_out((1, d))],
        out_shape=[jax.ShapeDtypeStruct((t, d), F32), jax.ShapeDtypeStruct((d, d), WIRE_DTYPE),
                   jax.ShapeDtypeStruct((N_DEV, pd, bn), WIRE_DTYPE), jax.ShapeDtypeStruct((1, d), F32)],
        scratch_shapes=[pltpu.VMEM((d, d), F32), pltpu.VMEM((pd, d), F32)],
        compiler_params=_params(1, 48),
    )(dx2, x1, p_l, plg, gate_w, proj_w)


def _outproj_bwd(dx1, y, proj, w_out):
    t, d = dx1.shape
    e = y.shape[1]
    tm = TM_MIX
    nsteps = t // tm

    def body(dx1_ref, y_ref, z_ref, wo_ref, dy_ref, dz_ref, dwo_ref, wo_acc):
        i = pl.program_id(0)

        @pl.when(i == 0)
        def _():
            wo_acc[...] = jnp.zeros_like(wo_acc)
        dx1 = dx1_ref[...].astype(MXU_DTYPE)
        y = y_ref[...]
        z = z_ref[...]
        s = _sigmoid(z)
        sz = z * s
        q = (y * sz).astype(MXU_DTYPE)
        wo_acc[...] += _dot_tn(q, dx1)
        dq = _dot_nt(dx1, wo_ref[...])
        dy_ref[...] = dq * sz
        dz_ref[...] = (dq * y * _silu_grad(z, s)).astype(dz_ref.dtype)

        @pl.when(i == nsteps - 1)
        def _():
            dwo_ref[...] = wo_acc[...].astype(dwo_ref.dtype)

    return pl.pallas_call(
        body, name="outproj_bwd", grid=(nsteps,),
        in_specs=[pl.BlockSpec((tm, d), lambda i: (i, 0)), pl.BlockSpec((tm, e), lambda i: (i, 0)),
                  pl.BlockSpec((tm, e), lambda i: (i, 2)), _whole((e, d))],
        out_specs=[pl.BlockSpec((tm, e), lambda i: (i, 0)), pl.BlockSpec((tm, e), lambda i: (i, 0)),
                   _acc_out((e, d))],
        out_shape=[jax.ShapeDtypeStruct((t, e), F32), jax.ShapeDtypeStruct((t, e), MXU_DTYPE),
                   jax.ShapeDtypeStruct((e, d), WIRE_DTYPE)],
        scratch_shapes=[pltpu.VMEM((e, d), F32)],
        compiler_params=_params(1, 48),
    )(dx1, y, proj, w_out)


def _conv_ln_bwd(dy, y1, lg, lb):
    t, e = dy.shape
    tm = TM_MIX

    def body(dy_ref, y1_ref, lg_ref, lb_ref, dy1_ref, dlg_ref, dlb_ref, dcb_ref):
        @pl.when(pl.program_id(0) == 0)
        def _():
            dlg_ref[...] = jnp.zeros_like(dlg_ref)
            dlb_ref[...] = jnp.zeros_like(dlb_ref)
            dcb_ref[...] = jnp.zeros_like(dcb_ref)
        xhat, rstd = _ln_stats(y1_ref[...])
        lg = lg_ref[...]
        y2 = xhat * lg + lb_ref[...]
        dy2 = dy_ref[...] * _silu_grad(y2, _sigmoid(y2))
        dlg_ref[...] += jnp.sum(dy2 * xhat, axis=0, keepdims=True)
        dlb_ref[...] += jnp.sum(dy2, axis=0, keepdims=True)
        dy1 = _ln_bwd(dy2 * lg, xhat, rstd)
        dy1_ref[...] = dy1
        dcb_ref[...] += jnp.sum(dy1, axis=0, keepdims=True)

    row = pl.BlockSpec((tm, e), lambda i: (i, 0))
    return pl.pallas_call(
        body, name="conv_ln_bwd", grid=(t // tm,),
        in_specs=[row, row, _whole((1, e)), _whole((1, e))],
        out_specs=[row, _acc_out((1, e)), _acc_out((1, e)), _acc_out((1, e))],
        out_shape=[jax.ShapeDtypeStruct((t, e), F32)] + [jax.ShapeDtypeStruct((1, e), F32)] * 3,
        compiler_params=_params(1, 48),
    )(dy, y1, lg, lb)


def _conv_bwd(dy1, proj, dz, cw, seq):
    t, e = dy1.shape
    tm = TM_MIX
    nt = seq // tm
    hb = tm // HALO
    n_halo_blocks = t // HALO

    def body(d_ref, dn_ref, a_ref, b_ref, ah_ref, bh_ref, dz_ref, cw_ref, dproj_ref, dcw_ref,
             y0s, d1s, zs, dsh, dcw8, w8):
        i = pl.program_id(0)
        pos = lax.rem(i, nt)

        @pl.when(i == 0)
        def _():
            dcw8[...] = jnp.zeros_like(dcw8)
        _conv_weights_to_sublanes(cw_ref, w8)
        a = a_ref[...]
        sb = _sigmoid(b_ref[...])
        y0s[pl.ds(HALO, tm), :] = a * sb
        d1s[pl.ds(0, tm), :] = d_ref[...]

        @pl.when(pos == 0)
        def _():
            y0s[pl.ds(0, HALO), :] = jnp.zeros((HALO, e), F32)

        @pl.when(pos != 0)
        def _():
            y0s[pl.ds(0, HALO), :] = ah_ref[...] * _sigmoid(bh_ref[...])

        @pl.when(pos == nt - 1)
        def _():
            d1s[pl.ds(tm, HALO), :] = jnp.zeros((HALO, e), F32)

        @pl.when(pos != nt - 1)
        def _():
            d1s[pl.ds(tm, HALO), :] = dn_ref[...]

        base = HALO - (CONV_K - 1)
        for c0 in range(0, e, DCW_CC):
            cols = slice(c0, c0 + DCW_CC)
            dcur = d_ref[:, cols]
            for s in range(1, 8):
                dsh[s - 1, pl.ds(0, 8), :] = jnp.zeros((8, DCW_CC), F32)
                dsh[s - 1, pl.ds(tm, 8), :] = jnp.zeros((8, DCW_CC), F32)
                dsh[s - 1, pl.ds(s, tm), :] = dcur
            for s in range(8):
                taps = [k for k in range(CONV_K) if (base + k) % 8 == s]
                off0 = base + taps[0] - s
                n, ch = (tm, DCW_RC) if s == 0 else (tm + 8, (tm + 8) // DCW_CHUNKS)
                sums = [None] * len(taps)
                for r in range(0, n, ch):
                    dch = d_ref[r:r + ch, cols] if s == 0 else dsh[s - 1, r:r + ch, :]
                    window = y0s[pl.ds(off0 + r, ch + 8 * (len(taps) - 1)), cols]
                    for m in range(len(taps)):
                        part = jnp.sum((dch * window[8 * m:8 * m + ch]).reshape(ch // 8, 8, DCW_CC), axis=0)
                        sums[m] = part if sums[m] is None else sums[m] + part
                for m, k in enumerate(taps):
                    dcw8[k, :, cols] += sums[m]

        def emit(r0, c0, dy0):
            rs, cs = slice(r0, r0 + CONV_RC), slice(c0, c0 + CONV_CC)
            sbv = _sigmoid(b_ref[rs, cs])
            av = a_ref[rs, cs]
            dproj_ref[rs, c0:c0 + CONV_CC] = (dy0 * sbv).astype(dproj_ref.dtype)
            dproj_ref[rs, e + c0:e + c0 + CONV_CC] = (dy0 * av * sbv * (1.0 - sbv)).astype(dproj_ref.dtype)
        _conv_apply(d1s, w8, zs, 0, tm, e, True, emit)
        dproj_ref[:, 2 * e:3 * e] = dz_ref[...]

        @pl.when(i == t // tm - 1)
        def _():
            dcw_ref[...] = jnp.sum(dcw8[...], axis=1)

    tile = lambda col: pl.BlockSpec((tm, e), lambda i: (i, col))
    prev = lambda col: pl.BlockSpec((HALO, e), lambda i: (jnp.maximum(i * hb - 1, 0), col))
    nxt = pl.BlockSpec((HALO, e), lambda i: (jnp.minimum((i + 1) * hb, n_halo_blocks - 1), 0))
    return pl.pallas_call(
        body, name="conv_bwd", grid=(t // tm,),
        in_specs=[tile(0), nxt, tile(0), tile(1), prev(0), prev(1), tile(0), _whole(cw.shape)],
        out_specs=[pl.BlockSpec((tm, 3 * e), lambda i: (i, 0)), _acc_out(cw.shape)],
        out_shape=[jax.ShapeDtypeStruct((t, 3 * e), MXU_DTYPE), jax.ShapeDtypeStruct(cw.shape, F32)],
        scratch_shapes=[pltpu.VMEM((tm + HALO, e), F32), pltpu.VMEM((tm + HALO, e), F32),
                        pltpu.VMEM((7, CONV_RC + 8, CONV_CC), F32), pltpu.VMEM((7, tm + 8, DCW_CC), F32),
                        pltpu.VMEM((CONV_K, 8, e), F32), pltpu.VMEM((CONV_K, 8, e), F32)],
        compiler_params=_params(1, 56),
    )(dy1, dy1, proj, proj, proj, proj, dz, cw)


def _sgu_bwd(dy, proj, dz, lg, lb, sw, swt, sbt):
    t, e = dy.shape
    eg = e // GROUPS
    tm = TM_MIX
    nsteps = t // tm

    def body(dy_ref, a_ref, b_ref, dz_ref, lg_ref, lb_ref, sw_ref, swt_ref, sbt_ref,
             dproj_ref, dsw_ref, dsb_ref, dlg_ref, dlb_ref, mixed_s, dv_s, sb_acc):
        i = pl.program_id(0)

        @pl.when(i == 0)
        def _():
            dsw_ref[...] = jnp.zeros_like(dsw_ref)
            sb_acc[...] = jnp.zeros_like(sb_acc)
            dlg_ref[...] = jnp.zeros_like(dlg_ref)
            dlb_ref[...] = jnp.zeros_like(dlb_ref)
        a = a_ref[...]
        b = b_ref[...]
        lg = lg_ref[...]
        ea, eb, u, xhat, rstd, v = _sgu_parts(a, b, lg, lb_ref[...], sw_ref, sbt_ref, mixed_s, tm, e)
        dy = dy_ref[...]
        du = dy * mixed_s[...]
        dmixed = (dy * u).astype(MXU_DTYPE)
        mask = _tril_mask()
        mask_t = (lax.broadcasted_iota(jnp.int32, (CHUNK, CHUNK), 0)
                  <= lax.broadcasted_iota(jnp.int32, (CHUNK, CHUNK), 1))
        ones = jnp.ones((8, eg), MXU_DTYPE)
        for g in range(GROUPS):
            wtt = jnp.where(mask_t, swt_ref[g], 0.0).astype(MXU_DTYPE)
            cols = slice(g * eg, (g + 1) * eg)
            for ch in range(tm // CHUNK):
                rows = slice(ch * CHUNK, (ch + 1) * CHUNK)
                dm = dmixed[rows, cols]
                dv_s[rows, cols] = jnp.dot(wtt, dm, preferred_element_type=F32)
                dsw_ref[g] += _dot_nt(dm, v[rows, cols])
                sb_acc[g] += _dot_nt(ones, dm)
        dv = dv_s[...]
        dlg_ref[...] += jnp.sum(dv * xhat, axis=0, keepdims=True)
        dlb_ref[...] += jnp.sum(dv, axis=0, keepdims=True)
        dv0 = _ln_bwd(dv * lg, xhat, rstd)
        pdf_a = jnp.exp(-0.5 * a * a) * INV_SQRT_2PI
        pdf_b = jnp.exp(-0.5 * b * b) * INV_SQRT_2PI
        dproj_ref[:, 0:e] = (du * (0.5 * (1.0 + ea) + a * pdf_a)).astype(dproj_ref.dtype)
        dproj_ref[:, e:2 * e] = (dv0 * (0.5 * (1.0 + eb) + b * pdf_b)).astype(dproj_ref.dtype)
        dproj_ref[:, 2 * e:3 * e] = dz_ref[...]

        @pl.when(i == nsteps - 1)
        def _():
            for g in range(GROUPS):
                dsw_ref[g] = jnp.where(mask, dsw_ref[g], 0.0)
                dsb_ref[g:g + 1, :] = sb_acc[g, 0:1, :]

    tile = lambda col: pl.BlockSpec((tm, e), lambda i: (i, col))
    return pl.pallas_call(
        body, name="sgu_bwd", grid=(nsteps,),
        in_specs=[tile(0), tile(0), tile(1), tile(0), _whole((1, e)), _whole((1, e)), _whole(sw.shape),
                  _whole(swt.shape), _whole(sbt.shape)],
        out_specs=[pl.BlockSpec((tm, 3 * e), lambda i: (i, 0)), _acc_out(sw.shape), _acc_out((GROUPS, CHUNK)),
                   _acc_out((1, e)), _acc_out((1, e))],
        out_shape=[jax.ShapeDtypeStruct((t, 3 * e), MXU_DTYPE), jax.ShapeDtypeStruct(sw.shape, F32),
                   jax.ShapeDtypeStruct((GROUPS, CHUNK), F32), jax.ShapeDtypeStruct((1, e), F32),
                   jax.ShapeDtypeStruct((1, e), F32)],
        scratch_shapes=[pltpu.VMEM((tm, e), F32), pltpu.VMEM((tm, e), F32), pltpu.VMEM((GROUPS, 8, CHUNK), F32)],
        compiler_params=_params(1, 56),
    )(dy, proj, proj, dz, lg, lb, sw, swt, sbt)


def _inproj_bwd_x(dproj, w_blk, dx1, x, g):
    t, d = x.shape
    nb, _, bn = w_blk.shape
    tm = TM_IN

    def body(dp_ref, w_ref, dx1_ref, x_ref, g_ref, dx_ref, dg_ref):
        @pl.when(pl.program_id(0) == 0)
        def _():
            dg_ref[...] = jnp.zeros_like(dg_ref)
        dh = None
        for j in range(nb):
            term = _dot_nt(dp_ref[:, j * bn:(j + 1) * bn], w_ref[j])
            dh = term if dh is None else dh + term
        xv = x_ref[...]
        dx, dg = _rms_bwd(dh, xv, _rms_rstd(xv), g_ref[...])
        dx_ref[...] = dx1_ref[...] + dx
        dg_ref[...] += dg

    row = lambda w: pl.BlockSpec((tm, w), lambda i: (i, 0))
    return pl.pallas_call(
        body, name="inproj_bwd_x", grid=(t // tm,),
        in_specs=[row(nb * bn), _whole(w_blk.shape), row(d), row(d), _whole((1, d))],
        out_specs=[row(d), _acc_out((1, d))],
        out_shape=[jax.ShapeDtypeStruct((t, d), F32), jax.ShapeDtypeStruct((1, d), F32)],
        compiler_params=_params(1, 56),
    )(dproj, w_blk, dx1, x, g)


def _inproj_bwd_w(h, dproj):
    t, d = h.shape
    bn = dproj.shape[1] // N_DEV
    tm = TM_IN
    nsteps = t // tm

    nh = 2
    per = N_DEV // nh

    def body(h_ref, dp_ref, dw_ref, acc):
        i = pl.program_id(1)

        @pl.when(i == 0)
        def _():
            acc[...] = jnp.zeros_like(acc)
        hv = h_ref[...]
        for jj in range(per):
            acc[jj] += _dot_tn(hv, dp_ref[:, jj * bn:(jj + 1) * bn])

        @pl.when(i == nsteps - 1)
        def _():
            dw_ref[...] = acc[...].astype(dw_ref.dtype)

    return pl.pallas_call(
        body, name="inproj_bwd_w", grid=(nh, nsteps),
        in_specs=[pl.BlockSpec((tm, d), lambda hh, i: (i, 0)), pl.BlockSpec((tm, per * bn), lambda hh, i: (i, hh))],
        out_specs=[pl.BlockSpec((per, d, bn), lambda hh, i: (hh, 0, 0))],
        out_shape=[jax.ShapeDtypeStruct((N_DEV, d, bn), WIRE_DTYPE)],
        scratch_shapes=[pltpu.VMEM((per, d, bn), F32)],
        compiler_params=_params(2, 56),
    )(h, dproj)[0]


def _adamw(parts, w, m, v):
    nl, r, c = w.shape
    tr = r
    for cand in (512, 256, 128, 64, 32, 16, 8):
        if r % cand == 0 and cand * c * 4 <= (1 << 19):
            tr = cand
            break
    bc1 = 1.0 - ADAM_B1 ** ADAM_STEP
    bc2 = 1.0 - ADAM_B2 ** ADAM_STEP

    def body(*refs):
        p_refs = refs[:nl]
        w_ref, m_ref, v_ref, g_ref, d_ref, nm_ref, nv_ref = refs[nl:]

        def update(p_ref):
            g = p_ref[0].astype(F32)
            for s in range(1, N_DEV):
                g = g + p_ref[s].astype(F32)
            nm = ADAM_B1 * m_ref[0] + (1.0 - ADAM_B1) * g
            nv = ADAM_B2 * v_ref[0] + (1.0 - ADAM_B2) * (g * g)
            g_ref[0] = g
            nm_ref[0] = nm
            nv_ref[0] = nv
            d_ref[0] = -ADAM_LR * ((nm / bc1) / (jnp.sqrt(nv / bc2) + ADAM_EPS) + ADAM_WD * w_ref[0])

        if nl == 1:
            update(p_refs[0])
        else:
            for kk in range(nl):
                pl.when(pl.program_id(0) == kk)(lambda kk=kk: update(p_refs[kk]))

    part_spec = lambda kk: pl.BlockSpec((N_DEV, tr, c), lambda l, i: (0, jnp.where(l == kk, i, 0), 0))
    row = pl.BlockSpec((1, tr, c), lambda l, i: (l, i, 0))
    return pl.pallas_call(
        body, name="adamw", grid=(nl, r // tr),
        in_specs=[part_spec(kk) for kk in range(nl)] + [row, row, row],
        out_specs=[row] * 4,
        out_shape=[jax.ShapeDtypeStruct((nl, r, c), F32)] * 4,
        compiler_params=_params(2, 48),
    )(*parts, w, m, v)


def _pack(arrays):
    flat = jnp.concatenate([a.reshape(-1).astype(F32) for a in arrays])
    unit = 8 * PACK_COLS
    padded = -(-flat.shape[0] // unit) * unit
    return jnp.pad(flat, (0, padded - flat.shape[0])).reshape(-1, PACK_COLS)


def _unpack(packed, shapes):
    flat = packed.reshape(-1)
    out, off = [], 0
    for s in shapes:
        n = math.prod(s)
        out.append(flat[off:off + n].reshape(s))
        off += n
    return out


def kernel(x, p, norm_g, w_in, w_out, conv_w, conv_b, conv_ln_g, conv_ln_b, sgu_ln_g, sgu_ln_b, sgu_w, sgu_b, pl_norm_g, pl_gate_w, pl_proj_w, final_g, loss_target, m_norm_g, m_w_in, m_w_out, m_conv_w, m_conv_b, m_conv_ln_g, m_conv_ln_b, m_sgu_ln_g, m_sgu_ln_b, m_sgu_w, m_sgu_b, m_pl_norm_g, m_pl_gate_w, m_pl_proj_w, m_final_g, v_norm_g, v_w_in, v_w_out, v_conv_w, v_conv_b, v_conv_ln_g, v_conv_ln_b, v_sgu_ln_g, v_sgu_ln_b, v_sgu_w, v_sgu_b, v_pl_norm_g, v_pl_gate_w, v_pl_proj_w, v_final_g):
    bsz, seq, d = x.shape
    t = bsz * seq
    depth = w_in.shape[0]
    e = w_out.shape[1] * N_DEV
    pd = p.shape[-1]
    n_conv, n_sgu = conv_w.shape[0], sgu_ln_g.shape[0]
    eg = e // N_DEV

    small_shapes = [conv_w.shape, sgu_ln_g.shape, sgu_ln_b.shape]
    cast = lambda a: a.astype(MXU_DTYPE)
    first = [cast(w_in[0]), cast(w_out[0]), cast(pl_gate_w[0]), cast(pl_proj_w), _pack([conv_w, sgu_ln_g, sgu_ln_b])]
    later = [[cast(w_in[l]), cast(w_out[l]), cast(pl_gate_w[l])] for l in range(1, depth)]
    gathered = _all_gather(first, "gather_weights")
    gather_pending, gather_tokens = {}, 0.0
    for l in range(1, depth):
        lands = _place_own(later[l - 1], False, "place_own_weights")
        send, recv, srcs, lnds, token = _exchange_start(later[l - 1], lands, False, f"gather_start_{l}")
        gather_pending[l] = (send, recv, srcs, lnds)
        gather_tokens = gather_tokens + token[0, 0]
    w_in_g = {0: gathered[0]}
    w_out_g = {0: gathered[1].reshape(e, d)}
    gate_g = {0: gathered[2].reshape(d, d)}
    proj_g = jnp.transpose(gathered[3], (1, 2, 0, 3)).reshape(depth, pd, d)
    small_g = [_unpack(gathered[4][s], small_shapes) for s in range(N_DEV)]
    conv_w_g = jnp.concatenate([sg[0] for sg in small_g], axis=-1)
    sgu_ln_g_g = jnp.concatenate([sg[1] for sg in small_g], axis=-1)
    sgu_ln_b_g = jnp.concatenate([sg[2] for sg in small_g], axis=-1)
    sgu_wt = jnp.swapaxes(sgu_w, -1, -2)
    sgu_bt = jnp.swapaxes(sgu_b, -1, -2)

    xs = [x.reshape(t, d)]
    saved = []
    for l in range(depth):
        j = l // 2
        if l == 0:
            g_l = norm_g[0:1] + gather_tokens
        else:
            g_l = norm_g[l:l + 1]
            got = _exchange_wait(*gather_pending.pop(l), xs[-1], False, f"gather_wait_{l}")
            w_in_g[l], w_out_g[l], gate_g[l] = got[0], got[1].reshape(e, d), got[2].reshape(d, d)
        p_l = p[l].reshape(t, pd)
        common = (xs[-1], p_l, g_l, w_in_g[l], w_out_g[l], pl_norm_g[l:l + 1], gate_g[l], proj_g[l], seq)
        if l % 2 == 0:
            h, proj, y1, y, x1, x2 = _layer_fwd(
                *common, conv=(conv_w_g[j], conv_b[j:j + 1], conv_ln_g[j:j + 1], conv_ln_b[j:j + 1]))
        else:
            y1 = None
            h, proj, y, x1, x2 = _layer_fwd(
                *common, sgu=(sgu_ln_g_g[j:j + 1], sgu_ln_b_g[j:j + 1], sgu_w[j], sgu_bt[j]))
        saved.append((h, proj, y1, y, x1, p_l))
        xs.append(x2)

    loss_part, dx, d_final_g = _loss_head(xs[-1], final_g.reshape(1, d), loss_target.reshape(t, d))
    loss = lax.psum(loss_part[0, 0], ("x", "y", "c"))

    d_norm_g, d_pl_norm_g = [None] * depth, [None] * depth
    scatter_pending = {}
    d_conv_w, d_conv_b, d_conv_ln_g, d_conv_ln_b = [None] * n_conv, [None] * n_conv, [None] * n_conv, [None] * n_conv
    d_sgu_ln_g, d_sgu_ln_b, d_sgu_w, d_sgu_b = [None] * n_sgu, [None] * n_sgu, [None] * n_sgu, [None] * n_sgu
    for l in reversed(range(depth)):
        j = l // 2
        h, proj, y1, y, x1, p_l = saved[l]
        dx1, dgate_p, dprojw_p, d_pl_norm_g[l] = _ple_bwd(dx, x1, p_l, pl_norm_g[l:l + 1], gate_g[l], proj_g[l])
        dy, dz, dw_out_p = _outproj_bwd(dx1, y, proj, w_out_g[l])
        if l % 2 == 0:
            dy1, d_conv_ln_g[j], d_conv_ln_b[j], d_conv_b[j] = _conv_ln_bwd(dy, y1, conv_ln_g[j:j + 1], conv_ln_b[j:j + 1])
            dproj, d_conv_w[j] = _conv_bwd(dy1, proj, dz, conv_w_g[j], seq)
        else:
            dproj, d_sgu_w[j], d_sgu_b[j], d_sgu_ln_g[j], d_sgu_ln_b[j] = _sgu_bwd(
                dy, proj, dz, sgu_ln_g_g[j:j + 1], sgu_ln_b_g[j:j + 1], sgu_w[j], sgu_wt[j], sgu_bt[j])
        parts = [_inproj_bwd_w(h, dproj), dw_out_p.reshape(N_DEV, e // N_DEV, d),
                 dgate_p.reshape(N_DEV, d // N_DEV, d), dprojw_p]
        send, recv, srcs, lnds, token = _exchange_start(parts, _place_own(parts, True, "place_own_grads"), True,
                                                        f"scatter_start_{l}")
        scatter_pending[l] = (send, recv, srcs, lnds)
        dx, d_norm_g[l] = _inproj_bwd_x(dproj, w_in_g[l], dx1, xs[l], norm_g[l:l + 1] + token[0, 0])
    grad_x = dx.reshape(bsz, seq, d)

    d_conv_w_full = jnp.stack(d_conv_w)
    d_sgu_ln_g_full = jnp.concatenate(d_sgu_ln_g, axis=0)
    d_sgu_ln_b_full = jnp.concatenate(d_sgu_ln_b, axis=0)
    small_part = jnp.stack([
        _pack([d_conv_w_full[..., s * eg:(s + 1) * eg], d_sgu_ln_g_full[:, s * eg:(s + 1) * eg],
               d_sgu_ln_b_full[:, s * eg:(s + 1) * eg]]) for s in range(N_DEV)])
    r_small = _all_to_all([[small_part]], "scatter_small_grads")[0]
    rep_part = _pack([jnp.concatenate(d_norm_g, axis=0), jnp.concatenate(d_conv_b, axis=0),
                      jnp.concatenate(d_conv_ln_g, axis=0), jnp.concatenate(d_conv_ln_b, axis=0),
                      jnp.stack(d_sgu_w), jnp.stack(d_sgu_b), jnp.concatenate(d_pl_norm_g, axis=0), d_final_g])
    r_rep = _all_gather([rep_part], "gather_replicated_grads")[0]

    landed = {}
    for l in reversed(range(depth)):
        landed[l] = _exchange_wait(*scatter_pending.pop(l), r_rep, True, f"scatter_wait_{l}")

    o_w_in = _adamw([landed[l][0] for l in range(depth)], w_in, m_w_in, v_w_in)
    o_w_out = _adamw([landed[l][1] for l in range(depth)], w_out, m_w_out, v_w_out)
    o_gate = _adamw([landed[l][2] for l in range(depth)], pl_gate_w, m_pl_gate_w, v_pl_gate_w)
    o_projw = _adamw([landed[l][3] for l in range(depth)], pl_proj_w, m_pl_proj_w, v_pl_proj_w)

    def packed(parts, ws, ms, vs):
        shapes = [a.shape for a in ws]
        outs = _adamw([parts.reshape(N_DEV, -1, PACK_COLS)], _pack(ws)[None], _pack(ms)[None], _pack(vs)[None])
        return [_unpack(o[0], shapes) for o in outs]

    o_small = packed(r_small, [conv_w, sgu_ln_g, sgu_ln_b], [m_conv_w, m_sgu_ln_g, m_sgu_ln_b],
                     [v_conv_w, v_sgu_ln_g, v_sgu_ln_b])
    o_rep = packed(r_rep, [norm_g, conv_b, conv_ln_g, conv_ln_b, sgu_w, sgu_b, pl_norm_g, final_g],
                   [m_norm_g, m_conv_b, m_conv_ln_g, m_conv_ln_b, m_sgu_w, m_sgu_b, m_pl_norm_g, m_final_g],
                   [v_norm_g, v_conv_b, v_conv_ln_g, v_conv_ln_b, v_sgu_w, v_sgu_b, v_pl_norm_g, v_final_g])

    def leaf(kind):
        rep, small = o_rep[kind], o_small[kind]
        return [rep[0], o_w_in[kind], o_w_out[kind], small[0], rep[1], rep[2], rep[3], small[1], small[2], rep[4],
                rep[5], rep[6], o_gate[kind], o_projw[kind], rep[7]]

    return (loss, grad_x, *leaf(0), *leaf(1), *leaf(2), *leaf(3))
```

```python
import math

import jax
import jax.numpy as jnp
from jax import lax
from jax.experimental import pallas as pl
from jax.experimental.pallas import tpu as pltpu

F32 = jnp.float32
MXU_DTYPE = jnp.bfloat16
WIRE_DTYPE = jnp.bfloat16

EPS = 1e-6
CONV_K = 31
CHUNK = 128
GROUPS = 8
HALO = 32
N_DEV = 8
DEPTH = 4

ADAM_LR = 0.001
ADAM_B1 = 0.9
ADAM_B2 = 0.999
ADAM_EPS = 1e-08
ADAM_WD = 0.01
ADAM_STEP = 10

TM_IN = 512
TM_MIX = 256
TM_OUT = 512
FUSE_SB = 128
CONV_RC = 64
CONV_CC = 128
DCW_CC = 256
DCW_RC = 64
DCW_CHUNKS = 3
PACK_COLS = 1024

MESH_ID = pl.DeviceIdType.MESH
INV_SQRT2 = 1.0 / math.sqrt(2.0)
INV_SQRT_2PI = 1.0 / math.sqrt(2.0 * math.pi)


def _params(n_grid, vmem_mb):
    return pltpu.CompilerParams(dimension_semantics=("arbitrary",) * n_grid, vmem_limit_bytes=vmem_mb << 20)


def _whole(shape):
    nd = len(shape)
    return pl.BlockSpec(shape, lambda *_: (0,) * nd, pipeline_mode=pl.Buffered(1))


def _acc_out(shape):
    nd = len(shape)
    return pl.BlockSpec(shape, lambda *_: (0,) * nd)


def _dot(a, b):
    return jnp.dot(a.astype(MXU_DTYPE), b.astype(MXU_DTYPE), preferred_element_type=F32)


def _dot_nt(a, b):
    return lax.dot_general(a.astype(MXU_DTYPE), b.astype(MXU_DTYPE), (((1,), (1,)), ((), ())),
                           preferred_element_type=F32)


def _dot_tn(a, b):
    return lax.dot_general(a.astype(MXU_DTYPE), b.astype(MXU_DTYPE), (((0,), (0,)), ((), ())),
                           preferred_element_type=F32)


def _sigmoid(x):
    return jax.nn.sigmoid(x)


def _rms_rstd(x):
    return lax.rsqrt(jnp.mean(x * x, axis=-1, keepdims=True) + EPS)


def _rms_bwd(dy, x, rstd, g):
    gy = dy * g
    xr = x * rstd
    dx = rstd * (gy - xr * jnp.mean(gy * xr, axis=-1, keepdims=True))
    dg = jnp.sum(dy * xr, axis=0, keepdims=True)
    return dx, dg


def _ln_stats(x):
    mu = jnp.mean(x, axis=-1, keepdims=True)
    xc = x - mu
    var = jnp.mean(xc * xc, axis=-1, keepdims=True)
    rstd = lax.rsqrt(var + EPS)
    return xc * rstd, rstd


def _ln_bwd(dxhat, xhat, rstd):
    return rstd * (dxhat - jnp.mean(dxhat, axis=-1, keepdims=True)
                   - xhat * jnp.mean(dxhat * xhat, axis=-1, keepdims=True))


def _silu_grad(x, s):
    return s * (1.0 + x * (1.0 - s))


def _tril_mask():
    r = lax.broadcasted_iota(jnp.int32, (CHUNK, CHUNK), 0)
    c = lax.broadcasted_iota(jnp.int32, (CHUNK, CHUNK), 1)
    return r >= c


def _conv_weights_to_sublanes(w_ref, w8_ref):
    for k in range(CONV_K):
        w8_ref[k] = jnp.broadcast_to(w_ref[k:k + 1, :], w8_ref.shape[1:])


def _conv_apply(src_ref, w8_ref, zs_ref, base, tm, e, flip, emit):
    def row_block(i, carry):
        r0 = pl.multiple_of(i * CONV_RC, CONV_RC)
        for c0 in range(0, e, CONV_CC):
            cols = slice(c0, c0 + CONV_CC)
            acc = None
            for s in range(8):
                nrows = CONV_RC if s == 0 else CONV_RC + 8
                taps = [k for k in range(CONV_K) if (base + k) % 8 == s]
                off0 = base + taps[0] - s
                span = nrows + 8 * (len(taps) - 1)
                window = src_ref[pl.ds(r0 + off0, span), cols].reshape(span // 8, 8, CONV_CC)
                z = None
                for m, k in enumerate(taps):
                    wk = (CONV_K - 1 - k) if flip else k
                    term = w8_ref[wk, :, cols][None] * window[m:m + nrows // 8]
                    z = term if z is None else z + term
                z = z.reshape(nrows, CONV_CC)
                if s == 0:
                    acc = z
                else:
                    zs_ref[s - 1, pl.ds(0, nrows), :] = z
                    acc = acc + zs_ref[s - 1, pl.ds(s, CONV_RC), :]
            emit(r0, c0, acc)
        return carry

    lax.fori_loop(0, tm // CONV_RC, row_block, 0)


def _mesh_pos():
    return lax.axis_index("x"), lax.axis_index("y"), lax.axis_index("c")


def _slot(px, py, pc):
    return 4 * px + 2 * py + pc


def _peers(x, y, c):
    return [((1 - x) if (k & 4) else x, (1 - y) if (k & 2) else y, (1 - c) if (k & 1) else c)
            for k in range(1, N_DEV)]


HBM_SPEC = pl.BlockSpec(memory_space=pltpu.HBM)
SEM_SPEC = pl.BlockSpec(memory_space=pltpu.SEMAPHORE)
SIDE_EFFECT = pltpu.SideEffectType.DATAFLOW_SIDE_EFFECTING


def _exchange_copy(src_refs, land_refs, send_sems, recv_sems, i, k, peer, scatter, me):
    slot = _slot(*peer)
    return pltpu.make_async_remote_copy(
        src_ref=src_refs[i].at[slot] if scatter else src_refs[i],
        dst_ref=land_refs[i].at[me if me is not None else slot],
        send_sem=send_sems.at[i * 7 + k], recv_sem=recv_sems.at[i * 7 + k],
        device_id=peer, device_id_type=MESH_ID)


def _exchange_start(srcs, lands, scatter, name):
    n = len(srcs)

    def body(*refs):
        src_refs, land_refs = refs[:n], refs[n:2 * n]
        send_sems, recv_sems, token = refs[2 * n], refs[2 * n + 1], refs[-1]
        x, y, c = _mesh_pos()
        me = _slot(x, y, c)
        for i in range(n):
            for k, peer in enumerate(_peers(x, y, c)):
                _exchange_copy(src_refs, land_refs, send_sems, recv_sems, i, k, peer, scatter, me).start()
        token[...] = jnp.zeros_like(token)

    arrays = list(srcs) + list(lands)
    outs = pl.pallas_call(
        body, name=name,
        out_shape=(pltpu.SemaphoreType.DMA((7 * n,)), pltpu.SemaphoreType.DMA((7 * n,)),
                   *[pltpu.HBM(a.shape, a.dtype) for a in arrays], jax.ShapeDtypeStruct((8, 128), F32)),
        in_specs=[HBM_SPEC] * (2 * n),
        out_specs=(SEM_SPEC, SEM_SPEC, *[HBM_SPEC] * (2 * n), pl.BlockSpec(memory_space=pltpu.VMEM)),
        input_output_aliases={i: 2 + i for i in range(2 * n)},
        compiler_params=pltpu.CompilerParams(has_side_effects=SIDE_EFFECT),
    )(*[pltpu.with_memory_space_constraint(a, pltpu.HBM) for a in arrays])
    return outs[0], outs[1], list(outs[2:2 + n]), list(outs[2 + n:2 + 2 * n]), outs[-1]


def _exchange_wait(send_sems, recv_sems, srcs, lands, after, scatter, name):
    n = len(srcs)

    def body(*refs):
        src_refs, land_refs = refs[:n], refs[n:2 * n]
        send, recv = refs[2 * n], refs[2 * n + 1]
        x, y, c = _mesh_pos()
        for i in range(n):
            for k, peer in enumerate(_peers(x, y, c)):
                cp = _exchange_copy(src_refs, land_refs, send, recv, i, k, peer, scatter, None)
                cp.wait_send()
                cp.wait_recv()

    arrays = list(srcs) + list(lands)
    outs = pl.pallas_call(
        body, name=name,
        out_shape=tuple(pltpu.HBM(a.shape, a.dtype) for a in arrays),
        in_specs=[HBM_SPEC] * (2 * n) + [SEM_SPEC, SEM_SPEC, pl.BlockSpec(memory_space=pl.ANY)],
        out_specs=tuple([HBM_SPEC] * (2 * n)),
        input_output_aliases={i: i for i in range(2 * n)},
        compiler_params=pltpu.CompilerParams(has_side_effects=SIDE_EFFECT),
    )(*arrays, send_sems, recv_sems, after)
    return list(outs[n:])


def _place_own(parts, scatter, name):
    n = len(parts)
    me = jnp.reshape(_slot(*_mesh_pos()), (1,)).astype(jnp.int32)

    def body(me_ref, *refs):
        for i in range(n):
            refs[n + i][0] = refs[i][0] if scatter else refs[i][...]

    def slot_spec(shape):
        rest = len(shape)
        return pl.BlockSpec((1,) + tuple(shape), lambda i, me_ref: (me_ref[0],) + (0,) * rest)

    def whole_spec(shape):
        nd = len(shape)
        return pl.BlockSpec(tuple(shape), lambda i, me_ref: (0,) * nd)

    blocks = [a.shape[1:] if scatter else a.shape for a in parts]
    return pl.pallas_call(
        body, name=name,
        grid_spec=pltpu.PrefetchScalarGridSpec(
            num_scalar_prefetch=1, grid=(1,),
            in_specs=[slot_spec(b) if scatter else whole_spec(b) for b in blocks],
            out_specs=[slot_spec(b) for b in blocks]),
        out_shape=[jax.ShapeDtypeStruct((N_DEV,) + tuple(b), a.dtype) for a, b in zip(parts, blocks)],
        compiler_params=_params(1, 32),
    )(me, *parts)


def _all_gather(items, name):
    n = len(items)

    def body(*refs):
        in_refs, out_refs = refs[:n], refs[n:2 * n]
        send_sems, recv_sems, local_sems = refs[2 * n:]
        x, y, c = _mesh_pos()
        me, sibling = (x, y, c), (x, y, 1 - c)
        chips = [(1 - x, y), (x, 1 - y), (1 - x, 1 - y)]

        def copy(i, k, block, to, src=None):
            dst = out_refs[i].at[_slot(*block)]
            return pltpu.make_async_remote_copy(
                src_ref=dst if src is None else src, dst_ref=dst,
                send_sem=send_sems.at[i * 7 + k], recv_sem=recv_sems.at[i * 7 + k],
                device_id=to, device_id_type=MESH_ID)

        mine = [pltpu.make_async_copy(in_refs[i], out_refs[i].at[_slot(*me)], local_sems.at[i]) for i in range(n)]
        for cp in mine:
            cp.start()
        first = []
        for i in range(n):
            first.append(copy(i, 0, me, sibling, src=in_refs[i]))
            for j, chip in enumerate(chips):
                first.append(copy(i, 1 + j, me, (*chip, c), src=in_refs[i]))
        for cp in first:
            cp.start()
        passed = []
        for j, chip in enumerate(chips):
            for i in range(n):
                copy(i, 1 + j, (*chip, c), me).wait_recv()
                fwd = copy(i, 4 + j, (*chip, c), sibling)
                fwd.start()
                passed.append(fwd)
        for i in range(n):
            copy(i, 0, sibling, me).wait_recv()
            for j, chip in enumerate(chips):
                copy(i, 4 + j, (*chip, 1 - c), me).wait_recv()
        for cp in first + passed:
            cp.wait_send()
        for cp in mine:
            cp.wait()

    any_spec = pl.BlockSpec(memory_space=pl.ANY)
    return pl.pallas_call(
        body, name=name,
        out_shape=[jax.ShapeDtypeStruct((N_DEV,) + a.shape, a.dtype) for a in items],
        in_specs=[any_spec] * n, out_specs=[any_spec] * n,
        scratch_shapes=[pltpu.SemaphoreType.DMA((7 * n,)), pltpu.SemaphoreType.DMA((7 * n,)),
                        pltpu.SemaphoreType.DMA((n,))],
    )(*items)


def _layer_fwd(x, p_all, layer, g, w_blk, w_out, plg, gate_w, proj_w, seq, conv=None, sgu=None):
    t, d = x.shape
    nb, _, bn = w_blk.shape
    e = w_out.shape[0]
    pd = p_all.shape[-1]
    tm = TM_MIX
    nt = seq // tm
    is_conv = conv is not None
    mixer_args = conv if is_conv else sgu
    n_mix = len(mixer_args)

    def body(*refs):
        x_ref, p_ref, g_ref, w_ref, wo_ref, plg_ref, gw_ref, pw_ref = refs[:8]
        mix = refs[8:8 + n_mix]
        outs = refs[8 + n_mix:]
        if is_conv:
            cw_ref, cb_ref, lg_ref, lb_ref = mix
            h_ref, proj_ref, y1_ref, y_ref, x1_ref, x2_ref, y0s, zs, w8 = outs

            @pl.when(lax.rem(pl.program_id(0), nt) == 0)
            def _():
                y0s[pl.ds(0, HALO), :] = jnp.zeros((HALO, e), F32)
            _conv_weights_to_sublanes(cw_ref, w8)
        else:
            lg_ref, lb_ref, sw_ref, sbt_ref = mix
            h_ref, proj_ref, y_ref, x1_ref, x2_ref, mixed_s = outs

        for sb in range(tm // FUSE_SB):
            rows = pl.ds(sb * FUSE_SB, FUSE_SB)
            xv = x_ref[rows, :]
            hv = (xv * _rms_rstd(xv) * g_ref[...]).astype(MXU_DTYPE)
            h_ref[rows, :] = hv
            for j in range(nb):
                proj_ref[rows, j * bn:(j + 1) * bn] = jnp.dot(hv, w_ref[j], preferred_element_type=F32)
            if is_conv:
                y0s[pl.ds(HALO + sb * FUSE_SB, FUSE_SB), :] = proj_ref[rows, 0:e] * _sigmoid(proj_ref[rows, e:2 * e])

        if is_conv:
            def emit(r0, c0, acc):
                y1_ref[pl.ds(r0, CONV_RC), c0:c0 + CONV_CC] = acc + cb_ref[:, c0:c0 + CONV_CC]
            _conv_apply(y0s, w8, zs, HALO - (CONV_K - 1), tm, e, False, emit)

        for sb in range(tm // FUSE_SB):
            rows = pl.ds(sb * FUSE_SB, FUSE_SB)
            if is_conv:
                xhat, _ = _ln_stats(y1_ref[rows, :])
                y2 = xhat * lg_ref[...] + lb_ref[...]
                y = y2 * _sigmoid(y2)
            else:
                _, _, u, _, _, _ = _sgu_parts(proj_ref[rows, 0:e], proj_ref[rows, e:2 * e], lg_ref[...], lb_ref[...],
                                              sw_ref, sbt_ref, mixed_s, FUSE_SB, e)
                y = u * mixed_s[...]
            y_ref[rows, :] = y
            z = proj_ref[rows, 2 * e:3 * e]
            q = (y * (z * _sigmoid(z))).astype(MXU_DTYPE)
            x1 = x_ref[rows, :] + jnp.dot(q, wo_ref[...], preferred_element_type=F32)
            x1_ref[rows, :] = x1
            rn = x1 * _rms_rstd(x1) * plg_ref[...]
            gate = _sigmoid(_dot(rn, gw_ref[...]))
            x2_ref[rows, :] = x1 + gate * _dot(p_ref[0, rows, :], pw_ref[...])

        if is_conv:
            y0s[pl.ds(0, HALO), :] = y0s[pl.ds(tm, HALO), :]

    row = lambda w: pl.BlockSpec((tm, w), lambda i: (i, 0))
    f32 = lambda w: jax.ShapeDtypeStruct((t, w), F32)
    out_shape = [jax.ShapeDtypeStruct((t, d), MXU_DTYPE), f32(3 * e)] + ([f32(e)] if is_conv else []) + [f32(e), f32(d), f32(d)]
    out_specs = [row(d), row(3 * e)] + ([row(e)] if is_conv else []) + [row(e), row(d), row(d)]
    scratch = ([pltpu.VMEM((tm + HALO, e), F32), pltpu.VMEM((7, CONV_RC + 8, CONV_CC), F32),
                pltpu.VMEM((CONV_K, 8, e), F32)] if is_conv else [pltpu.VMEM((FUSE_SB, e), F32)])
    return pl.pallas_call(
        body, name="layer_fwd_conv" if is_conv else "layer_fwd_sgu", grid=(t // tm,),
        in_specs=[row(d), pl.BlockSpec((1, tm, pd), lambda i: (layer, i, 0)), _whole((1, d)), _whole(w_blk.shape),
                  _whole((e, d)), _whole((1, d)),
                  _whole((d, d)), _whole((pd, d))] + [_whole(a.shape) for a in mixer_args],
        out_specs=out_specs, out_shape=out_shape, scratch_shapes=scratch,
        compiler_params=_params(1, 60),
    )(x, p_all, g, w_blk, w_out, plg, gate_w, proj_w, *mixer_args)


def _sgu_parts(a, b, lg, lb, sw_ref, sbt_ref, mixed_s, tm, e):
    eg = e // GROUPS
    ea = lax.erf(a * INV_SQRT2)
    eb = lax.erf(b * INV_SQRT2)
    u = 0.5 * a * (1.0 + ea)
    v0 = 0.5 * b * (1.0 + eb)
    xhat, rstd = _ln_stats(v0)
    v = (xhat * lg + lb).astype(MXU_DTYPE)
    mask = _tril_mask()
    for g in range(GROUPS):
        wt = jnp.where(mask, sw_ref[g], 0.0).astype(MXU_DTYPE)
        bcol = sbt_ref[:, g:g + 1]
        for ch in range(tm // CHUNK):
            rows = slice(ch * CHUNK, (ch + 1) * CHUNK)
            cols = slice(g * eg, (g + 1) * eg)
            mixed_s[rows, cols] = jnp.dot(wt, v[rows, cols], preferred_element_type=F32) + bcol
    return ea, eb, u, xhat, rstd, v


def _loss_head(xf, fg, tgt):
    t, d = xf.shape
    tm = TM_OUT
    nsteps = t // tm

    def body(x_ref, g_ref, t_ref, loss_ref, dx_ref, dg_ref, sq_s):
        i = pl.program_id(0)

        @pl.when(i == 0)
        def _():
            sq_s[...] = jnp.zeros_like(sq_s)
            dg_ref[...] = jnp.zeros_like(dg_ref)
        x = x_ref[...]
        rstd = _rms_rstd(x)
        err = x * rstd * g_ref[...] - t_ref[...]
        sq_s[...] += jnp.sum(err * err, axis=0, keepdims=True)
        dx, dg = _rms_bwd(err * (1.0 / d), x, rstd, g_ref[...])
        dx_ref[...] = dx
        dg_ref[...] += dg

        @pl.when(i == nsteps - 1)
        def _():
            loss_ref[...] = jnp.sum(sq_s[...], axis=1, keepdims=True) * (0.5 / d)

    row = pl.BlockSpec((tm, d), lambda i: (i, 0))
    return pl.pallas_call(
        body, name="loss_head", grid=(nsteps,),
        in_specs=[row, _whole((1, d)), row],
        out_specs=[_acc_out((1, 1)), row, _acc_out((1, d))],
        out_shape=[jax.ShapeDtypeStruct((1, 1), F32), jax.ShapeDtypeStruct((t, d), F32),
                   jax.ShapeDtypeStruct((1, d), F32)],
        scratch_shapes=[pltpu.VMEM((1, d), F32)],
        compiler_params=_params(1, 32),
    )(xf, fg, tgt)


def _ple_bwd(dx2, x1, p_all, layer, plg, gate_w, proj_w):
    t, d = x1.shape
    pd = p_all.shape[-1]
    tm = TM_OUT
    nsteps = t // tm
    bn = d // N_DEV

    def body(dx2_ref, x1_ref, p_ref, plg_ref, gw_ref, pw_ref, dx1_ref, dgw_ref, dpw_ref, dplg_ref, gw_acc, pw_acc):
        i = pl.program_id(0)

        @pl.when(i == 0)
        def _():
            gw_acc[...] = jnp.zeros_like(gw_acc)
            pw_acc[...] = jnp.zeros_like(pw_acc)
            dplg_ref[...] = jnp.zeros_like(dplg_ref)
        dx2 = dx2_ref[...]
        x1 = x1_ref[...]
        plg = plg_ref[...]
        rstd = _rms_rstd(x1)
        rn = (x1 * rstd * plg).astype(MXU_DTYPE)
        gate = _sigmoid(jnp.dot(rn, gw_ref[...], preferred_element_type=F32))
        p_b = p_ref[0].astype(MXU_DTYPE)
        pp = jnp.dot(p_b, pw_ref[...], preferred_element_type=F32)
        dpp = (dx2 * gate).astype(MXU_DTYPE)
        dgpre = (dx2 * pp * gate * (1.0 - gate)).astype(MXU_DTYPE)
        pw_acc[...] += _dot_tn(p_b, dpp)
        gw_acc[...] += _dot_tn(rn, dgpre)
        drn = _dot_nt(dgpre, gw_ref[...])
        dx, dg = _rms_bwd(drn, x1, rstd, plg)
        dx1_ref[...] = dx2 + dx
        dplg_ref[...] += dg

        @pl.when(i == nsteps - 1)
        def _():
            dgw_ref[...] = gw_acc[...].astype(dgw_ref.dtype)
            for j in range(N_DEV):
                dpw_ref[j] = pw_acc[:, j * bn:(j + 1) * bn].astype(dpw_ref.dtype)

    row = lambda w: pl.BlockSpec((tm, w), lambda i: (i, 0))
    return pl.pallas_call(
        body, name="ple_bwd", grid=(nsteps,),
        in_specs=[row(d), row(d), pl.BlockSpec((1, tm, pd), lambda i: (layer, i, 0)), _whole((1, d)), _whole((d, d)),
                  _whole((pd, d))],
        out_specs=[row(d), _acc_out((d, d)), _acc_out((N_DEV, pd, bn)), _acc_out((1, d))],
        out_shape=[jax.ShapeDtypeStruct((t, d), F32), jax.ShapeDtypeStruct((d, d), WIRE_DTYPE),
                   jax.ShapeDtypeStruct((N_DEV, pd, bn), WIRE_DTYPE), jax.ShapeDtypeStruct((1, d), F32)],
        scratch_shapes=[pltpu.VMEM((d, d), F32), pltpu.VMEM((pd, d), F32)],
        compiler_params=_params(1, 48),
    )(dx2, x1, p_all, plg, gate_w, proj_w)


def _outproj_bwd(dx1, y, proj, w_out):
    t, d = dx1.shape
    e = y.shape[1]
    tm = TM_MIX
    nsteps = t // tm

    def body(dx1_ref, y_ref, z_ref, wo_ref, dy_ref, dz_ref, dwo_ref, wo_acc):
        i = pl.program_id(0)

        @pl.when(i == 0)
        def _():
            wo_acc[...] = jnp.zeros_like(wo_acc)
        dx1 = dx1_ref[...].astype(MXU_DTYPE)
        y = y_ref[...]
        z = z_ref[...]
        s = _sigmoid(z)
        sz = z * s
        q = (y * sz).astype(MXU_DTYPE)
        wo_acc[...] += _dot_tn(q, dx1)
        dq = _dot_nt(dx1, wo_ref[...])
        dy_ref[...] = dq * sz
        dz_ref[...] = (dq * y * _silu_grad(z, s)).astype(dz_ref.dtype)

        @pl.when(i == nsteps - 1)
        def _():
            dwo_ref[...] = wo_acc[...].astype(dwo_ref.dtype)

    return pl.pallas_call(
        body, name="outproj_bwd", grid=(nsteps,),
        in_specs=[pl.BlockSpec((tm, d), lambda i: (i, 0)), pl.BlockSpec((tm, e), lambda i: (i, 0)),
                  pl.BlockSpec((tm, e), lambda i: (i, 2)), _whole((e, d))],
        out_specs=[pl.BlockSpec((tm, e), lambda i: (i, 0)), pl.BlockSpec((tm, e), lambda i: (i, 0)),
                   _acc_out((e, d))],
        out_shape=[jax.ShapeDtypeStruct((t, e), F32), jax.ShapeDtypeStruct((t, e), MXU_DTYPE),
                   jax.ShapeDtypeStruct((e, d), WIRE_DTYPE)],
        scratch_shapes=[pltpu.VMEM((e, d), F32)],
        compiler_params=_params(1, 48),
    )(dx1, y, proj, w_out)


def _conv_ln_bwd(dy, y1, lg, lb):
    t, e = dy.shape
    tm = TM_MIX

    def body(dy_ref, y1_ref, lg_ref, lb_ref, dy1_ref, dlg_ref, dlb_ref, dcb_ref):
        @pl.when(pl.program_id(0) == 0)
        def _():
            dlg_ref[...] = jnp.zeros_like(dlg_ref)
            dlb_ref[...] = jnp.zeros_like(dlb_ref)
            dcb_ref[...] = jnp.zeros_like(dcb_ref)
        xhat, rstd = _ln_stats(y1_ref[...])
        lg = lg_ref[...]
        y2 = xhat * lg + lb_ref[...]
        dy2 = dy_ref[...] * _silu_grad(y2, _sigmoid(y2))
        dlg_ref[...] += jnp.sum(dy2 * xhat, axis=0, keepdims=True)
        dlb_ref[...] += jnp.sum(dy2, axis=0, keepdims=True)
        dy1 = _ln_bwd(dy2 * lg, xhat, rstd)
        dy1_ref[...] = dy1
        dcb_ref[...] += jnp.sum(dy1, axis=0, keepdims=True)

    row = pl.BlockSpec((tm, e), lambda i: (i, 0))
    return pl.pallas_call(
        body, name="conv_ln_bwd", grid=(t // tm,),
        in_specs=[row, row, _whole((1, e)), _whole((1, e))],
        out_specs=[row, _acc_out((1, e)), _acc_out((1, e)), _acc_out((1, e))],
        out_shape=[jax.ShapeDtypeStruct((t, e), F32)] + [jax.ShapeDtypeStruct((1, e), F32)] * 3,
        compiler_params=_params(1, 48),
    )(dy, y1, lg, lb)


def _conv_bwd(dy1, proj, dz, cw, seq):
    t, e = dy1.shape
    tm = TM_MIX
    nt = seq // tm
    hb = tm // HALO
    n_halo_blocks = t // HALO

    def body(d_ref, dn_ref, a_ref, b_ref, ah_ref, bh_ref, dz_ref, cw_ref, dproj_ref, dcw_ref,
             y0s, d1s, zs, dsh, dcw8, w8):
        i = pl.program_id(0)
        pos = lax.rem(i, nt)

        @pl.when(i == 0)
        def _():
            dcw8[...] = jnp.zeros_like(dcw8)
        _conv_weights_to_sublanes(cw_ref, w8)
        a = a_ref[...]
        sb = _sigmoid(b_ref[...])
        y0s[pl.ds(HALO, tm), :] = a * sb
        d1s[pl.ds(0, tm), :] = d_ref[...]

        @pl.when(pos == 0)
        def _():
            y0s[pl.ds(0, HALO), :] = jnp.zeros((HALO, e), F32)

        @pl.when(pos != 0)
        def _():
            y0s[pl.ds(0, HALO), :] = ah_ref[...] * _sigmoid(bh_ref[...])

        @pl.when(pos == nt - 1)
        def _():
            d1s[pl.ds(tm, HALO), :] = jnp.zeros((HALO, e), F32)

        @pl.when(pos != nt - 1)
        def _():
            d1s[pl.ds(tm, HALO), :] = dn_ref[...]

        base = HALO - (CONV_K - 1)
        for c0 in range(0, e, DCW_CC):
            cols = slice(c0, c0 + DCW_CC)
            dcur = d_ref[:, cols]
            for s in range(1, 8):
                dsh[s - 1, pl.ds(0, 8), :] = jnp.zeros((8, DCW_CC), F32)
                dsh[s - 1, pl.ds(tm, 8), :] = jnp.zeros((8, DCW_CC), F32)
                dsh[s - 1, pl.ds(s, tm), :] = dcur
            for s in range(8):
                taps = [k for k in range(CONV_K) if (base + k) % 8 == s]
                off0 = base + taps[0] - s
                n, ch = (tm, DCW_RC) if s == 0 else (tm + 8, (tm + 8) // DCW_CHUNKS)
                sums = [None] * len(taps)
                for r in range(0, n, ch):
                    dch = d_ref[r:r + ch, cols] if s == 0 else dsh[s - 1, r:r + ch, :]
                    window = y0s[pl.ds(off0 + r, ch + 8 * (len(taps) - 1)), cols]
                    for m in range(len(taps)):
                        part = jnp.sum((dch * window[8 * m:8 * m + ch]).reshape(ch // 8, 8, DCW_CC), axis=0)
                        sums[m] = part if sums[m] is None else sums[m] + part
                for m, k in enumerate(taps):
                    dcw8[k, :, cols] += sums[m]

        def emit(r0, c0, dy0):
            rs, cs = pl.ds(r0, CONV_RC), slice(c0, c0 + CONV_CC)
            sbv = _sigmoid(b_ref[rs, cs])
            av = a_ref[rs, cs]
            dproj_ref[rs, c0:c0 + CONV_CC] = (dy0 * sbv).astype(dproj_ref.dtype)
            dproj_ref[rs, e + c0:e + c0 + CONV_CC] = (dy0 * av * sbv * (1.0 - sbv)).astype(dproj_ref.dtype)
        _conv_apply(d1s, w8, zs, 0, tm, e, True, emit)
        dproj_ref[:, 2 * e:3 * e] = dz_ref[...]

        @pl.when(i == t // tm - 1)
        def _():
            dcw_ref[...] = jnp.sum(dcw8[...], axis=1)

    tile = lambda col: pl.BlockSpec((tm, e), lambda i: (i, col))
    prev = lambda col: pl.BlockSpec((HALO, e), lambda i: (jnp.maximum(i * hb - 1, 0), col))
    nxt = pl.BlockSpec((HALO, e), lambda i: (jnp.minimum((i + 1) * hb, n_halo_blocks - 1), 0))
    return pl.pallas_call(
        body, name="conv_bwd", grid=(t // tm,),
        in_specs=[tile(0), nxt, tile(0), tile(1), prev(0), prev(1), tile(0), _whole(cw.shape)],
        out_specs=[pl.BlockSpec((tm, 3 * e), lambda i: (i, 0)), _acc_out(cw.shape)],
        out_shape=[jax.ShapeDtypeStruct((t, 3 * e), MXU_DTYPE), jax.ShapeDtypeStruct(cw.shape, F32)],
        scratch_shapes=[pltpu.VMEM((tm + HALO, e), F32), pltpu.VMEM((tm + HALO, e), F32),
                        pltpu.VMEM((7, CONV_RC + 8, CONV_CC), F32), pltpu.VMEM((7, tm + 8, DCW_CC), F32),
                        pltpu.VMEM((CONV_K, 8, e), F32), pltpu.VMEM((CONV_K, 8, e), F32)],
        compiler_params=_params(1, 56),
    )(dy1, dy1, proj, proj, proj, proj, dz, cw)


def _sgu_bwd(dy, proj, dz, lg, lb, sw, swt, sbt):
    t, e = dy.shape
    eg = e // GROUPS
    tm = TM_MIX
    nsteps = t // tm

    def body(dy_ref, a_ref, b_ref, dz_ref, lg_ref, lb_ref, sw_ref, swt_ref, sbt_ref,
             dproj_ref, dsw_ref, dsb_ref, dlg_ref, dlb_ref, mixed_s, dv_s, sb_acc):
        i = pl.program_id(0)

        @pl.when(i == 0)
        def _():
            dsw_ref[...] = jnp.zeros_like(dsw_ref)
            sb_acc[...] = jnp.zeros_like(sb_acc)
            dlg_ref[...] = jnp.zeros_like(dlg_ref)
            dlb_ref[...] = jnp.zeros_like(dlb_ref)
        a = a_ref[...]
        b = b_ref[...]
        lg = lg_ref[...]
        ea, eb, u, xhat, rstd, v = _sgu_parts(a, b, lg, lb_ref[...], sw_ref, sbt_ref, mixed_s, tm, e)
        dy = dy_ref[...]
        du = dy * mixed_s[...]
        dmixed = (dy * u).astype(MXU_DTYPE)
        mask = _tril_mask()
        mask_t = (lax.broadcasted_iota(jnp.int32, (CHUNK, CHUNK), 0)
                  <= lax.broadcasted_iota(jnp.int32, (CHUNK, CHUNK), 1))
        ones = jnp.ones((8, eg), MXU_DTYPE)
        for g in range(GROUPS):
            wtt = jnp.where(mask_t, swt_ref[g], 0.0).astype(MXU_DTYPE)
            cols = slice(g * eg, (g + 1) * eg)
            for ch in range(tm // CHUNK):
                rows = slice(ch * CHUNK, (ch + 1) * CHUNK)
                dm = dmixed[rows, cols]
                dv_s[rows, cols] = jnp.dot(wtt, dm, preferred_element_type=F32)
                dsw_ref[g] += _dot_nt(dm, v[rows, cols])
                sb_acc[g] += _dot_nt(ones, dm)
        dv = dv_s[...]
        dlg_ref[...] += jnp.sum(dv * xhat, axis=0, keepdims=True)
        dlb_ref[...] += jnp.sum(dv, axis=0, keepdims=True)
        dv0 = _ln_bwd(dv * lg, xhat, rstd)
        pdf_a = jnp.exp(-0.5 * a * a) * INV_SQRT_2PI
        pdf_b = jnp.exp(-0.5 * b * b) * INV_SQRT_2PI
        dproj_ref[:, 0:e] = (du * (0.5 * (1.0 + ea) + a * pdf_a)).astype(dproj_ref.dtype)
        dproj_ref[:, e:2 * e] = (dv0 * (0.5 * (1.0 + eb) + b * pdf_b)).astype(dproj_ref.dtype)
        dproj_ref[:, 2 * e:3 * e] = dz_ref[...]

        @pl.when(i == nsteps - 1)
        def _():
            for g in range(GROUPS):
                dsw_ref[g] = jnp.where(mask, dsw_ref[g], 0.0)
                dsb_ref[g:g + 1, :] = sb_acc[g, 0:1, :]

    tile = lambda col: pl.BlockSpec((tm, e), lambda i: (i, col))
    return pl.pallas_call(
        body, name="sgu_bwd", grid=(nsteps,),
        in_specs=[tile(0), tile(0), tile(1), tile(0), _whole((1, e)), _whole((1, e)), _whole(sw.shape),
                  _whole(swt.shape), _whole(sbt.shape)],
        out_specs=[pl.BlockSpec((tm, 3 * e), lambda i: (i, 0)), _acc_out(sw.shape), _acc_out((GROUPS, CHUNK)),
                   _acc_out((1, e)), _acc_out((1, e))],
        out_shape=[jax.ShapeDtypeStruct((t, 3 * e), MXU_DTYPE), jax.ShapeDtypeStruct(sw.shape, F32),
                   jax.ShapeDtypeStruct((GROUPS, CHUNK), F32), jax.ShapeDtypeStruct((1, e), F32),
                   jax.ShapeDtypeStruct((1, e), F32)],
        scratch_shapes=[pltpu.VMEM((tm, e), F32), pltpu.VMEM((tm, e), F32), pltpu.VMEM((GROUPS, 8, CHUNK), F32)],
        compiler_params=_params(1, 56),
    )(dy, proj, proj, dz, lg, lb, sw, swt, sbt)


def _inproj_bwd_x(dproj, w_blk, dx1, x, g):
    t, d = x.shape
    nb, _, bn = w_blk.shape
    tm = TM_IN

    def body(dp_ref, w_ref, dx1_ref, x_ref, g_ref, dx_ref, dg_ref):
        @pl.when(pl.program_id(0) == 0)
        def _():
            dg_ref[...] = jnp.zeros_like(dg_ref)
        dh = None
        for j in range(nb):
            term = _dot_nt(dp_ref[:, j * bn:(j + 1) * bn], w_ref[j])
            dh = term if dh is None else dh + term
        xv = x_ref[...]
        dx, dg = _rms_bwd(dh, xv, _rms_rstd(xv), g_ref[...])
        dx_ref[...] = dx1_ref[...] + dx
        dg_ref[...] += dg

    row = lambda w: pl.BlockSpec((tm, w), lambda i: (i, 0))
    return pl.pallas_call(
        body, name="inproj_bwd_x", grid=(t // tm,),
        in_specs=[row(nb * bn), _whole(w_blk.shape), row(d), row(d), _whole((1, d))],
        out_specs=[row(d), _acc_out((1, d))],
        out_shape=[jax.ShapeDtypeStruct((t, d), F32), jax.ShapeDtypeStruct((1, d), F32)],
        compiler_params=_params(1, 56),
    )(dproj, w_blk, dx1, x, g)


def _inproj_bwd_w(h, dproj):
    t, d = h.shape
    bn = dproj.shape[1] // N_DEV
    tm = TM_IN
    nsteps = t // tm

    nh = 2
    per = N_DEV // nh

    def body(h_ref, dp_ref, dw_ref, acc):
        i = pl.program_id(1)

        @pl.when(i == 0)
        def _():
            acc[...] = jnp.zeros_like(acc)
        hv = h_ref[...]
        for jj in range(per):
            acc[jj] += _dot_tn(hv, dp_ref[:, jj * bn:(jj + 1) * bn])

        @pl.when(i == nsteps - 1)
        def _():
            dw_ref[...] = acc[...].astype(dw_ref.dtype)

    return pl.pallas_call(
        body, name="inproj_bwd_w", grid=(nh, nsteps),
        in_specs=[pl.BlockSpec((tm, d), lambda hh, i: (i, 0)), pl.BlockSpec((tm, per * bn), lambda hh, i: (i, hh))],
        out_specs=[pl.BlockSpec((per, d, bn), lambda hh, i: (hh, 0, 0))],
        out_shape=[jax.ShapeDtypeStruct((N_DEV, d, bn), WIRE_DTYPE)],
        scratch_shapes=[pltpu.VMEM((per, d, bn), F32)],
        compiler_params=_params(2, 56),
    )(h, dproj)[0]


def _adamw(parts, w, m, v):
    nl, r, c = w.shape
    tr = r
    for cand in (512, 256, 128, 64, 32, 16, 8):
        if r % cand == 0 and cand * c * 4 <= (1 << 19):
            tr = cand
            break
    bc1 = 1.0 - ADAM_B1 ** ADAM_STEP
    bc2 = 1.0 - ADAM_B2 ** ADAM_STEP

    def body(*refs):
        p_refs = refs[:nl]
        w_ref, m_ref, v_ref, g_ref, d_ref, nm_ref, nv_ref = refs[nl:]

        def update(p_ref):
            g = p_ref[0].astype(F32)
            for s in range(1, N_DEV):
                g = g + p_ref[s].astype(F32)
            nm = ADAM_B1 * m_ref[0] + (1.0 - ADAM_B1) * g
            nv = ADAM_B2 * v_ref[0] + (1.0 - ADAM_B2) * (g * g)
            g_ref[0] = g
            nm_ref[0] = nm
            nv_ref[0] = nv
            d_ref[0] = -ADAM_LR * ((nm / bc1) / (jnp.sqrt(nv / bc2) + ADAM_EPS) + ADAM_WD * w_ref[0])

        if nl == 1:
            update(p_refs[0])
        else:
            for kk in range(nl):
                pl.when(pl.program_id(0) == kk)(lambda kk=kk: update(p_refs[kk]))

    part_spec = lambda kk: pl.BlockSpec((N_DEV, tr, c), lambda l, i: (0, jnp.where(l == kk, i, 0), 0))
    row = pl.BlockSpec((1, tr, c), lambda l, i: (l, i, 0))
    return pl.pallas_call(
        body, name="adamw", grid=(nl, r // tr),
        in_specs=[part_spec(kk) for kk in range(nl)] + [row, row, row],
        out_specs=[row] * 4,
        out_shape=[jax.ShapeDtypeStruct((nl, r, c), F32)] * 4,
        compiler_params=_params(2, 48),
    )(*parts, w, m, v)


def _pack(arrays):
    flat = jnp.concatenate([a.reshape(-1).astype(F32) for a in arrays])
    unit = 8 * PACK_COLS
    padded = -(-flat.shape[0] // unit) * unit
    return jnp.pad(flat, (0, padded - flat.shape[0])).reshape(-1, PACK_COLS)


def _unpack(packed, shapes):
    flat = packed.reshape(-1)
    out, off = [], 0
    for s in shapes:
        n = math.prod(s)
        out.append(flat[off:off + n].reshape(s))
        off += n
    return out


def _shard_last(arrays):
    cols = []
    for a in arrays:
        b = a.astype(F32).reshape(a.shape[:-1] + (N_DEV, a.shape[-1] // N_DEV))
        cols.append(jnp.moveaxis(b, -2, 0).reshape(N_DEV, -1))
    flat = jnp.concatenate(cols, axis=1)
    unit = 8 * PACK_COLS
    padded = -(-flat.shape[1] // unit) * unit
    return jnp.pad(flat, ((0, 0), (0, padded - flat.shape[1]))).reshape(N_DEV, -1, PACK_COLS)


def _unshard_last(packed, shard_shapes):
    flat = packed.reshape(N_DEV, -1)
    out, off = [], 0
    for s in shard_shapes:
        n = math.prod(s)
        a = jnp.moveaxis(flat[:, off:off + n].reshape((N_DEV,) + tuple(s)), 0, -2)
        out.append(a.reshape(tuple(s[:-1]) + (N_DEV * s[-1],)))
        off += n
    return out


def kernel(x, p, norm_g, w_in, w_out, conv_w, conv_b, conv_ln_g, conv_ln_b, sgu_ln_g, sgu_ln_b, sgu_w, sgu_b, pl_norm_g, pl_gate_w, pl_proj_w, final_g, loss_target, m_norm_g, m_w_in, m_w_out, m_conv_w, m_conv_b, m_conv_ln_g, m_conv_ln_b, m_sgu_ln_g, m_sgu_ln_b, m_sgu_w, m_sgu_b, m_pl_norm_g, m_pl_gate_w, m_pl_proj_w, m_final_g, v_norm_g, v_w_in, v_w_out, v_conv_w, v_conv_b, v_conv_ln_g, v_conv_ln_b, v_sgu_ln_g, v_sgu_ln_b, v_sgu_w, v_sgu_b, v_pl_norm_g, v_pl_gate_w, v_pl_proj_w, v_final_g):
    bsz, seq, d = x.shape
    t = bsz * seq
    depth = w_in.shape[0]
    e = w_out.shape[1] * N_DEV
    pd = p.shape[-1]
    n_conv, n_sgu = conv_w.shape[0], sgu_ln_g.shape[0]

    small_shapes = [conv_w.shape, sgu_ln_g.shape, sgu_ln_b.shape]
    cast = lambda a: a.astype(MXU_DTYPE)
    first = [cast(w_in[0]), cast(w_out[0]), cast(pl_gate_w[0]), cast(pl_proj_w), _pack([conv_w, sgu_ln_g, sgu_ln_b])]
    later = [[cast(w_in[l]), cast(w_out[l]), cast(pl_gate_w[l])] for l in range(1, depth)]
    gathered = _all_gather(first, "gather_weights")
    gather_pending, gather_tokens = {}, 0.0
    for l in range(1, depth):
        lands = _place_own(later[l - 1], False, "place_own_weights")
        send, recv, srcs, lnds, token = _exchange_start(later[l - 1], lands, False, f"gather_start_{l}")
        gather_pending[l] = (send, recv, srcs, lnds)
        gather_tokens = gather_tokens + token[0, 0]
    w_in_g = {0: gathered[0]}
    w_out_g = {0: gathered[1].reshape(e, d)}
    gate_g = {0: gathered[2].reshape(d, d)}
    proj_g = jnp.transpose(gathered[3], (1, 2, 0, 3)).reshape(depth, pd, d)
    conv_w_g, sgu_ln_g_g, sgu_ln_b_g = _unshard_last(gathered[4], small_shapes)
    sgu_wt = jnp.swapaxes(sgu_w, -1, -2)
    sgu_bt = jnp.swapaxes(sgu_b, -1, -2)

    xs = [x.reshape(t, d)]
    p_all = p.reshape(depth, t, pd)
    saved = []
    for l in range(depth):
        j = l // 2
        if l == 0:
            g_l = norm_g[0:1] + gather_tokens
        else:
            g_l = norm_g[l:l + 1]
            got = _exchange_wait(*gather_pending.pop(l), xs[-1], False, f"gather_wait_{l}")
            w_in_g[l], w_out_g[l], gate_g[l] = got[0], got[1].reshape(e, d), got[2].reshape(d, d)
        common = (xs[-1], p_all, l, g_l, w_in_g[l], w_out_g[l], pl_norm_g[l:l + 1], gate_g[l], proj_g[l], seq)
        if l % 2 == 0:
            h, proj, y1, y, x1, x2 = _layer_fwd(
                *common, conv=(conv_w_g[j], conv_b[j:j + 1], conv_ln_g[j:j + 1], conv_ln_b[j:j + 1]))
        else:
            y1 = None
            h, proj, y, x1, x2 = _layer_fwd(
                *common, sgu=(sgu_ln_g_g[j:j + 1], sgu_ln_b_g[j:j + 1], sgu_w[j], sgu_bt[j]))
        saved.append((h, proj, y1, y, x1))
        xs.append(x2)

    loss_part, dx, d_final_g = _loss_head(xs[-1], final_g.reshape(1, d), loss_target.reshape(t, d))
    loss = lax.psum(loss_part[0, 0], ("x", "y", "c"))

    d_norm_g, d_pl_norm_g = [None] * depth, [None] * depth
    scatter_pending = {}
    d_conv_w, d_conv_b, d_conv_ln_g, d_conv_ln_b = [None] * n_conv, [None] * n_conv, [None] * n_conv, [None] * n_conv
    d_sgu_ln_g, d_sgu_ln_b, d_sgu_w, d_sgu_b = [None] * n_sgu, [None] * n_sgu, [None] * n_sgu, [None] * n_sgu
    def scatter(parts, name):
        send, recv, srcs, lnds, token = _exchange_start(parts, _place_own(parts, True, "place_own_grads"), True, name)
        return (send, recv, srcs, lnds), token[0, 0]

    for l in reversed(range(depth)):
        j = l // 2
        h, proj, y1, y, x1 = saved[l]
        dx1, dgate_p, dprojw_p, d_pl_norm_g[l] = _ple_bwd(dx, x1, p_all, l, pl_norm_g[l:l + 1], gate_g[l], proj_g[l])
        dy, dz, dw_out_p = _outproj_bwd(dx1, y, proj, w_out_g[l])
        early = [dw_out_p.reshape(N_DEV, e // N_DEV, d), dgate_p.reshape(N_DEV, d // N_DEV, d), dprojw_p]
        early_token = 0.0
        if l == 0:
            scatter_pending["0_early"], early_token = scatter(early, "scatter_start_0_early")
            early = []
        if l % 2 == 0:
            dy1, d_conv_ln_g[j], d_conv_ln_b[j], d_conv_b[j] = _conv_ln_bwd(
                dy, y1, conv_ln_g[j:j + 1] + early_token, conv_ln_b[j:j + 1])
            dproj, d_conv_w[j] = _conv_bwd(dy1, proj, dz, conv_w_g[j], seq)
        else:
            dproj, d_sgu_w[j], d_sgu_b[j], d_sgu_ln_g[j], d_sgu_ln_b[j] = _sgu_bwd(
                dy, proj, dz, sgu_ln_g_g[j:j + 1] + early_token, sgu_ln_b_g[j:j + 1], sgu_w[j], sgu_wt[j], sgu_bt[j])
        scatter_pending[l], token = scatter([_inproj_bwd_w(h, dproj)] + early, f"scatter_start_{l}")
        dx, d_norm_g[l] = _inproj_bwd_x(dproj, w_in_g[l], dx1, xs[l], norm_g[l:l + 1] + token)
    grad_x = dx.reshape(bsz, seq, d)

    small_part = _shard_last([jnp.stack(d_conv_w), jnp.concatenate(d_sgu_ln_g, axis=0),
                              jnp.concatenate(d_sgu_ln_b, axis=0)])
    rep_part = _pack([jnp.concatenate(d_norm_g, axis=0), jnp.concatenate(d_conv_b, axis=0),
                      jnp.concatenate(d_conv_ln_g, axis=0), jnp.concatenate(d_conv_ln_b, axis=0),
                      jnp.stack(d_sgu_w), jnp.stack(d_sgu_b), jnp.concatenate(d_pl_norm_g, axis=0), d_final_g])
    small_send, small_recv, small_srcs, small_lnds, small_token = _exchange_start(
        [small_part], _place_own([small_part], True, "place_own_small"), True, "scatter_small_start")
    rep_send, rep_recv, rep_srcs, rep_lnds, rep_token = _exchange_start(
        [rep_part], _place_own([rep_part], False, "place_own_replicated"), False, "gather_replicated_start")

    landed = {}
    for key in list(scatter_pending):
        landed[key] = _exchange_wait(*scatter_pending.pop(key), small_token + rep_token, True, f"scatter_wait_{key}")
    dw_in_l = [landed[l][0] for l in range(depth)]
    rest = [landed["0_early"]] + [landed[l][1:] for l in range(1, depth)]

    o_w_in = _adamw(dw_in_l, w_in, m_w_in, v_w_in)
    o_w_out = _adamw([r[0] for r in rest], w_out, m_w_out, v_w_out)
    o_gate = _adamw([r[1] for r in rest], pl_gate_w, m_pl_gate_w, v_pl_gate_w)
    o_projw = _adamw([r[2] for r in rest], pl_proj_w, m_pl_proj_w, v_pl_proj_w)
    r_small = _exchange_wait(small_send, small_recv, small_srcs, small_lnds, o_projw[1], True, "scatter_small_wait")[0]
    r_rep = _exchange_wait(rep_send, rep_recv, rep_srcs, rep_lnds, o_w_in[1], False, "gather_replicated_wait")[0]

    def packed(parts, ws, ms, vs):
        shapes = [a.shape for a in ws]
        outs = _adamw([parts.reshape(N_DEV, -1, PACK_COLS)], _pack(ws)[None], _pack(ms)[None], _pack(vs)[None])
        return [_unpack(o[0], shapes) for o in outs]

    o_small = packed(r_small, [conv_w, sgu_ln_g, sgu_ln_b], [m_conv_w, m_sgu_ln_g, m_sgu_ln_b],
                     [v_conv_w, v_sgu_ln_g, v_sgu_ln_b])
    o_rep = packed(r_rep, [norm_g, conv_b, conv_ln_g, conv_ln_b, sgu_w, sgu_b, pl_norm_g, final_g],
                   [m_norm_g, m_conv_b, m_conv_ln_g, m_conv_ln_b, m_sgu_w, m_sgu_b, m_pl_norm_g, m_final_g],
                   [v_norm_g, v_conv_b, v_conv_ln_g, v_conv_ln_b, v_sgu_w, v_sgu_b, v_pl_norm_g, v_final_g])

    def leaf(kind):
        rep, small = o_rep[kind], o_small[kind]
        return [rep[0], o_w_in[kind], o_w_out[kind], small[0], rep[1], rep[2], rep[3], small[1], small[2], rep[4],
                rep[5], rep[6], o_gate[kind], o_projw[kind], rep[7]]

    return (loss, grad_x, *leaf(0), *leaf(1), *leaf(2), *leaf(3))
```

```python
import math

import jax
import jax.numpy as jnp
from jax import lax
from jax.experimental import pallas as pl
from jax.experimental.pallas import tpu as pltpu

F32 = jnp.float32
MXU_DTYPE = jnp.bfloat16
WIRE_DTYPE = jnp.bfloat16

EPS = 1e-6
CONV_K = 31
CHUNK = 128
GROUPS = 8
HALO = 32
N_DEV = 8
DEPTH = 4

ADAM_LR = 0.001
ADAM_B1 = 0.9
ADAM_B2 = 0.999
ADAM_EPS = 1e-08
ADAM_WD = 0.01
ADAM_STEP = 10

TM_IN = 512
TM_MIX = 256
TM_OUT = 512
FUSE_SB = 128
CONV_RC = 64
CONV_CC = 128
DCW_CC = 256
DCW_RC = 64
DCW_CHUNKS = 3
PACK_COLS = 1024

MESH_ID = pl.DeviceIdType.MESH
INV_SQRT2 = 1.0 / math.sqrt(2.0)
INV_SQRT_2PI = 1.0 / math.sqrt(2.0 * math.pi)


def _params(n_grid, vmem_mb):
    return pltpu.CompilerParams(dimension_semantics=("arbitrary",) * n_grid, vmem_limit_bytes=vmem_mb << 20)


def _whole(shape):
    nd = len(shape)
    return pl.BlockSpec(shape, lambda *_: (0,) * nd, pipeline_mode=pl.Buffered(1))


def _acc_out(shape):
    nd = len(shape)
    return pl.BlockSpec(shape, lambda *_: (0,) * nd)


def _dot(a, b):
    return jnp.dot(a.astype(MXU_DTYPE), b.astype(MXU_DTYPE), preferred_element_type=F32)


def _dot_nt(a, b):
    return lax.dot_general(a.astype(MXU_DTYPE), b.astype(MXU_DTYPE), (((1,), (1,)), ((), ())),
                           preferred_element_type=F32)


def _dot_tn(a, b):
    return lax.dot_general(a.astype(MXU_DTYPE), b.astype(MXU_DTYPE), (((0,), (0,)), ((), ())),
                           preferred_element_type=F32)


def _sigmoid(x):
    return jax.nn.sigmoid(x)


def _rms_rstd(x):
    return lax.rsqrt(jnp.mean(x * x, axis=-1, keepdims=True) + EPS)


def _rms_bwd(dy, x, rstd, g):
    gy = dy * g
    xr = x * rstd
    dx = rstd * (gy - xr * jnp.mean(gy * xr, axis=-1, keepdims=True))
    dg = jnp.sum(dy * xr, axis=0, keepdims=True)
    return dx, dg


def _ln_stats(x):
    mu = jnp.mean(x, axis=-1, keepdims=True)
    xc = x - mu
    var = jnp.mean(xc * xc, axis=-1, keepdims=True)
    rstd = lax.rsqrt(var + EPS)
    return xc * rstd, rstd


def _ln_bwd(dxhat, xhat, rstd):
    return rstd * (dxhat - jnp.mean(dxhat, axis=-1, keepdims=True)
                   - xhat * jnp.mean(dxhat * xhat, axis=-1, keepdims=True))


def _silu_grad(x, s):
    return s * (1.0 + x * (1.0 - s))


def _tril_mask():
    r = lax.broadcasted_iota(jnp.int32, (CHUNK, CHUNK), 0)
    c = lax.broadcasted_iota(jnp.int32, (CHUNK, CHUNK), 1)
    return r >= c


def _conv_weights_to_sublanes(w_ref, w8_ref):
    for k in range(CONV_K):
        w8_ref[k] = jnp.broadcast_to(w_ref[k:k + 1, :], w8_ref.shape[1:])


def _conv_apply(src_ref, w8_ref, zs_ref, base, tm, e, flip, emit):
    def row_block(i, carry):
        r0 = pl.multiple_of(i * CONV_RC, CONV_RC)
        for c0 in range(0, e, CONV_CC):
            cols = slice(c0, c0 + CONV_CC)
            acc = None
            for s in range(8):
                nrows = CONV_RC if s == 0 else CONV_RC + 8
                taps = [k for k in range(CONV_K) if (base + k) % 8 == s]
                off0 = base + taps[0] - s
                span = nrows + 8 * (len(taps) - 1)
                window = src_ref[pl.ds(r0 + off0, span), cols].reshape(span // 8, 8, CONV_CC)
                z = None
                for m, k in enumerate(taps):
                    wk = (CONV_K - 1 - k) if flip else k
                    term = w8_ref[wk, :, cols][None] * window[m:m + nrows // 8]
                    z = term if z is None else z + term
                z = z.reshape(nrows, CONV_CC)
                if s == 0:
                    acc = z
                else:
                    zs_ref[s - 1, pl.ds(0, nrows), :] = z
                    acc = acc + zs_ref[s - 1, pl.ds(s, CONV_RC), :]
            emit(r0, c0, acc)
        return carry

    lax.fori_loop(0, tm // CONV_RC, row_block, 0)


def _mesh_pos():
    return lax.axis_index("x"), lax.axis_index("y"), lax.axis_index("c")


def _slot(px, py, pc):
    return 4 * px + 2 * py + pc


def _peers(x, y, c):
    return [((1 - x) if (k & 4) else x, (1 - y) if (k & 2) else y, (1 - c) if (k & 1) else c)
            for k in range(1, N_DEV)]


HBM_SPEC = pl.BlockSpec(memory_space=pltpu.HBM)
SEM_SPEC = pl.BlockSpec(memory_space=pltpu.SEMAPHORE)
SIDE_EFFECT = pltpu.SideEffectType.DATAFLOW_SIDE_EFFECTING


def _exchange_copy(src_refs, land_refs, send_sems, recv_sems, i, k, peer, scatter, me):
    slot = _slot(*peer)
    return pltpu.make_async_remote_copy(
        src_ref=src_refs[i].at[slot] if scatter else src_refs[i],
        dst_ref=land_refs[i].at[me if me is not None else slot],
        send_sem=send_sems.at[i * 7 + k], recv_sem=recv_sems.at[i * 7 + k],
        device_id=peer, device_id_type=MESH_ID)


def _exchange_start(srcs, lands, after, scatter, name):
    n = len(srcs)

    def body(*refs):
        src_refs, land_refs = refs[:n], refs[n:2 * n]
        send_sems, recv_sems, token = refs[2 * n + 1], refs[2 * n + 2], refs[-1]
        x, y, c = _mesh_pos()
        me = _slot(x, y, c)
        for i in range(n):
            for k, peer in enumerate(_peers(x, y, c)):
                _exchange_copy(src_refs, land_refs, send_sems, recv_sems, i, k, peer, scatter, me).start()
        token[...] = jnp.zeros_like(token)

    arrays = list(srcs) + list(lands)
    outs = pl.pallas_call(
        body, name=name,
        out_shape=(pltpu.SemaphoreType.DMA((7 * n,)), pltpu.SemaphoreType.DMA((7 * n,)),
                   *[pltpu.HBM(a.shape, a.dtype) for a in arrays], jax.ShapeDtypeStruct((8, 128), F32)),
        in_specs=[HBM_SPEC] * (2 * n) + [pl.BlockSpec(memory_space=pl.ANY)],
        out_specs=(SEM_SPEC, SEM_SPEC, *[HBM_SPEC] * (2 * n), pl.BlockSpec(memory_space=pltpu.VMEM)),
        input_output_aliases={i: 2 + i for i in range(2 * n)},
        compiler_params=pltpu.CompilerParams(has_side_effects=SIDE_EFFECT),
    )(*[pltpu.with_memory_space_constraint(a, pltpu.HBM) for a in arrays], after)
    return outs[0], outs[1], list(outs[2:2 + n]), list(outs[2 + n:2 + 2 * n]), outs[-1]


def _exchange_wait(send_sems, recv_sems, srcs, lands, after, scatter, name):
    n = len(srcs)

    def body(*refs):
        src_refs, land_refs = refs[:n], refs[n:2 * n]
        send, recv = refs[2 * n], refs[2 * n + 1]
        x, y, c = _mesh_pos()
        for i in range(n):
            for k, peer in enumerate(_peers(x, y, c)):
                cp = _exchange_copy(src_refs, land_refs, send, recv, i, k, peer, scatter, None)
                cp.wait_send()
                cp.wait_recv()

    arrays = list(srcs) + list(lands)
    outs = pl.pallas_call(
        body, name=name,
        out_shape=tuple(pltpu.HBM(a.shape, a.dtype) for a in arrays),
        in_specs=[HBM_SPEC] * (2 * n) + [SEM_SPEC, SEM_SPEC, pl.BlockSpec(memory_space=pl.ANY)],
        out_specs=tuple([HBM_SPEC] * (2 * n)),
        input_output_aliases={i: i for i in range(2 * n)},
        compiler_params=pltpu.CompilerParams(has_side_effects=SIDE_EFFECT),
    )(*arrays, send_sems, recv_sems, after)
    return list(outs[n:])


def _place_own(parts, scatter, name):
    n = len(parts)
    me = jnp.reshape(_slot(*_mesh_pos()), (1,)).astype(jnp.int32)

    def body(me_ref, *refs):
        for i in range(n):
            refs[n + i][0] = refs[i][0] if scatter else refs[i][...]

    def slot_spec(shape):
        rest = len(shape)
        return pl.BlockSpec((1,) + tuple(shape), lambda i, me_ref: (me_ref[0],) + (0,) * rest)

    def whole_spec(shape):
        nd = len(shape)
        return pl.BlockSpec(tuple(shape), lambda i, me_ref: (0,) * nd)

    blocks = [a.shape[1:] if scatter else a.shape for a in parts]
    return pl.pallas_call(
        body, name=name,
        grid_spec=pltpu.PrefetchScalarGridSpec(
            num_scalar_prefetch=1, grid=(1,),
            in_specs=[slot_spec(b) if scatter else whole_spec(b) for b in blocks],
            out_specs=[slot_spec(b) for b in blocks]),
        out_shape=[jax.ShapeDtypeStruct((N_DEV,) + tuple(b), a.dtype) for a, b in zip(parts, blocks)],
        compiler_params=_params(1, 32),
    )(me, *parts)


def _all_gather(items, name):
    n = len(items)

    def body(*refs):
        in_refs, out_refs = refs[:n], refs[n:2 * n]
        send_sems, recv_sems, local_sems = refs[2 * n:]
        x, y, c = _mesh_pos()
        me, sibling = (x, y, c), (x, y, 1 - c)
        chips = [(1 - x, y), (x, 1 - y), (1 - x, 1 - y)]

        def copy(i, k, block, to, src=None):
            dst = out_refs[i].at[_slot(*block)]
            return pltpu.make_async_remote_copy(
                src_ref=dst if src is None else src, dst_ref=dst,
                send_sem=send_sems.at[i * 7 + k], recv_sem=recv_sems.at[i * 7 + k],
                device_id=to, device_id_type=MESH_ID)

        mine = [pltpu.make_async_copy(in_refs[i], out_refs[i].at[_slot(*me)], local_sems.at[i]) for i in range(n)]
        for cp in mine:
            cp.start()
        first = []
        for i in range(n):
            first.append(copy(i, 0, me, sibling, src=in_refs[i]))
            for j, chip in enumerate(chips):
                first.append(copy(i, 1 + j, me, (*chip, c), src=in_refs[i]))
        for cp in first:
            cp.start()
        passed = []
        for j, chip in enumerate(chips):
            for i in range(n):
                copy(i, 1 + j, (*chip, c), me).wait_recv()
                fwd = copy(i, 4 + j, (*chip, c), sibling)
                fwd.start()
                passed.append(fwd)
        for i in range(n):
            copy(i, 0, sibling, me).wait_recv()
            for j, chip in enumerate(chips):
                copy(i, 4 + j, (*chip, 1 - c), me).wait_recv()
        for cp in first + passed:
            cp.wait_send()
        for cp in mine:
            cp.wait()

    any_spec = pl.BlockSpec(memory_space=pl.ANY)
    return pl.pallas_call(
        body, name=name,
        out_shape=[jax.ShapeDtypeStruct((N_DEV,) + a.shape, a.dtype) for a in items],
        in_specs=[any_spec] * n, out_specs=[any_spec] * n,
        scratch_shapes=[pltpu.SemaphoreType.DMA((7 * n,)), pltpu.SemaphoreType.DMA((7 * n,)),
                        pltpu.SemaphoreType.DMA((n,))],
    )(*items)


def _layer_fwd(x, p_all, layer, g, w_blk, w_out, plg, gate_w, proj_w, seq, conv=None, sgu=None):
    t, d = x.shape
    nb, _, bn = w_blk.shape
    e = w_out.shape[0]
    pd = p_all.shape[-1]
    tm = TM_MIX
    nt = seq // tm
    is_conv = conv is not None
    mixer_args = conv if is_conv else sgu
    n_mix = len(mixer_args)

    def body(*refs):
        x_ref, p_ref, g_ref, w_ref, wo_ref, plg_ref, gw_ref, pw_ref = refs[:8]
        mix = refs[8:8 + n_mix]
        outs = refs[8 + n_mix:]
        if is_conv:
            cw_ref, cb_ref, lg_ref, lb_ref = mix
            h_ref, proj_ref, y1_ref, y_ref, x1_ref, x2_ref, y0s, zs, w8 = outs

            @pl.when(lax.rem(pl.program_id(0), nt) == 0)
            def _():
                y0s[pl.ds(0, HALO), :] = jnp.zeros((HALO, e), F32)
            _conv_weights_to_sublanes(cw_ref, w8)
        else:
            lg_ref, lb_ref, sw_ref, sbt_ref = mix
            h_ref, proj_ref, y_ref, x1_ref, x2_ref, mixed_s = outs

        for sb in range(tm // FUSE_SB):
            rows = pl.ds(sb * FUSE_SB, FUSE_SB)
            xv = x_ref[rows, :]
            hv = (xv * _rms_rstd(xv) * g_ref[...]).astype(MXU_DTYPE)
            h_ref[rows, :] = hv
            for j in range(nb):
                proj_ref[rows, j * bn:(j + 1) * bn] = jnp.dot(hv, w_ref[j], preferred_element_type=F32)
            if is_conv:
                y0s[pl.ds(HALO + sb * FUSE_SB, FUSE_SB), :] = proj_ref[rows, 0:e] * _sigmoid(proj_ref[rows, e:2 * e])

        if is_conv:
            def emit(r0, c0, acc):
                y1_ref[pl.ds(r0, CONV_RC), c0:c0 + CONV_CC] = acc + cb_ref[:, c0:c0 + CONV_CC]
            _conv_apply(y0s, w8, zs, HALO - (CONV_K - 1), tm, e, False, emit)

        for sb in range(tm // FUSE_SB):
            rows = pl.ds(sb * FUSE_SB, FUSE_SB)
            if is_conv:
                xhat, _ = _ln_stats(y1_ref[rows, :])
                y2 = xhat * lg_ref[...] + lb_ref[...]
                y = y2 * _sigmoid(y2)
            else:
                _, _, u, _, _, _ = _sgu_parts(proj_ref[rows, 0:e], proj_ref[rows, e:2 * e], lg_ref[...], lb_ref[...],
                                              sw_ref, sbt_ref, mixed_s, FUSE_SB, e)
                y = u * mixed_s[...]
            y_ref[rows, :] = y
            z = proj_ref[rows, 2 * e:3 * e]
            q = (y * (z * _sigmoid(z))).astype(MXU_DTYPE)
            x1 = x_ref[rows, :] + jnp.dot(q, wo_ref[...], preferred_element_type=F32)
            x1_ref[rows, :] = x1
            rn = x1 * _rms_rstd(x1) * plg_ref[...]
            gate = _sigmoid(_dot(rn, gw_ref[...]))
            x2_ref[rows, :] = x1 + gate * _dot(p_ref[0, rows, :], pw_ref[...])

        if is_conv:
            y0s[pl.ds(0, HALO), :] = y0s[pl.ds(tm, HALO), :]

    row = lambda w: pl.BlockSpec((tm, w), lambda i: (i, 0))
    f32 = lambda w: jax.ShapeDtypeStruct((t, w), F32)
    out_shape = [jax.ShapeDtypeStruct((t, d), MXU_DTYPE), f32(3 * e)] + ([f32(e)] if is_conv else []) + [f32(e), f32(d), f32(d)]
    out_specs = [row(d), row(3 * e)] + ([row(e)] if is_conv else []) + [row(e), row(d), row(d)]
    scratch = ([pltpu.VMEM((tm + HALO, e), F32), pltpu.VMEM((7, CONV_RC + 8, CONV_CC), F32),
                pltpu.VMEM((CONV_K, 8, e), F32)] if is_conv else [pltpu.VMEM((FUSE_SB, e), F32)])
    return pl.pallas_call(
        body, name="layer_fwd_conv" if is_conv else "layer_fwd_sgu", grid=(t // tm,),
        in_specs=[row(d), pl.BlockSpec((1, tm, pd), lambda i: (layer, i, 0)), _whole((1, d)), _whole(w_blk.shape),
                  _whole((e, d)), _whole((1, d)),
                  _whole((d, d)), _whole((pd, d))] + [_whole(a.shape) for a in mixer_args],
        out_specs=out_specs, out_shape=out_shape, scratch_shapes=scratch,
        compiler_params=_params(1, 60),
    )(x, p_all, g, w_blk, w_out, plg, gate_w, proj_w, *mixer_args)


def _sgu_parts(a, b, lg, lb, sw_ref, sbt_ref, mixed_s, tm, e):
    eg = e // GROUPS
    ea = lax.erf(a * INV_SQRT2)
    eb = lax.erf(b * INV_SQRT2)
    u = 0.5 * a * (1.0 + ea)
    v0 = 0.5 * b * (1.0 + eb)
    xhat, rstd = _ln_stats(v0)
    v = (xhat * lg + lb).astype(MXU_DTYPE)
    mask = _tril_mask()
    for g in range(GROUPS):
        wt = jnp.where(mask, sw_ref[g], 0.0).astype(MXU_DTYPE)
        bcol = sbt_ref[:, g:g + 1]
        for ch in range(tm // CHUNK):
            rows = slice(ch * CHUNK, (ch + 1) * CHUNK)
            cols = slice(g * eg, (g + 1) * eg)
            mixed_s[rows, cols] = jnp.dot(wt, v[rows, cols], preferred_element_type=F32) + bcol
    return ea, eb, u, xhat, rstd, v


def _loss_head(xf, fg, tgt):
    t, d = xf.shape
    tm = TM_OUT
    nsteps = t // tm

    def body(x_ref, g_ref, t_ref, loss_ref, dx_ref, dg_ref, sq_s):
        i = pl.program_id(0)

        @pl.when(i == 0)
        def _():
            sq_s[...] = jnp.zeros_like(sq_s)
            dg_ref[...] = jnp.zeros_like(dg_ref)
        x = x_ref[...]
        rstd = _rms_rstd(x)
        err = x * rstd * g_ref[...] - t_ref[...]
        sq_s[...] += jnp.sum(err * err, axis=0, keepdims=True)
        dx, dg = _rms_bwd(err * (1.0 / d), x, rstd, g_ref[...])
        dx_ref[...] = dx
        dg_ref[...] += dg

        @pl.when(i == nsteps - 1)
        def _():
            loss_ref[...] = jnp.sum(sq_s[...], axis=1, keepdims=True) * (0.5 / d)

    row = pl.BlockSpec((tm, d), lambda i: (i, 0))
    return pl.pallas_call(
        body, name="loss_head", grid=(nsteps,),
        in_specs=[row, _whole((1, d)), row],
        out_specs=[_acc_out((1, 1)), row, _acc_out((1, d))],
        out_shape=[jax.ShapeDtypeStruct((1, 1), F32), jax.ShapeDtypeStruct((t, d), F32),
                   jax.ShapeDtypeStruct((1, d), F32)],
        scratch_shapes=[pltpu.VMEM((1, d), F32)],
        compiler_params=_params(1, 32),
    )(xf, fg, tgt)


def _ple_bwd(dx2, x1, p_all, layer, plg, gate_w, proj_w):
    t, d = x1.shape
    pd = p_all.shape[-1]
    tm = TM_OUT
    nsteps = t // tm
    bn = d // N_DEV

    def body(dx2_ref, x1_ref, p_ref, plg_ref, gw_ref, pw_ref, dx1_ref, dgw_ref, dpw_ref, dplg_ref, gw_acc, pw_acc):
        i = pl.program_id(0)

        @pl.when(i == 0)
        def _():
            gw_acc[...] = jnp.zeros_like(gw_acc)
            pw_acc[...] = jnp.zeros_like(pw_acc)
            dplg_ref[...] = jnp.zeros_like(dplg_ref)
        dx2 = dx2_ref[...]
        x1 = x1_ref[...]
        plg = plg_ref[...]
        rstd = _rms_rstd(x1)
        rn = (x1 * rstd * plg).astype(MXU_DTYPE)
        gate = _sigmoid(jnp.dot(rn, gw_ref[...], preferred_element_type=F32))
        p_b = p_ref[0].astype(MXU_DTYPE)
        pp = jnp.dot(p_b, pw_ref[...], preferred_element_type=F32)
        dpp = (dx2 * gate).astype(MXU_DTYPE)
        dgpre = (dx2 * pp * gate * (1.0 - gate)).astype(MXU_DTYPE)
        pw_acc[...] += _dot_tn(p_b, dpp)
        gw_acc[...] += _dot_tn(rn, dgpre)
        drn = _dot_nt(dgpre, gw_ref[...])
        dx, dg = _rms_bwd(drn, x1, rstd, plg)
        dx1_ref[...] = dx2 + dx
        dplg_ref[...] += dg

        @pl.when(i == nsteps - 1)
        def _():
            dgw_ref[...] = gw_acc[...].astype(dgw_ref.dtype)
            for j in range(N_DEV):
                dpw_ref[j] = pw_acc[:, j * bn:(j + 1) * bn].astype(dpw_ref.dtype)

    row = lambda w: pl.BlockSpec((tm, w), lambda i: (i, 0))
    return pl.pallas_call(
        body, name="ple_bwd", grid=(nsteps,),
        in_specs=[row(d), row(d), pl.BlockSpec((1, tm, pd), lambda i: (layer, i, 0)), _whole((1, d)), _whole((d, d)),
                  _whole((pd, d))],
        out_specs=[row(d), _acc_out((d, d)), _acc_out((N_DEV, pd, bn)), _acc_out((1, d))],
        out_shape=[jax.ShapeDtypeStruct((t, d), F32), jax.ShapeDtypeStruct((d, d), WIRE_DTYPE),
                   jax.ShapeDtypeStruct((N_DEV, pd, bn), WIRE_DTYPE), jax.ShapeDtypeStruct((1, d), F32)],
        scratch_shapes=[pltpu.VMEM((d, d), F32), pltpu.VMEM((pd, d), F32)],
        compiler_params=_params(1, 48),
    )(dx2, x1, p_all, plg, gate_w, proj_w)


def _outproj_bwd(dx1, y, proj, w_out):
    t, d = dx1.shape
    e = y.shape[1]
    tm = TM_MIX
    nsteps = t // tm

    def body(dx1_ref, y_ref, z_ref, wo_ref, dy_ref, dz_ref, dwo_ref, wo_acc):
        i = pl.program_id(0)

        @pl.when(i == 0)
        def _():
            wo_acc[...] = jnp.zeros_like(wo_acc)
        dx1 = dx1_ref[...].astype(MXU_DTYPE)
        y = y_ref[...]
        z = z_ref[...]
        s = _sigmoid(z)
        sz = z * s
        q = (y * sz).astype(MXU_DTYPE)
        wo_acc[...] += _dot_tn(q, dx1)
        dq = _dot_nt(dx1, wo_ref[...])
        dy_ref[...] = dq * sz
        dz_ref[...] = (dq * y * _silu_grad(z, s)).astype(dz_ref.dtype)

        @pl.when(i == nsteps - 1)
        def _():
            dwo_ref[...] = wo_acc[...].astype(dwo_ref.dtype)

    return pl.pallas_call(
        body, name="outproj_bwd", grid=(nsteps,),
        in_specs=[pl.BlockSpec((tm, d), lambda i: (i, 0)), pl.BlockSpec((tm, e), lambda i: (i, 0)),
                  pl.BlockSpec((tm, e), lambda i: (i, 2)), _whole((e, d))],
        out_specs=[pl.BlockSpec((tm, e), lambda i: (i, 0)), pl.BlockSpec((tm, e), lambda i: (i, 0)),
                   _acc_out((e, d))],
        out_shape=[jax.ShapeDtypeStruct((t, e), F32), jax.ShapeDtypeStruct((t, e), MXU_DTYPE),
                   jax.ShapeDtypeStruct((e, d), WIRE_DTYPE)],
        scratch_shapes=[pltpu.VMEM((e, d), F32)],
        compiler_params=_params(1, 48),
    )(dx1, y, proj, w_out)


def _conv_ln_bwd(dy, y1, lg, lb):
    t, e = dy.shape
    tm = TM_MIX

    def body(dy_ref, y1_ref, lg_ref, lb_ref, dy1_ref, dlg_ref, dlb_ref, dcb_ref):
        @pl.when(pl.program_id(0) == 0)
        def _():
            dlg_ref[...] = jnp.zeros_like(dlg_ref)
            dlb_ref[...] = jnp.zeros_like(dlb_ref)
            dcb_ref[...] = jnp.zeros_like(dcb_ref)
        xhat, rstd = _ln_stats(y1_ref[...])
        lg = lg_ref[...]
        y2 = xhat * lg + lb_ref[...]
        dy2 = dy_ref[...] * _silu_grad(y2, _sigmoid(y2))
        dlg_ref[...] += jnp.sum(dy2 * xhat, axis=0, keepdims=True)
        dlb_ref[...] += jnp.sum(dy2, axis=0, keepdims=True)
        dy1 = _ln_bwd(dy2 * lg, xhat, rstd)
        dy1_ref[...] = dy1
        dcb_ref[...] += jnp.sum(dy1, axis=0, keepdims=True)

    row = pl.BlockSpec((tm, e), lambda i: (i, 0))
    return pl.pallas_call(
        body, name="conv_ln_bwd", grid=(t // tm,),
        in_specs=[row, row, _whole((1, e)), _whole((1, e))],
        out_specs=[row, _acc_out((1, e)), _acc_out((1, e)), _acc_out((1, e))],
        out_shape=[jax.ShapeDtypeStruct((t, e), F32)] + [jax.ShapeDtypeStruct((1, e), F32)] * 3,
        compiler_params=_params(1, 48),
    )(dy, y1, lg, lb)


def _conv_bwd(dy1, proj, dz, cw, seq):
    t, e = dy1.shape
    tm = TM_MIX
    nt = seq // tm
    hb = tm // HALO
    n_halo_blocks = t // HALO

    def body(d_ref, dn_ref, a_ref, b_ref, ah_ref, bh_ref, dz_ref, cw_ref, dproj_ref, dcw_ref,
             y0s, d1s, zs, dsh, dcw8, w8):
        i = pl.program_id(0)
        pos = lax.rem(i, nt)

        @pl.when(i == 0)
        def _():
            dcw8[...] = jnp.zeros_like(dcw8)
        _conv_weights_to_sublanes(cw_ref, w8)
        a = a_ref[...]
        sb = _sigmoid(b_ref[...])
        y0s[pl.ds(HALO, tm), :] = a * sb
        d1s[pl.ds(0, tm), :] = d_ref[...]

        @pl.when(pos == 0)
        def _():
            y0s[pl.ds(0, HALO), :] = jnp.zeros((HALO, e), F32)

        @pl.when(pos != 0)
        def _():
            y0s[pl.ds(0, HALO), :] = ah_ref[...] * _sigmoid(bh_ref[...])

        @pl.when(pos == nt - 1)
        def _():
            d1s[pl.ds(tm, HALO), :] = jnp.zeros((HALO, e), F32)

        @pl.when(pos != nt - 1)
        def _():
            d1s[pl.ds(tm, HALO), :] = dn_ref[...]

        base = HALO - (CONV_K - 1)
        for c0 in range(0, e, DCW_CC):
            cols = slice(c0, c0 + DCW_CC)
            dcur = d_ref[:, cols]
            for s in range(1, 8):
                dsh[s - 1, pl.ds(0, 8), :] = jnp.zeros((8, DCW_CC), F32)
                dsh[s - 1, pl.ds(tm, 8), :] = jnp.zeros((8, DCW_CC), F32)
                dsh[s - 1, pl.ds(s, tm), :] = dcur
            for s in range(8):
                taps = [k for k in range(CONV_K) if (base + k) % 8 == s]
                off0 = base + taps[0] - s
                n, ch = (tm, DCW_RC) if s == 0 else (tm + 8, (tm + 8) // DCW_CHUNKS)
                sums = [None] * len(taps)
                for r in range(0, n, ch):
                    dch = d_ref[r:r + ch, cols] if s == 0 else dsh[s - 1, r:r + ch, :]
                    window = y0s[pl.ds(off0 + r, ch + 8 * (len(taps) - 1)), cols]
                    for m in range(len(taps)):
                        part = jnp.sum((dch * window[8 * m:8 * m + ch]).reshape(ch // 8, 8, DCW_CC), axis=0)
                        sums[m] = part if sums[m] is None else sums[m] + part
                for m, k in enumerate(taps):
                    dcw8[k, :, cols] += sums[m]

        def emit(r0, c0, dy0):
            rs, cs = pl.ds(r0, CONV_RC), slice(c0, c0 + CONV_CC)
            sbv = _sigmoid(b_ref[rs, cs])
            av = a_ref[rs, cs]
            dproj_ref[rs, c0:c0 + CONV_CC] = (dy0 * sbv).astype(dproj_ref.dtype)
            dproj_ref[rs, e + c0:e + c0 + CONV_CC] = (dy0 * av * sbv * (1.0 - sbv)).astype(dproj_ref.dtype)
        _conv_apply(d1s, w8, zs, 0, tm, e, True, emit)
        dproj_ref[:, 2 * e:3 * e] = dz_ref[...]

        @pl.when(i == t // tm - 1)
        def _():
            dcw_ref[...] = jnp.sum(dcw8[...], axis=1)

    tile = lambda col: pl.BlockSpec((tm, e), lambda i: (i, col))
    prev = lambda col: pl.BlockSpec((HALO, e), lambda i: (jnp.maximum(i * hb - 1, 0), col))
    nxt = pl.BlockSpec((HALO, e), lambda i: (jnp.minimum((i + 1) * hb, n_halo_blocks - 1), 0))
    return pl.pallas_call(
        body, name="conv_bwd", grid=(t // tm,),
        in_specs=[tile(0), nxt, tile(0), tile(1), prev(0), prev(1), tile(0), _whole(cw.shape)],
        out_specs=[pl.BlockSpec((tm, 3 * e), lambda i: (i, 0)), _acc_out(cw.shape)],
        out_shape=[jax.ShapeDtypeStruct((t, 3 * e), MXU_DTYPE), jax.ShapeDtypeStruct(cw.shape, F32)],
        scratch_shapes=[pltpu.VMEM((tm + HALO, e), F32), pltpu.VMEM((tm + HALO, e), F32),
                        pltpu.VMEM((7, CONV_RC + 8, CONV_CC), F32), pltpu.VMEM((7, tm + 8, DCW_CC), F32),
                        pltpu.VMEM((CONV_K, 8, e), F32), pltpu.VMEM((CONV_K, 8, e), F32)],
        compiler_params=_params(1, 56),
    )(dy1, dy1, proj, proj, proj, proj, dz, cw)


def _sgu_bwd(dy, proj, dz, lg, lb, sw, swt, sbt):
    t, e = dy.shape
    eg = e // GROUPS
    tm = TM_MIX
    nsteps = t // tm

    def body(dy_ref, a_ref, b_ref, dz_ref, lg_ref, lb_ref, sw_ref, swt_ref, sbt_ref,
             dproj_ref, dsw_ref, dsb_ref, dlg_ref, dlb_ref, mixed_s, dv_s, sb_acc):
        i = pl.program_id(0)

        @pl.when(i == 0)
        def _():
            dsw_ref[...] = jnp.zeros_like(dsw_ref)
            sb_acc[...] = jnp.zeros_like(sb_acc)
            dlg_ref[...] = jnp.zeros_like(dlg_ref)
            dlb_ref[...] = jnp.zeros_like(dlb_ref)
        a = a_ref[...]
        b = b_ref[...]
        lg = lg_ref[...]
        ea, eb, u, xhat, rstd, v = _sgu_parts(a, b, lg, lb_ref[...], sw_ref, sbt_ref, mixed_s, tm, e)
        dy = dy_ref[...]
        du = dy * mixed_s[...]
        dmixed = (dy * u).astype(MXU_DTYPE)
        mask = _tril_mask()
        mask_t = (lax.broadcasted_iota(jnp.int32, (CHUNK, CHUNK), 0)
                  <= lax.broadcasted_iota(jnp.int32, (CHUNK, CHUNK), 1))
        ones = jnp.ones((8, eg), MXU_DTYPE)
        for g in range(GROUPS):
            wtt = jnp.where(mask_t, swt_ref[g], 0.0).astype(MXU_DTYPE)
            cols = slice(g * eg, (g + 1) * eg)
            for ch in range(tm // CHUNK):
                rows = slice(ch * CHUNK, (ch + 1) * CHUNK)
                dm = dmixed[rows, cols]
                dv_s[rows, cols] = jnp.dot(wtt, dm, preferred_element_type=F32)
                dsw_ref[g] += _dot_nt(dm, v[rows, cols])
                sb_acc[g] += _dot_nt(ones, dm)
        dv = dv_s[...]
        dlg_ref[...] += jnp.sum(dv * xhat, axis=0, keepdims=True)
        dlb_ref[...] += jnp.sum(dv, axis=0, keepdims=True)
        dv0 = _ln_bwd(dv * lg, xhat, rstd)
        pdf_a = jnp.exp(-0.5 * a * a) * INV_SQRT_2PI
        pdf_b = jnp.exp(-0.5 * b * b) * INV_SQRT_2PI
        dproj_ref[:, 0:e] = (du * (0.5 * (1.0 + ea) + a * pdf_a)).astype(dproj_ref.dtype)
        dproj_ref[:, e:2 * e] = (dv0 * (0.5 * (1.0 + eb) + b * pdf_b)).astype(dproj_ref.dtype)
        dproj_ref[:, 2 * e:3 * e] = dz_ref[...]

        @pl.when(i == nsteps - 1)
        def _():
            for g in range(GROUPS):
                dsw_ref[g] = jnp.where(mask, dsw_ref[g], 0.0)
                dsb_ref[g:g + 1, :] = sb_acc[g, 0:1, :]

    tile = lambda col: pl.BlockSpec((tm, e), lambda i: (i, col))
    return pl.pallas_call(
        body, name="sgu_bwd", grid=(nsteps,),
        in_specs=[tile(0), tile(0), tile(1), tile(0), _whole((1, e)), _whole((1, e)), _whole(sw.shape),
                  _whole(swt.shape), _whole(sbt.shape)],
        out_specs=[pl.BlockSpec((tm, 3 * e), lambda i: (i, 0)), _acc_out(sw.shape), _acc_out((GROUPS, CHUNK)),
                   _acc_out((1, e)), _acc_out((1, e))],
        out_shape=[jax.ShapeDtypeStruct((t, 3 * e), MXU_DTYPE), jax.ShapeDtypeStruct(sw.shape, F32),
                   jax.ShapeDtypeStruct((GROUPS, CHUNK), F32), jax.ShapeDtypeStruct((1, e), F32),
                   jax.ShapeDtypeStruct((1, e), F32)],
        scratch_shapes=[pltpu.VMEM((tm, e), F32), pltpu.VMEM((tm, e), F32), pltpu.VMEM((GROUPS, 8, CHUNK), F32)],
        compiler_params=_params(1, 56),
    )(dy, proj, proj, dz, lg, lb, sw, swt, sbt)


def _inproj_bwd_x(dproj, w_blk, dx1, x, g):
    t, d = x.shape
    nb, _, bn = w_blk.shape
    tm = TM_IN

    def body(dp_ref, w_ref, dx1_ref, x_ref, g_ref, dx_ref, dg_ref):
        @pl.when(pl.program_id(0) == 0)
        def _():
            dg_ref[...] = jnp.zeros_like(dg_ref)
        dh = None
        for j in range(nb):
            term = _dot_nt(dp_ref[:, j * bn:(j + 1) * bn], w_ref[j])
            dh = term if dh is None else dh + term
        xv = x_ref[...]
        dx, dg = _rms_bwd(dh, xv, _rms_rstd(xv), g_ref[...])
        dx_ref[...] = dx1_ref[...] + dx
        dg_ref[...] += dg

    row = lambda w: pl.BlockSpec((tm, w), lambda i: (i, 0))
    return pl.pallas_call(
        body, name="inproj_bwd_x", grid=(t // tm,),
        in_specs=[row(nb * bn), _whole(w_blk.shape), row(d), row(d), _whole((1, d))],
        out_specs=[row(d), _acc_out((1, d))],
        out_shape=[jax.ShapeDtypeStruct((t, d), F32), jax.ShapeDtypeStruct((1, d), F32)],
        compiler_params=_params(1, 56),
    )(dproj, w_blk, dx1, x, g)


def _inproj_bwd_w(h, dproj):
    t, d = h.shape
    bn = dproj.shape[1] // N_DEV
    tm = TM_IN
    nsteps = t // tm

    nh = 2
    per = N_DEV // nh

    def body(h_ref, dp_ref, dw_ref, acc):
        i = pl.program_id(1)

        @pl.when(i == 0)
        def _():
            acc[...] = jnp.zeros_like(acc)
        hv = h_ref[...]
        for jj in range(per):
            acc[jj] += _dot_tn(hv, dp_ref[:, jj * bn:(jj + 1) * bn])

        @pl.when(i == nsteps - 1)
        def _():
            dw_ref[...] = acc[...].astype(dw_ref.dtype)

    return pl.pallas_call(
        body, name="inproj_bwd_w", grid=(nh, nsteps),
        in_specs=[pl.BlockSpec((tm, d), lambda hh, i: (i, 0)), pl.BlockSpec((tm, per * bn), lambda hh, i: (i, hh))],
        out_specs=[pl.BlockSpec((per, d, bn), lambda hh, i: (hh, 0, 0))],
        out_shape=[jax.ShapeDtypeStruct((N_DEV, d, bn), WIRE_DTYPE)],
        scratch_shapes=[pltpu.VMEM((per, d, bn), F32)],
        compiler_params=_params(2, 56),
    )(h, dproj)[0]


def _adamw(parts, w, m, v):
    nl, r, c = w.shape
    tr = r
    for cand in (512, 256, 128, 64, 32, 16, 8):
        if r % cand == 0 and cand * c * 4 <= (1 << 19):
            tr = cand
            break
    bc1 = 1.0 - ADAM_B1 ** ADAM_STEP
    bc2 = 1.0 - ADAM_B2 ** ADAM_STEP

    def body(*refs):
        p_refs = refs[:nl]
        w_ref, m_ref, v_ref, g_ref, d_ref, nm_ref, nv_ref = refs[nl:]

        def update(p_ref):
            g = p_ref[0].astype(F32)
            for s in range(1, N_DEV):
                g = g + p_ref[s].astype(F32)
            nm = ADAM_B1 * m_ref[0] + (1.0 - ADAM_B1) * g
            nv = ADAM_B2 * v_ref[0] + (1.0 - ADAM_B2) * (g * g)
            g_ref[0] = g
            nm_ref[0] = nm
            nv_ref[0] = nv
            d_ref[0] = -ADAM_LR * ((nm / bc1) / (jnp.sqrt(nv / bc2) + ADAM_EPS) + ADAM_WD * w_ref[0])

        if nl == 1:
            update(p_refs[0])
        else:
            for kk in range(nl):
                pl.when(pl.program_id(0) == kk)(lambda kk=kk: update(p_refs[kk]))

    part_spec = lambda kk: pl.BlockSpec((N_DEV, tr, c), lambda l, i: (0, jnp.where(l == kk, i, 0), 0))
    row = pl.BlockSpec((1, tr, c), lambda l, i: (l, i, 0))
    return pl.pallas_call(
        body, name="adamw", grid=(nl, r // tr),
        in_specs=[part_spec(kk) for kk in range(nl)] + [row, row, row],
        out_specs=[row] * 4,
        out_shape=[jax.ShapeDtypeStruct((nl, r, c), F32)] * 4,
        compiler_params=_params(2, 48),
    )(*parts, w, m, v)


def _pack(arrays):
    flat = jnp.concatenate([a.reshape(-1).astype(F32) for a in arrays])
    unit = 8 * PACK_COLS
    padded = -(-flat.shape[0] // unit) * unit
    return jnp.pad(flat, (0, padded - flat.shape[0])).reshape(-1, PACK_COLS)


def _unpack(packed, shapes):
    flat = packed.reshape(-1)
    out, off = [], 0
    for s in shapes:
        n = math.prod(s)
        out.append(flat[off:off + n].reshape(s))
        off += n
    return out


def _shard_last(arrays):
    cols = []
    for a in arrays:
        b = a.astype(F32).reshape(a.shape[:-1] + (N_DEV, a.shape[-1] // N_DEV))
        cols.append(jnp.moveaxis(b, -2, 0).reshape(N_DEV, -1))
    flat = jnp.concatenate(cols, axis=1)
    unit = 8 * PACK_COLS
    padded = -(-flat.shape[1] // unit) * unit
    return jnp.pad(flat, ((0, 0), (0, padded - flat.shape[1]))).reshape(N_DEV, -1, PACK_COLS)


def _unshard_last(packed, shard_shapes):
    flat = packed.reshape(N_DEV, -1)
    out, off = [], 0
    for s in shard_shapes:
        n = math.prod(s)
        a = jnp.moveaxis(flat[:, off:off + n].reshape((N_DEV,) + tuple(s)), 0, -2)
        out.append(a.reshape(tuple(s[:-1]) + (N_DEV * s[-1],)))
        off += n
    return out


def kernel(x, p, norm_g, w_in, w_out, conv_w, conv_b, conv_ln_g, conv_ln_b, sgu_ln_g, sgu_ln_b, sgu_w, sgu_b, pl_norm_g, pl_gate_w, pl_proj_w, final_g, loss_target, m_norm_g, m_w_in, m_w_out, m_conv_w, m_conv_b, m_conv_ln_g, m_conv_ln_b, m_sgu_ln_g, m_sgu_ln_b, m_sgu_w, m_sgu_b, m_pl_norm_g, m_pl_gate_w, m_pl_proj_w, m_final_g, v_norm_g, v_w_in, v_w_out, v_conv_w, v_conv_b, v_conv_ln_g, v_conv_ln_b, v_sgu_ln_g, v_sgu_ln_b, v_sgu_w, v_sgu_b, v_pl_norm_g, v_pl_gate_w, v_pl_proj_w, v_final_g):
    bsz, seq, d = x.shape
    t = bsz * seq
    depth = w_in.shape[0]
    e = w_out.shape[1] * N_DEV
    pd = p.shape[-1]
    n_conv, n_sgu = conv_w.shape[0], sgu_ln_g.shape[0]

    small_shapes = [conv_w.shape, sgu_ln_g.shape, sgu_ln_b.shape]
    cast = lambda a: a.astype(MXU_DTYPE)
    first = [cast(w_in[0]), cast(w_out[0]), cast(pl_gate_w[0]), cast(pl_proj_w), _pack([conv_w, sgu_ln_g, sgu_ln_b])]
    later = [[cast(w_in[l]), cast(w_out[l]), cast(pl_gate_w[l])] for l in range(1, depth)]
    gathered = _all_gather(first, "gather_weights")
    gather_pending, gather_tokens = {}, 0.0
    for l in range(1, depth):
        lands = _place_own(later[l - 1], False, "place_own_weights")
        send, recv, srcs, lnds, token = _exchange_start(later[l - 1], lands, gathered[0], False, f"gather_start_{l}")
        gather_pending[l] = (send, recv, srcs, lnds)
        gather_tokens = gather_tokens + token[0, 0]
    w_in_g = {0: gathered[0]}
    w_out_g = {0: gathered[1].reshape(e, d)}
    gate_g = {0: gathered[2].reshape(d, d)}
    proj_g = jnp.transpose(gathered[3], (1, 2, 0, 3)).reshape(depth, pd, d)
    conv_w_g, sgu_ln_g_g, sgu_ln_b_g = _unshard_last(gathered[4], small_shapes)
    sgu_wt = jnp.swapaxes(sgu_w, -1, -2)
    sgu_bt = jnp.swapaxes(sgu_b, -1, -2)

    xs = [x.reshape(t, d)]
    p_all = p.reshape(depth, t, pd)
    saved = []
    for l in range(depth):
        j = l // 2
        if l == 0:
            g_l = norm_g[0:1] + gather_tokens
        else:
            g_l = norm_g[l:l + 1]
            got = _exchange_wait(*gather_pending.pop(l), xs[-1], False, f"gather_wait_{l}")
            w_in_g[l], w_out_g[l], gate_g[l] = got[0], got[1].reshape(e, d), got[2].reshape(d, d)
        common = (xs[-1], p_all, l, g_l, w_in_g[l], w_out_g[l], pl_norm_g[l:l + 1], gate_g[l], proj_g[l], seq)
        if l % 2 == 0:
            h, proj, y1, y, x1, x2 = _layer_fwd(
                *common, conv=(conv_w_g[j], conv_b[j:j + 1], conv_ln_g[j:j + 1], conv_ln_b[j:j + 1]))
        else:
            y1 = None
            h, proj, y, x1, x2 = _layer_fwd(
                *common, sgu=(sgu_ln_g_g[j:j + 1], sgu_ln_b_g[j:j + 1], sgu_w[j], sgu_bt[j]))
        saved.append((h, proj, y1, y, x1))
        xs.append(x2)

    loss_part, dx, d_final_g = _loss_head(xs[-1], final_g.reshape(1, d), loss_target.reshape(t, d))
    loss = lax.psum(loss_part[0, 0], ("x", "y", "c"))

    d_norm_g, d_pl_norm_g = [None] * depth, [None] * depth
    scatter_pending = {}
    d_conv_w, d_conv_b, d_conv_ln_g, d_conv_ln_b = [None] * n_conv, [None] * n_conv, [None] * n_conv, [None] * n_conv
    d_sgu_ln_g, d_sgu_ln_b, d_sgu_w, d_sgu_b = [None] * n_sgu, [None] * n_sgu, [None] * n_sgu, [None] * n_sgu
    def scatter(parts, name):
        send, recv, srcs, lnds, token = _exchange_start(parts, _place_own(parts, True, "place_own_grads"), parts[0],
                                                        True, name)
        return (send, recv, srcs, lnds), token[0, 0]

    for l in reversed(range(depth)):
        j = l // 2
        h, proj, y1, y, x1 = saved[l]
        dx1, dgate_p, dprojw_p, d_pl_norm_g[l] = _ple_bwd(dx, x1, p_all, l, pl_norm_g[l:l + 1], gate_g[l], proj_g[l])
        dy, dz, dw_out_p = _outproj_bwd(dx1, y, proj, w_out_g[l])
        early = [dw_out_p.reshape(N_DEV, e // N_DEV, d), dgate_p.reshape(N_DEV, d // N_DEV, d), dprojw_p]
        early_token = 0.0
        if l == 0:
            scatter_pending["0_early"], early_token = scatter(early, "scatter_start_0_early")
            early = []
        if l % 2 == 0:
            dy1, d_conv_ln_g[j], d_conv_ln_b[j], d_conv_b[j] = _conv_ln_bwd(
                dy, y1, conv_ln_g[j:j + 1] + early_token, conv_ln_b[j:j + 1])
            dproj, d_conv_w[j] = _conv_bwd(dy1, proj, dz, conv_w_g[j], seq)
        else:
            dproj, d_sgu_w[j], d_sgu_b[j], d_sgu_ln_g[j], d_sgu_ln_b[j] = _sgu_bwd(
                dy, proj, dz, sgu_ln_g_g[j:j + 1] + early_token, sgu_ln_b_g[j:j + 1], sgu_w[j], sgu_wt[j], sgu_bt[j])
        scatter_pending[l], token = scatter([_inproj_bwd_w(h, dproj)] + early, f"scatter_start_{l}")
        dx, d_norm_g[l] = _inproj_bwd_x(dproj, w_in_g[l], dx1, xs[l], norm_g[l:l + 1] + token)
    grad_x = dx.reshape(bsz, seq, d)

    small_part = _shard_last([jnp.stack(d_conv_w), jnp.concatenate(d_sgu_ln_g, axis=0),
                              jnp.concatenate(d_sgu_ln_b, axis=0)])
    rep_part = _pack([jnp.concatenate(d_norm_g, axis=0), jnp.concatenate(d_conv_b, axis=0),
                      jnp.concatenate(d_conv_ln_g, axis=0), jnp.concatenate(d_conv_ln_b, axis=0),
                      jnp.stack(d_sgu_w), jnp.stack(d_sgu_b), jnp.concatenate(d_pl_norm_g, axis=0), d_final_g])
    small_send, small_recv, small_srcs, small_lnds, small_token = _exchange_start(
        [small_part], _place_own([small_part], True, "place_own_small"), grad_x, True, "scatter_small_start")
    rep_send, rep_recv, rep_srcs, rep_lnds, rep_token = _exchange_start(
        [rep_part], _place_own([rep_part], False, "place_own_replicated"), grad_x, False, "gather_replicated_start")

    landed = {}
    for key in list(scatter_pending):
        landed[key] = _exchange_wait(*scatter_pending.pop(key), small_token + rep_token, True, f"scatter_wait_{key}")
    dw_in_l = [landed[l][0] for l in range(depth)]
    rest = [landed["0_early"]] + [landed[l][1:] for l in range(1, depth)]

    o_w_in = _adamw(dw_in_l, w_in, m_w_in, v_w_in)
    o_w_out = _adamw([r[0] for r in rest], w_out, m_w_out, v_w_out)
    o_gate = _adamw([r[1] for r in rest], pl_gate_w, m_pl_gate_w, v_pl_gate_w)
    o_projw = _adamw([r[2] for r in rest], pl_proj_w, m_pl_proj_w, v_pl_proj_w)
    r_small = _exchange_wait(small_send, small_recv, small_srcs, small_lnds, o_projw[1], True, "scatter_small_wait")[0]
    r_rep = _exchange_wait(rep_send, rep_recv, rep_srcs, rep_lnds, o_w_in[1], False, "gather_replicated_wait")[0]

    def packed(parts, ws, ms, vs):
        shapes = [a.shape for a in ws]
        outs = _adamw([parts.reshape(N_DEV, -1, PACK_COLS)], _pack(ws)[None], _pack(ms)[None], _pack(vs)[None])
        return [_unpack(o[0], shapes) for o in outs]

    o_small = packed(r_small, [conv_w, sgu_ln_g, sgu_ln_b], [m_conv_w, m_sgu_ln_g, m_sgu_ln_b],
                     [v_conv_w, v_sgu_ln_g, v_sgu_ln_b])
    o_rep = packed(r_rep, [norm_g, conv_b, conv_ln_g, conv_ln_b, sgu_w, sgu_b, pl_norm_g, final_g],
                   [m_norm_g, m_conv_b, m_conv_ln_g, m_conv_ln_b, m_sgu_w, m_sgu_b, m_pl_norm_g, m_final_g],
                   [v_norm_g, v_conv_b, v_conv_ln_g, v_conv_ln_b, v_sgu_w, v_sgu_b, v_pl_norm_g, v_final_g])

    def leaf(kind):
        rep, small = o_rep[kind], o_small[kind]
        return [rep[0], o_w_in[kind], o_w_out[kind], small[0], rep[1], rep[2], rep[3], small[1], small[2], rep[4],
                rep[5], rep[6], o_gate[kind], o_projw[kind], rep[7]]

    return (loss, grad_x, *leaf(0), *leaf(1), *leaf(2), *leaf(3))
```

```python
import math

import jax
import jax.numpy as jnp
from jax import lax
from jax.experimental import pallas as pl
from jax.experimental.pallas import tpu as pltpu

F32 = jnp.float32
MXU_DTYPE = jnp.bfloat16
WIRE_DTYPE = jnp.bfloat16

EPS = 1e-6
CONV_K = 31
CHUNK = 128
GROUPS = 8
HALO = 32
N_DEV = 8
DEPTH = 4

ADAM_LR = 0.001
ADAM_B1 = 0.9
ADAM_B2 = 0.999
ADAM_EPS = 1e-08
ADAM_WD = 0.01
ADAM_STEP = 10

TM_IN = 512
TM_MIX = 256
TM_OUT = 512
FUSE_SB = 256
CONV_RC = 64
CONV_CC = 128
DCW_CC = 256
DCW_RC = 64
DCW_CHUNKS = 3
PACK_COLS = 1024

MESH_ID = pl.DeviceIdType.MESH
INV_SQRT2 = 1.0 / math.sqrt(2.0)
INV_SQRT_2PI = 1.0 / math.sqrt(2.0 * math.pi)


def _params(n_grid, vmem_mb):
    return pltpu.CompilerParams(dimension_semantics=("arbitrary",) * n_grid, vmem_limit_bytes=vmem_mb << 20)


def _whole(shape):
    nd = len(shape)
    return pl.BlockSpec(shape, lambda *_: (0,) * nd, pipeline_mode=pl.Buffered(1))


def _acc_out(shape):
    nd = len(shape)
    return pl.BlockSpec(shape, lambda *_: (0,) * nd)


def _dot(a, b):
    return jnp.dot(a.astype(MXU_DTYPE), b.astype(MXU_DTYPE), preferred_element_type=F32)


def _dot_nt(a, b):
    return lax.dot_general(a.astype(MXU_DTYPE), b.astype(MXU_DTYPE), (((1,), (1,)), ((), ())),
                           preferred_element_type=F32)


def _dot_tn(a, b):
    return lax.dot_general(a.astype(MXU_DTYPE), b.astype(MXU_DTYPE), (((0,), (0,)), ((), ())),
                           preferred_element_type=F32)


def _sigmoid(x):
    return jax.nn.sigmoid(x)


def _rms_rstd(x):
    return lax.rsqrt(jnp.mean(x * x, axis=-1, keepdims=True) + EPS)


def _rms_bwd(dy, x, rstd, g):
    gy = dy * g
    xr = x * rstd
    dx = rstd * (gy - xr * jnp.mean(gy * xr, axis=-1, keepdims=True))
    dg = jnp.sum(dy * xr, axis=0, keepdims=True)
    return dx, dg


def _ln_stats(x):
    mu = jnp.mean(x, axis=-1, keepdims=True)
    xc = x - mu
    var = jnp.mean(xc * xc, axis=-1, keepdims=True)
    rstd = lax.rsqrt(var + EPS)
    return xc * rstd, rstd


def _ln_bwd(dxhat, xhat, rstd):
    return rstd * (dxhat - jnp.mean(dxhat, axis=-1, keepdims=True)
                   - xhat * jnp.mean(dxhat * xhat, axis=-1, keepdims=True))


def _silu_grad(x, s):
    return s * (1.0 + x * (1.0 - s))


def _tril_mask():
    r = lax.broadcasted_iota(jnp.int32, (CHUNK, CHUNK), 0)
    c = lax.broadcasted_iota(jnp.int32, (CHUNK, CHUNK), 1)
    return r >= c


def _conv_weights_to_sublanes(w_ref, w8_ref):
    for k in range(CONV_K):
        w8_ref[k] = jnp.broadcast_to(w_ref[k:k + 1, :], w8_ref.shape[1:])


def _conv_apply(src_ref, w8_ref, zs_ref, base, tm, e, flip, emit):
    def row_block(i, carry):
        r0 = pl.multiple_of(i * CONV_RC, CONV_RC)
        for c0 in range(0, e, CONV_CC):
            cols = slice(c0, c0 + CONV_CC)
            acc = None
            for s in range(8):
                nrows = CONV_RC if s == 0 else CONV_RC + 8
                taps = [k for k in range(CONV_K) if (base + k) % 8 == s]
                off0 = base + taps[0] - s
                span = nrows + 8 * (len(taps) - 1)
                window = src_ref[pl.ds(r0 + off0, span), cols].reshape(span // 8, 8, CONV_CC)
                z = None
                for m, k in enumerate(taps):
                    wk = (CONV_K - 1 - k) if flip else k
                    term = w8_ref[wk, :, cols][None] * window[m:m + nrows // 8]
                    z = term if z is None else z + term
                z = z.reshape(nrows, CONV_CC)
                if s == 0:
                    acc = z
                else:
                    zs_ref[s - 1, pl.ds(0, nrows), :] = z
                    acc = acc + zs_ref[s - 1, pl.ds(s, CONV_RC), :]
            emit(r0, c0, acc)
        return carry

    lax.fori_loop(0, tm // CONV_RC, row_block, 0)


def _mesh_pos():
    return lax.axis_index("x"), lax.axis_index("y"), lax.axis_index("c")


def _slot(px, py, pc):
    return 4 * px + 2 * py + pc


def _peers(x, y, c):
    return [((1 - x) if (k & 4) else x, (1 - y) if (k & 2) else y, (1 - c) if (k & 1) else c)
            for k in range(1, N_DEV)]


HBM_SPEC = pl.BlockSpec(memory_space=pltpu.HBM)
SEM_SPEC = pl.BlockSpec(memory_space=pltpu.SEMAPHORE)
SIDE_EFFECT = pltpu.SideEffectType.DATAFLOW_SIDE_EFFECTING


def _exchange_copy(src_refs, land_refs, send_sems, recv_sems, i, k, peer, scatter, me):
    slot = _slot(*peer)
    return pltpu.make_async_remote_copy(
        src_ref=src_refs[i].at[slot] if scatter else src_refs[i],
        dst_ref=land_refs[i].at[me if me is not None else slot],
        send_sem=send_sems.at[i * 7 + k], recv_sem=recv_sems.at[i * 7 + k],
        device_id=peer, device_id_type=MESH_ID)


def _exchange_start(srcs, lands, after, scatter, name):
    n = len(srcs)

    def body(*refs):
        src_refs, land_refs = refs[:n], refs[n:2 * n]
        send_sems, recv_sems, token = refs[2 * n + 1], refs[2 * n + 2], refs[-1]
        x, y, c = _mesh_pos()
        me = _slot(x, y, c)
        for i in range(n):
            for k, peer in enumerate(_peers(x, y, c)):
                _exchange_copy(src_refs, land_refs, send_sems, recv_sems, i, k, peer, scatter, me).start()
        token[...] = jnp.zeros_like(token)

    arrays = list(srcs) + list(lands)
    outs = pl.pallas_call(
        body, name=name,
        out_shape=(pltpu.SemaphoreType.DMA((7 * n,)), pltpu.SemaphoreType.DMA((7 * n,)),
                   *[pltpu.HBM(a.shape, a.dtype) for a in arrays], jax.ShapeDtypeStruct((8, 128), F32)),
        in_specs=[HBM_SPEC] * (2 * n) + [pl.BlockSpec(memory_space=pl.ANY)],
        out_specs=(SEM_SPEC, SEM_SPEC, *[HBM_SPEC] * (2 * n), pl.BlockSpec(memory_space=pltpu.VMEM)),
        input_output_aliases={i: 2 + i for i in range(2 * n)},
        compiler_params=pltpu.CompilerParams(has_side_effects=SIDE_EFFECT),
    )(*[pltpu.with_memory_space_constraint(a, pltpu.HBM) for a in arrays], after)
    return outs[0], outs[1], list(outs[2:2 + n]), list(outs[2 + n:2 + 2 * n]), outs[-1]


def _exchange_wait(send_sems, recv_sems, srcs, lands, after, scatter, name):
    n = len(srcs)

    def body(*refs):
        src_refs, land_refs = refs[:n], refs[n:2 * n]
        send, recv = refs[2 * n], refs[2 * n + 1]
        x, y, c = _mesh_pos()
        for i in range(n):
            for k, peer in enumerate(_peers(x, y, c)):
                cp = _exchange_copy(src_refs, land_refs, send, recv, i, k, peer, scatter, None)
                cp.wait_send()
                cp.wait_recv()

    arrays = list(srcs) + list(lands)
    outs = pl.pallas_call(
        body, name=name,
        out_shape=tuple(pltpu.HBM(a.shape, a.dtype) for a in arrays),
        in_specs=[HBM_SPEC] * (2 * n) + [SEM_SPEC, SEM_SPEC, pl.BlockSpec(memory_space=pl.ANY)],
        out_specs=tuple([HBM_SPEC] * (2 * n)),
        input_output_aliases={i: i for i in range(2 * n)},
        compiler_params=pltpu.CompilerParams(has_side_effects=SIDE_EFFECT),
    )(*arrays, send_sems, recv_sems, after)
    return list(outs[n:])


def _place_own(parts, scatter, name):
    n = len(parts)
    me = jnp.reshape(_slot(*_mesh_pos()), (1,)).astype(jnp.int32)

    def body(me_ref, *refs):
        for i in range(n):
            refs[n + i][0] = refs[i][0] if scatter else refs[i][...]

    def slot_spec(shape):
        rest = len(shape)
        return pl.BlockSpec((1,) + tuple(shape), lambda i, me_ref: (me_ref[0],) + (0,) * rest)

    def whole_spec(shape):
        nd = len(shape)
        return pl.BlockSpec(tuple(shape), lambda i, me_ref: (0,) * nd)

    blocks = [a.shape[1:] if scatter else a.shape for a in parts]
    return pl.pallas_call(
        body, name=name,
        grid_spec=pltpu.PrefetchScalarGridSpec(
            num_scalar_prefetch=1, grid=(1,),
            in_specs=[slot_spec(b) if scatter else whole_spec(b) for b in blocks],
            out_specs=[slot_spec(b) for b in blocks]),
        out_shape=[jax.ShapeDtypeStruct((N_DEV,) + tuple(b), a.dtype) for a, b in zip(parts, blocks)],
        compiler_params=_params(1, 32),
    )(me, *parts)


def _all_gather(items, name):
    n = len(items)

    def body(*refs):
        in_refs, out_refs = refs[:n], refs[n:2 * n]
        send_sems, recv_sems, local_sems = refs[2 * n:]
        x, y, c = _mesh_pos()
        me, sibling = (x, y, c), (x, y, 1 - c)
        chips = [(1 - x, y), (x, 1 - y), (1 - x, 1 - y)]

        def copy(i, k, block, to, src=None):
            dst = out_refs[i].at[_slot(*block)]
            return pltpu.make_async_remote_copy(
                src_ref=dst if src is None else src, dst_ref=dst,
                send_sem=send_sems.at[i * 7 + k], recv_sem=recv_sems.at[i * 7 + k],
                device_id=to, device_id_type=MESH_ID)

        mine = [pltpu.make_async_copy(in_refs[i], out_refs[i].at[_slot(*me)], local_sems.at[i]) for i in range(n)]
        for cp in mine:
            cp.start()
        first = []
        for i in range(n):
            first.append(copy(i, 0, me, sibling, src=in_refs[i]))
            for j, chip in enumerate(chips):
                first.append(copy(i, 1 + j, me, (*chip, c), src=in_refs[i]))
        for cp in first:
            cp.start()
        passed = []
        for j, chip in enumerate(chips):
            for i in range(n):
                copy(i, 1 + j, (*chip, c), me).wait_recv()
                fwd = copy(i, 4 + j, (*chip, c), sibling)
                fwd.start()
                passed.append(fwd)
        for i in range(n):
            copy(i, 0, sibling, me).wait_recv()
            for j, chip in enumerate(chips):
                copy(i, 4 + j, (*chip, 1 - c), me).wait_recv()
        for cp in first + passed:
            cp.wait_send()
        for cp in mine:
            cp.wait()

    any_spec = pl.BlockSpec(memory_space=pl.ANY)
    return pl.pallas_call(
        body, name=name,
        out_shape=[jax.ShapeDtypeStruct((N_DEV,) + a.shape, a.dtype) for a in items],
        in_specs=[any_spec] * n, out_specs=[any_spec] * n,
        scratch_shapes=[pltpu.SemaphoreType.DMA((7 * n,)), pltpu.SemaphoreType.DMA((7 * n,)),
                        pltpu.SemaphoreType.DMA((n,))],
    )(*items)


def _layer_fwd(x, p_all, layer, g, w_blk, w_out, plg, gate_w, proj_w, seq, conv=None, sgu=None):
    t, d = x.shape
    nb, _, bn = w_blk.shape
    e = w_out.shape[0]
    pd = p_all.shape[-1]
    tm = TM_MIX
    nt = seq // tm
    is_conv = conv is not None
    mixer_args = conv if is_conv else sgu
    n_mix = len(mixer_args)

    def body(*refs):
        x_ref, p_ref, g_ref, w_ref, wo_ref, plg_ref, gw_ref, pw_ref = refs[:8]
        mix = refs[8:8 + n_mix]
        outs = refs[8 + n_mix:]
        if is_conv:
            cw_ref, cb_ref, lg_ref, lb_ref = mix
            h_ref, proj_ref, y1_ref, y_ref, x1_ref, x2_ref, y0s, zs, w8 = outs

            @pl.when(lax.rem(pl.program_id(0), nt) == 0)
            def _():
                y0s[pl.ds(0, HALO), :] = jnp.zeros((HALO, e), F32)
            _conv_weights_to_sublanes(cw_ref, w8)
        else:
            lg_ref, lb_ref, sw_ref, sbt_ref = mix
            h_ref, proj_ref, y_ref, x1_ref, x2_ref, mixed_s = outs

        for sb in range(tm // FUSE_SB):
            rows = pl.ds(sb * FUSE_SB, FUSE_SB)
            xv = x_ref[rows, :]
            hv = (xv * _rms_rstd(xv) * g_ref[...]).astype(MXU_DTYPE)
            h_ref[rows, :] = hv
            for j in range(nb):
                proj_ref[rows, j * bn:(j + 1) * bn] = jnp.dot(hv, w_ref[j], preferred_element_type=F32)
            if is_conv:
                y0s[pl.ds(HALO + sb * FUSE_SB, FUSE_SB), :] = proj_ref[rows, 0:e] * _sigmoid(proj_ref[rows, e:2 * e])

        if is_conv:
            def emit(r0, c0, acc):
                y1_ref[pl.ds(r0, CONV_RC), c0:c0 + CONV_CC] = acc + cb_ref[:, c0:c0 + CONV_CC]
            _conv_apply(y0s, w8, zs, HALO - (CONV_K - 1), tm, e, False, emit)

        for sb in range(tm // FUSE_SB):
            rows = pl.ds(sb * FUSE_SB, FUSE_SB)
            if is_conv:
                xhat, _ = _ln_stats(y1_ref[rows, :])
                y2 = xhat * lg_ref[...] + lb_ref[...]
                y = y2 * _sigmoid(y2)
            else:
                _, _, u, _, _, _ = _sgu_parts(proj_ref[rows, 0:e], proj_ref[rows, e:2 * e], lg_ref[...], lb_ref[...],
                                              sw_ref, sbt_ref, mixed_s, FUSE_SB, e)
                y = u * mixed_s[...]
            y_ref[rows, :] = y
            z = proj_ref[rows, 2 * e:3 * e]
            q = (y * (z * _sigmoid(z))).astype(MXU_DTYPE)
            x1 = x_ref[rows, :] + jnp.dot(q, wo_ref[...], preferred_element_type=F32)
            x1_ref[rows, :] = x1
            rn = x1 * _rms_rstd(x1) * plg_ref[...]
            gate = _sigmoid(_dot(rn, gw_ref[...]))
            x2_ref[rows, :] = x1 + gate * _dot(p_ref[0, rows, :], pw_ref[...])

        if is_conv:
            y0s[pl.ds(0, HALO), :] = y0s[pl.ds(tm, HALO), :]

    row = lambda w: pl.BlockSpec((tm, w), lambda i: (i, 0))
    f32 = lambda w: jax.ShapeDtypeStruct((t, w), F32)
    out_shape = [jax.ShapeDtypeStruct((t, d), MXU_DTYPE), f32(3 * e)] + ([f32(e)] if is_conv else []) + [f32(e), f32(d), f32(d)]
    out_specs = [row(d), row(3 * e)] + ([row(e)] if is_conv else []) + [row(e), row(d), row(d)]
    scratch = ([pltpu.VMEM((tm + HALO, e), F32), pltpu.VMEM((7, CONV_RC + 8, CONV_CC), F32),
                pltpu.VMEM((CONV_K, 8, e), F32)] if is_conv else [pltpu.VMEM((FUSE_SB, e), F32)])
    return pl.pallas_call(
        body, name="layer_fwd_conv" if is_conv else "layer_fwd_sgu", grid=(t // tm,),
        in_specs=[row(d), pl.BlockSpec((1, tm, pd), lambda i: (layer, i, 0)), _whole((1, d)), _whole(w_blk.shape),
                  _whole((e, d)), _whole((1, d)),
                  _whole((d, d)), _whole((pd, d))] + [_whole(a.shape) for a in mixer_args],
        out_specs=out_specs, out_shape=out_shape, scratch_shapes=scratch,
        compiler_params=_params(1, 60),
    )(x, p_all, g, w_blk, w_out, plg, gate_w, proj_w, *mixer_args)


def _sgu_parts(a, b, lg, lb, sw_ref, sbt_ref, mixed_s, tm, e):
    eg = e // GROUPS
    ea = lax.erf(a * INV_SQRT2)
    eb = lax.erf(b * INV_SQRT2)
    u = 0.5 * a * (1.0 + ea)
    v0 = 0.5 * b * (1.0 + eb)
    xhat, rstd = _ln_stats(v0)
    v = (xhat * lg + lb).astype(MXU_DTYPE)
    mask = _tril_mask()
    for g in range(GROUPS):
        wt = jnp.where(mask, sw_ref[g], 0.0).astype(MXU_DTYPE)
        bcol = sbt_ref[:, g:g + 1]
        for ch in range(tm // CHUNK):
            rows = slice(ch * CHUNK, (ch + 1) * CHUNK)
            cols = slice(g * eg, (g + 1) * eg)
            mixed_s[rows, cols] = jnp.dot(wt, v[rows, cols], preferred_element_type=F32) + bcol
    return ea, eb, u, xhat, rstd, v


def _loss_head(xf, fg, tgt):
    t, d = xf.shape
    tm = TM_OUT
    nsteps = t // tm

    def body(x_ref, g_ref, t_ref, loss_ref, dx_ref, dg_ref, sq_s):
        i = pl.program_id(0)

        @pl.when(i == 0)
        def _():
            sq_s[...] = jnp.zeros_like(sq_s)
            dg_ref[...] = jnp.zeros_like(dg_ref)
        x = x_ref[...]
        rstd = _rms_rstd(x)
        err = x * rstd * g_ref[...] - t_ref[...]
        sq_s[...] += jnp.sum(err * err, axis=0, keepdims=True)
        dx, dg = _rms_bwd(err * (1.0 / d), x, rstd, g_ref[...])
        dx_ref[...] = dx
        dg_ref[...] += dg

        @pl.when(i == nsteps - 1)
        def _():
            loss_ref[...] = jnp.sum(sq_s[...], axis=1, keepdims=True) * (0.5 / d)

    row = pl.BlockSpec((tm, d), lambda i: (i, 0))
    return pl.pallas_call(
        body, name="loss_head", grid=(nsteps,),
        in_specs=[row, _whole((1, d)), row],
        out_specs=[_acc_out((1, 1)), row, _acc_out((1, d))],
        out_shape=[jax.ShapeDtypeStruct((1, 1), F32), jax.ShapeDtypeStruct((t, d), F32),
                   jax.ShapeDtypeStruct((1, d), F32)],
        scratch_shapes=[pltpu.VMEM((1, d), F32)],
        compiler_params=_params(1, 32),
    )(xf, fg, tgt)


def _ple_bwd(dx2, x1, p_all, layer, plg, gate_w, proj_w):
    t, d = x1.shape
    pd = p_all.shape[-1]
    tm = TM_OUT
    nsteps = t // tm
    bn = d // N_DEV

    def body(dx2_ref, x1_ref, p_ref, plg_ref, gw_ref, pw_ref, dx1_ref, dgw_ref, dpw_ref, dplg_ref, gw_acc, pw_acc):
        i = pl.program_id(0)

        @pl.when(i == 0)
        def _():
            gw_acc[...] = jnp.zeros_like(gw_acc)
            pw_acc[...] = jnp.zeros_like(pw_acc)
            dplg_ref[...] = jnp.zeros_like(dplg_ref)
        dx2 = dx2_ref[...]
        x1 = x1_ref[...]
        plg = plg_ref[...]
        rstd = _rms_rstd(x1)
        rn = (x1 * rstd * plg).astype(MXU_DTYPE)
        gate = _sigmoid(jnp.dot(rn, gw_ref[...], preferred_element_type=F32))
        p_b = p_ref[0].astype(MXU_DTYPE)
        pp = jnp.dot(p_b, pw_ref[...], preferred_element_type=F32)
        dpp = (dx2 * gate).astype(MXU_DTYPE)
        dgpre = (dx2 * pp * gate * (1.0 - gate)).astype(MXU_DTYPE)
        pw_acc[...] += _dot_tn(p_b, dpp)
        gw_acc[...] += _dot_tn(rn, dgpre)
        drn = _dot_nt(dgpre, gw_ref[...])
        dx, dg = _rms_bwd(drn, x1, rstd, plg)
        dx1_ref[...] = dx2 + dx
        dplg_ref[...] += dg

        @pl.when(i == nsteps - 1)
        def _():
            dgw_ref[...] = gw_acc[...].astype(dgw_ref.dtype)
            for j in range(N_DEV):
                dpw_ref[j] = pw_acc[:, j * bn:(j + 1) * bn].astype(dpw_ref.dtype)

    row = lambda w: pl.BlockSpec((tm, w), lambda i: (i, 0))
    return pl.pallas_call(
        body, name="ple_bwd", grid=(nsteps,),
        in_specs=[row(d), row(d), pl.BlockSpec((1, tm, pd), lambda i: (layer, i, 0)), _whole((1, d)), _whole((d, d)),
                  _whole((pd, d))],
        out_specs=[row(d), _acc_out((d, d)), _acc_out((N_DEV, pd, bn)), _acc_out((1, d))],
        out_shape=[jax.ShapeDtypeStruct((t, d), F32), jax.ShapeDtypeStruct((d, d), WIRE_DTYPE),
                   jax.ShapeDtypeStruct((N_DEV, pd, bn), WIRE_DTYPE), jax.ShapeDtypeStruct((1, d), F32)],
        scratch_shapes=[pltpu.VMEM((d, d), F32), pltpu.VMEM((pd, d), F32)],
        compiler_params=_params(1, 48),
    )(dx2, x1, p_all, plg, gate_w, proj_w)


def _outproj_bwd(dx1, y, proj, w_out):
    t, d = dx1.shape
    e = y.shape[1]
    tm = TM_OUT
    nsteps = t // tm

    def body(dx1_ref, y_ref, z_ref, wo_ref, dy_ref, dz_ref, dwo_ref, wo_acc):
        i = pl.program_id(0)

        @pl.when(i == 0)
        def _():
            wo_acc[...] = jnp.zeros_like(wo_acc)
        dx1 = dx1_ref[...].astype(MXU_DTYPE)
        y = y_ref[...]
        z = z_ref[...]
        s = _sigmoid(z)
        sz = z * s
        q = (y * sz).astype(MXU_DTYPE)
        wo_acc[...] += _dot_tn(q, dx1)
        dq = _dot_nt(dx1, wo_ref[...])
        dy_ref[...] = dq * sz
        dz_ref[...] = (dq * y * _silu_grad(z, s)).astype(dz_ref.dtype)

        @pl.when(i == nsteps - 1)
        def _():
            dwo_ref[...] = wo_acc[...].astype(dwo_ref.dtype)

    return pl.pallas_call(
        body, name="outproj_bwd", grid=(nsteps,),
        in_specs=[pl.BlockSpec((tm, d), lambda i: (i, 0)), pl.BlockSpec((tm, e), lambda i: (i, 0)),
                  pl.BlockSpec((tm, e), lambda i: (i, 2)), _whole((e, d))],
        out_specs=[pl.BlockSpec((tm, e), lambda i: (i, 0)), pl.BlockSpec((tm, e), lambda i: (i, 0)),
                   _acc_out((e, d))],
        out_shape=[jax.ShapeDtypeStruct((t, e), F32), jax.ShapeDtypeStruct((t, e), MXU_DTYPE),
                   jax.ShapeDtypeStruct((e, d), WIRE_DTYPE)],
        scratch_shapes=[pltpu.VMEM((e, d), F32)],
        compiler_params=_params(1, 60),
    )(dx1, y, proj, w_out)


def _conv_ln_bwd(dy, y1, lg, lb):
    t, e = dy.shape
    tm = TM_MIX

    def body(dy_ref, y1_ref, lg_ref, lb_ref, dy1_ref, dlg_ref, dlb_ref, dcb_ref):
        @pl.when(pl.program_id(0) == 0)
        def _():
            dlg_ref[...] = jnp.zeros_like(dlg_ref)
            dlb_ref[...] = jnp.zeros_like(dlb_ref)
            dcb_ref[...] = jnp.zeros_like(dcb_ref)
        xhat, rstd = _ln_stats(y1_ref[...])
        lg = lg_ref[...]
        y2 = xhat * lg + lb_ref[...]
        dy2 = dy_ref[...] * _silu_grad(y2, _sigmoid(y2))
        dlg_ref[...] += jnp.sum(dy2 * xhat, axis=0, keepdims=True)
        dlb_ref[...] += jnp.sum(dy2, axis=0, keepdims=True)
        dy1 = _ln_bwd(dy2 * lg, xhat, rstd)
        dy1_ref[...] = dy1
        dcb_ref[...] += jnp.sum(dy1, axis=0, keepdims=True)

    row = pl.BlockSpec((tm, e), lambda i: (i, 0))
    return pl.pallas_call(
        body, name="conv_ln_bwd", grid=(t // tm,),
        in_specs=[row, row, _whole((1, e)), _whole((1, e))],
        out_specs=[row, _acc_out((1, e)), _acc_out((1, e)), _acc_out((1, e))],
        out_shape=[jax.ShapeDtypeStruct((t, e), F32)] + [jax.ShapeDtypeStruct((1, e), F32)] * 3,
        compiler_params=_params(1, 48),
    )(dy, y1, lg, lb)


def _conv_bwd(dy1, proj, dz, cw, seq):
    t, e = dy1.shape
    tm = TM_MIX
    nt = seq // tm
    hb = tm // HALO
    n_halo_blocks = t // HALO

    def body(d_ref, dn_ref, a_ref, b_ref, ah_ref, bh_ref, dz_ref, cw_ref, dproj_ref, dcw_ref,
             y0s, d1s, zs, dsh, dcw8, w8):
        i = pl.program_id(0)
        pos = lax.rem(i, nt)

        @pl.when(i == 0)
        def _():
            dcw8[...] = jnp.zeros_like(dcw8)
        _conv_weights_to_sublanes(cw_ref, w8)
        a = a_ref[...]
        sb = _sigmoid(b_ref[...])
        y0s[pl.ds(HALO, tm), :] = a * sb
        d1s[pl.ds(0, tm), :] = d_ref[...]

        @pl.when(pos == 0)
        def _():
            y0s[pl.ds(0, HALO), :] = jnp.zeros((HALO, e), F32)

        @pl.when(pos != 0)
        def _():
            y0s[pl.ds(0, HALO), :] = ah_ref[...] * _sigmoid(bh_ref[...])

        @pl.when(pos == nt - 1)
        def _():
            d1s[pl.ds(tm, HALO), :] = jnp.zeros((HALO, e), F32)

        @pl.when(pos != nt - 1)
        def _():
            d1s[pl.ds(tm, HALO), :] = dn_ref[...]

        base = HALO - (CONV_K - 1)
        for c0 in range(0, e, DCW_CC):
            cols = slice(c0, c0 + DCW_CC)
            dcur = d_ref[:, cols]
            for s in range(1, 8):
                dsh[s - 1, pl.ds(0, 8), :] = jnp.zeros((8, DCW_CC), F32)
                dsh[s - 1, pl.ds(tm, 8), :] = jnp.zeros((8, DCW_CC), F32)
                dsh[s - 1, pl.ds(s, tm), :] = dcur
            for s in range(8):
                taps = [k for k in range(CONV_K) if (base + k) % 8 == s]
                off0 = base + taps[0] - s
                n, ch = (tm, DCW_RC) if s == 0 else (tm + 8, (tm + 8) // DCW_CHUNKS)
                sums = [None] * len(taps)
                for r in range(0, n, ch):
                    dch = d_ref[r:r + ch, cols] if s == 0 else dsh[s - 1, r:r + ch, :]
                    window = y0s[pl.ds(off0 + r, ch + 8 * (len(taps) - 1)), cols]
                    for m in range(len(taps)):
                        part = jnp.sum((dch * window[8 * m:8 * m + ch]).reshape(ch // 8, 8, DCW_CC), axis=0)
                        sums[m] = part if sums[m] is None else sums[m] + part
                for m, k in enumerate(taps):
                    dcw8[k, :, cols] += sums[m]

        def emit(r0, c0, dy0):
            rs, cs = pl.ds(r0, CONV_RC), slice(c0, c0 + CONV_CC)
            sbv = _sigmoid(b_ref[rs, cs])
            av = a_ref[rs, cs]
            dproj_ref[rs, c0:c0 + CONV_CC] = (dy0 * sbv).astype(dproj_ref.dtype)
            dproj_ref[rs, e + c0:e + c0 + CONV_CC] = (dy0 * av * sbv * (1.0 - sbv)).astype(dproj_ref.dtype)
        _conv_apply(d1s, w8, zs, 0, tm, e, True, emit)
        dproj_ref[:, 2 * e:3 * e] = dz_ref[...]

        @pl.when(i == t // tm - 1)
        def _():
            dcw_ref[...] = jnp.sum(dcw8[...], axis=1)

    tile = lambda col: pl.BlockSpec((tm, e), lambda i: (i, col))
    prev = lambda col: pl.BlockSpec((HALO, e), lambda i: (jnp.maximum(i * hb - 1, 0), col))
    nxt = pl.BlockSpec((HALO, e), lambda i: (jnp.minimum((i + 1) * hb, n_halo_blocks - 1), 0))
    return pl.pallas_call(
        body, name="conv_bwd", grid=(t // tm,),
        in_specs=[tile(0), nxt, tile(0), tile(1), prev(0), prev(1), tile(0), _whole(cw.shape)],
        out_specs=[pl.BlockSpec((tm, 3 * e), lambda i: (i, 0)), _acc_out(cw.shape)],
        out_shape=[jax.ShapeDtypeStruct((t, 3 * e), MXU_DTYPE), jax.ShapeDtypeStruct(cw.shape, F32)],
        scratch_shapes=[pltpu.VMEM((tm + HALO, e), F32), pltpu.VMEM((tm + HALO, e), F32),
                        pltpu.VMEM((7, CONV_RC + 8, CONV_CC), F32), pltpu.VMEM((7, tm + 8, DCW_CC), F32),
                        pltpu.VMEM((CONV_K, 8, e), F32), pltpu.VMEM((CONV_K, 8, e), F32)],
        compiler_params=_params(1, 56),
    )(dy1, dy1, proj, proj, proj, proj, dz, cw)


def _sgu_bwd(dy, proj, dz, lg, lb, sw, swt, sbt):
    t, e = dy.shape
    eg = e // GROUPS
    tm = TM_MIX
    nsteps = t // tm

    def body(dy_ref, a_ref, b_ref, dz_ref, lg_ref, lb_ref, sw_ref, swt_ref, sbt_ref,
             dproj_ref, dsw_ref, dsb_ref, dlg_ref, dlb_ref, mixed_s, dv_s, sb_acc):
        i = pl.program_id(0)

        @pl.when(i == 0)
        def _():
            dsw_ref[...] = jnp.zeros_like(dsw_ref)
            sb_acc[...] = jnp.zeros_like(sb_acc)
            dlg_ref[...] = jnp.zeros_like(dlg_ref)
            dlb_ref[...] = jnp.zeros_like(dlb_ref)
        a = a_ref[...]
        b = b_ref[...]
        lg = lg_ref[...]
        ea, eb, u, xhat, rstd, v = _sgu_parts(a, b, lg, lb_ref[...], sw_ref, sbt_ref, mixed_s, tm, e)
        dy = dy_ref[...]
        du = dy * mixed_s[...]
        dmixed = (dy * u).astype(MXU_DTYPE)
        mask = _tril_mask()
        mask_t = (lax.broadcasted_iota(jnp.int32, (CHUNK, CHUNK), 0)
                  <= lax.broadcasted_iota(jnp.int32, (CHUNK, CHUNK), 1))
        ones = jnp.ones((8, eg), MXU_DTYPE)
        for g in range(GROUPS):
            wtt = jnp.where(mask_t, swt_ref[g], 0.0).astype(MXU_DTYPE)
            cols = slice(g * eg, (g + 1) * eg)
            for ch in range(tm // CHUNK):
                rows = slice(ch * CHUNK, (ch + 1) * CHUNK)
                dm = dmixed[rows, cols]
                dv_s[rows, cols] = jnp.dot(wtt, dm, preferred_element_type=F32)
                dsw_ref[g] += _dot_nt(dm, v[rows, cols])
                sb_acc[g] += _dot_nt(ones, dm)
        dv = dv_s[...]
        dlg_ref[...] += jnp.sum(dv * xhat, axis=0, keepdims=True)
        dlb_ref[...] += jnp.sum(dv, axis=0, keepdims=True)
        dv0 = _ln_bwd(dv * lg, xhat, rstd)
        pdf_a = jnp.exp(-0.5 * a * a) * INV_SQRT_2PI
        pdf_b = jnp.exp(-0.5 * b * b) * INV_SQRT_2PI
        dproj_ref[:, 0:e] = (du * (0.5 * (1.0 + ea) + a * pdf_a)).astype(dproj_ref.dtype)
        dproj_ref[:, e:2 * e] = (dv0 * (0.5 * (1.0 + eb) + b * pdf_b)).astype(dproj_ref.dtype)
        dproj_ref[:, 2 * e:3 * e] = dz_ref[...]

        @pl.when(i == nsteps - 1)
        def _():
            for g in range(GROUPS):
                dsw_ref[g] = jnp.where(mask, dsw_ref[g], 0.0)
                dsb_ref[g:g + 1, :] = sb_acc[g, 0:1, :]

    tile = lambda col: pl.BlockSpec((tm, e), lambda i: (i, col))
    return pl.pallas_call(
        body, name="sgu_bwd", grid=(nsteps,),
        in_specs=[tile(0), tile(0), tile(1), tile(0), _whole((1, e)), _whole((1, e)), _whole(sw.shape),
                  _whole(swt.shape), _whole(sbt.shape)],
        out_specs=[pl.BlockSpec((tm, 3 * e), lambda i: (i, 0)), _acc_out(sw.shape), _acc_out((GROUPS, CHUNK)),
                   _acc_out((1, e)), _acc_out((1, e))],
        out_shape=[jax.ShapeDtypeStruct((t, 3 * e), MXU_DTYPE), jax.ShapeDtypeStruct(sw.shape, F32),
                   jax.ShapeDtypeStruct((GROUPS, CHUNK), F32), jax.ShapeDtypeStruct((1, e), F32),
                   jax.ShapeDtypeStruct((1, e), F32)],
        scratch_shapes=[pltpu.VMEM((tm, e), F32), pltpu.VMEM((tm, e), F32), pltpu.VMEM((GROUPS, 8, CHUNK), F32)],
        compiler_params=_params(1, 56),
    )(dy, proj, proj, dz, lg, lb, sw, swt, sbt)


def _inproj_bwd_x(dproj, w_blk, dx1, x, g):
    t, d = x.shape
    nb, _, bn = w_blk.shape
    tm = TM_IN

    def body(dp_ref, w_ref, dx1_ref, x_ref, g_ref, dx_ref, dg_ref):
        @pl.when(pl.program_id(0) == 0)
        def _():
            dg_ref[...] = jnp.zeros_like(dg_ref)
        dh = None
        for j in range(nb):
            term = _dot_nt(dp_ref[:, j * bn:(j + 1) * bn], w_ref[j])
            dh = term if dh is None else dh + term
        xv = x_ref[...]
        dx, dg = _rms_bwd(dh, xv, _rms_rstd(xv), g_ref[...])
        dx_ref[...] = dx1_ref[...] + dx
        dg_ref[...] += dg

    row = lambda w: pl.BlockSpec((tm, w), lambda i: (i, 0))
    return pl.pallas_call(
        body, name="inproj_bwd_x", grid=(t // tm,),
        in_specs=[row(nb * bn), _whole(w_blk.shape), row(d), row(d), _whole((1, d))],
        out_specs=[row(d), _acc_out((1, d))],
        out_shape=[jax.ShapeDtypeStruct((t, d), F32), jax.ShapeDtypeStruct((1, d), F32)],
        compiler_params=_params(1, 56),
    )(dproj, w_blk, dx1, x, g)


def _inproj_bwd_w(h, dproj):
    t, d = h.shape
    bn = dproj.shape[1] // N_DEV
    tm = TM_IN
    nsteps = t // tm

    nh = 2
    per = N_DEV // nh

    def body(h_ref, dp_ref, dw_ref, acc):
        i = pl.program_id(1)

        @pl.when(i == 0)
        def _():
            acc[...] = jnp.zeros_like(acc)
        hv = h_ref[...]
        for jj in range(per):
            acc[jj] += _dot_tn(hv, dp_ref[:, jj * bn:(jj + 1) * bn])

        @pl.when(i == nsteps - 1)
        def _():
            dw_ref[...] = acc[...].astype(dw_ref.dtype)

    return pl.pallas_call(
        body, name="inproj_bwd_w", grid=(nh, nsteps),
        in_specs=[pl.BlockSpec((tm, d), lambda hh, i: (i, 0)), pl.BlockSpec((tm, per * bn), lambda hh, i: (i, hh))],
        out_specs=[pl.BlockSpec((per, d, bn), lambda hh, i: (hh, 0, 0))],
        out_shape=[jax.ShapeDtypeStruct((N_DEV, d, bn), WIRE_DTYPE)],
        scratch_shapes=[pltpu.VMEM((per, d, bn), F32)],
        compiler_params=_params(2, 56),
    )(h, dproj)[0]


def _adamw(parts, w, m, v):
    nl, r, c = w.shape
    tr = r
    for cand in (512, 256, 128, 64, 32, 16, 8):
        if r % cand == 0 and cand * c * 4 <= (1 << 19):
            tr = cand
            break
    bc1 = 1.0 - ADAM_B1 ** ADAM_STEP
    bc2 = 1.0 - ADAM_B2 ** ADAM_STEP

    def body(*refs):
        p_refs = refs[:nl]
        w_ref, m_ref, v_ref, g_ref, d_ref, nm_ref, nv_ref = refs[nl:]

        def update(p_ref):
            g = p_ref[0].astype(F32)
            for s in range(1, N_DEV):
                g = g + p_ref[s].astype(F32)
            nm = ADAM_B1 * m_ref[0] + (1.0 - ADAM_B1) * g
            nv = ADAM_B2 * v_ref[0] + (1.0 - ADAM_B2) * (g * g)
            g_ref[0] = g
            nm_ref[0] = nm
            nv_ref[0] = nv
            d_ref[0] = -ADAM_LR * ((nm / bc1) / (jnp.sqrt(nv / bc2) + ADAM_EPS) + ADAM_WD * w_ref[0])

        if nl == 1:
            update(p_refs[0])
        else:
            for kk in range(nl):
                pl.when(pl.program_id(0) == kk)(lambda kk=kk: update(p_refs[kk]))

    part_spec = lambda kk: pl.BlockSpec((N_DEV, tr, c), lambda l, i: (0, jnp.where(l == kk, i, 0), 0))
    row = pl.BlockSpec((1, tr, c), lambda l, i: (l, i, 0))
    return pl.pallas_call(
        body, name="adamw", grid=(nl, r // tr),
        in_specs=[part_spec(kk) for kk in range(nl)] + [row, row, row],
        out_specs=[row] * 4,
        out_shape=[jax.ShapeDtypeStruct((nl, r, c), F32)] * 4,
        compiler_params=_params(2, 48),
    )(*parts, w, m, v)


def _pack(arrays):
    flat = jnp.concatenate([a.reshape(-1).astype(F32) for a in arrays])
    unit = 8 * PACK_COLS
    padded = -(-flat.shape[0] // unit) * unit
    return jnp.pad(flat, (0, padded - flat.shape[0])).reshape(-1, PACK_COLS)


def _unpack(packed, shapes):
    flat = packed.reshape(-1)
    out, off = [], 0
    for s in shapes:
        n = math.prod(s)
        out.append(flat[off:off + n].reshape(s))
        off += n
    return out


def _shard_last(arrays):
    cols = []
    for a in arrays:
        b = a.astype(F32).reshape(a.shape[:-1] + (N_DEV, a.shape[-1] // N_DEV))
        cols.append(jnp.moveaxis(b, -2, 0).reshape(N_DEV, -1))
    flat = jnp.concatenate(cols, axis=1)
    unit = 8 * PACK_COLS
    padded = -(-flat.shape[1] // unit) * unit
    return jnp.pad(flat, ((0, 0), (0, padded - flat.shape[1]))).reshape(N_DEV, -1, PACK_COLS)


def _unshard_last(packed, shard_shapes):
    flat = packed.reshape(N_DEV, -1)
    out, off = [], 0
    for s in shard_shapes:
        n = math.prod(s)
        a = jnp.moveaxis(flat[:, off:off + n].reshape((N_DEV,) + tuple(s)), 0, -2)
        out.append(a.reshape(tuple(s[:-1]) + (N_DEV * s[-1],)))
        off += n
    return out


def kernel(x, p, norm_g, w_in, w_out, conv_w, conv_b, conv_ln_g, conv_ln_b, sgu_ln_g, sgu_ln_b, sgu_w, sgu_b, pl_norm_g, pl_gate_w, pl_proj_w, final_g, loss_target, m_norm_g, m_w_in, m_w_out, m_conv_w, m_conv_b, m_conv_ln_g, m_conv_ln_b, m_sgu_ln_g, m_sgu_ln_b, m_sgu_w, m_sgu_b, m_pl_norm_g, m_pl_gate_w, m_pl_proj_w, m_final_g, v_norm_g, v_w_in, v_w_out, v_conv_w, v_conv_b, v_conv_ln_g, v_conv_ln_b, v_sgu_ln_g, v_sgu_ln_b, v_sgu_w, v_sgu_b, v_pl_norm_g, v_pl_gate_w, v_pl_proj_w, v_final_g):
    bsz, seq, d = x.shape
    t = bsz * seq
    depth = w_in.shape[0]
    e = w_out.shape[1] * N_DEV
    pd = p.shape[-1]
    n_conv, n_sgu = conv_w.shape[0], sgu_ln_g.shape[0]

    small_shapes = [conv_w.shape, sgu_ln_g.shape, sgu_ln_b.shape]
    cast = lambda a: a.astype(MXU_DTYPE)
    first = [cast(w_in[0]), cast(w_out[0]), cast(pl_gate_w[0]), cast(pl_proj_w), _pack([conv_w, sgu_ln_g, sgu_ln_b])]
    later = [[cast(w_in[l]), cast(w_out[l]), cast(pl_gate_w[l])] for l in range(1, depth)]
    gathered = _all_gather(first, "gather_weights")
    gather_pending, gather_tokens = {}, 0.0
    for l in range(1, depth):
        lands = _place_own(later[l - 1], False, "place_own_weights")
        send, recv, srcs, lnds, token = _exchange_start(later[l - 1], lands, gathered[0], False, f"gather_start_{l}")
        gather_pending[l] = (send, recv, srcs, lnds)
        gather_tokens = gather_tokens + token[0, 0]
    w_in_g = {0: gathered[0]}
    w_out_g = {0: gathered[1].reshape(e, d)}
    gate_g = {0: gathered[2].reshape(d, d)}
    proj_g = jnp.transpose(gathered[3], (1, 2, 0, 3)).reshape(depth, pd, d)
    conv_w_g, sgu_ln_g_g, sgu_ln_b_g = _unshard_last(gathered[4], small_shapes)
    sgu_wt = jnp.swapaxes(sgu_w, -1, -2)
    sgu_bt = jnp.swapaxes(sgu_b, -1, -2)

    xs = [x.reshape(t, d)]
    p_all = p.reshape(depth, t, pd)
    saved = []
    for l in range(depth):
        j = l // 2
        if l == 0:
            g_l = norm_g[0:1] + gather_tokens
        else:
            g_l = norm_g[l:l + 1]
            got = _exchange_wait(*gather_pending.pop(l), xs[-1], False, f"gather_wait_{l}")
            w_in_g[l], w_out_g[l], gate_g[l] = got[0], got[1].reshape(e, d), got[2].reshape(d, d)
        common = (xs[-1], p_all, l, g_l, w_in_g[l], w_out_g[l], pl_norm_g[l:l + 1], gate_g[l], proj_g[l], seq)
        if l % 2 == 0:
            h, proj, y1, y, x1, x2 = _layer_fwd(
                *common, conv=(conv_w_g[j], conv_b[j:j + 1], conv_ln_g[j:j + 1], conv_ln_b[j:j + 1]))
        else:
            y1 = None
            h, proj, y, x1, x2 = _layer_fwd(
                *common, sgu=(sgu_ln_g_g[j:j + 1], sgu_ln_b_g[j:j + 1], sgu_w[j], sgu_bt[j]))
        saved.append((h, proj, y1, y, x1))
        xs.append(x2)

    loss_part, dx, d_final_g = _loss_head(xs[-1], final_g.reshape(1, d), loss_target.reshape(t, d))
    loss = lax.psum(loss_part[0, 0], ("x", "y", "c"))

    d_norm_g, d_pl_norm_g = [None] * depth, [None] * depth
    scatter_pending = {}
    d_conv_w, d_conv_b, d_conv_ln_g, d_conv_ln_b = [None] * n_conv, [None] * n_conv, [None] * n_conv, [None] * n_conv
    d_sgu_ln_g, d_sgu_ln_b, d_sgu_w, d_sgu_b = [None] * n_sgu, [None] * n_sgu, [None] * n_sgu, [None] * n_sgu
    def scatter(parts, name):
        send, recv, srcs, lnds, token = _exchange_start(parts, _place_own(parts, True, "place_own_grads"), parts[0],
                                                        True, name)
        return (send, recv, srcs, lnds), token[0, 0]

    for l in reversed(range(depth)):
        j = l // 2
        h, proj, y1, y, x1 = saved[l]
        dx1, dgate_p, dprojw_p, d_pl_norm_g[l] = _ple_bwd(dx, x1, p_all, l, pl_norm_g[l:l + 1], gate_g[l], proj_g[l])
        dy, dz, dw_out_p = _outproj_bwd(dx1, y, proj, w_out_g[l])
        early = [dw_out_p.reshape(N_DEV, e // N_DEV, d), dgate_p.reshape(N_DEV, d // N_DEV, d), dprojw_p]
        early_token = 0.0
        if l == 0:
            scatter_pending["0_early"], early_token = scatter(early, "scatter_start_0_early")
            early = []
        if l % 2 == 0:
            dy1, d_conv_ln_g[j], d_conv_ln_b[j], d_conv_b[j] = _conv_ln_bwd(
                dy, y1, conv_ln_g[j:j + 1] + early_token, conv_ln_b[j:j + 1])
            dproj, d_conv_w[j] = _conv_bwd(dy1, proj, dz, conv_w_g[j], seq)
        else:
            dproj, d_sgu_w[j], d_sgu_b[j], d_sgu_ln_g[j], d_sgu_ln_b[j] = _sgu_bwd(
                dy, proj, dz, sgu_ln_g_g[j:j + 1] + early_token, sgu_ln_b_g[j:j + 1], sgu_w[j], sgu_wt[j], sgu_bt[j])
        scatter_pending[l], token = scatter([_inproj_bwd_w(h, dproj)] + early, f"scatter_start_{l}")
        dx, d_norm_g[l] = _inproj_bwd_x(dproj, w_in_g[l], dx1, xs[l], norm_g[l:l + 1] + token)
    grad_x = dx.reshape(bsz, seq, d)

    small_part = _shard_last([jnp.stack(d_conv_w), jnp.concatenate(d_sgu_ln_g, axis=0),
                              jnp.concatenate(d_sgu_ln_b, axis=0)])
    rep_part = _pack([jnp.concatenate(d_norm_g, axis=0), jnp.concatenate(d_conv_b, axis=0),
                      jnp.concatenate(d_conv_ln_g, axis=0), jnp.concatenate(d_conv_ln_b, axis=0),
                      jnp.stack(d_sgu_w), jnp.stack(d_sgu_b), jnp.concatenate(d_pl_norm_g, axis=0), d_final_g])
    small_send, small_recv, small_srcs, small_lnds, small_token = _exchange_start(
        [small_part], _place_own([small_part], True, "place_own_small"), grad_x, True, "scatter_small_start")
    rep_send, rep_recv, rep_srcs, rep_lnds, rep_token = _exchange_start(
        [rep_part], _place_own([rep_part], False, "place_own_replicated"), grad_x, False, "gather_replicated_start")

    landed = {}
    for key in list(scatter_pending):
        landed[key] = _exchange_wait(*scatter_pending.pop(key), small_token + rep_token, True, f"scatter_wait_{key}")
    dw_in_l = [landed[l][0] for l in range(depth)]
    rest = [landed["0_early"]] + [landed[l][1:] for l in range(1, depth)]

    o_w_in = _adamw(dw_in_l, w_in, m_w_in, v_w_in)
    o_w_out = _adamw([r[0] for r in rest], w_out, m_w_out, v_w_out)
    o_gate = _adamw([r[1] for r in rest], pl_gate_w, m_pl_gate_w, v_pl_gate_w)
    o_projw = _adamw([r[2] for r in rest], pl_proj_w, m_pl_proj_w, v_pl_proj_w)
    r_small = _exchange_wait(small_send, small_recv, small_srcs, small_lnds, o_projw[1], True, "scatter_small_wait")[0]
    r_rep = _exchange_wait(rep_send, rep_recv, rep_srcs, rep_lnds, o_w_in[1], False, "gather_replicated_wait")[0]

    def packed(parts, ws, ms, vs):
        shapes = [a.shape for a in ws]
        outs = _adamw([parts.reshape(N_DEV, -1, PACK_COLS)], _pack(ws)[None], _pack(ms)[None], _pack(vs)[None])
        return [_unpack(o[0], shapes) for o in outs]

    o_small = packed(r_small, [conv_w, sgu_ln_g, sgu_ln_b], [m_conv_w, m_sgu_ln_g, m_sgu_ln_b],
                     [v_conv_w, v_sgu_ln_g, v_sgu_ln_b])
    o_rep = packed(r_rep, [norm_g, conv_b, conv_ln_g, conv_ln_b, sgu_w, sgu_b, pl_norm_g, final_g],
                   [m_norm_g, m_conv_b, m_conv_ln_g, m_conv_ln_b, m_sgu_w, m_sgu_b, m_pl_norm_g, m_final_g],
                   [v_norm_g, v_conv_b, v_conv_ln_g, v_conv_ln_b, v_sgu_w, v_sgu_b, v_pl_norm_g, v_final_g])

    def leaf(kind):
        rep, small = o_rep[kind], o_small[kind]
        return [rep[0], o_w_in[kind], o_w_out[kind], small[0], rep[1], rep[2], rep[3], small[1], small[2], rep[4],
                rep[5], rep[6], o_gate[kind], o_projw[kind], rep[7]]

    return (loss, grad_x, *leaf(0), *leaf(1), *leaf(2), *leaf(3))
```

```python
import math

import jax
import jax.numpy as jnp
from jax import lax
from jax.experimental import pallas as pl
from jax.experimental.pallas import tpu as pltpu

F32 = jnp.float32
MXU_DTYPE = jnp.bfloat16
WIRE_DTYPE = jnp.bfloat16

EPS = 1e-6
CONV_K = 31
CHUNK = 128
GROUPS = 8
HALO = 32
N_DEV = 8
DEPTH = 4

ADAM_LR = 0.001
ADAM_B1 = 0.9
ADAM_B2 = 0.999
ADAM_EPS = 1e-08
ADAM_WD = 0.01
ADAM_STEP = 10

TM_IN = 512
TM_MIX = 256
TM_OUT = 512
FUSE_SB = 256
CONV_RC = 64
CONV_CC = 128
DCW_CC = 256
DCW_RC = 64
DCW_CHUNKS = 3
PACK_COLS = 1024

MESH_ID = pl.DeviceIdType.MESH
INV_SQRT2 = 1.0 / math.sqrt(2.0)
INV_SQRT_2PI = 1.0 / math.sqrt(2.0 * math.pi)


def _params(n_grid, vmem_mb):
    return pltpu.CompilerParams(dimension_semantics=("arbitrary",) * n_grid, vmem_limit_bytes=vmem_mb << 20)


def _whole(shape):
    nd = len(shape)
    return pl.BlockSpec(shape, lambda *_: (0,) * nd, pipeline_mode=pl.Buffered(1))


def _acc_out(shape):
    nd = len(shape)
    return pl.BlockSpec(shape, lambda *_: (0,) * nd)


def _dot(a, b):
    return jnp.dot(a.astype(MXU_DTYPE), b.astype(MXU_DTYPE), preferred_element_type=F32)


def _dot_nt(a, b):
    return lax.dot_general(a.astype(MXU_DTYPE), b.astype(MXU_DTYPE), (((1,), (1,)), ((), ())),
                           preferred_element_type=F32)


def _dot_tn(a, b):
    return lax.dot_general(a.astype(MXU_DTYPE), b.astype(MXU_DTYPE), (((0,), (0,)), ((), ())),
                           preferred_element_type=F32)


def _sigmoid(x):
    return jax.nn.sigmoid(x)


def _rms_rstd(x):
    return lax.rsqrt(jnp.mean(x * x, axis=-1, keepdims=True) + EPS)


def _rms_bwd(dy, x, rstd, g):
    gy = dy * g
    xr = x * rstd
    dx = rstd * (gy - xr * jnp.mean(gy * xr, axis=-1, keepdims=True))
    dg = jnp.sum(dy * xr, axis=0, keepdims=True)
    return dx, dg


def _ln_stats(x):
    mu = jnp.mean(x, axis=-1, keepdims=True)
    xc = x - mu
    var = jnp.mean(xc * xc, axis=-1, keepdims=True)
    rstd = lax.rsqrt(var + EPS)
    return xc * rstd, rstd


def _ln_bwd(dxhat, xhat, rstd):
    return rstd * (dxhat - jnp.mean(dxhat, axis=-1, keepdims=True)
                   - xhat * jnp.mean(dxhat * xhat, axis=-1, keepdims=True))


def _silu_grad(x, s):
    return s * (1.0 + x * (1.0 - s))


def _tril_mask():
    r = lax.broadcasted_iota(jnp.int32, (CHUNK, CHUNK), 0)
    c = lax.broadcasted_iota(jnp.int32, (CHUNK, CHUNK), 1)
    return r >= c


def _conv_weights_to_sublanes(w_ref, w8_ref):
    for k in range(CONV_K):
        w8_ref[k] = jnp.broadcast_to(w_ref[k:k + 1, :], w8_ref.shape[1:])


def _conv_apply(src_ref, w8_ref, zs_ref, base, tm, e, flip, emit):
    def row_block(i, carry):
        r0 = pl.multiple_of(i * CONV_RC, CONV_RC)
        for c0 in range(0, e, CONV_CC):
            cols = slice(c0, c0 + CONV_CC)
            acc = None
            for s in range(8):
                nrows = CONV_RC if s == 0 else CONV_RC + 8
                taps = [k for k in range(CONV_K) if (base + k) % 8 == s]
                off0 = base + taps[0] - s
                span = nrows + 8 * (len(taps) - 1)
                window = src_ref[pl.ds(r0 + off0, span), cols].reshape(span // 8, 8, CONV_CC)
                z = None
                for m, k in enumerate(taps):
                    wk = (CONV_K - 1 - k) if flip else k
                    term = w8_ref[wk, :, cols][None] * window[m:m + nrows // 8]
                    z = term if z is None else z + term
                z = z.reshape(nrows, CONV_CC)
                if s == 0:
                    acc = z
                else:
                    zs_ref[s - 1, pl.ds(0, nrows), :] = z
                    acc = acc + zs_ref[s - 1, pl.ds(s, CONV_RC), :]
            emit(r0, c0, acc)
        return carry

    lax.fori_loop(0, tm // CONV_RC, row_block, 0)


def _mesh_pos():
    return lax.axis_index("x"), lax.axis_index("y"), lax.axis_index("c")


def _slot(px, py, pc):
    return 4 * px + 2 * py + pc


def _peers(x, y, c):
    return [((1 - x) if (k & 4) else x, (1 - y) if (k & 2) else y, (1 - c) if (k & 1) else c)
            for k in range(1, N_DEV)]


HBM_SPEC = pl.BlockSpec(memory_space=pltpu.HBM)
SEM_SPEC = pl.BlockSpec(memory_space=pltpu.SEMAPHORE)
SIDE_EFFECT = pltpu.SideEffectType.DATAFLOW_SIDE_EFFECTING


def _exchange_copy(src_refs, land_refs, send_sems, recv_sems, i, k, peer, scatter, me):
    slot = _slot(*peer)
    return pltpu.make_async_remote_copy(
        src_ref=src_refs[i].at[slot] if scatter else src_refs[i],
        dst_ref=land_refs[i].at[me if me is not None else slot],
        send_sem=send_sems.at[i * 7 + k], recv_sem=recv_sems.at[i * 7 + k],
        device_id=peer, device_id_type=MESH_ID)


def _exchange_start(srcs, lands, after, scatter, name):
    n = len(srcs)

    def body(*refs):
        src_refs, land_refs = refs[:n], refs[n:2 * n]
        send_sems, recv_sems, token = refs[2 * n + 1], refs[2 * n + 2], refs[-1]
        x, y, c = _mesh_pos()
        me = _slot(x, y, c)
        for i in range(n):
            for k, peer in enumerate(_peers(x, y, c)):
                _exchange_copy(src_refs, land_refs, send_sems, recv_sems, i, k, peer, scatter, me).start()
        token[...] = jnp.zeros_like(token)

    arrays = list(srcs) + list(lands)
    outs = pl.pallas_call(
        body, name=name,
        out_shape=(pltpu.SemaphoreType.DMA((7 * n,)), pltpu.SemaphoreType.DMA((7 * n,)),
                   *[pltpu.HBM(a.shape, a.dtype) for a in arrays], jax.ShapeDtypeStruct((8, 128), F32)),
        in_specs=[HBM_SPEC] * (2 * n) + [pl.BlockSpec(memory_space=pl.ANY)],
        out_specs=(SEM_SPEC, SEM_SPEC, *[HBM_SPEC] * (2 * n), pl.BlockSpec(memory_space=pltpu.VMEM)),
        input_output_aliases={i: 2 + i for i in range(2 * n)},
        compiler_params=pltpu.CompilerParams(has_side_effects=SIDE_EFFECT),
    )(*[pltpu.with_memory_space_constraint(a, pltpu.HBM) for a in arrays], after)
    return outs[0], outs[1], list(outs[2:2 + n]), list(outs[2 + n:2 + 2 * n]), outs[-1]


def _exchange_wait(send_sems, recv_sems, srcs, lands, after, scatter, name):
    n = len(srcs)

    def body(*refs):
        src_refs, land_refs = refs[:n], refs[n:2 * n]
        send, recv = refs[2 * n], refs[2 * n + 1]
        x, y, c = _mesh_pos()
        for i in range(n):
            for k, peer in enumerate(_peers(x, y, c)):
                cp = _exchange_copy(src_refs, land_refs, send, recv, i, k, peer, scatter, None)
                cp.wait_send()
                cp.wait_recv()

    arrays = list(srcs) + list(lands)
    outs = pl.pallas_call(
        body, name=name,
        out_shape=tuple(pltpu.HBM(a.shape, a.dtype) for a in arrays),
        in_specs=[HBM_SPEC] * (2 * n) + [SEM_SPEC, SEM_SPEC, pl.BlockSpec(memory_space=pl.ANY)],
        out_specs=tuple([HBM_SPEC] * (2 * n)),
        input_output_aliases={i: i for i in range(2 * n)},
        compiler_params=pltpu.CompilerParams(has_side_effects=SIDE_EFFECT),
    )(*arrays, send_sems, recv_sems, after)
    return list(outs[n:])


def _place_own(parts, scatter, name):
    n = len(parts)
    me = jnp.reshape(_slot(*_mesh_pos()), (1,)).astype(jnp.int32)

    def body(me_ref, *refs):
        for i in range(n):
            refs[n + i][0] = refs[i][0] if scatter else refs[i][...]

    def slot_spec(shape):
        rest = len(shape)
        return pl.BlockSpec((1,) + tuple(shape), lambda i, me_ref: (me_ref[0],) + (0,) * rest)

    def whole_spec(shape):
        nd = len(shape)
        return pl.BlockSpec(tuple(shape), lambda i, me_ref: (0,) * nd)

    blocks = [a.shape[1:] if scatter else a.shape for a in parts]
    return pl.pallas_call(
        body, name=name,
        grid_spec=pltpu.PrefetchScalarGridSpec(
            num_scalar_prefetch=1, grid=(1,),
            in_specs=[slot_spec(b) if scatter else whole_spec(b) for b in blocks],
            out_specs=[slot_spec(b) for b in blocks]),
        out_shape=[jax.ShapeDtypeStruct((N_DEV,) + tuple(b), a.dtype) for a, b in zip(parts, blocks)],
        compiler_params=_params(1, 32),
    )(me, *parts)


def _all_gather(items, name):
    n = len(items)

    def body(*refs):
        in_refs, out_refs = refs[:n], refs[n:2 * n]
        send_sems, recv_sems, local_sems = refs[2 * n:]
        x, y, c = _mesh_pos()
        me, sibling = (x, y, c), (x, y, 1 - c)
        chips = [(1 - x, y), (x, 1 - y), (1 - x, 1 - y)]

        def copy(i, k, block, to, src=None):
            dst = out_refs[i].at[_slot(*block)]
            return pltpu.make_async_remote_copy(
                src_ref=dst if src is None else src, dst_ref=dst,
                send_sem=send_sems.at[i * 7 + k], recv_sem=recv_sems.at[i * 7 + k],
                device_id=to, device_id_type=MESH_ID)

        mine = [pltpu.make_async_copy(in_refs[i], out_refs[i].at[_slot(*me)], local_sems.at[i]) for i in range(n)]
        for cp in mine:
            cp.start()
        first = []
        for i in range(n):
            first.append(copy(i, 0, me, sibling, src=in_refs[i]))
            for j, chip in enumerate(chips):
                first.append(copy(i, 1 + j, me, (*chip, c), src=in_refs[i]))
        for cp in first:
            cp.start()
        passed = []
        for j, chip in enumerate(chips):
            for i in range(n):
                copy(i, 1 + j, (*chip, c), me).wait_recv()
                fwd = copy(i, 4 + j, (*chip, c), sibling)
                fwd.start()
                passed.append(fwd)
        for i in range(n):
            copy(i, 0, sibling, me).wait_recv()
            for j, chip in enumerate(chips):
                copy(i, 4 + j, (*chip, 1 - c), me).wait_recv()
        for cp in first + passed:
            cp.wait_send()
        for cp in mine:
            cp.wait()

    any_spec = pl.BlockSpec(memory_space=pl.ANY)
    return pl.pallas_call(
        body, name=name,
        out_shape=[jax.ShapeDtypeStruct((N_DEV,) + a.shape, a.dtype) for a in items],
        in_specs=[any_spec] * n, out_specs=[any_spec] * n,
        scratch_shapes=[pltpu.SemaphoreType.DMA((7 * n,)), pltpu.SemaphoreType.DMA((7 * n,)),
                        pltpu.SemaphoreType.DMA((n,))],
    )(*items)


def _layer_fwd(x, p_all, layer, g, w_blk, w_out, plg, gate_w, proj_w, seq, conv=None, sgu=None):
    t, d = x.shape
    nb, _, bn = w_blk.shape
    e = w_out.shape[0]
    pd = p_all.shape[-1]
    tm = TM_MIX
    nt = seq // tm
    is_conv = conv is not None
    mixer_args = conv if is_conv else sgu
    n_mix = len(mixer_args)

    def body(*refs):
        x_ref, p_ref, g_ref, w_ref, wo_ref, plg_ref, gw_ref, pw_ref = refs[:8]
        mix = refs[8:8 + n_mix]
        outs = refs[8 + n_mix:]
        if is_conv:
            cw_ref, cb_ref, lg_ref, lb_ref = mix
            h_ref, proj_ref, y1_ref, y_ref, x1_ref, gate_ref, x2_ref, y0s, zs, w8 = outs

            @pl.when(lax.rem(pl.program_id(0), nt) == 0)
            def _():
                y0s[pl.ds(0, HALO), :] = jnp.zeros((HALO, e), F32)
            _conv_weights_to_sublanes(cw_ref, w8)
        else:
            lg_ref, lb_ref, sw_ref, sbt_ref = mix
            h_ref, proj_ref, y_ref, x1_ref, gate_ref, x2_ref, mixed_s = outs

        for sb in range(tm // FUSE_SB):
            rows = pl.ds(sb * FUSE_SB, FUSE_SB)
            xv = x_ref[rows, :]
            hv = (xv * _rms_rstd(xv) * g_ref[...]).astype(MXU_DTYPE)
            h_ref[rows, :] = hv
            for j in range(nb):
                proj_ref[rows, j * bn:(j + 1) * bn] = jnp.dot(hv, w_ref[j], preferred_element_type=F32)
            if is_conv:
                y0s[pl.ds(HALO + sb * FUSE_SB, FUSE_SB), :] = proj_ref[rows, 0:e] * _sigmoid(proj_ref[rows, e:2 * e])

        if is_conv:
            def emit(r0, c0, acc):
                y1_ref[pl.ds(r0, CONV_RC), c0:c0 + CONV_CC] = acc + cb_ref[:, c0:c0 + CONV_CC]
            _conv_apply(y0s, w8, zs, HALO - (CONV_K - 1), tm, e, False, emit)

        for sb in range(tm // FUSE_SB):
            rows = pl.ds(sb * FUSE_SB, FUSE_SB)
            if is_conv:
                xhat, _ = _ln_stats(y1_ref[rows, :])
                y2 = xhat * lg_ref[...] + lb_ref[...]
                y = y2 * _sigmoid(y2)
            else:
                _, _, u, _, _, _ = _sgu_parts(proj_ref[rows, 0:e], proj_ref[rows, e:2 * e], lg_ref[...], lb_ref[...],
                                              sw_ref, sbt_ref, mixed_s, FUSE_SB, e)
                y = u * mixed_s[...]
            y_ref[rows, :] = y
            z = proj_ref[rows, 2 * e:3 * e]
            q = (y * (z * _sigmoid(z))).astype(MXU_DTYPE)
            x1 = x_ref[rows, :] + jnp.dot(q, wo_ref[...], preferred_element_type=F32)
            x1_ref[rows, :] = x1
            rn = x1 * _rms_rstd(x1) * plg_ref[...]
            gate = _sigmoid(_dot(rn, gw_ref[...]))
            gate_ref[rows, :] = gate
            x2_ref[rows, :] = x1 + gate * _dot(p_ref[0, rows, :], pw_ref[...])

        if is_conv:
            y0s[pl.ds(0, HALO), :] = y0s[pl.ds(tm, HALO), :]

    row = lambda w: pl.BlockSpec((tm, w), lambda i: (i, 0))
    f32 = lambda w: jax.ShapeDtypeStruct((t, w), F32)
    out_shape = ([jax.ShapeDtypeStruct((t, d), MXU_DTYPE), f32(3 * e)] + ([f32(e)] if is_conv else [])
                 + [f32(e), f32(d), f32(d), f32(d)])
    out_specs = [row(d), row(3 * e)] + ([row(e)] if is_conv else []) + [row(e), row(d), row(d), row(d)]
    scratch = ([pltpu.VMEM((tm + HALO, e), F32), pltpu.VMEM((7, CONV_RC + 8, CONV_CC), F32),
                pltpu.VMEM((CONV_K, 8, e), F32)] if is_conv else [pltpu.VMEM((FUSE_SB, e), F32)])
    return pl.pallas_call(
        body, name="layer_fwd_conv" if is_conv else "layer_fwd_sgu", grid=(t // tm,),
        in_specs=[row(d), pl.BlockSpec((1, tm, pd), lambda i: (layer, i, 0)), _whole((1, d)), _whole(w_blk.shape),
                  _whole((e, d)), _whole((1, d)),
                  _whole((d, d)), _whole((pd, d))] + [_whole(a.shape) for a in mixer_args],
        out_specs=out_specs, out_shape=out_shape, scratch_shapes=scratch,
        compiler_params=_params(1, 60),
    )(x, p_all, g, w_blk, w_out, plg, gate_w, proj_w, *mixer_args)


def _sgu_parts(a, b, lg, lb, sw_ref, sbt_ref, mixed_s, tm, e):
    eg = e // GROUPS
    ea = lax.erf(a * INV_SQRT2)
    eb = lax.erf(b * INV_SQRT2)
    u = 0.5 * a * (1.0 + ea)
    v0 = 0.5 * b * (1.0 + eb)
    xhat, rstd = _ln_stats(v0)
    v = (xhat * lg + lb).astype(MXU_DTYPE)
    mask = _tril_mask()
    for g in range(GROUPS):
        wt = jnp.where(mask, sw_ref[g], 0.0).astype(MXU_DTYPE)
        bcol = sbt_ref[:, g:g + 1]
        for ch in range(tm // CHUNK):
            rows = slice(ch * CHUNK, (ch + 1) * CHUNK)
            cols = slice(g * eg, (g + 1) * eg)
            mixed_s[rows, cols] = jnp.dot(wt, v[rows, cols], preferred_element_type=F32) + bcol
    return ea, eb, u, xhat, rstd, v


def _loss_head(xf, fg, tgt):
    t, d = xf.shape
    tm = TM_OUT
    nsteps = t // tm

    def body(x_ref, g_ref, t_ref, loss_ref, dx_ref, dg_ref, sq_s):
        i = pl.program_id(0)

        @pl.when(i == 0)
        def _():
            sq_s[...] = jnp.zeros_like(sq_s)
            dg_ref[...] = jnp.zeros_like(dg_ref)
        x = x_ref[...]
        rstd = _rms_rstd(x)
        err = x * rstd * g_ref[...] - t_ref[...]
        sq_s[...] += jnp.sum(err * err, axis=0, keepdims=True)
        dx, dg = _rms_bwd(err * (1.0 / d), x, rstd, g_ref[...])
        dx_ref[...] = dx
        dg_ref[...] += dg

        @pl.when(i == nsteps - 1)
        def _():
            loss_ref[...] = jnp.sum(sq_s[...], axis=1, keepdims=True) * (0.5 / d)

    row = pl.BlockSpec((tm, d), lambda i: (i, 0))
    return pl.pallas_call(
        body, name="loss_head", grid=(nsteps,),
        in_specs=[row, _whole((1, d)), row],
        out_specs=[_acc_out((1, 1)), row, _acc_out((1, d))],
        out_shape=[jax.ShapeDtypeStruct((1, 1), F32), jax.ShapeDtypeStruct((t, d), F32),
                   jax.ShapeDtypeStruct((1, d), F32)],
        scratch_shapes=[pltpu.VMEM((1, d), F32)],
        compiler_params=_params(1, 32),
    )(xf, fg, tgt)


def _ple_bwd(dx2, x1, gate, p_all, layer, plg, gate_w, proj_w):
    t, d = x1.shape
    pd = p_all.shape[-1]
    tm = TM_OUT
    nsteps = t // tm
    bn = d // N_DEV

    def body(dx2_ref, x1_ref, gate_ref, p_ref, plg_ref, gw_ref, pw_ref, dx1_ref, dgw_ref, dpw_ref, dplg_ref,
             gw_acc, pw_acc):
        i = pl.program_id(0)

        @pl.when(i == 0)
        def _():
            gw_acc[...] = jnp.zeros_like(gw_acc)
            pw_acc[...] = jnp.zeros_like(pw_acc)
            dplg_ref[...] = jnp.zeros_like(dplg_ref)
        dx2 = dx2_ref[...]
        x1 = x1_ref[...]
        plg = plg_ref[...]
        rstd = _rms_rstd(x1)
        rn = (x1 * rstd * plg).astype(MXU_DTYPE)
        gate = gate_ref[...]
        p_b = p_ref[0].astype(MXU_DTYPE)
        pp = jnp.dot(p_b, pw_ref[...], preferred_element_type=F32)
        dpp = (dx2 * gate).astype(MXU_DTYPE)
        dgpre = (dx2 * pp * gate * (1.0 - gate)).astype(MXU_DTYPE)
        pw_acc[...] += _dot_tn(p_b, dpp)
        gw_acc[...] += _dot_tn(rn, dgpre)
        drn = _dot_nt(dgpre, gw_ref[...])
        dx, dg = _rms_bwd(drn, x1, rstd, plg)
        dx1_ref[...] = dx2 + dx
        dplg_ref[...] += dg

        @pl.when(i == nsteps - 1)
        def _():
            dgw_ref[...] = gw_acc[...].astype(dgw_ref.dtype)
            for j in range(N_DEV):
                dpw_ref[j] = pw_acc[:, j * bn:(j + 1) * bn].astype(dpw_ref.dtype)

    row = lambda w: pl.BlockSpec((tm, w), lambda i: (i, 0))
    return pl.pallas_call(
        body, name="ple_bwd", grid=(nsteps,),
        in_specs=[row(d), row(d), row(d), pl.BlockSpec((1, tm, pd), lambda i: (layer, i, 0)), _whole((1, d)),
                  _whole((d, d)), _whole((pd, d))],
        out_specs=[row(d), _acc_out((d, d)), _acc_out((N_DEV, pd, bn)), _acc_out((1, d))],
        out_shape=[jax.ShapeDtypeStruct((t, d), F32), jax.ShapeDtypeStruct((d, d), WIRE_DTYPE),
                   jax.ShapeDtypeStruct((N_DEV, pd, bn), WIRE_DTYPE), jax.ShapeDtypeStruct((1, d), F32)],
        scratch_shapes=[pltpu.VMEM((d, d), F32), pltpu.VMEM((pd, d), F32)],
        compiler_params=_params(1, 48),
    )(dx2, x1, gate, p_all, plg, gate_w, proj_w)


def _outproj_bwd(dx1, y, proj, w_out):
    t, d = dx1.shape
    e = y.shape[1]
    tm = TM_OUT
    nsteps = t // tm

    def body(dx1_ref, y_ref, z_ref, wo_ref, dy_ref, dz_ref, dwo_ref, wo_acc):
        i = pl.program_id(0)

        @pl.when(i == 0)
        def _():
            wo_acc[...] = jnp.zeros_like(wo_acc)
        dx1 = dx1_ref[...].astype(MXU_DTYPE)
        y = y_ref[...]
        z = z_ref[...]
        s = _sigmoid(z)
        sz = z * s
        q = (y * sz).astype(MXU_DTYPE)
        wo_acc[...] += _dot_tn(q, dx1)
        dq = _dot_nt(dx1, wo_ref[...])
        dy_ref[...] = dq * sz
        dz_ref[...] = (dq * y * _silu_grad(z, s)).astype(dz_ref.dtype)

        @pl.when(i == nsteps - 1)
        def _():
            dwo_ref[...] = wo_acc[...].astype(dwo_ref.dtype)

    return pl.pallas_call(
        body, name="outproj_bwd", grid=(nsteps,),
        in_specs=[pl.BlockSpec((tm, d), lambda i: (i, 0)), pl.BlockSpec((tm, e), lambda i: (i, 0)),
                  pl.BlockSpec((tm, e), lambda i: (i, 2)), _whole((e, d))],
        out_specs=[pl.BlockSpec((tm, e), lambda i: (i, 0)), pl.BlockSpec((tm, e), lambda i: (i, 0)),
                   _acc_out((e, d))],
        out_shape=[jax.ShapeDtypeStruct((t, e), F32), jax.ShapeDtypeStruct((t, e), MXU_DTYPE),
                   jax.ShapeDtypeStruct((e, d), WIRE_DTYPE)],
        scratch_shapes=[pltpu.VMEM((e, d), F32)],
        compiler_params=_params(1, 60),
    )(dx1, y, proj, w_out)


def _conv_ln_bwd(dy, y1, lg, lb):
    t, e = dy.shape
    tm = TM_MIX

    def body(dy_ref, y1_ref, lg_ref, lb_ref, dy1_ref, dlg_ref, dlb_ref, dcb_ref):
        @pl.when(pl.program_id(0) == 0)
        def _():
            dlg_ref[...] = jnp.zeros_like(dlg_ref)
            dlb_ref[...] = jnp.zeros_like(dlb_ref)
            dcb_ref[...] = jnp.zeros_like(dcb_ref)
        xhat, rstd = _ln_stats(y1_ref[...])
        lg = lg_ref[...]
        y2 = xhat * lg + lb_ref[...]
        dy2 = dy_ref[...] * _silu_grad(y2, _sigmoid(y2))
        dlg_ref[...] += jnp.sum(dy2 * xhat, axis=0, keepdims=True)
        dlb_ref[...] += jnp.sum(dy2, axis=0, keepdims=True)
        dy1 = _ln_bwd(dy2 * lg, xhat, rstd)
        dy1_ref[...] = dy1
        dcb_ref[...] += jnp.sum(dy1, axis=0, keepdims=True)

    row = pl.BlockSpec((tm, e), lambda i: (i, 0))
    return pl.pallas_call(
        body, name="conv_ln_bwd", grid=(t // tm,),
        in_specs=[row, row, _whole((1, e)), _whole((1, e))],
        out_specs=[row, _acc_out((1, e)), _acc_out((1, e)), _acc_out((1, e))],
        out_shape=[jax.ShapeDtypeStruct((t, e), F32)] + [jax.ShapeDtypeStruct((1, e), F32)] * 3,
        compiler_params=_params(1, 48),
    )(dy, y1, lg, lb)


def _conv_bwd(dy1, proj, dz, cw, seq):
    t, e = dy1.shape
    tm = TM_MIX
    nt = seq // tm
    hb = tm // HALO
    n_halo_blocks = t // HALO

    def body(d_ref, dn_ref, a_ref, b_ref, ah_ref, bh_ref, dz_ref, cw_ref, dproj_ref, dcw_ref,
             y0s, d1s, zs, dsh, dcw8, w8):
        i = pl.program_id(0)
        pos = lax.rem(i, nt)

        @pl.when(i == 0)
        def _():
            dcw8[...] = jnp.zeros_like(dcw8)
        _conv_weights_to_sublanes(cw_ref, w8)
        a = a_ref[...]
        sb = _sigmoid(b_ref[...])
        y0s[pl.ds(HALO, tm), :] = a * sb
        d1s[pl.ds(0, tm), :] = d_ref[...]

        @pl.when(pos == 0)
        def _():
            y0s[pl.ds(0, HALO), :] = jnp.zeros((HALO, e), F32)

        @pl.when(pos != 0)
        def _():
            y0s[pl.ds(0, HALO), :] = ah_ref[...] * _sigmoid(bh_ref[...])

        @pl.when(pos == nt - 1)
        def _():
            d1s[pl.ds(tm, HALO), :] = jnp.zeros((HALO, e), F32)

        @pl.when(pos != nt - 1)
        def _():
            d1s[pl.ds(tm, HALO), :] = dn_ref[...]

        base = HALO - (CONV_K - 1)
        for c0 in range(0, e, DCW_CC):
            cols = slice(c0, c0 + DCW_CC)
            dcur = d_ref[:, cols]
            for s in range(1, 8):
                dsh[s - 1, pl.ds(0, 8), :] = jnp.zeros((8, DCW_CC), F32)
                dsh[s - 1, pl.ds(tm, 8), :] = jnp.zeros((8, DCW_CC), F32)
                dsh[s - 1, pl.ds(s, tm), :] = dcur
            for s in range(8):
                taps = [k for k in range(CONV_K) if (base + k) % 8 == s]
                off0 = base + taps[0] - s
                n, ch = (tm, DCW_RC) if s == 0 else (tm + 8, (tm + 8) // DCW_CHUNKS)
                sums = [None] * len(taps)
                for r in range(0, n, ch):
                    dch = d_ref[r:r + ch, cols] if s == 0 else dsh[s - 1, r:r + ch, :]
                    window = y0s[pl.ds(off0 + r, ch + 8 * (len(taps) - 1)), cols]
                    for m in range(len(taps)):
                        part = jnp.sum((dch * window[8 * m:8 * m + ch]).reshape(ch // 8, 8, DCW_CC), axis=0)
                        sums[m] = part if sums[m] is None else sums[m] + part
                for m, k in enumerate(taps):
                    dcw8[k, :, cols] += sums[m]

        def emit(r0, c0, dy0):
            rs, cs = pl.ds(r0, CONV_RC), slice(c0, c0 + CONV_CC)
            sbv = _sigmoid(b_ref[rs, cs])
            av = a_ref[rs, cs]
            dproj_ref[rs, c0:c0 + CONV_CC] = (dy0 * sbv).astype(dproj_ref.dtype)
            dproj_ref[rs, e + c0:e + c0 + CONV_CC] = (dy0 * av * sbv * (1.0 - sbv)).astype(dproj_ref.dtype)
        _conv_apply(d1s, w8, zs, 0, tm, e, True, emit)
        dproj_ref[:, 2 * e:3 * e] = dz_ref[...]

        @pl.when(i == t // tm - 1)
        def _():
            dcw_ref[...] = jnp.sum(dcw8[...], axis=1)

    tile = lambda col: pl.BlockSpec((tm, e), lambda i: (i, col))
    prev = lambda col: pl.BlockSpec((HALO, e), lambda i: (jnp.maximum(i * hb - 1, 0), col))
    nxt = pl.BlockSpec((HALO, e), lambda i: (jnp.minimum((i + 1) * hb, n_halo_blocks - 1), 0))
    return pl.pallas_call(
        body, name="conv_bwd", grid=(t // tm,),
        in_specs=[tile(0), nxt, tile(0), tile(1), prev(0), prev(1), tile(0), _whole(cw.shape)],
        out_specs=[pl.BlockSpec((tm, 3 * e), lambda i: (i, 0)), _acc_out(cw.shape)],
        out_shape=[jax.ShapeDtypeStruct((t, 3 * e), MXU_DTYPE), jax.ShapeDtypeStruct(cw.shape, F32)],
        scratch_shapes=[pltpu.VMEM((tm + HALO, e), F32), pltpu.VMEM((tm + HALO, e), F32),
                        pltpu.VMEM((7, CONV_RC + 8, CONV_CC), F32), pltpu.VMEM((7, tm + 8, DCW_CC), F32),
                        pltpu.VMEM((CONV_K, 8, e), F32), pltpu.VMEM((CONV_K, 8, e), F32)],
        compiler_params=_params(1, 56),
    )(dy1, dy1, proj, proj, proj, proj, dz, cw)


def _sgu_bwd(dy, proj, dz, lg, lb, sw, swt, sbt):
    t, e = dy.shape
    eg = e // GROUPS
    tm = TM_MIX
    nsteps = t // tm

    def body(dy_ref, a_ref, b_ref, dz_ref, lg_ref, lb_ref, sw_ref, swt_ref, sbt_ref,
             dproj_ref, dsw_ref, dsb_ref, dlg_ref, dlb_ref, mixed_s, dv_s, sb_acc):
        i = pl.program_id(0)

        @pl.when(i == 0)
        def _():
            dsw_ref[...] = jnp.zeros_like(dsw_ref)
            sb_acc[...] = jnp.zeros_like(sb_acc)
            dlg_ref[...] = jnp.zeros_like(dlg_ref)
            dlb_ref[...] = jnp.zeros_like(dlb_ref)
        a = a_ref[...]
        b = b_ref[...]
        lg = lg_ref[...]
        ea, eb, u, xhat, rstd, v = _sgu_parts(a, b, lg, lb_ref[...], sw_ref, sbt_ref, mixed_s, tm, e)
        dy = dy_ref[...]
        du = dy * mixed_s[...]
        dmixed = (dy * u).astype(MXU_DTYPE)
        mask = _tril_mask()
        mask_t = (lax.broadcasted_iota(jnp.int32, (CHUNK, CHUNK), 0)
                  <= lax.broadcasted_iota(jnp.int32, (CHUNK, CHUNK), 1))
        ones = jnp.ones((8, eg), MXU_DTYPE)
        for g in range(GROUPS):
            wtt = jnp.where(mask_t, swt_ref[g], 0.0).astype(MXU_DTYPE)
            cols = slice(g * eg, (g + 1) * eg)
            for ch in range(tm // CHUNK):
                rows = slice(ch * CHUNK, (ch + 1) * CHUNK)
                dm = dmixed[rows, cols]
                dv_s[rows, cols] = jnp.dot(wtt, dm, preferred_element_type=F32)
                dsw_ref[g] += _dot_nt(dm, v[rows, cols])
                sb_acc[g] += _dot_nt(ones, dm)
        dv = dv_s[...]
        dlg_ref[...] += jnp.sum(dv * xhat, axis=0, keepdims=True)
        dlb_ref[...] += jnp.sum(dv, axis=0, keepdims=True)
        dv0 = _ln_bwd(dv * lg, xhat, rstd)
        pdf_a = jnp.exp(-0.5 * a * a) * INV_SQRT_2PI
        pdf_b = jnp.exp(-0.5 * b * b) * INV_SQRT_2PI
        dproj_ref[:, 0:e] = (du * (0.5 * (1.0 + ea) + a * pdf_a)).astype(dproj_ref.dtype)
        dproj_ref[:, e:2 * e] = (dv0 * (0.5 * (1.0 + eb) + b * pdf_b)).astype(dproj_ref.dtype)
        dproj_ref[:, 2 * e:3 * e] = dz_ref[...]

        @pl.when(i == nsteps - 1)
        def _():
            for g in range(GROUPS):
                dsw_ref[g] = jnp.where(mask, dsw_ref[g], 0.0)
                dsb_ref[g:g + 1, :] = sb_acc[g, 0:1, :]

    tile = lambda col: pl.BlockSpec((tm, e), lambda i: (i, col))
    return pl.pallas_call(
        body, name="sgu_bwd", grid=(nsteps,),
        in_specs=[tile(0), tile(0), tile(1), tile(0), _whole((1, e)), _whole((1, e)), _whole(sw.shape),
                  _whole(swt.shape), _whole(sbt.shape)],
        out_specs=[pl.BlockSpec((tm, 3 * e), lambda i: (i, 0)), _acc_out(sw.shape), _acc_out((GROUPS, CHUNK)),
                   _acc_out((1, e)), _acc_out((1, e))],
        out_shape=[jax.ShapeDtypeStruct((t, 3 * e), MXU_DTYPE), jax.ShapeDtypeStruct(sw.shape, F32),
                   jax.ShapeDtypeStruct((GROUPS, CHUNK), F32), jax.ShapeDtypeStruct((1, e), F32),
                   jax.ShapeDtypeStruct((1, e), F32)],
        scratch_shapes=[pltpu.VMEM((tm, e), F32), pltpu.VMEM((tm, e), F32), pltpu.VMEM((GROUPS, 8, CHUNK), F32)],
        compiler_params=_params(1, 56),
    )(dy, proj, proj, dz, lg, lb, sw, swt, sbt)


def _inproj_bwd_x(dproj, w_blk, dx1, x, g):
    t, d = x.shape
    nb, _, bn = w_blk.shape
    tm = TM_IN

    def body(dp_ref, w_ref, dx1_ref, x_ref, g_ref, dx_ref, dg_ref):
        @pl.when(pl.program_id(0) == 0)
        def _():
            dg_ref[...] = jnp.zeros_like(dg_ref)
        dh = None
        for j in range(nb):
            term = _dot_nt(dp_ref[:, j * bn:(j + 1) * bn], w_ref[j])
            dh = term if dh is None else dh + term
        xv = x_ref[...]
        dx, dg = _rms_bwd(dh, xv, _rms_rstd(xv), g_ref[...])
        dx_ref[...] = dx1_ref[...] + dx
        dg_ref[...] += dg

    row = lambda w: pl.BlockSpec((tm, w), lambda i: (i, 0))
    return pl.pallas_call(
        body, name="inproj_bwd_x", grid=(t // tm,),
        in_specs=[row(nb * bn), _whole(w_blk.shape), row(d), row(d), _whole((1, d))],
        out_specs=[row(d), _acc_out((1, d))],
        out_shape=[jax.ShapeDtypeStruct((t, d), F32), jax.ShapeDtypeStruct((1, d), F32)],
        compiler_params=_params(1, 56),
    )(dproj, w_blk, dx1, x, g)


def _inproj_bwd_w(h, dproj):
    t, d = h.shape
    bn = dproj.shape[1] // N_DEV
    tm = TM_IN
    nsteps = t // tm

    nh = 2
    per = N_DEV // nh

    def body(h_ref, dp_ref, dw_ref, acc):
        i = pl.program_id(1)

        @pl.when(i == 0)
        def _():
            acc[...] = jnp.zeros_like(acc)
        hv = h_ref[...]
        for jj in range(per):
            acc[jj] += _dot_tn(hv, dp_ref[:, jj * bn:(jj + 1) * bn])

        @pl.when(i == nsteps - 1)
        def _():
            dw_ref[...] = acc[...].astype(dw_ref.dtype)

    return pl.pallas_call(
        body, name="inproj_bwd_w", grid=(nh, nsteps),
        in_specs=[pl.BlockSpec((tm, d), lambda hh, i: (i, 0)), pl.BlockSpec((tm, per * bn), lambda hh, i: (i, hh))],
        out_specs=[pl.BlockSpec((per, d, bn), lambda hh, i: (hh, 0, 0))],
        out_shape=[jax.ShapeDtypeStruct((N_DEV, d, bn), WIRE_DTYPE)],
        scratch_shapes=[pltpu.VMEM((per, d, bn), F32)],
        compiler_params=_params(2, 56),
    )(h, dproj)[0]


def _adamw(parts, w, m, v):
    nl, r, c = w.shape
    tr = r
    for cand in (512, 256, 128, 64, 32, 16, 8):
        if r % cand == 0 and cand * c * 4 <= (1 << 19):
            tr = cand
            break
    bc1 = 1.0 - ADAM_B1 ** ADAM_STEP
    bc2 = 1.0 - ADAM_B2 ** ADAM_STEP

    def body(*refs):
        p_refs = refs[:nl]
        w_ref, m_ref, v_ref, g_ref, d_ref, nm_ref, nv_ref = refs[nl:]

        def update(p_ref):
            g = p_ref[0].astype(F32)
            for s in range(1, N_DEV):
                g = g + p_ref[s].astype(F32)
            nm = ADAM_B1 * m_ref[0] + (1.0 - ADAM_B1) * g
            nv = ADAM_B2 * v_ref[0] + (1.0 - ADAM_B2) * (g * g)
            g_ref[0] = g
            nm_ref[0] = nm
            nv_ref[0] = nv
            d_ref[0] = -ADAM_LR * ((nm / bc1) / (jnp.sqrt(nv / bc2) + ADAM_EPS) + ADAM_WD * w_ref[0])

        if nl == 1:
            update(p_refs[0])
        else:
            for kk in range(nl):
                pl.when(pl.program_id(0) == kk)(lambda kk=kk: update(p_refs[kk]))

    part_spec = lambda kk: pl.BlockSpec((N_DEV, tr, c), lambda l, i: (0, jnp.where(l == kk, i, 0), 0))
    row = pl.BlockSpec((1, tr, c), lambda l, i: (l, i, 0))
    return pl.pallas_call(
        body, name="adamw", grid=(nl, r // tr),
        in_specs=[part_spec(kk) for kk in range(nl)] + [row, row, row],
        out_specs=[row] * 4,
        out_shape=[jax.ShapeDtypeStruct((nl, r, c), F32)] * 4,
        compiler_params=_params(2, 48),
    )(*parts, w, m, v)


def _pack(arrays):
    flat = jnp.concatenate([a.reshape(-1).astype(F32) for a in arrays])
    unit = 8 * PACK_COLS
    padded = -(-flat.shape[0] // unit) * unit
    return jnp.pad(flat, (0, padded - flat.shape[0])).reshape(-1, PACK_COLS)


def _unpack(packed, shapes):
    flat = packed.reshape(-1)
    out, off = [], 0
    for s in shapes:
        n = math.prod(s)
        out.append(flat[off:off + n].reshape(s))
        off += n
    return out


def _shard_last(arrays):
    cols = []
    for a in arrays:
        b = a.astype(F32).reshape(a.shape[:-1] + (N_DEV, a.shape[-1] // N_DEV))
        cols.append(jnp.moveaxis(b, -2, 0).reshape(N_DEV, -1))
    flat = jnp.concatenate(cols, axis=1)
    unit = 8 * PACK_COLS
    padded = -(-flat.shape[1] // unit) * unit
    return jnp.pad(flat, ((0, 0), (0, padded - flat.shape[1]))).reshape(N_DEV, -1, PACK_COLS)


def _unshard_last(packed, shard_shapes):
    flat = packed.reshape(N_DEV, -1)
    out, off = [], 0
    for s in shard_shapes:
        n = math.prod(s)
        a = jnp.moveaxis(flat[:, off:off + n].reshape((N_DEV,) + tuple(s)), 0, -2)
        out.append(a.reshape(tuple(s[:-1]) + (N_DEV * s[-1],)))
        off += n
    return out


def kernel(x, p, norm_g, w_in, w_out, conv_w, conv_b, conv_ln_g, conv_ln_b, sgu_ln_g, sgu_ln_b, sgu_w, sgu_b, pl_norm_g, pl_gate_w, pl_proj_w, final_g, loss_target, m_norm_g, m_w_in, m_w_out, m_conv_w, m_conv_b, m_conv_ln_g, m_conv_ln_b, m_sgu_ln_g, m_sgu_ln_b, m_sgu_w, m_sgu_b, m_pl_norm_g, m_pl_gate_w, m_pl_proj_w, m_final_g, v_norm_g, v_w_in, v_w_out, v_conv_w, v_conv_b, v_conv_ln_g, v_conv_ln_b, v_sgu_ln_g, v_sgu_ln_b, v_sgu_w, v_sgu_b, v_pl_norm_g, v_pl_gate_w, v_pl_proj_w, v_final_g):
    bsz, seq, d = x.shape
    t = bsz * seq
    depth = w_in.shape[0]
    e = w_out.shape[1] * N_DEV
    pd = p.shape[-1]
    n_conv, n_sgu = conv_w.shape[0], sgu_ln_g.shape[0]

    small_shapes = [conv_w.shape, sgu_ln_g.shape, sgu_ln_b.shape]
    cast = lambda a: a.astype(MXU_DTYPE)
    first = [cast(w_in[0]), cast(w_out[0]), cast(pl_gate_w[0]), cast(pl_proj_w), _pack([conv_w, sgu_ln_g, sgu_ln_b])]
    later = [[cast(w_in[l]), cast(w_out[l]), cast(pl_gate_w[l])] for l in range(1, depth)]
    gathered = _all_gather(first, "gather_weights")
    gather_pending, gather_tokens = {}, 0.0
    for l in range(1, depth):
        lands = _place_own(later[l - 1], False, "place_own_weights")
        send, recv, srcs, lnds, token = _exchange_start(later[l - 1], lands, gathered[0], False, f"gather_start_{l}")
        gather_pending[l] = (send, recv, srcs, lnds)
        gather_tokens = gather_tokens + token[0, 0]
    w_in_g = {0: gathered[0]}
    w_out_g = {0: gathered[1].reshape(e, d)}
    gate_g = {0: gathered[2].reshape(d, d)}
    proj_g = jnp.transpose(gathered[3], (1, 2, 0, 3)).reshape(depth, pd, d)
    conv_w_g, sgu_ln_g_g, sgu_ln_b_g = _unshard_last(gathered[4], small_shapes)
    sgu_wt = jnp.swapaxes(sgu_w, -1, -2)
    sgu_bt = jnp.swapaxes(sgu_b, -1, -2)

    xs = [x.reshape(t, d)]
    p_all = p.reshape(depth, t, pd)
    saved = []
    for l in range(depth):
        j = l // 2
        if l == 0:
            g_l = norm_g[0:1] + gather_tokens
        else:
            g_l = norm_g[l:l + 1]
            got = _exchange_wait(*gather_pending.pop(l), xs[-1], False, f"gather_wait_{l}")
            w_in_g[l], w_out_g[l], gate_g[l] = got[0], got[1].reshape(e, d), got[2].reshape(d, d)
        common = (xs[-1], p_all, l, g_l, w_in_g[l], w_out_g[l], pl_norm_g[l:l + 1], gate_g[l], proj_g[l], seq)
        if l % 2 == 0:
            h, proj, y1, y, x1, gate, x2 = _layer_fwd(
                *common, conv=(conv_w_g[j], conv_b[j:j + 1], conv_ln_g[j:j + 1], conv_ln_b[j:j + 1]))
        else:
            y1 = None
            h, proj, y, x1, gate, x2 = _layer_fwd(
                *common, sgu=(sgu_ln_g_g[j:j + 1], sgu_ln_b_g[j:j + 1], sgu_w[j], sgu_bt[j]))
        saved.append((h, proj, y1, y, x1, gate))
        xs.append(x2)

    loss_part, dx, d_final_g = _loss_head(xs[-1], final_g.reshape(1, d), loss_target.reshape(t, d))
    loss = lax.psum(loss_part[0, 0], ("x", "y", "c"))

    d_norm_g, d_pl_norm_g = [None] * depth, [None] * depth
    scatter_pending = {}
    d_conv_w, d_conv_b, d_conv_ln_g, d_conv_ln_b = [None] * n_conv, [None] * n_conv, [None] * n_conv, [None] * n_conv
    d_sgu_ln_g, d_sgu_ln_b, d_sgu_w, d_sgu_b = [None] * n_sgu, [None] * n_sgu, [None] * n_sgu, [None] * n_sgu
    def scatter(parts, name):
        send, recv, srcs, lnds, token = _exchange_start(parts, _place_own(parts, True, "place_own_grads"), parts[0],
                                                        True, name)
        return (send, recv, srcs, lnds), token[0, 0]

    for l in reversed(range(depth)):
        j = l // 2
        h, proj, y1, y, x1, gate = saved[l]
        dx1, dgate_p, dprojw_p, d_pl_norm_g[l] = _ple_bwd(dx, x1, gate, p_all, l, pl_norm_g[l:l + 1], gate_g[l],
                                                         proj_g[l])
        dy, dz, dw_out_p = _outproj_bwd(dx1, y, proj, w_out_g[l])
        early = [dw_out_p.reshape(N_DEV, e // N_DEV, d), dgate_p.reshape(N_DEV, d // N_DEV, d), dprojw_p]
        early_token = 0.0
        if l == 0:
            scatter_pending["0_early"], early_token = scatter(early, "scatter_start_0_early")
            early = []
        if l % 2 == 0:
            dy1, d_conv_ln_g[j], d_conv_ln_b[j], d_conv_b[j] = _conv_ln_bwd(
                dy, y1, conv_ln_g[j:j + 1] + early_token, conv_ln_b[j:j + 1])
            dproj, d_conv_w[j] = _conv_bwd(dy1, proj, dz, conv_w_g[j], seq)
        else:
            dproj, d_sgu_w[j], d_sgu_b[j], d_sgu_ln_g[j], d_sgu_ln_b[j] = _sgu_bwd(
                dy, proj, dz, sgu_ln_g_g[j:j + 1] + early_token, sgu_ln_b_g[j:j + 1], sgu_w[j], sgu_wt[j], sgu_bt[j])
        scatter_pending[l], token = scatter([_inproj_bwd_w(h, dproj)] + early, f"scatter_start_{l}")
        dx, d_norm_g[l] = _inproj_bwd_x(dproj, w_in_g[l], dx1, xs[l], norm_g[l:l + 1] + token)
    grad_x = dx.reshape(bsz, seq, d)

    small_part = _shard_last([jnp.stack(d_conv_w), jnp.concatenate(d_sgu_ln_g, axis=0),
                              jnp.concatenate(d_sgu_ln_b, axis=0)])
    rep_part = _pack([jnp.concatenate(d_norm_g, axis=0), jnp.concatenate(d_conv_b, axis=0),
                      jnp.concatenate(d_conv_ln_g, axis=0), jnp.concatenate(d_conv_ln_b, axis=0),
                      jnp.stack(d_sgu_w), jnp.stack(d_sgu_b), jnp.concatenate(d_pl_norm_g, axis=0), d_final_g])
    small_send, small_recv, small_srcs, small_lnds, small_token = _exchange_start(
        [small_part], _place_own([small_part], True, "place_own_small"), grad_x, True, "scatter_small_start")
    rep_send, rep_recv, rep_srcs, rep_lnds, rep_token = _exchange_start(
        [rep_part], _place_own([rep_part], False, "place_own_replicated"), grad_x, False, "gather_replicated_start")

    landed = {}
    for key in list(scatter_pending):
        landed[key] = _exchange_wait(*scatter_pending.pop(key), small_token + rep_token, True, f"scatter_wait_{key}")
    dw_in_l = [landed[l][0] for l in range(depth)]
    rest = [landed["0_early"]] + [landed[l][1:] for l in range(1, depth)]

    o_w_in = _adamw(dw_in_l, w_in, m_w_in, v_w_in)
    o_w_out = _adamw([r[0] for r in rest], w_out, m_w_out, v_w_out)
    o_gate = _adamw([r[1] for r in rest], pl_gate_w, m_pl_gate_w, v_pl_gate_w)
    o_projw = _adamw([r[2] for r in rest], pl_proj_w, m_pl_proj_w, v_pl_proj_w)
    r_small = _exchange_wait(small_send, small_recv, small_srcs, small_lnds, o_projw[1], True, "scatter_small_wait")[0]
    r_rep = _exchange_wait(rep_send, rep_recv, rep_srcs, rep_lnds, o_w_in[1], False, "gather_replicated_wait")[0]

    def packed(parts, ws, ms, vs):
        shapes = [a.shape for a in ws]
        outs = _adamw([parts.reshape(N_DEV, -1, PACK_COLS)], _pack(ws)[None], _pack(ms)[None], _pack(vs)[None])
        return [_unpack(o[0], shapes) for o in outs]

    o_small = packed(r_small, [conv_w, sgu_ln_g, sgu_ln_b], [m_conv_w, m_sgu_ln_g, m_sgu_ln_b],
                     [v_conv_w, v_sgu_ln_g, v_sgu_ln_b])
    o_rep = packed(r_rep, [norm_g, conv_b, conv_ln_g, conv_ln_b, sgu_w, sgu_b, pl_norm_g, final_g],
                   [m_norm_g, m_conv_b, m_conv_ln_g, m_conv_ln_b, m_sgu_w, m_sgu_b, m_pl_norm_g, m_final_g],
                   [v_norm_g, v_conv_b, v_conv_ln_g, v_conv_ln_b, v_sgu_w, v_sgu_b, v_pl_norm_g, v_final_g])

    def leaf(kind):
        rep, small = o_rep[kind], o_small[kind]
        return [rep[0], o_w_in[kind], o_w_out[kind], small[0], rep[1], rep[2], rep[3], small[1], small[2], rep[4],
                rep[5], rep[6], o_gate[kind], o_projw[kind], rep[7]]

    return (loss, grad_x, *leaf(0), *leaf(1), *leaf(2), *leaf(3))
```

```python
import math

import jax
import jax.numpy as jnp
from jax import lax
from jax.experimental import pallas as pl
from jax.experimental.pallas import tpu as pltpu

F32 = jnp.float32
MXU_DTYPE = jnp.bfloat16
WIRE_DTYPE = jnp.bfloat16

EPS = 1e-6
CONV_K = 31
CHUNK = 128
GROUPS = 8
HALO = 32
N_DEV = 8
DEPTH = 4

ADAM_LR = 0.001
ADAM_B1 = 0.9
ADAM_B2 = 0.999
ADAM_EPS = 1e-08
ADAM_WD = 0.01
ADAM_STEP = 10

TM_IN = 512
TM_MIX = 256
TM_OUT = 512
FUSE_SB = 256
CONV_RC = 64
CONV_CC = 128
DCW_CC = 256
DCW_RC = 64
DCW_CHUNKS = 3
PACK_COLS = 1024

MESH_ID = pl.DeviceIdType.MESH
INV_SQRT2 = 1.0 / math.sqrt(2.0)
INV_SQRT_2PI = 1.0 / math.sqrt(2.0 * math.pi)


def _params(n_grid, vmem_mb):
    return pltpu.CompilerParams(dimension_semantics=("arbitrary",) * n_grid, vmem_limit_bytes=vmem_mb << 20)


def _whole(shape):
    nd = len(shape)
    return pl.BlockSpec(shape, lambda *_: (0,) * nd, pipeline_mode=pl.Buffered(1))


def _acc_out(shape):
    nd = len(shape)
    return pl.BlockSpec(shape, lambda *_: (0,) * nd)


def _dot(a, b):
    return jnp.dot(a.astype(MXU_DTYPE), b.astype(MXU_DTYPE), preferred_element_type=F32)


def _dot_nt(a, b):
    return lax.dot_general(a.astype(MXU_DTYPE), b.astype(MXU_DTYPE), (((1,), (1,)), ((), ())),
                           preferred_element_type=F32)


def _dot_tn(a, b):
    return lax.dot_general(a.astype(MXU_DTYPE), b.astype(MXU_DTYPE), (((0,), (0,)), ((), ())),
                           preferred_element_type=F32)


def _sigmoid(x):
    return jax.nn.sigmoid(x)


def _rms_rstd(x):
    return lax.rsqrt(jnp.mean(x * x, axis=-1, keepdims=True) + EPS)


def _rms_bwd(dy, x, rstd, g):
    gy = dy * g
    xr = x * rstd
    dx = rstd * (gy - xr * jnp.mean(gy * xr, axis=-1, keepdims=True))
    dg = jnp.sum(dy * xr, axis=0, keepdims=True)
    return dx, dg


def _ln_stats(x):
    mu = jnp.mean(x, axis=-1, keepdims=True)
    xc = x - mu
    var = jnp.mean(xc * xc, axis=-1, keepdims=True)
    rstd = lax.rsqrt(var + EPS)
    return xc * rstd, rstd


def _ln_bwd(dxhat, xhat, rstd):
    return rstd * (dxhat - jnp.mean(dxhat, axis=-1, keepdims=True)
                   - xhat * jnp.mean(dxhat * xhat, axis=-1, keepdims=True))


def _silu_grad(x, s):
    return s * (1.0 + x * (1.0 - s))


def _tril_mask():
    r = lax.broadcasted_iota(jnp.int32, (CHUNK, CHUNK), 0)
    c = lax.broadcasted_iota(jnp.int32, (CHUNK, CHUNK), 1)
    return r >= c


def _conv_weights_to_sublanes(w_ref, w8_ref):
    for k in range(CONV_K):
        w8_ref[k] = jnp.broadcast_to(w_ref[k:k + 1, :], w8_ref.shape[1:])


def _conv_apply(src_ref, w8_ref, zs_ref, base, tm, e, flip, emit):
    def row_block(i, carry):
        r0 = pl.multiple_of(i * CONV_RC, CONV_RC)
        for c0 in range(0, e, CONV_CC):
            cols = slice(c0, c0 + CONV_CC)
            acc = None
            for s in range(8):
                nrows = CONV_RC if s == 0 else CONV_RC + 8
                taps = [k for k in range(CONV_K) if (base + k) % 8 == s]
                off0 = base + taps[0] - s
                span = nrows + 8 * (len(taps) - 1)
                window = src_ref[pl.ds(r0 + off0, span), cols].reshape(span // 8, 8, CONV_CC)
                z = None
                for m, k in enumerate(taps):
                    wk = (CONV_K - 1 - k) if flip else k
                    term = w8_ref[wk, :, cols][None] * window[m:m + nrows // 8]
                    z = term if z is None else z + term
                z = z.reshape(nrows, CONV_CC)
                if s == 0:
                    acc = z
                else:
                    zs_ref[s - 1, pl.ds(0, nrows), :] = z
                    acc = acc + zs_ref[s - 1, pl.ds(s, CONV_RC), :]
            emit(r0, c0, acc)
        return carry

    lax.fori_loop(0, tm // CONV_RC, row_block, 0)


def _mesh_pos():
    return lax.axis_index("x"), lax.axis_index("y"), lax.axis_index("c")


def _slot(px, py, pc):
    return 4 * px + 2 * py + pc


def _peers(x, y, c):
    return [((1 - x) if (k & 4) else x, (1 - y) if (k & 2) else y, (1 - c) if (k & 1) else c)
            for k in range(1, N_DEV)]


HBM_SPEC = pl.BlockSpec(memory_space=pltpu.HBM)
SEM_SPEC = pl.BlockSpec(memory_space=pltpu.SEMAPHORE)
SIDE_EFFECT = pltpu.SideEffectType.DATAFLOW_SIDE_EFFECTING


def _exchange_copy(src_refs, land_refs, send_sems, recv_sems, i, k, peer, scatter, me):
    slot = _slot(*peer)
    return pltpu.make_async_remote_copy(
        src_ref=src_refs[i].at[slot] if scatter else src_refs[i],
        dst_ref=land_refs[i].at[me if me is not None else slot],
        send_sem=send_sems.at[i * 7 + k], recv_sem=recv_sems.at[i * 7 + k],
        device_id=peer, device_id_type=MESH_ID)


def _exchange_start(srcs, lands, after, scatter, name):
    n = len(srcs)

    def body(*refs):
        src_refs, land_refs = refs[:n], refs[n:2 * n]
        send_sems, recv_sems, token = refs[2 * n + 1], refs[2 * n + 2], refs[-1]
        x, y, c = _mesh_pos()
        me = _slot(x, y, c)
        for i in range(n):
            for k, peer in enumerate(_peers(x, y, c)):
                _exchange_copy(src_refs, land_refs, send_sems, recv_sems, i, k, peer, scatter, me).start()
        token[...] = jnp.zeros_like(token)

    arrays = list(srcs) + list(lands)
    outs = pl.pallas_call(
        body, name=name,
        out_shape=(pltpu.SemaphoreType.DMA((7 * n,)), pltpu.SemaphoreType.DMA((7 * n,)),
                   *[pltpu.HBM(a.shape, a.dtype) for a in lands], jax.ShapeDtypeStruct((8, 128), F32)),
        in_specs=[HBM_SPEC] * (2 * n) + [pl.BlockSpec(memory_space=pl.ANY)],
        out_specs=(SEM_SPEC, SEM_SPEC, *[HBM_SPEC] * n, pl.BlockSpec(memory_space=pltpu.VMEM)),
        input_output_aliases={n + i: 2 + i for i in range(n)},
        compiler_params=pltpu.CompilerParams(has_side_effects=SIDE_EFFECT),
    )(*[pltpu.with_memory_space_constraint(a, pltpu.HBM) for a in arrays], after)
    return outs[0], outs[1], list(srcs), list(outs[2:2 + n]), outs[-1]


def _exchange_wait(send_sems, recv_sems, srcs, lands, after, scatter, name):
    n = len(srcs)

    def body(*refs):
        src_refs, land_refs = refs[:n], refs[n:2 * n]
        send, recv = refs[2 * n], refs[2 * n + 1]
        x, y, c = _mesh_pos()
        for i in range(n):
            for k, peer in enumerate(_peers(x, y, c)):
                cp = _exchange_copy(src_refs, land_refs, send, recv, i, k, peer, scatter, None)
                cp.wait_send()
                cp.wait_recv()

    arrays = list(srcs) + list(lands)
    outs = pl.pallas_call(
        body, name=name,
        out_shape=tuple(pltpu.HBM(a.shape, a.dtype) for a in lands),
        in_specs=[HBM_SPEC] * (2 * n) + [SEM_SPEC, SEM_SPEC, pl.BlockSpec(memory_space=pl.ANY)],
        out_specs=tuple([HBM_SPEC] * n),
        input_output_aliases={n + i: i for i in range(n)},
        compiler_params=pltpu.CompilerParams(has_side_effects=SIDE_EFFECT),
    )(*arrays, send_sems, recv_sems, after)
    return list(outs)


def _place_own(parts, scatter, name):
    n = len(parts)
    me = jnp.reshape(_slot(*_mesh_pos()), (1,)).astype(jnp.int32)

    def body(me_ref, *refs):
        for i in range(n):
            refs[n + i][0] = refs[i][0] if scatter else refs[i][...]

    def slot_spec(shape):
        rest = len(shape)
        return pl.BlockSpec((1,) + tuple(shape), lambda i, me_ref: (me_ref[0],) + (0,) * rest)

    def whole_spec(shape):
        nd = len(shape)
        return pl.BlockSpec(tuple(shape), lambda i, me_ref: (0,) * nd)

    blocks = [a.shape[1:] if scatter else a.shape for a in parts]
    return pl.pallas_call(
        body, name=name,
        grid_spec=pltpu.PrefetchScalarGridSpec(
            num_scalar_prefetch=1, grid=(1,),
            in_specs=[slot_spec(b) if scatter else whole_spec(b) for b in blocks],
            out_specs=[slot_spec(b) for b in blocks]),
        out_shape=[jax.ShapeDtypeStruct((N_DEV,) + tuple(b), a.dtype) for a, b in zip(parts, blocks)],
        compiler_params=_params(1, 32),
    )(me, *parts)


def _all_gather(items, name):
    n = len(items)

    def body(*refs):
        in_refs, out_refs = refs[:n], refs[n:2 * n]
        send_sems, recv_sems, local_sems = refs[2 * n:]
        x, y, c = _mesh_pos()
        me, sibling = (x, y, c), (x, y, 1 - c)
        chips = [(1 - x, y), (x, 1 - y), (1 - x, 1 - y)]

        def copy(i, k, block, to, src=None):
            dst = out_refs[i].at[_slot(*block)]
            return pltpu.make_async_remote_copy(
                src_ref=dst if src is None else src, dst_ref=dst,
                send_sem=send_sems.at[i * 7 + k], recv_sem=recv_sems.at[i * 7 + k],
                device_id=to, device_id_type=MESH_ID)

        mine = [pltpu.make_async_copy(in_refs[i], out_refs[i].at[_slot(*me)], local_sems.at[i]) for i in range(n)]
        for cp in mine:
            cp.start()
        first = []
        for i in range(n):
            first.append(copy(i, 0, me, sibling, src=in_refs[i]))
            for j, chip in enumerate(chips):
                first.append(copy(i, 1 + j, me, (*chip, c), src=in_refs[i]))
        for cp in first:
            cp.start()
        passed = []
        for j, chip in enumerate(chips):
            for i in range(n):
                copy(i, 1 + j, (*chip, c), me).wait_recv()
                fwd = copy(i, 4 + j, (*chip, c), sibling)
                fwd.start()
                passed.append(fwd)
        for i in range(n):
            copy(i, 0, sibling, me).wait_recv()
            for j, chip in enumerate(chips):
                copy(i, 4 + j, (*chip, 1 - c), me).wait_recv()
        for cp in first + passed:
            cp.wait_send()
        for cp in mine:
            cp.wait()

    any_spec = pl.BlockSpec(memory_space=pl.ANY)
    return pl.pallas_call(
        body, name=name,
        out_shape=[jax.ShapeDtypeStruct((N_DEV,) + a.shape, a.dtype) for a in items],
        in_specs=[any_spec] * n, out_specs=[any_spec] * n,
        scratch_shapes=[pltpu.SemaphoreType.DMA((7 * n,)), pltpu.SemaphoreType.DMA((7 * n,)),
                        pltpu.SemaphoreType.DMA((n,))],
    )(*items)


def _layer_fwd(x, p_all, layer, g, w_blk, w_out, plg, gate_w, proj_w, seq, conv=None, sgu=None):
    t, d = x.shape
    nb, _, bn = w_blk.shape
    e = w_out.shape[0]
    pd = p_all.shape[-1]
    tm = TM_MIX
    nt = seq // tm
    is_conv = conv is not None
    mixer_args = conv if is_conv else sgu
    n_mix = len(mixer_args)

    def body(*refs):
        x_ref, p_ref, g_ref, w_ref, wo_ref, plg_ref, gw_ref, pw_ref = refs[:8]
        mix = refs[8:8 + n_mix]
        outs = refs[8 + n_mix:]
        if is_conv:
            cw_ref, cb_ref, lg_ref, lb_ref = mix
            h_ref, proj_ref, y1_ref, y_ref, x1_ref, gate_ref, x2_ref, y0s, zs, w8 = outs

            @pl.when(lax.rem(pl.program_id(0), nt) == 0)
            def _():
                y0s[pl.ds(0, HALO), :] = jnp.zeros((HALO, e), F32)
            _conv_weights_to_sublanes(cw_ref, w8)
        else:
            lg_ref, lb_ref, sw_ref, sbt_ref = mix
            h_ref, proj_ref, y_ref, x1_ref, gate_ref, x2_ref, mixed_s = outs

        for sb in range(tm // FUSE_SB):
            rows = pl.ds(sb * FUSE_SB, FUSE_SB)
            xv = x_ref[rows, :]
            hv = (xv * _rms_rstd(xv) * g_ref[...]).astype(MXU_DTYPE)
            h_ref[rows, :] = hv
            for j in range(nb):
                proj_ref[rows, j * bn:(j + 1) * bn] = jnp.dot(hv, w_ref[j], preferred_element_type=F32)
            if is_conv:
                y0s[pl.ds(HALO + sb * FUSE_SB, FUSE_SB), :] = proj_ref[rows, 0:e] * _sigmoid(proj_ref[rows, e:2 * e])

        if is_conv:
            def emit(r0, c0, acc):
                y1_ref[pl.ds(r0, CONV_RC), c0:c0 + CONV_CC] = acc + cb_ref[:, c0:c0 + CONV_CC]
            _conv_apply(y0s, w8, zs, HALO - (CONV_K - 1), tm, e, False, emit)

        for sb in range(tm // FUSE_SB):
            rows = pl.ds(sb * FUSE_SB, FUSE_SB)
            if is_conv:
                xhat, _ = _ln_stats(y1_ref[rows, :])
                y2 = xhat * lg_ref[...] + lb_ref[...]
                y = y2 * _sigmoid(y2)
            else:
                _, _, u, _, _, _ = _sgu_parts(proj_ref[rows, 0:e], proj_ref[rows, e:2 * e], lg_ref[...], lb_ref[...],
                                              sw_ref, sbt_ref, mixed_s, FUSE_SB, e)
                y = u * mixed_s[...]
            y_ref[rows, :] = y
            z = proj_ref[rows, 2 * e:3 * e]
            q = (y * (z * _sigmoid(z))).astype(MXU_DTYPE)
            x1 = x_ref[rows, :] + jnp.dot(q, wo_ref[...], preferred_element_type=F32)
            x1_ref[rows, :] = x1
            rn = x1 * _rms_rstd(x1) * plg_ref[...]
            gate = _sigmoid(_dot(rn, gw_ref[...]))
            gate_ref[rows, :] = gate
            x2_ref[rows, :] = x1 + gate * _dot(p_ref[0, rows, :], pw_ref[...])

        if is_conv:
            y0s[pl.ds(0, HALO), :] = y0s[pl.ds(tm, HALO), :]

    row = lambda w: pl.BlockSpec((tm, w), lambda i: (i, 0))
    f32 = lambda w: jax.ShapeDtypeStruct((t, w), F32)
    out_shape = ([jax.ShapeDtypeStruct((t, d), MXU_DTYPE), f32(3 * e)] + ([f32(e)] if is_conv else [])
                 + [f32(e), f32(d), f32(d), f32(d)])
    out_specs = [row(d), row(3 * e)] + ([row(e)] if is_conv else []) + [row(e), row(d), row(d), row(d)]
    scratch = ([pltpu.VMEM((tm + HALO, e), F32), pltpu.VMEM((7, CONV_RC + 8, CONV_CC), F32),
                pltpu.VMEM((CONV_K, 8, e), F32)] if is_conv else [pltpu.VMEM((FUSE_SB, e), F32)])
    return pl.pallas_call(
        body, name="layer_fwd_conv" if is_conv else "layer_fwd_sgu", grid=(t // tm,),
        in_specs=[row(d), pl.BlockSpec((1, tm, pd), lambda i: (layer, i, 0)), _whole((1, d)), _whole(w_blk.shape),
                  _whole((e, d)), _whole((1, d)),
                  _whole((d, d)), _whole((pd, d))] + [_whole(a.shape) for a in mixer_args],
        out_specs=out_specs, out_shape=out_shape, scratch_shapes=scratch,
        compiler_params=_params(1, 60),
    )(x, p_all, g, w_blk, w_out, plg, gate_w, proj_w, *mixer_args)


def _sgu_parts(a, b, lg, lb, sw_ref, sbt_ref, mixed_s, tm, e):
    eg = e // GROUPS
    ea = lax.erf(a * INV_SQRT2)
    eb = lax.erf(b * INV_SQRT2)
    u = 0.5 * a * (1.0 + ea)
    v0 = 0.5 * b * (1.0 + eb)
    xhat, rstd = _ln_stats(v0)
    v = (xhat * lg + lb).astype(MXU_DTYPE)
    mask = _tril_mask()
    for g in range(GROUPS):
        wt = jnp.where(mask, sw_ref[g], 0.0).astype(MXU_DTYPE)
        bcol = sbt_ref[:, g:g + 1]
        for ch in range(tm // CHUNK):
            rows = slice(ch * CHUNK, (ch + 1) * CHUNK)
            cols = slice(g * eg, (g + 1) * eg)
            mixed_s[rows, cols] = jnp.dot(wt, v[rows, cols], preferred_element_type=F32) + bcol
    return ea, eb, u, xhat, rstd, v


def _loss_head(xf, fg, tgt):
    t, d = xf.shape
    tm = TM_OUT
    nsteps = t // tm

    def body(x_ref, g_ref, t_ref, loss_ref, dx_ref, dg_ref, sq_s):
        i = pl.program_id(0)

        @pl.when(i == 0)
        def _():
            sq_s[...] = jnp.zeros_like(sq_s)
            dg_ref[...] = jnp.zeros_like(dg_ref)
        x = x_ref[...]
        rstd = _rms_rstd(x)
        err = x * rstd * g_ref[...] - t_ref[...]
        sq_s[...] += jnp.sum(err * err, axis=0, keepdims=True)
        dx, dg = _rms_bwd(err * (1.0 / d), x, rstd, g_ref[...])
        dx_ref[...] = dx
        dg_ref[...] += dg

        @pl.when(i == nsteps - 1)
        def _():
            loss_ref[...] = jnp.sum(sq_s[...], axis=1, keepdims=True) * (0.5 / d)

    row = pl.BlockSpec((tm, d), lambda i: (i, 0))
    return pl.pallas_call(
        body, name="loss_head", grid=(nsteps,),
        in_specs=[row, _whole((1, d)), row],
        out_specs=[_acc_out((1, 1)), row, _acc_out((1, d))],
        out_shape=[jax.ShapeDtypeStruct((1, 1), F32), jax.ShapeDtypeStruct((t, d), F32),
                   jax.ShapeDtypeStruct((1, d), F32)],
        scratch_shapes=[pltpu.VMEM((1, d), F32)],
        compiler_params=_params(1, 32),
    )(xf, fg, tgt)


def _ple_bwd(dx2, x1, gate, p_all, layer, plg, gate_w, proj_w):
    t, d = x1.shape
    pd = p_all.shape[-1]
    tm = TM_OUT
    nsteps = t // tm
    bn = d // N_DEV

    def body(dx2_ref, x1_ref, gate_ref, p_ref, plg_ref, gw_ref, pw_ref, dx1_ref, dgw_ref, dpw_ref, dplg_ref,
             gw_acc, pw_acc):
        i = pl.program_id(0)

        @pl.when(i == 0)
        def _():
            gw_acc[...] = jnp.zeros_like(gw_acc)
            pw_acc[...] = jnp.zeros_like(pw_acc)
            dplg_ref[...] = jnp.zeros_like(dplg_ref)
        dx2 = dx2_ref[...]
        x1 = x1_ref[...]
        plg = plg_ref[...]
        rstd = _rms_rstd(x1)
        rn = (x1 * rstd * plg).astype(MXU_DTYPE)
        gate = gate_ref[...]
        p_b = p_ref[0].astype(MXU_DTYPE)
        pp = jnp.dot(p_b, pw_ref[...], preferred_element_type=F32)
        dpp = (dx2 * gate).astype(MXU_DTYPE)
        dgpre = (dx2 * pp * gate * (1.0 - gate)).astype(MXU_DTYPE)
        pw_acc[...] += _dot_tn(p_b, dpp)
        gw_acc[...] += _dot_tn(rn, dgpre)
        drn = _dot_nt(dgpre, gw_ref[...])
        dx, dg = _rms_bwd(drn, x1, rstd, plg)
        dx1_ref[...] = dx2 + dx
        dplg_ref[...] += dg

        @pl.when(i == nsteps - 1)
        def _():
            dgw_ref[...] = gw_acc[...].astype(dgw_ref.dtype)
            for j in range(N_DEV):
                dpw_ref[j] = pw_acc[:, j * bn:(j + 1) * bn].astype(dpw_ref.dtype)

    row = lambda w: pl.BlockSpec((tm, w), lambda i: (i, 0))
    return pl.pallas_call(
        body, name="ple_bwd", grid=(nsteps,),
        in_specs=[row(d), row(d), row(d), pl.BlockSpec((1, tm, pd), lambda i: (layer, i, 0)), _whole((1, d)),
                  _whole((d, d)), _whole((pd, d))],
        out_specs=[row(d), _acc_out((d, d)), _acc_out((N_DEV, pd, bn)), _acc_out((1, d))],
        out_shape=[jax.ShapeDtypeStruct((t, d), F32), jax.ShapeDtypeStruct((d, d), WIRE_DTYPE),
                   jax.ShapeDtypeStruct((N_DEV, pd, bn), WIRE_DTYPE), jax.ShapeDtypeStruct((1, d), F32)],
        scratch_shapes=[pltpu.VMEM((d, d), F32), pltpu.VMEM((pd, d), F32)],
        compiler_params=_params(1, 48),
    )(dx2, x1, gate, p_all, plg, gate_w, proj_w)


def _outproj_bwd(dx1, y, proj, w_out):
    t, d = dx1.shape
    e = y.shape[1]
    tm = TM_OUT
    nsteps = t // tm

    def body(dx1_ref, y_ref, z_ref, wo_ref, dy_ref, dz_ref, dwo_ref, wo_acc):
        i = pl.program_id(0)

        @pl.when(i == 0)
        def _():
            wo_acc[...] = jnp.zeros_like(wo_acc)
        dx1 = dx1_ref[...].astype(MXU_DTYPE)
        y = y_ref[...]
        z = z_ref[...]
        s = _sigmoid(z)
        sz = z * s
        q = (y * sz).astype(MXU_DTYPE)
        wo_acc[...] += _dot_tn(q, dx1)
        dq = _dot_nt(dx1, wo_ref[...])
        dy_ref[...] = dq * sz
        dz_ref[...] = (dq * y * _silu_grad(z, s)).astype(dz_ref.dtype)

        @pl.when(i == nsteps - 1)
        def _():
            dwo_ref[...] = wo_acc[...].astype(dwo_ref.dtype)

    return pl.pallas_call(
        body, name="outproj_bwd", grid=(nsteps,),
        in_specs=[pl.BlockSpec((tm, d), lambda i: (i, 0)), pl.BlockSpec((tm, e), lambda i: (i, 0)),
                  pl.BlockSpec((tm, e), lambda i: (i, 2)), _whole((e, d))],
        out_specs=[pl.BlockSpec((tm, e), lambda i: (i, 0)), pl.BlockSpec((tm, e), lambda i: (i, 0)),
                   _acc_out((e, d))],
        out_shape=[jax.ShapeDtypeStruct((t, e), F32), jax.ShapeDtypeStruct((t, e), MXU_DTYPE),
                   jax.ShapeDtypeStruct((e, d), WIRE_DTYPE)],
        scratch_shapes=[pltpu.VMEM((e, d), F32)],
        compiler_params=_params(1, 60),
    )(dx1, y, proj, w_out)


def _conv_ln_bwd(dy, y1, lg, lb):
    t, e = dy.shape
    tm = TM_MIX

    def body(dy_ref, y1_ref, lg_ref, lb_ref, dy1_ref, dlg_ref, dlb_ref, dcb_ref):
        @pl.when(pl.program_id(0) == 0)
        def _():
            dlg_ref[...] = jnp.zeros_like(dlg_ref)
            dlb_ref[...] = jnp.zeros_like(dlb_ref)
            dcb_ref[...] = jnp.zeros_like(dcb_ref)
        xhat, rstd = _ln_stats(y1_ref[...])
        lg = lg_ref[...]
        y2 = xhat * lg + lb_ref[...]
        dy2 = dy_ref[...] * _silu_grad(y2, _sigmoid(y2))
        dlg_ref[...] += jnp.sum(dy2 * xhat, axis=0, keepdims=True)
        dlb_ref[...] += jnp.sum(dy2, axis=0, keepdims=True)
        dy1 = _ln_bwd(dy2 * lg, xhat, rstd)
        dy1_ref[...] = dy1
        dcb_ref[...] += jnp.sum(dy1, axis=0, keepdims=True)

    row = pl.BlockSpec((tm, e), lambda i: (i, 0))
    return pl.pallas_call(
        body, name="conv_ln_bwd", grid=(t // tm,),
        in_specs=[row, row, _whole((1, e)), _whole((1, e))],
        out_specs=[row, _acc_out((1, e)), _acc_out((1, e)), _acc_out((1, e))],
        out_shape=[jax.ShapeDtypeStruct((t, e), F32)] + [jax.ShapeDtypeStruct((1, e), F32)] * 3,
        compiler_params=_params(1, 48),
    )(dy, y1, lg, lb)


def _conv_bwd(dy1, proj, dz, cw, seq):
    t, e = dy1.shape
    tm = TM_MIX
    nt = seq // tm
    hb = tm // HALO
    n_halo_blocks = t // HALO

    def body(d_ref, dn_ref, a_ref, b_ref, ah_ref, bh_ref, dz_ref, cw_ref, dproj_ref, dcw_ref,
             y0s, d1s, zs, dsh, dcw8, w8):
        i = pl.program_id(0)
        pos = lax.rem(i, nt)

        @pl.when(i == 0)
        def _():
            dcw8[...] = jnp.zeros_like(dcw8)
        _conv_weights_to_sublanes(cw_ref, w8)
        a = a_ref[...]
        sb = _sigmoid(b_ref[...])
        y0s[pl.ds(HALO, tm), :] = a * sb
        d1s[pl.ds(0, tm), :] = d_ref[...]

        @pl.when(pos == 0)
        def _():
            y0s[pl.ds(0, HALO), :] = jnp.zeros((HALO, e), F32)

        @pl.when(pos != 0)
        def _():
            y0s[pl.ds(0, HALO), :] = ah_ref[...] * _sigmoid(bh_ref[...])

        @pl.when(pos == nt - 1)
        def _():
            d1s[pl.ds(tm, HALO), :] = jnp.zeros((HALO, e), F32)

        @pl.when(pos != nt - 1)
        def _():
            d1s[pl.ds(tm, HALO), :] = dn_ref[...]

        base = HALO - (CONV_K - 1)
        for c0 in range(0, e, DCW_CC):
            cols = slice(c0, c0 + DCW_CC)
            dcur = d_ref[:, cols]
            for s in range(1, 8):
                dsh[s - 1, pl.ds(0, 8), :] = jnp.zeros((8, DCW_CC), F32)
                dsh[s - 1, pl.ds(tm, 8), :] = jnp.zeros((8, DCW_CC), F32)
                dsh[s - 1, pl.ds(s, tm), :] = dcur
            for s in range(8):
                taps = [k for k in range(CONV_K) if (base + k) % 8 == s]
                off0 = base + taps[0] - s
                n, ch = (tm, DCW_RC) if s == 0 else (tm + 8, (tm + 8) // DCW_CHUNKS)
                sums = [None] * len(taps)
                for r in range(0, n, ch):
                    dch = d_ref[r:r + ch, cols] if s == 0 else dsh[s - 1, r:r + ch, :]
                    window = y0s[pl.ds(off0 + r, ch + 8 * (len(taps) - 1)), cols]
                    for m in range(len(taps)):
                        part = jnp.sum((dch * window[8 * m:8 * m + ch]).reshape(ch // 8, 8, DCW_CC), axis=0)
                        sums[m] = part if sums[m] is None else sums[m] + part
                for m, k in enumerate(taps):
                    dcw8[k, :, cols] += sums[m]

        def emit(r0, c0, dy0):
            rs, cs = pl.ds(r0, CONV_RC), slice(c0, c0 + CONV_CC)
            sbv = _sigmoid(b_ref[rs, cs])
            av = a_ref[rs, cs]
            dproj_ref[rs, c0:c0 + CONV_CC] = (dy0 * sbv).astype(dproj_ref.dtype)
            dproj_ref[rs, e + c0:e + c0 + CONV_CC] = (dy0 * av * sbv * (1.0 - sbv)).astype(dproj_ref.dtype)
        _conv_apply(d1s, w8, zs, 0, tm, e, True, emit)
        dproj_ref[:, 2 * e:3 * e] = dz_ref[...]

        @pl.when(i == t // tm - 1)
        def _():
            dcw_ref[...] = jnp.sum(dcw8[...], axis=1)

    tile = lambda col: pl.BlockSpec((tm, e), lambda i: (i, col))
    prev = lambda col: pl.BlockSpec((HALO, e), lambda i: (jnp.maximum(i * hb - 1, 0), col))
    nxt = pl.BlockSpec((HALO, e), lambda i: (jnp.minimum((i + 1) * hb, n_halo_blocks - 1), 0))
    return pl.pallas_call(
        body, name="conv_bwd", grid=(t // tm,),
        in_specs=[tile(0), nxt, tile(0), tile(1), prev(0), prev(1), tile(0), _whole(cw.shape)],
        out_specs=[pl.BlockSpec((tm, 3 * e), lambda i: (i, 0)), _acc_out(cw.shape)],
        out_shape=[jax.ShapeDtypeStruct((t, 3 * e), MXU_DTYPE), jax.ShapeDtypeStruct(cw.shape, F32)],
        scratch_shapes=[pltpu.VMEM((tm + HALO, e), F32), pltpu.VMEM((tm + HALO, e), F32),
                        pltpu.VMEM((7, CONV_RC + 8, CONV_CC), F32), pltpu.VMEM((7, tm + 8, DCW_CC), F32),
                        pltpu.VMEM((CONV_K, 8, e), F32), pltpu.VMEM((CONV_K, 8, e), F32)],
        compiler_params=_params(1, 56),
    )(dy1, dy1, proj, proj, proj, proj, dz, cw)


def _sgu_bwd(dy, proj, dz, lg, lb, sw, swt, sbt):
    t, e = dy.shape
    eg = e // GROUPS
    tm = TM_MIX
    nsteps = t // tm

    def body(dy_ref, a_ref, b_ref, dz_ref, lg_ref, lb_ref, sw_ref, swt_ref, sbt_ref,
             dproj_ref, dsw_ref, dsb_ref, dlg_ref, dlb_ref, mixed_s, dv_s, sb_acc):
        i = pl.program_id(0)

        @pl.when(i == 0)
        def _():
            dsw_ref[...] = jnp.zeros_like(dsw_ref)
            sb_acc[...] = jnp.zeros_like(sb_acc)
            dlg_ref[...] = jnp.zeros_like(dlg_ref)
            dlb_ref[...] = jnp.zeros_like(dlb_ref)
        a = a_ref[...]
        b = b_ref[...]
        lg = lg_ref[...]
        ea, eb, u, xhat, rstd, v = _sgu_parts(a, b, lg, lb_ref[...], sw_ref, sbt_ref, mixed_s, tm, e)
        dy = dy_ref[...]
        du = dy * mixed_s[...]
        dmixed = (dy * u).astype(MXU_DTYPE)
        mask = _tril_mask()
        mask_t = (lax.broadcasted_iota(jnp.int32, (CHUNK, CHUNK), 0)
                  <= lax.broadcasted_iota(jnp.int32, (CHUNK, CHUNK), 1))
        ones = jnp.ones((8, eg), MXU_DTYPE)
        for g in range(GROUPS):
            wtt = jnp.where(mask_t, swt_ref[g], 0.0).astype(MXU_DTYPE)
            cols = slice(g * eg, (g + 1) * eg)
            for ch in range(tm // CHUNK):
                rows = slice(ch * CHUNK, (ch + 1) * CHUNK)
                dm = dmixed[rows, cols]
                dv_s[rows, cols] = jnp.dot(wtt, dm, preferred_element_type=F32)
                dsw_ref[g] += _dot_nt(dm, v[rows, cols])
                sb_acc[g] += _dot_nt(ones, dm)
        dv = dv_s[...]
        dlg_ref[...] += jnp.sum(dv * xhat, axis=0, keepdims=True)
        dlb_ref[...] += jnp.sum(dv, axis=0, keepdims=True)
        dv0 = _ln_bwd(dv * lg, xhat, rstd)
        pdf_a = jnp.exp(-0.5 * a * a) * INV_SQRT_2PI
        pdf_b = jnp.exp(-0.5 * b * b) * INV_SQRT_2PI
        dproj_ref[:, 0:e] = (du * (0.5 * (1.0 + ea) + a * pdf_a)).astype(dproj_ref.dtype)
        dproj_ref[:, e:2 * e] = (dv0 * (0.5 * (1.0 + eb) + b * pdf_b)).astype(dproj_ref.dtype)
        dproj_ref[:, 2 * e:3 * e] = dz_ref[...]

        @pl.when(i == nsteps - 1)
        def _():
            for g in range(GROUPS):
                dsw_ref[g] = jnp.where(mask, dsw_ref[g], 0.0)
                dsb_ref[g:g + 1, :] = sb_acc[g, 0:1, :]

    tile = lambda col: pl.BlockSpec((tm, e), lambda i: (i, col))
    return pl.pallas_call(
        body, name="sgu_bwd", grid=(nsteps,),
        in_specs=[tile(0), tile(0), tile(1), tile(0), _whole((1, e)), _whole((1, e)), _whole(sw.shape),
                  _whole(swt.shape), _whole(sbt.shape)],
        out_specs=[pl.BlockSpec((tm, 3 * e), lambda i: (i, 0)), _acc_out(sw.shape), _acc_out((GROUPS, CHUNK)),
                   _acc_out((1, e)), _acc_out((1, e))],
        out_shape=[jax.ShapeDtypeStruct((t, 3 * e), MXU_DTYPE), jax.ShapeDtypeStruct(sw.shape, F32),
                   jax.ShapeDtypeStruct((GROUPS, CHUNK), F32), jax.ShapeDtypeStruct((1, e), F32),
                   jax.ShapeDtypeStruct((1, e), F32)],
        scratch_shapes=[pltpu.VMEM((tm, e), F32), pltpu.VMEM((tm, e), F32), pltpu.VMEM((GROUPS, 8, CHUNK), F32)],
        compiler_params=_params(1, 56),
    )(dy, proj, proj, dz, lg, lb, sw, swt, sbt)


def _inproj_bwd_x(dproj, w_blk, dx1, x, g):
    t, d = x.shape
    nb, _, bn = w_blk.shape
    tm = TM_IN

    def body(dp_ref, w_ref, dx1_ref, x_ref, g_ref, dx_ref, dg_ref):
        @pl.when(pl.program_id(0) == 0)
        def _():
            dg_ref[...] = jnp.zeros_like(dg_ref)
        dh = None
        for j in range(nb):
            term = _dot_nt(dp_ref[:, j * bn:(j + 1) * bn], w_ref[j])
            dh = term if dh is None else dh + term
        xv = x_ref[...]
        dx, dg = _rms_bwd(dh, xv, _rms_rstd(xv), g_ref[...])
        dx_ref[...] = dx1_ref[...] + dx
        dg_ref[...] += dg

    row = lambda w: pl.BlockSpec((tm, w), lambda i: (i, 0))
    return pl.pallas_call(
        body, name="inproj_bwd_x", grid=(t // tm,),
        in_specs=[row(nb * bn), _whole(w_blk.shape), row(d), row(d), _whole((1, d))],
        out_specs=[row(d), _acc_out((1, d))],
        out_shape=[jax.ShapeDtypeStruct((t, d), F32), jax.ShapeDtypeStruct((1, d), F32)],
        compiler_params=_params(1, 56),
    )(dproj, w_blk, dx1, x, g)


def _inproj_bwd_w(h, dproj):
    t, d = h.shape
    bn = dproj.shape[1] // N_DEV
    tm = TM_IN
    nsteps = t // tm

    nh = 2
    per = N_DEV // nh

    def body(h_ref, dp_ref, dw_ref, acc):
        i = pl.program_id(1)

        @pl.when(i == 0)
        def _():
            acc[...] = jnp.zeros_like(acc)
        hv = h_ref[...]
        for jj in range(per):
            acc[jj] += _dot_tn(hv, dp_ref[:, jj * bn:(jj + 1) * bn])

        @pl.when(i == nsteps - 1)
        def _():
            dw_ref[...] = acc[...].astype(dw_ref.dtype)

    return pl.pallas_call(
        body, name="inproj_bwd_w", grid=(nh, nsteps),
        in_specs=[pl.BlockSpec((tm, d), lambda hh, i: (i, 0)), pl.BlockSpec((tm, per * bn), lambda hh, i: (i, hh))],
        out_specs=[pl.BlockSpec((per, d, bn), lambda hh, i: (hh, 0, 0))],
        out_shape=[jax.ShapeDtypeStruct((N_DEV, d, bn), WIRE_DTYPE)],
        scratch_shapes=[pltpu.VMEM((per, d, bn), F32)],
        compiler_params=_params(2, 56),
    )(h, dproj)[0]


def _adamw(parts, w, m, v):
    nl, r, c = w.shape
    tr = r
    for cand in (512, 256, 128, 64, 32, 16, 8):
        if r % cand == 0 and cand * c * 4 <= (1 << 19):
            tr = cand
            break
    bc1 = 1.0 - ADAM_B1 ** ADAM_STEP
    bc2 = 1.0 - ADAM_B2 ** ADAM_STEP

    def body(*refs):
        p_refs = refs[:nl]
        w_ref, m_ref, v_ref, g_ref, d_ref, nm_ref, nv_ref = refs[nl:]

        def update(p_ref):
            g = p_ref[0].astype(F32)
            for s in range(1, N_DEV):
                g = g + p_ref[s].astype(F32)
            nm = ADAM_B1 * m_ref[0] + (1.0 - ADAM_B1) * g
            nv = ADAM_B2 * v_ref[0] + (1.0 - ADAM_B2) * (g * g)
            g_ref[0] = g
            nm_ref[0] = nm
            nv_ref[0] = nv
            d_ref[0] = -ADAM_LR * ((nm / bc1) / (jnp.sqrt(nv / bc2) + ADAM_EPS) + ADAM_WD * w_ref[0])

        if nl == 1:
            update(p_refs[0])
        else:
            for kk in range(nl):
                pl.when(pl.program_id(0) == kk)(lambda kk=kk: update(p_refs[kk]))

    part_spec = lambda kk: pl.BlockSpec((N_DEV, tr, c), lambda l, i: (0, jnp.where(l == kk, i, 0), 0))
    row = pl.BlockSpec((1, tr, c), lambda l, i: (l, i, 0))
    return pl.pallas_call(
        body, name="adamw", grid=(nl, r // tr),
        in_specs=[part_spec(kk) for kk in range(nl)] + [row, row, row],
        out_specs=[row] * 4,
        out_shape=[jax.ShapeDtypeStruct((nl, r, c), F32)] * 4,
        compiler_params=_params(2, 48),
    )(*parts, w, m, v)


def _adamw_small(parts, ws, ms, vs, loss_parts):
    n = len(ws)
    bc1 = 1.0 - ADAM_B1 ** ADAM_STEP
    bc2 = 1.0 - ADAM_B2 ** ADAM_STEP

    def total(ref):
        acc = ref[0]
        for s in range(1, N_DEV):
            acc = acc + ref[s]
        return acc

    def body(*refs):
        p_refs, w_refs, m_refs, v_refs = refs[:n], refs[n:2 * n], refs[2 * n:3 * n], refs[3 * n:4 * n]
        outs = refs[4 * n + 1:]
        for i in range(n):
            g = total(p_refs[i])
            nm = ADAM_B1 * m_refs[i][...] + (1.0 - ADAM_B1) * g
            nv = ADAM_B2 * v_refs[i][...] + (1.0 - ADAM_B2) * (g * g)
            outs[i][...] = g
            outs[n + i][...] = -ADAM_LR * ((nm / bc1) / (jnp.sqrt(nv / bc2) + ADAM_EPS) + ADAM_WD * w_refs[i][...])
            outs[2 * n + i][...] = nm
            outs[3 * n + i][...] = nv
        outs[4 * n][...] = total(refs[4 * n])

    vmem = pl.BlockSpec(memory_space=pltpu.VMEM)
    shapes = [jax.ShapeDtypeStruct(w.shape, F32) for w in ws]
    outs = pl.pallas_call(
        body, name="adamw_small",
        in_specs=[vmem] * (4 * n + 1), out_specs=[vmem] * (4 * n + 1),
        out_shape=shapes * 4 + [jax.ShapeDtypeStruct((8, 128), F32)],
        compiler_params=pltpu.CompilerParams(vmem_limit_bytes=48 << 20),
    )(*parts, *ws, *ms, *vs, loss_parts)
    return outs[:n], outs[n:2 * n], outs[2 * n:3 * n], outs[3 * n:4 * n], outs[4 * n]


def _pack(arrays):
    flat = jnp.concatenate([a.reshape(-1).astype(F32) for a in arrays])
    unit = 8 * PACK_COLS
    padded = -(-flat.shape[0] // unit) * unit
    return jnp.pad(flat, (0, padded - flat.shape[0])).reshape(-1, PACK_COLS)


def _unshard_last(packed, shard_shapes):
    flat = packed.reshape(N_DEV, -1)
    out, off = [], 0
    for s in shard_shapes:
        n = math.prod(s)
        a = jnp.moveaxis(flat[:, off:off + n].reshape((N_DEV,) + tuple(s)), 0, -2)
        out.append(a.reshape(tuple(s[:-1]) + (N_DEV * s[-1],)))
        off += n
    return out


def kernel(x, p, norm_g, w_in, w_out, conv_w, conv_b, conv_ln_g, conv_ln_b, sgu_ln_g, sgu_ln_b, sgu_w, sgu_b, pl_norm_g, pl_gate_w, pl_proj_w, final_g, loss_target, m_norm_g, m_w_in, m_w_out, m_conv_w, m_conv_b, m_conv_ln_g, m_conv_ln_b, m_sgu_ln_g, m_sgu_ln_b, m_sgu_w, m_sgu_b, m_pl_norm_g, m_pl_gate_w, m_pl_proj_w, m_final_g, v_norm_g, v_w_in, v_w_out, v_conv_w, v_conv_b, v_conv_ln_g, v_conv_ln_b, v_sgu_ln_g, v_sgu_ln_b, v_sgu_w, v_sgu_b, v_pl_norm_g, v_pl_gate_w, v_pl_proj_w, v_final_g):
    bsz, seq, d = x.shape
    t = bsz * seq
    depth = w_in.shape[0]
    e = w_out.shape[1] * N_DEV
    pd = p.shape[-1]
    n_conv, n_sgu = conv_w.shape[0], sgu_ln_g.shape[0]

    small_shapes = [conv_w.shape, sgu_ln_g.shape, sgu_ln_b.shape]
    cast = lambda a: a.astype(MXU_DTYPE)
    first = [cast(w_in[0]), cast(w_out[0]), cast(pl_gate_w[0]), cast(pl_proj_w), _pack([conv_w, sgu_ln_g, sgu_ln_b])]
    later = [[cast(w_in[l]), cast(w_out[l]), cast(pl_gate_w[l])] for l in range(1, depth)]
    gathered = _all_gather(first, "gather_weights")
    gather_pending, gather_tokens = {}, 0.0
    for l in range(1, depth):
        lands = _place_own(later[l - 1], False, "place_own_weights")
        send, recv, srcs, lnds, token = _exchange_start(later[l - 1], lands, gathered[0], False, f"gather_start_{l}")
        gather_pending[l] = (send, recv, srcs, lnds)
        gather_tokens = gather_tokens + token[0, 0]
    w_in_g = {0: gathered[0]}
    w_out_g = {0: gathered[1].reshape(e, d)}
    gate_g = {0: gathered[2].reshape(d, d)}
    proj_g = jnp.transpose(gathered[3], (1, 2, 0, 3)).reshape(depth, pd, d)
    conv_w_g, sgu_ln_g_g, sgu_ln_b_g = _unshard_last(gathered[4], small_shapes)
    sgu_wt = jnp.swapaxes(sgu_w, -1, -2)
    sgu_bt = jnp.swapaxes(sgu_b, -1, -2)

    xs = [x.reshape(t, d)]
    p_all = p.reshape(depth, t, pd)
    saved = []
    for l in range(depth):
        j = l // 2
        if l == 0:
            g_l = norm_g[0:1] + gather_tokens
        else:
            g_l = norm_g[l:l + 1]
            got = _exchange_wait(*gather_pending.pop(l), xs[-1], False, f"gather_wait_{l}")
            w_in_g[l], w_out_g[l], gate_g[l] = got[0], got[1].reshape(e, d), got[2].reshape(d, d)
        common = (xs[-1], p_all, l, g_l, w_in_g[l], w_out_g[l], pl_norm_g[l:l + 1], gate_g[l], proj_g[l], seq)
        if l % 2 == 0:
            h, proj, y1, y, x1, gate, x2 = _layer_fwd(
                *common, conv=(conv_w_g[j], conv_b[j:j + 1], conv_ln_g[j:j + 1], conv_ln_b[j:j + 1]))
        else:
            y1 = None
            h, proj, y, x1, gate, x2 = _layer_fwd(
                *common, sgu=(sgu_ln_g_g[j:j + 1], sgu_ln_b_g[j:j + 1], sgu_w[j], sgu_bt[j]))
        saved.append((h, proj, y1, y, x1, gate))
        xs.append(x2)

    loss_part, dx, d_final_g = _loss_head(xs[-1], final_g.reshape(1, d), loss_target.reshape(t, d))

    d_norm_g, d_pl_norm_g = [None] * depth, [None] * depth
    scatter_pending = {}
    d_conv_w, d_conv_b, d_conv_ln_g, d_conv_ln_b = [None] * n_conv, [None] * n_conv, [None] * n_conv, [None] * n_conv
    d_sgu_ln_g, d_sgu_ln_b, d_sgu_w, d_sgu_b = [None] * n_sgu, [None] * n_sgu, [None] * n_sgu, [None] * n_sgu
    def scatter(parts, name):
        send, recv, srcs, lnds, token = _exchange_start(parts, _place_own(parts, True, "place_own_grads"), parts[0],
                                                        True, name)
        return (send, recv, srcs, lnds), token[0, 0]

    for l in reversed(range(depth)):
        j = l // 2
        h, proj, y1, y, x1, gate = saved[l]
        dx1, dgate_p, dprojw_p, d_pl_norm_g[l] = _ple_bwd(dx, x1, gate, p_all, l, pl_norm_g[l:l + 1], gate_g[l],
                                                         proj_g[l])
        dy, dz, dw_out_p = _outproj_bwd(dx1, y, proj, w_out_g[l])
        early = [dw_out_p.reshape(N_DEV, e // N_DEV, d), dgate_p.reshape(N_DEV, d // N_DEV, d), dprojw_p]
        early_token = 0.0
        if l == 0:
            scatter_pending["0_early"], early_token = scatter(early, "scatter_start_0_early")
            early = []
        if l % 2 == 0:
            dy1, d_conv_ln_g[j], d_conv_ln_b[j], d_conv_b[j] = _conv_ln_bwd(
                dy, y1, conv_ln_g[j:j + 1] + early_token, conv_ln_b[j:j + 1])
            dproj, d_conv_w[j] = _conv_bwd(dy1, proj, dz, conv_w_g[j], seq)
        else:
            dproj, d_sgu_w[j], d_sgu_b[j], d_sgu_ln_g[j], d_sgu_ln_b[j] = _sgu_bwd(
                dy, proj, dz, sgu_ln_g_g[j:j + 1] + early_token, sgu_ln_b_g[j:j + 1], sgu_w[j], sgu_wt[j], sgu_bt[j])
        scatter_pending[l], token = scatter([_inproj_bwd_w(h, dproj)] + early, f"scatter_start_{l}")
        dx, d_norm_g[l] = _inproj_bwd_x(dproj, w_in_g[l], dx1, xs[l], norm_g[l:l + 1] + token)
    grad_x = dx.reshape(bsz, seq, d)

    def own_eighths(full):
        return jnp.moveaxis(full.reshape(full.shape[:-1] + (N_DEV, full.shape[-1] // N_DEV)), -2, 0)

    small_parts = [own_eighths(jnp.stack(d_conv_w)), own_eighths(jnp.concatenate(d_sgu_ln_g, axis=0)),
                   own_eighths(jnp.concatenate(d_sgu_ln_b, axis=0))]
    rep_parts = [jnp.concatenate(d_norm_g, axis=0), jnp.concatenate(d_conv_b, axis=0),
                 jnp.concatenate(d_conv_ln_g, axis=0), jnp.concatenate(d_conv_ln_b, axis=0), jnp.stack(d_sgu_w),
                 jnp.stack(d_sgu_b), jnp.concatenate(d_pl_norm_g, axis=0), d_final_g,
                 jnp.broadcast_to(loss_part, (8, 128))]
    small_send, small_recv, small_srcs, small_lnds, small_token = _exchange_start(
        small_parts, _place_own(small_parts, True, "place_own_small"), grad_x, True, "scatter_small_start")
    rep_send, rep_recv, rep_srcs, rep_lnds, rep_token = _exchange_start(
        rep_parts, _place_own(rep_parts, False, "place_own_replicated"), grad_x, False, "gather_replicated_start")

    landed = {}
    for key in list(scatter_pending):
        landed[key] = _exchange_wait(*scatter_pending.pop(key), small_token + rep_token, True, f"scatter_wait_{key}")
    dw_in_l = [landed[l][0] for l in range(depth)]
    rest = [landed["0_early"]] + [landed[l][1:] for l in range(1, depth)]

    o_w_in = _adamw(dw_in_l, w_in, m_w_in, v_w_in)
    o_w_out = _adamw([r[0] for r in rest], w_out, m_w_out, v_w_out)
    o_gate = _adamw([r[1] for r in rest], pl_gate_w, m_pl_gate_w, v_pl_gate_w)
    o_projw = _adamw([r[2] for r in rest], pl_proj_w, m_pl_proj_w, v_pl_proj_w)
    r_small = _exchange_wait(small_send, small_recv, small_srcs, small_lnds, o_projw[1], True, "scatter_small_wait")
    r_rep = _exchange_wait(rep_send, rep_recv, rep_srcs, rep_lnds, o_w_in[1], False, "gather_replicated_wait")

    row = lambda a: a.reshape(1, -1)
    o_small = _adamw_small(
        r_rep[:8] + r_small,
        [norm_g, conv_b, conv_ln_g, conv_ln_b, sgu_w, sgu_b, pl_norm_g, row(final_g), conv_w, sgu_ln_g, sgu_ln_b],
        [m_norm_g, m_conv_b, m_conv_ln_g, m_conv_ln_b, m_sgu_w, m_sgu_b, m_pl_norm_g, row(m_final_g), m_conv_w,
         m_sgu_ln_g, m_sgu_ln_b],
        [v_norm_g, v_conv_b, v_conv_ln_g, v_conv_ln_b, v_sgu_w, v_sgu_b, v_pl_norm_g, row(v_final_g), v_conv_w,
         v_sgu_ln_g, v_sgu_ln_b],
        r_rep[8])
    loss = o_small[4][0, 0]

    def leaf(kind):
        sm = o_small[kind]
        return [sm[0], o_w_in[kind], o_w_out[kind], sm[8], sm[1], sm[2], sm[3], sm[9], sm[10], sm[4], sm[5], sm[6],
                o_gate[kind], o_projw[kind], sm[7].reshape(final_g.shape)]

    return (loss, grad_x, *leaf(0), *leaf(1), *leaf(2), *leaf(3))
```

```python
import math

import jax
import jax.numpy as jnp
from jax import lax
from jax.experimental import pallas as pl
from jax.experimental.pallas import tpu as pltpu

F32 = jnp.float32
MXU_DTYPE = jnp.bfloat16
WIRE_DTYPE = jnp.bfloat16

EPS = 1e-6
CONV_K = 31
CHUNK = 128
GROUPS = 8
HALO = 32
N_DEV = 8
DEPTH = 4

ADAM_LR = 0.001
ADAM_B1 = 0.9
ADAM_B2 = 0.999
ADAM_EPS = 1e-08
ADAM_WD = 0.01
ADAM_STEP = 10

TM_IN = 512
TM_MIX = 256
TM_OUT = 512
FUSE_SB = 256
CONV_RC = 64
CONV_CC = 128
DCW_CC = 256
DCW_RC = 64
DCW_CHUNKS = 3
PACK_COLS = 1024

MESH_ID = pl.DeviceIdType.MESH
INV_SQRT2 = 1.0 / math.sqrt(2.0)
INV_SQRT_2PI = 1.0 / math.sqrt(2.0 * math.pi)


def _params(n_grid, vmem_mb):
    return pltpu.CompilerParams(dimension_semantics=("arbitrary",) * n_grid, vmem_limit_bytes=vmem_mb << 20)


def _whole(shape):
    nd = len(shape)
    return pl.BlockSpec(shape, lambda *_: (0,) * nd, pipeline_mode=pl.Buffered(1))


def _acc_out(shape):
    nd = len(shape)
    return pl.BlockSpec(shape, lambda *_: (0,) * nd)


def _dot(a, b):
    return jnp.dot(a.astype(MXU_DTYPE), b.astype(MXU_DTYPE), preferred_element_type=F32)


def _dot_nt(a, b):
    return lax.dot_general(a.astype(MXU_DTYPE), b.astype(MXU_DTYPE), (((1,), (1,)), ((), ())),
                           preferred_element_type=F32)


def _dot_tn(a, b):
    return lax.dot_general(a.astype(MXU_DTYPE), b.astype(MXU_DTYPE), (((0,), (0,)), ((), ())),
                           preferred_element_type=F32)


def _sigmoid(x):
    return jax.nn.sigmoid(x)


def _rms_rstd(x):
    return lax.rsqrt(jnp.mean(x * x, axis=-1, keepdims=True) + EPS)


def _rms_bwd(dy, x, rstd, g):
    gy = dy * g
    xr = x * rstd
    dx = rstd * (gy - xr * jnp.mean(gy * xr, axis=-1, keepdims=True))
    dg = jnp.sum(dy * xr, axis=0, keepdims=True)
    return dx, dg


def _ln_stats(x):
    mu = jnp.mean(x, axis=-1, keepdims=True)
    xc = x - mu
    var = jnp.mean(xc * xc, axis=-1, keepdims=True)
    rstd = lax.rsqrt(var + EPS)
    return xc * rstd, rstd


def _ln_bwd(dxhat, xhat, rstd):
    return rstd * (dxhat - jnp.mean(dxhat, axis=-1, keepdims=True)
                   - xhat * jnp.mean(dxhat * xhat, axis=-1, keepdims=True))


def _silu_grad(x, s):
    return s * (1.0 + x * (1.0 - s))


def _tril_mask():
    r = lax.broadcasted_iota(jnp.int32, (CHUNK, CHUNK), 0)
    c = lax.broadcasted_iota(jnp.int32, (CHUNK, CHUNK), 1)
    return r >= c


def _conv_weights_to_sublanes(w_ref, w8_ref):
    for k in range(CONV_K):
        w8_ref[k] = jnp.broadcast_to(w_ref[k:k + 1, :], w8_ref.shape[1:])


def _conv_apply(src_ref, w8_ref, zs_ref, base, tm, e, flip, emit):
    def row_block(i, carry):
        r0 = pl.multiple_of(i * CONV_RC, CONV_RC)
        for c0 in range(0, e, CONV_CC):
            cols = slice(c0, c0 + CONV_CC)
            acc = None
            for s in range(8):
                nrows = CONV_RC if s == 0 else CONV_RC + 8
                taps = [k for k in range(CONV_K) if (base + k) % 8 == s]
                off0 = base + taps[0] - s
                span = nrows + 8 * (len(taps) - 1)
                window = src_ref[pl.ds(r0 + off0, span), cols].reshape(span // 8, 8, CONV_CC)
                z = None
                for m, k in enumerate(taps):
                    wk = (CONV_K - 1 - k) if flip else k
                    term = w8_ref[wk, :, cols][None] * window[m:m + nrows // 8]
                    z = term if z is None else z + term
                z = z.reshape(nrows, CONV_CC)
                if s == 0:
                    acc = z
                else:
                    zs_ref[s - 1, pl.ds(0, nrows), :] = z
                    acc = acc + zs_ref[s - 1, pl.ds(s, CONV_RC), :]
            emit(r0, c0, acc)
        return carry

    lax.fori_loop(0, tm // CONV_RC, row_block, 0)


def _mesh_pos():
    return lax.axis_index("x"), lax.axis_index("y"), lax.axis_index("c")


def _slot(px, py, pc):
    return 4 * px + 2 * py + pc


def _peers(x, y, c):
    return [((1 - x) if (k & 4) else x, (1 - y) if (k & 2) else y, (1 - c) if (k & 1) else c)
            for k in range(1, N_DEV)]


HBM_SPEC = pl.BlockSpec(memory_space=pltpu.HBM)
SEM_SPEC = pl.BlockSpec(memory_space=pltpu.SEMAPHORE)
SIDE_EFFECT = pltpu.SideEffectType.DATAFLOW_SIDE_EFFECTING


def _exchange_copy(src_refs, land_refs, send_sems, recv_sems, i, k, peer, scatter, me):
    slot = _slot(*peer)
    return pltpu.make_async_remote_copy(
        src_ref=src_refs[i].at[slot] if scatter else src_refs[i],
        dst_ref=land_refs[i].at[me if me is not None else slot],
        send_sem=send_sems.at[i * 7 + k], recv_sem=recv_sems.at[i * 7 + k],
        device_id=peer, device_id_type=MESH_ID)


def _exchange_start(srcs, lands, after, scatter, name):
    n = len(srcs)

    def body(*refs):
        src_refs, land_refs = refs[:n], refs[n:2 * n]
        send_sems, recv_sems, token = refs[2 * n + 1], refs[2 * n + 2], refs[-1]
        x, y, c = _mesh_pos()
        me = _slot(x, y, c)
        for i in range(n):
            for k, peer in enumerate(_peers(x, y, c)):
                _exchange_copy(src_refs, land_refs, send_sems, recv_sems, i, k, peer, scatter, me).start()
        token[...] = jnp.zeros_like(token)

    arrays = list(srcs) + list(lands)
    outs = pl.pallas_call(
        body, name=name,
        out_shape=(pltpu.SemaphoreType.DMA((7 * n,)), pltpu.SemaphoreType.DMA((7 * n,)),
                   *[pltpu.HBM(a.shape, a.dtype) for a in lands], jax.ShapeDtypeStruct((8, 128), F32)),
        in_specs=[HBM_SPEC] * (2 * n) + [pl.BlockSpec(memory_space=pl.ANY)],
        out_specs=(SEM_SPEC, SEM_SPEC, *[HBM_SPEC] * n, pl.BlockSpec(memory_space=pltpu.VMEM)),
        input_output_aliases={n + i: 2 + i for i in range(n)},
        compiler_params=pltpu.CompilerParams(has_side_effects=SIDE_EFFECT),
    )(*[pltpu.with_memory_space_constraint(a, pltpu.HBM) for a in arrays], after)
    return outs[0], outs[1], list(srcs), list(outs[2:2 + n]), outs[-1]


def _exchange_wait(send_sems, recv_sems, srcs, lands, after, scatter, name):
    n = len(srcs)

    def body(*refs):
        src_refs, land_refs = refs[:n], refs[n:2 * n]
        send, recv = refs[2 * n], refs[2 * n + 1]
        x, y, c = _mesh_pos()
        for i in range(n):
            for k, peer in enumerate(_peers(x, y, c)):
                cp = _exchange_copy(src_refs, land_refs, send, recv, i, k, peer, scatter, None)
                cp.wait_send()
                cp.wait_recv()

    arrays = list(srcs) + list(lands)
    outs = pl.pallas_call(
        body, name=name,
        out_shape=tuple(pltpu.HBM(a.shape, a.dtype) for a in lands),
        in_specs=[HBM_SPEC] * (2 * n) + [SEM_SPEC, SEM_SPEC, pl.BlockSpec(memory_space=pl.ANY)],
        out_specs=tuple([HBM_SPEC] * n),
        input_output_aliases={n + i: i for i in range(n)},
        compiler_params=pltpu.CompilerParams(has_side_effects=SIDE_EFFECT),
    )(*arrays, send_sems, recv_sems, after)
    return list(outs)


def _place_own(parts, scatter, name):
    n = len(parts)
    me = jnp.reshape(_slot(*_mesh_pos()), (1,)).astype(jnp.int32)

    def body(me_ref, *refs):
        for i in range(n):
            refs[n + i][0] = refs[i][0] if scatter else refs[i][...]

    def slot_spec(shape):
        rest = len(shape)
        return pl.BlockSpec((1,) + tuple(shape), lambda i, me_ref: (me_ref[0],) + (0,) * rest)

    def whole_spec(shape):
        nd = len(shape)
        return pl.BlockSpec(tuple(shape), lambda i, me_ref: (0,) * nd)

    blocks = [a.shape[1:] if scatter else a.shape for a in parts]
    return pl.pallas_call(
        body, name=name,
        grid_spec=pltpu.PrefetchScalarGridSpec(
            num_scalar_prefetch=1, grid=(1,),
            in_specs=[slot_spec(b) if scatter else whole_spec(b) for b in blocks],
            out_specs=[slot_spec(b) for b in blocks]),
        out_shape=[jax.ShapeDtypeStruct((N_DEV,) + tuple(b), a.dtype) for a, b in zip(parts, blocks)],
        compiler_params=_params(1, 32),
    )(me, *parts)


def _all_gather(items, name):
    n = len(items)

    def body(*refs):
        in_refs, out_refs = refs[:n], refs[n:2 * n]
        send_sems, recv_sems, local_sems = refs[2 * n:]
        x, y, c = _mesh_pos()
        me, sibling = (x, y, c), (x, y, 1 - c)
        chips = [(1 - x, y), (x, 1 - y), (1 - x, 1 - y)]

        def copy(i, k, block, to, src=None):
            dst = out_refs[i].at[_slot(*block)]
            return pltpu.make_async_remote_copy(
                src_ref=dst if src is None else src, dst_ref=dst,
                send_sem=send_sems.at[i * 7 + k], recv_sem=recv_sems.at[i * 7 + k],
                device_id=to, device_id_type=MESH_ID)

        mine = [pltpu.make_async_copy(in_refs[i], out_refs[i].at[_slot(*me)], local_sems.at[i]) for i in range(n)]
        for cp in mine:
            cp.start()
        first = []
        for i in range(n):
            first.append(copy(i, 0, me, sibling, src=in_refs[i]))
            for j, chip in enumerate(chips):
                first.append(copy(i, 1 + j, me, (*chip, c), src=in_refs[i]))
        for cp in first:
            cp.start()
        passed = []
        for j, chip in enumerate(chips):
            for i in range(n):
                copy(i, 1 + j, (*chip, c), me).wait_recv()
                fwd = copy(i, 4 + j, (*chip, c), sibling)
                fwd.start()
                passed.append(fwd)
        for i in range(n):
            copy(i, 0, sibling, me).wait_recv()
            for j, chip in enumerate(chips):
                copy(i, 4 + j, (*chip, 1 - c), me).wait_recv()
        for cp in first + passed:
            cp.wait_send()
        for cp in mine:
            cp.wait()

    any_spec = pl.BlockSpec(memory_space=pl.ANY)
    return pl.pallas_call(
        body, name=name,
        out_shape=[jax.ShapeDtypeStruct((N_DEV,) + a.shape, a.dtype) for a in items],
        in_specs=[any_spec] * n, out_specs=[any_spec] * n,
        scratch_shapes=[pltpu.SemaphoreType.DMA((7 * n,)), pltpu.SemaphoreType.DMA((7 * n,)),
                        pltpu.SemaphoreType.DMA((n,))],
    )(*items)


def _layer_fwd(x, p_all, layer, g, w_blk, w_out, plg, gate_w, proj_w, seq, conv=None, sgu=None):
    t, d = x.shape
    nb, _, bn = w_blk.shape
    e = w_out.shape[0]
    pd = p_all.shape[-1]
    tm = TM_MIX
    nt = seq // tm
    is_conv = conv is not None
    mixer_args = conv if is_conv else sgu
    n_mix = len(mixer_args)

    def body(*refs):
        x_ref, p_ref, g_ref, w_ref, wo_ref, plg_ref, gw_ref, pw_ref = refs[:8]
        mix = refs[8:8 + n_mix]
        outs = refs[8 + n_mix:]
        if is_conv:
            cw_ref, cb_ref, lg_ref, lb_ref = mix
            h_ref, proj_ref, y1_ref, x1_ref, gate_ref, x2_ref, y0s, zs, w8 = outs

            @pl.when(lax.rem(pl.program_id(0), nt) == 0)
            def _():
                y0s[pl.ds(0, HALO), :] = jnp.zeros((HALO, e), F32)
            _conv_weights_to_sublanes(cw_ref, w8)
        else:
            lg_ref, lb_ref, sw_ref, sbt_ref = mix
            h_ref, proj_ref, x1_ref, gate_ref, x2_ref, mixed_s = outs

        for sb in range(tm // FUSE_SB):
            rows = pl.ds(sb * FUSE_SB, FUSE_SB)
            xv = x_ref[rows, :]
            hv = (xv * _rms_rstd(xv) * g_ref[...]).astype(MXU_DTYPE)
            h_ref[rows, :] = hv
            for j in range(nb):
                proj_ref[rows, j * bn:(j + 1) * bn] = jnp.dot(hv, w_ref[j], preferred_element_type=F32)
            if is_conv:
                y0s[pl.ds(HALO + sb * FUSE_SB, FUSE_SB), :] = proj_ref[rows, 0:e] * _sigmoid(proj_ref[rows, e:2 * e])

        if is_conv:
            def emit(r0, c0, acc):
                y1_ref[pl.ds(r0, CONV_RC), c0:c0 + CONV_CC] = acc + cb_ref[:, c0:c0 + CONV_CC]
            _conv_apply(y0s, w8, zs, HALO - (CONV_K - 1), tm, e, False, emit)

        for sb in range(tm // FUSE_SB):
            rows = pl.ds(sb * FUSE_SB, FUSE_SB)
            if is_conv:
                xhat, _ = _ln_stats(y1_ref[rows, :])
                y2 = xhat * lg_ref[...] + lb_ref[...]
                y = y2 * _sigmoid(y2)
            else:
                _, _, u, _, _, _ = _sgu_parts(proj_ref[rows, 0:e], proj_ref[rows, e:2 * e], lg_ref[...], lb_ref[...],
                                              sw_ref, sbt_ref, mixed_s, FUSE_SB, e)
                y = u * mixed_s[...]
            z = proj_ref[rows, 2 * e:3 * e]
            q = (y * (z * _sigmoid(z))).astype(MXU_DTYPE)
            x1 = x_ref[rows, :] + jnp.dot(q, wo_ref[...], preferred_element_type=F32)
            x1_ref[rows, :] = x1
            rn = x1 * _rms_rstd(x1) * plg_ref[...]
            gate = _sigmoid(_dot(rn, gw_ref[...]))
            gate_ref[rows, :] = gate
            x2_ref[rows, :] = x1 + gate * _dot(p_ref[0, rows, :], pw_ref[...])

        if is_conv:
            y0s[pl.ds(0, HALO), :] = y0s[pl.ds(tm, HALO), :]

    row = lambda w: pl.BlockSpec((tm, w), lambda i: (i, 0))
    f32 = lambda w: jax.ShapeDtypeStruct((t, w), F32)
    out_shape = ([jax.ShapeDtypeStruct((t, d), MXU_DTYPE), f32(3 * e)] + ([f32(e)] if is_conv else [])
                 + [f32(d), f32(d), f32(d)])
    out_specs = [row(d), row(3 * e)] + ([row(e)] if is_conv else []) + [row(d), row(d), row(d)]
    scratch = ([pltpu.VMEM((tm + HALO, e), F32), pltpu.VMEM((7, CONV_RC + 8, CONV_CC), F32),
                pltpu.VMEM((CONV_K, 8, e), F32)] if is_conv else [pltpu.VMEM((FUSE_SB, e), F32)])
    return pl.pallas_call(
        body, name="layer_fwd_conv" if is_conv else "layer_fwd_sgu", grid=(t // tm,),
        in_specs=[row(d), pl.BlockSpec((1, tm, pd), lambda i: (layer, i, 0)), _whole((1, d)), _whole(w_blk.shape),
                  _whole((e, d)), _whole((1, d)),
                  _whole((d, d)), _whole((pd, d))] + [_whole(a.shape) for a in mixer_args],
        out_specs=out_specs, out_shape=out_shape, scratch_shapes=scratch,
        compiler_params=_params(1, 60),
    )(x, p_all, g, w_blk, w_out, plg, gate_w, proj_w, *mixer_args)


def _sgu_parts(a, b, lg, lb, sw_ref, sbt_ref, mixed_s, tm, e):
    eg = e // GROUPS
    ea = lax.erf(a * INV_SQRT2)
    eb = lax.erf(b * INV_SQRT2)
    u = 0.5 * a * (1.0 + ea)
    v0 = 0.5 * b * (1.0 + eb)
    xhat, rstd = _ln_stats(v0)
    v = (xhat * lg + lb).astype(MXU_DTYPE)
    mask = _tril_mask()
    for g in range(GROUPS):
        wt = jnp.where(mask, sw_ref[g], 0.0).astype(MXU_DTYPE)
        bcol = sbt_ref[:, g:g + 1]
        for ch in range(tm // CHUNK):
            rows = slice(ch * CHUNK, (ch + 1) * CHUNK)
            cols = slice(g * eg, (g + 1) * eg)
            mixed_s[rows, cols] = jnp.dot(wt, v[rows, cols], preferred_element_type=F32) + bcol
    return ea, eb, u, xhat, rstd, v


def _loss_head(xf, fg, tgt):
    t, d = xf.shape
    tm = TM_OUT
    nsteps = t // tm

    def body(x_ref, g_ref, t_ref, loss_ref, dx_ref, dg_ref, sq_s):
        i = pl.program_id(0)

        @pl.when(i == 0)
        def _():
            sq_s[...] = jnp.zeros_like(sq_s)
            dg_ref[...] = jnp.zeros_like(dg_ref)
        x = x_ref[...]
        rstd = _rms_rstd(x)
        err = x * rstd * g_ref[...] - t_ref[...]
        sq_s[...] += jnp.sum(err * err, axis=0, keepdims=True)
        dx, dg = _rms_bwd(err * (1.0 / d), x, rstd, g_ref[...])
        dx_ref[...] = dx
        dg_ref[...] += dg

        @pl.when(i == nsteps - 1)
        def _():
            loss_ref[...] = jnp.sum(sq_s[...], axis=1, keepdims=True) * (0.5 / d)

    row = pl.BlockSpec((tm, d), lambda i: (i, 0))
    return pl.pallas_call(
        body, name="loss_head", grid=(nsteps,),
        in_specs=[row, _whole((1, d)), row],
        out_specs=[_acc_out((1, 1)), row, _acc_out((1, d))],
        out_shape=[jax.ShapeDtypeStruct((1, 1), F32), jax.ShapeDtypeStruct((t, d), F32),
                   jax.ShapeDtypeStruct((1, d), F32)],
        scratch_shapes=[pltpu.VMEM((1, d), F32)],
        compiler_params=_params(1, 32),
    )(xf, fg, tgt)


def _ple_bwd(dx2, x1, gate, p_all, layer, plg, gate_w, proj_w):
    t, d = x1.shape
    pd = p_all.shape[-1]
    tm = TM_OUT
    nsteps = t // tm
    bn = d // N_DEV

    def body(dx2_ref, x1_ref, gate_ref, p_ref, plg_ref, gw_ref, pw_ref, dx1_ref, dgw_ref, dpw_ref, dplg_ref,
             gw_acc, pw_acc):
        i = pl.program_id(0)

        @pl.when(i == 0)
        def _():
            gw_acc[...] = jnp.zeros_like(gw_acc)
            pw_acc[...] = jnp.zeros_like(pw_acc)
            dplg_ref[...] = jnp.zeros_like(dplg_ref)
        dx2 = dx2_ref[...]
        x1 = x1_ref[...]
        plg = plg_ref[...]
        rstd = _rms_rstd(x1)
        rn = (x1 * rstd * plg).astype(MXU_DTYPE)
        gate = gate_ref[...]
        p_b = p_ref[0].astype(MXU_DTYPE)
        pp = jnp.dot(p_b, pw_ref[...], preferred_element_type=F32)
        dpp = (dx2 * gate).astype(MXU_DTYPE)
        dgpre = (dx2 * pp * gate * (1.0 - gate)).astype(MXU_DTYPE)
        pw_acc[...] += _dot_tn(p_b, dpp)
        gw_acc[...] += _dot_tn(rn, dgpre)
        drn = _dot_nt(dgpre, gw_ref[...])
        dx, dg = _rms_bwd(drn, x1, rstd, plg)
        dx1_ref[...] = dx2 + dx
        dplg_ref[...] += dg

        @pl.when(i == nsteps - 1)
        def _():
            dgw_ref[...] = gw_acc[...].astype(dgw_ref.dtype)
            for j in range(N_DEV):
                dpw_ref[j] = pw_acc[:, j * bn:(j + 1) * bn].astype(dpw_ref.dtype)

    row = lambda w: pl.BlockSpec((tm, w), lambda i: (i, 0))
    return pl.pallas_call(
        body, name="ple_bwd", grid=(nsteps,),
        in_specs=[row(d), row(d), row(d), pl.BlockSpec((1, tm, pd), lambda i: (layer, i, 0)), _whole((1, d)),
                  _whole((d, d)), _whole((pd, d))],
        out_specs=[row(d), _acc_out((d, d)), _acc_out((N_DEV, pd, bn)), _acc_out((1, d))],
        out_shape=[jax.ShapeDtypeStruct((t, d), F32), jax.ShapeDtypeStruct((d, d), WIRE_DTYPE),
                   jax.ShapeDtypeStruct((N_DEV, pd, bn), WIRE_DTYPE), jax.ShapeDtypeStruct((1, d), F32)],
        scratch_shapes=[pltpu.VMEM((d, d), F32), pltpu.VMEM((pd, d), F32)],
        compiler_params=_params(1, 48),
    )(dx2, x1, gate, p_all, plg, gate_w, proj_w)


def _outproj_conv_bwd(dx1, y1, proj, w_out, lg, lb):
    t, d = dx1.shape
    e = y1.shape[1]
    tm = TM_MIX
    nsteps = t // tm

    def body(dx1_ref, y1_ref, z_ref, wo_ref, lg_ref, lb_ref, dy1_ref, dz_ref, dwo_ref, dlg_ref, dlb_ref, dcb_ref,
             wo_acc):
        i = pl.program_id(0)

        @pl.when(i == 0)
        def _():
            wo_acc[...] = jnp.zeros_like(wo_acc)
            dlg_ref[...] = jnp.zeros_like(dlg_ref)
            dlb_ref[...] = jnp.zeros_like(dlb_ref)
            dcb_ref[...] = jnp.zeros_like(dcb_ref)
        xhat, rstd = _ln_stats(y1_ref[...])
        lg = lg_ref[...]
        y2 = xhat * lg + lb_ref[...]
        s2 = _sigmoid(y2)
        y = y2 * s2
        z = z_ref[...]
        s = _sigmoid(z)
        sz = z * s
        dx1 = dx1_ref[...].astype(MXU_DTYPE)
        wo_acc[...] += _dot_tn((y * sz).astype(MXU_DTYPE), dx1)
        dq = _dot_nt(dx1, wo_ref[...])
        dz_ref[...] = (dq * y * _silu_grad(z, s)).astype(dz_ref.dtype)
        dy2 = dq * sz * _silu_grad(y2, s2)
        dlg_ref[...] += jnp.sum(dy2 * xhat, axis=0, keepdims=True)
        dlb_ref[...] += jnp.sum(dy2, axis=0, keepdims=True)
        dy1 = _ln_bwd(dy2 * lg, xhat, rstd)
        dy1_ref[...] = dy1
        dcb_ref[...] += jnp.sum(dy1, axis=0, keepdims=True)

        @pl.when(i == nsteps - 1)
        def _():
            dwo_ref[...] = wo_acc[...].astype(dwo_ref.dtype)

    row = lambda w: pl.BlockSpec((tm, w), lambda i: (i, 0))
    return pl.pallas_call(
        body, name="outproj_conv_bwd", grid=(nsteps,),
        in_specs=[row(d), row(e), pl.BlockSpec((tm, e), lambda i: (i, 2)), _whole((e, d)), _whole((1, e)),
                  _whole((1, e))],
        out_specs=[row(e), row(e), _acc_out((e, d)), _acc_out((1, e)), _acc_out((1, e)), _acc_out((1, e))],
        out_shape=[jax.ShapeDtypeStruct((t, e), F32), jax.ShapeDtypeStruct((t, e), MXU_DTYPE),
                   jax.ShapeDtypeStruct((e, d), WIRE_DTYPE)] + [jax.ShapeDtypeStruct((1, e), F32)] * 3,
        scratch_shapes=[pltpu.VMEM((e, d), F32)],
        compiler_params=_params(1, 56),
    )(dx1, y1, proj, w_out, lg, lb)


def _conv_bwd(dy1, proj, dz, cw, seq):
    t, e = dy1.shape
    tm = TM_MIX
    nt = seq // tm
    hb = tm // HALO
    n_halo_blocks = t // HALO

    def body(d_ref, dn_ref, a_ref, b_ref, ah_ref, bh_ref, dz_ref, cw_ref, dproj_ref, dcw_ref,
             y0s, d1s, zs, dsh, dcw8, w8):
        i = pl.program_id(0)
        pos = lax.rem(i, nt)

        @pl.when(i == 0)
        def _():
            dcw8[...] = jnp.zeros_like(dcw8)
        _conv_weights_to_sublanes(cw_ref, w8)
        a = a_ref[...]
        sb = _sigmoid(b_ref[...])
        y0s[pl.ds(HALO, tm), :] = a * sb
        d1s[pl.ds(0, tm), :] = d_ref[...]

        @pl.when(pos == 0)
        def _():
            y0s[pl.ds(0, HALO), :] = jnp.zeros((HALO, e), F32)

        @pl.when(pos != 0)
        def _():
            y0s[pl.ds(0, HALO), :] = ah_ref[...] * _sigmoid(bh_ref[...])

        @pl.when(pos == nt - 1)
        def _():
            d1s[pl.ds(tm, HALO), :] = jnp.zeros((HALO, e), F32)

        @pl.when(pos != nt - 1)
        def _():
            d1s[pl.ds(tm, HALO), :] = dn_ref[...]

        base = HALO - (CONV_K - 1)
        for c0 in range(0, e, DCW_CC):
            cols = slice(c0, c0 + DCW_CC)
            dcur = d_ref[:, cols]
            for s in range(1, 8):
                dsh[s - 1, pl.ds(0, 8), :] = jnp.zeros((8, DCW_CC), F32)
                dsh[s - 1, pl.ds(tm, 8), :] = jnp.zeros((8, DCW_CC), F32)
                dsh[s - 1, pl.ds(s, tm), :] = dcur
            for s in range(8):
                taps = [k for k in range(CONV_K) if (base + k) % 8 == s]
                off0 = base + taps[0] - s
                n, ch = (tm, DCW_RC) if s == 0 else (tm + 8, (tm + 8) // DCW_CHUNKS)
                sums = [None] * len(taps)
                for r in range(0, n, ch):
                    dch = d_ref[r:r + ch, cols] if s == 0 else dsh[s - 1, r:r + ch, :]
                    window = y0s[pl.ds(off0 + r, ch + 8 * (len(taps) - 1)), cols]
                    for m in range(len(taps)):
                        part = jnp.sum((dch * window[8 * m:8 * m + ch]).reshape(ch // 8, 8, DCW_CC), axis=0)
                        sums[m] = part if sums[m] is None else sums[m] + part
                for m, k in enumerate(taps):
                    dcw8[k, :, cols] += sums[m]

        def emit(r0, c0, dy0):
            rs, cs = pl.ds(r0, CONV_RC), slice(c0, c0 + CONV_CC)
            sbv = _sigmoid(b_ref[rs, cs])
            av = a_ref[rs, cs]
            dproj_ref[rs, c0:c0 + CONV_CC] = (dy0 * sbv).astype(dproj_ref.dtype)
            dproj_ref[rs, e + c0:e + c0 + CONV_CC] = (dy0 * av * sbv * (1.0 - sbv)).astype(dproj_ref.dtype)
        _conv_apply(d1s, w8, zs, 0, tm, e, True, emit)
        dproj_ref[:, 2 * e:3 * e] = dz_ref[...]

        @pl.when(i == t // tm - 1)
        def _():
            dcw_ref[...] = jnp.sum(dcw8[...], axis=1)

    tile = lambda col: pl.BlockSpec((tm, e), lambda i: (i, col))
    prev = lambda col: pl.BlockSpec((HALO, e), lambda i: (jnp.maximum(i * hb - 1, 0), col))
    nxt = pl.BlockSpec((HALO, e), lambda i: (jnp.minimum((i + 1) * hb, n_halo_blocks - 1), 0))
    return pl.pallas_call(
        body, name="conv_bwd", grid=(t // tm,),
        in_specs=[tile(0), nxt, tile(0), tile(1), prev(0), prev(1), tile(0), _whole(cw.shape)],
        out_specs=[pl.BlockSpec((tm, 3 * e), lambda i: (i, 0)), _acc_out(cw.shape)],
        out_shape=[jax.ShapeDtypeStruct((t, 3 * e), MXU_DTYPE), jax.ShapeDtypeStruct(cw.shape, F32)],
        scratch_shapes=[pltpu.VMEM((tm + HALO, e), F32), pltpu.VMEM((tm + HALO, e), F32),
                        pltpu.VMEM((7, CONV_RC + 8, CONV_CC), F32), pltpu.VMEM((7, tm + 8, DCW_CC), F32),
                        pltpu.VMEM((CONV_K, 8, e), F32), pltpu.VMEM((CONV_K, 8, e), F32)],
        compiler_params=_params(1, 56),
    )(dy1, dy1, proj, proj, proj, proj, dz, cw)


def _outproj_sgu_bwd(dx1, proj, w_out, lg, lb, sw, swt, sbt):
    t, d = dx1.shape
    e = w_out.shape[0]
    eg = e // GROUPS
    tm = TM_MIX
    nsteps = t // tm

    def body(dx1_ref, a_ref, b_ref, z_ref, wo_ref, lg_ref, lb_ref, sw_ref, swt_ref, sbt_ref,
             dproj_ref, dwo_ref, dsw_ref, dsb_ref, dlg_ref, dlb_ref, mixed_s, dv_s, sb_acc, wo_acc):
        i = pl.program_id(0)

        @pl.when(i == 0)
        def _():
            wo_acc[...] = jnp.zeros_like(wo_acc)
            dsw_ref[...] = jnp.zeros_like(dsw_ref)
            sb_acc[...] = jnp.zeros_like(sb_acc)
            dlg_ref[...] = jnp.zeros_like(dlg_ref)
            dlb_ref[...] = jnp.zeros_like(dlb_ref)
        a = a_ref[...]
        b = b_ref[...]
        lg = lg_ref[...]
        ea, eb, u, xhat, rstd, v = _sgu_parts(a, b, lg, lb_ref[...], sw_ref, sbt_ref, mixed_s, tm, e)
        mixed = mixed_s[...]
        y = u * mixed
        z = z_ref[...]
        s = _sigmoid(z)
        sz = z * s
        dx1 = dx1_ref[...].astype(MXU_DTYPE)
        wo_acc[...] += _dot_tn((y * sz).astype(MXU_DTYPE), dx1)
        dq = _dot_nt(dx1, wo_ref[...])
        dproj_ref[:, 2 * e:3 * e] = (dq * y * _silu_grad(z, s)).astype(dproj_ref.dtype)
        dy = dq * sz
        du = dy * mixed
        dmixed = (dy * u).astype(MXU_DTYPE)
        mask = _tril_mask()
        mask_t = (lax.broadcasted_iota(jnp.int32, (CHUNK, CHUNK), 0)
                  <= lax.broadcasted_iota(jnp.int32, (CHUNK, CHUNK), 1))
        ones = jnp.ones((8, eg), MXU_DTYPE)
        for g in range(GROUPS):
            wtt = jnp.where(mask_t, swt_ref[g], 0.0).astype(MXU_DTYPE)
            cols = slice(g * eg, (g + 1) * eg)
            for ch in range(tm // CHUNK):
                rows = slice(ch * CHUNK, (ch + 1) * CHUNK)
                dm = dmixed[rows, cols]
                dv_s[rows, cols] = jnp.dot(wtt, dm, preferred_element_type=F32)
                dsw_ref[g] += _dot_nt(dm, v[rows, cols])
                sb_acc[g] += _dot_nt(ones, dm)
        dv = dv_s[...]
        dlg_ref[...] += jnp.sum(dv * xhat, axis=0, keepdims=True)
        dlb_ref[...] += jnp.sum(dv, axis=0, keepdims=True)
        dv0 = _ln_bwd(dv * lg, xhat, rstd)
        pdf_a = jnp.exp(-0.5 * a * a) * INV_SQRT_2PI
        pdf_b = jnp.exp(-0.5 * b * b) * INV_SQRT_2PI
        dproj_ref[:, 0:e] = (du * (0.5 * (1.0 + ea) + a * pdf_a)).astype(dproj_ref.dtype)
        dproj_ref[:, e:2 * e] = (dv0 * (0.5 * (1.0 + eb) + b * pdf_b)).astype(dproj_ref.dtype)

        @pl.when(i == nsteps - 1)
        def _():
            dwo_ref[...] = wo_acc[...].astype(dwo_ref.dtype)
            for g in range(GROUPS):
                dsw_ref[g] = jnp.where(mask, dsw_ref[g], 0.0)
                dsb_ref[g:g + 1, :] = sb_acc[g, 0:1, :]

    tile = lambda col: pl.BlockSpec((tm, e), lambda i: (i, col))
    return pl.pallas_call(
        body, name="outproj_sgu_bwd", grid=(nsteps,),
        in_specs=[pl.BlockSpec((tm, d), lambda i: (i, 0)), tile(0), tile(1), tile(2), _whole((e, d)), _whole((1, e)),
                  _whole((1, e)), _whole(sw.shape), _whole(swt.shape), _whole(sbt.shape)],
        out_specs=[pl.BlockSpec((tm, 3 * e), lambda i: (i, 0)), _acc_out((e, d)), _acc_out(sw.shape),
                   _acc_out((GROUPS, CHUNK)), _acc_out((1, e)), _acc_out((1, e))],
        out_shape=[jax.ShapeDtypeStruct((t, 3 * e), MXU_DTYPE), jax.ShapeDtypeStruct((e, d), WIRE_DTYPE),
                   jax.ShapeDtypeStruct(sw.shape, F32), jax.ShapeDtypeStruct((GROUPS, CHUNK), F32),
                   jax.ShapeDtypeStruct((1, e), F32), jax.ShapeDtypeStruct((1, e), F32)],
        scratch_shapes=[pltpu.VMEM((tm, e), F32), pltpu.VMEM((tm, e), F32), pltpu.VMEM((GROUPS, 8, CHUNK), F32),
                        pltpu.VMEM((e, d), F32)],
        compiler_params=_params(1, 60),
    )(dx1, proj, proj, proj, w_out, lg, lb, sw, swt, sbt)


def _inproj_bwd_x(dproj, w_blk, dx1, x, g):
    t, d = x.shape
    nb, _, bn = w_blk.shape
    tm = TM_IN

    def body(dp_ref, w_ref, dx1_ref, x_ref, g_ref, dx_ref, dg_ref):
        @pl.when(pl.program_id(0) == 0)
        def _():
            dg_ref[...] = jnp.zeros_like(dg_ref)
        dh = None
        for j in range(nb):
            term = _dot_nt(dp_ref[:, j * bn:(j + 1) * bn], w_ref[j])
            dh = term if dh is None else dh + term
        xv = x_ref[...]
        dx, dg = _rms_bwd(dh, xv, _rms_rstd(xv), g_ref[...])
        dx_ref[...] = dx1_ref[...] + dx
        dg_ref[...] += dg

    row = lambda w: pl.BlockSpec((tm, w), lambda i: (i, 0))
    return pl.pallas_call(
        body, name="inproj_bwd_x", grid=(t // tm,),
        in_specs=[row(nb * bn), _whole(w_blk.shape), row(d), row(d), _whole((1, d))],
        out_specs=[row(d), _acc_out((1, d))],
        out_shape=[jax.ShapeDtypeStruct((t, d), F32), jax.ShapeDtypeStruct((1, d), F32)],
        compiler_params=_params(1, 56),
    )(dproj, w_blk, dx1, x, g)


def _inproj_bwd_w(h, dproj):
    t, d = h.shape
    bn = dproj.shape[1] // N_DEV
    tm = TM_IN
    nsteps = t // tm

    nh = 2
    per = N_DEV // nh

    def body(h_ref, dp_ref, dw_ref, acc):
        i = pl.program_id(1)

        @pl.when(i == 0)
        def _():
            acc[...] = jnp.zeros_like(acc)
        hv = h_ref[...]
        for jj in range(per):
            acc[jj] += _dot_tn(hv, dp_ref[:, jj * bn:(jj + 1) * bn])

        @pl.when(i == nsteps - 1)
        def _():
            dw_ref[...] = acc[...].astype(dw_ref.dtype)

    return pl.pallas_call(
        body, name="inproj_bwd_w", grid=(nh, nsteps),
        in_specs=[pl.BlockSpec((tm, d), lambda hh, i: (i, 0)), pl.BlockSpec((tm, per * bn), lambda hh, i: (i, hh))],
        out_specs=[pl.BlockSpec((per, d, bn), lambda hh, i: (hh, 0, 0))],
        out_shape=[jax.ShapeDtypeStruct((N_DEV, d, bn), WIRE_DTYPE)],
        scratch_shapes=[pltpu.VMEM((per, d, bn), F32)],
        compiler_params=_params(2, 56),
    )(h, dproj)[0]


def _adamw(parts, w, m, v):
    nl, r, c = w.shape
    tr = r
    for cand in (512, 256, 128, 64, 32, 16, 8):
        if r % cand == 0 and cand * c * 4 <= (1 << 19):
            tr = cand
            break
    bc1 = 1.0 - ADAM_B1 ** ADAM_STEP
    bc2 = 1.0 - ADAM_B2 ** ADAM_STEP

    def body(*refs):
        p_refs = refs[:nl]
        w_ref, m_ref, v_ref, g_ref, d_ref, nm_ref, nv_ref = refs[nl:]

        def update(p_ref):
            g = p_ref[0].astype(F32)
            for s in range(1, N_DEV):
                g = g + p_ref[s].astype(F32)
            nm = ADAM_B1 * m_ref[0] + (1.0 - ADAM_B1) * g
            nv = ADAM_B2 * v_ref[0] + (1.0 - ADAM_B2) * (g * g)
            g_ref[0] = g
            nm_ref[0] = nm
            nv_ref[0] = nv
            d_ref[0] = -ADAM_LR * ((nm / bc1) / (jnp.sqrt(nv / bc2) + ADAM_EPS) + ADAM_WD * w_ref[0])

        if nl == 1:
            update(p_refs[0])
        else:
            for kk in range(nl):
                pl.when(pl.program_id(0) == kk)(lambda kk=kk: update(p_refs[kk]))

    part_spec = lambda kk: pl.BlockSpec((N_DEV, tr, c), lambda l, i: (0, jnp.where(l == kk, i, 0), 0))
    row = pl.BlockSpec((1, tr, c), lambda l, i: (l, i, 0))
    return pl.pallas_call(
        body, name="adamw", grid=(nl, r // tr),
        in_specs=[part_spec(kk) for kk in range(nl)] + [row, row, row],
        out_specs=[row] * 4,
        out_shape=[jax.ShapeDtypeStruct((nl, r, c), F32)] * 4,
        compiler_params=_params(2, 48),
    )(*parts, w, m, v)


def _adamw_small(parts, ws, ms, vs, loss_parts):
    n = len(ws)
    bc1 = 1.0 - ADAM_B1 ** ADAM_STEP
    bc2 = 1.0 - ADAM_B2 ** ADAM_STEP

    def total(ref):
        acc = ref[0]
        for s in range(1, N_DEV):
            acc = acc + ref[s]
        return acc

    def body(*refs):
        p_refs, w_refs, m_refs, v_refs = refs[:n], refs[n:2 * n], refs[2 * n:3 * n], refs[3 * n:4 * n]
        outs = refs[4 * n + 1:]
        for i in range(n):
            g = total(p_refs[i])
            nm = ADAM_B1 * m_refs[i][...] + (1.0 - ADAM_B1) * g
            nv = ADAM_B2 * v_refs[i][...] + (1.0 - ADAM_B2) * (g * g)
            outs[i][...] = g
            outs[n + i][...] = -ADAM_LR * ((nm / bc1) / (jnp.sqrt(nv / bc2) + ADAM_EPS) + ADAM_WD * w_refs[i][...])
            outs[2 * n + i][...] = nm
            outs[3 * n + i][...] = nv
        outs[4 * n][...] = total(refs[4 * n])

    vmem = pl.BlockSpec(memory_space=pltpu.VMEM)
    shapes = [jax.ShapeDtypeStruct(w.shape, F32) for w in ws]
    outs = pl.pallas_call(
        body, name="adamw_small",
        in_specs=[vmem] * (4 * n + 1), out_specs=[vmem] * (4 * n + 1),
        out_shape=shapes * 4 + [jax.ShapeDtypeStruct((8, 128), F32)],
        compiler_params=pltpu.CompilerParams(vmem_limit_bytes=48 << 20),
    )(*parts, *ws, *ms, *vs, loss_parts)
    return outs[:n], outs[n:2 * n], outs[2 * n:3 * n], outs[3 * n:4 * n], outs[4 * n]


def _pack(arrays):
    flat = jnp.concatenate([a.reshape(-1).astype(F32) for a in arrays])
    unit = 8 * PACK_COLS
    padded = -(-flat.shape[0] // unit) * unit
    return jnp.pad(flat, (0, padded - flat.shape[0])).reshape(-1, PACK_COLS)


def _unshard_last(packed, shard_shapes):
    flat = packed.reshape(N_DEV, -1)
    out, off = [], 0
    for s in shard_shapes:
        n = math.prod(s)
        a = jnp.moveaxis(flat[:, off:off + n].reshape((N_DEV,) + tuple(s)), 0, -2)
        out.append(a.reshape(tuple(s[:-1]) + (N_DEV * s[-1],)))
        off += n
    return out


def kernel(x, p, norm_g, w_in, w_out, conv_w, conv_b, conv_ln_g, conv_ln_b, sgu_ln_g, sgu_ln_b, sgu_w, sgu_b, pl_norm_g, pl_gate_w, pl_proj_w, final_g, loss_target, m_norm_g, m_w_in, m_w_out, m_conv_w, m_conv_b, m_conv_ln_g, m_conv_ln_b, m_sgu_ln_g, m_sgu_ln_b, m_sgu_w, m_sgu_b, m_pl_norm_g, m_pl_gate_w, m_pl_proj_w, m_final_g, v_norm_g, v_w_in, v_w_out, v_conv_w, v_conv_b, v_conv_ln_g, v_conv_ln_b, v_sgu_ln_g, v_sgu_ln_b, v_sgu_w, v_sgu_b, v_pl_norm_g, v_pl_gate_w, v_pl_proj_w, v_final_g):
    bsz, seq, d = x.shape
    t = bsz * seq
    depth = w_in.shape[0]
    e = w_out.shape[1] * N_DEV
    pd = p.shape[-1]
    n_conv, n_sgu = conv_w.shape[0], sgu_ln_g.shape[0]

    small_shapes = [conv_w.shape, sgu_ln_g.shape, sgu_ln_b.shape]
    cast = lambda a: a.astype(MXU_DTYPE)
    first = [cast(w_in[0]), cast(w_out[0]), cast(pl_gate_w[0]), cast(pl_proj_w), _pack([conv_w, sgu_ln_g, sgu_ln_b])]
    later = [[cast(w_in[l]), cast(w_out[l]), cast(pl_gate_w[l])] for l in range(1, depth)]
    gathered = _all_gather(first, "gather_weights")
    gather_pending, gather_tokens = {}, 0.0
    for l in range(1, depth):
        lands = _place_own(later[l - 1], False, "place_own_weights")
        send, recv, srcs, lnds, token = _exchange_start(later[l - 1], lands, gathered[0], False, f"gather_start_{l}")
        gather_pending[l] = (send, recv, srcs, lnds)
        gather_tokens = gather_tokens + token[0, 0]
    w_in_g = {0: gathered[0]}
    w_out_g = {0: gathered[1].reshape(e, d)}
    gate_g = {0: gathered[2].reshape(d, d)}
    proj_g = jnp.transpose(gathered[3], (1, 2, 0, 3)).reshape(depth, pd, d)
    conv_w_g, sgu_ln_g_g, sgu_ln_b_g = _unshard_last(gathered[4], small_shapes)
    sgu_wt = jnp.swapaxes(sgu_w, -1, -2)
    sgu_bt = jnp.swapaxes(sgu_b, -1, -2)

    xs = [x.reshape(t, d)]
    p_all = p.reshape(depth, t, pd)
    saved = []
    for l in range(depth):
        j = l // 2
        if l == 0:
            g_l = norm_g[0:1] + gather_tokens
        else:
            g_l = norm_g[l:l + 1]
            got = _exchange_wait(*gather_pending.pop(l), xs[-1], False, f"gather_wait_{l}")
            w_in_g[l], w_out_g[l], gate_g[l] = got[0], got[1].reshape(e, d), got[2].reshape(d, d)
        common = (xs[-1], p_all, l, g_l, w_in_g[l], w_out_g[l], pl_norm_g[l:l + 1], gate_g[l], proj_g[l], seq)
        if l % 2 == 0:
            h, proj, y1, x1, gate, x2 = _layer_fwd(
                *common, conv=(conv_w_g[j], conv_b[j:j + 1], conv_ln_g[j:j + 1], conv_ln_b[j:j + 1]))
        else:
            y1 = None
            h, proj, x1, gate, x2 = _layer_fwd(
                *common, sgu=(sgu_ln_g_g[j:j + 1], sgu_ln_b_g[j:j + 1], sgu_w[j], sgu_bt[j]))
        saved.append((h, proj, y1, x1, gate))
        xs.append(x2)

    loss_part, dx, d_final_g = _loss_head(xs[-1], final_g.reshape(1, d), loss_target.reshape(t, d))

    d_norm_g, d_pl_norm_g = [None] * depth, [None] * depth
    scatter_pending = {}
    d_conv_w, d_conv_b, d_conv_ln_g, d_conv_ln_b = [None] * n_conv, [None] * n_conv, [None] * n_conv, [None] * n_conv
    d_sgu_ln_g, d_sgu_ln_b, d_sgu_w, d_sgu_b = [None] * n_sgu, [None] * n_sgu, [None] * n_sgu, [None] * n_sgu
    def scatter(parts, name):
        send, recv, srcs, lnds, token = _exchange_start(parts, _place_own(parts, True, "place_own_grads"), parts[0],
                                                        True, name)
        return (send, recv, srcs, lnds), token[0, 0]

    for l in reversed(range(depth)):
        j = l // 2
        h, proj, y1, x1, gate = saved[l]
        dx1, dgate_p, dprojw_p, d_pl_norm_g[l] = _ple_bwd(dx, x1, gate, p_all, l, pl_norm_g[l:l + 1], gate_g[l],
                                                         proj_g[l])
        if l % 2 == 0:
            dy1, dz, dw_out_p, d_conv_ln_g[j], d_conv_ln_b[j], d_conv_b[j] = _outproj_conv_bwd(
                dx1, y1, proj, w_out_g[l], conv_ln_g[j:j + 1], conv_ln_b[j:j + 1])
        else:
            dproj, dw_out_p, d_sgu_w[j], d_sgu_b[j], d_sgu_ln_g[j], d_sgu_ln_b[j] = _outproj_sgu_bwd(
                dx1, proj, w_out_g[l], sgu_ln_g_g[j:j + 1], sgu_ln_b_g[j:j + 1], sgu_w[j], sgu_wt[j], sgu_bt[j])
        early = [dw_out_p.reshape(N_DEV, e // N_DEV, d), dgate_p.reshape(N_DEV, d // N_DEV, d), dprojw_p]
        early_token = 0.0
        if l == 0:
            scatter_pending["0_early"], early_token = scatter(early, "scatter_start_0_early")
            early = []
        if l % 2 == 0:
            dproj, d_conv_w[j] = _conv_bwd(dy1, proj, dz, conv_w_g[j] + early_token, seq)
        scatter_pending[l], token = scatter([_inproj_bwd_w(h, dproj)] + early, f"scatter_start_{l}")
        dx, d_norm_g[l] = _inproj_bwd_x(dproj, w_in_g[l], dx1, xs[l], norm_g[l:l + 1] + token)
    grad_x = dx.reshape(bsz, seq, d)

    def own_eighths(full):
        return jnp.moveaxis(full.reshape(full.shape[:-1] + (N_DEV, full.shape[-1] // N_DEV)), -2, 0)

    small_parts = [own_eighths(jnp.stack(d_conv_w)), own_eighths(jnp.concatenate(d_sgu_ln_g, axis=0)),
                   own_eighths(jnp.concatenate(d_sgu_ln_b, axis=0))]
    rep_parts = [jnp.concatenate(d_norm_g, axis=0), jnp.concatenate(d_conv_b, axis=0),
                 jnp.concatenate(d_conv_ln_g, axis=0), jnp.concatenate(d_conv_ln_b, axis=0), jnp.stack(d_sgu_w),
                 jnp.stack(d_sgu_b), jnp.concatenate(d_pl_norm_g, axis=0), d_final_g,
                 jnp.broadcast_to(loss_part, (8, 128))]
    small_send, small_recv, small_srcs, small_lnds, small_token = _exchange_start(
        small_parts, _place_own(small_parts, True, "place_own_small"), grad_x, True, "scatter_small_start")
    rep_send, rep_recv, rep_srcs, rep_lnds, rep_token = _exchange_start(
        rep_parts, _place_own(rep_parts, False, "place_own_replicated"), grad_x, False, "gather_replicated_start")

    landed = {}
    for key in list(scatter_pending):
        landed[key] = _exchange_wait(*scatter_pending.pop(key), small_token + rep_token, True, f"scatter_wait_{key}")
    dw_in_l = [landed[l][0] for l in range(depth)]
    rest = [landed["0_early"]] + [landed[l][1:] for l in range(1, depth)]

    o_w_in = _adamw(dw_in_l, w_in, m_w_in, v_w_in)
    o_w_out = _adamw([r[0] for r in rest], w_out, m_w_out, v_w_out)
    o_gate = _adamw([r[1] for r in rest], pl_gate_w, m_pl_gate_w, v_pl_gate_w)
    o_projw = _adamw([r[2] for r in rest], pl_proj_w, m_pl_proj_w, v_pl_proj_w)
    r_small = _exchange_wait(small_send, small_recv, small_srcs, small_lnds, o_projw[1], True, "scatter_small_wait")
    r_rep = _exchange_wait(rep_send, rep_recv, rep_srcs, rep_lnds, o_w_in[1], False, "gather_replicated_wait")

    row = lambda a: a.reshape(1, -1)
    o_small = _adamw_small(
        r_rep[:8] + r_small,
        [norm_g, conv_b, conv_ln_g, conv_ln_b, sgu_w, sgu_b, pl_norm_g, row(final_g), conv_w, sgu_ln_g, sgu_ln_b],
        [m_norm_g, m_conv_b, m_conv_ln_g, m_conv_ln_b, m_sgu_w, m_sgu_b, m_pl_norm_g, row(m_final_g), m_conv_w,
         m_sgu_ln_g, m_sgu_ln_b],
        [v_norm_g, v_conv_b, v_conv_ln_g, v_conv_ln_b, v_sgu_w, v_sgu_b, v_pl_norm_g, row(v_final_g), v_conv_w,
         v_sgu_ln_g, v_sgu_ln_b],
        r_rep[8])
    loss = o_small[4][0, 0]

    def leaf(kind):
        sm = o_small[kind]
        return [sm[0], o_w_in[kind], o_w_out[kind], sm[8], sm[1], sm[2], sm[3], sm[9], sm[10], sm[4], sm[5], sm[6],
                o_gate[kind], o_projw[kind], sm[7].reshape(final_g.shape)]

    return (loss, grad_x, *leaf(0), *leaf(1), *leaf(2), *leaf(3))
```

```python
import math

import jax
import jax.numpy as jnp
from jax import lax
from jax.experimental import pallas as pl
from jax.experimental.pallas import tpu as pltpu

F32 = jnp.float32
MXU_DTYPE = jnp.bfloat16
WIRE_DTYPE = jnp.bfloat16

EPS = 1e-6
CONV_K = 31
CHUNK = 128
GROUPS = 8
HALO = 32
N_DEV = 8
DEPTH = 4

ADAM_LR = 0.001
ADAM_B1 = 0.9
ADAM_B2 = 0.999
ADAM_EPS = 1e-08
ADAM_WD = 0.01
ADAM_STEP = 10

TM_IN = 512
TM_MIX = 256
TM_OUT = 512
FUSE_SB = 256
CONV_RC = 64
CONV_CC = 128
DCW_CC = 256
DCW_RC = 64
DCW_CHUNKS = 3
PACK_COLS = 1024

MESH_ID = pl.DeviceIdType.MESH
INV_SQRT2 = 1.0 / math.sqrt(2.0)
INV_SQRT_2PI = 1.0 / math.sqrt(2.0 * math.pi)


def _params(n_grid, vmem_mb):
    return pltpu.CompilerParams(dimension_semantics=("arbitrary",) * n_grid, vmem_limit_bytes=vmem_mb << 20)


def _whole(shape):
    nd = len(shape)
    return pl.BlockSpec(shape, lambda *_: (0,) * nd, pipeline_mode=pl.Buffered(1))


def _acc_out(shape):
    nd = len(shape)
    return pl.BlockSpec(shape, lambda *_: (0,) * nd)


def _dot(a, b):
    return jnp.dot(a.astype(MXU_DTYPE), b.astype(MXU_DTYPE), preferred_element_type=F32)


def _dot_nt(a, b):
    return lax.dot_general(a.astype(MXU_DTYPE), b.astype(MXU_DTYPE), (((1,), (1,)), ((), ())),
                           preferred_element_type=F32)


def _dot_tn(a, b):
    return lax.dot_general(a.astype(MXU_DTYPE), b.astype(MXU_DTYPE), (((0,), (0,)), ((), ())),
                           preferred_element_type=F32)


def _sigmoid(x):
    return jax.nn.sigmoid(x)


def _rms_rstd(x):
    return lax.rsqrt(jnp.mean(x * x, axis=-1, keepdims=True) + EPS)


def _rms_bwd(dy, x, rstd, g):
    gy = dy * g
    xr = x * rstd
    dx = rstd * (gy - xr * jnp.mean(gy * xr, axis=-1, keepdims=True))
    dg = jnp.sum(dy * xr, axis=0, keepdims=True)
    return dx, dg


def _ln_stats(x):
    mu = jnp.mean(x, axis=-1, keepdims=True)
    xc = x - mu
    var = jnp.mean(xc * xc, axis=-1, keepdims=True)
    rstd = lax.rsqrt(var + EPS)
    return xc * rstd, rstd


def _ln_bwd(dxhat, xhat, rstd):
    return rstd * (dxhat - jnp.mean(dxhat, axis=-1, keepdims=True)
                   - xhat * jnp.mean(dxhat * xhat, axis=-1, keepdims=True))


def _silu_grad(x, s):
    return s * (1.0 + x * (1.0 - s))


def _tril_mask():
    r = lax.broadcasted_iota(jnp.int32, (CHUNK, CHUNK), 0)
    c = lax.broadcasted_iota(jnp.int32, (CHUNK, CHUNK), 1)
    return r >= c


def _conv_weights_to_sublanes(w_ref, w8_ref):
    for k in range(CONV_K):
        w8_ref[k] = jnp.broadcast_to(w_ref[k:k + 1, :], w8_ref.shape[1:])


def _conv_apply(src_ref, w8_ref, zs_ref, base, tm, e, flip, emit):
    def row_block(i, carry):
        r0 = pl.multiple_of(i * CONV_RC, CONV_RC)
        for c0 in range(0, e, CONV_CC):
            cols = slice(c0, c0 + CONV_CC)
            acc = None
            for s in range(8):
                nrows = CONV_RC if s == 0 else CONV_RC + 8
                taps = [k for k in range(CONV_K) if (base + k) % 8 == s]
                off0 = base + taps[0] - s
                span = nrows + 8 * (len(taps) - 1)
                window = src_ref[pl.ds(r0 + off0, span), cols].reshape(span // 8, 8, CONV_CC)
                z = None
                for m, k in enumerate(taps):
                    wk = (CONV_K - 1 - k) if flip else k
                    term = w8_ref[wk, :, cols][None] * window[m:m + nrows // 8]
                    z = term if z is None else z + term
                z = z.reshape(nrows, CONV_CC)
                if s == 0:
                    acc = z
                else:
                    zs_ref[s - 1, pl.ds(0, nrows), :] = z
                    acc = acc + zs_ref[s - 1, pl.ds(s, CONV_RC), :]
            emit(r0, c0, acc)
        return carry

    lax.fori_loop(0, tm // CONV_RC, row_block, 0)


def _mesh_pos():
    return lax.axis_index("x"), lax.axis_index("y"), lax.axis_index("c")


def _slot(px, py, pc):
    return 4 * px + 2 * py + pc


def _peers(x, y, c):
    return [((1 - x) if (k & 4) else x, (1 - y) if (k & 2) else y, (1 - c) if (k & 1) else c)
            for k in range(1, N_DEV)]


HBM_SPEC = pl.BlockSpec(memory_space=pltpu.HBM)
SEM_SPEC = pl.BlockSpec(memory_space=pltpu.SEMAPHORE)
SIDE_EFFECT = pltpu.SideEffectType.DATAFLOW_SIDE_EFFECTING


def _exchange_copy(src_refs, land_refs, send_sems, recv_sems, i, k, peer, scatter, me):
    slot = _slot(*peer)
    return pltpu.make_async_remote_copy(
        src_ref=src_refs[i].at[slot] if scatter else src_refs[i],
        dst_ref=land_refs[i].at[me if me is not None else slot],
        send_sem=send_sems.at[i * 7 + k], recv_sem=recv_sems.at[i * 7 + k],
        device_id=peer, device_id_type=MESH_ID)


def _exchange_start(srcs, lands, after, scatter, name):
    n = len(srcs)

    def body(*refs):
        src_refs, land_refs = refs[:n], refs[n:2 * n]
        send_sems, recv_sems, token = refs[2 * n + 1], refs[2 * n + 2], refs[-1]
        x, y, c = _mesh_pos()
        me = _slot(x, y, c)
        for i in range(n):
            for k, peer in enumerate(_peers(x, y, c)):
                _exchange_copy(src_refs, land_refs, send_sems, recv_sems, i, k, peer, scatter, me).start()
        token[...] = jnp.zeros_like(token)

    arrays = list(srcs) + list(lands)
    outs = pl.pallas_call(
        body, name=name,
        out_shape=(pltpu.SemaphoreType.DMA((7 * n,)), pltpu.SemaphoreType.DMA((7 * n,)),
                   *[pltpu.HBM(a.shape, a.dtype) for a in lands], jax.ShapeDtypeStruct((8, 128), F32)),
        in_specs=[HBM_SPEC] * (2 * n) + [pl.BlockSpec(memory_space=pl.ANY)],
        out_specs=(SEM_SPEC, SEM_SPEC, *[HBM_SPEC] * n, pl.BlockSpec(memory_space=pltpu.VMEM)),
        input_output_aliases={n + i: 2 + i for i in range(n)},
        compiler_params=pltpu.CompilerParams(has_side_effects=SIDE_EFFECT),
    )(*[pltpu.with_memory_space_constraint(a, pltpu.HBM) for a in arrays], after)
    return outs[0], outs[1], list(srcs), list(outs[2:2 + n]), outs[-1]


def _exchange_wait(send_sems, recv_sems, srcs, lands, after, scatter, name):
    n = len(srcs)

    def body(*refs):
        src_refs, land_refs = refs[:n], refs[n:2 * n]
        send, recv = refs[2 * n], refs[2 * n + 1]
        x, y, c = _mesh_pos()
        for i in range(n):
            for k, peer in enumerate(_peers(x, y, c)):
                cp = _exchange_copy(src_refs, land_refs, send, recv, i, k, peer, scatter, None)
                cp.wait_send()
                cp.wait_recv()

    arrays = list(srcs) + list(lands)
    outs = pl.pallas_call(
        body, name=name,
        out_shape=tuple(pltpu.HBM(a.shape, a.dtype) for a in lands),
        in_specs=[HBM_SPEC] * (2 * n) + [SEM_SPEC, SEM_SPEC, pl.BlockSpec(memory_space=pl.ANY)],
        out_specs=tuple([HBM_SPEC] * n),
        input_output_aliases={n + i: i for i in range(n)},
        compiler_params=pltpu.CompilerParams(has_side_effects=SIDE_EFFECT),
    )(*arrays, send_sems, recv_sems, after)
    return list(outs)


def _place_own(parts, scatter, name):
    n = len(parts)
    me = jnp.reshape(_slot(*_mesh_pos()), (1,)).astype(jnp.int32)

    def body(me_ref, *refs):
        for i in range(n):
            refs[n + i][0] = refs[i][0] if scatter else refs[i][...]

    def slot_spec(shape):
        rest = len(shape)
        return pl.BlockSpec((1,) + tuple(shape), lambda i, me_ref: (me_ref[0],) + (0,) * rest)

    def whole_spec(shape):
        nd = len(shape)
        return pl.BlockSpec(tuple(shape), lambda i, me_ref: (0,) * nd)

    blocks = [a.shape[1:] if scatter else a.shape for a in parts]
    return pl.pallas_call(
        body, name=name,
        grid_spec=pltpu.PrefetchScalarGridSpec(
            num_scalar_prefetch=1, grid=(1,),
            in_specs=[slot_spec(b) if scatter else whole_spec(b) for b in blocks],
            out_specs=[slot_spec(b) for b in blocks]),
        out_shape=[jax.ShapeDtypeStruct((N_DEV,) + tuple(b), a.dtype) for a, b in zip(parts, blocks)],
        compiler_params=_params(1, 32),
    )(me, *parts)


def _all_gather(items, name):
    n = len(items)

    def body(*refs):
        in_refs, out_refs = refs[:n], refs[n:2 * n]
        send_sems, recv_sems, local_sems = refs[2 * n:]
        x, y, c = _mesh_pos()
        me, sibling = (x, y, c), (x, y, 1 - c)
        chips = [(1 - x, y), (x, 1 - y), (1 - x, 1 - y)]

        def copy(i, k, block, to, src=None):
            dst = out_refs[i].at[_slot(*block)]
            return pltpu.make_async_remote_copy(
                src_ref=dst if src is None else src, dst_ref=dst,
                send_sem=send_sems.at[i * 7 + k], recv_sem=recv_sems.at[i * 7 + k],
                device_id=to, device_id_type=MESH_ID)

        mine = [pltpu.make_async_copy(in_refs[i], out_refs[i].at[_slot(*me)], local_sems.at[i]) for i in range(n)]
        for cp in mine:
            cp.start()
        first = []
        for i in range(n):
            first.append(copy(i, 0, me, sibling, src=in_refs[i]))
            for j, chip in enumerate(chips):
                first.append(copy(i, 1 + j, me, (*chip, c), src=in_refs[i]))
        for cp in first:
            cp.start()
        passed = []
        for j, chip in enumerate(chips):
            for i in range(n):
                copy(i, 1 + j, (*chip, c), me).wait_recv()
                fwd = copy(i, 4 + j, (*chip, c), sibling)
                fwd.start()
                passed.append(fwd)
        for i in range(n):
            copy(i, 0, sibling, me).wait_recv()
            for j, chip in enumerate(chips):
                copy(i, 4 + j, (*chip, 1 - c), me).wait_recv()
        for cp in first + passed:
            cp.wait_send()
        for cp in mine:
            cp.wait()

    any_spec = pl.BlockSpec(memory_space=pl.ANY)
    return pl.pallas_call(
        body, name=name,
        out_shape=[jax.ShapeDtypeStruct((N_DEV,) + a.shape, a.dtype) for a in items],
        in_specs=[any_spec] * n, out_specs=[any_spec] * n,
        scratch_shapes=[pltpu.SemaphoreType.DMA((7 * n,)), pltpu.SemaphoreType.DMA((7 * n,)),
                        pltpu.SemaphoreType.DMA((n,))],
    )(*items)


def _layer_fwd(x, p_all, layer, g, w_blk, w_out, plg, gate_w, proj_w, seq, conv=None, sgu=None):
    t, d = x.shape
    nb, _, bn = w_blk.shape
    e = w_out.shape[0]
    pd = p_all.shape[-1]
    tm = TM_MIX
    nt = seq // tm
    is_conv = conv is not None
    mixer_args = conv if is_conv else sgu
    n_mix = len(mixer_args)

    def body(*refs):
        x_ref, p_ref, g_ref, w_ref, wo_ref, plg_ref, gw_ref, pw_ref = refs[:8]
        mix = refs[8:8 + n_mix]
        outs = refs[8 + n_mix:]
        if is_conv:
            cw_ref, cb_ref, lg_ref, lb_ref = mix
            h_ref, proj_ref, y1_ref, x1_ref, gate_ref, x2_ref, y0s, zs, w8 = outs

            @pl.when(lax.rem(pl.program_id(0), nt) == 0)
            def _():
                y0s[pl.ds(0, HALO), :] = jnp.zeros((HALO, e), F32)
            _conv_weights_to_sublanes(cw_ref, w8)
        else:
            lg_ref, lb_ref, sw_ref, sbt_ref = mix
            h_ref, proj_ref, x1_ref, gate_ref, x2_ref, mixed_s = outs

        for sb in range(tm // FUSE_SB):
            rows = pl.ds(sb * FUSE_SB, FUSE_SB)
            xv = x_ref[rows, :]
            hv = (xv * _rms_rstd(xv) * g_ref[...]).astype(MXU_DTYPE)
            h_ref[rows, :] = hv
            for j in range(nb):
                proj_ref[rows, j * bn:(j + 1) * bn] = jnp.dot(hv, w_ref[j], preferred_element_type=F32)
            if is_conv:
                y0s[pl.ds(HALO + sb * FUSE_SB, FUSE_SB), :] = proj_ref[rows, 0:e] * _sigmoid(proj_ref[rows, e:2 * e])

        if is_conv:
            def emit(r0, c0, acc):
                y1_ref[pl.ds(r0, CONV_RC), c0:c0 + CONV_CC] = acc + cb_ref[:, c0:c0 + CONV_CC]
            _conv_apply(y0s, w8, zs, HALO - (CONV_K - 1), tm, e, False, emit)

        for sb in range(tm // FUSE_SB):
            rows = pl.ds(sb * FUSE_SB, FUSE_SB)
            if is_conv:
                xhat, _ = _ln_stats(y1_ref[rows, :])
                y2 = xhat * lg_ref[...] + lb_ref[...]
                y = y2 * _sigmoid(y2)
            else:
                _, _, u, _, _, _ = _sgu_parts(proj_ref[rows, 0:e], proj_ref[rows, e:2 * e], lg_ref[...], lb_ref[...],
                                              sw_ref, sbt_ref, mixed_s, FUSE_SB, e)
                y = u * mixed_s[...]
            z = proj_ref[rows, 2 * e:3 * e]
            q = (y * (z * _sigmoid(z))).astype(MXU_DTYPE)
            x1 = x_ref[rows, :] + jnp.dot(q, wo_ref[...], preferred_element_type=F32)
            x1_ref[rows, :] = x1
            rn = x1 * _rms_rstd(x1) * plg_ref[...]
            gate = _sigmoid(_dot(rn, gw_ref[...]))
            gate_ref[rows, :] = gate
            x2_ref[rows, :] = x1 + gate * _dot(p_ref[0, rows, :], pw_ref[...])

        if is_conv:
            y0s[pl.ds(0, HALO), :] = y0s[pl.ds(tm, HALO), :]

    row = lambda w: pl.BlockSpec((tm, w), lambda i: (i, 0))
    f32 = lambda w: jax.ShapeDtypeStruct((t, w), F32)
    out_shape = ([jax.ShapeDtypeStruct((t, d), MXU_DTYPE), f32(3 * e)] + ([f32(e)] if is_conv else [])
                 + [f32(d), f32(d), f32(d)])
    out_specs = [row(d), row(3 * e)] + ([row(e)] if is_conv else []) + [row(d), row(d), row(d)]
    scratch = ([pltpu.VMEM((tm + HALO, e), F32), pltpu.VMEM((7, CONV_RC + 8, CONV_CC), F32),
                pltpu.VMEM((CONV_K, 8, e), F32)] if is_conv else [pltpu.VMEM((FUSE_SB, e), F32)])
    return pl.pallas_call(
        body, name="layer_fwd_conv" if is_conv else "layer_fwd_sgu", grid=(t // tm,),
        in_specs=[row(d), pl.BlockSpec((1, tm, pd), lambda i: (layer, i, 0)), _whole((1, d)), _whole(w_blk.shape),
                  _whole((e, d)), _whole((1, d)),
                  _whole((d, d)), _whole((pd, d))] + [_whole(a.shape) for a in mixer_args],
        out_specs=out_specs, out_shape=out_shape, scratch_shapes=scratch,
        compiler_params=_params(1, 60),
    )(x, p_all, g, w_blk, w_out, plg, gate_w, proj_w, *mixer_args)


def _sgu_parts(a, b, lg, lb, sw_ref, sbt_ref, mixed_s, tm, e):
    eg = e // GROUPS
    ea = lax.erf(a * INV_SQRT2)
    eb = lax.erf(b * INV_SQRT2)
    u = 0.5 * a * (1.0 + ea)
    v0 = 0.5 * b * (1.0 + eb)
    xhat, rstd = _ln_stats(v0)
    v = (xhat * lg + lb).astype(MXU_DTYPE)
    mask = _tril_mask()
    for g in range(GROUPS):
        wt = jnp.where(mask, sw_ref[g], 0.0).astype(MXU_DTYPE)
        bcol = sbt_ref[:, g:g + 1]
        for ch in range(tm // CHUNK):
            rows = slice(ch * CHUNK, (ch + 1) * CHUNK)
            cols = slice(g * eg, (g + 1) * eg)
            mixed_s[rows, cols] = jnp.dot(wt, v[rows, cols], preferred_element_type=F32) + bcol
    return ea, eb, u, xhat, rstd, v


def _loss_head(xf, fg, tgt):
    t, d = xf.shape
    tm = TM_OUT
    nsteps = t // tm

    def body(x_ref, g_ref, t_ref, loss_ref, dx_ref, dg_ref, sq_s):
        i = pl.program_id(0)

        @pl.when(i == 0)
        def _():
            sq_s[...] = jnp.zeros_like(sq_s)
            dg_ref[...] = jnp.zeros_like(dg_ref)
        x = x_ref[...]
        rstd = _rms_rstd(x)
        err = x * rstd * g_ref[...] - t_ref[...]
        sq_s[...] += jnp.sum(err * err, axis=0, keepdims=True)
        dx, dg = _rms_bwd(err * (1.0 / d), x, rstd, g_ref[...])
        dx_ref[...] = dx
        dg_ref[...] += dg

        @pl.when(i == nsteps - 1)
        def _():
            loss_ref[...] = jnp.sum(sq_s[...], axis=1, keepdims=True) * (0.5 / d)

    row = pl.BlockSpec((tm, d), lambda i: (i, 0))
    return pl.pallas_call(
        body, name="loss_head", grid=(nsteps,),
        in_specs=[row, _whole((1, d)), row],
        out_specs=[_acc_out((1, 1)), row, _acc_out((1, d))],
        out_shape=[jax.ShapeDtypeStruct((1, 1), F32), jax.ShapeDtypeStruct((t, d), F32),
                   jax.ShapeDtypeStruct((1, d), F32)],
        scratch_shapes=[pltpu.VMEM((1, d), F32)],
        compiler_params=_params(1, 32),
    )(xf, fg, tgt)


def _ple_bwd_tile(dx2, x1, gate, p_b, plg, gw_ref, pw_ref, gw_acc, pw_acc):
    rstd = _rms_rstd(x1)
    rn = (x1 * rstd * plg).astype(MXU_DTYPE)
    pp = jnp.dot(p_b, pw_ref[...], preferred_element_type=F32)
    dpp = (dx2 * gate).astype(MXU_DTYPE)
    dgpre = (dx2 * pp * gate * (1.0 - gate)).astype(MXU_DTYPE)
    pw_acc[...] += _dot_tn(p_b, dpp)
    gw_acc[...] += _dot_tn(rn, dgpre)
    dx, dg = _rms_bwd(_dot_nt(dgpre, gw_ref[...]), x1, rstd, plg)
    return dx2 + dx, dg


def _ple_bwd_finish(dgw_ref, dpw_ref, gw_acc, pw_acc):
    bn = pw_acc.shape[1] // N_DEV
    dgw_ref[...] = gw_acc[...].astype(dgw_ref.dtype)
    for j in range(N_DEV):
        dpw_ref[j] = pw_acc[:, j * bn:(j + 1) * bn].astype(dpw_ref.dtype)


def _outproj_conv_tile(dx1, y1_ref, z_ref, wo_ref, lg, lb, wo_acc, dy1_ref, dz_ref, dlg_ref, dlb_ref, dcb_ref):
    xhat, rstd = _ln_stats(y1_ref[...])
    y2 = xhat * lg + lb
    s2 = _sigmoid(y2)
    y = y2 * s2
    z = z_ref[...]
    s = _sigmoid(z)
    sz = z * s
    dx1 = dx1.astype(MXU_DTYPE)
    wo_acc[...] += _dot_tn((y * sz).astype(MXU_DTYPE), dx1)
    dq = _dot_nt(dx1, wo_ref[...])
    dz_ref[...] = (dq * y * _silu_grad(z, s)).astype(dz_ref.dtype)
    dy2 = dq * sz * _silu_grad(y2, s2)
    dlg_ref[...] += jnp.sum(dy2 * xhat, axis=0, keepdims=True)
    dlb_ref[...] += jnp.sum(dy2, axis=0, keepdims=True)
    dy1 = _ln_bwd(dy2 * lg, xhat, rstd)
    dy1_ref[...] = dy1
    dcb_ref[...] += jnp.sum(dy1, axis=0, keepdims=True)


def _ple_bwd(dx2, x1, gate, p_all, layer, plg, gate_w, proj_w):
    t, d = x1.shape
    pd = p_all.shape[-1]
    tm = TM_OUT
    nsteps = t // tm

    def body(dx2_ref, x1_ref, gate_ref, p_ref, plg_ref, gw_ref, pw_ref, dx1_ref, dgw_ref, dpw_ref, dplg_ref,
             gw_acc, pw_acc):
        i = pl.program_id(0)

        @pl.when(i == 0)
        def _():
            gw_acc[...] = jnp.zeros_like(gw_acc)
            pw_acc[...] = jnp.zeros_like(pw_acc)
            dplg_ref[...] = jnp.zeros_like(dplg_ref)
        dx1, dg = _ple_bwd_tile(dx2_ref[...], x1_ref[...], gate_ref[...], p_ref[0].astype(MXU_DTYPE), plg_ref[...],
                                gw_ref, pw_ref, gw_acc, pw_acc)
        dx1_ref[...] = dx1
        dplg_ref[...] += dg

        @pl.when(i == nsteps - 1)
        def _():
            _ple_bwd_finish(dgw_ref, dpw_ref, gw_acc, pw_acc)

    row = lambda w: pl.BlockSpec((tm, w), lambda i: (i, 0))
    return pl.pallas_call(
        body, name="ple_bwd", grid=(nsteps,),
        in_specs=[row(d), row(d), row(d), pl.BlockSpec((1, tm, pd), lambda i: (layer, i, 0)), _whole((1, d)),
                  _whole((d, d)), _whole((pd, d))],
        out_specs=[row(d), _acc_out((d, d)), _acc_out((N_DEV, pd, d // N_DEV)), _acc_out((1, d))],
        out_shape=[jax.ShapeDtypeStruct((t, d), F32), jax.ShapeDtypeStruct((d, d), WIRE_DTYPE),
                   jax.ShapeDtypeStruct((N_DEV, pd, d // N_DEV), WIRE_DTYPE), jax.ShapeDtypeStruct((1, d), F32)],
        scratch_shapes=[pltpu.VMEM((d, d), F32), pltpu.VMEM((pd, d), F32)],
        compiler_params=_params(1, 48),
    )(dx2, x1, gate, p_all, plg, gate_w, proj_w)


def _ple_outproj_conv_bwd(dx2, x1, gate, p_all, layer, plg, gate_w, proj_w, y1, proj, w_out, lg, lb):
    t, d = x1.shape
    e = y1.shape[1]
    pd = p_all.shape[-1]
    tm = TM_MIX
    nsteps = t // tm

    def body(dx2_ref, x1_ref, gate_ref, p_ref, plg_ref, gw_ref, pw_ref, y1_ref, z_ref, wo_ref, lg_ref, lb_ref,
             dx1_ref, dgw_ref, dpw_ref, dwo_ref, dplg_ref, dy1_ref, dz_ref, dlg_ref, dlb_ref, dcb_ref,
             gw_acc, pw_acc, wo_acc):
        i = pl.program_id(0)

        @pl.when(i == 0)
        def _():
            for ref in (gw_acc, pw_acc, wo_acc, dplg_ref, dlg_ref, dlb_ref, dcb_ref):
                ref[...] = jnp.zeros_like(ref)
        dx1, dg = _ple_bwd_tile(dx2_ref[...], x1_ref[...], gate_ref[...], p_ref[0].astype(MXU_DTYPE), plg_ref[...],
                                gw_ref, pw_ref, gw_acc, pw_acc)
        dx1_ref[...] = dx1
        dplg_ref[...] += dg
        _outproj_conv_tile(dx1, y1_ref, z_ref, wo_ref, lg_ref[...], lb_ref[...], wo_acc, dy1_ref, dz_ref, dlg_ref,
                           dlb_ref, dcb_ref)

        @pl.when(i == nsteps - 1)
        def _():
            _ple_bwd_finish(dgw_ref, dpw_ref, gw_acc, pw_acc)
            dwo_ref[...] = wo_acc[...].astype(dwo_ref.dtype)

    row = lambda w: pl.BlockSpec((tm, w), lambda i: (i, 0))
    vec = lambda w: jax.ShapeDtypeStruct((1, w), F32)
    return pl.pallas_call(
        body, name="ple_outproj_conv_bwd", grid=(nsteps,),
        in_specs=[row(d), row(d), row(d), pl.BlockSpec((1, tm, pd), lambda i: (layer, i, 0)), _whole((1, d)),
                  _whole((d, d)), _whole((pd, d)), row(e), pl.BlockSpec((tm, e), lambda i: (i, 2)), _whole((e, d)),
                  _whole((1, e)), _whole((1, e))],
        out_specs=[row(d), _acc_out((d, d)), _acc_out((N_DEV, pd, d // N_DEV)), _acc_out((e, d)), _acc_out((1, d)),
                   row(e), row(e), _acc_out((1, e)), _acc_out((1, e)), _acc_out((1, e))],
        out_shape=[jax.ShapeDtypeStruct((t, d), F32), jax.ShapeDtypeStruct((d, d), WIRE_DTYPE),
                   jax.ShapeDtypeStruct((N_DEV, pd, d // N_DEV), WIRE_DTYPE), jax.ShapeDtypeStruct((e, d), WIRE_DTYPE),
                   vec(d), jax.ShapeDtypeStruct((t, e), F32), jax.ShapeDtypeStruct((t, e), MXU_DTYPE), vec(e), vec(e),
                   vec(e)],
        scratch_shapes=[pltpu.VMEM((d, d), F32), pltpu.VMEM((pd, d), F32), pltpu.VMEM((e, d), F32)],
        compiler_params=_params(1, 63),
    )(dx2, x1, gate, p_all, plg, gate_w, proj_w, y1, proj, w_out, lg, lb)


def _conv_bwd(dy1, proj, dz, cw, seq):
    t, e = dy1.shape
    tm = TM_MIX
    nt = seq // tm
    hb = tm // HALO
    n_halo_blocks = t // HALO

    def body(d_ref, dn_ref, a_ref, b_ref, ah_ref, bh_ref, dz_ref, cw_ref, dproj_ref, dcw_ref,
             y0s, d1s, zs, dsh, dcw8, w8):
        i = pl.program_id(0)
        pos = lax.rem(i, nt)

        @pl.when(i == 0)
        def _():
            dcw8[...] = jnp.zeros_like(dcw8)
        _conv_weights_to_sublanes(cw_ref, w8)
        a = a_ref[...]
        sb = _sigmoid(b_ref[...])
        y0s[pl.ds(HALO, tm), :] = a * sb
        d1s[pl.ds(0, tm), :] = d_ref[...]

        @pl.when(pos == 0)
        def _():
            y0s[pl.ds(0, HALO), :] = jnp.zeros((HALO, e), F32)

        @pl.when(pos != 0)
        def _():
            y0s[pl.ds(0, HALO), :] = ah_ref[...] * _sigmoid(bh_ref[...])

        @pl.when(pos == nt - 1)
        def _():
            d1s[pl.ds(tm, HALO), :] = jnp.zeros((HALO, e), F32)

        @pl.when(pos != nt - 1)
        def _():
            d1s[pl.ds(tm, HALO), :] = dn_ref[...]

        base = HALO - (CONV_K - 1)
        for c0 in range(0, e, DCW_CC):
            cols = slice(c0, c0 + DCW_CC)
            dcur = d_ref[:, cols]
            for s in range(1, 8):
                dsh[s - 1, pl.ds(0, 8), :] = jnp.zeros((8, DCW_CC), F32)
                dsh[s - 1, pl.ds(tm, 8), :] = jnp.zeros((8, DCW_CC), F32)
                dsh[s - 1, pl.ds(s, tm), :] = dcur
            for s in range(8):
                taps = [k for k in range(CONV_K) if (base + k) % 8 == s]
                off0 = base + taps[0] - s
                n, ch = (tm, DCW_RC) if s == 0 else (tm + 8, (tm + 8) // DCW_CHUNKS)
                sums = [None] * len(taps)
                for r in range(0, n, ch):
                    dch = d_ref[r:r + ch, cols] if s == 0 else dsh[s - 1, r:r + ch, :]
                    window = y0s[pl.ds(off0 + r, ch + 8 * (len(taps) - 1)), cols]
                    for m in range(len(taps)):
                        part = jnp.sum((dch * window[8 * m:8 * m + ch]).reshape(ch // 8, 8, DCW_CC), axis=0)
                        sums[m] = part if sums[m] is None else sums[m] + part
                for m, k in enumerate(taps):
                    dcw8[k, :, cols] += sums[m]

        def emit(r0, c0, dy0):
            rs, cs = pl.ds(r0, CONV_RC), slice(c0, c0 + CONV_CC)
            sbv = _sigmoid(b_ref[rs, cs])
            av = a_ref[rs, cs]
            dproj_ref[rs, c0:c0 + CONV_CC] = (dy0 * sbv).astype(dproj_ref.dtype)
            dproj_ref[rs, e + c0:e + c0 + CONV_CC] = (dy0 * av * sbv * (1.0 - sbv)).astype(dproj_ref.dtype)
        _conv_apply(d1s, w8, zs, 0, tm, e, True, emit)
        dproj_ref[:, 2 * e:3 * e] = dz_ref[...]

        @pl.when(i == t // tm - 1)
        def _():
            dcw_ref[...] = jnp.sum(dcw8[...], axis=1)

    tile = lambda col: pl.BlockSpec((tm, e), lambda i: (i, col))
    prev = lambda col: pl.BlockSpec((HALO, e), lambda i: (jnp.maximum(i * hb - 1, 0), col))
    nxt = pl.BlockSpec((HALO, e), lambda i: (jnp.minimum((i + 1) * hb, n_halo_blocks - 1), 0))
    return pl.pallas_call(
        body, name="conv_bwd", grid=(t // tm,),
        in_specs=[tile(0), nxt, tile(0), tile(1), prev(0), prev(1), tile(0), _whole(cw.shape)],
        out_specs=[pl.BlockSpec((tm, 3 * e), lambda i: (i, 0)), _acc_out(cw.shape)],
        out_shape=[jax.ShapeDtypeStruct((t, 3 * e), MXU_DTYPE), jax.ShapeDtypeStruct(cw.shape, F32)],
        scratch_shapes=[pltpu.VMEM((tm + HALO, e), F32), pltpu.VMEM((tm + HALO, e), F32),
                        pltpu.VMEM((7, CONV_RC + 8, CONV_CC), F32), pltpu.VMEM((7, tm + 8, DCW_CC), F32),
                        pltpu.VMEM((CONV_K, 8, e), F32), pltpu.VMEM((CONV_K, 8, e), F32)],
        compiler_params=_params(1, 56),
    )(dy1, dy1, proj, proj, proj, proj, dz, cw)


def _outproj_sgu_bwd(dx1, proj, w_out, lg, lb, sw, swt, sbt):
    t, d = dx1.shape
    e = w_out.shape[0]
    eg = e // GROUPS
    tm = TM_MIX
    nsteps = t // tm

    def body(dx1_ref, a_ref, b_ref, z_ref, wo_ref, lg_ref, lb_ref, sw_ref, swt_ref, sbt_ref,
             dproj_ref, dwo_ref, dsw_ref, dsb_ref, dlg_ref, dlb_ref, mixed_s, dv_s, sb_acc, wo_acc):
        i = pl.program_id(0)

        @pl.when(i == 0)
        def _():
            wo_acc[...] = jnp.zeros_like(wo_acc)
            dsw_ref[...] = jnp.zeros_like(dsw_ref)
            sb_acc[...] = jnp.zeros_like(sb_acc)
            dlg_ref[...] = jnp.zeros_like(dlg_ref)
            dlb_ref[...] = jnp.zeros_like(dlb_ref)
        a = a_ref[...]
        b = b_ref[...]
        lg = lg_ref[...]
        ea, eb, u, xhat, rstd, v = _sgu_parts(a, b, lg, lb_ref[...], sw_ref, sbt_ref, mixed_s, tm, e)
        mixed = mixed_s[...]
        y = u * mixed
        z = z_ref[...]
        s = _sigmoid(z)
        sz = z * s
        dx1 = dx1_ref[...].astype(MXU_DTYPE)
        wo_acc[...] += _dot_tn((y * sz).astype(MXU_DTYPE), dx1)
        dq = _dot_nt(dx1, wo_ref[...])
        dproj_ref[:, 2 * e:3 * e] = (dq * y * _silu_grad(z, s)).astype(dproj_ref.dtype)
        dy = dq * sz
        du = dy * mixed
        dmixed = (dy * u).astype(MXU_DTYPE)
        mask = _tril_mask()
        mask_t = (lax.broadcasted_iota(jnp.int32, (CHUNK, CHUNK), 0)
                  <= lax.broadcasted_iota(jnp.int32, (CHUNK, CHUNK), 1))
        ones = jnp.ones((8, eg), MXU_DTYPE)
        for g in range(GROUPS):
            wtt = jnp.where(mask_t, swt_ref[g], 0.0).astype(MXU_DTYPE)
            cols = slice(g * eg, (g + 1) * eg)
            for ch in range(tm // CHUNK):
                rows = slice(ch * CHUNK, (ch + 1) * CHUNK)
                dm = dmixed[rows, cols]
                dv_s[rows, cols] = jnp.dot(wtt, dm, preferred_element_type=F32)
                dsw_ref[g] += _dot_nt(dm, v[rows, cols])
                sb_acc[g] += _dot_nt(ones, dm)
        dv = dv_s[...]
        dlg_ref[...] += jnp.sum(dv * xhat, axis=0, keepdims=True)
        dlb_ref[...] += jnp.sum(dv, axis=0, keepdims=True)
        dv0 = _ln_bwd(dv * lg, xhat, rstd)
        pdf_a = jnp.exp(-0.5 * a * a) * INV_SQRT_2PI
        pdf_b = jnp.exp(-0.5 * b * b) * INV_SQRT_2PI
        dproj_ref[:, 0:e] = (du * (0.5 * (1.0 + ea) + a * pdf_a)).astype(dproj_ref.dtype)
        dproj_ref[:, e:2 * e] = (dv0 * (0.5 * (1.0 + eb) + b * pdf_b)).astype(dproj_ref.dtype)

        @pl.when(i == nsteps - 1)
        def _():
            dwo_ref[...] = wo_acc[...].astype(dwo_ref.dtype)
            for g in range(GROUPS):
                dsw_ref[g] = jnp.where(mask, dsw_ref[g], 0.0)
                dsb_ref[g:g + 1, :] = sb_acc[g, 0:1, :]

    tile = lambda col: pl.BlockSpec((tm, e), lambda i: (i, col))
    return pl.pallas_call(
        body, name="outproj_sgu_bwd", grid=(nsteps,),
        in_specs=[pl.BlockSpec((tm, d), lambda i: (i, 0)), tile(0), tile(1), tile(2), _whole((e, d)), _whole((1, e)),
                  _whole((1, e)), _whole(sw.shape), _whole(swt.shape), _whole(sbt.shape)],
        out_specs=[pl.BlockSpec((tm, 3 * e), lambda i: (i, 0)), _acc_out((e, d)), _acc_out(sw.shape),
                   _acc_out((GROUPS, CHUNK)), _acc_out((1, e)), _acc_out((1, e))],
        out_shape=[jax.ShapeDtypeStruct((t, 3 * e), MXU_DTYPE), jax.ShapeDtypeStruct((e, d), WIRE_DTYPE),
                   jax.ShapeDtypeStruct(sw.shape, F32), jax.ShapeDtypeStruct((GROUPS, CHUNK), F32),
                   jax.ShapeDtypeStruct((1, e), F32), jax.ShapeDtypeStruct((1, e), F32)],
        scratch_shapes=[pltpu.VMEM((tm, e), F32), pltpu.VMEM((tm, e), F32), pltpu.VMEM((GROUPS, 8, CHUNK), F32),
                        pltpu.VMEM((e, d), F32)],
        compiler_params=_params(1, 60),
    )(dx1, proj, proj, proj, w_out, lg, lb, sw, swt, sbt)


def _inproj_bwd_x(dproj, w_blk, dx1, x, g):
    t, d = x.shape
    nb, _, bn = w_blk.shape
    tm = TM_IN

    def body(dp_ref, w_ref, dx1_ref, x_ref, g_ref, dx_ref, dg_ref):
        @pl.when(pl.program_id(0) == 0)
        def _():
            dg_ref[...] = jnp.zeros_like(dg_ref)
        dh = None
        for j in range(nb):
            term = _dot_nt(dp_ref[:, j * bn:(j + 1) * bn], w_ref[j])
            dh = term if dh is None else dh + term
        xv = x_ref[...]
        dx, dg = _rms_bwd(dh, xv, _rms_rstd(xv), g_ref[...])
        dx_ref[...] = dx1_ref[...] + dx
        dg_ref[...] += dg

    row = lambda w: pl.BlockSpec((tm, w), lambda i: (i, 0))
    return pl.pallas_call(
        body, name="inproj_bwd_x", grid=(t // tm,),
        in_specs=[row(nb * bn), _whole(w_blk.shape), row(d), row(d), _whole((1, d))],
        out_specs=[row(d), _acc_out((1, d))],
        out_shape=[jax.ShapeDtypeStruct((t, d), F32), jax.ShapeDtypeStruct((1, d), F32)],
        compiler_params=_params(1, 56),
    )(dproj, w_blk, dx1, x, g)


def _inproj_bwd_w(h, dproj):
    t, d = h.shape
    bn = dproj.shape[1] // N_DEV
    tm = TM_IN
    nsteps = t // tm

    nh = 2
    per = N_DEV // nh

    def body(h_ref, dp_ref, dw_ref, acc):
        i = pl.program_id(1)

        @pl.when(i == 0)
        def _():
            acc[...] = jnp.zeros_like(acc)
        hv = h_ref[...]
        for jj in range(per):
            acc[jj] += _dot_tn(hv, dp_ref[:, jj * bn:(jj + 1) * bn])

        @pl.when(i == nsteps - 1)
        def _():
            dw_ref[...] = acc[...].astype(dw_ref.dtype)

    return pl.pallas_call(
        body, name="inproj_bwd_w", grid=(nh, nsteps),
        in_specs=[pl.BlockSpec((tm, d), lambda hh, i: (i, 0)), pl.BlockSpec((tm, per * bn), lambda hh, i: (i, hh))],
        out_specs=[pl.BlockSpec((per, d, bn), lambda hh, i: (hh, 0, 0))],
        out_shape=[jax.ShapeDtypeStruct((N_DEV, d, bn), WIRE_DTYPE)],
        scratch_shapes=[pltpu.VMEM((per, d, bn), F32)],
        compiler_params=_params(2, 56),
    )(h, dproj)[0]


def _adamw(parts, w, m, v):
    nl, r, c = w.shape
    tr = r
    for cand in (512, 256, 128, 64, 32, 16, 8):
        if r % cand == 0 and cand * c * 4 <= (1 << 19):
            tr = cand
            break
    bc1 = 1.0 - ADAM_B1 ** ADAM_STEP
    bc2 = 1.0 - ADAM_B2 ** ADAM_STEP

    def body(*refs):
        p_refs = refs[:nl]
        w_ref, m_ref, v_ref, g_ref, d_ref, nm_ref, nv_ref = refs[nl:]

        def update(p_ref):
            g = p_ref[0].astype(F32)
            for s in range(1, N_DEV):
                g = g + p_ref[s].astype(F32)
            nm = ADAM_B1 * m_ref[0] + (1.0 - ADAM_B1) * g
            nv = ADAM_B2 * v_ref[0] + (1.0 - ADAM_B2) * (g * g)
            g_ref[0] = g
            nm_ref[0] = nm
            nv_ref[0] = nv
            d_ref[0] = -ADAM_LR * ((nm / bc1) / (jnp.sqrt(nv / bc2) + ADAM_EPS) + ADAM_WD * w_ref[0])

        if nl == 1:
            update(p_refs[0])
        else:
            for kk in range(nl):
                pl.when(pl.program_id(0) == kk)(lambda kk=kk: update(p_refs[kk]))

    part_spec = lambda kk: pl.BlockSpec((N_DEV, tr, c), lambda l, i: (0, jnp.where(l == kk, i, 0), 0))
    row = pl.BlockSpec((1, tr, c), lambda l, i: (l, i, 0))
    return pl.pallas_call(
        body, name="adamw", grid=(nl, r // tr),
        in_specs=[part_spec(kk) for kk in range(nl)] + [row, row, row],
        out_specs=[row] * 4,
        out_shape=[jax.ShapeDtypeStruct((nl, r, c), F32)] * 4,
        compiler_params=_params(2, 48),
    )(*parts, w, m, v)


def _adamw_small(parts, ws, ms, vs, loss_parts):
    n = len(ws)
    bc1 = 1.0 - ADAM_B1 ** ADAM_STEP
    bc2 = 1.0 - ADAM_B2 ** ADAM_STEP

    def total(ref):
        acc = ref[0]
        for s in range(1, N_DEV):
            acc = acc + ref[s]
        return acc

    def body(*refs):
        p_refs, w_refs, m_refs, v_refs = refs[:n], refs[n:2 * n], refs[2 * n:3 * n], refs[3 * n:4 * n]
        outs = refs[4 * n + 1:]
        for i in range(n):
            g = total(p_refs[i])
            nm = ADAM_B1 * m_refs[i][...] + (1.0 - ADAM_B1) * g
            nv = ADAM_B2 * v_refs[i][...] + (1.0 - ADAM_B2) * (g * g)
            outs[i][...] = g
            outs[n + i][...] = -ADAM_LR * ((nm / bc1) / (jnp.sqrt(nv / bc2) + ADAM_EPS) + ADAM_WD * w_refs[i][...])
            outs[2 * n + i][...] = nm
            outs[3 * n + i][...] = nv
        outs[4 * n][...] = total(refs[4 * n])

    vmem = pl.BlockSpec(memory_space=pltpu.VMEM)
    shapes = [jax.ShapeDtypeStruct(w.shape, F32) for w in ws]
    outs = pl.pallas_call(
        body, name="adamw_small",
        in_specs=[vmem] * (4 * n + 1), out_specs=[vmem] * (4 * n + 1),
        out_shape=shapes * 4 + [jax.ShapeDtypeStruct((8, 128), F32)],
        compiler_params=pltpu.CompilerParams(vmem_limit_bytes=48 << 20),
    )(*parts, *ws, *ms, *vs, loss_parts)
    return outs[:n], outs[n:2 * n], outs[2 * n:3 * n], outs[3 * n:4 * n], outs[4 * n]


def _pack(arrays):
    flat = jnp.concatenate([a.reshape(-1).astype(F32) for a in arrays])
    unit = 8 * PACK_COLS
    padded = -(-flat.shape[0] // unit) * unit
    return jnp.pad(flat, (0, padded - flat.shape[0])).reshape(-1, PACK_COLS)


def _unshard_last(packed, shard_shapes):
    flat = packed.reshape(N_DEV, -1)
    out, off = [], 0
    for s in shard_shapes:
        n = math.prod(s)
        a = jnp.moveaxis(flat[:, off:off + n].reshape((N_DEV,) + tuple(s)), 0, -2)
        out.append(a.reshape(tuple(s[:-1]) + (N_DEV * s[-1],)))
        off += n
    return out


def kernel(x, p, norm_g, w_in, w_out, conv_w, conv_b, conv_ln_g, conv_ln_b, sgu_ln_g, sgu_ln_b, sgu_w, sgu_b, pl_norm_g, pl_gate_w, pl_proj_w, final_g, loss_target, m_norm_g, m_w_in, m_w_out, m_conv_w, m_conv_b, m_conv_ln_g, m_conv_ln_b, m_sgu_ln_g, m_sgu_ln_b, m_sgu_w, m_sgu_b, m_pl_norm_g, m_pl_gate_w, m_pl_proj_w, m_final_g, v_norm_g, v_w_in, v_w_out, v_conv_w, v_conv_b, v_conv_ln_g, v_conv_ln_b, v_sgu_ln_g, v_sgu_ln_b, v_sgu_w, v_sgu_b, v_pl_norm_g, v_pl_gate_w, v_pl_proj_w, v_final_g):
    bsz, seq, d = x.shape
    t = bsz * seq
    depth = w_in.shape[0]
    e = w_out.shape[1] * N_DEV
    pd = p.shape[-1]
    n_conv, n_sgu = conv_w.shape[0], sgu_ln_g.shape[0]

    small_shapes = [conv_w.shape, sgu_ln_g.shape, sgu_ln_b.shape]
    cast = lambda a: a.astype(MXU_DTYPE)
    first = [cast(w_in[0]), cast(w_out[0]), cast(pl_gate_w[0]), cast(pl_proj_w), _pack([conv_w, sgu_ln_g, sgu_ln_b])]
    later = [[cast(w_in[l]), cast(w_out[l]), cast(pl_gate_w[l])] for l in range(1, depth)]
    gathered = _all_gather(first, "gather_weights")
    gather_pending, gather_tokens = {}, 0.0
    for l in range(1, depth):
        lands = _place_own(later[l - 1], False, "place_own_weights")
        send, recv, srcs, lnds, token = _exchange_start(later[l - 1], lands, gathered[0], False, f"gather_start_{l}")
        gather_pending[l] = (send, recv, srcs, lnds)
        gather_tokens = gather_tokens + token[0, 0]
    w_in_g = {0: gathered[0]}
    w_out_g = {0: gathered[1].reshape(e, d)}
    gate_g = {0: gathered[2].reshape(d, d)}
    proj_g = jnp.transpose(gathered[3], (1, 2, 0, 3)).reshape(depth, pd, d)
    conv_w_g, sgu_ln_g_g, sgu_ln_b_g = _unshard_last(gathered[4], small_shapes)
    sgu_wt = jnp.swapaxes(sgu_w, -1, -2)
    sgu_bt = jnp.swapaxes(sgu_b, -1, -2)

    xs = [x.reshape(t, d)]
    p_all = p.reshape(depth, t, pd)
    saved = []
    for l in range(depth):
        j = l // 2
        if l == 0:
            g_l = norm_g[0:1] + gather_tokens
        else:
            g_l = norm_g[l:l + 1]
            got = _exchange_wait(*gather_pending.pop(l), xs[-1], False, f"gather_wait_{l}")
            w_in_g[l], w_out_g[l], gate_g[l] = got[0], got[1].reshape(e, d), got[2].reshape(d, d)
        common = (xs[-1], p_all, l, g_l, w_in_g[l], w_out_g[l], pl_norm_g[l:l + 1], gate_g[l], proj_g[l], seq)
        if l % 2 == 0:
            h, proj, y1, x1, gate, x2 = _layer_fwd(
                *common, conv=(conv_w_g[j], conv_b[j:j + 1], conv_ln_g[j:j + 1], conv_ln_b[j:j + 1]))
        else:
            y1 = None
            h, proj, x1, gate, x2 = _layer_fwd(
                *common, sgu=(sgu_ln_g_g[j:j + 1], sgu_ln_b_g[j:j + 1], sgu_w[j], sgu_bt[j]))
        saved.append((h, proj, y1, x1, gate))
        xs.append(x2)

    loss_part, dx, d_final_g = _loss_head(xs[-1], final_g.reshape(1, d), loss_target.reshape(t, d))

    d_norm_g, d_pl_norm_g = [None] * depth, [None] * depth
    scatter_pending = {}
    d_conv_w, d_conv_b, d_conv_ln_g, d_conv_ln_b = [None] * n_conv, [None] * n_conv, [None] * n_conv, [None] * n_conv
    d_sgu_ln_g, d_sgu_ln_b, d_sgu_w, d_sgu_b = [None] * n_sgu, [None] * n_sgu, [None] * n_sgu, [None] * n_sgu
    def scatter(parts, name):
        send, recv, srcs, lnds, token = _exchange_start(parts, _place_own(parts, True, "place_own_grads"), parts[0],
                                                        True, name)
        return (send, recv, srcs, lnds), token[0, 0]

    for l in reversed(range(depth)):
        j = l // 2
        h, proj, y1, x1, gate = saved[l]
        ple_args = (dx, x1, gate, p_all, l, pl_norm_g[l:l + 1], gate_g[l], proj_g[l])
        if l % 2 == 0:
            (dx1, dgate_p, dprojw_p, dw_out_p, d_pl_norm_g[l], dy1, dz, d_conv_ln_g[j], d_conv_ln_b[j],
             d_conv_b[j]) = _ple_outproj_conv_bwd(*ple_args, y1, proj, w_out_g[l], conv_ln_g[j:j + 1],
                                                  conv_ln_b[j:j + 1])
        else:
            dx1, dgate_p, dprojw_p, d_pl_norm_g[l] = _ple_bwd(*ple_args)
            dproj, dw_out_p, d_sgu_w[j], d_sgu_b[j], d_sgu_ln_g[j], d_sgu_ln_b[j] = _outproj_sgu_bwd(
                dx1, proj, w_out_g[l], sgu_ln_g_g[j:j + 1], sgu_ln_b_g[j:j + 1], sgu_w[j], sgu_wt[j], sgu_bt[j])
        early = [dw_out_p.reshape(N_DEV, e // N_DEV, d), dgate_p.reshape(N_DEV, d // N_DEV, d), dprojw_p]
        early_token = 0.0
        if l == 0:
            scatter_pending["0_early"], early_token = scatter(early, "scatter_start_0_early")
            early = []
        if l % 2 == 0:
            dproj, d_conv_w[j] = _conv_bwd(dy1, proj, dz, conv_w_g[j] + early_token, seq)
        scatter_pending[l], token = scatter([_inproj_bwd_w(h, dproj)] + early, f"scatter_start_{l}")
        dx, d_norm_g[l] = _inproj_bwd_x(dproj, w_in_g[l], dx1, xs[l], norm_g[l:l + 1] + token)
    grad_x = dx.reshape(bsz, seq, d)

    def own_eighths(full):
        return jnp.moveaxis(full.reshape(full.shape[:-1] + (N_DEV, full.shape[-1] // N_DEV)), -2, 0)

    small_parts = [own_eighths(jnp.stack(d_conv_w)), own_eighths(jnp.concatenate(d_sgu_ln_g, axis=0)),
                   own_eighths(jnp.concatenate(d_sgu_ln_b, axis=0))]
    rep_parts = [jnp.concatenate(d_norm_g, axis=0), jnp.concatenate(d_conv_b, axis=0),
                 jnp.concatenate(d_conv_ln_g, axis=0), jnp.concatenate(d_conv_ln_b, axis=0), jnp.stack(d_sgu_w),
                 jnp.stack(d_sgu_b), jnp.concatenate(d_pl_norm_g, axis=0), d_final_g,
                 jnp.broadcast_to(loss_part, (8, 128))]
    small_send, small_recv, small_srcs, small_lnds, small_token = _exchange_start(
        small_parts, _place_own(small_parts, True, "place_own_small"), grad_x, True, "scatter_small_start")
    rep_send, rep_recv, rep_srcs, rep_lnds, rep_token = _exchange_start(
        rep_parts, _place_own(rep_parts, False, "place_own_replicated"), grad_x, False, "gather_replicated_start")

    landed = {}
    for key in list(scatter_pending):
        landed[key] = _exchange_wait(*scatter_pending.pop(key), small_token + rep_token, True, f"scatter_wait_{key}")
    dw_in_l = [landed[l][0] for l in range(depth)]
    rest = [landed["0_early"]] + [landed[l][1:] for l in range(1, depth)]

    o_w_in = _adamw(dw_in_l, w_in, m_w_in, v_w_in)
    o_w_out = _adamw([r[0] for r in rest], w_out, m_w_out, v_w_out)
    o_gate = _adamw([r[1] for r in rest], pl_gate_w, m_pl_gate_w, v_pl_gate_w)
    o_projw = _adamw([r[2] for r in rest], pl_proj_w, m_pl_proj_w, v_pl_proj_w)
    r_small = _exchange_wait(small_send, small_recv, small_srcs, small_lnds, o_projw[1], True, "scatter_small_wait")
    r_rep = _exchange_wait(rep_send, rep_recv, rep_srcs, rep_lnds, o_w_in[1], False, "gather_replicated_wait")

    row = lambda a: a.reshape(1, -1)
    o_small = _adamw_small(
        r_rep[:8] + r_small,
        [norm_g, conv_b, conv_ln_g, conv_ln_b, sgu_w, sgu_b, pl_norm_g, row(final_g), conv_w, sgu_ln_g, sgu_ln_b],
        [m_norm_g, m_conv_b, m_conv_ln_g, m_conv_ln_b, m_sgu_w, m_sgu_b, m_pl_norm_g, row(m_final_g), m_conv_w,
         m_sgu_ln_g, m_sgu_ln_b],
        [v_norm_g, v_conv_b, v_conv_ln_g, v_conv_ln_b, v_sgu_w, v_sgu_b, v_pl_norm_g, row(v_final_g), v_conv_w,
         v_sgu_ln_g, v_sgu_ln_b],
        r_rep[8])
    loss = o_small[4][0, 0]

    def leaf(kind):
        sm = o_small[kind]
        return [sm[0], o_w_in[kind], o_w_out[kind], sm[8], sm[1], sm[2], sm[3], sm[9], sm[10], sm[4], sm[5], sm[6],
                o_gate[kind], o_projw[kind], sm[7].reshape(final_g.shape)]

    return (loss, grad_x, *leaf(0), *leaf(1), *leaf(2), *leaf(3))
```

```python
import math

import jax
import jax.numpy as jnp
from jax import lax
from jax.experimental import pallas as pl
from jax.experimental.pallas import tpu as pltpu

F32 = jnp.float32
MXU_DTYPE = jnp.bfloat16
WIRE_DTYPE = jnp.bfloat16

EPS = 1e-6
CONV_K = 31
CHUNK = 128
GROUPS = 8
HALO = 32
N_DEV = 8
DEPTH = 4

ADAM_LR = 0.001
ADAM_B1 = 0.9
ADAM_B2 = 0.999
ADAM_EPS = 1e-08
ADAM_WD = 0.01
ADAM_STEP = 10

TM_IN = 512
TM_MIX = 256
TM_OUT = 512
FUSE_SB = 256
CONV_RC = 64
CONV_CC = 128
DCW_CC = 256
DCW_RC = 64
DCW_CHUNKS = 3
PACK_COLS = 1024

MESH_ID = pl.DeviceIdType.MESH
INV_SQRT2 = 1.0 / math.sqrt(2.0)
INV_SQRT_2PI = 1.0 / math.sqrt(2.0 * math.pi)


def _params(n_grid, vmem_mb):
    return pltpu.CompilerParams(dimension_semantics=("arbitrary",) * n_grid, vmem_limit_bytes=vmem_mb << 20)


def _whole(shape):
    nd = len(shape)
    return pl.BlockSpec(shape, lambda *_: (0,) * nd, pipeline_mode=pl.Buffered(1))


def _acc_out(shape):
    nd = len(shape)
    return pl.BlockSpec(shape, lambda *_: (0,) * nd)


def _dot(a, b):
    return jnp.dot(a.astype(MXU_DTYPE), b.astype(MXU_DTYPE), preferred_element_type=F32)


def _dot_nt(a, b):
    return lax.dot_general(a.astype(MXU_DTYPE), b.astype(MXU_DTYPE), (((1,), (1,)), ((), ())),
                           preferred_element_type=F32)


def _dot_tn(a, b):
    return lax.dot_general(a.astype(MXU_DTYPE), b.astype(MXU_DTYPE), (((0,), (0,)), ((), ())),
                           preferred_element_type=F32)


def _sigmoid(x):
    return jax.nn.sigmoid(x)


def _rms_rstd(x):
    return lax.rsqrt(jnp.mean(x * x, axis=-1, keepdims=True) + EPS)


def _rms_bwd(dy, x, rstd, g):
    gy = dy * g
    xr = x * rstd
    dx = rstd * (gy - xr * jnp.mean(gy * xr, axis=-1, keepdims=True))
    dg = jnp.sum(dy * xr, axis=0, keepdims=True)
    return dx, dg


def _ln_stats(x):
    mu = jnp.mean(x, axis=-1, keepdims=True)
    xc = x - mu
    var = jnp.mean(xc * xc, axis=-1, keepdims=True)
    rstd = lax.rsqrt(var + EPS)
    return xc * rstd, rstd


def _ln_bwd(dxhat, xhat, rstd):
    return rstd * (dxhat - jnp.mean(dxhat, axis=-1, keepdims=True)
                   - xhat * jnp.mean(dxhat * xhat, axis=-1, keepdims=True))


def _silu_grad(x, s):
    return s * (1.0 + x * (1.0 - s))


def _tril_mask():
    r = lax.broadcasted_iota(jnp.int32, (CHUNK, CHUNK), 0)
    c = lax.broadcasted_iota(jnp.int32, (CHUNK, CHUNK), 1)
    return r >= c


def _conv_weights_to_sublanes(w_ref, w8_ref):
    for k in range(CONV_K):
        w8_ref[k] = jnp.broadcast_to(w_ref[k:k + 1, :], w8_ref.shape[1:])


def _conv_apply(src_ref, w8_ref, zs_ref, base, tm, e, flip, emit):
    def row_block(i, carry):
        r0 = pl.multiple_of(i * CONV_RC, CONV_RC)
        for c0 in range(0, e, CONV_CC):
            cols = slice(c0, c0 + CONV_CC)
            acc = None
            for s in range(8):
                nrows = CONV_RC if s == 0 else CONV_RC + 8
                taps = [k for k in range(CONV_K) if (base + k) % 8 == s]
                off0 = base + taps[0] - s
                span = nrows + 8 * (len(taps) - 1)
                window = src_ref[pl.ds(r0 + off0, span), cols].reshape(span // 8, 8, CONV_CC)
                z = None
                for m, k in enumerate(taps):
                    wk = (CONV_K - 1 - k) if flip else k
                    term = w8_ref[wk, :, cols][None] * window[m:m + nrows // 8]
                    z = term if z is None else z + term
                z = z.reshape(nrows, CONV_CC)
                if s == 0:
                    acc = z
                else:
                    zs_ref[s - 1, pl.ds(0, nrows), :] = z
                    acc = acc + zs_ref[s - 1, pl.ds(s, CONV_RC), :]
            emit(r0, c0, acc)
        return carry

    lax.fori_loop(0, tm // CONV_RC, row_block, 0)


def _mesh_pos():
    return lax.axis_index("x"), lax.axis_index("y"), lax.axis_index("c")


def _slot(px, py, pc):
    return 4 * px + 2 * py + pc


def _peers(x, y, c):
    return [((1 - x) if (k & 4) else x, (1 - y) if (k & 2) else y, (1 - c) if (k & 1) else c)
            for k in range(1, N_DEV)]


HBM_SPEC = pl.BlockSpec(memory_space=pltpu.HBM)
SEM_SPEC = pl.BlockSpec(memory_space=pltpu.SEMAPHORE)
SIDE_EFFECT = pltpu.SideEffectType.DATAFLOW_SIDE_EFFECTING


def _exchange_copy(src_refs, land_refs, send_sems, recv_sems, i, k, peer, scatter, me):
    slot = _slot(*peer)
    return pltpu.make_async_remote_copy(
        src_ref=src_refs[i].at[slot] if scatter else src_refs[i],
        dst_ref=land_refs[i].at[me if me is not None else slot],
        send_sem=send_sems.at[i * 7 + k], recv_sem=recv_sems.at[i * 7 + k],
        device_id=peer, device_id_type=MESH_ID)


def _exchange_start(srcs, lands, after, scatter, name):
    n = len(srcs)

    def body(*refs):
        src_refs, land_refs = refs[:n], refs[n:2 * n]
        send_sems, recv_sems, token = refs[2 * n + 1], refs[2 * n + 2], refs[-1]
        x, y, c = _mesh_pos()
        me = _slot(x, y, c)
        for i in range(n):
            for k, peer in enumerate(_peers(x, y, c)):
                _exchange_copy(src_refs, land_refs, send_sems, recv_sems, i, k, peer, scatter, me).start()
        token[...] = jnp.zeros_like(token)

    arrays = list(srcs) + list(lands)
    outs = pl.pallas_call(
        body, name=name,
        out_shape=(pltpu.SemaphoreType.DMA((7 * n,)), pltpu.SemaphoreType.DMA((7 * n,)),
                   *[pltpu.HBM(a.shape, a.dtype) for a in lands], jax.ShapeDtypeStruct((8, 128), F32)),
        in_specs=[HBM_SPEC] * (2 * n) + [pl.BlockSpec(memory_space=pl.ANY)],
        out_specs=(SEM_SPEC, SEM_SPEC, *[HBM_SPEC] * n, pl.BlockSpec(memory_space=pltpu.VMEM)),
        input_output_aliases={n + i: 2 + i for i in range(n)},
        compiler_params=pltpu.CompilerParams(has_side_effects=SIDE_EFFECT),
    )(*arrays, after)
    return outs[0], outs[1], list(srcs), list(outs[2:2 + n]), outs[-1]


def _exchange_wait(send_sems, recv_sems, srcs, lands, after, scatter, name):
    n = len(srcs)

    def body(*refs):
        src_refs, land_refs = refs[:n], refs[n:2 * n]
        send, recv = refs[2 * n], refs[2 * n + 1]
        x, y, c = _mesh_pos()
        for i in range(n):
            for k, peer in enumerate(_peers(x, y, c)):
                cp = _exchange_copy(src_refs, land_refs, send, recv, i, k, peer, scatter, None)
                cp.wait_send()
                cp.wait_recv()

    arrays = list(srcs) + list(lands)
    outs = pl.pallas_call(
        body, name=name,
        out_shape=tuple(pltpu.HBM(a.shape, a.dtype) for a in lands),
        in_specs=[HBM_SPEC] * (2 * n) + [SEM_SPEC, SEM_SPEC, pl.BlockSpec(memory_space=pl.ANY)],
        out_specs=tuple([HBM_SPEC] * n),
        input_output_aliases={n + i: i for i in range(n)},
        compiler_params=pltpu.CompilerParams(has_side_effects=SIDE_EFFECT),
    )(*arrays, send_sems, recv_sems, after)
    return list(outs)


def _place_own(parts, scatter, name):
    n = len(parts)
    me = jnp.reshape(_slot(*_mesh_pos()), (1,)).astype(jnp.int32)

    def body(me_ref, *refs):
        for i in range(n):
            refs[n + i][0] = refs[i][0] if scatter else refs[i][...]

    def slot_spec(shape):
        rest = len(shape)
        return pl.BlockSpec((1,) + tuple(shape), lambda i, me_ref: (me_ref[0],) + (0,) * rest)

    def whole_spec(shape):
        nd = len(shape)
        return pl.BlockSpec(tuple(shape), lambda i, me_ref: (0,) * nd)

    blocks = [a.shape[1:] if scatter else a.shape for a in parts]
    return pl.pallas_call(
        body, name=name,
        grid_spec=pltpu.PrefetchScalarGridSpec(
            num_scalar_prefetch=1, grid=(1,),
            in_specs=[slot_spec(b) if scatter else whole_spec(b) for b in blocks],
            out_specs=[slot_spec(b) for b in blocks]),
        out_shape=[pltpu.HBM((N_DEV,) + tuple(b), a.dtype) for a, b in zip(parts, blocks)],
        compiler_params=_params(1, 32),
    )(me, *parts)


def _all_gather(items, name):
    n = len(items)

    def body(*refs):
        in_refs, out_refs = refs[:n], refs[n:2 * n]
        send_sems, recv_sems, local_sems = refs[2 * n:]
        x, y, c = _mesh_pos()
        me, sibling = (x, y, c), (x, y, 1 - c)
        chips = [(1 - x, y), (x, 1 - y), (1 - x, 1 - y)]

        def copy(i, k, block, to, src=None):
            dst = out_refs[i].at[_slot(*block)]
            return pltpu.make_async_remote_copy(
                src_ref=dst if src is None else src, dst_ref=dst,
                send_sem=send_sems.at[i * 7 + k], recv_sem=recv_sems.at[i * 7 + k],
                device_id=to, device_id_type=MESH_ID)

        mine = [pltpu.make_async_copy(in_refs[i], out_refs[i].at[_slot(*me)], local_sems.at[i]) for i in range(n)]
        for cp in mine:
            cp.start()
        first = []
        for i in range(n):
            first.append(copy(i, 0, me, sibling, src=in_refs[i]))
            for j, chip in enumerate(chips):
                first.append(copy(i, 1 + j, me, (*chip, c), src=in_refs[i]))
        for cp in first:
            cp.start()
        passed = []
        for j, chip in enumerate(chips):
            for i in range(n):
                copy(i, 1 + j, (*chip, c), me).wait_recv()
                fwd = copy(i, 4 + j, (*chip, c), sibling)
                fwd.start()
                passed.append(fwd)
        for i in range(n):
            copy(i, 0, sibling, me).wait_recv()
            for j, chip in enumerate(chips):
                copy(i, 4 + j, (*chip, 1 - c), me).wait_recv()
        for cp in first + passed:
            cp.wait_send()
        for cp in mine:
            cp.wait()

    any_spec = pl.BlockSpec(memory_space=pl.ANY)
    return pl.pallas_call(
        body, name=name,
        out_shape=[jax.ShapeDtypeStruct((N_DEV,) + a.shape, a.dtype) for a in items],
        in_specs=[any_spec] * n, out_specs=[any_spec] * n,
        scratch_shapes=[pltpu.SemaphoreType.DMA((7 * n,)), pltpu.SemaphoreType.DMA((7 * n,)),
                        pltpu.SemaphoreType.DMA((n,))],
    )(*items)


def _layer_fwd(x, p_all, layer, g, w_blk, w_out, plg, gate_w, proj_w, seq, conv=None, sgu=None):
    t, d = x.shape
    nb, _, bn = w_blk.shape
    e = w_out.shape[0]
    pd = p_all.shape[-1]
    tm = TM_MIX
    nt = seq // tm
    is_conv = conv is not None
    mixer_args = conv if is_conv else sgu
    n_mix = len(mixer_args)

    def body(*refs):
        x_ref, p_ref, g_ref, w_ref, wo_ref, plg_ref, gw_ref, pw_ref = refs[:8]
        mix = refs[8:8 + n_mix]
        outs = refs[8 + n_mix:]
        if is_conv:
            cw_ref, cb_ref, lg_ref, lb_ref = mix
            h_ref, proj_ref, y1_ref, x1_ref, gate_ref, x2_ref, y0s, zs, w8 = outs

            @pl.when(lax.rem(pl.program_id(0), nt) == 0)
            def _():
                y0s[pl.ds(0, HALO), :] = jnp.zeros((HALO, e), F32)
            _conv_weights_to_sublanes(cw_ref, w8)
        else:
            lg_ref, lb_ref, sw_ref, sbt_ref = mix
            h_ref, proj_ref, x1_ref, gate_ref, x2_ref, mixed_s = outs

        for sb in range(tm // FUSE_SB):
            rows = pl.ds(sb * FUSE_SB, FUSE_SB)
            xv = x_ref[rows, :]
            hv = (xv * _rms_rstd(xv) * g_ref[...]).astype(MXU_DTYPE)
            h_ref[rows, :] = hv
            for j in range(nb):
                proj_ref[rows, j * bn:(j + 1) * bn] = jnp.dot(hv, w_ref[j], preferred_element_type=F32)
            if is_conv:
                y0s[pl.ds(HALO + sb * FUSE_SB, FUSE_SB), :] = proj_ref[rows, 0:e] * _sigmoid(proj_ref[rows, e:2 * e])

        if is_conv:
            def emit(r0, c0, acc):
                y1_ref[pl.ds(r0, CONV_RC), c0:c0 + CONV_CC] = acc + cb_ref[:, c0:c0 + CONV_CC]
            _conv_apply(y0s, w8, zs, HALO - (CONV_K - 1), tm, e, False, emit)

        for sb in range(tm // FUSE_SB):
            rows = pl.ds(sb * FUSE_SB, FUSE_SB)
            if is_conv:
                xhat, _ = _ln_stats(y1_ref[rows, :])
                y2 = xhat * lg_ref[...] + lb_ref[...]
                y = y2 * _sigmoid(y2)
            else:
                _, _, u, _, _, _ = _sgu_parts(proj_ref[rows, 0:e], proj_ref[rows, e:2 * e], lg_ref[...], lb_ref[...],
                                              sw_ref, sbt_ref, mixed_s, FUSE_SB, e)
                y = u * mixed_s[...]
            z = proj_ref[rows, 2 * e:3 * e]
            q = (y * (z * _sigmoid(z))).astype(MXU_DTYPE)
            x1 = x_ref[rows, :] + jnp.dot(q, wo_ref[...], preferred_element_type=F32)
            x1_ref[rows, :] = x1
            rn = x1 * _rms_rstd(x1) * plg_ref[...]
            gate = _sigmoid(_dot(rn, gw_ref[...]))
            gate_ref[rows, :] = gate
            x2_ref[rows, :] = x1 + gate * _dot(p_ref[0, rows, :], pw_ref[...])

        if is_conv:
            y0s[pl.ds(0, HALO), :] = y0s[pl.ds(tm, HALO), :]

    row = lambda w: pl.BlockSpec((tm, w), lambda i: (i, 0))
    f32 = lambda w: jax.ShapeDtypeStruct((t, w), F32)
    out_shape = ([jax.ShapeDtypeStruct((t, d), MXU_DTYPE), f32(3 * e)] + ([f32(e)] if is_conv else [])
                 + [f32(d), f32(d), f32(d)])
    out_specs = [row(d), row(3 * e)] + ([row(e)] if is_conv else []) + [row(d), row(d), row(d)]
    scratch = ([pltpu.VMEM((tm + HALO, e), F32), pltpu.VMEM((7, CONV_RC + 8, CONV_CC), F32),
                pltpu.VMEM((CONV_K, 8, e), F32)] if is_conv else [pltpu.VMEM((FUSE_SB, e), F32)])
    return pl.pallas_call(
        body, name="layer_fwd_conv" if is_conv else "layer_fwd_sgu", grid=(t // tm,),
        in_specs=[row(d), pl.BlockSpec((1, tm, pd), lambda i: (layer, i, 0)), _whole((1, d)), _whole(w_blk.shape),
                  _whole((e, d)), _whole((1, d)),
                  _whole((d, d)), _whole((pd, d))] + [_whole(a.shape) for a in mixer_args],
        out_specs=out_specs, out_shape=out_shape, scratch_shapes=scratch,
        compiler_params=_params(1, 60),
    )(x, p_all, g, w_blk, w_out, plg, gate_w, proj_w, *mixer_args)


def _sgu_parts(a, b, lg, lb, sw_ref, sbt_ref, mixed_s, tm, e):
    eg = e // GROUPS
    ea = lax.erf(a * INV_SQRT2)
    eb = lax.erf(b * INV_SQRT2)
    u = 0.5 * a * (1.0 + ea)
    v0 = 0.5 * b * (1.0 + eb)
    xhat, rstd = _ln_stats(v0)
    v = (xhat * lg + lb).astype(MXU_DTYPE)
    mask = _tril_mask()
    for g in range(GROUPS):
        wt = jnp.where(mask, sw_ref[g], 0.0).astype(MXU_DTYPE)
        bcol = sbt_ref[:, g:g + 1]
        for ch in range(tm // CHUNK):
            rows = slice(ch * CHUNK, (ch + 1) * CHUNK)
            cols = slice(g * eg, (g + 1) * eg)
            mixed_s[rows, cols] = jnp.dot(wt, v[rows, cols], preferred_element_type=F32) + bcol
    return ea, eb, u, xhat, rstd, v


def _loss_head(xf, fg, tgt):
    t, d = xf.shape
    tm = TM_OUT
    nsteps = t // tm

    def body(x_ref, g_ref, t_ref, loss_ref, dx_ref, dg_ref, sq_s):
        i = pl.program_id(0)

        @pl.when(i == 0)
        def _():
            sq_s[...] = jnp.zeros_like(sq_s)
            dg_ref[...] = jnp.zeros_like(dg_ref)
        x = x_ref[...]
        rstd = _rms_rstd(x)
        err = x * rstd * g_ref[...] - t_ref[...]
        sq_s[...] += jnp.sum(err * err, axis=0, keepdims=True)
        dx, dg = _rms_bwd(err * (1.0 / d), x, rstd, g_ref[...])
        dx_ref[...] = dx
        dg_ref[...] += dg

        @pl.when(i == nsteps - 1)
        def _():
            loss_ref[...] = jnp.sum(sq_s[...], axis=1, keepdims=True) * (0.5 / d)

    row = pl.BlockSpec((tm, d), lambda i: (i, 0))
    return pl.pallas_call(
        body, name="loss_head", grid=(nsteps,),
        in_specs=[row, _whole((1, d)), row],
        out_specs=[_acc_out((1, 1)), row, _acc_out((1, d))],
        out_shape=[jax.ShapeDtypeStruct((1, 1), F32), jax.ShapeDtypeStruct((t, d), F32),
                   jax.ShapeDtypeStruct((1, d), F32)],
        scratch_shapes=[pltpu.VMEM((1, d), F32)],
        compiler_params=_params(1, 32),
    )(xf, fg, tgt)


def _ple_bwd(dx2, x1, gate, p_all, layer, plg, gate_w, proj_w):
    t, d = x1.shape
    pd = p_all.shape[-1]
    tm = TM_OUT
    nsteps = t // tm
    bn = d // N_DEV

    def body(dx2_ref, x1_ref, gate_ref, p_ref, plg_ref, gw_ref, pw_ref, dx1_ref, dgw_ref, dpw_ref, dplg_ref,
             gw_acc, pw_acc):
        i = pl.program_id(0)

        @pl.when(i == 0)
        def _():
            gw_acc[...] = jnp.zeros_like(gw_acc)
            pw_acc[...] = jnp.zeros_like(pw_acc)
            dplg_ref[...] = jnp.zeros_like(dplg_ref)
        dx2 = dx2_ref[...]
        x1 = x1_ref[...]
        plg = plg_ref[...]
        rstd = _rms_rstd(x1)
        rn = (x1 * rstd * plg).astype(MXU_DTYPE)
        gate = gate_ref[...]
        p_b = p_ref[0].astype(MXU_DTYPE)
        pp = jnp.dot(p_b, pw_ref[...], preferred_element_type=F32)
        dpp = (dx2 * gate).astype(MXU_DTYPE)
        dgpre = (dx2 * pp * gate * (1.0 - gate)).astype(MXU_DTYPE)
        pw_acc[...] += _dot_tn(p_b, dpp)
        gw_acc[...] += _dot_tn(rn, dgpre)
        drn = _dot_nt(dgpre, gw_ref[...])
        dx, dg = _rms_bwd(drn, x1, rstd, plg)
        dx1_ref[...] = dx2 + dx
        dplg_ref[...] += dg

        @pl.when(i == nsteps - 1)
        def _():
            dgw_ref[...] = gw_acc[...].astype(dgw_ref.dtype)
            for j in range(N_DEV):
                dpw_ref[j] = pw_acc[:, j * bn:(j + 1) * bn].astype(dpw_ref.dtype)

    row = lambda w: pl.BlockSpec((tm, w), lambda i: (i, 0))
    return pl.pallas_call(
        body, name="ple_bwd", grid=(nsteps,),
        in_specs=[row(d), row(d), row(d), pl.BlockSpec((1, tm, pd), lambda i: (layer, i, 0)), _whole((1, d)),
                  _whole((d, d)), _whole((pd, d))],
        out_specs=[row(d), _acc_out((d, d)), _acc_out((N_DEV, pd, bn)), _acc_out((1, d))],
        out_shape=[jax.ShapeDtypeStruct((t, d), F32), pltpu.HBM((d, d), WIRE_DTYPE),
                   pltpu.HBM((N_DEV, pd, bn), WIRE_DTYPE), jax.ShapeDtypeStruct((1, d), F32)],
        scratch_shapes=[pltpu.VMEM((d, d), F32), pltpu.VMEM((pd, d), F32)],
        compiler_params=_params(1, 48),
    )(dx2, x1, gate, p_all, plg, gate_w, proj_w)


def _outproj_conv_bwd(dx1, y1, proj, w_out, lg, lb):
    t, d = dx1.shape
    e = y1.shape[1]
    tm = TM_MIX
    nsteps = t // tm

    def body(dx1_ref, y1_ref, z_ref, wo_ref, lg_ref, lb_ref, dy1_ref, dz_ref, dwo_ref, dlg_ref, dlb_ref, dcb_ref,
             wo_acc):
        i = pl.program_id(0)

        @pl.when(i == 0)
        def _():
            wo_acc[...] = jnp.zeros_like(wo_acc)
            dlg_ref[...] = jnp.zeros_like(dlg_ref)
            dlb_ref[...] = jnp.zeros_like(dlb_ref)
            dcb_ref[...] = jnp.zeros_like(dcb_ref)
        xhat, rstd = _ln_stats(y1_ref[...])
        lg = lg_ref[...]
        y2 = xhat * lg + lb_ref[...]
        s2 = _sigmoid(y2)
        y = y2 * s2
        z = z_ref[...]
        s = _sigmoid(z)
        sz = z * s
        dx1 = dx1_ref[...].astype(MXU_DTYPE)
        wo_acc[...] += _dot_tn((y * sz).astype(MXU_DTYPE), dx1)
        dq = _dot_nt(dx1, wo_ref[...])
        dz_ref[...] = (dq * y * _silu_grad(z, s)).astype(dz_ref.dtype)
        dy2 = dq * sz * _silu_grad(y2, s2)
        dlg_ref[...] += jnp.sum(dy2 * xhat, axis=0, keepdims=True)
        dlb_ref[...] += jnp.sum(dy2, axis=0, keepdims=True)
        dy1 = _ln_bwd(dy2 * lg, xhat, rstd)
        dy1_ref[...] = dy1
        dcb_ref[...] += jnp.sum(dy1, axis=0, keepdims=True)

        @pl.when(i == nsteps - 1)
        def _():
            dwo_ref[...] = wo_acc[...].astype(dwo_ref.dtype)

    row = lambda w: pl.BlockSpec((tm, w), lambda i: (i, 0))
    return pl.pallas_call(
        body, name="outproj_conv_bwd", grid=(nsteps,),
        in_specs=[row(d), row(e), pl.BlockSpec((tm, e), lambda i: (i, 2)), _whole((e, d)), _whole((1, e)),
                  _whole((1, e))],
        out_specs=[row(e), row(e), _acc_out((e, d)), _acc_out((1, e)), _acc_out((1, e)), _acc_out((1, e))],
        out_shape=[jax.ShapeDtypeStruct((t, e), F32), jax.ShapeDtypeStruct((t, e), MXU_DTYPE),
                   pltpu.HBM((e, d), WIRE_DTYPE)] + [jax.ShapeDtypeStruct((1, e), F32)] * 3,
        scratch_shapes=[pltpu.VMEM((e, d), F32)],
        compiler_params=_params(1, 56),
    )(dx1, y1, proj, w_out, lg, lb)


def _conv_bwd(dy1, proj, dz, cw, seq):
    t, e = dy1.shape
    tm = TM_MIX
    nt = seq // tm
    hb = tm // HALO
    n_halo_blocks = t // HALO

    def body(d_ref, dn_ref, a_ref, b_ref, ah_ref, bh_ref, dz_ref, cw_ref, dproj_ref, dcw_ref,
             y0s, d1s, zs, dsh, dcw8, w8):
        i = pl.program_id(0)
        pos = lax.rem(i, nt)

        @pl.when(i == 0)
        def _():
            dcw8[...] = jnp.zeros_like(dcw8)
        _conv_weights_to_sublanes(cw_ref, w8)
        a = a_ref[...]
        sb = _sigmoid(b_ref[...])
        y0s[pl.ds(HALO, tm), :] = a * sb
        d1s[pl.ds(0, tm), :] = d_ref[...]

        @pl.when(pos == 0)
        def _():
            y0s[pl.ds(0, HALO), :] = jnp.zeros((HALO, e), F32)

        @pl.when(pos != 0)
        def _():
            y0s[pl.ds(0, HALO), :] = ah_ref[...] * _sigmoid(bh_ref[...])

        @pl.when(pos == nt - 1)
        def _():
            d1s[pl.ds(tm, HALO), :] = jnp.zeros((HALO, e), F32)

        @pl.when(pos != nt - 1)
        def _():
            d1s[pl.ds(tm, HALO), :] = dn_ref[...]

        base = HALO - (CONV_K - 1)
        for c0 in range(0, e, DCW_CC):
            cols = slice(c0, c0 + DCW_CC)
            dcur = d_ref[:, cols]
            for s in range(1, 8):
                dsh[s - 1, pl.ds(0, 8), :] = jnp.zeros((8, DCW_CC), F32)
                dsh[s - 1, pl.ds(tm, 8), :] = jnp.zeros((8, DCW_CC), F32)
                dsh[s - 1, pl.ds(s, tm), :] = dcur
            for s in range(8):
                taps = [k for k in range(CONV_K) if (base + k) % 8 == s]
                off0 = base + taps[0] - s
                n, ch = (tm, DCW_RC) if s == 0 else (tm + 8, (tm + 8) // DCW_CHUNKS)
                sums = [None] * len(taps)
                for r in range(0, n, ch):
                    dch = d_ref[r:r + ch, cols] if s == 0 else dsh[s - 1, r:r + ch, :]
                    window = y0s[pl.ds(off0 + r, ch + 8 * (len(taps) - 1)), cols]
                    for m in range(len(taps)):
                        part = jnp.sum((dch * window[8 * m:8 * m + ch]).reshape(ch // 8, 8, DCW_CC), axis=0)
                        sums[m] = part if sums[m] is None else sums[m] + part
                for m, k in enumerate(taps):
                    dcw8[k, :, cols] += sums[m]

        def emit(r0, c0, dy0):
            rs, cs = pl.ds(r0, CONV_RC), slice(c0, c0 + CONV_CC)
            sbv = _sigmoid(b_ref[rs, cs])
            av = a_ref[rs, cs]
            dproj_ref[rs, c0:c0 + CONV_CC] = (dy0 * sbv).astype(dproj_ref.dtype)
            dproj_ref[rs, e + c0:e + c0 + CONV_CC] = (dy0 * av * sbv * (1.0 - sbv)).astype(dproj_ref.dtype)
        _conv_apply(d1s, w8, zs, 0, tm, e, True, emit)
        dproj_ref[:, 2 * e:3 * e] = dz_ref[...]

        @pl.when(i == t // tm - 1)
        def _():
            dcw_ref[...] = jnp.sum(dcw8[...], axis=1)

    tile = lambda col: pl.BlockSpec((tm, e), lambda i: (i, col))
    prev = lambda col: pl.BlockSpec((HALO, e), lambda i: (jnp.maximum(i * hb - 1, 0), col))
    nxt = pl.BlockSpec((HALO, e), lambda i: (jnp.minimum((i + 1) * hb, n_halo_blocks - 1), 0))
    return pl.pallas_call(
        body, name="conv_bwd", grid=(t // tm,),
        in_specs=[tile(0), nxt, tile(0), tile(1), prev(0), prev(1), tile(0), _whole(cw.shape)],
        out_specs=[pl.BlockSpec((tm, 3 * e), lambda i: (i, 0)), _acc_out(cw.shape)],
        out_shape=[jax.ShapeDtypeStruct((t, 3 * e), MXU_DTYPE), jax.ShapeDtypeStruct(cw.shape, F32)],
        scratch_shapes=[pltpu.VMEM((tm + HALO, e), F32), pltpu.VMEM((tm + HALO, e), F32),
                        pltpu.VMEM((7, CONV_RC + 8, CONV_CC), F32), pltpu.VMEM((7, tm + 8, DCW_CC), F32),
                        pltpu.VMEM((CONV_K, 8, e), F32), pltpu.VMEM((CONV_K, 8, e), F32)],
        compiler_params=_params(1, 56),
    )(dy1, dy1, proj, proj, proj, proj, dz, cw)


def _outproj_sgu_bwd(dx1, proj, w_out, lg, lb, sw, swt, sbt):
    t, d = dx1.shape
    e = w_out.shape[0]
    eg = e // GROUPS
    tm = TM_MIX
    nsteps = t // tm

    def body(dx1_ref, a_ref, b_ref, z_ref, wo_ref, lg_ref, lb_ref, sw_ref, swt_ref, sbt_ref,
             dproj_ref, dwo_ref, dsw_ref, dsb_ref, dlg_ref, dlb_ref, mixed_s, dv_s, sb_acc, wo_acc):
        i = pl.program_id(0)

        @pl.when(i == 0)
        def _():
            wo_acc[...] = jnp.zeros_like(wo_acc)
            dsw_ref[...] = jnp.zeros_like(dsw_ref)
            sb_acc[...] = jnp.zeros_like(sb_acc)
            dlg_ref[...] = jnp.zeros_like(dlg_ref)
            dlb_ref[...] = jnp.zeros_like(dlb_ref)
        a = a_ref[...]
        b = b_ref[...]
        lg = lg_ref[...]
        ea, eb, u, xhat, rstd, v = _sgu_parts(a, b, lg, lb_ref[...], sw_ref, sbt_ref, mixed_s, tm, e)
        mixed = mixed_s[...]
        y = u * mixed
        z = z_ref[...]
        s = _sigmoid(z)
        sz = z * s
        dx1 = dx1_ref[...].astype(MXU_DTYPE)
        wo_acc[...] += _dot_tn((y * sz).astype(MXU_DTYPE), dx1)
        dq = _dot_nt(dx1, wo_ref[...])
        dproj_ref[:, 2 * e:3 * e] = (dq * y * _silu_grad(z, s)).astype(dproj_ref.dtype)
        dy = dq * sz
        du = dy * mixed
        dmixed = (dy * u).astype(MXU_DTYPE)
        mask = _tril_mask()
        mask_t = (lax.broadcasted_iota(jnp.int32, (CHUNK, CHUNK), 0)
                  <= lax.broadcasted_iota(jnp.int32, (CHUNK, CHUNK), 1))
        ones = jnp.ones((8, eg), MXU_DTYPE)
        for g in range(GROUPS):
            wtt = jnp.where(mask_t, swt_ref[g], 0.0).astype(MXU_DTYPE)
            cols = slice(g * eg, (g + 1) * eg)
            for ch in range(tm // CHUNK):
                rows = slice(ch * CHUNK, (ch + 1) * CHUNK)
                dm = dmixed[rows, cols]
                dv_s[rows, cols] = jnp.dot(wtt, dm, preferred_element_type=F32)
                dsw_ref[g] += _dot_nt(dm, v[rows, cols])
                sb_acc[g] += _dot_nt(ones, dm)
        dv = dv_s[...]
        dlg_ref[...] += jnp.sum(dv * xhat, axis=0, keepdims=True)
        dlb_ref[...] += jnp.sum(dv, axis=0, keepdims=True)
        dv0 = _ln_bwd(dv * lg, xhat, rstd)
        pdf_a = jnp.exp(-0.5 * a * a) * INV_SQRT_2PI
        pdf_b = jnp.exp(-0.5 * b * b) * INV_SQRT_2PI
        dproj_ref[:, 0:e] = (du * (0.5 * (1.0 + ea) + a * pdf_a)).astype(dproj_ref.dtype)
        dproj_ref[:, e:2 * e] = (dv0 * (0.5 * (1.0 + eb) + b * pdf_b)).astype(dproj_ref.dtype)

        @pl.when(i == nsteps - 1)
        def _():
            dwo_ref[...] = wo_acc[...].astype(dwo_ref.dtype)
            for g in range(GROUPS):
                dsw_ref[g] = jnp.where(mask, dsw_ref[g], 0.0)
                dsb_ref[g:g + 1, :] = sb_acc[g, 0:1, :]

    tile = lambda col: pl.BlockSpec((tm, e), lambda i: (i, col))
    return pl.pallas_call(
        body, name="outproj_sgu_bwd", grid=(nsteps,),
        in_specs=[pl.BlockSpec((tm, d), lambda i: (i, 0)), tile(0), tile(1), tile(2), _whole((e, d)), _whole((1, e)),
                  _whole((1, e)), _whole(sw.shape), _whole(swt.shape), _whole(sbt.shape)],
        out_specs=[pl.BlockSpec((tm, 3 * e), lambda i: (i, 0)), _acc_out((e, d)), _acc_out(sw.shape),
                   _acc_out((GROUPS, CHUNK)), _acc_out((1, e)), _acc_out((1, e))],
        out_shape=[jax.ShapeDtypeStruct((t, 3 * e), MXU_DTYPE), pltpu.HBM((e, d), WIRE_DTYPE),
                   jax.ShapeDtypeStruct(sw.shape, F32), jax.ShapeDtypeStruct((GROUPS, CHUNK), F32),
                   jax.ShapeDtypeStruct((1, e), F32), jax.ShapeDtypeStruct((1, e), F32)],
        scratch_shapes=[pltpu.VMEM((tm, e), F32), pltpu.VMEM((tm, e), F32), pltpu.VMEM((GROUPS, 8, CHUNK), F32),
                        pltpu.VMEM((e, d), F32)],
        compiler_params=_params(1, 60),
    )(dx1, proj, proj, proj, w_out, lg, lb, sw, swt, sbt)


def _inproj_bwd_x(dproj, w_blk, dx1, x, g):
    t, d = x.shape
    nb, _, bn = w_blk.shape
    tm = TM_IN

    def body(dp_ref, w_ref, dx1_ref, x_ref, g_ref, dx_ref, dg_ref):
        @pl.when(pl.program_id(0) == 0)
        def _():
            dg_ref[...] = jnp.zeros_like(dg_ref)
        dh = None
        for j in range(nb):
            term = _dot_nt(dp_ref[:, j * bn:(j + 1) * bn], w_ref[j])
            dh = term if dh is None else dh + term
        xv = x_ref[...]
        dx, dg = _rms_bwd(dh, xv, _rms_rstd(xv), g_ref[...])
        dx_ref[...] = dx1_ref[...] + dx
        dg_ref[...] += dg

    row = lambda w: pl.BlockSpec((tm, w), lambda i: (i, 0))
    return pl.pallas_call(
        body, name="inproj_bwd_x", grid=(t // tm,),
        in_specs=[row(nb * bn), _whole(w_blk.shape), row(d), row(d), _whole((1, d))],
        out_specs=[row(d), _acc_out((1, d))],
        out_shape=[jax.ShapeDtypeStruct((t, d), F32), jax.ShapeDtypeStruct((1, d), F32)],
        compiler_params=_params(1, 56),
    )(dproj, w_blk, dx1, x, g)


def _inproj_bwd_w(h, dproj):
    t, d = h.shape
    bn = dproj.shape[1] // N_DEV
    tm = TM_IN
    nsteps = t // tm

    nh = 2
    per = N_DEV // nh

    def body(h_ref, dp_ref, dw_ref, acc):
        i = pl.program_id(1)

        @pl.when(i == 0)
        def _():
            acc[...] = jnp.zeros_like(acc)
        hv = h_ref[...]
        for jj in range(per):
            acc[jj] += _dot_tn(hv, dp_ref[:, jj * bn:(jj + 1) * bn])

        @pl.when(i == nsteps - 1)
        def _():
            dw_ref[...] = acc[...].astype(dw_ref.dtype)

    return pl.pallas_call(
        body, name="inproj_bwd_w", grid=(nh, nsteps),
        in_specs=[pl.BlockSpec((tm, d), lambda hh, i: (i, 0)), pl.BlockSpec((tm, per * bn), lambda hh, i: (i, hh))],
        out_specs=[pl.BlockSpec((per, d, bn), lambda hh, i: (hh, 0, 0))],
        out_shape=[pltpu.HBM((N_DEV, d, bn), WIRE_DTYPE)],
        scratch_shapes=[pltpu.VMEM((per, d, bn), F32)],
        compiler_params=_params(2, 56),
    )(h, dproj)[0]


def _adamw(parts, w, m, v):
    nl, r, c = w.shape
    tr = r
    for cand in (512, 256, 128, 64, 32, 16, 8):
        if r % cand == 0 and cand * c * 4 <= (1 << 19):
            tr = cand
            break
    bc1 = 1.0 - ADAM_B1 ** ADAM_STEP
    bc2 = 1.0 - ADAM_B2 ** ADAM_STEP

    def body(*refs):
        p_refs = refs[:nl]
        w_ref, m_ref, v_ref, g_ref, d_ref, nm_ref, nv_ref = refs[nl:]

        def update(p_ref):
            g = p_ref[0].astype(F32)
            for s in range(1, N_DEV):
                g = g + p_ref[s].astype(F32)
            nm = ADAM_B1 * m_ref[0] + (1.0 - ADAM_B1) * g
            nv = ADAM_B2 * v_ref[0] + (1.0 - ADAM_B2) * (g * g)
            g_ref[0] = g
            nm_ref[0] = nm
            nv_ref[0] = nv
            d_ref[0] = -ADAM_LR * ((nm / bc1) / (jnp.sqrt(nv / bc2) + ADAM_EPS) + ADAM_WD * w_ref[0])

        if nl == 1:
            update(p_refs[0])
        else:
            for kk in range(nl):
                pl.when(pl.program_id(0) == kk)(lambda kk=kk: update(p_refs[kk]))

    part_spec = lambda kk: pl.BlockSpec((N_DEV, tr, c), lambda l, i: (0, jnp.where(l == kk, i, 0), 0))
    row = pl.BlockSpec((1, tr, c), lambda l, i: (l, i, 0))
    return pl.pallas_call(
        body, name="adamw", grid=(nl, r // tr),
        in_specs=[part_spec(kk) for kk in range(nl)] + [row, row, row],
        out_specs=[row] * 4,
        out_shape=[jax.ShapeDtypeStruct((nl, r, c), F32)] * 4,
        compiler_params=_params(2, 48),
    )(*parts, w, m, v)


def _adamw_small(parts, ws, ms, vs, loss_parts):
    n = len(ws)
    bc1 = 1.0 - ADAM_B1 ** ADAM_STEP
    bc2 = 1.0 - ADAM_B2 ** ADAM_STEP

    def total(ref):
        acc = ref[0]
        for s in range(1, N_DEV):
            acc = acc + ref[s]
        return acc

    def body(*refs):
        p_refs, w_refs, m_refs, v_refs = refs[:n], refs[n:2 * n], refs[2 * n:3 * n], refs[3 * n:4 * n]
        outs = refs[4 * n + 1:]
        for i in range(n):
            g = total(p_refs[i])
            nm = ADAM_B1 * m_refs[i][...] + (1.0 - ADAM_B1) * g
            nv = ADAM_B2 * v_refs[i][...] + (1.0 - ADAM_B2) * (g * g)
            outs[i][...] = g
            outs[n + i][...] = -ADAM_LR * ((nm / bc1) / (jnp.sqrt(nv / bc2) + ADAM_EPS) + ADAM_WD * w_refs[i][...])
            outs[2 * n + i][...] = nm
            outs[3 * n + i][...] = nv
        outs[4 * n][...] = total(refs[4 * n])

    vmem = pl.BlockSpec(memory_space=pltpu.VMEM)
    shapes = [jax.ShapeDtypeStruct(w.shape, F32) for w in ws]
    outs = pl.pallas_call(
        body, name="adamw_small",
        in_specs=[vmem] * (4 * n + 1), out_specs=[vmem] * (4 * n + 1),
        out_shape=shapes * 4 + [jax.ShapeDtypeStruct((8, 128), F32)],
        compiler_params=pltpu.CompilerParams(vmem_limit_bytes=48 << 20),
    )(*parts, *ws, *ms, *vs, loss_parts)
    return outs[:n], outs[n:2 * n], outs[2 * n:3 * n], outs[3 * n:4 * n], outs[4 * n]


def _pack(arrays):
    flat = jnp.concatenate([a.reshape(-1).astype(F32) for a in arrays])
    unit = 8 * PACK_COLS
    padded = -(-flat.shape[0] // unit) * unit
    return jnp.pad(flat, (0, padded - flat.shape[0])).reshape(-1, PACK_COLS)


def _unshard_last(packed, shard_shapes):
    flat = packed.reshape(N_DEV, -1)
    out, off = [], 0
    for s in shard_shapes:
        n = math.prod(s)
        a = jnp.moveaxis(flat[:, off:off + n].reshape((N_DEV,) + tuple(s)), 0, -2)
        out.append(a.reshape(tuple(s[:-1]) + (N_DEV * s[-1],)))
        off += n
    return out


def kernel(x, p, norm_g, w_in, w_out, conv_w, conv_b, conv_ln_g, conv_ln_b, sgu_ln_g, sgu_ln_b, sgu_w, sgu_b, pl_norm_g, pl_gate_w, pl_proj_w, final_g, loss_target, m_norm_g, m_w_in, m_w_out, m_conv_w, m_conv_b, m_conv_ln_g, m_conv_ln_b, m_sgu_ln_g, m_sgu_ln_b, m_sgu_w, m_sgu_b, m_pl_norm_g, m_pl_gate_w, m_pl_proj_w, m_final_g, v_norm_g, v_w_in, v_w_out, v_conv_w, v_conv_b, v_conv_ln_g, v_conv_ln_b, v_sgu_ln_g, v_sgu_ln_b, v_sgu_w, v_sgu_b, v_pl_norm_g, v_pl_gate_w, v_pl_proj_w, v_final_g):
    bsz, seq, d = x.shape
    t = bsz * seq
    depth = w_in.shape[0]
    e = w_out.shape[1] * N_DEV
    pd = p.shape[-1]
    n_conv, n_sgu = conv_w.shape[0], sgu_ln_g.shape[0]

    small_shapes = [conv_w.shape, sgu_ln_g.shape, sgu_ln_b.shape]
    cast = lambda a: a.astype(MXU_DTYPE)
    first = [cast(w_in[0]), cast(w_out[0]), cast(pl_gate_w[0]), cast(pl_proj_w), _pack([conv_w, sgu_ln_g, sgu_ln_b])]
    later = [[cast(w_in[l]), cast(w_out[l]), cast(pl_gate_w[l])] for l in range(1, depth)]
    gathered = _all_gather(first, "gather_weights")
    gather_pending, gather_tokens = {}, 0.0
    for l in range(1, depth):
        lands = _place_own(later[l - 1], False, "place_own_weights")
        send, recv, srcs, lnds, token = _exchange_start(later[l - 1], lands, gathered[0], False, f"gather_start_{l}")
        gather_pending[l] = (send, recv, srcs, lnds)
        gather_tokens = gather_tokens + token[0, 0]
    w_in_g = {0: gathered[0]}
    w_out_g = {0: gathered[1].reshape(e, d)}
    gate_g = {0: gathered[2].reshape(d, d)}
    proj_g = jnp.transpose(gathered[3], (1, 2, 0, 3)).reshape(depth, pd, d)
    conv_w_g, sgu_ln_g_g, sgu_ln_b_g = _unshard_last(gathered[4], small_shapes)
    sgu_wt = jnp.swapaxes(sgu_w, -1, -2)
    sgu_bt = jnp.swapaxes(sgu_b, -1, -2)

    xs = [x.reshape(t, d)]
    p_all = p.reshape(depth, t, pd)
    saved = []
    for l in range(depth):
        j = l // 2
        if l == 0:
            g_l = norm_g[0:1] + gather_tokens
        else:
            g_l = norm_g[l:l + 1]
            got = _exchange_wait(*gather_pending.pop(l), xs[-1], False, f"gather_wait_{l}")
            w_in_g[l], w_out_g[l], gate_g[l] = got[0], got[1].reshape(e, d), got[2].reshape(d, d)
        common = (xs[-1], p_all, l, g_l, w_in_g[l], w_out_g[l], pl_norm_g[l:l + 1], gate_g[l], proj_g[l], seq)
        if l % 2 == 0:
            h, proj, y1, x1, gate, x2 = _layer_fwd(
                *common, conv=(conv_w_g[j], conv_b[j:j + 1], conv_ln_g[j:j + 1], conv_ln_b[j:j + 1]))
        else:
            y1 = None
            h, proj, x1, gate, x2 = _layer_fwd(
                *common, sgu=(sgu_ln_g_g[j:j + 1], sgu_ln_b_g[j:j + 1], sgu_w[j], sgu_bt[j]))
        saved.append((h, proj, y1, x1, gate))
        xs.append(x2)

    loss_part, dx, d_final_g = _loss_head(xs[-1], final_g.reshape(1, d), loss_target.reshape(t, d))

    d_norm_g, d_pl_norm_g = [None] * depth, [None] * depth
    scatter_pending = {}
    d_conv_w, d_conv_b, d_conv_ln_g, d_conv_ln_b = [None] * n_conv, [None] * n_conv, [None] * n_conv, [None] * n_conv
    d_sgu_ln_g, d_sgu_ln_b, d_sgu_w, d_sgu_b = [None] * n_sgu, [None] * n_sgu, [None] * n_sgu, [None] * n_sgu
    def scatter(parts, name):
        send, recv, srcs, lnds, token = _exchange_start(parts, _place_own(parts, True, "place_own_grads"), parts[0],
                                                        True, name)
        return (send, recv, srcs, lnds), token[0, 0]

    for l in reversed(range(depth)):
        j = l // 2
        h, proj, y1, x1, gate = saved[l]
        dx1, dgate_p, dprojw_p, d_pl_norm_g[l] = _ple_bwd(dx, x1, gate, p_all, l, pl_norm_g[l:l + 1], gate_g[l],
                                                         proj_g[l])
        if l % 2 == 0:
            dy1, dz, dw_out_p, d_conv_ln_g[j], d_conv_ln_b[j], d_conv_b[j] = _outproj_conv_bwd(
                dx1, y1, proj, w_out_g[l], conv_ln_g[j:j + 1], conv_ln_b[j:j + 1])
        else:
            dproj, dw_out_p, d_sgu_w[j], d_sgu_b[j], d_sgu_ln_g[j], d_sgu_ln_b[j] = _outproj_sgu_bwd(
                dx1, proj, w_out_g[l], sgu_ln_g_g[j:j + 1], sgu_ln_b_g[j:j + 1], sgu_w[j], sgu_wt[j], sgu_bt[j])
        early = [dw_out_p.reshape(N_DEV, e // N_DEV, d), dgate_p.reshape(N_DEV, d // N_DEV, d), dprojw_p]
        early_token = 0.0
        if l == 0:
            scatter_pending["0_early"], early_token = scatter(early, "scatter_start_0_early")
            early = []
        if l % 2 == 0:
            dproj, d_conv_w[j] = _conv_bwd(dy1, proj, dz, conv_w_g[j] + early_token, seq)
        scatter_pending[l], token = scatter([_inproj_bwd_w(h, dproj)] + early, f"scatter_start_{l}")
        dx, d_norm_g[l] = _inproj_bwd_x(dproj, w_in_g[l], dx1, xs[l], norm_g[l:l + 1] + token)
    grad_x = dx.reshape(bsz, seq, d)

    def own_eighths(full):
        return jnp.moveaxis(full.reshape(full.shape[:-1] + (N_DEV, full.shape[-1] // N_DEV)), -2, 0)

    small_parts = [own_eighths(jnp.stack(d_conv_w)), own_eighths(jnp.concatenate(d_sgu_ln_g, axis=0)),
                   own_eighths(jnp.concatenate(d_sgu_ln_b, axis=0))]
    rep_parts = [jnp.concatenate(d_norm_g, axis=0), jnp.concatenate(d_conv_b, axis=0),
                 jnp.concatenate(d_conv_ln_g, axis=0), jnp.concatenate(d_conv_ln_b, axis=0), jnp.stack(d_sgu_w),
                 jnp.stack(d_sgu_b), jnp.concatenate(d_pl_norm_g, axis=0), d_final_g,
                 jnp.broadcast_to(loss_part, (8, 128))]
    small_send, small_recv, small_srcs, small_lnds, small_token = _exchange_start(
        small_parts, _place_own(small_parts, True, "place_own_small"), grad_x, True, "scatter_small_start")
    rep_send, rep_recv, rep_srcs, rep_lnds, rep_token = _exchange_start(
        rep_parts, _place_own(rep_parts, False, "place_own_replicated"), grad_x, False, "gather_replicated_start")

    landed = {}
    for key in list(scatter_pending):
        landed[key] = _exchange_wait(*scatter_pending.pop(key), small_token + rep_token, True, f"scatter_wait_{key}")
    dw_in_l = [landed[l][0] for l in range(depth)]
    rest = [landed["0_early"]] + [landed[l][1:] for l in range(1, depth)]

    o_w_in = _adamw(dw_in_l, w_in, m_w_in, v_w_in)
    o_w_out = _adamw([r[0] for r in rest], w_out, m_w_out, v_w_out)
    o_gate = _adamw([r[1] for r in rest], pl_gate_w, m_pl_gate_w, v_pl_gate_w)
    o_projw = _adamw([r[2] for r in rest], pl_proj_w, m_pl_proj_w, v_pl_proj_w)
    r_small = _exchange_wait(small_send, small_recv, small_srcs, small_lnds, o_projw[1], True, "scatter_small_wait")
    r_rep = _exchange_wait(rep_send, rep_recv, rep_srcs, rep_lnds, o_w_in[1], False, "gather_replicated_wait")

    row = lambda a: a.reshape(1, -1)
    o_small = _adamw_small(
        r_rep[:8] + r_small,
        [norm_g, conv_b, conv_ln_g, conv_ln_b, sgu_w, sgu_b, pl_norm_g, row(final_g), conv_w, sgu_ln_g, sgu_ln_b],
        [m_norm_g, m_conv_b, m_conv_ln_g, m_conv_ln_b, m_sgu_w, m_sgu_b, m_pl_norm_g, row(m_final_g), m_conv_w,
         m_sgu_ln_g, m_sgu_ln_b],
        [v_norm_g, v_conv_b, v_conv_ln_g, v_conv_ln_b, v_sgu_w, v_sgu_b, v_pl_norm_g, row(v_final_g), v_conv_w,
         v_sgu_ln_g, v_sgu_ln_b],
        r_rep[8])
    loss = o_small[4][0, 0]

    def leaf(kind):
        sm = o_small[kind]
        return [sm[0], o_w_in[kind], o_w_out[kind], sm[8], sm[1], sm[2], sm[3], sm[9], sm[10], sm[4], sm[5], sm[6],
                o_gate[kind], o_projw[kind], sm[7].reshape(final_g.shape)]

    return (loss, grad_x, *leaf(0), *leaf(1), *leaf(2), *leaf(3))
```

```python
import math

import jax
import jax.numpy as jnp
from jax import lax
from jax.experimental import pallas as pl
from jax.experimental.pallas import tpu as pltpu

F32 = jnp.float32
MXU_DTYPE = jnp.bfloat16
WIRE_DTYPE = jnp.bfloat16

EPS = 1e-6
CONV_K = 31
CHUNK = 128
GROUPS = 8
HALO = 32
N_DEV = 8
DEPTH = 4

ADAM_LR = 0.001
ADAM_B1 = 0.9
ADAM_B2 = 0.999
ADAM_EPS = 1e-08
ADAM_WD = 0.01
ADAM_STEP = 10

TM_IN = 512
TM_MIX = 256
TM_OUT = 512
FUSE_SB = 256
CONV_RC = 64
CONV_CC = 128
DCW_CC = 128
DCW_RC = 64
DCW_CHUNKS = 3
PACK_COLS = 1024

MESH_ID = pl.DeviceIdType.MESH
INV_SQRT2 = 1.0 / math.sqrt(2.0)
INV_SQRT_2PI = 1.0 / math.sqrt(2.0 * math.pi)


def _params(n_grid, vmem_mb):
    return pltpu.CompilerParams(dimension_semantics=("arbitrary",) * n_grid, vmem_limit_bytes=vmem_mb << 20)


def _whole(shape):
    nd = len(shape)
    return pl.BlockSpec(shape, lambda *_: (0,) * nd, pipeline_mode=pl.Buffered(1))


def _acc_out(shape):
    nd = len(shape)
    return pl.BlockSpec(shape, lambda *_: (0,) * nd)


def _dot(a, b):
    return jnp.dot(a.astype(MXU_DTYPE), b.astype(MXU_DTYPE), preferred_element_type=F32)


def _dot_nt(a, b):
    return lax.dot_general(a.astype(MXU_DTYPE), b.astype(MXU_DTYPE), (((1,), (1,)), ((), ())),
                           preferred_element_type=F32)


def _dot_tn(a, b):
    return lax.dot_general(a.astype(MXU_DTYPE), b.astype(MXU_DTYPE), (((0,), (0,)), ((), ())),
                           preferred_element_type=F32)


def _sigmoid(x):
    return jax.nn.sigmoid(x)


def _rms_rstd(x):
    return lax.rsqrt(jnp.mean(x * x, axis=-1, keepdims=True) + EPS)


def _rms_bwd(dy, x, rstd, g):
    gy = dy * g
    xr = x * rstd
    dx = rstd * (gy - xr * jnp.mean(gy * xr, axis=-1, keepdims=True))
    dg = jnp.sum(dy * xr, axis=0, keepdims=True)
    return dx, dg


def _ln_stats(x):
    mu = jnp.mean(x, axis=-1, keepdims=True)
    xc = x - mu
    var = jnp.mean(xc * xc, axis=-1, keepdims=True)
    rstd = lax.rsqrt(var + EPS)
    return xc * rstd, rstd


def _ln_bwd(dxhat, xhat, rstd):
    return rstd * (dxhat - jnp.mean(dxhat, axis=-1, keepdims=True)
                   - xhat * jnp.mean(dxhat * xhat, axis=-1, keepdims=True))


def _silu_grad(x, s):
    return s * (1.0 + x * (1.0 - s))


def _tril_mask():
    r = lax.broadcasted_iota(jnp.int32, (CHUNK, CHUNK), 0)
    c = lax.broadcasted_iota(jnp.int32, (CHUNK, CHUNK), 1)
    return r >= c


def _conv_weights_to_sublanes(w_ref, w8_ref):
    for k in range(CONV_K):
        w8_ref[k] = jnp.broadcast_to(w_ref[k:k + 1, :], w8_ref.shape[1:])


def _conv_apply(src_ref, w8_ref, zs_ref, base, tm, e, flip, emit):
    def row_block(i, carry):
        r0 = pl.multiple_of(i * CONV_RC, CONV_RC)
        for c0 in range(0, e, CONV_CC):
            cols = slice(c0, c0 + CONV_CC)
            acc = None
            for s in range(8):
                nrows = CONV_RC if s == 0 else CONV_RC + 8
                taps = [k for k in range(CONV_K) if (base + k) % 8 == s]
                off0 = base + taps[0] - s
                span = nrows + 8 * (len(taps) - 1)
                window = src_ref[pl.ds(r0 + off0, span), cols].reshape(span // 8, 8, CONV_CC)
                z = None
                for m, k in enumerate(taps):
                    wk = (CONV_K - 1 - k) if flip else k
                    term = w8_ref[wk, :, cols][None] * window[m:m + nrows // 8]
                    z = term if z is None else z + term
                z = z.reshape(nrows, CONV_CC)
                if s == 0:
                    acc = z
                else:
                    zs_ref[s - 1, pl.ds(0, nrows), :] = z
                    acc = acc + zs_ref[s - 1, pl.ds(s, CONV_RC), :]
            emit(r0, c0, acc)
        return carry

    lax.fori_loop(0, tm // CONV_RC, row_block, 0)


def _mesh_pos():
    return lax.axis_index("x"), lax.axis_index("y"), lax.axis_index("c")


def _slot(px, py, pc):
    return 4 * px + 2 * py + pc


def _peers(x, y, c):
    return [((1 - x) if (k & 4) else x, (1 - y) if (k & 2) else y, (1 - c) if (k & 1) else c)
            for k in range(1, N_DEV)]


HBM_SPEC = pl.BlockSpec(memory_space=pltpu.HBM)
SEM_SPEC = pl.BlockSpec(memory_space=pltpu.SEMAPHORE)
SIDE_EFFECT = pltpu.SideEffectType.DATAFLOW_SIDE_EFFECTING


def _exchange_copy(src_refs, land_refs, send_sems, recv_sems, i, k, peer, scatter, me):
    slot = _slot(*peer)
    return pltpu.make_async_remote_copy(
        src_ref=src_refs[i].at[slot] if scatter else src_refs[i],
        dst_ref=land_refs[i].at[me if me is not None else slot],
        send_sem=send_sems.at[i * 7 + k], recv_sem=recv_sems.at[i * 7 + k],
        device_id=peer, device_id_type=MESH_ID)


def _exchange_start(srcs, lands, after, scatter, name):
    n = len(srcs)

    def body(*refs):
        src_refs, land_refs = refs[:n], refs[n:2 * n]
        send_sems, recv_sems, token = refs[2 * n + 1], refs[2 * n + 2], refs[-1]
        x, y, c = _mesh_pos()
        me = _slot(x, y, c)
        for i in range(n):
            for k, peer in enumerate(_peers(x, y, c)):
                _exchange_copy(src_refs, land_refs, send_sems, recv_sems, i, k, peer, scatter, me).start()
        token[...] = jnp.zeros_like(token)

    arrays = list(srcs) + list(lands)
    outs = pl.pallas_call(
        body, name=name,
        out_shape=(pltpu.SemaphoreType.DMA((7 * n,)), pltpu.SemaphoreType.DMA((7 * n,)),
                   *[pltpu.HBM(a.shape, a.dtype) for a in lands], jax.ShapeDtypeStruct((8, 128), F32)),
        in_specs=[HBM_SPEC] * (2 * n) + [pl.BlockSpec(memory_space=pl.ANY)],
        out_specs=(SEM_SPEC, SEM_SPEC, *[HBM_SPEC] * n, pl.BlockSpec(memory_space=pltpu.VMEM)),
        input_output_aliases={n + i: 2 + i for i in range(n)},
        compiler_params=pltpu.CompilerParams(has_side_effects=SIDE_EFFECT),
    )(*arrays, after)
    return outs[0], outs[1], list(srcs), list(outs[2:2 + n]), outs[-1]


def _exchange_wait(send_sems, recv_sems, srcs, lands, after, scatter, name):
    n = len(srcs)

    def body(*refs):
        src_refs, land_refs = refs[:n], refs[n:2 * n]
        send, recv = refs[2 * n], refs[2 * n + 1]
        x, y, c = _mesh_pos()
        for i in range(n):
            for k, peer in enumerate(_peers(x, y, c)):
                cp = _exchange_copy(src_refs, land_refs, send, recv, i, k, peer, scatter, None)
                cp.wait_send()
                cp.wait_recv()

    arrays = list(srcs) + list(lands)
    outs = pl.pallas_call(
        body, name=name,
        out_shape=tuple(pltpu.HBM(a.shape, a.dtype) for a in lands),
        in_specs=[HBM_SPEC] * (2 * n) + [SEM_SPEC, SEM_SPEC, pl.BlockSpec(memory_space=pl.ANY)],
        out_specs=tuple([HBM_SPEC] * n),
        input_output_aliases={n + i: i for i in range(n)},
        compiler_params=pltpu.CompilerParams(has_side_effects=SIDE_EFFECT),
    )(*arrays, send_sems, recv_sems, after)
    return list(outs)


def _place_own(parts, scatter, name):
    n = len(parts)
    me = jnp.reshape(_slot(*_mesh_pos()), (1,)).astype(jnp.int32)

    def body(me_ref, *refs):
        for i in range(n):
            refs[n + i][0] = refs[i][0] if scatter else refs[i][...]

    def slot_spec(shape):
        rest = len(shape)
        return pl.BlockSpec((1,) + tuple(shape), lambda i, me_ref: (me_ref[0],) + (0,) * rest)

    def whole_spec(shape):
        nd = len(shape)
        return pl.BlockSpec(tuple(shape), lambda i, me_ref: (0,) * nd)

    blocks = [a.shape[1:] if scatter else a.shape for a in parts]
    return pl.pallas_call(
        body, name=name,
        grid_spec=pltpu.PrefetchScalarGridSpec(
            num_scalar_prefetch=1, grid=(1,),
            in_specs=[slot_spec(b) if scatter else whole_spec(b) for b in blocks],
            out_specs=[slot_spec(b) for b in blocks]),
        out_shape=[pltpu.HBM((N_DEV,) + tuple(b), a.dtype) for a, b in zip(parts, blocks)],
        compiler_params=_params(1, 32),
    )(me, *parts)


def _all_gather(items, name):
    n = len(items)

    def body(*refs):
        in_refs, out_refs = refs[:n], refs[n:2 * n]
        send_sems, recv_sems, local_sems = refs[2 * n:]
        x, y, c = _mesh_pos()
        me, sibling = (x, y, c), (x, y, 1 - c)
        chips = [(1 - x, y), (x, 1 - y), (1 - x, 1 - y)]

        def copy(i, k, block, to, src=None):
            dst = out_refs[i].at[_slot(*block)]
            return pltpu.make_async_remote_copy(
                src_ref=dst if src is None else src, dst_ref=dst,
                send_sem=send_sems.at[i * 7 + k], recv_sem=recv_sems.at[i * 7 + k],
                device_id=to, device_id_type=MESH_ID)

        mine = [pltpu.make_async_copy(in_refs[i], out_refs[i].at[_slot(*me)], local_sems.at[i]) for i in range(n)]
        for cp in mine:
            cp.start()
        first = []
        for i in range(n):
            first.append(copy(i, 0, me, sibling, src=in_refs[i]))
            for j, chip in enumerate(chips):
                first.append(copy(i, 1 + j, me, (*chip, c), src=in_refs[i]))
        for cp in first:
            cp.start()
        passed = []
        for j, chip in enumerate(chips):
            for i in range(n):
                copy(i, 1 + j, (*chip, c), me).wait_recv()
                fwd = copy(i, 4 + j, (*chip, c), sibling)
                fwd.start()
                passed.append(fwd)
        for i in range(n):
            copy(i, 0, sibling, me).wait_recv()
            for j, chip in enumerate(chips):
                copy(i, 4 + j, (*chip, 1 - c), me).wait_recv()
        for cp in first + passed:
            cp.wait_send()
        for cp in mine:
            cp.wait()

    any_spec = pl.BlockSpec(memory_space=pl.ANY)
    return pl.pallas_call(
        body, name=name,
        out_shape=[jax.ShapeDtypeStruct((N_DEV,) + a.shape, a.dtype) for a in items],
        in_specs=[any_spec] * n, out_specs=[any_spec] * n,
        scratch_shapes=[pltpu.SemaphoreType.DMA((7 * n,)), pltpu.SemaphoreType.DMA((7 * n,)),
                        pltpu.SemaphoreType.DMA((n,))],
    )(*items)


def _layer_fwd(x, p_all, layer, g, w_blk, w_out, plg, gate_w, proj_w, seq, conv=None, sgu=None):
    t, d = x.shape
    nb, _, bn = w_blk.shape
    e = w_out.shape[0]
    pd = p_all.shape[-1]
    tm = TM_MIX
    nt = seq // tm
    is_conv = conv is not None
    mixer_args = conv if is_conv else sgu
    n_mix = len(mixer_args)

    def body(*refs):
        x_ref, p_ref, g_ref, w_ref, wo_ref, plg_ref, gw_ref, pw_ref = refs[:8]
        mix = refs[8:8 + n_mix]
        outs = refs[8 + n_mix:]
        if is_conv:
            cw_ref, cb_ref, lg_ref, lb_ref = mix
            h_ref, proj_ref, y1_ref, x1_ref, gate_ref, x2_ref, y0s, zs, w8 = outs

            @pl.when(lax.rem(pl.program_id(0), nt) == 0)
            def _():
                y0s[pl.ds(0, HALO), :] = jnp.zeros((HALO, e), F32)
            _conv_weights_to_sublanes(cw_ref, w8)
        else:
            lg_ref, lb_ref, sw_ref, sbt_ref = mix
            h_ref, proj_ref, x1_ref, gate_ref, x2_ref, mixed_s = outs

        for sb in range(tm // FUSE_SB):
            rows = pl.ds(sb * FUSE_SB, FUSE_SB)
            xv = x_ref[rows, :]
            hv = (xv * _rms_rstd(xv) * g_ref[...]).astype(MXU_DTYPE)
            h_ref[rows, :] = hv
            for j in range(nb):
                proj_ref[rows, j * bn:(j + 1) * bn] = jnp.dot(hv, w_ref[j], preferred_element_type=F32)
            if is_conv:
                y0s[pl.ds(HALO + sb * FUSE_SB, FUSE_SB), :] = proj_ref[rows, 0:e] * _sigmoid(proj_ref[rows, e:2 * e])

        if is_conv:
            def emit(r0, c0, acc):
                y1_ref[pl.ds(r0, CONV_RC), c0:c0 + CONV_CC] = acc + cb_ref[:, c0:c0 + CONV_CC]
            _conv_apply(y0s, w8, zs, HALO - (CONV_K - 1), tm, e, False, emit)

        for sb in range(tm // FUSE_SB):
            rows = pl.ds(sb * FUSE_SB, FUSE_SB)
            if is_conv:
                xhat, _ = _ln_stats(y1_ref[rows, :])
                y2 = xhat * lg_ref[...] + lb_ref[...]
                y = y2 * _sigmoid(y2)
            else:
                _, _, u, _, _, _ = _sgu_parts(proj_ref[rows, 0:e], proj_ref[rows, e:2 * e], lg_ref[...], lb_ref[...],
                                              sw_ref, sbt_ref, mixed_s, FUSE_SB, e)
                y = u * mixed_s[...]
            z = proj_ref[rows, 2 * e:3 * e]
            q = (y * (z * _sigmoid(z))).astype(MXU_DTYPE)
            x1 = x_ref[rows, :] + jnp.dot(q, wo_ref[...], preferred_element_type=F32)
            x1_ref[rows, :] = x1
            rn = x1 * _rms_rstd(x1) * plg_ref[...]
            gate = _sigmoid(_dot(rn, gw_ref[...]))
            gate_ref[rows, :] = gate
            x2_ref[rows, :] = x1 + gate * _dot(p_ref[0, rows, :], pw_ref[...])

        if is_conv:
            y0s[pl.ds(0, HALO), :] = y0s[pl.ds(tm, HALO), :]

    row = lambda w: pl.BlockSpec((tm, w), lambda i: (i, 0))
    f32 = lambda w: jax.ShapeDtypeStruct((t, w), F32)
    out_shape = ([jax.ShapeDtypeStruct((t, d), MXU_DTYPE), f32(3 * e)] + ([f32(e)] if is_conv else [])
                 + [f32(d), f32(d), f32(d)])
    out_specs = [row(d), row(3 * e)] + ([row(e)] if is_conv else []) + [row(d), row(d), row(d)]
    scratch = ([pltpu.VMEM((tm + HALO, e), F32), pltpu.VMEM((7, CONV_RC + 8, CONV_CC), F32),
                pltpu.VMEM((CONV_K, 8, e), F32)] if is_conv else [pltpu.VMEM((FUSE_SB, e), F32)])
    return pl.pallas_call(
        body, name="layer_fwd_conv" if is_conv else "layer_fwd_sgu", grid=(t // tm,),
        in_specs=[row(d), pl.BlockSpec((1, tm, pd), lambda i: (layer, i, 0)), _whole((1, d)), _whole(w_blk.shape),
                  _whole((e, d)), _whole((1, d)),
                  _whole((d, d)), _whole((pd, d))] + [_whole(a.shape) for a in mixer_args],
        out_specs=out_specs, out_shape=out_shape, scratch_shapes=scratch,
        compiler_params=_params(1, 60),
    )(x, p_all, g, w_blk, w_out, plg, gate_w, proj_w, *mixer_args)


def _sgu_parts(a, b, lg, lb, sw_ref, sbt_ref, mixed_s, tm, e):
    eg = e // GROUPS
    ea = lax.erf(a * INV_SQRT2)
    eb = lax.erf(b * INV_SQRT2)
    u = 0.5 * a * (1.0 + ea)
    v0 = 0.5 * b * (1.0 + eb)
    xhat, rstd = _ln_stats(v0)
    v = (xhat * lg + lb).astype(MXU_DTYPE)
    mask = _tril_mask()
    for g in range(GROUPS):
        wt = jnp.where(mask, sw_ref[g], 0.0).astype(MXU_DTYPE)
        bcol = sbt_ref[:, g:g + 1]
        for ch in range(tm // CHUNK):
            rows = slice(ch * CHUNK, (ch + 1) * CHUNK)
            cols = slice(g * eg, (g + 1) * eg)
            mixed_s[rows, cols] = jnp.dot(wt, v[rows, cols], preferred_element_type=F32) + bcol
    return ea, eb, u, xhat, rstd, v


def _loss_head(xf, fg, tgt):
    t, d = xf.shape
    tm = TM_OUT
    nsteps = t // tm

    def body(x_ref, g_ref, t_ref, loss_ref, dx_ref, dg_ref, sq_s):
        i = pl.program_id(0)

        @pl.when(i == 0)
        def _():
            sq_s[...] = jnp.zeros_like(sq_s)
            dg_ref[...] = jnp.zeros_like(dg_ref)
        x = x_ref[...]
        rstd = _rms_rstd(x)
        err = x * rstd * g_ref[...] - t_ref[...]
        sq_s[...] += jnp.sum(err * err, axis=0, keepdims=True)
        dx, dg = _rms_bwd(err * (1.0 / d), x, rstd, g_ref[...])
        dx_ref[...] = dx
        dg_ref[...] += dg

        @pl.when(i == nsteps - 1)
        def _():
            loss_ref[...] = jnp.sum(sq_s[...], axis=1, keepdims=True) * (0.5 / d)

    row = pl.BlockSpec((tm, d), lambda i: (i, 0))
    return pl.pallas_call(
        body, name="loss_head", grid=(nsteps,),
        in_specs=[row, _whole((1, d)), row],
        out_specs=[_acc_out((1, 1)), row, _acc_out((1, d))],
        out_shape=[jax.ShapeDtypeStruct((1, 1), F32), jax.ShapeDtypeStruct((t, d), F32),
                   jax.ShapeDtypeStruct((1, d), F32)],
        scratch_shapes=[pltpu.VMEM((1, d), F32)],
        compiler_params=_params(1, 32),
    )(xf, fg, tgt)


def _ple_bwd(dx2, x1, gate, p_all, layer, plg, gate_w, proj_w):
    t, d = x1.shape
    pd = p_all.shape[-1]
    tm = TM_OUT
    nsteps = t // tm
    bn = d // N_DEV

    def body(dx2_ref, x1_ref, gate_ref, p_ref, plg_ref, gw_ref, pw_ref, dx1_ref, dgw_ref, dpw_ref, dplg_ref,
             gw_acc, pw_acc):
        i = pl.program_id(0)

        @pl.when(i == 0)
        def _():
            gw_acc[...] = jnp.zeros_like(gw_acc)
            pw_acc[...] = jnp.zeros_like(pw_acc)
            dplg_ref[...] = jnp.zeros_like(dplg_ref)
        dx2 = dx2_ref[...]
        x1 = x1_ref[...]
        plg = plg_ref[...]
        rstd = _rms_rstd(x1)
        rn = (x1 * rstd * plg).astype(MXU_DTYPE)
        gate = gate_ref[...]
        p_b = p_ref[0].astype(MXU_DTYPE)
        pp = jnp.dot(p_b, pw_ref[...], preferred_element_type=F32)
        dpp = (dx2 * gate).astype(MXU_DTYPE)
        dgpre = (dx2 * pp * gate * (1.0 - gate)).astype(MXU_DTYPE)
        pw_acc[...] += _dot_tn(p_b, dpp)
        gw_acc[...] += _dot_tn(rn, dgpre)
        drn = _dot_nt(dgpre, gw_ref[...])
        dx, dg = _rms_bwd(drn, x1, rstd, plg)
        dx1_ref[...] = dx2 + dx
        dplg_ref[...] += dg

        @pl.when(i == nsteps - 1)
        def _():
            dgw_ref[...] = gw_acc[...].astype(dgw_ref.dtype)
            for j in range(N_DEV):
                dpw_ref[j] = pw_acc[:, j * bn:(j + 1) * bn].astype(dpw_ref.dtype)

    row = lambda w: pl.BlockSpec((tm, w), lambda i: (i, 0))
    return pl.pallas_call(
        body, name="ple_bwd", grid=(nsteps,),
        in_specs=[row(d), row(d), row(d), pl.BlockSpec((1, tm, pd), lambda i: (layer, i, 0)), _whole((1, d)),
                  _whole((d, d)), _whole((pd, d))],
        out_specs=[row(d), _acc_out((d, d)), _acc_out((N_DEV, pd, bn)), _acc_out((1, d))],
        out_shape=[jax.ShapeDtypeStruct((t, d), F32), pltpu.HBM((d, d), WIRE_DTYPE),
                   pltpu.HBM((N_DEV, pd, bn), WIRE_DTYPE), jax.ShapeDtypeStruct((1, d), F32)],
        scratch_shapes=[pltpu.VMEM((d, d), F32), pltpu.VMEM((pd, d), F32)],
        compiler_params=_params(1, 48),
    )(dx2, x1, gate, p_all, plg, gate_w, proj_w)


def _outproj_conv_bwd(dx1, y1, proj, w_out, lg, lb):
    t, d = dx1.shape
    e = y1.shape[1]
    tm = TM_MIX
    nsteps = t // tm

    def body(dx1_ref, y1_ref, z_ref, wo_ref, lg_ref, lb_ref, dy1_ref, dz_ref, dwo_ref, dlg_ref, dlb_ref, dcb_ref,
             wo_acc):
        i = pl.program_id(0)

        @pl.when(i == 0)
        def _():
            wo_acc[...] = jnp.zeros_like(wo_acc)
            dlg_ref[...] = jnp.zeros_like(dlg_ref)
            dlb_ref[...] = jnp.zeros_like(dlb_ref)
            dcb_ref[...] = jnp.zeros_like(dcb_ref)
        xhat, rstd = _ln_stats(y1_ref[...])
        lg = lg_ref[...]
        y2 = xhat * lg + lb_ref[...]
        s2 = _sigmoid(y2)
        y = y2 * s2
        z = z_ref[...]
        s = _sigmoid(z)
        sz = z * s
        dx1 = dx1_ref[...].astype(MXU_DTYPE)
        wo_acc[...] += _dot_tn((y * sz).astype(MXU_DTYPE), dx1)
        dq = _dot_nt(dx1, wo_ref[...])
        dz_ref[...] = (dq * y * _silu_grad(z, s)).astype(dz_ref.dtype)
        dy2 = dq * sz * _silu_grad(y2, s2)
        dlg_ref[...] += jnp.sum(dy2 * xhat, axis=0, keepdims=True)
        dlb_ref[...] += jnp.sum(dy2, axis=0, keepdims=True)
        dy1 = _ln_bwd(dy2 * lg, xhat, rstd)
        dy1_ref[...] = dy1
        dcb_ref[...] += jnp.sum(dy1, axis=0, keepdims=True)

        @pl.when(i == nsteps - 1)
        def _():
            dwo_ref[...] = wo_acc[...].astype(dwo_ref.dtype)

    row = lambda w: pl.BlockSpec((tm, w), lambda i: (i, 0))
    return pl.pallas_call(
        body, name="outproj_conv_bwd", grid=(nsteps,),
        in_specs=[row(d), row(e), pl.BlockSpec((tm, e), lambda i: (i, 2)), _whole((e, d)), _whole((1, e)),
                  _whole((1, e))],
        out_specs=[row(e), row(e), _acc_out((e, d)), _acc_out((1, e)), _acc_out((1, e)), _acc_out((1, e))],
        out_shape=[jax.ShapeDtypeStruct((t, e), F32), jax.ShapeDtypeStruct((t, e), MXU_DTYPE),
                   pltpu.HBM((e, d), WIRE_DTYPE)] + [jax.ShapeDtypeStruct((1, e), F32)] * 3,
        scratch_shapes=[pltpu.VMEM((e, d), F32)],
        compiler_params=_params(1, 56),
    )(dx1, y1, proj, w_out, lg, lb)


def _conv_bwd(dy1, proj, dz, cw, seq):
    t, e = dy1.shape
    tm = TM_MIX
    nt = seq // tm
    hb = tm // HALO
    n_halo_blocks = t // HALO

    def body(d_ref, dn_ref, a_ref, b_ref, ah_ref, bh_ref, dz_ref, cw_ref, dproj_ref, dcw_ref,
             y0s, d1s, zs, dsh, dcw8, w8):
        i = pl.program_id(0)
        pos = lax.rem(i, nt)

        @pl.when(i == 0)
        def _():
            dcw8[...] = jnp.zeros_like(dcw8)
        _conv_weights_to_sublanes(cw_ref, w8)
        a = a_ref[...]
        sb = _sigmoid(b_ref[...])
        y0s[pl.ds(HALO, tm), :] = a * sb
        d1s[pl.ds(0, tm), :] = d_ref[...]

        @pl.when(pos == 0)
        def _():
            y0s[pl.ds(0, HALO), :] = jnp.zeros((HALO, e), F32)

        @pl.when(pos != 0)
        def _():
            y0s[pl.ds(0, HALO), :] = ah_ref[...] * _sigmoid(bh_ref[...])

        @pl.when(pos == nt - 1)
        def _():
            d1s[pl.ds(tm, HALO), :] = jnp.zeros((HALO, e), F32)

        @pl.when(pos != nt - 1)
        def _():
            d1s[pl.ds(tm, HALO), :] = dn_ref[...]

        base = HALO - (CONV_K - 1)
        for c0 in range(0, e, DCW_CC):
            cols = slice(c0, c0 + DCW_CC)
            dcur = d_ref[:, cols]
            for s in range(1, 8):
                dsh[s - 1, pl.ds(0, 8), :] = jnp.zeros((8, DCW_CC), F32)
                dsh[s - 1, pl.ds(tm, 8), :] = jnp.zeros((8, DCW_CC), F32)
                dsh[s - 1, pl.ds(s, tm), :] = dcur
            for s in range(8):
                taps = [k for k in range(CONV_K) if (base + k) % 8 == s]
                off0 = base + taps[0] - s
                n, ch = (tm, DCW_RC) if s == 0 else (tm + 8, (tm + 8) // DCW_CHUNKS)
                sums = [None] * len(taps)
                for r in range(0, n, ch):
                    dch = d_ref[r:r + ch, cols] if s == 0 else dsh[s - 1, r:r + ch, :]
                    window = y0s[pl.ds(off0 + r, ch + 8 * (len(taps) - 1)), cols]
                    for m in range(len(taps)):
                        part = jnp.sum((dch * window[8 * m:8 * m + ch]).reshape(ch // 8, 8, DCW_CC), axis=0)
                        sums[m] = part if sums[m] is None else sums[m] + part
                for m, k in enumerate(taps):
                    dcw8[k, :, cols] += sums[m]

        def emit(r0, c0, dy0):
            rs, cs = pl.ds(r0, CONV_RC), slice(c0, c0 + CONV_CC)
            sbv = _sigmoid(b_ref[rs, cs])
            av = a_ref[rs, cs]
            dproj_ref[rs, c0:c0 + CONV_CC] = (dy0 * sbv).astype(dproj_ref.dtype)
            dproj_ref[rs, e + c0:e + c0 + CONV_CC] = (dy0 * av * sbv * (1.0 - sbv)).astype(dproj_ref.dtype)
        _conv_apply(d1s, w8, zs, 0, tm, e, True, emit)
        dproj_ref[:, 2 * e:3 * e] = dz_ref[...]

        @pl.when(i == t // tm - 1)
        def _():
            dcw_ref[...] = jnp.sum(dcw8[...], axis=1)

    tile = lambda col: pl.BlockSpec((tm, e), lambda i: (i, col))
    prev = lambda col: pl.BlockSpec((HALO, e), lambda i: (jnp.maximum(i * hb - 1, 0), col))
    nxt = pl.BlockSpec((HALO, e), lambda i: (jnp.minimum((i + 1) * hb, n_halo_blocks - 1), 0))
    return pl.pallas_call(
        body, name="conv_bwd", grid=(t // tm,),
        in_specs=[tile(0), nxt, tile(0), tile(1), prev(0), prev(1), tile(0), _whole(cw.shape)],
        out_specs=[pl.BlockSpec((tm, 3 * e), lambda i: (i, 0)), _acc_out(cw.shape)],
        out_shape=[jax.ShapeDtypeStruct((t, 3 * e), MXU_DTYPE), jax.ShapeDtypeStruct(cw.shape, F32)],
        scratch_shapes=[pltpu.VMEM((tm + HALO, e), F32), pltpu.VMEM((tm + HALO, e), F32),
                        pltpu.VMEM((7, CONV_RC + 8, CONV_CC), F32), pltpu.VMEM((7, tm + 8, DCW_CC), F32),
                        pltpu.VMEM((CONV_K, 8, e), F32), pltpu.VMEM((CONV_K, 8, e), F32)],
        compiler_params=_params(1, 56),
    )(dy1, dy1, proj, proj, proj, proj, dz, cw)


def _outproj_sgu_bwd(dx1, proj, w_out, lg, lb, sw, swt, sbt):
    t, d = dx1.shape
    e = w_out.shape[0]
    eg = e // GROUPS
    tm = TM_MIX
    nsteps = t // tm

    def body(dx1_ref, a_ref, b_ref, z_ref, wo_ref, lg_ref, lb_ref, sw_ref, swt_ref, sbt_ref,
             dproj_ref, dwo_ref, dsw_ref, dsb_ref, dlg_ref, dlb_ref, mixed_s, dv_s, sb_acc, wo_acc):
        i = pl.program_id(0)

        @pl.when(i == 0)
        def _():
            wo_acc[...] = jnp.zeros_like(wo_acc)
            dsw_ref[...] = jnp.zeros_like(dsw_ref)
            sb_acc[...] = jnp.zeros_like(sb_acc)
            dlg_ref[...] = jnp.zeros_like(dlg_ref)
            dlb_ref[...] = jnp.zeros_like(dlb_ref)
        a = a_ref[...]
        b = b_ref[...]
        lg = lg_ref[...]
        ea, eb, u, xhat, rstd, v = _sgu_parts(a, b, lg, lb_ref[...], sw_ref, sbt_ref, mixed_s, tm, e)
        mixed = mixed_s[...]
        y = u * mixed
        z = z_ref[...]
        s = _sigmoid(z)
        sz = z * s
        dx1 = dx1_ref[...].astype(MXU_DTYPE)
        wo_acc[...] += _dot_tn((y * sz).astype(MXU_DTYPE), dx1)
        dq = _dot_nt(dx1, wo_ref[...])
        dproj_ref[:, 2 * e:3 * e] = (dq * y * _silu_grad(z, s)).astype(dproj_ref.dtype)
        dy = dq * sz
        du = dy * mixed
        dmixed = (dy * u).astype(MXU_DTYPE)
        mask = _tril_mask()
        mask_t = (lax.broadcasted_iota(jnp.int32, (CHUNK, CHUNK), 0)
                  <= lax.broadcasted_iota(jnp.int32, (CHUNK, CHUNK), 1))
        ones = jnp.ones((8, eg), MXU_DTYPE)
        for g in range(GROUPS):
            wtt = jnp.where(mask_t, swt_ref[g], 0.0).astype(MXU_DTYPE)
            cols = slice(g * eg, (g + 1) * eg)
            for ch in range(tm // CHUNK):
                rows = slice(ch * CHUNK, (ch + 1) * CHUNK)
                dm = dmixed[rows, cols]
                dv_s[rows, cols] = jnp.dot(wtt, dm, preferred_element_type=F32)
                dsw_ref[g] += _dot_nt(dm, v[rows, cols])
                sb_acc[g] += _dot_nt(ones, dm)
        dv = dv_s[...]
        dlg_ref[...] += jnp.sum(dv * xhat, axis=0, keepdims=True)
        dlb_ref[...] += jnp.sum(dv, axis=0, keepdims=True)
        dv0 = _ln_bwd(dv * lg, xhat, rstd)
        pdf_a = jnp.exp(-0.5 * a * a) * INV_SQRT_2PI
        pdf_b = jnp.exp(-0.5 * b * b) * INV_SQRT_2PI
        dproj_ref[:, 0:e] = (du * (0.5 * (1.0 + ea) + a * pdf_a)).astype(dproj_ref.dtype)
        dproj_ref[:, e:2 * e] = (dv0 * (0.5 * (1.0 + eb) + b * pdf_b)).astype(dproj_ref.dtype)

        @pl.when(i == nsteps - 1)
        def _():
            dwo_ref[...] = wo_acc[...].astype(dwo_ref.dtype)
            for g in range(GROUPS):
                dsw_ref[g] = jnp.where(mask, dsw_ref[g], 0.0)
                dsb_ref[g:g + 1, :] = sb_acc[g, 0:1, :]

    tile = lambda col: pl.BlockSpec((tm, e), lambda i: (i, col))
    return pl.pallas_call(
        body, name="outproj_sgu_bwd", grid=(nsteps,),
        in_specs=[pl.BlockSpec((tm, d), lambda i: (i, 0)), tile(0), tile(1), tile(2), _whole((e, d)), _whole((1, e)),
                  _whole((1, e)), _whole(sw.shape), _whole(swt.shape), _whole(sbt.shape)],
        out_specs=[pl.BlockSpec((tm, 3 * e), lambda i: (i, 0)), _acc_out((e, d)), _acc_out(sw.shape),
                   _acc_out((GROUPS, CHUNK)), _acc_out((1, e)), _acc_out((1, e))],
        out_shape=[jax.ShapeDtypeStruct((t, 3 * e), MXU_DTYPE), pltpu.HBM((e, d), WIRE_DTYPE),
                   jax.ShapeDtypeStruct(sw.shape, F32), jax.ShapeDtypeStruct((GROUPS, CHUNK), F32),
                   jax.ShapeDtypeStruct((1, e), F32), jax.ShapeDtypeStruct((1, e), F32)],
        scratch_shapes=[pltpu.VMEM((tm, e), F32), pltpu.VMEM((tm, e), F32), pltpu.VMEM((GROUPS, 8, CHUNK), F32),
                        pltpu.VMEM((e, d), F32)],
        compiler_params=_params(1, 60),
    )(dx1, proj, proj, proj, w_out, lg, lb, sw, swt, sbt)


def _inproj_bwd_x(dproj, w_blk, dx1, x, g):
    t, d = x.shape
    nb, _, bn = w_blk.shape
    tm = TM_IN

    def body(dp_ref, w_ref, dx1_ref, x_ref, g_ref, dx_ref, dg_ref):
        @pl.when(pl.program_id(0) == 0)
        def _():
            dg_ref[...] = jnp.zeros_like(dg_ref)
        dh = None
        for j in range(nb):
            term = _dot_nt(dp_ref[:, j * bn:(j + 1) * bn], w_ref[j])
            dh = term if dh is None else dh + term
        xv = x_ref[...]
        dx, dg = _rms_bwd(dh, xv, _rms_rstd(xv), g_ref[...])
        dx_ref[...] = dx1_ref[...] + dx
        dg_ref[...] += dg

    row = lambda w: pl.BlockSpec((tm, w), lambda i: (i, 0))
    return pl.pallas_call(
        body, name="inproj_bwd_x", grid=(t // tm,),
        in_specs=[row(nb * bn), _whole(w_blk.shape), row(d), row(d), _whole((1, d))],
        out_specs=[row(d), _acc_out((1, d))],
        out_shape=[jax.ShapeDtypeStruct((t, d), F32), jax.ShapeDtypeStruct((1, d), F32)],
        compiler_params=_params(1, 56),
    )(dproj, w_blk, dx1, x, g)


def _inproj_bwd_w(h, dproj):
    t, d = h.shape
    bn = dproj.shape[1] // N_DEV
    tm = TM_IN
    nsteps = t // tm

    nh = 2
    per = N_DEV // nh

    def body(h_ref, dp_ref, dw_ref, acc):
        i = pl.program_id(1)

        @pl.when(i == 0)
        def _():
            acc[...] = jnp.zeros_like(acc)
        hv = h_ref[...]
        for jj in range(per):
            acc[jj] += _dot_tn(hv, dp_ref[:, jj * bn:(jj + 1) * bn])

        @pl.when(i == nsteps - 1)
        def _():
            dw_ref[...] = acc[...].astype(dw_ref.dtype)

    return pl.pallas_call(
        body, name="inproj_bwd_w", grid=(nh, nsteps),
        in_specs=[pl.BlockSpec((tm, d), lambda hh, i: (i, 0)), pl.BlockSpec((tm, per * bn), lambda hh, i: (i, hh))],
        out_specs=[pl.BlockSpec((per, d, bn), lambda hh, i: (hh, 0, 0))],
        out_shape=[pltpu.HBM((N_DEV, d, bn), WIRE_DTYPE)],
        scratch_shapes=[pltpu.VMEM((per, d, bn), F32)],
        compiler_params=_params(2, 56),
    )(h, dproj)[0]


def _adamw(parts, w, m, v):
    nl, r, c = w.shape
    tr = r
    for cand in (512, 256, 128, 64, 32, 16, 8):
        if r % cand == 0 and cand * c * 4 <= (1 << 19):
            tr = cand
            break
    bc1 = 1.0 - ADAM_B1 ** ADAM_STEP
    bc2 = 1.0 - ADAM_B2 ** ADAM_STEP

    def body(*refs):
        p_refs = refs[:nl]
        w_ref, m_ref, v_ref, g_ref, d_ref, nm_ref, nv_ref = refs[nl:]

        def update(p_ref):
            g = p_ref[0].astype(F32)
            for s in range(1, N_DEV):
                g = g + p_ref[s].astype(F32)
            nm = ADAM_B1 * m_ref[0] + (1.0 - ADAM_B1) * g
            nv = ADAM_B2 * v_ref[0] + (1.0 - ADAM_B2) * (g * g)
            g_ref[0] = g
            nm_ref[0] = nm
            nv_ref[0] = nv
            d_ref[0] = -ADAM_LR * ((nm / bc1) / (jnp.sqrt(nv / bc2) + ADAM_EPS) + ADAM_WD * w_ref[0])

        if nl == 1:
            update(p_refs[0])
        else:
            for kk in range(nl):
                pl.when(pl.program_id(0) == kk)(lambda kk=kk: update(p_refs[kk]))

    part_spec = lambda kk: pl.BlockSpec((N_DEV, tr, c), lambda l, i: (0, jnp.where(l == kk, i, 0), 0))
    row = pl.BlockSpec((1, tr, c), lambda l, i: (l, i, 0))
    return pl.pallas_call(
        body, name="adamw", grid=(nl, r // tr),
        in_specs=[part_spec(kk) for kk in range(nl)] + [row, row, row],
        out_specs=[row] * 4,
        out_shape=[jax.ShapeDtypeStruct((nl, r, c), F32)] * 4,
        compiler_params=_params(2, 48),
    )(*parts, w, m, v)


def _adamw_small(parts, ws, ms, vs, loss_parts):
    n = len(ws)
    bc1 = 1.0 - ADAM_B1 ** ADAM_STEP
    bc2 = 1.0 - ADAM_B2 ** ADAM_STEP

    def total(ref):
        acc = ref[0]
        for s in range(1, N_DEV):
            acc = acc + ref[s]
        return acc

    def body(*refs):
        p_refs, w_refs, m_refs, v_refs = refs[:n], refs[n:2 * n], refs[2 * n:3 * n], refs[3 * n:4 * n]
        outs = refs[4 * n + 1:]
        for i in range(n):
            g = total(p_refs[i])
            nm = ADAM_B1 * m_refs[i][...] + (1.0 - ADAM_B1) * g
            nv = ADAM_B2 * v_refs[i][...] + (1.0 - ADAM_B2) * (g * g)
            outs[i][...] = g
            outs[n + i][...] = -ADAM_LR * ((nm / bc1) / (jnp.sqrt(nv / bc2) + ADAM_EPS) + ADAM_WD * w_refs[i][...])
            outs[2 * n + i][...] = nm
            outs[3 * n + i][...] = nv
        outs[4 * n][...] = total(refs[4 * n])

    vmem = pl.BlockSpec(memory_space=pltpu.VMEM)
    shapes = [jax.ShapeDtypeStruct(w.shape, F32) for w in ws]
    outs = pl.pallas_call(
        body, name="adamw_small",
        in_specs=[vmem] * (4 * n + 1), out_specs=[vmem] * (4 * n + 1),
        out_shape=shapes * 4 + [jax.ShapeDtypeStruct((8, 128), F32)],
        compiler_params=pltpu.CompilerParams(vmem_limit_bytes=48 << 20),
    )(*parts, *ws, *ms, *vs, loss_parts)
    return outs[:n], outs[n:2 * n], outs[2 * n:3 * n], outs[3 * n:4 * n], outs[4 * n]


def _pack(arrays):
    flat = jnp.concatenate([a.reshape(-1).astype(F32) for a in arrays])
    unit = 8 * PACK_COLS
    padded = -(-flat.shape[0] // unit) * unit
    return jnp.pad(flat, (0, padded - flat.shape[0])).reshape(-1, PACK_COLS)


def _unshard_last(packed, shard_shapes):
    flat = packed.reshape(N_DEV, -1)
    out, off = [], 0
    for s in shard_shapes:
        n = math.prod(s)
        a = jnp.moveaxis(flat[:, off:off + n].reshape((N_DEV,) + tuple(s)), 0, -2)
        out.append(a.reshape(tuple(s[:-1]) + (N_DEV * s[-1],)))
        off += n
    return out


def kernel(x, p, norm_g, w_in, w_out, conv_w, conv_b, conv_ln_g, conv_ln_b, sgu_ln_g, sgu_ln_b, sgu_w, sgu_b, pl_norm_g, pl_gate_w, pl_proj_w, final_g, loss_target, m_norm_g, m_w_in, m_w_out, m_conv_w, m_conv_b, m_conv_ln_g, m_conv_ln_b, m_sgu_ln_g, m_sgu_ln_b, m_sgu_w, m_sgu_b, m_pl_norm_g, m_pl_gate_w, m_pl_proj_w, m_final_g, v_norm_g, v_w_in, v_w_out, v_conv_w, v_conv_b, v_conv_ln_g, v_conv_ln_b, v_sgu_ln_g, v_sgu_ln_b, v_sgu_w, v_sgu_b, v_pl_norm_g, v_pl_gate_w, v_pl_proj_w, v_final_g):
    bsz, seq, d = x.shape
    t = bsz * seq
    depth = w_in.shape[0]
    e = w_out.shape[1] * N_DEV
    pd = p.shape[-1]
    n_conv, n_sgu = conv_w.shape[0], sgu_ln_g.shape[0]

    small_shapes = [conv_w.shape, sgu_ln_g.shape, sgu_ln_b.shape]
    cast = lambda a: a.astype(MXU_DTYPE)
    first = [cast(w_in[0]), cast(w_out[0]), cast(pl_gate_w[0]), cast(pl_proj_w), _pack([conv_w, sgu_ln_g, sgu_ln_b])]
    later = [[cast(w_in[l]), cast(w_out[l]), cast(pl_gate_w[l])] for l in range(1, depth)]
    gathered = _all_gather(first, "gather_weights")
    gather_pending, gather_tokens = {}, 0.0
    for l in range(1, depth):
        lands = _place_own(later[l - 1], False, "place_own_weights")
        send, recv, srcs, lnds, token = _exchange_start(later[l - 1], lands, gathered[0], False, f"gather_start_{l}")
        gather_pending[l] = (send, recv, srcs, lnds)
        gather_tokens = gather_tokens + token[0, 0]
    w_in_g = {0: gathered[0]}
    w_out_g = {0: gathered[1].reshape(e, d)}
    gate_g = {0: gathered[2].reshape(d, d)}
    proj_g = jnp.transpose(gathered[3], (1, 2, 0, 3)).reshape(depth, pd, d)
    conv_w_g, sgu_ln_g_g, sgu_ln_b_g = _unshard_last(gathered[4], small_shapes)
    sgu_wt = jnp.swapaxes(sgu_w, -1, -2)
    sgu_bt = jnp.swapaxes(sgu_b, -1, -2)

    xs = [x.reshape(t, d)]
    p_all = p.reshape(depth, t, pd)
    saved = []
    for l in range(depth):
        j = l // 2
        if l == 0:
            g_l = norm_g[0:1] + gather_tokens
        else:
            g_l = norm_g[l:l + 1]
            got = _exchange_wait(*gather_pending.pop(l), xs[-1], False, f"gather_wait_{l}")
            w_in_g[l], w_out_g[l], gate_g[l] = got[0], got[1].reshape(e, d), got[2].reshape(d, d)
        common = (xs[-1], p_all, l, g_l, w_in_g[l], w_out_g[l], pl_norm_g[l:l + 1], gate_g[l], proj_g[l], seq)
        if l % 2 == 0:
            h, proj, y1, x1, gate, x2 = _layer_fwd(
                *common, conv=(conv_w_g[j], conv_b[j:j + 1], conv_ln_g[j:j + 1], conv_ln_b[j:j + 1]))
        else:
            y1 = None
            h, proj, x1, gate, x2 = _layer_fwd(
                *common, sgu=(sgu_ln_g_g[j:j + 1], sgu_ln_b_g[j:j + 1], sgu_w[j], sgu_bt[j]))
        saved.append((h, proj, y1, x1, gate))
        xs.append(x2)

    loss_part, dx, d_final_g = _loss_head(xs[-1], final_g.reshape(1, d), loss_target.reshape(t, d))

    d_norm_g, d_pl_norm_g = [None] * depth, [None] * depth
    scatter_pending = {}
    d_conv_w, d_conv_b, d_conv_ln_g, d_conv_ln_b = [None] * n_conv, [None] * n_conv, [None] * n_conv, [None] * n_conv
    d_sgu_ln_g, d_sgu_ln_b, d_sgu_w, d_sgu_b = [None] * n_sgu, [None] * n_sgu, [None] * n_sgu, [None] * n_sgu
    def scatter(parts, name):
        send, recv, srcs, lnds, token = _exchange_start(parts, _place_own(parts, True, "place_own_grads"), parts[0],
                                                        True, name)
        return (send, recv, srcs, lnds), token[0, 0]

    for l in reversed(range(depth)):
        j = l // 2
        h, proj, y1, x1, gate = saved[l]
        dx1, dgate_p, dprojw_p, d_pl_norm_g[l] = _ple_bwd(dx, x1, gate, p_all, l, pl_norm_g[l:l + 1], gate_g[l],
                                                         proj_g[l])
        if l % 2 == 0:
            dy1, dz, dw_out_p, d_conv_ln_g[j], d_conv_ln_b[j], d_conv_b[j] = _outproj_conv_bwd(
                dx1, y1, proj, w_out_g[l], conv_ln_g[j:j + 1], conv_ln_b[j:j + 1])
        else:
            dproj, dw_out_p, d_sgu_w[j], d_sgu_b[j], d_sgu_ln_g[j], d_sgu_ln_b[j] = _outproj_sgu_bwd(
                dx1, proj, w_out_g[l], sgu_ln_g_g[j:j + 1], sgu_ln_b_g[j:j + 1], sgu_w[j], sgu_wt[j], sgu_bt[j])
        early = [dw_out_p.reshape(N_DEV, e // N_DEV, d), dgate_p.reshape(N_DEV, d // N_DEV, d), dprojw_p]
        early_token = 0.0
        if l == 0:
            scatter_pending["0_early"], early_token = scatter(early, "scatter_start_0_early")
            early = []
        if l % 2 == 0:
            dproj, d_conv_w[j] = _conv_bwd(dy1, proj, dz, conv_w_g[j] + early_token, seq)
        scatter_pending[l], token = scatter([_inproj_bwd_w(h, dproj)] + early, f"scatter_start_{l}")
        dx, d_norm_g[l] = _inproj_bwd_x(dproj, w_in_g[l], dx1, xs[l], norm_g[l:l + 1] + token)
    grad_x = dx.reshape(bsz, seq, d)

    def own_eighths(full):
        return jnp.moveaxis(full.reshape(full.shape[:-1] + (N_DEV, full.shape[-1] // N_DEV)), -2, 0)

    small_parts = [own_eighths(jnp.stack(d_conv_w)), own_eighths(jnp.concatenate(d_sgu_ln_g, axis=0)),
                   own_eighths(jnp.concatenate(d_sgu_ln_b, axis=0))]
    rep_parts = [jnp.concatenate(d_norm_g, axis=0), jnp.concatenate(d_conv_b, axis=0),
                 jnp.concatenate(d_conv_ln_g, axis=0), jnp.concatenate(d_conv_ln_b, axis=0), jnp.stack(d_sgu_w),
                 jnp.stack(d_sgu_b), jnp.concatenate(d_pl_norm_g, axis=0), d_final_g,
                 jnp.broadcast_to(loss_part, (8, 128))]
    small_send, small_recv, small_srcs, small_lnds, small_token = _exchange_start(
        small_parts, _place_own(small_parts, True, "place_own_small"), grad_x, True, "scatter_small_start")
    rep_send, rep_recv, rep_srcs, rep_lnds, rep_token = _exchange_start(
        rep_parts, _place_own(rep_parts, False, "place_own_replicated"), grad_x, False, "gather_replicated_start")

    landed = {}
    for key in list(scatter_pending):
        landed[key] = _exchange_wait(*scatter_pending.pop(key), small_token + rep_token, True, f"scatter_wait_{key}")
    dw_in_l = [landed[l][0] for l in range(depth)]
    rest = [landed["0_early"]] + [landed[l][1:] for l in range(1, depth)]

    o_w_in = _adamw(dw_in_l, w_in, m_w_in, v_w_in)
    o_w_out = _adamw([r[0] for r in rest], w_out, m_w_out, v_w_out)
    o_gate = _adamw([r[1] for r in rest], pl_gate_w, m_pl_gate_w, v_pl_gate_w)
    o_projw = _adamw([r[2] for r in rest], pl_proj_w, m_pl_proj_w, v_pl_proj_w)
    r_small = _exchange_wait(small_send, small_recv, small_srcs, small_lnds, o_projw[1], True, "scatter_small_wait")
    r_rep = _exchange_wait(rep_send, rep_recv, rep_srcs, rep_lnds, o_w_in[1], False, "gather_replicated_wait")

    row = lambda a: a.reshape(1, -1)
    o_small = _adamw_small(
        r_rep[:8] + r_small,
        [norm_g, conv_b, conv_ln_g, conv_ln_b, sgu_w, sgu_b, pl_norm_g, row(final_g), conv_w, sgu_ln_g, sgu_ln_b],
        [m_norm_g, m_conv_b, m_conv_ln_g, m_conv_ln_b, m_sgu_w, m_sgu_b, m_pl_norm_g, row(m_final_g), m_conv_w,
         m_sgu_ln_g, m_sgu_ln_b],
        [v_norm_g, v_conv_b, v_conv_ln_g, v_conv_ln_b, v_sgu_w, v_sgu_b, v_pl_norm_g, row(v_final_g), v_conv_w,
         v_sgu_ln_g, v_sgu_ln_b],
        r_rep[8])
    loss = o_small[4][0, 0]

    def leaf(kind):
        sm = o_small[kind]
        return [sm[0], o_w_in[kind], o_w_out[kind], sm[8], sm[1], sm[2], sm[3], sm[9], sm[10], sm[4], sm[5], sm[6],
                o_gate[kind], o_projw[kind], sm[7].reshape(final_g.shape)]

    return (loss, grad_x, *leaf(0), *leaf(1), *leaf(2), *leaf(3))
```

```python
import math

import jax
import jax.numpy as jnp
from jax import lax
from jax.experimental import pallas as pl
from jax.experimental.pallas import tpu as pltpu

F32 = jnp.float32
MXU_DTYPE = jnp.bfloat16
WIRE_DTYPE = jnp.bfloat16

EPS = 1e-6
CONV_K = 31
CHUNK = 128
GROUPS = 8
HALO = 32
N_DEV = 8
DEPTH = 4

ADAM_LR = 0.001
ADAM_B1 = 0.9
ADAM_B2 = 0.999
ADAM_EPS = 1e-08
ADAM_WD = 0.01
ADAM_STEP = 10

TM_IN = 512
TM_MIX = 256
TM_OUT = 512
FUSE_SB = 256
CONV_RC = 64
CONV_CC = 128
DCW_CC = 128
DCW_RC = 32
DCW_CHUNKS = 11
PACK_COLS = 1024

MESH_ID = pl.DeviceIdType.MESH
INV_SQRT2 = 1.0 / math.sqrt(2.0)
INV_SQRT_2PI = 1.0 / math.sqrt(2.0 * math.pi)


def _params(n_grid, vmem_mb):
    return pltpu.CompilerParams(dimension_semantics=("arbitrary",) * n_grid, vmem_limit_bytes=vmem_mb << 20)


def _whole(shape):
    nd = len(shape)
    return pl.BlockSpec(shape, lambda *_: (0,) * nd, pipeline_mode=pl.Buffered(1))


def _acc_out(shape):
    nd = len(shape)
    return pl.BlockSpec(shape, lambda *_: (0,) * nd)


def _dot(a, b):
    return jnp.dot(a.astype(MXU_DTYPE), b.astype(MXU_DTYPE), preferred_element_type=F32)


def _dot_nt(a, b):
    return lax.dot_general(a.astype(MXU_DTYPE), b.astype(MXU_DTYPE), (((1,), (1,)), ((), ())),
                           preferred_element_type=F32)


def _dot_tn(a, b):
    return lax.dot_general(a.astype(MXU_DTYPE), b.astype(MXU_DTYPE), (((0,), (0,)), ((), ())),
                           preferred_element_type=F32)


def _sigmoid(x):
    return jax.nn.sigmoid(x)


def _rms_rstd(x):
    return lax.rsqrt(jnp.mean(x * x, axis=-1, keepdims=True) + EPS)


def _rms_bwd(dy, x, rstd, g):
    gy = dy * g
    xr = x * rstd
    dx = rstd * (gy - xr * jnp.mean(gy * xr, axis=-1, keepdims=True))
    dg = jnp.sum(dy * xr, axis=0, keepdims=True)
    return dx, dg


def _ln_stats(x):
    mu = jnp.mean(x, axis=-1, keepdims=True)
    xc = x - mu
    var = jnp.mean(xc * xc, axis=-1, keepdims=True)
    rstd = lax.rsqrt(var + EPS)
    return xc * rstd, rstd


def _ln_bwd(dxhat, xhat, rstd):
    return rstd * (dxhat - jnp.mean(dxhat, axis=-1, keepdims=True)
                   - xhat * jnp.mean(dxhat * xhat, axis=-1, keepdims=True))


def _silu_grad(x, s):
    return s * (1.0 + x * (1.0 - s))


def _tril_mask():
    r = lax.broadcasted_iota(jnp.int32, (CHUNK, CHUNK), 0)
    c = lax.broadcasted_iota(jnp.int32, (CHUNK, CHUNK), 1)
    return r >= c


def _conv_weights_to_sublanes(w_ref, w8_ref):
    for k in range(CONV_K):
        w8_ref[k] = jnp.broadcast_to(w_ref[k:k + 1, :], w8_ref.shape[1:])


def _conv_apply(src_ref, w8_ref, zs_ref, base, tm, e, flip, emit):
    def row_block(i, carry):
        r0 = pl.multiple_of(i * CONV_RC, CONV_RC)
        for c0 in range(0, e, CONV_CC):
            cols = slice(c0, c0 + CONV_CC)
            acc = None
            for s in range(8):
                nrows = CONV_RC if s == 0 else CONV_RC + 8
                taps = [k for k in range(CONV_K) if (base + k) % 8 == s]
                off0 = base + taps[0] - s
                span = nrows + 8 * (len(taps) - 1)
                window = src_ref[pl.ds(r0 + off0, span), cols].reshape(span // 8, 8, CONV_CC)
                z = None
                for m, k in enumerate(taps):
                    wk = (CONV_K - 1 - k) if flip else k
                    term = w8_ref[wk, :, cols][None] * window[m:m + nrows // 8]
                    z = term if z is None else z + term
                z = z.reshape(nrows, CONV_CC)
                if s == 0:
                    acc = z
                else:
                    zs_ref[s - 1, pl.ds(0, nrows), :] = z
                    acc = acc + zs_ref[s - 1, pl.ds(s, CONV_RC), :]
            emit(r0, c0, acc)
        return carry

    lax.fori_loop(0, tm // CONV_RC, row_block, 0)


def _mesh_pos():
    return lax.axis_index("x"), lax.axis_index("y"), lax.axis_index("c")


def _slot(px, py, pc):
    return 4 * px + 2 * py + pc


def _peers(x, y, c):
    return [((1 - x) if (k & 4) else x, (1 - y) if (k & 2) else y, (1 - c) if (k & 1) else c)
            for k in range(1, N_DEV)]


HBM_SPEC = pl.BlockSpec(memory_space=pltpu.HBM)
SEM_SPEC = pl.BlockSpec(memory_space=pltpu.SEMAPHORE)
SIDE_EFFECT = pltpu.SideEffectType.DATAFLOW_SIDE_EFFECTING


def _exchange_copy(src_refs, land_refs, send_sems, recv_sems, i, k, peer, scatter, me):
    slot = _slot(*peer)
    return pltpu.make_async_remote_copy(
        src_ref=src_refs[i].at[slot] if scatter else src_refs[i],
        dst_ref=land_refs[i].at[me if me is not None else slot],
        send_sem=send_sems.at[i * 7 + k], recv_sem=recv_sems.at[i * 7 + k],
        device_id=peer, device_id_type=MESH_ID)


def _exchange_start(srcs, lands, after, scatter, name):
    n = len(srcs)

    def body(*refs):
        src_refs, land_refs = refs[:n], refs[n:2 * n]
        send_sems, recv_sems, token = refs[2 * n + 1], refs[2 * n + 2], refs[-1]
        x, y, c = _mesh_pos()
        me = _slot(x, y, c)
        for i in range(n):
            for k, peer in enumerate(_peers(x, y, c)):
                _exchange_copy(src_refs, land_refs, send_sems, recv_sems, i, k, peer, scatter, me).start()
        token[...] = jnp.zeros_like(token)

    arrays = list(srcs) + list(lands)
    outs = pl.pallas_call(
        body, name=name,
        out_shape=(pltpu.SemaphoreType.DMA((7 * n,)), pltpu.SemaphoreType.DMA((7 * n,)),
                   *[pltpu.HBM(a.shape, a.dtype) for a in lands], jax.ShapeDtypeStruct((8, 128), F32)),
        in_specs=[HBM_SPEC] * (2 * n) + [pl.BlockSpec(memory_space=pl.ANY)],
        out_specs=(SEM_SPEC, SEM_SPEC, *[HBM_SPEC] * n, pl.BlockSpec(memory_space=pltpu.VMEM)),
        input_output_aliases={n + i: 2 + i for i in range(n)},
        compiler_params=pltpu.CompilerParams(has_side_effects=SIDE_EFFECT),
    )(*arrays, after)
    return outs[0], outs[1], list(srcs), list(outs[2:2 + n]), outs[-1]


def _exchange_wait(send_sems, recv_sems, srcs, lands, after, scatter, name):
    n = len(srcs)

    def body(*refs):
        src_refs, land_refs = refs[:n], refs[n:2 * n]
        send, recv = refs[2 * n], refs[2 * n + 1]
        x, y, c = _mesh_pos()
        for i in range(n):
            for k, peer in enumerate(_peers(x, y, c)):
                cp = _exchange_copy(src_refs, land_refs, send, recv, i, k, peer, scatter, None)
                cp.wait_send()
                cp.wait_recv()

    arrays = list(srcs) + list(lands)
    outs = pl.pallas_call(
        body, name=name,
        out_shape=tuple(pltpu.HBM(a.shape, a.dtype) for a in lands),
        in_specs=[HBM_SPEC] * (2 * n) + [SEM_SPEC, SEM_SPEC, pl.BlockSpec(memory_space=pl.ANY)],
        out_specs=tuple([HBM_SPEC] * n),
        input_output_aliases={n + i: i for i in range(n)},
        compiler_params=pltpu.CompilerParams(has_side_effects=SIDE_EFFECT),
    )(*arrays, send_sems, recv_sems, after)
    return list(outs)


def _place_own(parts, scatter, name):
    n = len(parts)
    me = jnp.reshape(_slot(*_mesh_pos()), (1,)).astype(jnp.int32)

    def body(me_ref, *refs):
        for i in range(n):
            refs[n + i][0] = refs[i][0] if scatter else refs[i][...]

    def slot_spec(shape):
        rest = len(shape)
        return pl.BlockSpec((1,) + tuple(shape), lambda i, me_ref: (me_ref[0],) + (0,) * rest)

    def whole_spec(shape):
        nd = len(shape)
        return pl.BlockSpec(tuple(shape), lambda i, me_ref: (0,) * nd)

    blocks = [a.shape[1:] if scatter else a.shape for a in parts]
    return pl.pallas_call(
        body, name=name,
        grid_spec=pltpu.PrefetchScalarGridSpec(
            num_scalar_prefetch=1, grid=(1,),
            in_specs=[slot_spec(b) if scatter else whole_spec(b) for b in blocks],
            out_specs=[slot_spec(b) for b in blocks]),
        out_shape=[pltpu.HBM((N_DEV,) + tuple(b), a.dtype) for a, b in zip(parts, blocks)],
        compiler_params=_params(1, 32),
    )(me, *parts)


def _all_gather(items, name):
    n = len(items)

    def body(*refs):
        in_refs, out_refs = refs[:n], refs[n:2 * n]
        send_sems, recv_sems, local_sems = refs[2 * n:]
        x, y, c = _mesh_pos()
        me, sibling = (x, y, c), (x, y, 1 - c)
        chips = [(1 - x, y), (x, 1 - y), (1 - x, 1 - y)]

        def copy(i, k, block, to, src=None):
            dst = out_refs[i].at[_slot(*block)]
            return pltpu.make_async_remote_copy(
                src_ref=dst if src is None else src, dst_ref=dst,
                send_sem=send_sems.at[i * 7 + k], recv_sem=recv_sems.at[i * 7 + k],
                device_id=to, device_id_type=MESH_ID)

        mine = [pltpu.make_async_copy(in_refs[i], out_refs[i].at[_slot(*me)], local_sems.at[i]) for i in range(n)]
        for cp in mine:
            cp.start()
        first = []
        for i in range(n):
            first.append(copy(i, 0, me, sibling, src=in_refs[i]))
            for j, chip in enumerate(chips):
                first.append(copy(i, 1 + j, me, (*chip, c), src=in_refs[i]))
        for cp in first:
            cp.start()
        passed = []
        for j, chip in enumerate(chips):
            for i in range(n):
                copy(i, 1 + j, (*chip, c), me).wait_recv()
                fwd = copy(i, 4 + j, (*chip, c), sibling)
                fwd.start()
                passed.append(fwd)
        for i in range(n):
            copy(i, 0, sibling, me).wait_recv()
            for j, chip in enumerate(chips):
                copy(i, 4 + j, (*chip, 1 - c), me).wait_recv()
        for cp in first + passed:
            cp.wait_send()
        for cp in mine:
            cp.wait()

    any_spec = pl.BlockSpec(memory_space=pl.ANY)
    return pl.pallas_call(
        body, name=name,
        out_shape=[jax.ShapeDtypeStruct((N_DEV,) + a.shape, a.dtype) for a in items],
        in_specs=[any_spec] * n, out_specs=[any_spec] * n,
        scratch_shapes=[pltpu.SemaphoreType.DMA((7 * n,)), pltpu.SemaphoreType.DMA((7 * n,)),
                        pltpu.SemaphoreType.DMA((n,))],
    )(*items)


def _layer_fwd(x, p_all, layer, g, w_blk, w_out, plg, gate_w, proj_w, seq, conv=None, sgu=None):
    t, d = x.shape
    nb, _, bn = w_blk.shape
    e = w_out.shape[0]
    pd = p_all.shape[-1]
    tm = TM_MIX
    nt = seq // tm
    is_conv = conv is not None
    mixer_args = conv if is_conv else sgu
    n_mix = len(mixer_args)

    def body(*refs):
        x_ref, p_ref, g_ref, w_ref, wo_ref, plg_ref, gw_ref, pw_ref = refs[:8]
        mix = refs[8:8 + n_mix]
        outs = refs[8 + n_mix:]
        if is_conv:
            cw_ref, cb_ref, lg_ref, lb_ref = mix
            h_ref, proj_ref, y1_ref, x1_ref, gate_ref, x2_ref, y0s, zs, w8 = outs

            @pl.when(lax.rem(pl.program_id(0), nt) == 0)
            def _():
                y0s[pl.ds(0, HALO), :] = jnp.zeros((HALO, e), F32)
            _conv_weights_to_sublanes(cw_ref, w8)
        else:
            lg_ref, lb_ref, sw_ref, sbt_ref = mix
            h_ref, proj_ref, x1_ref, gate_ref, x2_ref, mixed_s = outs

        for sb in range(tm // FUSE_SB):
            rows = pl.ds(sb * FUSE_SB, FUSE_SB)
            xv = x_ref[rows, :]
            hv = (xv * _rms_rstd(xv) * g_ref[...]).astype(MXU_DTYPE)
            h_ref[rows, :] = hv
            for j in range(nb):
                proj_ref[rows, j * bn:(j + 1) * bn] = jnp.dot(hv, w_ref[j], preferred_element_type=F32)
            if is_conv:
                y0s[pl.ds(HALO + sb * FUSE_SB, FUSE_SB), :] = proj_ref[rows, 0:e] * _sigmoid(proj_ref[rows, e:2 * e])

        if is_conv:
            def emit(r0, c0, acc):
                y1_ref[pl.ds(r0, CONV_RC), c0:c0 + CONV_CC] = acc + cb_ref[:, c0:c0 + CONV_CC]
            _conv_apply(y0s, w8, zs, HALO - (CONV_K - 1), tm, e, False, emit)

        for sb in range(tm // FUSE_SB):
            rows = pl.ds(sb * FUSE_SB, FUSE_SB)
            if is_conv:
                xhat, _ = _ln_stats(y1_ref[rows, :])
                y2 = xhat * lg_ref[...] + lb_ref[...]
                y = y2 * _sigmoid(y2)
            else:
                _, _, u, _, _, _ = _sgu_parts(proj_ref[rows, 0:e], proj_ref[rows, e:2 * e], lg_ref[...], lb_ref[...],
                                              sw_ref, sbt_ref, mixed_s, FUSE_SB, e)
                y = u * mixed_s[...]
            z = proj_ref[rows, 2 * e:3 * e]
            q = (y * (z * _sigmoid(z))).astype(MXU_DTYPE)
            x1 = x_ref[rows, :] + jnp.dot(q, wo_ref[...], preferred_element_type=F32)
            x1_ref[rows, :] = x1
            rn = x1 * _rms_rstd(x1) * plg_ref[...]
            gate = _sigmoid(_dot(rn, gw_ref[...]))
            gate_ref[rows, :] = gate
            x2_ref[rows, :] = x1 + gate * _dot(p_ref[0, rows, :], pw_ref[...])

        if is_conv:
            y0s[pl.ds(0, HALO), :] = y0s[pl.ds(tm, HALO), :]

    row = lambda w: pl.BlockSpec((tm, w), lambda i: (i, 0))
    f32 = lambda w: jax.ShapeDtypeStruct((t, w), F32)
    out_shape = ([jax.ShapeDtypeStruct((t, d), MXU_DTYPE), f32(3 * e)] + ([f32(e)] if is_conv else [])
                 + [f32(d), f32(d), f32(d)])
    out_specs = [row(d), row(3 * e)] + ([row(e)] if is_conv else []) + [row(d), row(d), row(d)]
    scratch = ([pltpu.VMEM((tm + HALO, e), F32), pltpu.VMEM((7, CONV_RC + 8, CONV_CC), F32),
                pltpu.VMEM((CONV_K, 8, e), F32)] if is_conv else [pltpu.VMEM((FUSE_SB, e), F32)])
    return pl.pallas_call(
        body, name="layer_fwd_conv" if is_conv else "layer_fwd_sgu", grid=(t // tm,),
        in_specs=[row(d), pl.BlockSpec((1, tm, pd), lambda i: (layer, i, 0)), _whole((1, d)), _whole(w_blk.shape),
                  _whole((e, d)), _whole((1, d)),
                  _whole((d, d)), _whole((pd, d))] + [_whole(a.shape) for a in mixer_args],
        out_specs=out_specs, out_shape=out_shape, scratch_shapes=scratch,
        compiler_params=_params(1, 60),
    )(x, p_all, g, w_blk, w_out, plg, gate_w, proj_w, *mixer_args)


def _sgu_parts(a, b, lg, lb, sw_ref, sbt_ref, mixed_s, tm, e):
    eg = e // GROUPS
    ea = lax.erf(a * INV_SQRT2)
    eb = lax.erf(b * INV_SQRT2)
    u = 0.5 * a * (1.0 + ea)
    v0 = 0.5 * b * (1.0 + eb)
    xhat, rstd = _ln_stats(v0)
    v = (xhat * lg + lb).astype(MXU_DTYPE)
    mask = _tril_mask()
    for g in range(GROUPS):
        wt = jnp.where(mask, sw_ref[g], 0.0).astype(MXU_DTYPE)
        bcol = sbt_ref[:, g:g + 1]
        for ch in range(tm // CHUNK):
            rows = slice(ch * CHUNK, (ch + 1) * CHUNK)
            cols = slice(g * eg, (g + 1) * eg)
            mixed_s[rows, cols] = jnp.dot(wt, v[rows, cols], preferred_element_type=F32) + bcol
    return ea, eb, u, xhat, rstd, v


def _loss_head(xf, fg, tgt):
    t, d = xf.shape
    tm = TM_OUT
    nsteps = t // tm

    def body(x_ref, g_ref, t_ref, loss_ref, dx_ref, dg_ref, sq_s):
        i = pl.program_id(0)

        @pl.when(i == 0)
        def _():
            sq_s[...] = jnp.zeros_like(sq_s)
            dg_ref[...] = jnp.zeros_like(dg_ref)
        x = x_ref[...]
        rstd = _rms_rstd(x)
        err = x * rstd * g_ref[...] - t_ref[...]
        sq_s[...] += jnp.sum(err * err, axis=0, keepdims=True)
        dx, dg = _rms_bwd(err * (1.0 / d), x, rstd, g_ref[...])
        dx_ref[...] = dx
        dg_ref[...] += dg

        @pl.when(i == nsteps - 1)
        def _():
            loss_ref[...] = jnp.sum(sq_s[...], axis=1, keepdims=True) * (0.5 / d)

    row = pl.BlockSpec((tm, d), lambda i: (i, 0))
    return pl.pallas_call(
        body, name="loss_head", grid=(nsteps,),
        in_specs=[row, _whole((1, d)), row],
        out_specs=[_acc_out((1, 1)), row, _acc_out((1, d))],
        out_shape=[jax.ShapeDtypeStruct((1, 1), F32), jax.ShapeDtypeStruct((t, d), F32),
                   jax.ShapeDtypeStruct((1, d), F32)],
        scratch_shapes=[pltpu.VMEM((1, d), F32)],
        compiler_params=_params(1, 32),
    )(xf, fg, tgt)


def _ple_bwd(dx2, x1, gate, p_all, layer, plg, gate_w, proj_w):
    t, d = x1.shape
    pd = p_all.shape[-1]
    tm = TM_OUT
    nsteps = t // tm
    bn = d // N_DEV

    def body(dx2_ref, x1_ref, gate_ref, p_ref, plg_ref, gw_ref, pw_ref, dx1_ref, dgw_ref, dpw_ref, dplg_ref,
             gw_acc, pw_acc):
        i = pl.program_id(0)

        @pl.when(i == 0)
        def _():
            gw_acc[...] = jnp.zeros_like(gw_acc)
            pw_acc[...] = jnp.zeros_like(pw_acc)
            dplg_ref[...] = jnp.zeros_like(dplg_ref)
        dx2 = dx2_ref[...]
        x1 = x1_ref[...]
        plg = plg_ref[...]
        rstd = _rms_rstd(x1)
        rn = (x1 * rstd * plg).astype(MXU_DTYPE)
        gate = gate_ref[...]
        p_b = p_ref[0].astype(MXU_DTYPE)
        pp = jnp.dot(p_b, pw_ref[...], preferred_element_type=F32)
        dpp = (dx2 * gate).astype(MXU_DTYPE)
        dgpre = (dx2 * pp * gate * (1.0 - gate)).astype(MXU_DTYPE)
        pw_acc[...] += _dot_tn(p_b, dpp)
        gw_acc[...] += _dot_tn(rn, dgpre)
        drn = _dot_nt(dgpre, gw_ref[...])
        dx, dg = _rms_bwd(drn, x1, rstd, plg)
        dx1_ref[...] = dx2 + dx
        dplg_ref[...] += dg

        @pl.when(i == nsteps - 1)
        def _():
            dgw_ref[...] = gw_acc[...].astype(dgw_ref.dtype)
            for j in range(N_DEV):
                dpw_ref[j] = pw_acc[:, j * bn:(j + 1) * bn].astype(dpw_ref.dtype)

    row = lambda w: pl.BlockSpec((tm, w), lambda i: (i, 0))
    return pl.pallas_call(
        body, name="ple_bwd", grid=(nsteps,),
        in_specs=[row(d), row(d), row(d), pl.BlockSpec((1, tm, pd), lambda i: (layer, i, 0)), _whole((1, d)),
                  _whole((d, d)), _whole((pd, d))],
        out_specs=[row(d), _acc_out((d, d)), _acc_out((N_DEV, pd, bn)), _acc_out((1, d))],
        out_shape=[jax.ShapeDtypeStruct((t, d), F32), pltpu.HBM((d, d), WIRE_DTYPE),
                   pltpu.HBM((N_DEV, pd, bn), WIRE_DTYPE), jax.ShapeDtypeStruct((1, d), F32)],
        scratch_shapes=[pltpu.VMEM((d, d), F32), pltpu.VMEM((pd, d), F32)],
        compiler_params=_params(1, 48),
    )(dx2, x1, gate, p_all, plg, gate_w, proj_w)


def _outproj_conv_bwd(dx1, y1, proj, w_out, lg, lb):
    t, d = dx1.shape
    e = y1.shape[1]
    tm = TM_MIX
    nsteps = t // tm

    def body(dx1_ref, y1_ref, z_ref, wo_ref, lg_ref, lb_ref, dy1_ref, dz_ref, dwo_ref, dlg_ref, dlb_ref, dcb_ref,
             wo_acc):
        i = pl.program_id(0)

        @pl.when(i == 0)
        def _():
            wo_acc[...] = jnp.zeros_like(wo_acc)
            dlg_ref[...] = jnp.zeros_like(dlg_ref)
            dlb_ref[...] = jnp.zeros_like(dlb_ref)
            dcb_ref[...] = jnp.zeros_like(dcb_ref)
        xhat, rstd = _ln_stats(y1_ref[...])
        lg = lg_ref[...]
        y2 = xhat * lg + lb_ref[...]
        s2 = _sigmoid(y2)
        y = y2 * s2
        z = z_ref[...]
        s = _sigmoid(z)
        sz = z * s
        dx1 = dx1_ref[...].astype(MXU_DTYPE)
        wo_acc[...] += _dot_tn((y * sz).astype(MXU_DTYPE), dx1)
        dq = _dot_nt(dx1, wo_ref[...])
        dz_ref[...] = (dq * y * _silu_grad(z, s)).astype(dz_ref.dtype)
        dy2 = dq * sz * _silu_grad(y2, s2)
        dlg_ref[...] += jnp.sum(dy2 * xhat, axis=0, keepdims=True)
        dlb_ref[...] += jnp.sum(dy2, axis=0, keepdims=True)
        dy1 = _ln_bwd(dy2 * lg, xhat, rstd)
        dy1_ref[...] = dy1
        dcb_ref[...] += jnp.sum(dy1, axis=0, keepdims=True)

        @pl.when(i == nsteps - 1)
        def _():
            dwo_ref[...] = wo_acc[...].astype(dwo_ref.dtype)

    row = lambda w: pl.BlockSpec((tm, w), lambda i: (i, 0))
    return pl.pallas_call(
        body, name="outproj_conv_bwd", grid=(nsteps,),
        in_specs=[row(d), row(e), pl.BlockSpec((tm, e), lambda i: (i, 2)), _whole((e, d)), _whole((1, e)),
                  _whole((1, e))],
        out_specs=[row(e), row(e), _acc_out((e, d)), _acc_out((1, e)), _acc_out((1, e)), _acc_out((1, e))],
        out_shape=[jax.ShapeDtypeStruct((t, e), F32), jax.ShapeDtypeStruct((t, e), MXU_DTYPE),
                   pltpu.HBM((e, d), WIRE_DTYPE)] + [jax.ShapeDtypeStruct((1, e), F32)] * 3,
        scratch_shapes=[pltpu.VMEM((e, d), F32)],
        compiler_params=_params(1, 56),
    )(dx1, y1, proj, w_out, lg, lb)


def _conv_bwd(dy1, proj, dz, cw, seq):
    t, e = dy1.shape
    tm = TM_MIX
    nt = seq // tm
    hb = tm // HALO
    n_halo_blocks = t // HALO

    def body(d_ref, dn_ref, a_ref, b_ref, ah_ref, bh_ref, dz_ref, cw_ref, dproj_ref, dcw_ref,
             y0s, d1s, zs, dsh, dcw8, w8):
        i = pl.program_id(0)
        pos = lax.rem(i, nt)

        @pl.when(i == 0)
        def _():
            dcw8[...] = jnp.zeros_like(dcw8)
        _conv_weights_to_sublanes(cw_ref, w8)
        a = a_ref[...]
        sb = _sigmoid(b_ref[...])
        y0s[pl.ds(HALO, tm), :] = a * sb
        d1s[pl.ds(0, tm), :] = d_ref[...]

        @pl.when(pos == 0)
        def _():
            y0s[pl.ds(0, HALO), :] = jnp.zeros((HALO, e), F32)

        @pl.when(pos != 0)
        def _():
            y0s[pl.ds(0, HALO), :] = ah_ref[...] * _sigmoid(bh_ref[...])

        @pl.when(pos == nt - 1)
        def _():
            d1s[pl.ds(tm, HALO), :] = jnp.zeros((HALO, e), F32)

        @pl.when(pos != nt - 1)
        def _():
            d1s[pl.ds(tm, HALO), :] = dn_ref[...]

        base = HALO - (CONV_K - 1)
        for c0 in range(0, e, DCW_CC):
            cols = slice(c0, c0 + DCW_CC)
            dcur = d_ref[:, cols]
            for s in range(1, 8):
                dsh[s - 1, pl.ds(0, 8), :] = jnp.zeros((8, DCW_CC), F32)
                dsh[s - 1, pl.ds(tm, 8), :] = jnp.zeros((8, DCW_CC), F32)
                dsh[s - 1, pl.ds(s, tm), :] = dcur
            for s in range(8):
                taps = [k for k in range(CONV_K) if (base + k) % 8 == s]
                off0 = base + taps[0] - s
                n, ch = (tm, DCW_RC) if s == 0 else (tm + 8, (tm + 8) // DCW_CHUNKS)
                sums = [None] * len(taps)
                for r in range(0, n, ch):
                    dch = d_ref[r:r + ch, cols] if s == 0 else dsh[s - 1, r:r + ch, :]
                    window = y0s[pl.ds(off0 + r, ch + 8 * (len(taps) - 1)), cols]
                    for m in range(len(taps)):
                        part = jnp.sum((dch * window[8 * m:8 * m + ch]).reshape(ch // 8, 8, DCW_CC), axis=0)
                        sums[m] = part if sums[m] is None else sums[m] + part
                for m, k in enumerate(taps):
                    dcw8[k, :, cols] += sums[m]

        def emit(r0, c0, dy0):
            rs, cs = pl.ds(r0, CONV_RC), slice(c0, c0 + CONV_CC)
            sbv = _sigmoid(b_ref[rs, cs])
            av = a_ref[rs, cs]
            dproj_ref[rs, c0:c0 + CONV_CC] = (dy0 * sbv).astype(dproj_ref.dtype)
            dproj_ref[rs, e + c0:e + c0 + CONV_CC] = (dy0 * av * sbv * (1.0 - sbv)).astype(dproj_ref.dtype)
        _conv_apply(d1s, w8, zs, 0, tm, e, True, emit)
        dproj_ref[:, 2 * e:3 * e] = dz_ref[...]

        @pl.when(i == t // tm - 1)
        def _():
            dcw_ref[...] = jnp.sum(dcw8[...], axis=1)

    tile = lambda col: pl.BlockSpec((tm, e), lambda i: (i, col))
    prev = lambda col: pl.BlockSpec((HALO, e), lambda i: (jnp.maximum(i * hb - 1, 0), col))
    nxt = pl.BlockSpec((HALO, e), lambda i: (jnp.minimum((i + 1) * hb, n_halo_blocks - 1), 0))
    return pl.pallas_call(
        body, name="conv_bwd", grid=(t // tm,),
        in_specs=[tile(0), nxt, tile(0), tile(1), prev(0), prev(1), tile(0), _whole(cw.shape)],
        out_specs=[pl.BlockSpec((tm, 3 * e), lambda i: (i, 0)), _acc_out(cw.shape)],
        out_shape=[jax.ShapeDtypeStruct((t, 3 * e), MXU_DTYPE), jax.ShapeDtypeStruct(cw.shape, F32)],
        scratch_shapes=[pltpu.VMEM((tm + HALO, e), F32), pltpu.VMEM((tm + HALO, e), F32),
                        pltpu.VMEM((7, CONV_RC + 8, CONV_CC), F32), pltpu.VMEM((7, tm + 8, DCW_CC), F32),
                        pltpu.VMEM((CONV_K, 8, e), F32), pltpu.VMEM((CONV_K, 8, e), F32)],
        compiler_params=_params(1, 56),
    )(dy1, dy1, proj, proj, proj, proj, dz, cw)


def _outproj_sgu_bwd(dx1, proj, w_out, lg, lb, sw, swt, sbt):
    t, d = dx1.shape
    e = w_out.shape[0]
    eg = e // GROUPS
    tm = TM_MIX
    nsteps = t // tm

    def body(dx1_ref, a_ref, b_ref, z_ref, wo_ref, lg_ref, lb_ref, sw_ref, swt_ref, sbt_ref,
             dproj_ref, dwo_ref, dsw_ref, dsb_ref, dlg_ref, dlb_ref, mixed_s, dv_s, sb_acc, wo_acc):
        i = pl.program_id(0)

        @pl.when(i == 0)
        def _():
            wo_acc[...] = jnp.zeros_like(wo_acc)
            dsw_ref[...] = jnp.zeros_like(dsw_ref)
            sb_acc[...] = jnp.zeros_like(sb_acc)
            dlg_ref[...] = jnp.zeros_like(dlg_ref)
            dlb_ref[...] = jnp.zeros_like(dlb_ref)
        a = a_ref[...]
        b = b_ref[...]
        lg = lg_ref[...]
        ea, eb, u, xhat, rstd, v = _sgu_parts(a, b, lg, lb_ref[...], sw_ref, sbt_ref, mixed_s, tm, e)
        mixed = mixed_s[...]
        y = u * mixed
        z = z_ref[...]
        s = _sigmoid(z)
        sz = z * s
        dx1 = dx1_ref[...].astype(MXU_DTYPE)
        wo_acc[...] += _dot_tn((y * sz).astype(MXU_DTYPE), dx1)
        dq = _dot_nt(dx1, wo_ref[...])
        dproj_ref[:, 2 * e:3 * e] = (dq * y * _silu_grad(z, s)).astype(dproj_ref.dtype)
        dy = dq * sz
        du = dy * mixed
        dmixed = (dy * u).astype(MXU_DTYPE)
        mask = _tril_mask()
        mask_t = (lax.broadcasted_iota(jnp.int32, (CHUNK, CHUNK), 0)
                  <= lax.broadcasted_iota(jnp.int32, (CHUNK, CHUNK), 1))
        ones = jnp.ones((8, eg), MXU_DTYPE)
        for g in range(GROUPS):
            wtt = jnp.where(mask_t, swt_ref[g], 0.0).astype(MXU_DTYPE)
            cols = slice(g * eg, (g + 1) * eg)
            for ch in range(tm // CHUNK):
                rows = slice(ch * CHUNK, (ch + 1) * CHUNK)
                dm = dmixed[rows, cols]
                dv_s[rows, cols] = jnp.dot(wtt, dm, preferred_element_type=F32)
                dsw_ref[g] += _dot_nt(dm, v[rows, cols])
                sb_acc[g] += _dot_nt(ones, dm)
        dv = dv_s[...]
        dlg_ref[...] += jnp.sum(dv * xhat, axis=0, keepdims=True)
        dlb_ref[...] += jnp.sum(dv, axis=0, keepdims=True)
        dv0 = _ln_bwd(dv * lg, xhat, rstd)
        pdf_a = jnp.exp(-0.5 * a * a) * INV_SQRT_2PI
        pdf_b = jnp.exp(-0.5 * b * b) * INV_SQRT_2PI
        dproj_ref[:, 0:e] = (du * (0.5 * (1.0 + ea) + a * pdf_a)).astype(dproj_ref.dtype)
        dproj_ref[:, e:2 * e] = (dv0 * (0.5 * (1.0 + eb) + b * pdf_b)).astype(dproj_ref.dtype)

        @pl.when(i == nsteps - 1)
        def _():
            dwo_ref[...] = wo_acc[...].astype(dwo_ref.dtype)
            for g in range(GROUPS):
                dsw_ref[g] = jnp.where(mask, dsw_ref[g], 0.0)
                dsb_ref[g:g + 1, :] = sb_acc[g, 0:1, :]

    tile = lambda col: pl.BlockSpec((tm, e), lambda i: (i, col))
    return pl.pallas_call(
        body, name="outproj_sgu_bwd", grid=(nsteps,),
        in_specs=[pl.BlockSpec((tm, d), lambda i: (i, 0)), tile(0), tile(1), tile(2), _whole((e, d)), _whole((1, e)),
                  _whole((1, e)), _whole(sw.shape), _whole(swt.shape), _whole(sbt.shape)],
        out_specs=[pl.BlockSpec((tm, 3 * e), lambda i: (i, 0)), _acc_out((e, d)), _acc_out(sw.shape),
                   _acc_out((GROUPS, CHUNK)), _acc_out((1, e)), _acc_out((1, e))],
        out_shape=[jax.ShapeDtypeStruct((t, 3 * e), MXU_DTYPE), pltpu.HBM((e, d), WIRE_DTYPE),
                   jax.ShapeDtypeStruct(sw.shape, F32), jax.ShapeDtypeStruct((GROUPS, CHUNK), F32),
                   jax.ShapeDtypeStruct((1, e), F32), jax.ShapeDtypeStruct((1, e), F32)],
        scratch_shapes=[pltpu.VMEM((tm, e), F32), pltpu.VMEM((tm, e), F32), pltpu.VMEM((GROUPS, 8, CHUNK), F32),
                        pltpu.VMEM((e, d), F32)],
        compiler_params=_params(1, 60),
    )(dx1, proj, proj, proj, w_out, lg, lb, sw, swt, sbt)


def _inproj_bwd_x(dproj, w_blk, dx1, x, g):
    t, d = x.shape
    nb, _, bn = w_blk.shape
    tm = TM_IN

    def body(dp_ref, w_ref, dx1_ref, x_ref, g_ref, dx_ref, dg_ref):
        @pl.when(pl.program_id(0) == 0)
        def _():
            dg_ref[...] = jnp.zeros_like(dg_ref)
        dh = None
        for j in range(nb):
            term = _dot_nt(dp_ref[:, j * bn:(j + 1) * bn], w_ref[j])
            dh = term if dh is None else dh + term
        xv = x_ref[...]
        dx, dg = _rms_bwd(dh, xv, _rms_rstd(xv), g_ref[...])
        dx_ref[...] = dx1_ref[...] + dx
        dg_ref[...] += dg

    row = lambda w: pl.BlockSpec((tm, w), lambda i: (i, 0))
    return pl.pallas_call(
        body, name="inproj_bwd_x", grid=(t // tm,),
        in_specs=[row(nb * bn), _whole(w_blk.shape), row(d), row(d), _whole((1, d))],
        out_specs=[row(d), _acc_out((1, d))],
        out_shape=[jax.ShapeDtypeStruct((t, d), F32), jax.ShapeDtypeStruct((1, d), F32)],
        compiler_params=_params(1, 56),
    )(dproj, w_blk, dx1, x, g)


def _inproj_bwd_w(h, dproj):
    t, d = h.shape
    bn = dproj.shape[1] // N_DEV
    tm = TM_IN
    nsteps = t // tm

    nh = 2
    per = N_DEV // nh

    def body(h_ref, dp_ref, dw_ref, acc):
        i = pl.program_id(1)

        @pl.when(i == 0)
        def _():
            acc[...] = jnp.zeros_like(acc)
        hv = h_ref[...]
        for jj in range(per):
            acc[jj] += _dot_tn(hv, dp_ref[:, jj * bn:(jj + 1) * bn])

        @pl.when(i == nsteps - 1)
        def _():
            dw_ref[...] = acc[...].astype(dw_ref.dtype)

    return pl.pallas_call(
        body, name="inproj_bwd_w", grid=(nh, nsteps),
        in_specs=[pl.BlockSpec((tm, d), lambda hh, i: (i, 0)), pl.BlockSpec((tm, per * bn), lambda hh, i: (i, hh))],
        out_specs=[pl.BlockSpec((per, d, bn), lambda hh, i: (hh, 0, 0))],
        out_shape=[pltpu.HBM((N_DEV, d, bn), WIRE_DTYPE)],
        scratch_shapes=[pltpu.VMEM((per, d, bn), F32)],
        compiler_params=_params(2, 56),
    )(h, dproj)[0]


def _adamw(parts, w, m, v):
    nl, r, c = w.shape
    tr = r
    for cand in (512, 256, 128, 64, 32, 16, 8):
        if r % cand == 0 and cand * c * 4 <= (1 << 19):
            tr = cand
            break
    bc1 = 1.0 - ADAM_B1 ** ADAM_STEP
    bc2 = 1.0 - ADAM_B2 ** ADAM_STEP

    def body(*refs):
        p_refs = refs[:nl]
        w_ref, m_ref, v_ref, g_ref, d_ref, nm_ref, nv_ref = refs[nl:]

        def update(p_ref):
            g = p_ref[0].astype(F32)
            for s in range(1, N_DEV):
                g = g + p_ref[s].astype(F32)
            nm = ADAM_B1 * m_ref[0] + (1.0 - ADAM_B1) * g
            nv = ADAM_B2 * v_ref[0] + (1.0 - ADAM_B2) * (g * g)
            g_ref[0] = g
            nm_ref[0] = nm
            nv_ref[0] = nv
            d_ref[0] = -ADAM_LR * ((nm / bc1) / (jnp.sqrt(nv / bc2) + ADAM_EPS) + ADAM_WD * w_ref[0])

        if nl == 1:
            update(p_refs[0])
        else:
            for kk in range(nl):
                pl.when(pl.program_id(0) == kk)(lambda kk=kk: update(p_refs[kk]))

    part_spec = lambda kk: pl.BlockSpec((N_DEV, tr, c), lambda l, i: (0, jnp.where(l == kk, i, 0), 0))
    row = pl.BlockSpec((1, tr, c), lambda l, i: (l, i, 0))
    return pl.pallas_call(
        body, name="adamw", grid=(nl, r // tr),
        in_specs=[part_spec(kk) for kk in range(nl)] + [row, row, row],
        out_specs=[row] * 4,
        out_shape=[jax.ShapeDtypeStruct((nl, r, c), F32)] * 4,
        compiler_params=_params(2, 48),
    )(*parts, w, m, v)


def _adamw_small(parts, ws, ms, vs, loss_parts):
    n = len(ws)
    bc1 = 1.0 - ADAM_B1 ** ADAM_STEP
    bc2 = 1.0 - ADAM_B2 ** ADAM_STEP

    def total(ref):
        acc = ref[0]
        for s in range(1, N_DEV):
            acc = acc + ref[s]
        return acc

    def body(*refs):
        p_refs, w_refs, m_refs, v_refs = refs[:n], refs[n:2 * n], refs[2 * n:3 * n], refs[3 * n:4 * n]
        outs = refs[4 * n + 1:]
        for i in range(n):
            g = total(p_refs[i])
            nm = ADAM_B1 * m_refs[i][...] + (1.0 - ADAM_B1) * g
            nv = ADAM_B2 * v_refs[i][...] + (1.0 - ADAM_B2) * (g * g)
            outs[i][...] = g
            outs[n + i][...] = -ADAM_LR * ((nm / bc1) / (jnp.sqrt(nv / bc2) + ADAM_EPS) + ADAM_WD * w_refs[i][...])
            outs[2 * n + i][...] = nm
            outs[3 * n + i][...] = nv
        outs[4 * n][...] = total(refs[4 * n])

    vmem = pl.BlockSpec(memory_space=pltpu.VMEM)
    shapes = [jax.ShapeDtypeStruct(w.shape, F32) for w in ws]
    outs = pl.pallas_call(
        body, name="adamw_small",
        in_specs=[vmem] * (4 * n + 1), out_specs=[vmem] * (4 * n + 1),
        out_shape=shapes * 4 + [jax.ShapeDtypeStruct((8, 128), F32)],
        compiler_params=pltpu.CompilerParams(vmem_limit_bytes=48 << 20),
    )(*parts, *ws, *ms, *vs, loss_parts)
    return outs[:n], outs[n:2 * n], outs[2 * n:3 * n], outs[3 * n:4 * n], outs[4 * n]


def _pack(arrays):
    flat = jnp.concatenate([a.reshape(-1).astype(F32) for a in arrays])
    unit = 8 * PACK_COLS
    padded = -(-flat.shape[0] // unit) * unit
    return jnp.pad(flat, (0, padded - flat.shape[0])).reshape(-1, PACK_COLS)


def _unshard_last(packed, shard_shapes):
    flat = packed.reshape(N_DEV, -1)
    out, off = [], 0
    for s in shard_shapes:
        n = math.prod(s)
        a = jnp.moveaxis(flat[:, off:off + n].reshape((N_DEV,) + tuple(s)), 0, -2)
        out.append(a.reshape(tuple(s[:-1]) + (N_DEV * s[-1],)))
        off += n
    return out


def kernel(x, p, norm_g, w_in, w_out, conv_w, conv_b, conv_ln_g, conv_ln_b, sgu_ln_g, sgu_ln_b, sgu_w, sgu_b, pl_norm_g, pl_gate_w, pl_proj_w, final_g, loss_target, m_norm_g, m_w_in, m_w_out, m_conv_w, m_conv_b, m_conv_ln_g, m_conv_ln_b, m_sgu_ln_g, m_sgu_ln_b, m_sgu_w, m_sgu_b, m_pl_norm_g, m_pl_gate_w, m_pl_proj_w, m_final_g, v_norm_g, v_w_in, v_w_out, v_conv_w, v_conv_b, v_conv_ln_g, v_conv_ln_b, v_sgu_ln_g, v_sgu_ln_b, v_sgu_w, v_sgu_b, v_pl_norm_g, v_pl_gate_w, v_pl_proj_w, v_final_g):
    bsz, seq, d = x.shape
    t = bsz * seq
    depth = w_in.shape[0]
    e = w_out.shape[1] * N_DEV
    pd = p.shape[-1]
    n_conv, n_sgu = conv_w.shape[0], sgu_ln_g.shape[0]

    small_shapes = [conv_w.shape, sgu_ln_g.shape, sgu_ln_b.shape]
    cast = lambda a: a.astype(MXU_DTYPE)
    first = [cast(w_in[0]), cast(w_out[0]), cast(pl_gate_w[0]), cast(pl_proj_w), _pack([conv_w, sgu_ln_g, sgu_ln_b])]
    later = [[cast(w_in[l]), cast(w_out[l]), cast(pl_gate_w[l])] for l in range(1, depth)]
    gathered = _all_gather(first, "gather_weights")
    gather_pending, gather_tokens = {}, 0.0
    for l in range(1, depth):
        lands = _place_own(later[l - 1], False, "place_own_weights")
        send, recv, srcs, lnds, token = _exchange_start(later[l - 1], lands, gathered[0], False, f"gather_start_{l}")
        gather_pending[l] = (send, recv, srcs, lnds)
        gather_tokens = gather_tokens + token[0, 0]
    w_in_g = {0: gathered[0]}
    w_out_g = {0: gathered[1].reshape(e, d)}
    gate_g = {0: gathered[2].reshape(d, d)}
    proj_g = jnp.transpose(gathered[3], (1, 2, 0, 3)).reshape(depth, pd, d)
    conv_w_g, sgu_ln_g_g, sgu_ln_b_g = _unshard_last(gathered[4], small_shapes)
    sgu_wt = jnp.swapaxes(sgu_w, -1, -2)
    sgu_bt = jnp.swapaxes(sgu_b, -1, -2)

    xs = [x.reshape(t, d)]
    p_all = p.reshape(depth, t, pd)
    saved = []
    for l in range(depth):
        j = l // 2
        if l == 0:
            g_l = norm_g[0:1] + gather_tokens
        else:
            g_l = norm_g[l:l + 1]
            got = _exchange_wait(*gather_pending.pop(l), xs[-1], False, f"gather_wait_{l}")
            w_in_g[l], w_out_g[l], gate_g[l] = got[0], got[1].reshape(e, d), got[2].reshape(d, d)
        common = (xs[-1], p_all, l, g_l, w_in_g[l], w_out_g[l], pl_norm_g[l:l + 1], gate_g[l], proj_g[l], seq)
        if l % 2 == 0:
            h, proj, y1, x1, gate, x2 = _layer_fwd(
                *common, conv=(conv_w_g[j], conv_b[j:j + 1], conv_ln_g[j:j + 1], conv_ln_b[j:j + 1]))
        else:
            y1 = None
            h, proj, x1, gate, x2 = _layer_fwd(
                *common, sgu=(sgu_ln_g_g[j:j + 1], sgu_ln_b_g[j:j + 1], sgu_w[j], sgu_bt[j]))
        saved.append((h, proj, y1, x1, gate))
        xs.append(x2)

    loss_part, dx, d_final_g = _loss_head(xs[-1], final_g.reshape(1, d), loss_target.reshape(t, d))

    d_norm_g, d_pl_norm_g = [None] * depth, [None] * depth
    scatter_pending = {}
    d_conv_w, d_conv_b, d_conv_ln_g, d_conv_ln_b = [None] * n_conv, [None] * n_conv, [None] * n_conv, [None] * n_conv
    d_sgu_ln_g, d_sgu_ln_b, d_sgu_w, d_sgu_b = [None] * n_sgu, [None] * n_sgu, [None] * n_sgu, [None] * n_sgu
    def scatter(parts, name):
        send, recv, srcs, lnds, token = _exchange_start(parts, _place_own(parts, True, "place_own_grads"), parts[0],
                                                        True, name)
        return (send, recv, srcs, lnds), token[0, 0]

    for l in reversed(range(depth)):
        j = l // 2
        h, proj, y1, x1, gate = saved[l]
        dx1, dgate_p, dprojw_p, d_pl_norm_g[l] = _ple_bwd(dx, x1, gate, p_all, l, pl_norm_g[l:l + 1], gate_g[l],
                                                         proj_g[l])
        if l % 2 == 0:
            dy1, dz, dw_out_p, d_conv_ln_g[j], d_conv_ln_b[j], d_conv_b[j] = _outproj_conv_bwd(
                dx1, y1, proj, w_out_g[l], conv_ln_g[j:j + 1], conv_ln_b[j:j + 1])
        else:
            dproj, dw_out_p, d_sgu_w[j], d_sgu_b[j], d_sgu_ln_g[j], d_sgu_ln_b[j] = _outproj_sgu_bwd(
                dx1, proj, w_out_g[l], sgu_ln_g_g[j:j + 1], sgu_ln_b_g[j:j + 1], sgu_w[j], sgu_wt[j], sgu_bt[j])
        early = [dw_out_p.reshape(N_DEV, e // N_DEV, d), dgate_p.reshape(N_DEV, d // N_DEV, d), dprojw_p]
        early_token = 0.0
        if l == 0:
            scatter_pending["0_early"], early_token = scatter(early, "scatter_start_0_early")
            early = []
        if l % 2 == 0:
            dproj, d_conv_w[j] = _conv_bwd(dy1, proj, dz, conv_w_g[j] + early_token, seq)
        scatter_pending[l], token = scatter([_inproj_bwd_w(h, dproj)] + early, f"scatter_start_{l}")
        dx, d_norm_g[l] = _inproj_bwd_x(dproj, w_in_g[l], dx1, xs[l], norm_g[l:l + 1] + token)
    grad_x = dx.reshape(bsz, seq, d)

    def own_eighths(full):
        return jnp.moveaxis(full.reshape(full.shape[:-1] + (N_DEV, full.shape[-1] // N_DEV)), -2, 0)

    small_parts = [own_eighths(jnp.stack(d_conv_w)), own_eighths(jnp.concatenate(d_sgu_ln_g, axis=0)),
                   own_eighths(jnp.concatenate(d_sgu_ln_b, axis=0))]
    rep_parts = [jnp.concatenate(d_norm_g, axis=0), jnp.concatenate(d_conv_b, axis=0),
                 jnp.concatenate(d_conv_ln_g, axis=0), jnp.concatenate(d_conv_ln_b, axis=0), jnp.stack(d_sgu_w),
                 jnp.stack(d_sgu_b), jnp.concatenate(d_pl_norm_g, axis=0), d_final_g,
                 jnp.broadcast_to(loss_part, (8, 128))]
    small_send, small_recv, small_srcs, small_lnds, small_token = _exchange_start(
        small_parts, _place_own(small_parts, True, "place_own_small"), grad_x, True, "scatter_small_start")
    rep_send, rep_recv, rep_srcs, rep_lnds, rep_token = _exchange_start(
        rep_parts, _place_own(rep_parts, False, "place_own_replicated"), grad_x, False, "gather_replicated_start")

    landed = {}
    for key in list(scatter_pending):
        landed[key] = _exchange_wait(*scatter_pending.pop(key), small_token + rep_token, True, f"scatter_wait_{key}")
    dw_in_l = [landed[l][0] for l in range(depth)]
    rest = [landed["0_early"]] + [landed[l][1:] for l in range(1, depth)]

    o_w_in = _adamw(dw_in_l, w_in, m_w_in, v_w_in)
    o_w_out = _adamw([r[0] for r in rest], w_out, m_w_out, v_w_out)
    o_gate = _adamw([r[1] for r in rest], pl_gate_w, m_pl_gate_w, v_pl_gate_w)
    o_projw = _adamw([r[2] for r in rest], pl_proj_w, m_pl_proj_w, v_pl_proj_w)
    r_small = _exchange_wait(small_send, small_recv, small_srcs, small_lnds, o_projw[1], True, "scatter_small_wait")
    r_rep = _exchange_wait(rep_send, rep_recv, rep_srcs, rep_lnds, o_w_in[1], False, "gather_replicated_wait")

    row = lambda a: a.reshape(1, -1)
    o_small = _adamw_small(
        r_rep[:8] + r_small,
        [norm_g, conv_b, conv_ln_g, conv_ln_b, sgu_w, sgu_b, pl_norm_g, row(final_g), conv_w, sgu_ln_g, sgu_ln_b],
        [m_norm_g, m_conv_b, m_conv_ln_g, m_conv_ln_b, m_sgu_w, m_sgu_b, m_pl_norm_g, row(m_final_g), m_conv_w,
         m_sgu_ln_g, m_sgu_ln_b],
        [v_norm_g, v_conv_b, v_conv_ln_g, v_conv_ln_b, v_sgu_w, v_sgu_b, v_pl_norm_g, row(v_final_g), v_conv_w,
         v_sgu_ln_g, v_sgu_ln_b],
        r_rep[8])
    loss = o_small[4][0, 0]

    def leaf(kind):
        sm = o_small[kind]
        return [sm[0], o_w_in[kind], o_w_out[kind], sm[8], sm[1], sm[2], sm[3], sm[9], sm[10], sm[4], sm[5], sm[6],
                o_gate[kind], o_projw[kind], sm[7].reshape(final_g.shape)]

    return (loss, grad_x, *leaf(0), *leaf(1), *leaf(2), *leaf(3))
```

```python
import math

import jax
import jax.numpy as jnp
from jax import lax
from jax.experimental import pallas as pl
from jax.experimental.pallas import tpu as pltpu

F32 = jnp.float32
MXU_DTYPE = jnp.bfloat16
WIRE_DTYPE = jnp.bfloat16

EPS = 1e-6
CONV_K = 31
CHUNK = 128
GROUPS = 8
HALO = 32
N_DEV = 8
DEPTH = 4

ADAM_LR = 0.001
ADAM_B1 = 0.9
ADAM_B2 = 0.999
ADAM_EPS = 1e-08
ADAM_WD = 0.01
ADAM_STEP = 10

TM_IN = 512
TM_MIX = 256
TM_OUT = 512
FUSE_SB = 256
CONV_RC = 64
CONV_CC = 128
DCW_CC = 128
DCW_RC = 16
DCW_CHUNKS = 33
PACK_COLS = 1024

MESH_ID = pl.DeviceIdType.MESH
INV_SQRT2 = 1.0 / math.sqrt(2.0)
INV_SQRT_2PI = 1.0 / math.sqrt(2.0 * math.pi)


def _params(n_grid, vmem_mb):
    return pltpu.CompilerParams(dimension_semantics=("arbitrary",) * n_grid, vmem_limit_bytes=vmem_mb << 20)


def _whole(shape):
    nd = len(shape)
    return pl.BlockSpec(shape, lambda *_: (0,) * nd, pipeline_mode=pl.Buffered(1))


def _acc_out(shape):
    nd = len(shape)
    return pl.BlockSpec(shape, lambda *_: (0,) * nd)


def _dot(a, b):
    return jnp.dot(a.astype(MXU_DTYPE), b.astype(MXU_DTYPE), preferred_element_type=F32)


def _dot_nt(a, b):
    return lax.dot_general(a.astype(MXU_DTYPE), b.astype(MXU_DTYPE), (((1,), (1,)), ((), ())),
                           preferred_element_type=F32)


def _dot_tn(a, b):
    return lax.dot_general(a.astype(MXU_DTYPE), b.astype(MXU_DTYPE), (((0,), (0,)), ((), ())),
                           preferred_element_type=F32)


def _sigmoid(x):
    return jax.nn.sigmoid(x)


def _rms_rstd(x):
    return lax.rsqrt(jnp.mean(x * x, axis=-1, keepdims=True) + EPS)


def _rms_bwd(dy, x, rstd, g):
    gy = dy * g
    xr = x * rstd
    dx = rstd * (gy - xr * jnp.mean(gy * xr, axis=-1, keepdims=True))
    dg = jnp.sum(dy * xr, axis=0, keepdims=True)
    return dx, dg


def _ln_stats(x):
    mu = jnp.mean(x, axis=-1, keepdims=True)
    xc = x - mu
    var = jnp.mean(xc * xc, axis=-1, keepdims=True)
    rstd = lax.rsqrt(var + EPS)
    return xc * rstd, rstd


def _ln_bwd(dxhat, xhat, rstd):
    return rstd * (dxhat - jnp.mean(dxhat, axis=-1, keepdims=True)
                   - xhat * jnp.mean(dxhat * xhat, axis=-1, keepdims=True))


def _silu_grad(x, s):
    return s * (1.0 + x * (1.0 - s))


def _tril_mask():
    r = lax.broadcasted_iota(jnp.int32, (CHUNK, CHUNK), 0)
    c = lax.broadcasted_iota(jnp.int32, (CHUNK, CHUNK), 1)
    return r >= c


def _conv_weights_to_sublanes(w_ref, w8_ref):
    for k in range(CONV_K):
        w8_ref[k] = jnp.broadcast_to(w_ref[k:k + 1, :], w8_ref.shape[1:])


def _conv_apply(src_ref, w8_ref, zs_ref, base, tm, e, flip, emit):
    def row_block(i, carry):
        r0 = pl.multiple_of(i * CONV_RC, CONV_RC)
        for c0 in range(0, e, CONV_CC):
            cols = slice(c0, c0 + CONV_CC)
            acc = None
            for s in range(8):
                nrows = CONV_RC if s == 0 else CONV_RC + 8
                taps = [k for k in range(CONV_K) if (base + k) % 8 == s]
                off0 = base + taps[0] - s
                span = nrows + 8 * (len(taps) - 1)
                window = src_ref[pl.ds(r0 + off0, span), cols].reshape(span // 8, 8, CONV_CC)
                z = None
                for m, k in enumerate(taps):
                    wk = (CONV_K - 1 - k) if flip else k
                    term = w8_ref[wk, :, cols][None] * window[m:m + nrows // 8]
                    z = term if z is None else z + term
                z = z.reshape(nrows, CONV_CC)
                if s == 0:
                    acc = z
                else:
                    zs_ref[s - 1, pl.ds(0, nrows), :] = z
                    acc = acc + zs_ref[s - 1, pl.ds(s, CONV_RC), :]
            emit(r0, c0, acc)
        return carry

    lax.fori_loop(0, tm // CONV_RC, row_block, 0)


def _mesh_pos():
    return lax.axis_index("x"), lax.axis_index("y"), lax.axis_index("c")


def _slot(px, py, pc):
    return 4 * px + 2 * py + pc


def _peers(x, y, c):
    return [((1 - x) if (k & 4) else x, (1 - y) if (k & 2) else y, (1 - c) if (k & 1) else c)
            for k in range(1, N_DEV)]


HBM_SPEC = pl.BlockSpec(memory_space=pltpu.HBM)
SEM_SPEC = pl.BlockSpec(memory_space=pltpu.SEMAPHORE)
SIDE_EFFECT = pltpu.SideEffectType.DATAFLOW_SIDE_EFFECTING


def _exchange_copy(src_refs, land_refs, send_sems, recv_sems, i, k, peer, scatter, me):
    slot = _slot(*peer)
    return pltpu.make_async_remote_copy(
        src_ref=src_refs[i].at[slot] if scatter else src_refs[i],
        dst_ref=land_refs[i].at[me if me is not None else slot],
        send_sem=send_sems.at[i * 7 + k], recv_sem=recv_sems.at[i * 7 + k],
        device_id=peer, device_id_type=MESH_ID)


def _exchange_start(srcs, lands, after, scatter, name):
    n = len(srcs)

    def body(*refs):
        src_refs, land_refs = refs[:n], refs[n:2 * n]
        send_sems, recv_sems, token = refs[2 * n + 1], refs[2 * n + 2], refs[-1]
        x, y, c = _mesh_pos()
        me = _slot(x, y, c)
        for i in range(n):
            for k, peer in enumerate(_peers(x, y, c)):
                _exchange_copy(src_refs, land_refs, send_sems, recv_sems, i, k, peer, scatter, me).start()
        token[...] = jnp.zeros_like(token)

    arrays = list(srcs) + list(lands)
    outs = pl.pallas_call(
        body, name=name,
        out_shape=(pltpu.SemaphoreType.DMA((7 * n,)), pltpu.SemaphoreType.DMA((7 * n,)),
                   *[pltpu.HBM(a.shape, a.dtype) for a in lands], jax.ShapeDtypeStruct((8, 128), F32)),
        in_specs=[HBM_SPEC] * (2 * n) + [pl.BlockSpec(memory_space=pl.ANY)],
        out_specs=(SEM_SPEC, SEM_SPEC, *[HBM_SPEC] * n, pl.BlockSpec(memory_space=pltpu.VMEM)),
        input_output_aliases={n + i: 2 + i for i in range(n)},
        compiler_params=pltpu.CompilerParams(has_side_effects=SIDE_EFFECT),
    )(*arrays, after)
    return outs[0], outs[1], list(srcs), list(outs[2:2 + n]), outs[-1]


def _exchange_wait(send_sems, recv_sems, srcs, lands, after, scatter, name):
    n = len(srcs)

    def body(*refs):
        src_refs, land_refs = refs[:n], refs[n:2 * n]
        send, recv = refs[2 * n], refs[2 * n + 1]
        x, y, c = _mesh_pos()
        for i in range(n):
            for k, peer in enumerate(_peers(x, y, c)):
                cp = _exchange_copy(src_refs, land_refs, send, recv, i, k, peer, scatter, None)
                cp.wait_send()
                cp.wait_recv()

    arrays = list(srcs) + list(lands)
    outs = pl.pallas_call(
        body, name=name,
        out_shape=tuple(pltpu.HBM(a.shape, a.dtype) for a in lands),
        in_specs=[HBM_SPEC] * (2 * n) + [SEM_SPEC, SEM_SPEC, pl.BlockSpec(memory_space=pl.ANY)],
        out_specs=tuple([HBM_SPEC] * n),
        input_output_aliases={n + i: i for i in range(n)},
        compiler_params=pltpu.CompilerParams(has_side_effects=SIDE_EFFECT),
    )(*arrays, send_sems, recv_sems, after)
    return list(outs)


def _place_own(parts, scatter, name):
    n = len(parts)
    me = jnp.reshape(_slot(*_mesh_pos()), (1,)).astype(jnp.int32)

    def body(me_ref, *refs):
        for i in range(n):
            refs[n + i][0] = refs[i][0] if scatter else refs[i][...]

    def slot_spec(shape):
        rest = len(shape)
        return pl.BlockSpec((1,) + tuple(shape), lambda i, me_ref: (me_ref[0],) + (0,) * rest)

    def whole_spec(shape):
        nd = len(shape)
        return pl.BlockSpec(tuple(shape), lambda i, me_ref: (0,) * nd)

    blocks = [a.shape[1:] if scatter else a.shape for a in parts]
    return pl.pallas_call(
        body, name=name,
        grid_spec=pltpu.PrefetchScalarGridSpec(
            num_scalar_prefetch=1, grid=(1,),
            in_specs=[slot_spec(b) if scatter else whole_spec(b) for b in blocks],
            out_specs=[slot_spec(b) for b in blocks]),
        out_shape=[pltpu.HBM((N_DEV,) + tuple(b), a.dtype) for a, b in zip(parts, blocks)],
        compiler_params=_params(1, 32),
    )(me, *parts)


def _all_gather(items, name):
    n = len(items)

    def body(*refs):
        in_refs, out_refs = refs[:n], refs[n:2 * n]
        send_sems, recv_sems, local_sems = refs[2 * n:]
        x, y, c = _mesh_pos()
        me, sibling = (x, y, c), (x, y, 1 - c)
        chips = [(1 - x, y), (x, 1 - y), (1 - x, 1 - y)]

        def copy(i, k, block, to, src=None):
            dst = out_refs[i].at[_slot(*block)]
            return pltpu.make_async_remote_copy(
                src_ref=dst if src is None else src, dst_ref=dst,
                send_sem=send_sems.at[i * 7 + k], recv_sem=recv_sems.at[i * 7 + k],
                device_id=to, device_id_type=MESH_ID)

        mine = [pltpu.make_async_copy(in_refs[i], out_refs[i].at[_slot(*me)], local_sems.at[i]) for i in range(n)]
        for cp in mine:
            cp.start()
        first = []
        for i in range(n):
            first.append(copy(i, 0, me, sibling, src=in_refs[i]))
            for j, chip in enumerate(chips):
                first.append(copy(i, 1 + j, me, (*chip, c), src=in_refs[i]))
        for cp in first:
            cp.start()
        passed = []
        for j, chip in enumerate(chips):
            for i in range(n):
                copy(i, 1 + j, (*chip, c), me).wait_recv()
                fwd = copy(i, 4 + j, (*chip, c), sibling)
                fwd.start()
                passed.append(fwd)
        for i in range(n):
            copy(i, 0, sibling, me).wait_recv()
            for j, chip in enumerate(chips):
                copy(i, 4 + j, (*chip, 1 - c), me).wait_recv()
        for cp in first + passed:
            cp.wait_send()
        for cp in mine:
            cp.wait()

    any_spec = pl.BlockSpec(memory_space=pl.ANY)
    return pl.pallas_call(
        body, name=name,
        out_shape=[jax.ShapeDtypeStruct((N_DEV,) + a.shape, a.dtype) for a in items],
        in_specs=[any_spec] * n, out_specs=[any_spec] * n,
        scratch_shapes=[pltpu.SemaphoreType.DMA((7 * n,)), pltpu.SemaphoreType.DMA((7 * n,)),
                        pltpu.SemaphoreType.DMA((n,))],
    )(*items)


def _layer_fwd(x, p_all, layer, g, w_blk, w_out, plg, gate_w, proj_w, seq, conv=None, sgu=None):
    t, d = x.shape
    nb, _, bn = w_blk.shape
    e = w_out.shape[0]
    pd = p_all.shape[-1]
    tm = TM_MIX
    nt = seq // tm
    is_conv = conv is not None
    mixer_args = conv if is_conv else sgu
    n_mix = len(mixer_args)

    def body(*refs):
        x_ref, p_ref, g_ref, w_ref, wo_ref, plg_ref, gw_ref, pw_ref = refs[:8]
        mix = refs[8:8 + n_mix]
        outs = refs[8 + n_mix:]
        if is_conv:
            cw_ref, cb_ref, lg_ref, lb_ref = mix
            h_ref, proj_ref, y1_ref, x1_ref, gate_ref, x2_ref, y0s, zs, w8 = outs

            @pl.when(lax.rem(pl.program_id(0), nt) == 0)
            def _():
                y0s[pl.ds(0, HALO), :] = jnp.zeros((HALO, e), F32)
            _conv_weights_to_sublanes(cw_ref, w8)
        else:
            lg_ref, lb_ref, sw_ref, sbt_ref = mix
            h_ref, proj_ref, x1_ref, gate_ref, x2_ref, mixed_s = outs

        for sb in range(tm // FUSE_SB):
            rows = pl.ds(sb * FUSE_SB, FUSE_SB)
            xv = x_ref[rows, :]
            hv = (xv * _rms_rstd(xv) * g_ref[...]).astype(MXU_DTYPE)
            h_ref[rows, :] = hv
            for j in range(nb):
                proj_ref[rows, j * bn:(j + 1) * bn] = jnp.dot(hv, w_ref[j], preferred_element_type=F32)
            if is_conv:
                y0s[pl.ds(HALO + sb * FUSE_SB, FUSE_SB), :] = proj_ref[rows, 0:e] * _sigmoid(proj_ref[rows, e:2 * e])

        if is_conv:
            def emit(r0, c0, acc):
                y1_ref[pl.ds(r0, CONV_RC), c0:c0 + CONV_CC] = acc + cb_ref[:, c0:c0 + CONV_CC]
            _conv_apply(y0s, w8, zs, HALO - (CONV_K - 1), tm, e, False, emit)

        for sb in range(tm // FUSE_SB):
            rows = pl.ds(sb * FUSE_SB, FUSE_SB)
            if is_conv:
                xhat, _ = _ln_stats(y1_ref[rows, :])
                y2 = xhat * lg_ref[...] + lb_ref[...]
                y = y2 * _sigmoid(y2)
            else:
                _, _, u, _, _, _ = _sgu_parts(proj_ref[rows, 0:e], proj_ref[rows, e:2 * e], lg_ref[...], lb_ref[...],
                                              sw_ref, sbt_ref, mixed_s, FUSE_SB, e)
                y = u * mixed_s[...]
            z = proj_ref[rows, 2 * e:3 * e]
            q = (y * (z * _sigmoid(z))).astype(MXU_DTYPE)
            x1 = x_ref[rows, :] + jnp.dot(q, wo_ref[...], preferred_element_type=F32)
            x1_ref[rows, :] = x1
            rn = x1 * _rms_rstd(x1) * plg_ref[...]
            gate = _sigmoid(_dot(rn, gw_ref[...]))
            gate_ref[rows, :] = gate
            x2_ref[rows, :] = x1 + gate * _dot(p_ref[0, rows, :], pw_ref[...])

        if is_conv:
            y0s[pl.ds(0, HALO), :] = y0s[pl.ds(tm, HALO), :]

    row = lambda w: pl.BlockSpec((tm, w), lambda i: (i, 0))
    f32 = lambda w: jax.ShapeDtypeStruct((t, w), F32)
    out_shape = ([jax.ShapeDtypeStruct((t, d), MXU_DTYPE), f32(3 * e)] + ([f32(e)] if is_conv else [])
                 + [f32(d), f32(d), f32(d)])
    out_specs = [row(d), row(3 * e)] + ([row(e)] if is_conv else []) + [row(d), row(d), row(d)]
    scratch = ([pltpu.VMEM((tm + HALO, e), F32), pltpu.VMEM((7, CONV_RC + 8, CONV_CC), F32),
                pltpu.VMEM((CONV_K, 8, e), F32)] if is_conv else [pltpu.VMEM((FUSE_SB, e), F32)])
    return pl.pallas_call(
        body, name="layer_fwd_conv" if is_conv else "layer_fwd_sgu", grid=(t // tm,),
        in_specs=[row(d), pl.BlockSpec((1, tm, pd), lambda i: (layer, i, 0)), _whole((1, d)), _whole(w_blk.shape),
                  _whole((e, d)), _whole((1, d)),
                  _whole((d, d)), _whole((pd, d))] + [_whole(a.shape) for a in mixer_args],
        out_specs=out_specs, out_shape=out_shape, scratch_shapes=scratch,
        compiler_params=_params(1, 60),
    )(x, p_all, g, w_blk, w_out, plg, gate_w, proj_w, *mixer_args)


def _sgu_parts(a, b, lg, lb, sw_ref, sbt_ref, mixed_s, tm, e):
    eg = e // GROUPS
    ea = lax.erf(a * INV_SQRT2)
    eb = lax.erf(b * INV_SQRT2)
    u = 0.5 * a * (1.0 + ea)
    v0 = 0.5 * b * (1.0 + eb)
    xhat, rstd = _ln_stats(v0)
    v = (xhat * lg + lb).astype(MXU_DTYPE)
    mask = _tril_mask()
    for g in range(GROUPS):
        wt = jnp.where(mask, sw_ref[g], 0.0).astype(MXU_DTYPE)
        bcol = sbt_ref[:, g:g + 1]
        for ch in range(tm // CHUNK):
            rows = slice(ch * CHUNK, (ch + 1) * CHUNK)
            cols = slice(g * eg, (g + 1) * eg)
            mixed_s[rows, cols] = jnp.dot(wt, v[rows, cols], preferred_element_type=F32) + bcol
    return ea, eb, u, xhat, rstd, v


def _loss_head(xf, fg, tgt):
    t, d = xf.shape
    tm = TM_OUT
    nsteps = t // tm

    def body(x_ref, g_ref, t_ref, loss_ref, dx_ref, dg_ref, sq_s):
        i = pl.program_id(0)

        @pl.when(i == 0)
        def _():
            sq_s[...] = jnp.zeros_like(sq_s)
            dg_ref[...] = jnp.zeros_like(dg_ref)
        x = x_ref[...]
        rstd = _rms_rstd(x)
        err = x * rstd * g_ref[...] - t_ref[...]
        sq_s[...] += jnp.sum(err * err, axis=0, keepdims=True)
        dx, dg = _rms_bwd(err * (1.0 / d), x, rstd, g_ref[...])
        dx_ref[...] = dx
        dg_ref[...] += dg

        @pl.when(i == nsteps - 1)
        def _():
            loss_ref[...] = jnp.sum(sq_s[...], axis=1, keepdims=True) * (0.5 / d)

    row = pl.BlockSpec((tm, d), lambda i: (i, 0))
    return pl.pallas_call(
        body, name="loss_head", grid=(nsteps,),
        in_specs=[row, _whole((1, d)), row],
        out_specs=[_acc_out((1, 1)), row, _acc_out((1, d))],
        out_shape=[jax.ShapeDtypeStruct((1, 1), F32), jax.ShapeDtypeStruct((t, d), F32),
                   jax.ShapeDtypeStruct((1, d), F32)],
        scratch_shapes=[pltpu.VMEM((1, d), F32)],
        compiler_params=_params(1, 32),
    )(xf, fg, tgt)


def _ple_bwd(dx2, x1, gate, p_all, layer, plg, gate_w, proj_w):
    t, d = x1.shape
    pd = p_all.shape[-1]
    tm = TM_OUT
    nsteps = t // tm
    bn = d // N_DEV

    def body(dx2_ref, x1_ref, gate_ref, p_ref, plg_ref, gw_ref, pw_ref, dx1_ref, dgw_ref, dpw_ref, dplg_ref,
             gw_acc, pw_acc):
        i = pl.program_id(0)

        @pl.when(i == 0)
        def _():
            gw_acc[...] = jnp.zeros_like(gw_acc)
            pw_acc[...] = jnp.zeros_like(pw_acc)
            dplg_ref[...] = jnp.zeros_like(dplg_ref)
        dx2 = dx2_ref[...]
        x1 = x1_ref[...]
        plg = plg_ref[...]
        rstd = _rms_rstd(x1)
        rn = (x1 * rstd * plg).astype(MXU_DTYPE)
        gate = gate_ref[...]
        p_b = p_ref[0].astype(MXU_DTYPE)
        pp = jnp.dot(p_b, pw_ref[...], preferred_element_type=F32)
        dpp = (dx2 * gate).astype(MXU_DTYPE)
        dgpre = (dx2 * pp * gate * (1.0 - gate)).astype(MXU_DTYPE)
        pw_acc[...] += _dot_tn(p_b, dpp)
        gw_acc[...] += _dot_tn(rn, dgpre)
        drn = _dot_nt(dgpre, gw_ref[...])
        dx, dg = _rms_bwd(drn, x1, rstd, plg)
        dx1_ref[...] = dx2 + dx
        dplg_ref[...] += dg

        @pl.when(i == nsteps - 1)
        def _():
            dgw_ref[...] = gw_acc[...].astype(dgw_ref.dtype)
            for j in range(N_DEV):
                dpw_ref[j] = pw_acc[:, j * bn:(j + 1) * bn].astype(dpw_ref.dtype)

    row = lambda w: pl.BlockSpec((tm, w), lambda i: (i, 0))
    return pl.pallas_call(
        body, name="ple_bwd", grid=(nsteps,),
        in_specs=[row(d), row(d), row(d), pl.BlockSpec((1, tm, pd), lambda i: (layer, i, 0)), _whole((1, d)),
                  _whole((d, d)), _whole((pd, d))],
        out_specs=[row(d), _acc_out((d, d)), _acc_out((N_DEV, pd, bn)), _acc_out((1, d))],
        out_shape=[jax.ShapeDtypeStruct((t, d), F32), pltpu.HBM((d, d), WIRE_DTYPE),
                   pltpu.HBM((N_DEV, pd, bn), WIRE_DTYPE), jax.ShapeDtypeStruct((1, d), F32)],
        scratch_shapes=[pltpu.VMEM((d, d), F32), pltpu.VMEM((pd, d), F32)],
        compiler_params=_params(1, 48),
    )(dx2, x1, gate, p_all, plg, gate_w, proj_w)


def _outproj_conv_bwd(dx1, y1, proj, w_out, lg, lb):
    t, d = dx1.shape
    e = y1.shape[1]
    tm = TM_MIX
    nsteps = t // tm

    def body(dx1_ref, y1_ref, z_ref, wo_ref, lg_ref, lb_ref, dy1_ref, dz_ref, dwo_ref, dlg_ref, dlb_ref, dcb_ref,
             wo_acc):
        i = pl.program_id(0)

        @pl.when(i == 0)
        def _():
            wo_acc[...] = jnp.zeros_like(wo_acc)
            dlg_ref[...] = jnp.zeros_like(dlg_ref)
            dlb_ref[...] = jnp.zeros_like(dlb_ref)
            dcb_ref[...] = jnp.zeros_like(dcb_ref)
        xhat, rstd = _ln_stats(y1_ref[...])
        lg = lg_ref[...]
        y2 = xhat * lg + lb_ref[...]
        s2 = _sigmoid(y2)
        y = y2 * s2
        z = z_ref[...]
        s = _sigmoid(z)
        sz = z * s
        dx1 = dx1_ref[...].astype(MXU_DTYPE)
        wo_acc[...] += _dot_tn((y * sz).astype(MXU_DTYPE), dx1)
        dq = _dot_nt(dx1, wo_ref[...])
        dz_ref[...] = (dq * y * _silu_grad(z, s)).astype(dz_ref.dtype)
        dy2 = dq * sz * _silu_grad(y2, s2)
        dlg_ref[...] += jnp.sum(dy2 * xhat, axis=0, keepdims=True)
        dlb_ref[...] += jnp.sum(dy2, axis=0, keepdims=True)
        dy1 = _ln_bwd(dy2 * lg, xhat, rstd)
        dy1_ref[...] = dy1
        dcb_ref[...] += jnp.sum(dy1, axis=0, keepdims=True)

        @pl.when(i == nsteps - 1)
        def _():
            dwo_ref[...] = wo_acc[...].astype(dwo_ref.dtype)

    row = lambda w: pl.BlockSpec((tm, w), lambda i: (i, 0))
    return pl.pallas_call(
        body, name="outproj_conv_bwd", grid=(nsteps,),
        in_specs=[row(d), row(e), pl.BlockSpec((tm, e), lambda i: (i, 2)), _whole((e, d)), _whole((1, e)),
                  _whole((1, e))],
        out_specs=[row(e), row(e), _acc_out((e, d)), _acc_out((1, e)), _acc_out((1, e)), _acc_out((1, e))],
        out_shape=[jax.ShapeDtypeStruct((t, e), F32), jax.ShapeDtypeStruct((t, e), MXU_DTYPE),
                   pltpu.HBM((e, d), WIRE_DTYPE)] + [jax.ShapeDtypeStruct((1, e), F32)] * 3,
        scratch_shapes=[pltpu.VMEM((e, d), F32)],
        compiler_params=_params(1, 56),
    )(dx1, y1, proj, w_out, lg, lb)


def _conv_bwd(dy1, proj, dz, cw, seq):
    t, e = dy1.shape
    tm = TM_MIX
    nt = seq // tm
    hb = tm // HALO
    n_halo_blocks = t // HALO

    def body(d_ref, dn_ref, a_ref, b_ref, ah_ref, bh_ref, dz_ref, cw_ref, dproj_ref, dcw_ref,
             y0s, d1s, zs, dsh, dcw8, w8):
        i = pl.program_id(0)
        pos = lax.rem(i, nt)

        @pl.when(i == 0)
        def _():
            dcw8[...] = jnp.zeros_like(dcw8)
        _conv_weights_to_sublanes(cw_ref, w8)
        a = a_ref[...]
        sb = _sigmoid(b_ref[...])
        y0s[pl.ds(HALO, tm), :] = a * sb
        d1s[pl.ds(0, tm), :] = d_ref[...]

        @pl.when(pos == 0)
        def _():
            y0s[pl.ds(0, HALO), :] = jnp.zeros((HALO, e), F32)

        @pl.when(pos != 0)
        def _():
            y0s[pl.ds(0, HALO), :] = ah_ref[...] * _sigmoid(bh_ref[...])

        @pl.when(pos == nt - 1)
        def _():
            d1s[pl.ds(tm, HALO), :] = jnp.zeros((HALO, e), F32)

        @pl.when(pos != nt - 1)
        def _():
            d1s[pl.ds(tm, HALO), :] = dn_ref[...]

        base = HALO - (CONV_K - 1)
        for c0 in range(0, e, DCW_CC):
            cols = slice(c0, c0 + DCW_CC)
            dcur = d_ref[:, cols]
            for s in range(1, 8):
                dsh[s - 1, pl.ds(0, 8), :] = jnp.zeros((8, DCW_CC), F32)
                dsh[s - 1, pl.ds(tm, 8), :] = jnp.zeros((8, DCW_CC), F32)
                dsh[s - 1, pl.ds(s, tm), :] = dcur
            for s in range(8):
                taps = [k for k in range(CONV_K) if (base + k) % 8 == s]
                off0 = base + taps[0] - s
                n, ch = (tm, DCW_RC) if s == 0 else (tm + 8, (tm + 8) // DCW_CHUNKS)
                sums = [None] * len(taps)
                for r in range(0, n, ch):
                    dch = d_ref[r:r + ch, cols] if s == 0 else dsh[s - 1, r:r + ch, :]
                    window = y0s[pl.ds(off0 + r, ch + 8 * (len(taps) - 1)), cols]
                    for m in range(len(taps)):
                        part = jnp.sum((dch * window[8 * m:8 * m + ch]).reshape(ch // 8, 8, DCW_CC), axis=0)
                        sums[m] = part if sums[m] is None else sums[m] + part
                for m, k in enumerate(taps):
                    dcw8[k, :, cols] += sums[m]

        def emit(r0, c0, dy0):
            rs, cs = pl.ds(r0, CONV_RC), slice(c0, c0 + CONV_CC)
            sbv = _sigmoid(b_ref[rs, cs])
            av = a_ref[rs, cs]
            dproj_ref[rs, c0:c0 + CONV_CC] = (dy0 * sbv).astype(dproj_ref.dtype)
            dproj_ref[rs, e + c0:e + c0 + CONV_CC] = (dy0 * av * sbv * (1.0 - sbv)).astype(dproj_ref.dtype)
        _conv_apply(d1s, w8, zs, 0, tm, e, True, emit)
        dproj_ref[:, 2 * e:3 * e] = dz_ref[...]

        @pl.when(i == t // tm - 1)
        def _():
            dcw_ref[...] = jnp.sum(dcw8[...], axis=1)

    tile = lambda col: pl.BlockSpec((tm, e), lambda i: (i, col))
    prev = lambda col: pl.BlockSpec((HALO, e), lambda i: (jnp.maximum(i * hb - 1, 0), col))
    nxt = pl.BlockSpec((HALO, e), lambda i: (jnp.minimum((i + 1) * hb, n_halo_blocks - 1), 0))
    return pl.pallas_call(
        body, name="conv_bwd", grid=(t // tm,),
        in_specs=[tile(0), nxt, tile(0), tile(1), prev(0), prev(1), tile(0), _whole(cw.shape)],
        out_specs=[pl.BlockSpec((tm, 3 * e), lambda i: (i, 0)), _acc_out(cw.shape)],
        out_shape=[jax.ShapeDtypeStruct((t, 3 * e), MXU_DTYPE), jax.ShapeDtypeStruct(cw.shape, F32)],
        scratch_shapes=[pltpu.VMEM((tm + HALO, e), F32), pltpu.VMEM((tm + HALO, e), F32),
                        pltpu.VMEM((7, CONV_RC + 8, CONV_CC), F32), pltpu.VMEM((7, tm + 8, DCW_CC), F32),
                        pltpu.VMEM((CONV_K, 8, e), F32), pltpu.VMEM((CONV_K, 8, e), F32)],
        compiler_params=_params(1, 56),
    )(dy1, dy1, proj, proj, proj, proj, dz, cw)


def _outproj_sgu_bwd(dx1, proj, w_out, lg, lb, sw, swt, sbt):
    t, d = dx1.shape
    e = w_out.shape[0]
    eg = e // GROUPS
    tm = TM_MIX
    nsteps = t // tm

    def body(dx1_ref, a_ref, b_ref, z_ref, wo_ref, lg_ref, lb_ref, sw_ref, swt_ref, sbt_ref,
             dproj_ref, dwo_ref, dsw_ref, dsb_ref, dlg_ref, dlb_ref, mixed_s, dv_s, sb_acc, wo_acc):
        i = pl.program_id(0)

        @pl.when(i == 0)
        def _():
            wo_acc[...] = jnp.zeros_like(wo_acc)
            dsw_ref[...] = jnp.zeros_like(dsw_ref)
            sb_acc[...] = jnp.zeros_like(sb_acc)
            dlg_ref[...] = jnp.zeros_like(dlg_ref)
            dlb_ref[...] = jnp.zeros_like(dlb_ref)
        a = a_ref[...]
        b = b_ref[...]
        lg = lg_ref[...]
        ea, eb, u, xhat, rstd, v = _sgu_parts(a, b, lg, lb_ref[...], sw_ref, sbt_ref, mixed_s, tm, e)
        mixed = mixed_s[...]
        y = u * mixed
        z = z_ref[...]
        s = _sigmoid(z)
        sz = z * s
        dx1 = dx1_ref[...].astype(MXU_DTYPE)
        wo_acc[...] += _dot_tn((y * sz).astype(MXU_DTYPE), dx1)
        dq = _dot_nt(dx1, wo_ref[...])
        dproj_ref[:, 2 * e:3 * e] = (dq * y * _silu_grad(z, s)).astype(dproj_ref.dtype)
        dy = dq * sz
        du = dy * mixed
        dmixed = (dy * u).astype(MXU_DTYPE)
        mask = _tril_mask()
        mask_t = (lax.broadcasted_iota(jnp.int32, (CHUNK, CHUNK), 0)
                  <= lax.broadcasted_iota(jnp.int32, (CHUNK, CHUNK), 1))
        ones = jnp.ones((8, eg), MXU_DTYPE)
        for g in range(GROUPS):
            wtt = jnp.where(mask_t, swt_ref[g], 0.0).astype(MXU_DTYPE)
            cols = slice(g * eg, (g + 1) * eg)
            for ch in range(tm // CHUNK):
                rows = slice(ch * CHUNK, (ch + 1) * CHUNK)
                dm = dmixed[rows, cols]
                dv_s[rows, cols] = jnp.dot(wtt, dm, preferred_element_type=F32)
                dsw_ref[g] += _dot_nt(dm, v[rows, cols])
                sb_acc[g] += _dot_nt(ones, dm)
        dv = dv_s[...]
        dlg_ref[...] += jnp.sum(dv * xhat, axis=0, keepdims=True)
        dlb_ref[...] += jnp.sum(dv, axis=0, keepdims=True)
        dv0 = _ln_bwd(dv * lg, xhat, rstd)
        pdf_a = jnp.exp(-0.5 * a * a) * INV_SQRT_2PI
        pdf_b = jnp.exp(-0.5 * b * b) * INV_SQRT_2PI
        dproj_ref[:, 0:e] = (du * (0.5 * (1.0 + ea) + a * pdf_a)).astype(dproj_ref.dtype)
        dproj_ref[:, e:2 * e] = (dv0 * (0.5 * (1.0 + eb) + b * pdf_b)).astype(dproj_ref.dtype)

        @pl.when(i == nsteps - 1)
        def _():
            dwo_ref[...] = wo_acc[...].astype(dwo_ref.dtype)
            for g in range(GROUPS):
                dsw_ref[g] = jnp.where(mask, dsw_ref[g], 0.0)
                dsb_ref[g:g + 1, :] = sb_acc[g, 0:1, :]

    tile = lambda col: pl.BlockSpec((tm, e), lambda i: (i, col))
    return pl.pallas_call(
        body, name="outproj_sgu_bwd", grid=(nsteps,),
        in_specs=[pl.BlockSpec((tm, d), lambda i: (i, 0)), tile(0), tile(1), tile(2), _whole((e, d)), _whole((1, e)),
                  _whole((1, e)), _whole(sw.shape), _whole(swt.shape), _whole(sbt.shape)],
        out_specs=[pl.BlockSpec((tm, 3 * e), lambda i: (i, 0)), _acc_out((e, d)), _acc_out(sw.shape),
                   _acc_out((GROUPS, CHUNK)), _acc_out((1, e)), _acc_out((1, e))],
        out_shape=[jax.ShapeDtypeStruct((t, 3 * e), MXU_DTYPE), pltpu.HBM((e, d), WIRE_DTYPE),
                   jax.ShapeDtypeStruct(sw.shape, F32), jax.ShapeDtypeStruct((GROUPS, CHUNK), F32),
                   jax.ShapeDtypeStruct((1, e), F32), jax.ShapeDtypeStruct((1, e), F32)],
        scratch_shapes=[pltpu.VMEM((tm, e), F32), pltpu.VMEM((tm, e), F32), pltpu.VMEM((GROUPS, 8, CHUNK), F32),
                        pltpu.VMEM((e, d), F32)],
        compiler_params=_params(1, 60),
    )(dx1, proj, proj, proj, w_out, lg, lb, sw, swt, sbt)


def _inproj_bwd_x(dproj, w_blk, dx1, x, g):
    t, d = x.shape
    nb, _, bn = w_blk.shape
    tm = TM_IN

    def body(dp_ref, w_ref, dx1_ref, x_ref, g_ref, dx_ref, dg_ref):
        @pl.when(pl.program_id(0) == 0)
        def _():
            dg_ref[...] = jnp.zeros_like(dg_ref)
        dh = None
        for j in range(nb):
            term = _dot_nt(dp_ref[:, j * bn:(j + 1) * bn], w_ref[j])
            dh = term if dh is None else dh + term
        xv = x_ref[...]
        dx, dg = _rms_bwd(dh, xv, _rms_rstd(xv), g_ref[...])
        dx_ref[...] = dx1_ref[...] + dx
        dg_ref[...] += dg

    row = lambda w: pl.BlockSpec((tm, w), lambda i: (i, 0))
    return pl.pallas_call(
        body, name="inproj_bwd_x", grid=(t // tm,),
        in_specs=[row(nb * bn), _whole(w_blk.shape), row(d), row(d), _whole((1, d))],
        out_specs=[row(d), _acc_out((1, d))],
        out_shape=[jax.ShapeDtypeStruct((t, d), F32), jax.ShapeDtypeStruct((1, d), F32)],
        compiler_params=_params(1, 56),
    )(dproj, w_blk, dx1, x, g)


def _inproj_bwd_w(h, dproj):
    t, d = h.shape
    bn = dproj.shape[1] // N_DEV
    tm = TM_IN
    nsteps = t // tm

    nh = 2
    per = N_DEV // nh

    def body(h_ref, dp_ref, dw_ref, acc):
        i = pl.program_id(1)

        @pl.when(i == 0)
        def _():
            acc[...] = jnp.zeros_like(acc)
        hv = h_ref[...]
        for jj in range(per):
            acc[jj] += _dot_tn(hv, dp_ref[:, jj * bn:(jj + 1) * bn])

        @pl.when(i == nsteps - 1)
        def _():
            dw_ref[...] = acc[...].astype(dw_ref.dtype)

    return pl.pallas_call(
        body, name="inproj_bwd_w", grid=(nh, nsteps),
        in_specs=[pl.BlockSpec((tm, d), lambda hh, i: (i, 0)), pl.BlockSpec((tm, per * bn), lambda hh, i: (i, hh))],
        out_specs=[pl.BlockSpec((per, d, bn), lambda hh, i: (hh, 0, 0))],
        out_shape=[pltpu.HBM((N_DEV, d, bn), WIRE_DTYPE)],
        scratch_shapes=[pltpu.VMEM((per, d, bn), F32)],
        compiler_params=_params(2, 56),
    )(h, dproj)[0]


def _adamw(parts, w, m, v):
    nl, r, c = w.shape
    tr = r
    for cand in (512, 256, 128, 64, 32, 16, 8):
        if r % cand == 0 and cand * c * 4 <= (1 << 19):
            tr = cand
            break
    bc1 = 1.0 - ADAM_B1 ** ADAM_STEP
    bc2 = 1.0 - ADAM_B2 ** ADAM_STEP

    def body(*refs):
        p_refs = refs[:nl]
        w_ref, m_ref, v_ref, g_ref, d_ref, nm_ref, nv_ref = refs[nl:]

        def update(p_ref):
            g = p_ref[0].astype(F32)
            for s in range(1, N_DEV):
                g = g + p_ref[s].astype(F32)
            nm = ADAM_B1 * m_ref[0] + (1.0 - ADAM_B1) * g
            nv = ADAM_B2 * v_ref[0] + (1.0 - ADAM_B2) * (g * g)
            g_ref[0] = g
            nm_ref[0] = nm
            nv_ref[0] = nv
            d_ref[0] = -ADAM_LR * ((nm / bc1) / (jnp.sqrt(nv / bc2) + ADAM_EPS) + ADAM_WD * w_ref[0])

        if nl == 1:
            update(p_refs[0])
        else:
            for kk in range(nl):
                pl.when(pl.program_id(0) == kk)(lambda kk=kk: update(p_refs[kk]))

    part_spec = lambda kk: pl.BlockSpec((N_DEV, tr, c), lambda l, i: (0, jnp.where(l == kk, i, 0), 0))
    row = pl.BlockSpec((1, tr, c), lambda l, i: (l, i, 0))
    return pl.pallas_call(
        body, name="adamw", grid=(nl, r // tr),
        in_specs=[part_spec(kk) for kk in range(nl)] + [row, row, row],
        out_specs=[row] * 4,
        out_shape=[jax.ShapeDtypeStruct((nl, r, c), F32)] * 4,
        compiler_params=_params(2, 48),
    )(*parts, w, m, v)


def _adamw_small(parts, ws, ms, vs, loss_parts):
    n = len(ws)
    bc1 = 1.0 - ADAM_B1 ** ADAM_STEP
    bc2 = 1.0 - ADAM_B2 ** ADAM_STEP

    def total(ref):
        acc = ref[0]
        for s in range(1, N_DEV):
            acc = acc + ref[s]
        return acc

    def body(*refs):
        p_refs, w_refs, m_refs, v_refs = refs[:n], refs[n:2 * n], refs[2 * n:3 * n], refs[3 * n:4 * n]
        outs = refs[4 * n + 1:]
        for i in range(n):
            g = total(p_refs[i])
            nm = ADAM_B1 * m_refs[i][...] + (1.0 - ADAM_B1) * g
            nv = ADAM_B2 * v_refs[i][...] + (1.0 - ADAM_B2) * (g * g)
            outs[i][...] = g
            outs[n + i][...] = -ADAM_LR * ((nm / bc1) / (jnp.sqrt(nv / bc2) + ADAM_EPS) + ADAM_WD * w_refs[i][...])
            outs[2 * n + i][...] = nm
            outs[3 * n + i][...] = nv
        outs[4 * n][...] = total(refs[4 * n])

    vmem = pl.BlockSpec(memory_space=pltpu.VMEM)
    shapes = [jax.ShapeDtypeStruct(w.shape, F32) for w in ws]
    outs = pl.pallas_call(
        body, name="adamw_small",
        in_specs=[vmem] * (4 * n + 1), out_specs=[vmem] * (4 * n + 1),
        out_shape=shapes * 4 + [jax.ShapeDtypeStruct((8, 128), F32)],
        compiler_params=pltpu.CompilerParams(vmem_limit_bytes=48 << 20),
    )(*parts, *ws, *ms, *vs, loss_parts)
    return outs[:n], outs[n:2 * n], outs[2 * n:3 * n], outs[3 * n:4 * n], outs[4 * n]


def _pack(arrays):
    flat = jnp.concatenate([a.reshape(-1).astype(F32) for a in arrays])
    unit = 8 * PACK_COLS
    padded = -(-flat.shape[0] // unit) * unit
    return jnp.pad(flat, (0, padded - flat.shape[0])).reshape(-1, PACK_COLS)


def _unshard_last(packed, shard_shapes):
    flat = packed.reshape(N_DEV, -1)
    out, off = [], 0
    for s in shard_shapes:
        n = math.prod(s)
        a = jnp.moveaxis(flat[:, off:off + n].reshape((N_DEV,) + tuple(s)), 0, -2)
        out.append(a.reshape(tuple(s[:-1]) + (N_DEV * s[-1],)))
        off += n
    return out


def kernel(x, p, norm_g, w_in, w_out, conv_w, conv_b, conv_ln_g, conv_ln_b, sgu_ln_g, sgu_ln_b, sgu_w, sgu_b, pl_norm_g, pl_gate_w, pl_proj_w, final_g, loss_target, m_norm_g, m_w_in, m_w_out, m_conv_w, m_conv_b, m_conv_ln_g, m_conv_ln_b, m_sgu_ln_g, m_sgu_ln_b, m_sgu_w, m_sgu_b, m_pl_norm_g, m_pl_gate_w, m_pl_proj_w, m_final_g, v_norm_g, v_w_in, v_w_out, v_conv_w, v_conv_b, v_conv_ln_g, v_conv_ln_b, v_sgu_ln_g, v_sgu_ln_b, v_sgu_w, v_sgu_b, v_pl_norm_g, v_pl_gate_w, v_pl_proj_w, v_final_g):
    bsz, seq, d = x.shape
    t = bsz * seq
    depth = w_in.shape[0]
    e = w_out.shape[1] * N_DEV
    pd = p.shape[-1]
    n_conv, n_sgu = conv_w.shape[0], sgu_ln_g.shape[0]

    small_shapes = [conv_w.shape, sgu_ln_g.shape, sgu_ln_b.shape]
    cast = lambda a: a.astype(MXU_DTYPE)
    first = [cast(w_in[0]), cast(w_out[0]), cast(pl_gate_w[0]), cast(pl_proj_w), _pack([conv_w, sgu_ln_g, sgu_ln_b])]
    later = [[cast(w_in[l]), cast(w_out[l]), cast(pl_gate_w[l])] for l in range(1, depth)]
    gathered = _all_gather(first, "gather_weights")
    gather_pending, gather_tokens = {}, 0.0
    for l in range(1, depth):
        lands = _place_own(later[l - 1], False, "place_own_weights")
        send, recv, srcs, lnds, token = _exchange_start(later[l - 1], lands, gathered[0], False, f"gather_start_{l}")
        gather_pending[l] = (send, recv, srcs, lnds)
        gather_tokens = gather_tokens + token[0, 0]
    w_in_g = {0: gathered[0]}
    w_out_g = {0: gathered[1].reshape(e, d)}
    gate_g = {0: gathered[2].reshape(d, d)}
    proj_g = jnp.transpose(gathered[3], (1, 2, 0, 3)).reshape(depth, pd, d)
    conv_w_g, sgu_ln_g_g, sgu_ln_b_g = _unshard_last(gathered[4], small_shapes)
    sgu_wt = jnp.swapaxes(sgu_w, -1, -2)
    sgu_bt = jnp.swapaxes(sgu_b, -1, -2)

    xs = [x.reshape(t, d)]
    p_all = p.reshape(depth, t, pd)
    saved = []
    for l in range(depth):
        j = l // 2
        if l == 0:
            g_l = norm_g[0:1] + gather_tokens
        else:
            g_l = norm_g[l:l + 1]
            got = _exchange_wait(*gather_pending.pop(l), xs[-1], False, f"gather_wait_{l}")
            w_in_g[l], w_out_g[l], gate_g[l] = got[0], got[1].reshape(e, d), got[2].reshape(d, d)
        common = (xs[-1], p_all, l, g_l, w_in_g[l], w_out_g[l], pl_norm_g[l:l + 1], gate_g[l], proj_g[l], seq)
        if l % 2 == 0:
            h, proj, y1, x1, gate, x2 = _layer_fwd(
                *common, conv=(conv_w_g[j], conv_b[j:j + 1], conv_ln_g[j:j + 1], conv_ln_b[j:j + 1]))
        else:
            y1 = None
            h, proj, x1, gate, x2 = _layer_fwd(
                *common, sgu=(sgu_ln_g_g[j:j + 1], sgu_ln_b_g[j:j + 1], sgu_w[j], sgu_bt[j]))
        saved.append((h, proj, y1, x1, gate))
        xs.append(x2)

    loss_part, dx, d_final_g = _loss_head(xs[-1], final_g.reshape(1, d), loss_target.reshape(t, d))

    d_norm_g, d_pl_norm_g = [None] * depth, [None] * depth
    scatter_pending = {}
    d_conv_w, d_conv_b, d_conv_ln_g, d_conv_ln_b = [None] * n_conv, [None] * n_conv, [None] * n_conv, [None] * n_conv
    d_sgu_ln_g, d_sgu_ln_b, d_sgu_w, d_sgu_b = [None] * n_sgu, [None] * n_sgu, [None] * n_sgu, [None] * n_sgu
    def scatter(parts, name):
        send, recv, srcs, lnds, token = _exchange_start(parts, _place_own(parts, True, "place_own_grads"), parts[0],
                                                        True, name)
        return (send, recv, srcs, lnds), token[0, 0]

    for l in reversed(range(depth)):
        j = l // 2
        h, proj, y1, x1, gate = saved[l]
        dx1, dgate_p, dprojw_p, d_pl_norm_g[l] = _ple_bwd(dx, x1, gate, p_all, l, pl_norm_g[l:l + 1], gate_g[l],
                                                         proj_g[l])
        if l % 2 == 0:
            dy1, dz, dw_out_p, d_conv_ln_g[j], d_conv_ln_b[j], d_conv_b[j] = _outproj_conv_bwd(
                dx1, y1, proj, w_out_g[l], conv_ln_g[j:j + 1], conv_ln_b[j:j + 1])
        else:
            dproj, dw_out_p, d_sgu_w[j], d_sgu_b[j], d_sgu_ln_g[j], d_sgu_ln_b[j] = _outproj_sgu_bwd(
                dx1, proj, w_out_g[l], sgu_ln_g_g[j:j + 1], sgu_ln_b_g[j:j + 1], sgu_w[j], sgu_wt[j], sgu_bt[j])
        early = [dw_out_p.reshape(N_DEV, e // N_DEV, d), dgate_p.reshape(N_DEV, d // N_DEV, d), dprojw_p]
        early_token = 0.0
        if l == 0:
            scatter_pending["0_early"], early_token = scatter(early, "scatter_start_0_early")
            early = []
        if l % 2 == 0:
            dproj, d_conv_w[j] = _conv_bwd(dy1, proj, dz, conv_w_g[j] + early_token, seq)
        scatter_pending[l], token = scatter([_inproj_bwd_w(h, dproj)] + early, f"scatter_start_{l}")
        dx, d_norm_g[l] = _inproj_bwd_x(dproj, w_in_g[l], dx1, xs[l], norm_g[l:l + 1] + token)
    grad_x = dx.reshape(bsz, seq, d)

    def own_eighths(full):
        return jnp.moveaxis(full.reshape(full.shape[:-1] + (N_DEV, full.shape[-1] // N_DEV)), -2, 0)

    small_parts = [own_eighths(jnp.stack(d_conv_w)), own_eighths(jnp.concatenate(d_sgu_ln_g, axis=0)),
                   own_eighths(jnp.concatenate(d_sgu_ln_b, axis=0))]
    rep_parts = [jnp.concatenate(d_norm_g, axis=0), jnp.concatenate(d_conv_b, axis=0),
                 jnp.concatenate(d_conv_ln_g, axis=0), jnp.concatenate(d_conv_ln_b, axis=0), jnp.stack(d_sgu_w),
                 jnp.stack(d_sgu_b), jnp.concatenate(d_pl_norm_g, axis=0), d_final_g,
                 jnp.broadcast_to(loss_part, (8, 128))]
    small_send, small_recv, small_srcs, small_lnds, small_token = _exchange_start(
        small_parts, _place_own(small_parts, True, "place_own_small"), grad_x, True, "scatter_small_start")
    rep_send, rep_recv, rep_srcs, rep_lnds, rep_token = _exchange_start(
        rep_parts, _place_own(rep_parts, False, "place_own_replicated"), grad_x, False, "gather_replicated_start")

    landed = {}
    for key in list(scatter_pending):
        landed[key] = _exchange_wait(*scatter_pending.pop(key), small_token + rep_token, True, f"scatter_wait_{key}")
    dw_in_l = [landed[l][0] for l in range(depth)]
    rest = [landed["0_early"]] + [landed[l][1:] for l in range(1, depth)]

    o_w_in = _adamw(dw_in_l, w_in, m_w_in, v_w_in)
    o_w_out = _adamw([r[0] for r in rest], w_out, m_w_out, v_w_out)
    o_gate = _adamw([r[1] for r in rest], pl_gate_w, m_pl_gate_w, v_pl_gate_w)
    o_projw = _adamw([r[2] for r in rest], pl_proj_w, m_pl_proj_w, v_pl_proj_w)
    r_small = _exchange_wait(small_send, small_recv, small_srcs, small_lnds, o_projw[1], True, "scatter_small_wait")
    r_rep = _exchange_wait(rep_send, rep_recv, rep_srcs, rep_lnds, o_w_in[1], False, "gather_replicated_wait")

    row = lambda a: a.reshape(1, -1)
    o_small = _adamw_small(
        r_rep[:8] + r_small,
        [norm_g, conv_b, conv_ln_g, conv_ln_b, sgu_w, sgu_b, pl_norm_g, row(final_g), conv_w, sgu_ln_g, sgu_ln_b],
        [m_norm_g, m_conv_b, m_conv_ln_g, m_conv_ln_b, m_sgu_w, m_sgu_b, m_pl_norm_g, row(m_final_g), m_conv_w,
         m_sgu_ln_g, m_sgu_ln_b],
        [v_norm_g, v_conv_b, v_conv_ln_g, v_conv_ln_b, v_sgu_w, v_sgu_b, v_pl_norm_g, row(v_final_g), v_conv_w,
         v_sgu_ln_g, v_sgu_ln_b],
        r_rep[8])
    loss = o_small[4][0, 0]

    def leaf(kind):
        sm = o_small[kind]
        return [sm[0], o_w_in[kind], o_w_out[kind], sm[8], sm[1], sm[2], sm[3], sm[9], sm[10], sm[4], sm[5], sm[6],
                o_gate[kind], o_projw[kind], sm[7].reshape(final_g.shape)]

    return (loss, grad_x, *leaf(0), *leaf(1), *leaf(2), *leaf(3))
```

```python
import math

import jax
import jax.numpy as jnp
from jax import lax
from jax.experimental import pallas as pl
from jax.experimental.pallas import tpu as pltpu

F32 = jnp.float32
MXU_DTYPE = jnp.bfloat16
WIRE_DTYPE = jnp.bfloat16

EPS = 1e-6
CONV_K = 31
CHUNK = 128
GROUPS = 8
HALO = 32
N_DEV = 8
DEPTH = 4

ADAM_LR = 0.001
ADAM_B1 = 0.9
ADAM_B2 = 0.999
ADAM_EPS = 1e-08
ADAM_WD = 0.01
ADAM_STEP = 10

TM_IN = 512
TM_MIX = 256
TM_OUT = 512
FUSE_SB = 256
CONV_RC = 64
CONV_CC = 128
DCW_CC = 128
DCW_RC = 16
DCW_CHUNKS = 33
PACK_COLS = 1024

MESH_ID = pl.DeviceIdType.MESH
INV_SQRT2 = 1.0 / math.sqrt(2.0)
INV_SQRT_2PI = 1.0 / math.sqrt(2.0 * math.pi)


def _params(n_grid, vmem_mb):
    return pltpu.CompilerParams(dimension_semantics=("arbitrary",) * n_grid, vmem_limit_bytes=vmem_mb << 20)


def _whole(shape):
    nd = len(shape)
    return pl.BlockSpec(shape, lambda *_: (0,) * nd, pipeline_mode=pl.Buffered(1))


def _acc_out(shape):
    nd = len(shape)
    return pl.BlockSpec(shape, lambda *_: (0,) * nd)


def _dot(a, b):
    return jnp.dot(a.astype(MXU_DTYPE), b.astype(MXU_DTYPE), preferred_element_type=F32)


def _dot_nt(a, b):
    return lax.dot_general(a.astype(MXU_DTYPE), b.astype(MXU_DTYPE), (((1,), (1,)), ((), ())),
                           preferred_element_type=F32)


def _dot_tn(a, b):
    return lax.dot_general(a.astype(MXU_DTYPE), b.astype(MXU_DTYPE), (((0,), (0,)), ((), ())),
                           preferred_element_type=F32)


def _sigmoid(x):
    return jax.nn.sigmoid(x)


def _rms_rstd(x):
    return lax.rsqrt(jnp.mean(x * x, axis=-1, keepdims=True) + EPS)


def _rms_bwd(dy, x, rstd, g):
    gy = dy * g
    xr = x * rstd
    dx = rstd * (gy - xr * jnp.mean(gy * xr, axis=-1, keepdims=True))
    dg = jnp.sum(dy * xr, axis=0, keepdims=True)
    return dx, dg


def _ln_stats(x):
    mu = jnp.mean(x, axis=-1, keepdims=True)
    xc = x - mu
    var = jnp.mean(xc * xc, axis=-1, keepdims=True)
    rstd = lax.rsqrt(var + EPS)
    return xc * rstd, rstd


def _ln_bwd(dxhat, xhat, rstd):
    return rstd * (dxhat - jnp.mean(dxhat, axis=-1, keepdims=True)
                   - xhat * jnp.mean(dxhat * xhat, axis=-1, keepdims=True))


def _silu_grad(x, s):
    return s * (1.0 + x * (1.0 - s))


def _tril_mask():
    r = lax.broadcasted_iota(jnp.int32, (CHUNK, CHUNK), 0)
    c = lax.broadcasted_iota(jnp.int32, (CHUNK, CHUNK), 1)
    return r >= c


def _conv_weights_to_sublanes(w_ref, w8_ref):
    for k in range(CONV_K):
        w8_ref[k] = jnp.broadcast_to(w_ref[k:k + 1, :], w8_ref.shape[1:])


def _conv_apply(src_ref, w8_ref, zs_ref, base, tm, e, flip, emit):
    def row_block(i, carry):
        r0 = pl.multiple_of(i * CONV_RC, CONV_RC)
        for c0 in range(0, e, CONV_CC):
            cols = slice(c0, c0 + CONV_CC)
            acc = None
            for s in range(8):
                nrows = CONV_RC if s == 0 else CONV_RC + 8
                taps = [k for k in range(CONV_K) if (base + k) % 8 == s]
                off0 = base + taps[0] - s
                span = nrows + 8 * (len(taps) - 1)
                window = src_ref[pl.ds(r0 + off0, span), cols].reshape(span // 8, 8, CONV_CC)
                z = None
                for m, k in enumerate(taps):
                    wk = (CONV_K - 1 - k) if flip else k
                    term = w8_ref[wk, :, cols][None] * window[m:m + nrows // 8]
                    z = term if z is None else z + term
                z = z.reshape(nrows, CONV_CC)
                if s == 0:
                    acc = z
                else:
                    zs_ref[s - 1, pl.ds(0, nrows), :] = z
                    acc = acc + zs_ref[s - 1, pl.ds(s, CONV_RC), :]
            emit(r0, c0, acc)
        return carry

    lax.fori_loop(0, tm // CONV_RC, row_block, 0)


def _mesh_pos():
    return lax.axis_index("x"), lax.axis_index("y"), lax.axis_index("c")


def _slot(px, py, pc):
    return 4 * px + 2 * py + pc


def _peers(x, y, c):
    return [((1 - x) if (k & 4) else x, (1 - y) if (k & 2) else y, (1 - c) if (k & 1) else c)
            for k in range(1, N_DEV)]


HBM_SPEC = pl.BlockSpec(memory_space=pltpu.HBM)
SEM_SPEC = pl.BlockSpec(memory_space=pltpu.SEMAPHORE)
SIDE_EFFECT = pltpu.SideEffectType.DATAFLOW_SIDE_EFFECTING


def _exchange_copy(src_refs, land_refs, send_sems, recv_sems, i, k, peer, scatter, me):
    slot = _slot(*peer)
    return pltpu.make_async_remote_copy(
        src_ref=src_refs[i].at[slot] if scatter else src_refs[i],
        dst_ref=land_refs[i].at[me if me is not None else slot],
        send_sem=send_sems.at[i * 7 + k], recv_sem=recv_sems.at[i * 7 + k],
        device_id=peer, device_id_type=MESH_ID)


def _exchange_start(srcs, lands, after, scatter, name):
    n = len(srcs)

    def body(*refs):
        src_refs, land_refs = refs[:n], refs[n:2 * n]
        send_sems, recv_sems, token = refs[2 * n + 1], refs[2 * n + 2], refs[-1]
        x, y, c = _mesh_pos()
        me = _slot(x, y, c)
        for i in range(n):
            for k, peer in enumerate(_peers(x, y, c)):
                _exchange_copy(src_refs, land_refs, send_sems, recv_sems, i, k, peer, scatter, me).start()
        token[...] = jnp.zeros_like(token)

    arrays = list(srcs) + list(lands)
    outs = pl.pallas_call(
        body, name=name,
        out_shape=(pltpu.SemaphoreType.DMA((7 * n,)), pltpu.SemaphoreType.DMA((7 * n,)),
                   *[pltpu.HBM(a.shape, a.dtype) for a in lands], jax.ShapeDtypeStruct((8, 128), F32)),
        in_specs=[HBM_SPEC] * (2 * n) + [pl.BlockSpec(memory_space=pl.ANY)],
        out_specs=(SEM_SPEC, SEM_SPEC, *[HBM_SPEC] * n, pl.BlockSpec(memory_space=pltpu.VMEM)),
        input_output_aliases={n + i: 2 + i for i in range(n)},
        compiler_params=pltpu.CompilerParams(has_side_effects=SIDE_EFFECT),
    )(*arrays, after)
    return outs[0], outs[1], list(srcs), list(outs[2:2 + n]), outs[-1]


def _exchange_wait(send_sems, recv_sems, srcs, lands, after, scatter, name):
    n = len(srcs)

    def body(*refs):
        src_refs, land_refs = refs[:n], refs[n:2 * n]
        send, recv = refs[2 * n], refs[2 * n + 1]
        x, y, c = _mesh_pos()
        for i in range(n):
            for k, peer in enumerate(_peers(x, y, c)):
                cp = _exchange_copy(src_refs, land_refs, send, recv, i, k, peer, scatter, None)
                cp.wait_send()
                cp.wait_recv()

    arrays = list(srcs) + list(lands)
    outs = pl.pallas_call(
        body, name=name,
        out_shape=tuple(pltpu.HBM(a.shape, a.dtype) for a in lands),
        in_specs=[HBM_SPEC] * (2 * n) + [SEM_SPEC, SEM_SPEC, pl.BlockSpec(memory_space=pl.ANY)],
        out_specs=tuple([HBM_SPEC] * n),
        input_output_aliases={n + i: i for i in range(n)},
        compiler_params=pltpu.CompilerParams(has_side_effects=SIDE_EFFECT),
    )(*arrays, send_sems, recv_sems, after)
    return list(outs)


def _place_own(parts, scatter, name):
    n = len(parts)
    me = jnp.reshape(_slot(*_mesh_pos()), (1,)).astype(jnp.int32)

    def body(me_ref, *refs):
        for i in range(n):
            refs[n + i][0] = refs[i][0] if scatter else refs[i][...]

    def slot_spec(shape):
        rest = len(shape)
        return pl.BlockSpec((1,) + tuple(shape), lambda i, me_ref: (me_ref[0],) + (0,) * rest)

    def whole_spec(shape):
        nd = len(shape)
        return pl.BlockSpec(tuple(shape), lambda i, me_ref: (0,) * nd)

    blocks = [a.shape[1:] if scatter else a.shape for a in parts]
    return pl.pallas_call(
        body, name=name,
        grid_spec=pltpu.PrefetchScalarGridSpec(
            num_scalar_prefetch=1, grid=(1,),
            in_specs=[slot_spec(b) if scatter else whole_spec(b) for b in blocks],
            out_specs=[slot_spec(b) for b in blocks]),
        out_shape=[pltpu.HBM((N_DEV,) + tuple(b), a.dtype) for a, b in zip(parts, blocks)],
        compiler_params=_params(1, 32),
    )(me, *parts)


def _all_gather(items, name):
    n = len(items)

    def body(*refs):
        in_refs, out_refs = refs[:n], refs[n:2 * n]
        send_sems, recv_sems, local_sems = refs[2 * n:]
        x, y, c = _mesh_pos()
        me, sibling = (x, y, c), (x, y, 1 - c)
        chips = [(1 - x, y), (x, 1 - y), (1 - x, 1 - y)]

        def copy(i, k, block, to, src=None):
            dst = out_refs[i].at[_slot(*block)]
            return pltpu.make_async_remote_copy(
                src_ref=dst if src is None else src, dst_ref=dst,
                send_sem=send_sems.at[i * 7 + k], recv_sem=recv_sems.at[i * 7 + k],
                device_id=to, device_id_type=MESH_ID)

        mine = [pltpu.make_async_copy(in_refs[i], out_refs[i].at[_slot(*me)], local_sems.at[i]) for i in range(n)]
        for cp in mine:
            cp.start()
        first = []
        for i in range(n):
            first.append(copy(i, 0, me, sibling, src=in_refs[i]))
            for j, chip in enumerate(chips):
                first.append(copy(i, 1 + j, me, (*chip, c), src=in_refs[i]))
        for cp in first:
            cp.start()
        passed = []
        for j, chip in enumerate(chips):
            for i in range(n):
                copy(i, 1 + j, (*chip, c), me).wait_recv()
                fwd = copy(i, 4 + j, (*chip, c), sibling)
                fwd.start()
                passed.append(fwd)
        for i in range(n):
            copy(i, 0, sibling, me).wait_recv()
            for j, chip in enumerate(chips):
                copy(i, 4 + j, (*chip, 1 - c), me).wait_recv()
        for cp in first + passed:
            cp.wait_send()
        for cp in mine:
            cp.wait()

    any_spec = pl.BlockSpec(memory_space=pl.ANY)
    return pl.pallas_call(
        body, name=name,
        out_shape=[jax.ShapeDtypeStruct((N_DEV,) + a.shape, a.dtype) for a in items],
        in_specs=[any_spec] * n, out_specs=[any_spec] * n,
        scratch_shapes=[pltpu.SemaphoreType.DMA((7 * n,)), pltpu.SemaphoreType.DMA((7 * n,)),
                        pltpu.SemaphoreType.DMA((n,))],
    )(*items)


def _layer_fwd(x, p_all, layer, g, w_blk, w_out, plg, gate_w, proj_w, seq, conv=None, sgu=None):
    t, d = x.shape
    nb, _, bn = w_blk.shape
    e = w_out.shape[0]
    pd = p_all.shape[-1]
    tm = TM_MIX
    nt = seq // tm
    is_conv = conv is not None
    mixer_args = conv if is_conv else sgu
    n_mix = len(mixer_args)

    def body(*refs):
        x_ref, p_ref, g_ref, w_ref, wo_ref, plg_ref, gw_ref, pw_ref = refs[:8]
        mix = refs[8:8 + n_mix]
        outs = refs[8 + n_mix:]
        if is_conv:
            cw_ref, cb_ref, lg_ref, lb_ref = mix
            h_ref, proj_ref, y1_ref, x1_ref, gate_ref, x2_ref, y0s, zs, w8 = outs

            @pl.when(lax.rem(pl.program_id(0), nt) == 0)
            def _():
                y0s[pl.ds(0, HALO), :] = jnp.zeros((HALO, e), F32)

            @pl.when(pl.program_id(0) == 0)
            def _():
                _conv_weights_to_sublanes(cw_ref, w8)
        else:
            lg_ref, lb_ref, sw_ref, sbt_ref = mix
            h_ref, proj_ref, x1_ref, gate_ref, x2_ref, mixed_s = outs

        for sb in range(tm // FUSE_SB):
            rows = pl.ds(sb * FUSE_SB, FUSE_SB)
            xv = x_ref[rows, :]
            hv = (xv * _rms_rstd(xv) * g_ref[...]).astype(MXU_DTYPE)
            h_ref[rows, :] = hv
            for j in range(nb):
                proj_ref[rows, j * bn:(j + 1) * bn] = jnp.dot(hv, w_ref[j], preferred_element_type=F32)
            if is_conv:
                y0s[pl.ds(HALO + sb * FUSE_SB, FUSE_SB), :] = proj_ref[rows, 0:e] * _sigmoid(proj_ref[rows, e:2 * e])

        if is_conv:
            def emit(r0, c0, acc):
                y1_ref[pl.ds(r0, CONV_RC), c0:c0 + CONV_CC] = acc + cb_ref[:, c0:c0 + CONV_CC]
            _conv_apply(y0s, w8, zs, HALO - (CONV_K - 1), tm, e, False, emit)

        for sb in range(tm // FUSE_SB):
            rows = pl.ds(sb * FUSE_SB, FUSE_SB)
            if is_conv:
                xhat, _ = _ln_stats(y1_ref[rows, :])
                y2 = xhat * lg_ref[...] + lb_ref[...]
                y = y2 * _sigmoid(y2)
            else:
                _, _, u, _, _, _ = _sgu_parts(proj_ref[rows, 0:e], proj_ref[rows, e:2 * e], lg_ref[...], lb_ref[...],
                                              sw_ref, sbt_ref, mixed_s, FUSE_SB, e)
                y = u * mixed_s[...]
            z = proj_ref[rows, 2 * e:3 * e]
            q = (y * (z * _sigmoid(z))).astype(MXU_DTYPE)
            x1 = x_ref[rows, :] + jnp.dot(q, wo_ref[...], preferred_element_type=F32)
            x1_ref[rows, :] = x1
            rn = x1 * _rms_rstd(x1) * plg_ref[...]
            gate = _sigmoid(_dot(rn, gw_ref[...]))
            gate_ref[rows, :] = gate
            x2_ref[rows, :] = x1 + gate * _dot(p_ref[0, rows, :], pw_ref[...])

        if is_conv:
            y0s[pl.ds(0, HALO), :] = y0s[pl.ds(tm, HALO), :]

    row = lambda w: pl.BlockSpec((tm, w), lambda i: (i, 0))
    f32 = lambda w: jax.ShapeDtypeStruct((t, w), F32)
    out_shape = ([jax.ShapeDtypeStruct((t, d), MXU_DTYPE), f32(3 * e)] + ([f32(e)] if is_conv else [])
                 + [f32(d), f32(d), f32(d)])
    out_specs = [row(d), row(3 * e)] + ([row(e)] if is_conv else []) + [row(d), row(d), row(d)]
    scratch = ([pltpu.VMEM((tm + HALO, e), F32), pltpu.VMEM((7, CONV_RC + 8, CONV_CC), F32),
                pltpu.VMEM((CONV_K, 8, e), F32)] if is_conv else [pltpu.VMEM((FUSE_SB, e), F32)])
    return pl.pallas_call(
        body, name="layer_fwd_conv" if is_conv else "layer_fwd_sgu", grid=(t // tm,),
        in_specs=[row(d), pl.BlockSpec((1, tm, pd), lambda i: (layer, i, 0)), _whole((1, d)), _whole(w_blk.shape),
                  _whole((e, d)), _whole((1, d)),
                  _whole((d, d)), _whole((pd, d))] + [_whole(a.shape) for a in mixer_args],
        out_specs=out_specs, out_shape=out_shape, scratch_shapes=scratch,
        compiler_params=_params(1, 60),
    )(x, p_all, g, w_blk, w_out, plg, gate_w, proj_w, *mixer_args)


def _sgu_parts(a, b, lg, lb, sw_ref, sbt_ref, mixed_s, tm, e):
    eg = e // GROUPS
    ea = lax.erf(a * INV_SQRT2)
    eb = lax.erf(b * INV_SQRT2)
    u = 0.5 * a * (1.0 + ea)
    v0 = 0.5 * b * (1.0 + eb)
    xhat, rstd = _ln_stats(v0)
    v = (xhat * lg + lb).astype(MXU_DTYPE)
    mask = _tril_mask()
    for g in range(GROUPS):
        wt = jnp.where(mask, sw_ref[g], 0.0).astype(MXU_DTYPE)
        bcol = sbt_ref[:, g:g + 1]
        for ch in range(tm // CHUNK):
            rows = slice(ch * CHUNK, (ch + 1) * CHUNK)
            cols = slice(g * eg, (g + 1) * eg)
            mixed_s[rows, cols] = jnp.dot(wt, v[rows, cols], preferred_element_type=F32) + bcol
    return ea, eb, u, xhat, rstd, v


def _loss_head(xf, fg, tgt):
    t, d = xf.shape
    tm = TM_OUT
    nsteps = t // tm

    def body(x_ref, g_ref, t_ref, loss_ref, dx_ref, dg_ref, sq_s):
        i = pl.program_id(0)

        @pl.when(i == 0)
        def _():
            sq_s[...] = jnp.zeros_like(sq_s)
            dg_ref[...] = jnp.zeros_like(dg_ref)
        x = x_ref[...]
        rstd = _rms_rstd(x)
        err = x * rstd * g_ref[...] - t_ref[...]
        sq_s[...] += jnp.sum(err * err, axis=0, keepdims=True)
        dx, dg = _rms_bwd(err * (1.0 / d), x, rstd, g_ref[...])
        dx_ref[...] = dx
        dg_ref[...] += dg

        @pl.when(i == nsteps - 1)
        def _():
            loss_ref[...] = jnp.sum(sq_s[...], axis=1, keepdims=True) * (0.5 / d)

    row = pl.BlockSpec((tm, d), lambda i: (i, 0))
    return pl.pallas_call(
        body, name="loss_head", grid=(nsteps,),
        in_specs=[row, _whole((1, d)), row],
        out_specs=[_acc_out((1, 1)), row, _acc_out((1, d))],
        out_shape=[jax.ShapeDtypeStruct((1, 1), F32), jax.ShapeDtypeStruct((t, d), F32),
                   jax.ShapeDtypeStruct((1, d), F32)],
        scratch_shapes=[pltpu.VMEM((1, d), F32)],
        compiler_params=_params(1, 32),
    )(xf, fg, tgt)


def _ple_bwd(dx2, x1, gate, p_all, layer, plg, gate_w, proj_w):
    t, d = x1.shape
    pd = p_all.shape[-1]
    tm = TM_OUT
    nsteps = t // tm
    bn = d // N_DEV

    def body(dx2_ref, x1_ref, gate_ref, p_ref, plg_ref, gw_ref, pw_ref, dx1_ref, dgw_ref, dpw_ref, dplg_ref,
             gw_acc, pw_acc):
        i = pl.program_id(0)

        @pl.when(i == 0)
        def _():
            gw_acc[...] = jnp.zeros_like(gw_acc)
            pw_acc[...] = jnp.zeros_like(pw_acc)
            dplg_ref[...] = jnp.zeros_like(dplg_ref)
        dx2 = dx2_ref[...]
        x1 = x1_ref[...]
        plg = plg_ref[...]
        rstd = _rms_rstd(x1)
        rn = (x1 * rstd * plg).astype(MXU_DTYPE)
        gate = gate_ref[...]
        p_b = p_ref[0].astype(MXU_DTYPE)
        pp = jnp.dot(p_b, pw_ref[...], preferred_element_type=F32)
        dpp = (dx2 * gate).astype(MXU_DTYPE)
        dgpre = (dx2 * pp * gate * (1.0 - gate)).astype(MXU_DTYPE)
        pw_acc[...] += _dot_tn(p_b, dpp)
        gw_acc[...] += _dot_tn(rn, dgpre)
        drn = _dot_nt(dgpre, gw_ref[...])
        dx, dg = _rms_bwd(drn, x1, rstd, plg)
        dx1_ref[...] = dx2 + dx
        dplg_ref[...] += dg

        @pl.when(i == nsteps - 1)
        def _():
            dgw_ref[...] = gw_acc[...].astype(dgw_ref.dtype)
            for j in range(N_DEV):
                dpw_ref[j] = pw_acc[:, j * bn:(j + 1) * bn].astype(dpw_ref.dtype)

    row = lambda w: pl.BlockSpec((tm, w), lambda i: (i, 0))
    return pl.pallas_call(
        body, name="ple_bwd", grid=(nsteps,),
        in_specs=[row(d), row(d), row(d), pl.BlockSpec((1, tm, pd), lambda i: (layer, i, 0)), _whole((1, d)),
                  _whole((d, d)), _whole((pd, d))],
        out_specs=[row(d), _acc_out((d, d)), _acc_out((N_DEV, pd, bn)), _acc_out((1, d))],
        out_shape=[jax.ShapeDtypeStruct((t, d), F32), pltpu.HBM((d, d), WIRE_DTYPE),
                   pltpu.HBM((N_DEV, pd, bn), WIRE_DTYPE), jax.ShapeDtypeStruct((1, d), F32)],
        scratch_shapes=[pltpu.VMEM((d, d), F32), pltpu.VMEM((pd, d), F32)],
        compiler_params=_params(1, 48),
    )(dx2, x1, gate, p_all, plg, gate_w, proj_w)


def _outproj_conv_bwd(dx1, y1, proj, w_out, lg, lb):
    t, d = dx1.shape
    e = y1.shape[1]
    tm = TM_MIX
    nsteps = t // tm

    def body(dx1_ref, y1_ref, z_ref, wo_ref, lg_ref, lb_ref, dy1_ref, dz_ref, dwo_ref, dlg_ref, dlb_ref, dcb_ref,
             wo_acc):
        i = pl.program_id(0)

        @pl.when(i == 0)
        def _():
            wo_acc[...] = jnp.zeros_like(wo_acc)
            dlg_ref[...] = jnp.zeros_like(dlg_ref)
            dlb_ref[...] = jnp.zeros_like(dlb_ref)
            dcb_ref[...] = jnp.zeros_like(dcb_ref)
        xhat, rstd = _ln_stats(y1_ref[...])
        lg = lg_ref[...]
        y2 = xhat * lg + lb_ref[...]
        s2 = _sigmoid(y2)
        y = y2 * s2
        z = z_ref[...]
        s = _sigmoid(z)
        sz = z * s
        dx1 = dx1_ref[...].astype(MXU_DTYPE)
        wo_acc[...] += _dot_tn((y * sz).astype(MXU_DTYPE), dx1)
        dq = _dot_nt(dx1, wo_ref[...])
        dz_ref[...] = (dq * y * _silu_grad(z, s)).astype(dz_ref.dtype)
        dy2 = dq * sz * _silu_grad(y2, s2)
        dlg_ref[...] += jnp.sum(dy2 * xhat, axis=0, keepdims=True)
        dlb_ref[...] += jnp.sum(dy2, axis=0, keepdims=True)
        dy1 = _ln_bwd(dy2 * lg, xhat, rstd)
        dy1_ref[...] = dy1
        dcb_ref[...] += jnp.sum(dy1, axis=0, keepdims=True)

        @pl.when(i == nsteps - 1)
        def _():
            dwo_ref[...] = wo_acc[...].astype(dwo_ref.dtype)

    row = lambda w: pl.BlockSpec((tm, w), lambda i: (i, 0))
    return pl.pallas_call(
        body, name="outproj_conv_bwd", grid=(nsteps,),
        in_specs=[row(d), row(e), pl.BlockSpec((tm, e), lambda i: (i, 2)), _whole((e, d)), _whole((1, e)),
                  _whole((1, e))],
        out_specs=[row(e), row(e), _acc_out((e, d)), _acc_out((1, e)), _acc_out((1, e)), _acc_out((1, e))],
        out_shape=[jax.ShapeDtypeStruct((t, e), F32), jax.ShapeDtypeStruct((t, e), MXU_DTYPE),
                   pltpu.HBM((e, d), WIRE_DTYPE)] + [jax.ShapeDtypeStruct((1, e), F32)] * 3,
        scratch_shapes=[pltpu.VMEM((e, d), F32)],
        compiler_params=_params(1, 56),
    )(dx1, y1, proj, w_out, lg, lb)


def _conv_bwd(dy1, proj, dz, cw, seq):
    t, e = dy1.shape
    tm = TM_MIX
    nt = seq // tm
    hb = tm // HALO
    n_halo_blocks = t // HALO

    def body(d_ref, dn_ref, a_ref, b_ref, ah_ref, bh_ref, dz_ref, cw_ref, dproj_ref, dcw_ref,
             y0s, d1s, zs, dsh, dcw8, w8):
        i = pl.program_id(0)
        pos = lax.rem(i, nt)

        @pl.when(i == 0)
        def _():
            dcw8[...] = jnp.zeros_like(dcw8)
            _conv_weights_to_sublanes(cw_ref, w8)
        a = a_ref[...]
        sb = _sigmoid(b_ref[...])
        y0s[pl.ds(HALO, tm), :] = a * sb
        d1s[pl.ds(0, tm), :] = d_ref[...]

        @pl.when(pos == 0)
        def _():
            y0s[pl.ds(0, HALO), :] = jnp.zeros((HALO, e), F32)

        @pl.when(pos != 0)
        def _():
            y0s[pl.ds(0, HALO), :] = ah_ref[...] * _sigmoid(bh_ref[...])

        @pl.when(pos == nt - 1)
        def _():
            d1s[pl.ds(tm, HALO), :] = jnp.zeros((HALO, e), F32)

        @pl.when(pos != nt - 1)
        def _():
            d1s[pl.ds(tm, HALO), :] = dn_ref[...]

        base = HALO - (CONV_K - 1)
        for c0 in range(0, e, DCW_CC):
            cols = slice(c0, c0 + DCW_CC)
            dcur = d_ref[:, cols]
            for s in range(1, 8):
                dsh[s - 1, pl.ds(0, 8), :] = jnp.zeros((8, DCW_CC), F32)
                dsh[s - 1, pl.ds(tm, 8), :] = jnp.zeros((8, DCW_CC), F32)
                dsh[s - 1, pl.ds(s, tm), :] = dcur
            for s in range(8):
                taps = [k for k in range(CONV_K) if (base + k) % 8 == s]
                off0 = base + taps[0] - s
                n, ch = (tm, DCW_RC) if s == 0 else (tm + 8, (tm + 8) // DCW_CHUNKS)
                sums = [None] * len(taps)
                for r in range(0, n, ch):
                    dch = d_ref[r:r + ch, cols] if s == 0 else dsh[s - 1, r:r + ch, :]
                    window = y0s[pl.ds(off0 + r, ch + 8 * (len(taps) - 1)), cols]
                    for m in range(len(taps)):
                        part = jnp.sum((dch * window[8 * m:8 * m + ch]).reshape(ch // 8, 8, DCW_CC), axis=0)
                        sums[m] = part if sums[m] is None else sums[m] + part
                for m, k in enumerate(taps):
                    dcw8[k, :, cols] += sums[m]

        def emit(r0, c0, dy0):
            rs, cs = pl.ds(r0, CONV_RC), slice(c0, c0 + CONV_CC)
            sbv = _sigmoid(b_ref[rs, cs])
            av = a_ref[rs, cs]
            dproj_ref[rs, c0:c0 + CONV_CC] = (dy0 * sbv).astype(dproj_ref.dtype)
            dproj_ref[rs, e + c0:e + c0 + CONV_CC] = (dy0 * av * sbv * (1.0 - sbv)).astype(dproj_ref.dtype)
        _conv_apply(d1s, w8, zs, 0, tm, e, True, emit)
        dproj_ref[:, 2 * e:3 * e] = dz_ref[...]

        @pl.when(i == t // tm - 1)
        def _():
            dcw_ref[...] = jnp.sum(dcw8[...], axis=1)

    tile = lambda col: pl.BlockSpec((tm, e), lambda i: (i, col))
    prev = lambda col: pl.BlockSpec((HALO, e), lambda i: (jnp.maximum(i * hb - 1, 0), col))
    nxt = pl.BlockSpec((HALO, e), lambda i: (jnp.minimum((i + 1) * hb, n_halo_blocks - 1), 0))
    return pl.pallas_call(
        body, name="conv_bwd", grid=(t // tm,),
        in_specs=[tile(0), nxt, tile(0), tile(1), prev(0), prev(1), tile(0), _whole(cw.shape)],
        out_specs=[pl.BlockSpec((tm, 3 * e), lambda i: (i, 0)), _acc_out(cw.shape)],
        out_shape=[jax.ShapeDtypeStruct((t, 3 * e), MXU_DTYPE), jax.ShapeDtypeStruct(cw.shape, F32)],
        scratch_shapes=[pltpu.VMEM((tm + HALO, e), F32), pltpu.VMEM((tm + HALO, e), F32),
                        pltpu.VMEM((7, CONV_RC + 8, CONV_CC), F32), pltpu.VMEM((7, tm + 8, DCW_CC), F32),
                        pltpu.VMEM((CONV_K, 8, e), F32), pltpu.VMEM((CONV_K, 8, e), F32)],
        compiler_params=_params(1, 56),
    )(dy1, dy1, proj, proj, proj, proj, dz, cw)


def _outproj_sgu_bwd(dx1, proj, w_out, lg, lb, sw, swt, sbt):
    t, d = dx1.shape
    e = w_out.shape[0]
    eg = e // GROUPS
    tm = TM_MIX
    nsteps = t // tm

    def body(dx1_ref, a_ref, b_ref, z_ref, wo_ref, lg_ref, lb_ref, sw_ref, swt_ref, sbt_ref,
             dproj_ref, dwo_ref, dsw_ref, dsb_ref, dlg_ref, dlb_ref, mixed_s, dv_s, sb_acc, wo_acc):
        i = pl.program_id(0)

        @pl.when(i == 0)
        def _():
            wo_acc[...] = jnp.zeros_like(wo_acc)
            dsw_ref[...] = jnp.zeros_like(dsw_ref)
            sb_acc[...] = jnp.zeros_like(sb_acc)
            dlg_ref[...] = jnp.zeros_like(dlg_ref)
            dlb_ref[...] = jnp.zeros_like(dlb_ref)
        a = a_ref[...]
        b = b_ref[...]
        lg = lg_ref[...]
        ea, eb, u, xhat, rstd, v = _sgu_parts(a, b, lg, lb_ref[...], sw_ref, sbt_ref, mixed_s, tm, e)
        mixed = mixed_s[...]
        y = u * mixed
        z = z_ref[...]
        s = _sigmoid(z)
        sz = z * s
        dx1 = dx1_ref[...].astype(MXU_DTYPE)
        wo_acc[...] += _dot_tn((y * sz).astype(MXU_DTYPE), dx1)
        dq = _dot_nt(dx1, wo_ref[...])
        dproj_ref[:, 2 * e:3 * e] = (dq * y * _silu_grad(z, s)).astype(dproj_ref.dtype)
        dy = dq * sz
        du = dy * mixed
        dmixed = (dy * u).astype(MXU_DTYPE)
        mask = _tril_mask()
        mask_t = (lax.broadcasted_iota(jnp.int32, (CHUNK, CHUNK), 0)
                  <= lax.broadcasted_iota(jnp.int32, (CHUNK, CHUNK), 1))
        ones = jnp.ones((8, eg), MXU_DTYPE)
        for g in range(GROUPS):
            wtt = jnp.where(mask_t, swt_ref[g], 0.0).astype(MXU_DTYPE)
            cols = slice(g * eg, (g + 1) * eg)
            for ch in range(tm // CHUNK):
                rows = slice(ch * CHUNK, (ch + 1) * CHUNK)
                dm = dmixed[rows, cols]
                dv_s[rows, cols] = jnp.dot(wtt, dm, preferred_element_type=F32)
                dsw_ref[g] += _dot_nt(dm, v[rows, cols])
                sb_acc[g] += _dot_nt(ones, dm)
        dv = dv_s[...]
        dlg_ref[...] += jnp.sum(dv * xhat, axis=0, keepdims=True)
        dlb_ref[...] += jnp.sum(dv, axis=0, keepdims=True)
        dv0 = _ln_bwd(dv * lg, xhat, rstd)
        pdf_a = jnp.exp(-0.5 * a * a) * INV_SQRT_2PI
        pdf_b = jnp.exp(-0.5 * b * b) * INV_SQRT_2PI
        dproj_ref[:, 0:e] = (du * (0.5 * (1.0 + ea) + a * pdf_a)).astype(dproj_ref.dtype)
        dproj_ref[:, e:2 * e] = (dv0 * (0.5 * (1.0 + eb) + b * pdf_b)).astype(dproj_ref.dtype)

        @pl.when(i == nsteps - 1)
        def _():
            dwo_ref[...] = wo_acc[...].astype(dwo_ref.dtype)
            for g in range(GROUPS):
                dsw_ref[g] = jnp.where(mask, dsw_ref[g], 0.0)
                dsb_ref[g:g + 1, :] = sb_acc[g, 0:1, :]

    tile = lambda col: pl.BlockSpec((tm, e), lambda i: (i, col))
    return pl.pallas_call(
        body, name="outproj_sgu_bwd", grid=(nsteps,),
        in_specs=[pl.BlockSpec((tm, d), lambda i: (i, 0)), tile(0), tile(1), tile(2), _whole((e, d)), _whole((1, e)),
                  _whole((1, e)), _whole(sw.shape), _whole(swt.shape), _whole(sbt.shape)],
        out_specs=[pl.BlockSpec((tm, 3 * e), lambda i: (i, 0)), _acc_out((e, d)), _acc_out(sw.shape),
                   _acc_out((GROUPS, CHUNK)), _acc_out((1, e)), _acc_out((1, e))],
        out_shape=[jax.ShapeDtypeStruct((t, 3 * e), MXU_DTYPE), pltpu.HBM((e, d), WIRE_DTYPE),
                   jax.ShapeDtypeStruct(sw.shape, F32), jax.ShapeDtypeStruct((GROUPS, CHUNK), F32),
                   jax.ShapeDtypeStruct((1, e), F32), jax.ShapeDtypeStruct((1, e), F32)],
        scratch_shapes=[pltpu.VMEM((tm, e), F32), pltpu.VMEM((tm, e), F32), pltpu.VMEM((GROUPS, 8, CHUNK), F32),
                        pltpu.VMEM((e, d), F32)],
        compiler_params=_params(1, 60),
    )(dx1, proj, proj, proj, w_out, lg, lb, sw, swt, sbt)


def _inproj_bwd_x(dproj, w_blk, dx1, x, g):
    t, d = x.shape
    nb, _, bn = w_blk.shape
    tm = TM_IN

    def body(dp_ref, w_ref, dx1_ref, x_ref, g_ref, dx_ref, dg_ref):
        @pl.when(pl.program_id(0) == 0)
        def _():
            dg_ref[...] = jnp.zeros_like(dg_ref)
        dh = None
        for j in range(nb):
            term = _dot_nt(dp_ref[:, j * bn:(j + 1) * bn], w_ref[j])
            dh = term if dh is None else dh + term
        xv = x_ref[...]
        dx, dg = _rms_bwd(dh, xv, _rms_rstd(xv), g_ref[...])
        dx_ref[...] = dx1_ref[...] + dx
        dg_ref[...] += dg

    row = lambda w: pl.BlockSpec((tm, w), lambda i: (i, 0))
    return pl.pallas_call(
        body, name="inproj_bwd_x", grid=(t // tm,),
        in_specs=[row(nb * bn), _whole(w_blk.shape), row(d), row(d), _whole((1, d))],
        out_specs=[row(d), _acc_out((1, d))],
        out_shape=[jax.ShapeDtypeStruct((t, d), F32), jax.ShapeDtypeStruct((1, d), F32)],
        compiler_params=_params(1, 56),
    )(dproj, w_blk, dx1, x, g)


def _inproj_bwd_w(h, dproj):
    t, d = h.shape
    bn = dproj.shape[1] // N_DEV
    tm = TM_IN
    nsteps = t // tm

    nh = 2
    per = N_DEV // nh

    def body(h_ref, dp_ref, dw_ref, acc):
        i = pl.program_id(1)

        @pl.when(i == 0)
        def _():
            acc[...] = jnp.zeros_like(acc)
        hv = h_ref[...]
        for jj in range(per):
            acc[jj] += _dot_tn(hv, dp_ref[:, jj * bn:(jj + 1) * bn])

        @pl.when(i == nsteps - 1)
        def _():
            dw_ref[...] = acc[...].astype(dw_ref.dtype)

    return pl.pallas_call(
        body, name="inproj_bwd_w", grid=(nh, nsteps),
        in_specs=[pl.BlockSpec((tm, d), lambda hh, i: (i, 0)), pl.BlockSpec((tm, per * bn), lambda hh, i: (i, hh))],
        out_specs=[pl.BlockSpec((per, d, bn), lambda hh, i: (hh, 0, 0))],
        out_shape=[pltpu.HBM((N_DEV, d, bn), WIRE_DTYPE)],
        scratch_shapes=[pltpu.VMEM((per, d, bn), F32)],
        compiler_params=_params(2, 56),
    )(h, dproj)[0]


def _adamw(parts, w, m, v):
    nl, r, c = w.shape
    tr = r
    for cand in (512, 256, 128, 64, 32, 16, 8):
        if r % cand == 0 and cand * c * 4 <= (1 << 19):
            tr = cand
            break
    bc1 = 1.0 - ADAM_B1 ** ADAM_STEP
    bc2 = 1.0 - ADAM_B2 ** ADAM_STEP

    def body(*refs):
        p_refs = refs[:nl]
        w_ref, m_ref, v_ref, g_ref, d_ref, nm_ref, nv_ref = refs[nl:]

        def update(p_ref):
            g = p_ref[0].astype(F32)
            for s in range(1, N_DEV):
                g = g + p_ref[s].astype(F32)
            nm = ADAM_B1 * m_ref[0] + (1.0 - ADAM_B1) * g
            nv = ADAM_B2 * v_ref[0] + (1.0 - ADAM_B2) * (g * g)
            g_ref[0] = g
            nm_ref[0] = nm
            nv_ref[0] = nv
            d_ref[0] = -ADAM_LR * ((nm / bc1) / (jnp.sqrt(nv / bc2) + ADAM_EPS) + ADAM_WD * w_ref[0])

        if nl == 1:
            update(p_refs[0])
        else:
            for kk in range(nl):
                pl.when(pl.program_id(0) == kk)(lambda kk=kk: update(p_refs[kk]))

    part_spec = lambda kk: pl.BlockSpec((N_DEV, tr, c), lambda l, i: (0, jnp.where(l == kk, i, 0), 0))
    row = pl.BlockSpec((1, tr, c), lambda l, i: (l, i, 0))
    return pl.pallas_call(
        body, name="adamw", grid=(nl, r // tr),
        in_specs=[part_spec(kk) for kk in range(nl)] + [row, row, row],
        out_specs=[row] * 4,
        out_shape=[jax.ShapeDtypeStruct((nl, r, c), F32)] * 4,
        compiler_params=_params(2, 48),
    )(*parts, w, m, v)


def _adamw_small(parts, ws, ms, vs, loss_parts):
    n = len(ws)
    bc1 = 1.0 - ADAM_B1 ** ADAM_STEP
    bc2 = 1.0 - ADAM_B2 ** ADAM_STEP

    def total(ref):
        acc = ref[0]
        for s in range(1, N_DEV):
            acc = acc + ref[s]
        return acc

    def body(*refs):
        p_refs, w_refs, m_refs, v_refs = refs[:n], refs[n:2 * n], refs[2 * n:3 * n], refs[3 * n:4 * n]
        outs = refs[4 * n + 1:]
        for i in range(n):
            g = total(p_refs[i])
            nm = ADAM_B1 * m_refs[i][...] + (1.0 - ADAM_B1) * g
            nv = ADAM_B2 * v_refs[i][...] + (1.0 - ADAM_B2) * (g * g)
            outs[i][...] = g
            outs[n + i][...] = -ADAM_LR * ((nm / bc1) / (jnp.sqrt(nv / bc2) + ADAM_EPS) + ADAM_WD * w_refs[i][...])
            outs[2 * n + i][...] = nm
            outs[3 * n + i][...] = nv
        outs[4 * n][...] = total(refs[4 * n])

    vmem = pl.BlockSpec(memory_space=pltpu.VMEM)
    shapes = [jax.ShapeDtypeStruct(w.shape, F32) for w in ws]
    outs = pl.pallas_call(
        body, name="adamw_small",
        in_specs=[vmem] * (4 * n + 1), out_specs=[vmem] * (4 * n + 1),
        out_shape=shapes * 4 + [jax.ShapeDtypeStruct((8, 128), F32)],
        compiler_params=pltpu.CompilerParams(vmem_limit_bytes=48 << 20),
    )(*parts, *ws, *ms, *vs, loss_parts)
    return outs[:n], outs[n:2 * n], outs[2 * n:3 * n], outs[3 * n:4 * n], outs[4 * n]


def _pack(arrays):
    flat = jnp.concatenate([a.reshape(-1).astype(F32) for a in arrays])
    unit = 8 * PACK_COLS
    padded = -(-flat.shape[0] // unit) * unit
    return jnp.pad(flat, (0, padded - flat.shape[0])).reshape(-1, PACK_COLS)


def _unshard_last(packed, shard_shapes):
    flat = packed.reshape(N_DEV, -1)
    out, off = [], 0
    for s in shard_shapes:
        n = math.prod(s)
        a = jnp.moveaxis(flat[:, off:off + n].reshape((N_DEV,) + tuple(s)), 0, -2)
        out.append(a.reshape(tuple(s[:-1]) + (N_DEV * s[-1],)))
        off += n
    return out


def kernel(x, p, norm_g, w_in, w_out, conv_w, conv_b, conv_ln_g, conv_ln_b, sgu_ln_g, sgu_ln_b, sgu_w, sgu_b, pl_norm_g, pl_gate_w, pl_proj_w, final_g, loss_target, m_norm_g, m_w_in, m_w_out, m_conv_w, m_conv_b, m_conv_ln_g, m_conv_ln_b, m_sgu_ln_g, m_sgu_ln_b, m_sgu_w, m_sgu_b, m_pl_norm_g, m_pl_gate_w, m_pl_proj_w, m_final_g, v_norm_g, v_w_in, v_w_out, v_conv_w, v_conv_b, v_conv_ln_g, v_conv_ln_b, v_sgu_ln_g, v_sgu_ln_b, v_sgu_w, v_sgu_b, v_pl_norm_g, v_pl_gate_w, v_pl_proj_w, v_final_g):
    bsz, seq, d = x.shape
    t = bsz * seq
    depth = w_in.shape[0]
    e = w_out.shape[1] * N_DEV
    pd = p.shape[-1]
    n_conv, n_sgu = conv_w.shape[0], sgu_ln_g.shape[0]

    small_shapes = [conv_w.shape, sgu_ln_g.shape, sgu_ln_b.shape]
    cast = lambda a: a.astype(MXU_DTYPE)
    first = [cast(w_in[0]), cast(w_out[0]), cast(pl_gate_w[0]), cast(pl_proj_w), _pack([conv_w, sgu_ln_g, sgu_ln_b])]
    later = [[cast(w_in[l]), cast(w_out[l]), cast(pl_gate_w[l])] for l in range(1, depth)]
    gathered = _all_gather(first, "gather_weights")
    gather_pending, gather_tokens = {}, 0.0
    for l in range(1, depth):
        lands = _place_own(later[l - 1], False, "place_own_weights")
        send, recv, srcs, lnds, token = _exchange_start(later[l - 1], lands, gathered[0], False, f"gather_start_{l}")
        gather_pending[l] = (send, recv, srcs, lnds)
        gather_tokens = gather_tokens + token[0, 0]
    w_in_g = {0: gathered[0]}
    w_out_g = {0: gathered[1].reshape(e, d)}
    gate_g = {0: gathered[2].reshape(d, d)}
    proj_g = jnp.transpose(gathered[3], (1, 2, 0, 3)).reshape(depth, pd, d)
    conv_w_g, sgu_ln_g_g, sgu_ln_b_g = _unshard_last(gathered[4], small_shapes)
    sgu_wt = jnp.swapaxes(sgu_w, -1, -2)
    sgu_bt = jnp.swapaxes(sgu_b, -1, -2)

    xs = [x.reshape(t, d)]
    p_all = p.reshape(depth, t, pd)
    saved = []
    for l in range(depth):
        j = l // 2
        if l == 0:
            g_l = norm_g[0:1] + gather_tokens
        else:
            g_l = norm_g[l:l + 1]
            got = _exchange_wait(*gather_pending.pop(l), xs[-1], False, f"gather_wait_{l}")
            w_in_g[l], w_out_g[l], gate_g[l] = got[0], got[1].reshape(e, d), got[2].reshape(d, d)
        common = (xs[-1], p_all, l, g_l, w_in_g[l], w_out_g[l], pl_norm_g[l:l + 1], gate_g[l], proj_g[l], seq)
        if l % 2 == 0:
            h, proj, y1, x1, gate, x2 = _layer_fwd(
                *common, conv=(conv_w_g[j], conv_b[j:j + 1], conv_ln_g[j:j + 1], conv_ln_b[j:j + 1]))
        else:
            y1 = None
            h, proj, x1, gate, x2 = _layer_fwd(
                *common, sgu=(sgu_ln_g_g[j:j + 1], sgu_ln_b_g[j:j + 1], sgu_w[j], sgu_bt[j]))
        saved.append((h, proj, y1, x1, gate))
        xs.append(x2)

    loss_part, dx, d_final_g = _loss_head(xs[-1], final_g.reshape(1, d), loss_target.reshape(t, d))

    d_norm_g, d_pl_norm_g = [None] * depth, [None] * depth
    scatter_pending = {}
    d_conv_w, d_conv_b, d_conv_ln_g, d_conv_ln_b = [None] * n_conv, [None] * n_conv, [None] * n_conv, [None] * n_conv
    d_sgu_ln_g, d_sgu_ln_b, d_sgu_w, d_sgu_b = [None] * n_sgu, [None] * n_sgu, [None] * n_sgu, [None] * n_sgu
    def scatter(parts, name):
        send, recv, srcs, lnds, token = _exchange_start(parts, _place_own(parts, True, "place_own_grads"), parts[0],
                                                        True, name)
        return (send, recv, srcs, lnds), token[0, 0]

    for l in reversed(range(depth)):
        j = l // 2
        h, proj, y1, x1, gate = saved[l]
        dx1, dgate_p, dprojw_p, d_pl_norm_g[l] = _ple_bwd(dx, x1, gate, p_all, l, pl_norm_g[l:l + 1], gate_g[l],
                                                         proj_g[l])
        if l % 2 == 0:
            dy1, dz, dw_out_p, d_conv_ln_g[j], d_conv_ln_b[j], d_conv_b[j] = _outproj_conv_bwd(
                dx1, y1, proj, w_out_g[l], conv_ln_g[j:j + 1], conv_ln_b[j:j + 1])
        else:
            dproj, dw_out_p, d_sgu_w[j], d_sgu_b[j], d_sgu_ln_g[j], d_sgu_ln_b[j] = _outproj_sgu_bwd(
                dx1, proj, w_out_g[l], sgu_ln_g_g[j:j + 1], sgu_ln_b_g[j:j + 1], sgu_w[j], sgu_wt[j], sgu_bt[j])
        early = [dw_out_p.reshape(N_DEV, e // N_DEV, d), dgate_p.reshape(N_DEV, d // N_DEV, d), dprojw_p]
        early_token = 0.0
        if l == 0:
            scatter_pending["0_early"], early_token = scatter(early, "scatter_start_0_early")
            early = []
        if l % 2 == 0:
            dproj, d_conv_w[j] = _conv_bwd(dy1, proj, dz, conv_w_g[j] + early_token, seq)
        scatter_pending[l], token = scatter([_inproj_bwd_w(h, dproj)] + early, f"scatter_start_{l}")
        dx, d_norm_g[l] = _inproj_bwd_x(dproj, w_in_g[l], dx1, xs[l], norm_g[l:l + 1] + token)
    grad_x = dx.reshape(bsz, seq, d)

    def own_eighths(full):
        return jnp.moveaxis(full.reshape(full.shape[:-1] + (N_DEV, full.shape[-1] // N_DEV)), -2, 0)

    small_parts = [own_eighths(jnp.stack(d_conv_w)), own_eighths(jnp.concatenate(d_sgu_ln_g, axis=0)),
                   own_eighths(jnp.concatenate(d_sgu_ln_b, axis=0))]
    rep_parts = [jnp.concatenate(d_norm_g, axis=0), jnp.concatenate(d_conv_b, axis=0),
                 jnp.concatenate(d_conv_ln_g, axis=0), jnp.concatenate(d_conv_ln_b, axis=0), jnp.stack(d_sgu_w),
                 jnp.stack(d_sgu_b), jnp.concatenate(d_pl_norm_g, axis=0), d_final_g,
                 jnp.broadcast_to(loss_part, (8, 128))]
    small_send, small_recv, small_srcs, small_lnds, small_token = _exchange_start(
        small_parts, _place_own(small_parts, True, "place_own_small"), grad_x, True, "scatter_small_start")
    rep_send, rep_recv, rep_srcs, rep_lnds, rep_token = _exchange_start(
        rep_parts, _place_own(rep_parts, False, "place_own_replicated"), grad_x, False, "gather_replicated_start")

    landed = {}
    for key in list(scatter_pending):
        landed[key] = _exchange_wait(*scatter_pending.pop(key), small_token + rep_token, True, f"scatter_wait_{key}")
    dw_in_l = [landed[l][0] for l in range(depth)]
    rest = [landed["0_early"]] + [landed[l][1:] for l in range(1, depth)]

    o_w_in = _adamw(dw_in_l, w_in, m_w_in, v_w_in)
    o_w_out = _adamw([r[0] for r in rest], w_out, m_w_out, v_w_out)
    o_gate = _adamw([r[1] for r in rest], pl_gate_w, m_pl_gate_w, v_pl_gate_w)
    o_projw = _adamw([r[2] for r in rest], pl_proj_w, m_pl_proj_w, v_pl_proj_w)
    r_small = _exchange_wait(small_send, small_recv, small_srcs, small_lnds, o_projw[1], True, "scatter_small_wait")
    r_rep = _exchange_wait(rep_send, rep_recv, rep_srcs, rep_lnds, o_w_in[1], False, "gather_replicated_wait")

    row = lambda a: a.reshape(1, -1)
    o_small = _adamw_small(
        r_rep[:8] + r_small,
        [norm_g, conv_b, conv_ln_g, conv_ln_b, sgu_w, sgu_b, pl_norm_g, row(final_g), conv_w, sgu_ln_g, sgu_ln_b],
        [m_norm_g, m_conv_b, m_conv_ln_g, m_conv_ln_b, m_sgu_w, m_sgu_b, m_pl_norm_g, row(m_final_g), m_conv_w,
         m_sgu_ln_g, m_sgu_ln_b],
        [v_norm_g, v_conv_b, v_conv_ln_g, v_conv_ln_b, v_sgu_w, v_sgu_b, v_pl_norm_g, row(v_final_g), v_conv_w,
         v_sgu_ln_g, v_sgu_ln_b],
        r_rep[8])
    loss = o_small[4][0, 0]

    def leaf(kind):
        sm = o_small[kind]
        return [sm[0], o_w_in[kind], o_w_out[kind], sm[8], sm[1], sm[2], sm[3], sm[9], sm[10], sm[4], sm[5], sm[6],
                o_gate[kind], o_projw[kind], sm[7].reshape(final_g.shape)]

    return (loss, grad_x, *leaf(0), *leaf(1), *leaf(2), *leaf(3))
```
